```python
import jax, jax.numpy as jnp
from jax import lax
import numpy as np

D_MODEL = 1024
BATCH = 8
SEQ = 2048
DEPTH = 1

RET_HEADS = 4
RET_DK = 256
RET_DV = 256
RET_CHUNK = 128
MLA_HEADS = 8
MLA_NOPE = 128
MLA_ROPE = 64
MLA_V = 128
MLA_Q_LORA = 384
MLA_KV_LORA = 256
ATTN_BLOCK = 128
ROPE_THETA = 10000.0
N_GROUPS = 4
EXPERTS_PER_GROUP = 8
TOP_K = 2
D_EXPERT = 256
EPS = 1e-6

RET_QK_W = RET_HEADS * RET_DK
RET_V_W = RET_HEADS * RET_DV
MLA_V_W = MLA_HEADS * MLA_V
SPLITS = [RET_QK_W, RET_QK_W, RET_V_W, RET_V_W, MLA_Q_LORA, MLA_KV_LORA, MLA_ROPE, D_MODEL, D_MODEL]
IN_W = sum(SPLITS)

kernel_name = "hybrid_retention_mla_hiermoe_adaln"


def rmsnorm(x, g):
    xf = x.astype(jnp.float32)
    y = xf * lax.rsqrt(jnp.mean(xf * xf, axis=-1, keepdims=True) + EPS)
    return (y * g.astype(jnp.float32)).astype(x.dtype)


def modulate(x, shift, scale):
    return x * (1.0 + scale) + shift


def rope_tables(pos, dim):
    inv = ROPE_THETA ** (-jnp.arange(0, dim, 2, dtype=jnp.float32) / dim)
    ang = pos[:, None] * inv[None, :]
    return jnp.cos(ang), jnp.sin(ang)


def apply_rope(x, cos, sin):
    x1, x2 = jnp.split(x.astype(jnp.float32), 2, axis=-1)
    cos = cos[None, :, None, :]
    sin = sin[None, :, None, :]
    return jnp.concatenate([x1 * cos - x2 * sin, x2 * cos + x1 * sin], axis=-1).astype(x.dtype)


def retention(q, k, v):
    b, s, h, dk = q.shape
    dv = v.shape[-1]
    c = RET_CHUNK
    n = s // c
    log_gamma = jnp.log1p(-jnp.exp2(-5.0 - jnp.arange(h, dtype=jnp.float32)))
    idx = jnp.arange(c, dtype=jnp.float32)
    rel = idx[:, None] - idx[None, :]
    decay_in = jnp.where(rel[None] >= 0, jnp.exp(log_gamma[:, None, None] * jnp.maximum(rel, 0.0)[None]), 0.0)
    xi = jnp.exp(log_gamma[:, None] * (idx[None, :] + 1.0))
    zeta = jnp.exp(log_gamma[:, None] * (c - 1.0 - idx[None, :]))
    chunk_decay = jnp.exp(log_gamma * c)

    def to_chunks(t):
        return t.astype(jnp.float32).reshape(b, n, c, h, t.shape[-1]).transpose(1, 0, 2, 3, 4)

    qc = to_chunks(q)
    kc = to_chunks(k) * (dk ** -0.5)
    vc = to_chunks(v)

    def step(state, inp):
        qi, ki, vi = inp
        scores = jnp.einsum('bihd,bjhd->bhij', qi, ki) * decay_in[None]
        y = jnp.einsum('bhij,bjhe->bihe', scores, vi) + jnp.einsum('bihd,hi,bhde->bihe', qi, xi, state)
        state = state * chunk_decay[None, :, None, None] + jnp.einsum('bjhd,hj,bjhe->bhde', ki, zeta, vi)
        return state, y

    state0 = jnp.zeros((b, h, dk, dv), jnp.float32)
    _, ys = lax.scan(step, state0, (qc, kc, vc))
    return ys.transpose(1, 0, 2, 3, 4).reshape(b, s, h, dv)


def head_norm(y):
    mu = jnp.mean(y, axis=-1, keepdims=True)
    var = jnp.mean(jnp.square(y - mu), axis=-1, keepdims=True)
    return (y - mu) * lax.rsqrt(var + EPS)


def mla(c_q, c_kv, k_pe, q_norm, kv_norm, w_uq, w_ukv, cos, sin):
    b, s, _ = c_q.shape
    q = (rmsnorm(c_q, q_norm) @ w_uq).reshape(b, s, MLA_HEADS, MLA_NOPE + MLA_ROPE)
    q_nope, q_pe = q[..., :MLA_NOPE], q[..., MLA_NOPE:]
    q_pe = apply_rope(q_pe, cos, sin)
    kv = (rmsnorm(c_kv, kv_norm) @ w_ukv).reshape(b, s, MLA_HEADS, MLA_NOPE + MLA_V)
    k_nope, v = kv[..., :MLA_NOPE], kv[..., MLA_NOPE:]
    k_pe = apply_rope(k_pe[:, :, None, :], cos, sin)[:, :, 0, :]
    scale = (MLA_NOPE + MLA_ROPE) ** -0.5
    outs = []
    for blk in range(s // ATTN_BLOCK):
        lo = blk * ATTN_BLOCK
        hi = lo + ATTN_BLOCK
        scores = (jnp.einsum('bqhd,bkhd->bhqk', q_nope[:, lo:hi], k_nope[:, :hi])
                  + jnp.einsum('bqhr,bkr->bhqk', q_pe[:, lo:hi], k_pe[:, :hi])).astype(jnp.float32) * scale
        mask = (lo + jnp.arange(ATTN_BLOCK))[:, None] >= jnp.arange(hi)[None, :]
        p = jax.nn.softmax(jnp.where(mask, scores, -jnp.inf), axis=-1).astype(v.dtype)
        outs.append(jnp.einsum('bhqk,bkhd->bqhd', p, v[:, :hi]))
    return jnp.concatenate(outs, axis=1).reshape(b, s, MLA_V_W)


def token_mixer(u, w_in, w_ret_o, q_norm, kv_norm, w_uq, w_ukv, w_mla_o, w_out, ret_rope, mla_rope):
    b, s, _ = u.shape
    proj = u @ w_in
    offs = np.cumsum(SPLITS)[:-1].tolist()
    rq, rk, rv, rg, c_q, c_kv, k_pe, gate_ret, gate_mla = jnp.split(proj, offs, axis=-1)
    rq = apply_rope(rq.reshape(b, s, RET_HEADS, RET_DK), *ret_rope)
    rk = apply_rope(rk.reshape(b, s, RET_HEADS, RET_DK), *ret_rope)
    rv = rv.reshape(b, s, RET_HEADS, RET_DV)
    y_ret = head_norm(retention(rq, rk, rv)).reshape(b, s, RET_V_W).astype(u.dtype)
    y_ret = (jax.nn.silu(rg) * y_ret) @ w_ret_o
    y_mla = mla(c_q, c_kv, k_pe, q_norm, kv_norm, w_uq, w_ukv, *mla_rope) @ w_mla_o
    merged = jax.nn.sigmoid(gate_ret) * y_ret + jax.nn.sigmoid(gate_mla) * y_mla
    return merged @ w_out


def hier_moe(u, w_grp, b_grp, w_exp, b_exp, w1, w3, w2):
    b, s, d = u.shape
    t = u.reshape(-1, d)
    n_tok = t.shape[0]
    grp_logits = (t @ w_grp).astype(jnp.float32) + b_grp.astype(jnp.float32)
    grp_prob = jax.nn.softmax(grp_logits, axis=-1)
    _, g_sel = lax.top_k(grp_logits, 1)
    p_grp = jnp.take_along_axis(grp_prob, g_sel, axis=-1)
    exp_logits = ((t @ w_exp).astype(jnp.float32) + b_exp.astype(jnp.float32)).reshape(n_tok, N_GROUPS, EXPERTS_PER_GROUP)
    in_grp = jnp.take_along_axis(exp_logits, g_sel[:, :, None], axis=1)[:, 0]
    top_val, top_idx = lax.top_k(in_grp, TOP_K)
    top_w = jax.nn.softmax(top_val, axis=-1) * p_grp
    exp_w = jnp.sum(jax.nn.one_hot(top_idx, EXPERTS_PER_GROUP, dtype=jnp.float32) * top_w[..., None], axis=1)
    gate = jax.nn.one_hot(g_sel[:, 0], N_GROUPS, dtype=jnp.float32)[:, :, None] * exp_w[:, None, :]
    out = jnp.zeros_like(t)
    for gi in range(N_GROUPS):
        hid = jax.nn.silu(jnp.einsum('td,edf->tef', t, w1[gi])) * jnp.einsum('td,edf->tef', t, w3[gi])
        hid = hid * gate[:, gi, :, None].astype(hid.dtype)
        out = out + jnp.einsum('tef,efd->td', hid, w2[gi])
    return out.reshape(b, s, d)


def setup_inputs(seed: int = 0) -> dict:
    key = jax.random.key(seed)
    ks = jax.random.split(key, 24)
    L = DEPTH
    G, E, F = N_GROUPS, EXPERTS_PER_GROUP, D_EXPERT

    def nrm(k, shape, fan_in, scale=1.0):
        return jax.random.normal(k, shape, jnp.float32) * (scale * fan_in ** -0.5)

    def gain(k, shape):
        return 1.0 + 0.02 * jax.random.normal(k, shape, jnp.float32)

    return {
        "x": jax.random.normal(ks[0], (BATCH, SEQ, D_MODEL), jnp.float32),
        "c": jax.random.normal(ks[1], (BATCH, D_MODEL), jnp.float32),
        "w_ada": nrm(ks[2], (L, D_MODEL, 6 * D_MODEL), D_MODEL, 0.5),
        "b_ada": 0.02 * jax.random.normal(ks[3], (L, 6 * D_MODEL), jnp.float32),
        "norm1": gain(ks[4], (L, D_MODEL)),
        "norm2": gain(ks[5], (L, D_MODEL)),
        "w_in": nrm(ks[6], (L, D_MODEL, IN_W), D_MODEL),
        "w_ret_o": nrm(ks[7], (L, RET_V_W, D_MODEL), RET_V_W),
        "q_norm": gain(ks[8], (L, MLA_Q_LORA)),
        "kv_norm": gain(ks[9], (L, MLA_KV_LORA)),
        "w_uq": nrm(ks[10], (L, MLA_Q_LORA, MLA_HEADS * (MLA_NOPE + MLA_ROPE)), MLA_Q_LORA),
        "w_ukv": nrm(ks[11], (L, MLA_KV_LORA, MLA_HEADS * (MLA_NOPE + MLA_V)), MLA_KV_LORA),
        "w_mla_o": nrm(ks[12], (L, MLA_V_W, D_MODEL), MLA_V_W),
        "w_out": nrm(ks[13], (L, D_MODEL, D_MODEL), D_MODEL),
        "w_grp": nrm(ks[14], (L, D_MODEL, G), D_MODEL),
        "b_grp": 0.01 * jax.random.normal(ks[15], (L, G), jnp.float32),
        "w_exp": nrm(ks[16], (L, D_MODEL, G * E), D_MODEL),
        "b_exp": 0.01 * jax.random.normal(ks[17], (L, G * E), jnp.float32),
        "w1": nrm(ks[18], (L, G, E, D_MODEL, F), D_MODEL),
        "w3": nrm(ks[19], (L, G, E, D_MODEL, F), D_MODEL),
        "w2": nrm(ks[20], (L, G, E, F, D_MODEL), F),
        "final_norm": gain(ks[21], (D_MODEL,)),
    }


def reference(x, c, w_ada, b_ada, norm1, norm2, w_in, w_ret_o, q_norm, kv_norm, w_uq, w_ukv, w_mla_o, w_out,
              w_grp, b_grp, w_exp, b_exp, w1, w3, w2, final_norm):
    s = x.shape[1]
    pos = jnp.arange(s, dtype=jnp.float32)
    ret_rope = rope_tables(pos, RET_DK)
    mla_rope = rope_tables(pos, MLA_ROPE)
    c_act = jax.nn.silu(c)
    h = x
    for l in range(DEPTH):
        mod = (c_act @ w_ada[l] + b_ada[l])[:, None, :]
        sh1, sc1, g1, sh2, sc2, g2 = jnp.split(mod, 6, axis=-1)
        u = modulate(rmsnorm(h, norm1[l]), sh1, sc1)
        h = h + g1 * token_mixer(u, w_in[l], w_ret_o[l], q_norm[l], kv_norm[l], w_uq[l], w_ukv[l],
                                 w_mla_o[l], w_out[l], ret_rope, mla_rope)
        u = modulate(rmsnorm(h, norm2[l]), sh2, sc2)
        h = h + g2 * hier_moe(u, w_grp[l], b_grp[l], w_exp[l], b_exp[l], w1[l], w3[l], w2[l])
    return rmsnorm(h, final_norm)
```

```python
import functools

import numpy as np
import jax
import jax.numpy as jnp
from jax import lax
from jax.experimental import pallas as pl
from jax.experimental.pallas import tpu as pltpu

D_MODEL = 1024
BATCH = 8
SEQ = 2048
N_TOK = BATCH * SEQ

RET_HEADS = 4
RET_DK = 256
RET_DV = 256
RET_CHUNK = 128
RET_W = RET_HEADS * RET_DK

MLA_HEADS = 8
MLA_NOPE = 128
MLA_ROPE = 64
MLA_V = 128
MLA_Q_LORA = 384
MLA_KV_LORA = 256
MLA_LAT_W = MLA_Q_LORA + MLA_KV_LORA + 2 * MLA_ROPE
MLA_QK = MLA_NOPE + MLA_ROPE
ROPE_THETA = 10000.0

N_GROUPS = 4
EXPERTS_PER_GROUP = 8
N_EXPERTS = N_GROUPS * EXPERTS_PER_GROUP
D_EXPERT = 256
EPS = 1e-6

LANES = 128
VMEM_LIMIT = 56 * 1024 * 1024

TM_PROJ = 512
TM_MOE = 1024
TQ = 256

F32 = jnp.float32
BF16 = jnp.bfloat16


def _sigmoid(x):
    return 1.0 / (1.0 + jnp.exp(-x))


def _rms(x):
    return x * lax.rsqrt(jnp.mean(x * x, axis=-1, keepdims=True) + EPS)


def _dot(a, b):
    return jnp.dot(a, b, preferred_element_type=F32)


def _dot_nt(a, b):
    return lax.dot_general(a, b, (((1,), (1,)), ((), ())), preferred_element_type=F32)


def _dot_tn(a, b):
    return lax.dot_general(a, b, (((0,), (0,)), ((), ())), preferred_element_type=F32)


def _params(*sem):
    return pltpu.CompilerParams(dimension_semantics=sem, vmem_limit_bytes=VMEM_LIMIT)


def _resident(shape):
    nd = len(shape)
    return pl.BlockSpec(shape, lambda *_: (0,) * nd, pipeline_mode=pl.Buffered(1))


def _ada_kernel(c_ref, w_ref, b_ref, o_ref):
    c = c_ref[...]
    act = (c * _sigmoid(c)).astype(BF16)
    o_ref[...] = _dot(act, w_ref[...].astype(BF16)) + b_ref[...]


def _ada(c, w_ada, b_ada):
    n = w_ada.shape[1]
    tn = D_MODEL
    return pl.pallas_call(
        _ada_kernel,
        grid=(n // tn,),
        in_specs=[pl.BlockSpec((BATCH, D_MODEL), lambda j: (0, 0)),
                  pl.BlockSpec((D_MODEL, tn), lambda j: (0, j)),
                  pl.BlockSpec((1, tn), lambda j: (0, j))],
        out_specs=pl.BlockSpec((BATCH, tn), lambda j: (0, j)),
        out_shape=jax.ShapeDtypeStruct((BATCH, n), F32),
        compiler_params=_params("arbitrary"),
        name="ada",
    )(c, w_ada, b_ada.reshape(1, n))


def _inproj_kernel(x_ref, mod_ref, n1_ref, wr_ref, wm_ref, wg_ref, ret_ref, lat_ref, gate_ref):
    y = _rms(x_ref[...]) * n1_ref[...]
    u = (y * (1.0 + mod_ref[1:2, :]) + mod_ref[0:1, :]).astype(BF16)
    step = 512
    for n in range(0, 4 * RET_W, step):
        ret_ref[:, n:n + step] = _dot(u, wr_ref[:, n:n + step]).astype(BF16)
    lat_ref[...] = _dot(u, wm_ref[...]).astype(BF16)
    for n in range(0, 2 * D_MODEL, step):
        gate_ref[:, n:n + step] = _dot(u, wg_ref[:, n:n + step]).astype(BF16)


def _inproj(x2, mod3, norm1, w_ret, w_lat, w_gate):
    tm = TM_PROJ
    per_b = SEQ // tm
    return pl.pallas_call(
        _inproj_kernel,
        grid=(N_TOK // tm,),
        in_specs=[pl.BlockSpec((tm, D_MODEL), lambda i: (i, 0)),
                  pl.BlockSpec((None, 6, D_MODEL), lambda i: (i // per_b, 0, 0)),
                  _resident((1, D_MODEL)),
                  _resident(w_ret.shape), _resident(w_lat.shape), _resident(w_gate.shape)],
        out_specs=[pl.BlockSpec((tm, 4 * RET_W), lambda i: (i, 0)),
                   pl.BlockSpec((tm, MLA_LAT_W), lambda i: (i, 0)),
                   pl.BlockSpec((tm, 2 * D_MODEL), lambda i: (i, 0))],
        out_shape=[jax.ShapeDtypeStruct((N_TOK, 4 * RET_W), BF16),
                   jax.ShapeDtypeStruct((N_TOK, MLA_LAT_W), BF16),
                   jax.ShapeDtypeStruct((N_TOK, 2 * D_MODEL), BF16)],
        compiler_params=_params("arbitrary"),
        name="inproj",
    )(x2, mod3, norm1, w_ret, w_lat, w_gate)


def _ret_kernel(q_ref, k_ref, v_ref, g_ref, cos_ref, sin_ref, dec_ref, xi_ref, zeta_ref, cd_ref,
                o_ref, state_ref):
    half = RET_DK // 2
    state_ref[...] = jnp.zeros_like(state_ref)

    def rope(x, cos, sin):
        x1, x2 = x[:, :half], x[:, half:]
        return jnp.concatenate([x1 * cos - x2 * sin, x2 * cos + x1 * sin], axis=-1)

    def body(c, carry):
        rows = pl.ds(pl.multiple_of(c * RET_CHUNK, RET_CHUNK), RET_CHUNK)
        cos, sin = cos_ref[rows, :], sin_ref[rows, :]
        q = rope(q_ref[rows, :].astype(F32), cos, sin)
        k = rope(k_ref[rows, :].astype(F32), cos, sin) * (RET_DK ** -0.5)
        v = v_ref[rows, :]
        scores = _dot_nt(q.astype(BF16), k.astype(BF16)) * dec_ref[...]
        state = state_ref[...]
        y = _dot(scores.astype(BF16), v) + _dot((q * xi_ref[...]).astype(BF16), state.astype(BF16))
        state_ref[...] = state * cd_ref[...] + _dot_tn((k * zeta_ref[...]).astype(BF16), v)
        mu = jnp.mean(y, axis=-1, keepdims=True)
        yc = y - mu
        yn = yc * lax.rsqrt(jnp.mean(yc * yc, axis=-1, keepdims=True) + EPS)
        g = g_ref[rows, :].astype(F32)
        o_ref[rows, :] = (g * _sigmoid(g) * yn).astype(BF16)
        return carry

    lax.fori_loop(0, SEQ // RET_CHUNK, body, 0)


def _retention(ret3, cos, sin, dec, xi, zeta, cd):
    c = RET_CHUNK
    blk = lambda off: pl.BlockSpec((None, SEQ, RET_DK), lambda b, h: (b, 0, off + h))
    tab = lambda shape: pl.BlockSpec((None,) + shape, lambda b, h: (h, 0, 0))
    return pl.pallas_call(
        _ret_kernel,
        grid=(BATCH, RET_HEADS),
        in_specs=[blk(0), blk(RET_HEADS), blk(2 * RET_HEADS), blk(3 * RET_HEADS),
                  pl.BlockSpec((SEQ, RET_DK // 2), lambda b, h: (0, 0)),
                  pl.BlockSpec((SEQ, RET_DK // 2), lambda b, h: (0, 0)),
                  tab((c, c)), tab((c, 1)), tab((c, 1)), tab((1, 1))],
        out_specs=pl.BlockSpec((None, SEQ, RET_DV), lambda b, h: (b, 0, h)),
        out_shape=jax.ShapeDtypeStruct((BATCH, SEQ, RET_HEADS * RET_DV), BF16),
        scratch_shapes=[pltpu.VMEM((RET_DK, RET_DV), F32)],
        compiler_params=_params("arbitrary", "arbitrary"),
        name="retention",
    )(ret3, ret3, ret3, ret3, cos, sin, dec, xi, zeta, cd)


def _mla_kernel(lat_ref, qn_ref, kvn_ref, wq_ref, wkv_ref, cos_ref, sin_ref, o_ref,
                cq_s, ckv_s, kpe_s, q_s, k_s, v_s):
    h = pl.program_id(1)
    o_q, o_kv, o_pe, o_rot = 0, MLA_Q_LORA, MLA_Q_LORA + MLA_KV_LORA, MLA_Q_LORA + MLA_KV_LORA + MLA_ROPE
    cos, sin = cos_ref[...], sin_ref[...]

    @pl.when(h == 0)
    def _():
        cq_s[...] = (_rms(lat_ref[:, o_q:o_kv].astype(F32)) * qn_ref[...]).astype(BF16)
        ckv_s[...] = (_rms(lat_ref[:, o_kv:o_pe].astype(F32)) * kvn_ref[...]).astype(BF16)
        pe = lat_ref[:, o_pe:o_rot].astype(F32)
        rot = lat_ref[:, o_rot:o_rot + MLA_ROPE].astype(F32)
        kpe_s[...] = (pe * cos + rot * sin).astype(BF16)

    scale = MLA_QK ** -0.5
    qf = _dot(cq_s[...], wq_ref[...])
    q_s[:, :MLA_NOPE] = (qf[:, :MLA_NOPE] * scale).astype(BF16)
    q_pe = qf[:, MLA_NOPE:MLA_QK] * cos + qf[:, MLA_QK:] * sin
    q_s[:, MLA_NOPE:] = (q_pe * scale).astype(BF16)
    kvf = _dot(ckv_s[...], wkv_ref[...])
    k_s[:, :MLA_NOPE] = kvf[:, :MLA_NOPE].astype(BF16)
    k_s[:, MLA_NOPE:] = kpe_s[...]
    v_s[...] = kvf[:, MLA_NOPE:].astype(BF16)

    for i in range(SEQ // TQ):
        lo, hi = i * TQ, (i + 1) * TQ
        s = _dot_nt(q_s[lo:hi, :], k_s[:hi, :])
        row = lo + lax.broadcasted_iota(jnp.int32, (TQ, hi), 0)
        col = lax.broadcasted_iota(jnp.int32, (TQ, hi), 1)
        s = jnp.where(row >= col, s, -jnp.inf)
        p = jnp.exp(s - jnp.max(s, axis=-1, keepdims=True))
        denom = jnp.sum(p, axis=-1, keepdims=True)
        o_ref[lo:hi, :] = (_dot(p.astype(BF16), v_s[:hi, :]) / denom).astype(BF16)


def _mla(lat3, q_norm, kv_norm, wq, wkv, cos, sin):
    return pl.pallas_call(
        _mla_kernel,
        grid=(BATCH, MLA_HEADS),
        in_specs=[pl.BlockSpec((None, SEQ, MLA_LAT_W), lambda b, h: (b, 0, 0)),
                  pl.BlockSpec((1, MLA_Q_LORA), lambda b, h: (0, 0)),
                  pl.BlockSpec((1, MLA_KV_LORA), lambda b, h: (0, 0)),
                  pl.BlockSpec((None, MLA_Q_LORA, MLA_QK + MLA_ROPE), lambda b, h: (h, 0, 0)),
                  pl.BlockSpec((None, MLA_KV_LORA, MLA_NOPE + MLA_V), lambda b, h: (h, 0, 0)),
                  pl.BlockSpec((SEQ, MLA_ROPE), lambda b, h: (0, 0)),
                  pl.BlockSpec((SEQ, MLA_ROPE), lambda b, h: (0, 0))],
        out_specs=pl.BlockSpec((None, SEQ, MLA_V), lambda b, h: (b, 0, h)),
        out_shape=jax.ShapeDtypeStruct((BATCH, SEQ, MLA_HEADS * MLA_V), BF16),
        scratch_shapes=[pltpu.VMEM((SEQ, MLA_Q_LORA), BF16),
                        pltpu.VMEM((SEQ, MLA_KV_LORA), BF16),
                        pltpu.VMEM((SEQ, MLA_ROPE), BF16),
                        pltpu.VMEM((SEQ, MLA_QK), BF16),
                        pltpu.VMEM((SEQ, MLA_QK), BF16),
                        pltpu.VMEM((SEQ, MLA_V), BF16)],
        compiler_params=_params("arbitrary", "arbitrary"),
        name="mla",
    )(lat3, q_norm, kv_norm, wq, wkv, cos, sin)


def _route(logits):
    lane = lax.broadcasted_iota(jnp.int32, logits.shape, 1)
    neg = -jnp.inf
    gl = jnp.where(lane < N_GROUPS, logits, neg)
    gmax = jnp.max(gl, axis=-1, keepdims=True)
    gsel = jnp.min(jnp.where(gl == gmax, lane, LANES), axis=-1, keepdims=True)
    p_grp = 1.0 / jnp.sum(jnp.exp(gl - gmax), axis=-1, keepdims=True)
    e_lane = lane - N_GROUPS
    in_grp = (e_lane >= 0) & (e_lane < N_EXPERTS) & ((e_lane >> 3) == gsel)
    el = jnp.where(in_grp, logits, neg)
    v0 = jnp.max(el, axis=-1, keepdims=True)
    i0 = jnp.min(jnp.where(el == v0, lane, LANES), axis=-1, keepdims=True)
    el1 = jnp.where(lane == i0, neg, el)
    v1 = jnp.max(el1, axis=-1, keepdims=True)
    i1 = jnp.min(jnp.where(el1 == v1, lane, LANES), axis=-1, keepdims=True)
    t = jnp.exp(v1 - v0)
    w0 = p_grp / (1.0 + t)
    w1 = p_grp * t / (1.0 + t)
    return jnp.where(lane == i0, w0, 0.0) + jnp.where(lane == i1, w1, 0.0)


def _merge_kernel(yr_ref, at_ref, gr_ref, gm_ref, x_ref, mod_ref, n2_ref, wro_ref, wmo_ref, wo_ref,
                  wrt_ref, brt_ref, h1_ref, u2_ref, gate_ref):
    y_ret = _dot(yr_ref[...], wro_ref[...])
    y_mla = _dot(at_ref[...], wmo_ref[...])
    merged = _sigmoid(gr_ref[...].astype(F32)) * y_ret + _sigmoid(gm_ref[...].astype(F32)) * y_mla
    o = _dot(merged.astype(BF16), wo_ref[...])
    h1 = x_ref[...] + mod_ref[2:3, :] * o
    h1_ref[...] = h1
    u2 = _rms(h1) * n2_ref[...] * (1.0 + mod_ref[4:5, :]) + mod_ref[3:4, :]
    u_hi = u2.astype(BF16)
    u2_ref[...] = u_hi
    u_lo = (u2 - u_hi.astype(F32)).astype(BF16)
    w = wrt_ref[...]
    w_hi = w.astype(BF16)
    w_lo = (w - w_hi.astype(F32)).astype(BF16)
    logits = _dot(u_hi, w_hi) + _dot(u_lo, w_hi) + _dot(u_hi, w_lo) + brt_ref[...]
    gate_ref[...] = _route(logits)


def _merge(y_ret, attn, gates, x2, mod3, norm2, w_ret_o, w_mla_o, w_out, w_rt, b_rt):
    tm = TM_PROJ
    per_b = SEQ // tm
    row = lambda j: pl.BlockSpec((tm, D_MODEL), lambda i: (i, j))
    return pl.pallas_call(
        _merge_kernel,
        grid=(N_TOK // tm,),
        in_specs=[row(0), row(0), row(0), row(1), row(0),
                  pl.BlockSpec((None, 6, D_MODEL), lambda i: (i // per_b, 0, 0)),
                  _resident((1, D_MODEL)),
                  _resident(w_ret_o.shape), _resident(w_mla_o.shape), _resident(w_out.shape),
                  _resident(w_rt.shape), _resident(b_rt.shape)],
        out_specs=[row(0), row(0), pl.BlockSpec((tm, LANES), lambda i: (i, 0))],
        out_shape=[jax.ShapeDtypeStruct((N_TOK, D_MODEL), F32),
                   jax.ShapeDtypeStruct((N_TOK, D_MODEL), BF16),
                   jax.ShapeDtypeStruct((N_TOK, LANES), F32)],
        compiler_params=_params("arbitrary"),
        name="merge",
    )(y_ret, attn, gates, gates, x2, mod3, norm2, w_ret_o, w_mla_o, w_out, w_rt, b_rt)


def _moe_kernel(u2_ref, gate_ref, h1_ref, mod_ref, fn_ref, w1_ref, w3_ref, w2_ref, o_ref, acc_ref):
    e = pl.program_id(1)

    @pl.when(e == 0)
    def _():
        acc_ref[...] = jnp.zeros_like(acc_ref)

    u = u2_ref[...]
    a = _dot(u, w1_ref[...].astype(BF16))
    b = _dot(u, w3_ref[...].astype(BF16))
    sel = (lax.broadcasted_iota(jnp.int32, (LANES, D_EXPERT), 0) == e + N_GROUPS).astype(BF16)
    gcol = _dot(gate_ref[...].astype(BF16), sel)
    hid = a * _sigmoid(a) * b * gcol
    acc_ref[...] += _dot(hid.astype(BF16), w2_ref[...].astype(BF16))

    @pl.when(e == N_EXPERTS - 1)
    def _():
        h2 = h1_ref[...] + mod_ref[5:6, :] * acc_ref[...]
        o_ref[...] = _rms(h2) * fn_ref[...]


def _moe(u2, gate, h1, mod3, final_norm, w1, w3, w2):
    tm = TM_MOE
    per_b = SEQ // tm
    return pl.pallas_call(
        _moe_kernel,
        grid=(N_TOK // tm, N_EXPERTS),
        in_specs=[pl.BlockSpec((tm, D_MODEL), lambda i, e: (i, 0)),
                  pl.BlockSpec((tm, LANES), lambda i, e: (i, 0)),
                  pl.BlockSpec((tm, D_MODEL), lambda i, e: (i, 0)),
                  pl.BlockSpec((None, 6, D_MODEL), lambda i, e: (i // per_b, 0, 0)),
                  pl.BlockSpec((1, D_MODEL), lambda i, e: (0, 0)),
                  pl.BlockSpec((None, D_MODEL, D_EXPERT), lambda i, e: (e, 0, 0)),
                  pl.BlockSpec((None, D_MODEL, D_EXPERT), lambda i, e: (e, 0, 0)),
                  pl.BlockSpec((None, D_EXPERT, D_MODEL), lambda i, e: (e, 0, 0))],
        out_specs=pl.BlockSpec((tm, D_MODEL), lambda i, e: (i, 0)),
        out_shape=jax.ShapeDtypeStruct((N_TOK, D_MODEL), F32),
        scratch_shapes=[pltpu.VMEM((tm, D_MODEL), F32)],
        compiler_params=_params("arbitrary", "arbitrary"),
        name="moe",
    )(u2, gate, h1, mod3, final_norm, w1, w3, w2)


def _rope_tables(dim):
    pos = jnp.arange(SEQ, dtype=F32)
    inv = ROPE_THETA ** (-jnp.arange(0, dim, 2, dtype=F32) / dim)
    ang = pos[:, None] * inv[None, :]
    return jnp.cos(ang), jnp.sin(ang)


def _decay_tables():
    c = RET_CHUNK
    log_gamma = jnp.log1p(-jnp.exp2(-5.0 - jnp.arange(RET_HEADS, dtype=F32)))
    idx = jnp.arange(c, dtype=F32)
    rel = idx[:, None] - idx[None, :]
    dec = jnp.where(rel[None] >= 0, jnp.exp(log_gamma[:, None, None] * jnp.maximum(rel, 0.0)[None]), 0.0)
    xi = jnp.exp(log_gamma[:, None] * (idx[None, :] + 1.0))[:, :, None]
    zeta = jnp.exp(log_gamma[:, None] * (c - 1.0 - idx[None, :]))[:, :, None]
    cd = jnp.exp(log_gamma * c)[:, None, None]
    return dec, xi, zeta, cd


def _rotate_half_cols(w):
    half = w.shape[-1] // 2
    return jnp.concatenate([-w[..., half:], w[..., :half]], axis=-1)


def kernel(x, c, w_ada, b_ada, norm1, norm2, w_in, w_ret_o, q_norm, kv_norm, w_uq, w_ukv, w_mla_o, w_out,
           w_grp, b_grp, w_exp, b_exp, w1, w3, w2, final_norm):
    assert x.shape == (BATCH, SEQ, D_MODEL) and w_ada.shape[0] == 1
    x2 = x.reshape(N_TOK, D_MODEL)

    o_lat = 4 * RET_W
    o_pe = o_lat + MLA_Q_LORA + MLA_KV_LORA
    o_gate = o_pe + MLA_ROPE
    wi = w_in[0]
    w_ret = wi[:, :o_lat].astype(BF16)
    w_lat = jnp.concatenate([wi[:, o_lat:o_gate], _rotate_half_cols(wi[:, o_pe:o_gate])], axis=1).astype(BF16)
    w_gate = wi[:, o_gate:].astype(BF16)
    wq = w_uq[0].reshape(MLA_Q_LORA, MLA_HEADS, MLA_QK)
    wq = jnp.concatenate([wq, _rotate_half_cols(wq[..., MLA_NOPE:])], axis=-1)
    wq = wq.transpose(1, 0, 2).astype(BF16)
    wkv = w_ukv[0].reshape(MLA_KV_LORA, MLA_HEADS, MLA_NOPE + MLA_V).transpose(1, 0, 2).astype(BF16)
    pad = LANES - N_GROUPS - N_EXPERTS
    w_rt = jnp.concatenate([w_grp[0], w_exp[0], jnp.zeros((D_MODEL, pad), F32)], axis=1)
    b_rt = jnp.concatenate([b_grp[0], b_exp[0], jnp.zeros((pad,), F32)])[None, :]

    ret_cos, ret_sin = _rope_tables(RET_DK)
    mla_cos, mla_sin = _rope_tables(MLA_ROPE)
    mla_cos = jnp.concatenate([mla_cos, mla_cos], axis=-1)
    mla_sin = jnp.concatenate([mla_sin, mla_sin], axis=-1)
    dec, xi, zeta, cd = _decay_tables()

    mod3 = _ada(c, w_ada[0], b_ada[0]).reshape(BATCH, 6, D_MODEL)
    ret, lat, gates = _inproj(x2, mod3, norm1, w_ret, w_lat, w_gate)
    y_ret = _retention(ret.reshape(BATCH, SEQ, 4 * RET_W), ret_cos, ret_sin, dec, xi, zeta, cd)
    attn = _mla(lat.reshape(BATCH, SEQ, MLA_LAT_W), q_norm, kv_norm, wq, wkv, mla_cos, mla_sin)
    h1, u2, gate = _merge(y_ret.reshape(N_TOK, D_MODEL), attn.reshape(N_TOK, D_MODEL), gates, x2, mod3, norm2,
                          w_ret_o[0].astype(BF16), w_mla_o[0].astype(BF16), w_out[0].astype(BF16), w_rt, b_rt)
    e_shape = (N_EXPERTS, D_MODEL, D_EXPERT)
    out = _moe(u2, gate, h1, mod3, final_norm.reshape(1, D_MODEL),
               w1[0].reshape(e_shape), w3[0].reshape(e_shape), w2[0].reshape(N_EXPERTS, D_EXPERT, D_MODEL))
    return out.reshape(BATCH, SEQ, D_MODEL)
```

```python
import functools

import numpy as np
import jax
import jax.numpy as jnp
from jax import lax
from jax.experimental import pallas as pl
from jax.experimental.pallas import tpu as pltpu

D_MODEL = 1024
BATCH = 8
SEQ = 2048
N_TOK = BATCH * SEQ

RET_HEADS = 4
RET_DK = 256
RET_DV = 256
RET_CHUNK = 128
RET_W = RET_HEADS * RET_DK

MLA_HEADS = 8
MLA_NOPE = 128
MLA_ROPE = 64
MLA_V = 128
MLA_Q_LORA = 384
MLA_KV_LORA = 256
MLA_LAT_W = MLA_Q_LORA + MLA_KV_LORA + 2 * MLA_ROPE
MLA_QK = MLA_NOPE + MLA_ROPE
ROPE_THETA = 10000.0

N_GROUPS = 4
EXPERTS_PER_GROUP = 8
N_EXPERTS = N_GROUPS * EXPERTS_PER_GROUP
D_EXPERT = 256
EPS = 1e-6

LANES = 128
SUBLANES = 8
ROW_TILES = D_MODEL // LANES
VMEM_LIMIT = 56 * 1024 * 1024

TM_PROJ = 512
TQ = 256
TE = 256
TOP_K = 2
N_TILES = N_TOK * TOP_K // TE + N_EXPERTS
N_SLOTS = N_TILES * TE

F32 = jnp.float32
BF16 = jnp.bfloat16


def _sigmoid(x):
    return 1.0 / (1.0 + jnp.exp(-x))


def _rms(x):
    return x * lax.rsqrt(jnp.mean(x * x, axis=-1, keepdims=True) + EPS)


def _dot(a, b):
    return jnp.dot(a, b, preferred_element_type=F32)


def _dot_nt(a, b):
    return lax.dot_general(a, b, (((1,), (1,)), ((), ())), preferred_element_type=F32)


def _dot_tn(a, b):
    return lax.dot_general(a, b, (((0,), (0,)), ((), ())), preferred_element_type=F32)


def _params(*sem):
    return pltpu.CompilerParams(dimension_semantics=sem, vmem_limit_bytes=VMEM_LIMIT)


def _resident(shape):
    nd = len(shape)
    return pl.BlockSpec(shape, lambda *_: (0,) * nd, pipeline_mode=pl.Buffered(1))


def _ada_kernel(c_ref, w_ref, b_ref, o_ref):
    c = c_ref[...]
    act = (c * _sigmoid(c)).astype(BF16)
    o_ref[...] = _dot(act, w_ref[...].astype(BF16)) + b_ref[...]


def _ada(c, w_ada, b_ada):
    n = w_ada.shape[1]
    tn = D_MODEL
    return pl.pallas_call(
        _ada_kernel,
        grid=(n // tn,),
        in_specs=[pl.BlockSpec((BATCH, D_MODEL), lambda j: (0, 0)),
                  pl.BlockSpec((D_MODEL, tn), lambda j: (0, j)),
                  pl.BlockSpec((1, tn), lambda j: (0, j))],
        out_specs=pl.BlockSpec((BATCH, tn), lambda j: (0, j)),
        out_shape=jax.ShapeDtypeStruct((BATCH, n), F32),
        compiler_params=_params("arbitrary"),
        name="ada",
    )(c, w_ada, b_ada.reshape(1, n))


def _inproj_kernel(x_ref, mod_ref, n1_ref, wr_ref, wm_ref, wg_ref, ret_ref, lat_ref, gate_ref):
    y = _rms(x_ref[...]) * n1_ref[...]
    u = (y * (1.0 + mod_ref[1:2, :]) + mod_ref[0:1, :]).astype(BF16)
    step = 512
    for n in range(0, 4 * RET_W, step):
        ret_ref[:, n:n + step] = _dot(u, wr_ref[:, n:n + step]).astype(BF16)
    lat_ref[...] = _dot(u, wm_ref[...]).astype(BF16)
    for n in range(0, 2 * D_MODEL, step):
        gate_ref[:, n:n + step] = _dot(u, wg_ref[:, n:n + step]).astype(BF16)


def _inproj(x2, mod3, norm1, w_ret, w_lat, w_gate):
    tm = TM_PROJ
    per_b = SEQ // tm
    return pl.pallas_call(
        _inproj_kernel,
        grid=(N_TOK // tm,),
        in_specs=[pl.BlockSpec((tm, D_MODEL), lambda i: (i, 0)),
                  pl.BlockSpec((None, 6, D_MODEL), lambda i: (i // per_b, 0, 0)),
                  _resident((1, D_MODEL)),
                  _resident(w_ret.shape), _resident(w_lat.shape), _resident(w_gate.shape)],
        out_specs=[pl.BlockSpec((tm, 4 * RET_W), lambda i: (i, 0)),
                   pl.BlockSpec((tm, MLA_LAT_W), lambda i: (i, 0)),
                   pl.BlockSpec((tm, 2 * D_MODEL), lambda i: (i, 0))],
        out_shape=[jax.ShapeDtypeStruct((N_TOK, 4 * RET_W), BF16),
                   jax.ShapeDtypeStruct((N_TOK, MLA_LAT_W), BF16),
                   jax.ShapeDtypeStruct((N_TOK, 2 * D_MODEL), BF16)],
        compiler_params=_params("arbitrary"),
        name="inproj",
    )(x2, mod3, norm1, w_ret, w_lat, w_gate)


def _ret_kernel(q_ref, k_ref, v_ref, g_ref, cos_ref, sin_ref, dec_ref, xi_ref, zeta_ref, cd_ref,
                o_ref, state_ref):
    half = RET_DK // 2
    state_ref[...] = jnp.zeros_like(state_ref)

    def rope(x, cos, sin):
        x1, x2 = x[:, :half], x[:, half:]
        return jnp.concatenate([x1 * cos - x2 * sin, x2 * cos + x1 * sin], axis=-1)

    def body(c, carry):
        rows = pl.ds(pl.multiple_of(c * RET_CHUNK, RET_CHUNK), RET_CHUNK)
        cos, sin = cos_ref[rows, :], sin_ref[rows, :]
        q = rope(q_ref[rows, :].astype(F32), cos, sin)
        k = rope(k_ref[rows, :].astype(F32), cos, sin) * (RET_DK ** -0.5)
        v = v_ref[rows, :]
        scores = _dot_nt(q.astype(BF16), k.astype(BF16)) * dec_ref[...]
        state = state_ref[...]
        y = _dot(scores.astype(BF16), v) + _dot((q * xi_ref[...]).astype(BF16), state.astype(BF16))
        state_ref[...] = state * cd_ref[...] + _dot_tn((k * zeta_ref[...]).astype(BF16), v)
        mu = jnp.mean(y, axis=-1, keepdims=True)
        yc = y - mu
        yn = yc * lax.rsqrt(jnp.mean(yc * yc, axis=-1, keepdims=True) + EPS)
        g = g_ref[rows, :].astype(F32)
        o_ref[rows, :] = (g * _sigmoid(g) * yn).astype(BF16)
        return carry

    lax.fori_loop(0, SEQ // RET_CHUNK, body, 0)


def _retention(ret3, cos, sin, dec, xi, zeta, cd):
    c = RET_CHUNK
    blk = lambda off: pl.BlockSpec((None, SEQ, RET_DK), lambda b, h: (b, 0, off + h))
    tab = lambda shape: pl.BlockSpec((None,) + shape, lambda b, h: (h, 0, 0))
    return pl.pallas_call(
        _ret_kernel,
        grid=(BATCH, RET_HEADS),
        in_specs=[blk(0), blk(RET_HEADS), blk(2 * RET_HEADS), blk(3 * RET_HEADS),
                  pl.BlockSpec((SEQ, RET_DK // 2), lambda b, h: (0, 0)),
                  pl.BlockSpec((SEQ, RET_DK // 2), lambda b, h: (0, 0)),
                  tab((c, c)), tab((c, 1)), tab((c, 1)), tab((1, 1))],
        out_specs=pl.BlockSpec((None, SEQ, RET_DV), lambda b, h: (b, 0, h)),
        out_shape=jax.ShapeDtypeStruct((BATCH, SEQ, RET_HEADS * RET_DV), BF16),
        scratch_shapes=[pltpu.VMEM((RET_DK, RET_DV), F32)],
        compiler_params=_params("arbitrary", "arbitrary"),
        name="retention",
    )(ret3, ret3, ret3, ret3, cos, sin, dec, xi, zeta, cd)


def _mla_kernel(lat_ref, qn_ref, kvn_ref, wq_ref, wkv_ref, cos_ref, sin_ref, o_ref,
                cq_s, ckv_s, kpe_s, q_s, k_s, v_s):
    h = pl.program_id(1)
    o_q, o_kv, o_pe, o_rot = 0, MLA_Q_LORA, MLA_Q_LORA + MLA_KV_LORA, MLA_Q_LORA + MLA_KV_LORA + MLA_ROPE
    cos, sin = cos_ref[...], sin_ref[...]

    @pl.when(h == 0)
    def _():
        cq_s[...] = (_rms(lat_ref[:, o_q:o_kv].astype(F32)) * qn_ref[...]).astype(BF16)
        ckv_s[...] = (_rms(lat_ref[:, o_kv:o_pe].astype(F32)) * kvn_ref[...]).astype(BF16)
        pe = lat_ref[:, o_pe:o_rot].astype(F32)
        rot = lat_ref[:, o_rot:o_rot + MLA_ROPE].astype(F32)
        kpe_s[...] = (pe * cos + rot * sin).astype(BF16)

    scale = MLA_QK ** -0.5
    qf = _dot(cq_s[...], wq_ref[...])
    q_s[:, :MLA_NOPE] = (qf[:, :MLA_NOPE] * scale).astype(BF16)
    q_pe = qf[:, MLA_NOPE:MLA_QK] * cos + qf[:, MLA_QK:] * sin
    q_s[:, MLA_NOPE:] = (q_pe * scale).astype(BF16)
    kvf = _dot(ckv_s[...], wkv_ref[...])
    k_s[:, :MLA_NOPE] = kvf[:, :MLA_NOPE].astype(BF16)
    k_s[:, MLA_NOPE:] = kpe_s[...]
    v_s[...] = kvf[:, MLA_NOPE:].astype(BF16)

    for i in range(SEQ // TQ):
        lo, hi = i * TQ, (i + 1) * TQ
        s = _dot_nt(q_s[lo:hi, :], k_s[:hi, :])
        row = lo + lax.broadcasted_iota(jnp.int32, (TQ, hi), 0)
        col = lax.broadcasted_iota(jnp.int32, (TQ, hi), 1)
        s = jnp.where(row >= col, s, -jnp.inf)
        p = jnp.exp(s - jnp.max(s, axis=-1, keepdims=True))
        denom = jnp.sum(p, axis=-1, keepdims=True)
        o_ref[lo:hi, :] = (_dot(p.astype(BF16), v_s[:hi, :]) / denom).astype(BF16)


def _mla(lat3, q_norm, kv_norm, wq, wkv, cos, sin):
    return pl.pallas_call(
        _mla_kernel,
        grid=(BATCH, MLA_HEADS),
        in_specs=[pl.BlockSpec((None, SEQ, MLA_LAT_W), lambda b, h: (b, 0, 0)),
                  pl.BlockSpec((1, MLA_Q_LORA), lambda b, h: (0, 0)),
                  pl.BlockSpec((1, MLA_KV_LORA), lambda b, h: (0, 0)),
                  pl.BlockSpec((None, MLA_Q_LORA, MLA_QK + MLA_ROPE), lambda b, h: (h, 0, 0)),
                  pl.BlockSpec((None, MLA_KV_LORA, MLA_NOPE + MLA_V), lambda b, h: (h, 0, 0)),
                  pl.BlockSpec((SEQ, MLA_ROPE), lambda b, h: (0, 0)),
                  pl.BlockSpec((SEQ, MLA_ROPE), lambda b, h: (0, 0))],
        out_specs=pl.BlockSpec((None, SEQ, MLA_V), lambda b, h: (b, 0, h)),
        out_shape=jax.ShapeDtypeStruct((BATCH, SEQ, MLA_HEADS * MLA_V), BF16),
        scratch_shapes=[pltpu.VMEM((SEQ, MLA_Q_LORA), BF16),
                        pltpu.VMEM((SEQ, MLA_KV_LORA), BF16),
                        pltpu.VMEM((SEQ, MLA_ROPE), BF16),
                        pltpu.VMEM((SEQ, MLA_QK), BF16),
                        pltpu.VMEM((SEQ, MLA_QK), BF16),
                        pltpu.VMEM((SEQ, MLA_V), BF16)],
        compiler_params=_params("arbitrary", "arbitrary"),
        name="mla",
    )(lat3, q_norm, kv_norm, wq, wkv, cos, sin)


def _route(logits):
    lane = lax.broadcasted_iota(jnp.int32, logits.shape, 1)
    neg = -jnp.inf
    gl = jnp.where(lane < N_GROUPS, logits, neg)
    gmax = jnp.max(gl, axis=-1, keepdims=True)
    gsel = jnp.min(jnp.where(gl == gmax, lane, LANES), axis=-1, keepdims=True)
    p_grp = 1.0 / jnp.sum(jnp.exp(gl - gmax), axis=-1, keepdims=True)
    e_lane = lane - N_GROUPS
    in_grp = (e_lane >= 0) & (e_lane < N_EXPERTS) & ((e_lane >> 3) == gsel)
    el = jnp.where(in_grp, logits, neg)
    v0 = jnp.max(el, axis=-1, keepdims=True)
    i0 = jnp.min(jnp.where(el == v0, lane, LANES), axis=-1, keepdims=True)
    el1 = jnp.where(lane == i0, neg, el)
    v1 = jnp.max(el1, axis=-1, keepdims=True)
    i1 = jnp.min(jnp.where(el1 == v1, lane, LANES), axis=-1, keepdims=True)
    t = jnp.exp(v1 - v0)
    w0 = p_grp / (1.0 + t)
    w1 = p_grp * t / (1.0 + t)
    return i0, i1, w0, w1


def _lane_pack(shape, cols):
    lane = lax.broadcasted_iota(jnp.int32, shape, 1)
    out = jnp.zeros(shape, F32)
    for k, col in enumerate(cols):
        out = jnp.where(lane == k, col, out)
    return out


def _to_token_tiles(ref, val):
    n = val.shape[0]
    for s in range(ROW_TILES):
        ref[pl.ds(s, n, stride=ROW_TILES), :] = val[:, s * LANES:(s + 1) * LANES]


def _from_token_tiles(ref):
    return jnp.concatenate([ref[:, s, :] for s in range(ROW_TILES)], axis=-1)


def _merge_kernel(yr_ref, at_ref, gr_ref, gm_ref, x_ref, mod_ref, n2_ref, wro_ref, wmo_ref, wo_ref,
                  wrt_ref, brt_ref, h1_ref, u2_ref, meta_ref, cnt_ref, carry_ref):
    tm = x_ref.shape[0]

    @pl.when(pl.program_id(0) == 0)
    def _():
        carry_ref[...] = jnp.zeros_like(carry_ref)

    y_ret = _dot(yr_ref[...], wro_ref[...])
    y_mla = _dot(at_ref[...], wmo_ref[...])
    merged = _sigmoid(gr_ref[...].astype(F32)) * y_ret + _sigmoid(gm_ref[...].astype(F32)) * y_mla
    o = _dot(merged.astype(BF16), wo_ref[...])
    h1 = x_ref[...] + mod_ref[2:3, :] * o
    h1_ref[...] = h1
    u2 = _rms(h1) * n2_ref[...] * (1.0 + mod_ref[4:5, :]) + mod_ref[3:4, :]
    _to_token_tiles(u2_ref, u2)
    u_hi = u2.astype(BF16)
    u_lo = (u2 - u_hi.astype(F32)).astype(BF16)
    w = wrt_ref[...]
    w_hi = w.astype(BF16)
    w_lo = (w - w_hi.astype(F32)).astype(BF16)
    logits = _dot(u_hi, w_hi) + _dot(u_lo, w_hi) + _dot(u_hi, w_lo) + brt_ref[...]
    i0, i1, w0, w1 = _route(logits)
    lane = lax.broadcasted_iota(jnp.int32, (tm, LANES), 1)
    m0, m1 = lane == i0, lane == i1
    member = jnp.where(m0 | m1, 1.0, 0.0)
    tri = jnp.where(lax.broadcasted_iota(jnp.int32, (tm, tm), 0) > lax.broadcasted_iota(jnp.int32, (tm, tm), 1),
                    1.0, 0.0).astype(BF16)
    prefix = _dot(tri, member.astype(BF16)) + carry_ref[0:1, :]
    rank0 = jnp.sum(jnp.where(m0, prefix, 0.0), axis=-1, keepdims=True)
    rank1 = jnp.sum(jnp.where(m1, prefix, 0.0), axis=-1, keepdims=True)
    carry_ref[...] = carry_ref[...] + jnp.sum(member, axis=0, keepdims=True)
    cnt_ref[...] = carry_ref[...]
    meta_ref[...] = _lane_pack((tm, LANES), [i0.astype(F32), i1.astype(F32), rank0, rank1, w0, w1])


def _merge(y_ret, attn, gates, x2, mod3, norm2, w_ret_o, w_mla_o, w_out, w_rt, b_rt):
    tm = TM_PROJ
    per_b = SEQ // tm
    row = lambda j: pl.BlockSpec((tm, D_MODEL), lambda i: (i, j))
    return pl.pallas_call(
        _merge_kernel,
        grid=(N_TOK // tm,),
        in_specs=[row(0), row(0), row(0), row(1), row(0),
                  pl.BlockSpec((None, 6, D_MODEL), lambda i: (i // per_b, 0, 0)),
                  _resident((1, D_MODEL)),
                  _resident(w_ret_o.shape), _resident(w_mla_o.shape), _resident(w_out.shape),
                  _resident(w_rt.shape), _resident(b_rt.shape)],
        out_specs=[row(0),
                   pl.BlockSpec((tm * ROW_TILES, LANES), lambda i: (i, 0)),
                   pl.BlockSpec((tm, LANES), lambda i: (i, 0)),
                   pl.BlockSpec((SUBLANES, LANES), lambda i: (0, 0))],
        out_shape=[jax.ShapeDtypeStruct((N_TOK, D_MODEL), F32),
                   jax.ShapeDtypeStruct((N_TOK * ROW_TILES, LANES), F32),
                   jax.ShapeDtypeStruct((N_TOK, LANES), F32),
                   jax.ShapeDtypeStruct((SUBLANES, LANES), F32)],
        scratch_shapes=[pltpu.VMEM((SUBLANES, LANES), F32)],
        compiler_params=_params("arbitrary"),
        name="merge",
    )(y_ret, attn, gates, gates, x2, mod3, norm2, w_ret_o, w_mla_o, w_out, w_rt, b_rt)


def _plan_kernel(meta_ref, off_ref, dst_ref):
    m = meta_ref[...]
    lane = lax.broadcasted_iota(jnp.int32, m.shape, 1)
    off = off_ref[...]
    i0 = m[:, 0:1].astype(jnp.int32)
    i1 = m[:, 1:2].astype(jnp.int32)
    d0 = jnp.sum(jnp.where(lane == i0, off, 0.0), axis=-1, keepdims=True) + m[:, 2:3]
    d1 = jnp.sum(jnp.where(lane == i1, off, 0.0), axis=-1, keepdims=True) + m[:, 3:4]
    dst_ref[...] = _lane_pack(m.shape, [d0, d1]).astype(jnp.int32)


def _plan(meta, off_row):
    tm = 2048
    return pl.pallas_call(
        _plan_kernel,
        grid=(N_TOK // tm,),
        in_specs=[pl.BlockSpec((tm, LANES), lambda i: (i, 0)),
                  pl.BlockSpec((1, LANES), lambda i: (0, 0))],
        out_specs=pl.BlockSpec((tm, LANES), lambda i: (i, 0)),
        out_shape=jax.ShapeDtypeStruct((N_TOK, LANES), jnp.int32),
        compiler_params=_params("arbitrary"),
        name="plan",
    )(meta, off_row)


def _row_copy_wait(src_like, dst_like, sem):
    pltpu.make_async_copy(src_like, dst_like, sem).wait()


def _dispatch_kernel(d0_ref, d1_ref, seg_ref, u_ref, xs_ref, zero_ref, sem, zsem):
    i = pl.program_id(0)
    tm = u_ref.shape[0]

    @pl.when(i == 0)
    def _():
        zero_ref[...] = jnp.zeros_like(zero_ref)

        def tail(e):
            end = seg_ref[e + 1]
            return end > seg_ref[e], xs_ref.at[pl.ds(pl.multiple_of(end - TE, TE), TE)]

        for e in range(N_EXPERTS):
            nonempty, dst = tail(e)

            @pl.when(nonempty)
            def _():
                pltpu.make_async_copy(zero_ref, dst, zsem).start()

        for e in range(N_EXPERTS):
            nonempty, dst = tail(e)

            @pl.when(nonempty)
            def _():
                pltpu.make_async_copy(zero_ref, dst, zsem).wait()

        def unused(j):
            return pltpu.make_async_copy(zero_ref, xs_ref.at[pl.ds(pl.multiple_of(j * TE, TE), TE)], zsem)

        first_unused = seg_ref[N_EXPERTS] // TE
        lax.fori_loop(first_unused, N_TILES, lambda j, c: (unused(j).start(), c)[1], 0)
        lax.fori_loop(first_unused, N_TILES, lambda j, c: (unused(j).wait(), c)[1], 0)

    base = i * tm

    def body(r, carry):
        pltpu.make_async_copy(u_ref.at[r], xs_ref.at[d0_ref[base + r]], sem).start(priority=0)
        pltpu.make_async_copy(u_ref.at[r], xs_ref.at[d1_ref[base + r]], sem).start(priority=1)
        return carry

    lax.fori_loop(0, tm, body, 0, unroll=8)
    _row_copy_wait(u_ref, xs_ref.at[pl.ds(0, tm)], sem)
    _row_copy_wait(u_ref, xs_ref.at[pl.ds(0, tm)], sem)


def _dispatch(d0, d1, seg, u2t):
    tm = TM_PROJ
    return pl.pallas_call(
        _dispatch_kernel,
        grid_spec=pltpu.PrefetchScalarGridSpec(
            num_scalar_prefetch=3,
            grid=(N_TOK // tm,),
            in_specs=[pl.BlockSpec((tm, ROW_TILES, LANES), lambda i, *_: (i, 0, 0))],
            out_specs=pl.BlockSpec(memory_space=pl.ANY),
            scratch_shapes=[pltpu.VMEM((TE, ROW_TILES, LANES), F32),
                            pltpu.SemaphoreType.DMA(()), pltpu.SemaphoreType.DMA(())]),
        out_shape=jax.ShapeDtypeStruct((N_SLOTS, ROW_TILES, LANES), F32),
        compiler_params=_params("arbitrary"),
        name="dispatch",
    )(d0, d1, seg, u2t)


def _expert_kernel(te_ref, nv_ref, x_ref, w1_ref, w3_ref, w2_ref, y_ref, w1_s, w3_s, w2_s):
    j = pl.program_id(0)

    @pl.when(j < nv_ref[0])
    def _():
        @pl.when((j == 0) | (te_ref[j] != te_ref[jnp.maximum(j - 1, 0)]))
        def _():
            w1_s[...] = w1_ref[...].astype(BF16)
            w3_s[...] = w3_ref[...].astype(BF16)
            w2_s[...] = w2_ref[...].astype(BF16)

        x = _from_token_tiles(x_ref).astype(BF16)
        a = _dot(x, w1_s[...])
        b = _dot(x, w3_s[...])
        hid = (a * _sigmoid(a) * b).astype(BF16)
        y = _dot(hid, w2_s[...])
        for s in range(ROW_TILES):
            y_ref[:, s, :] = y[:, s * LANES:(s + 1) * LANES]

    @pl.when(j >= nv_ref[0])
    def _():
        y_ref[...] = jnp.zeros_like(y_ref)


def _experts(tile_expert, n_valid, xs, w1, w3, w2):
    tile = lambda j, te, nv: jnp.minimum(j, nv[0] - 1)
    wspec = lambda shape: pl.BlockSpec((None,) + shape, lambda j, te, nv: (te[tile(j, te, nv)], 0, 0))
    slots = pl.BlockSpec((TE, ROW_TILES, LANES), lambda j, te, nv: (tile(j, te, nv), 0, 0))
    return pl.pallas_call(
        _expert_kernel,
        grid_spec=pltpu.PrefetchScalarGridSpec(
            num_scalar_prefetch=2,
            grid=(N_TILES,),
            in_specs=[slots, wspec((D_MODEL, D_EXPERT)), wspec((D_MODEL, D_EXPERT)), wspec((D_EXPERT, D_MODEL))],
            out_specs=pl.BlockSpec((TE, ROW_TILES, LANES), lambda j, te, nv: (j, 0, 0)),
            scratch_shapes=[pltpu.VMEM((D_MODEL, D_EXPERT), BF16), pltpu.VMEM((D_MODEL, D_EXPERT), BF16),
                            pltpu.VMEM((D_EXPERT, D_MODEL), BF16)]),
        out_shape=jax.ShapeDtypeStruct((N_SLOTS, ROW_TILES, LANES), F32),
        compiler_params=_params("arbitrary"),
        name="experts",
    )(tile_expert, n_valid, xs, w1, w3, w2)


def _final_kernel(d0_ref, d1_ref, h1_ref, meta_ref, mod_ref, fn_ref, ys_ref, o_ref, ybuf, sem):
    i = pl.program_id(0)
    tm = h1_ref.shape[0]
    base = i * tm

    def body(r, carry):
        pltpu.make_async_copy(ys_ref.at[d0_ref[base + r]], ybuf.at[0, r], sem).start(priority=0)
        pltpu.make_async_copy(ys_ref.at[d1_ref[base + r]], ybuf.at[1, r], sem).start(priority=1)
        return carry

    lax.fori_loop(0, tm, body, 0, unroll=8)
    _row_copy_wait(ys_ref.at[pl.ds(0, tm)], ybuf.at[0], sem)
    _row_copy_wait(ys_ref.at[pl.ds(0, tm)], ybuf.at[1], sem)
    m = meta_ref[...]
    moe = m[:, 4:5] * _from_token_tiles(ybuf.at[0]) + m[:, 5:6] * _from_token_tiles(ybuf.at[1])
    h2 = h1_ref[...] + mod_ref[5:6, :] * moe
    o_ref[...] = _rms(h2) * fn_ref[...]


def _final(d0, d1, h1, meta, mod3, final_norm, ys):
    tm = TM_PROJ
    per_b = SEQ // tm
    return pl.pallas_call(
        _final_kernel,
        grid_spec=pltpu.PrefetchScalarGridSpec(
            num_scalar_prefetch=2,
            grid=(N_TOK // tm,),
            in_specs=[pl.BlockSpec((tm, D_MODEL), lambda i, *_: (i, 0)),
                      pl.BlockSpec((tm, LANES), lambda i, *_: (i, 0)),
                      pl.BlockSpec((None, 6, D_MODEL), lambda i, *_: (i // per_b, 0, 0)),
                      pl.BlockSpec((1, D_MODEL), lambda i, *_: (0, 0)),
                      pl.BlockSpec(memory_space=pl.ANY)],
            out_specs=pl.BlockSpec((tm, D_MODEL), lambda i, *_: (i, 0)),
            scratch_shapes=[pltpu.VMEM((2, tm, ROW_TILES, LANES), F32), pltpu.SemaphoreType.DMA(())]),
        out_shape=jax.ShapeDtypeStruct((N_TOK, D_MODEL), F32),
        compiler_params=_params("arbitrary"),
        name="final",
    )(d0, d1, h1, meta, mod3, final_norm, ys)


def _slot_layout(counts):
    cnt = counts[0, N_GROUPS:N_GROUPS + N_EXPERTS].astype(jnp.int32)
    tile_end = jnp.cumsum((cnt + TE - 1) // TE)
    seg = jnp.concatenate([jnp.zeros((1,), jnp.int32), tile_end * TE])
    off_row = jnp.zeros((1, LANES), F32).at[0, N_GROUPS:N_GROUPS + N_EXPERTS].set(seg[:-1].astype(F32))
    tile_ids = jnp.arange(N_TILES, dtype=jnp.int32)
    tile_expert = jnp.sum((tile_end[None, :] <= tile_ids[:, None]).astype(jnp.int32), axis=1)
    tile_expert = jnp.minimum(tile_expert, N_EXPERTS - 1)
    return seg, off_row, tile_expert, tile_end[-1:]


def _rope_tables(dim):
    pos = jnp.arange(SEQ, dtype=F32)
    inv = ROPE_THETA ** (-jnp.arange(0, dim, 2, dtype=F32) / dim)
    ang = pos[:, None] * inv[None, :]
    return jnp.cos(ang), jnp.sin(ang)


def _decay_tables():
    c = RET_CHUNK
    log_gamma = jnp.log1p(-jnp.exp2(-5.0 - jnp.arange(RET_HEADS, dtype=F32)))
    idx = jnp.arange(c, dtype=F32)
    rel = idx[:, None] - idx[None, :]
    dec = jnp.where(rel[None] >= 0, jnp.exp(log_gamma[:, None, None] * jnp.maximum(rel, 0.0)[None]), 0.0)
    xi = jnp.exp(log_gamma[:, None] * (idx[None, :] + 1.0))[:, :, None]
    zeta = jnp.exp(log_gamma[:, None] * (c - 1.0 - idx[None, :]))[:, :, None]
    cd = jnp.exp(log_gamma * c)[:, None, None]
    return dec, xi, zeta, cd


def _rotate_half_cols(w):
    half = w.shape[-1] // 2
    return jnp.concatenate([-w[..., half:], w[..., :half]], axis=-1)


def kernel(x, c, w_ada, b_ada, norm1, norm2, w_in, w_ret_o, q_norm, kv_norm, w_uq, w_ukv, w_mla_o, w_out,
           w_grp, b_grp, w_exp, b_exp, w1, w3, w2, final_norm):
    assert x.shape == (BATCH, SEQ, D_MODEL) and w_ada.shape[0] == 1
    x2 = x.reshape(N_TOK, D_MODEL)

    o_lat = 4 * RET_W
    o_pe = o_lat + MLA_Q_LORA + MLA_KV_LORA
    o_gate = o_pe + MLA_ROPE
    wi = w_in[0]
    w_ret = wi[:, :o_lat].astype(BF16)
    w_lat = jnp.concatenate([wi[:, o_lat:o_gate], _rotate_half_cols(wi[:, o_pe:o_gate])], axis=1).astype(BF16)
    w_gate = wi[:, o_gate:].astype(BF16)
    wq = w_uq[0].reshape(MLA_Q_LORA, MLA_HEADS, MLA_QK)
    wq = jnp.concatenate([wq, _rotate_half_cols(wq[..., MLA_NOPE:])], axis=-1)
    wq = wq.transpose(1, 0, 2).astype(BF16)
    wkv = w_ukv[0].reshape(MLA_KV_LORA, MLA_HEADS, MLA_NOPE + MLA_V).transpose(1, 0, 2).astype(BF16)
    pad = LANES - N_GROUPS - N_EXPERTS
    w_rt = jnp.concatenate([w_grp[0], w_exp[0], jnp.zeros((D_MODEL, pad), F32)], axis=1)
    b_rt = jnp.concatenate([b_grp[0], b_exp[0], jnp.zeros((pad,), F32)])[None, :]

    ret_cos, ret_sin = _rope_tables(RET_DK)
    mla_cos, mla_sin = _rope_tables(MLA_ROPE)
    mla_cos = jnp.concatenate([mla_cos, mla_cos], axis=-1)
    mla_sin = jnp.concatenate([mla_sin, mla_sin], axis=-1)
    dec, xi, zeta, cd = _decay_tables()

    mod3 = _ada(c, w_ada[0], b_ada[0]).reshape(BATCH, 6, D_MODEL)
    ret, lat, gates = _inproj(x2, mod3, norm1, w_ret, w_lat, w_gate)
    y_ret = _retention(ret.reshape(BATCH, SEQ, 4 * RET_W), ret_cos, ret_sin, dec, xi, zeta, cd)
    attn = _mla(lat.reshape(BATCH, SEQ, MLA_LAT_W), q_norm, kv_norm, wq, wkv, mla_cos, mla_sin)
    h1, u2t, meta, counts = _merge(y_ret.reshape(N_TOK, D_MODEL), attn.reshape(N_TOK, D_MODEL), gates, x2, mod3,
                                   norm2, w_ret_o[0].astype(BF16), w_mla_o[0].astype(BF16),
                                   w_out[0].astype(BF16), w_rt, b_rt)
    seg, off_row, tile_expert, n_valid = _slot_layout(counts)
    dst = _plan(meta, off_row)
    d0, d1 = dst[:, 0], dst[:, 1]
    xs = _dispatch(d0, d1, seg, u2t.reshape(N_TOK, ROW_TILES, LANES))
    e_shape = (N_EXPERTS, D_MODEL, D_EXPERT)
    ys = _experts(tile_expert, n_valid, xs, w1[0].reshape(e_shape), w3[0].reshape(e_shape),
                  w2[0].reshape(N_EXPERTS, D_EXPERT, D_MODEL))
    out = _final(d0, d1, h1, meta, mod3, final_norm.reshape(1, D_MODEL), ys)
    return out.reshape(BATCH, SEQ, D_MODEL)
```

```python
import functools

import numpy as np
import jax
import jax.numpy as jnp
from jax import lax
from jax.experimental import pallas as pl
from jax.experimental.pallas import tpu as pltpu

D_MODEL = 1024
BATCH = 8
SEQ = 2048
N_TOK = BATCH * SEQ

RET_HEADS = 4
RET_DK = 256
RET_DV = 256
RET_CHUNK = 256
RET_BLK = 512
RET_W = RET_HEADS * RET_DK

MLA_HEADS = 8
MLA_NOPE = 128
MLA_ROPE = 64
MLA_V = 128
MLA_Q_LORA = 384
MLA_KV_LORA = 256
MLA_LAT_W = MLA_Q_LORA + MLA_KV_LORA + 2 * MLA_ROPE
MLA_QK = MLA_NOPE + MLA_ROPE
ROPE_THETA = 10000.0

N_GROUPS = 4
EXPERTS_PER_GROUP = 8
N_EXPERTS = N_GROUPS * EXPERTS_PER_GROUP
D_EXPERT = 256
EPS = 1e-6
LOG2_E = 1.4426950408889634

LANES = 128
SUBLANES = 8
ROW_TILES = D_MODEL // LANES
VMEM_LIMIT = 56 * 1024 * 1024

TM_PROJ = 512
TQ = 256
TE = 256
TOP_K = 2
N_TILES = N_TOK * TOP_K // TE + N_EXPERTS
N_SLOTS = N_TILES * TE

F32 = jnp.float32
BF16 = jnp.bfloat16


def _sigmoid(x):
    return 1.0 / (1.0 + jnp.exp(-x))


def _rms(x):
    return x * lax.rsqrt(jnp.mean(x * x, axis=-1, keepdims=True) + EPS)


def _dot(a, b):
    return jnp.dot(a, b, preferred_element_type=F32)


def _dot_nt(a, b):
    return lax.dot_general(a, b, (((1,), (1,)), ((), ())), preferred_element_type=F32)


def _dot_tn(a, b):
    return lax.dot_general(a, b, (((0,), (0,)), ((), ())), preferred_element_type=F32)


def _params(*sem):
    return pltpu.CompilerParams(dimension_semantics=sem, vmem_limit_bytes=VMEM_LIMIT)


def _resident(shape):
    nd = len(shape)
    return pl.BlockSpec(shape, lambda *_: (0,) * nd, pipeline_mode=pl.Buffered(1))


def _ada_kernel(c_ref, w_ref, b_ref, o_ref):
    c = c_ref[...]
    act = (c * _sigmoid(c)).astype(BF16)
    o_ref[...] = _dot(act, w_ref[...].astype(BF16)) + b_ref[...]


def _ada(c, w_ada, b_ada):
    n = w_ada.shape[1]
    tn = D_MODEL
    return pl.pallas_call(
        _ada_kernel,
        grid=(n // tn,),
        in_specs=[pl.BlockSpec((BATCH, D_MODEL), lambda j: (0, 0)),
                  pl.BlockSpec((D_MODEL, tn), lambda j: (0, j)),
                  pl.BlockSpec((1, tn), lambda j: (0, j))],
        out_specs=pl.BlockSpec((BATCH, tn), lambda j: (0, j)),
        out_shape=jax.ShapeDtypeStruct((BATCH, n), F32),
        compiler_params=_params("arbitrary"),
        name="ada",
    )(c, w_ada, b_ada.reshape(1, n))


def _inproj_kernel(x_ref, mod_ref, n1_ref, wr_ref, wm_ref, wg_ref, ret_ref, lat_ref, gate_ref):
    y = _rms(x_ref[...]) * n1_ref[...]
    u = (y * (1.0 + mod_ref[1:2, :]) + mod_ref[0:1, :]).astype(BF16)
    step = 512
    for n in range(0, 4 * RET_W, step):
        ret_ref[:, n:n + step] = _dot(u, wr_ref[:, n:n + step]).astype(BF16)
    lat_ref[...] = _dot(u, wm_ref[...]).astype(BF16)
    for n in range(0, 2 * D_MODEL, step):
        gate_ref[:, n:n + step] = _dot(u, wg_ref[:, n:n + step]).astype(BF16)


def _inproj(x2, mod3, norm1, w_ret, w_lat, w_gate):
    tm = TM_PROJ
    per_b = SEQ // tm
    return pl.pallas_call(
        _inproj_kernel,
        grid=(N_TOK // tm,),
        in_specs=[pl.BlockSpec((tm, D_MODEL), lambda i: (i, 0)),
                  pl.BlockSpec((None, 6, D_MODEL), lambda i: (i // per_b, 0, 0)),
                  _resident((1, D_MODEL)),
                  _resident(w_ret.shape), _resident(w_lat.shape), _resident(w_gate.shape)],
        out_specs=[pl.BlockSpec((tm, 4 * RET_W), lambda i: (i, 0)),
                   pl.BlockSpec((tm, MLA_LAT_W), lambda i: (i, 0)),
                   pl.BlockSpec((tm, 2 * D_MODEL), lambda i: (i, 0))],
        out_shape=[jax.ShapeDtypeStruct((N_TOK, 4 * RET_W), BF16),
                   jax.ShapeDtypeStruct((N_TOK, MLA_LAT_W), BF16),
                   jax.ShapeDtypeStruct((N_TOK, 2 * D_MODEL), BF16)],
        compiler_params=_params("arbitrary"),
        name="inproj",
    )(x2, mod3, norm1, w_ret, w_lat, w_gate)


def _ret_kernel(q_ref, k_ref, v_ref, g_ref, cos_ref, sin_ref, dec_ref, xi_ref, zeta_ref, cd_ref,
                o_ref, state_ref):
    half = RET_DK // 2

    @pl.when(pl.program_id(1) == 0)
    def _():
        state_ref[...] = jnp.zeros_like(state_ref)

    def rope(x, cos, sin):
        x1, x2 = x[:, :half], x[:, half:]
        return jnp.concatenate([x1 * cos - x2 * sin, x2 * cos + x1 * sin], axis=-1)

    for c in range(RET_BLK // RET_CHUNK):
        rows = slice(c * RET_CHUNK, (c + 1) * RET_CHUNK)
        cos, sin = cos_ref[rows, :], sin_ref[rows, :]
        for h in range(RET_HEADS):
            cols = slice(h * RET_DK, (h + 1) * RET_DK)
            q = rope(q_ref[rows, cols].astype(F32), cos, sin)
            k = rope(k_ref[rows, cols].astype(F32), cos, sin) * (RET_DK ** -0.5)
            v = v_ref[rows, cols]
            scores = _dot_nt(q.astype(BF16), k.astype(BF16)) * dec_ref[h]
            state = state_ref[h]
            y = _dot(scores.astype(BF16), v) + _dot((q * xi_ref[h]).astype(BF16), state.astype(BF16))
            state_ref[h] = state * cd_ref[h] + _dot_tn((k * zeta_ref[h]).astype(BF16), v)
            mu = jnp.mean(y, axis=-1, keepdims=True)
            yc = y - mu
            yn = yc * lax.rsqrt(jnp.mean(yc * yc, axis=-1, keepdims=True) + EPS)
            g = g_ref[rows, cols].astype(F32)
            o_ref[rows, cols] = (g * _sigmoid(g) * yn).astype(BF16)


def _retention(ret3, cos, sin, dec, xi, zeta, cd):
    blk = lambda part: pl.BlockSpec((None, RET_BLK, RET_W), lambda b, j: (b, j, part))
    rope_tab = pl.BlockSpec((RET_BLK, RET_DK // 2), lambda b, j: (j, 0))
    whole = lambda a: pl.BlockSpec(a.shape, lambda b, j: (0,) * a.ndim)
    return pl.pallas_call(
        _ret_kernel,
        grid=(BATCH, SEQ // RET_BLK),
        in_specs=[blk(0), blk(1), blk(2), blk(3), rope_tab, rope_tab,
                  whole(dec), whole(xi), whole(zeta), whole(cd)],
        out_specs=pl.BlockSpec((None, RET_BLK, RET_HEADS * RET_DV), lambda b, j: (b, j, 0)),
        out_shape=jax.ShapeDtypeStruct((BATCH, SEQ, RET_HEADS * RET_DV), BF16),
        scratch_shapes=[pltpu.VMEM((RET_HEADS, RET_DK, RET_DV), F32)],
        compiler_params=_params("arbitrary", "arbitrary"),
        name="retention",
    )(ret3, ret3, ret3, ret3, cos, sin, dec, xi, zeta, cd)


def _mla_kernel(lat_ref, qn_ref, kvn_ref, wq_ref, wkv_ref, cos_ref, sin_ref, o_ref,
                cq_s, ckv_s, kpe_s, q_s, k_s, v_s):
    h = pl.program_id(1)
    o_q, o_kv, o_pe, o_rot = 0, MLA_Q_LORA, MLA_Q_LORA + MLA_KV_LORA, MLA_Q_LORA + MLA_KV_LORA + MLA_ROPE
    cos, sin = cos_ref[...], sin_ref[...]

    @pl.when(h == 0)
    def _():
        cq_s[...] = (_rms(lat_ref[:, o_q:o_kv].astype(F32)) * qn_ref[...]).astype(BF16)
        ckv_s[...] = (_rms(lat_ref[:, o_kv:o_pe].astype(F32)) * kvn_ref[...]).astype(BF16)
        pe = lat_ref[:, o_pe:o_rot].astype(F32)
        rot = lat_ref[:, o_rot:o_rot + MLA_ROPE].astype(F32)
        kpe_s[...] = (pe * cos + rot * sin).astype(BF16)

    scale = (MLA_QK ** -0.5) * LOG2_E
    qf = _dot(cq_s[...], wq_ref[...])
    q_s[:, :MLA_NOPE] = (qf[:, :MLA_NOPE] * scale).astype(BF16)
    q_pe = qf[:, MLA_NOPE:MLA_QK] * cos + qf[:, MLA_QK:] * sin
    q_s[:, MLA_NOPE:] = (q_pe * scale).astype(BF16)
    kvf = _dot(ckv_s[...], wkv_ref[...])
    k_s[:, :MLA_NOPE] = kvf[:, :MLA_NOPE].astype(BF16)
    k_s[:, MLA_NOPE:] = kpe_s[...]
    v_s[:, :MLA_V] = kvf[:, MLA_NOPE:].astype(BF16)
    v_s[:, MLA_V:] = jnp.ones((SEQ, MLA_V), BF16)

    causal = lax.broadcasted_iota(jnp.int32, (TQ, TQ), 0) >= lax.broadcasted_iota(jnp.int32, (TQ, TQ), 1)
    for i in range(SEQ // TQ):
        lo, hi = i * TQ, (i + 1) * TQ
        q = q_s[lo:hi, :]
        s_diag = jnp.where(causal, _dot_nt(q, k_s[lo:hi, :]), -jnp.inf)
        m = jnp.max(s_diag, axis=-1, keepdims=True)
        if i > 0:
            s_past = _dot_nt(q, k_s[:lo, :])
            m = jnp.maximum(m, jnp.max(s_past, axis=-1, keepdims=True))
            acc = _dot(jnp.exp2(s_past - m).astype(BF16), v_s[:lo, :])
            acc = acc + _dot(jnp.exp2(s_diag - m).astype(BF16), v_s[lo:hi, :])
        else:
            acc = _dot(jnp.exp2(s_diag - m).astype(BF16), v_s[lo:hi, :])
        o_ref[lo:hi, :] = (acc[:, :MLA_V] / acc[:, MLA_V:]).astype(BF16)


def _mla(lat3, q_norm, kv_norm, wq, wkv, cos, sin):
    return pl.pallas_call(
        _mla_kernel,
        grid=(BATCH, MLA_HEADS),
        in_specs=[pl.BlockSpec((None, SEQ, MLA_LAT_W), lambda b, h: (b, 0, 0)),
                  pl.BlockSpec((1, MLA_Q_LORA), lambda b, h: (0, 0)),
                  pl.BlockSpec((1, MLA_KV_LORA), lambda b, h: (0, 0)),
                  pl.BlockSpec((None, MLA_Q_LORA, MLA_QK + MLA_ROPE), lambda b, h: (h, 0, 0)),
                  pl.BlockSpec((None, MLA_KV_LORA, MLA_NOPE + MLA_V), lambda b, h: (h, 0, 0)),
                  pl.BlockSpec((SEQ, MLA_ROPE), lambda b, h: (0, 0)),
                  pl.BlockSpec((SEQ, MLA_ROPE), lambda b, h: (0, 0))],
        out_specs=pl.BlockSpec((None, SEQ, MLA_V), lambda b, h: (b, 0, h)),
        out_shape=jax.ShapeDtypeStruct((BATCH, SEQ, MLA_HEADS * MLA_V), BF16),
        scratch_shapes=[pltpu.VMEM((SEQ, MLA_Q_LORA), BF16),
                        pltpu.VMEM((SEQ, MLA_KV_LORA), BF16),
                        pltpu.VMEM((SEQ, MLA_ROPE), BF16),
                        pltpu.VMEM((SEQ, MLA_QK), BF16),
                        pltpu.VMEM((SEQ, MLA_QK), BF16),
                        pltpu.VMEM((SEQ, 2 * MLA_V), BF16)],
        compiler_params=_params("arbitrary", "arbitrary"),
        name="mla",
    )(lat3, q_norm, kv_norm, wq, wkv, cos, sin)


def _route(logits):
    lane = lax.broadcasted_iota(jnp.int32, logits.shape, 1)
    neg = -jnp.inf
    gl = jnp.where(lane < N_GROUPS, logits, neg)
    gmax = jnp.max(gl, axis=-1, keepdims=True)
    gsel = jnp.min(jnp.where(gl == gmax, lane, LANES), axis=-1, keepdims=True)
    p_grp = 1.0 / jnp.sum(jnp.exp(gl - gmax), axis=-1, keepdims=True)
    e_lane = lane - N_GROUPS
    in_grp = (e_lane >= 0) & (e_lane < N_EXPERTS) & ((e_lane >> 3) == gsel)
    el = jnp.where(in_grp, logits, neg)
    v0 = jnp.max(el, axis=-1, keepdims=True)
    i0 = jnp.min(jnp.where(el == v0, lane, LANES), axis=-1, keepdims=True)
    el1 = jnp.where(lane == i0, neg, el)
    v1 = jnp.max(el1, axis=-1, keepdims=True)
    i1 = jnp.min(jnp.where(el1 == v1, lane, LANES), axis=-1, keepdims=True)
    t = jnp.exp(v1 - v0)
    w0 = p_grp / (1.0 + t)
    w1 = p_grp * t / (1.0 + t)
    return i0, i1, w0, w1


def _lane_pack(shape, cols):
    lane = lax.broadcasted_iota(jnp.int32, shape, 1)
    out = jnp.zeros(shape, F32)
    for k, col in enumerate(cols):
        out = jnp.where(lane == k, col, out)
    return out


def _to_token_tiles(ref, val):
    n = val.shape[0]
    for s in range(ROW_TILES):
        ref[pl.ds(s, n, stride=ROW_TILES), :] = val[:, s * LANES:(s + 1) * LANES]


def _from_token_tiles(ref):
    n = ref.shape[0] // ROW_TILES
    return jnp.concatenate([ref[pl.ds(s, n, stride=ROW_TILES), :] for s in range(ROW_TILES)], axis=-1)


def _token_rows(ref, t):
    return ref.at[pl.ds(pl.multiple_of(t * ROW_TILES, ROW_TILES), ROW_TILES)]


def _merge_kernel(yr_ref, at_ref, gr_ref, gm_ref, x_ref, mod_ref, n2_ref, wro_ref, wmo_ref, wo_ref,
                  wrt_ref, brt_ref, h1_ref, u2_ref, meta_ref, cnt_ref, carry_ref):
    tm = x_ref.shape[0]

    @pl.when(pl.program_id(0) == 0)
    def _():
        carry_ref[...] = jnp.zeros_like(carry_ref)

    y_ret = _dot(yr_ref[...], wro_ref[...])
    y_mla = _dot(at_ref[...], wmo_ref[...])
    merged = _sigmoid(gr_ref[...].astype(F32)) * y_ret + _sigmoid(gm_ref[...].astype(F32)) * y_mla
    o = _dot(merged.astype(BF16), wo_ref[...])
    h1 = x_ref[...] + mod_ref[2:3, :] * o
    h1_ref[...] = h1
    u2 = _rms(h1) * n2_ref[...] * (1.0 + mod_ref[4:5, :]) + mod_ref[3:4, :]
    _to_token_tiles(u2_ref, u2)
    u_hi = u2.astype(BF16)
    u_lo = (u2 - u_hi.astype(F32)).astype(BF16)
    w = wrt_ref[...]
    w_hi = w.astype(BF16)
    w_lo = (w - w_hi.astype(F32)).astype(BF16)
    logits = _dot(u_hi, w_hi) + _dot(u_lo, w_hi) + _dot(u_hi, w_lo) + brt_ref[...]
    i0, i1, w0, w1 = _route(logits)
    lane = lax.broadcasted_iota(jnp.int32, (tm, LANES), 1)
    m0, m1 = lane == i0, lane == i1
    member = jnp.where(m0 | m1, 1.0, 0.0)
    tri = jnp.where(lax.broadcasted_iota(jnp.int32, (tm, tm), 0) > lax.broadcasted_iota(jnp.int32, (tm, tm), 1),
                    1.0, 0.0).astype(BF16)
    prefix = _dot(tri, member.astype(BF16)) + carry_ref[0:1, :]
    rank0 = jnp.sum(jnp.where(m0, prefix, 0.0), axis=-1, keepdims=True)
    rank1 = jnp.sum(jnp.where(m1, prefix, 0.0), axis=-1, keepdims=True)
    carry_ref[...] = carry_ref[...] + jnp.sum(member, axis=0, keepdims=True)
    cnt_ref[...] = carry_ref[...]
    meta_ref[...] = _lane_pack((tm, LANES), [i0.astype(F32), i1.astype(F32), rank0, rank1, w0, w1])


def _merge(y_ret, attn, gates, x2, mod3, norm2, w_ret_o, w_mla_o, w_out, w_rt, b_rt):
    tm = TM_PROJ
    per_b = SEQ // tm
    row = lambda j: pl.BlockSpec((tm, D_MODEL), lambda i: (i, j))
    return pl.pallas_call(
        _merge_kernel,
        grid=(N_TOK // tm,),
        in_specs=[row(0), row(0), row(0), row(1), row(0),
                  pl.BlockSpec((None, 6, D_MODEL), lambda i: (i // per_b, 0, 0)),
                  _resident((1, D_MODEL)),
                  _resident(w_ret_o.shape), _resident(w_mla_o.shape), _resident(w_out.shape),
                  _resident(w_rt.shape), _resident(b_rt.shape)],
        out_specs=[row(0),
                   pl.BlockSpec((tm * ROW_TILES, LANES), lambda i: (i, 0)),
                   pl.BlockSpec((tm, LANES), lambda i: (i, 0)),
                   pl.BlockSpec((SUBLANES, LANES), lambda i: (0, 0))],
        out_shape=[jax.ShapeDtypeStruct((N_TOK, D_MODEL), F32),
                   jax.ShapeDtypeStruct((N_TOK * ROW_TILES, LANES), F32),
                   jax.ShapeDtypeStruct((N_TOK, LANES), F32),
                   jax.ShapeDtypeStruct((SUBLANES, LANES), F32)],
        scratch_shapes=[pltpu.VMEM((SUBLANES, LANES), F32)],
        compiler_params=_params("arbitrary"),
        name="merge",
    )(y_ret, attn, gates, gates, x2, mod3, norm2, w_ret_o, w_mla_o, w_out, w_rt, b_rt)


def _plan_kernel(meta_ref, off_ref, dst_ref):
    m = meta_ref[...]
    lane = lax.broadcasted_iota(jnp.int32, m.shape, 1)
    off = off_ref[...]
    i0 = m[:, 0:1].astype(jnp.int32)
    i1 = m[:, 1:2].astype(jnp.int32)
    d0 = jnp.sum(jnp.where(lane == i0, off, 0.0), axis=-1, keepdims=True) + m[:, 2:3]
    d1 = jnp.sum(jnp.where(lane == i1, off, 0.0), axis=-1, keepdims=True) + m[:, 3:4]
    dst_ref[...] = _lane_pack(m.shape, [d0, d1]).astype(jnp.int32)


def _plan(meta, off_row):
    tm = 2048
    return pl.pallas_call(
        _plan_kernel,
        grid=(N_TOK // tm,),
        in_specs=[pl.BlockSpec((tm, LANES), lambda i: (i, 0)),
                  pl.BlockSpec((1, LANES), lambda i: (0, 0))],
        out_specs=pl.BlockSpec((tm, LANES), lambda i: (i, 0)),
        out_shape=jax.ShapeDtypeStruct((N_TOK, LANES), jnp.int32),
        compiler_params=_params("arbitrary"),
        name="plan",
    )(meta, off_row)


def _row_copy_wait(src_like, dst_like, sem):
    pltpu.make_async_copy(src_like, dst_like, sem).wait()


def _dispatch_kernel(d0_ref, d1_ref, seg_ref, u_ref, xs_ref, zero_ref, sem, zsem):
    i = pl.program_id(0)
    tm = u_ref.shape[0] // ROW_TILES
    tile_rows = TE * ROW_TILES

    def slot_tile(j):
        return xs_ref.at[pl.ds(pl.multiple_of(j * tile_rows, tile_rows), tile_rows)]

    @pl.when(i == 0)
    def _():
        zero_ref[...] = jnp.zeros_like(zero_ref)

        def tail(e):
            end = seg_ref[e + 1]
            return end > seg_ref[e], slot_tile(end // TE - 1)

        for e in range(N_EXPERTS):
            nonempty, dst = tail(e)

            @pl.when(nonempty)
            def _():
                pltpu.make_async_copy(zero_ref, dst, zsem).start()

        for e in range(N_EXPERTS):
            nonempty, dst = tail(e)

            @pl.when(nonempty)
            def _():
                pltpu.make_async_copy(zero_ref, dst, zsem).wait()

        def unused(j):
            return pltpu.make_async_copy(zero_ref, slot_tile(j), zsem)

        first_unused = seg_ref[N_EXPERTS] // TE
        lax.fori_loop(first_unused, N_TILES, lambda j, c: (unused(j).start(), c)[1], 0)
        lax.fori_loop(first_unused, N_TILES, lambda j, c: (unused(j).wait(), c)[1], 0)

    base = i * tm

    def body(r, carry):
        src = _token_rows(u_ref, r)
        pltpu.make_async_copy(src, _token_rows(xs_ref, d0_ref[base + r]), sem).start(priority=0)
        pltpu.make_async_copy(src, _token_rows(xs_ref, d1_ref[base + r]), sem).start(priority=1)
        return carry

    lax.fori_loop(0, tm, body, 0, unroll=8)
    _row_copy_wait(u_ref, xs_ref.at[pl.ds(0, tm * ROW_TILES)], sem)
    _row_copy_wait(u_ref, xs_ref.at[pl.ds(0, tm * ROW_TILES)], sem)


def _dispatch(d0, d1, seg, u2t):
    tm = TM_PROJ
    return pl.pallas_call(
        _dispatch_kernel,
        grid_spec=pltpu.PrefetchScalarGridSpec(
            num_scalar_prefetch=3,
            grid=(N_TOK // tm,),
            in_specs=[pl.BlockSpec((tm * ROW_TILES, LANES), lambda i, *_: (i, 0))],
            out_specs=pl.BlockSpec(memory_space=pl.ANY),
            scratch_shapes=[pltpu.VMEM((TE * ROW_TILES, LANES), F32),
                            pltpu.SemaphoreType.DMA(()), pltpu.SemaphoreType.DMA(())]),
        out_shape=jax.ShapeDtypeStruct((N_SLOTS * ROW_TILES, LANES), F32),
        compiler_params=_params("arbitrary"),
        name="dispatch",
    )(d0, d1, seg, u2t)


def _expert_kernel(te_ref, nv_ref, x_ref, w1_ref, w3_ref, w2_ref, y_ref, w1_s, w3_s, w2_s):
    j = pl.program_id(0)

    @pl.when(j < nv_ref[0])
    def _():
        @pl.when((j == 0) | (te_ref[j] != te_ref[jnp.maximum(j - 1, 0)]))
        def _():
            w1_s[...] = w1_ref[...].astype(BF16)
            w3_s[...] = w3_ref[...].astype(BF16)
            w2_s[...] = w2_ref[...].astype(BF16)

        x = _from_token_tiles(x_ref).astype(BF16)
        a = _dot(x, w1_s[...])
        b = _dot(x, w3_s[...])
        hid = (a * _sigmoid(a) * b).astype(BF16)
        _to_token_tiles(y_ref, _dot(hid, w2_s[...]))

    @pl.when(j >= nv_ref[0])
    def _():
        y_ref[...] = jnp.zeros_like(y_ref)


def _experts(tile_expert, n_valid, xs, w1, w3, w2):
    tile = lambda j, te, nv: jnp.minimum(j, nv[0] - 1)
    wspec = lambda shape: pl.BlockSpec((None,) + shape, lambda j, te, nv: (te[tile(j, te, nv)], 0, 0))
    slots = pl.BlockSpec((TE * ROW_TILES, LANES), lambda j, te, nv: (tile(j, te, nv), 0))
    return pl.pallas_call(
        _expert_kernel,
        grid_spec=pltpu.PrefetchScalarGridSpec(
            num_scalar_prefetch=2,
            grid=(N_TILES,),
            in_specs=[slots, wspec((D_MODEL, D_EXPERT)), wspec((D_MODEL, D_EXPERT)), wspec((D_EXPERT, D_MODEL))],
            out_specs=pl.BlockSpec((TE * ROW_TILES, LANES), lambda j, te, nv: (j, 0)),
            scratch_shapes=[pltpu.VMEM((D_MODEL, D_EXPERT), BF16), pltpu.VMEM((D_MODEL, D_EXPERT), BF16),
                            pltpu.VMEM((D_EXPERT, D_MODEL), BF16)]),
        out_shape=jax.ShapeDtypeStruct((N_SLOTS * ROW_TILES, LANES), F32),
        compiler_params=_params("arbitrary"),
        name="experts",
    )(tile_expert, n_valid, xs, w1, w3, w2)


def _final_kernel(d0_ref, d1_ref, h1_ref, meta_ref, mod_ref, fn_ref, ys_ref, o_ref, ybuf, sem):
    i = pl.program_id(0)
    tm = h1_ref.shape[0]
    base = i * tm

    def body(r, carry):
        pltpu.make_async_copy(_token_rows(ys_ref, d0_ref[base + r]), _token_rows(ybuf.at[0], r), sem).start(priority=0)
        pltpu.make_async_copy(_token_rows(ys_ref, d1_ref[base + r]), _token_rows(ybuf.at[1], r), sem).start(priority=1)
        return carry

    lax.fori_loop(0, tm, body, 0, unroll=8)
    _row_copy_wait(ys_ref.at[pl.ds(0, tm * ROW_TILES)], ybuf.at[0], sem)
    _row_copy_wait(ys_ref.at[pl.ds(0, tm * ROW_TILES)], ybuf.at[1], sem)
    m = meta_ref[...]
    moe = m[:, 4:5] * _from_token_tiles(ybuf.at[0]) + m[:, 5:6] * _from_token_tiles(ybuf.at[1])
    h2 = h1_ref[...] + mod_ref[5:6, :] * moe
    o_ref[...] = _rms(h2) * fn_ref[...]


def _final(d0, d1, h1, meta, mod3, final_norm, ys):
    tm = TM_PROJ
    per_b = SEQ // tm
    return pl.pallas_call(
        _final_kernel,
        grid_spec=pltpu.PrefetchScalarGridSpec(
            num_scalar_prefetch=2,
            grid=(N_TOK // tm,),
            in_specs=[pl.BlockSpec((tm, D_MODEL), lambda i, *_: (i, 0)),
                      pl.BlockSpec((tm, LANES), lambda i, *_: (i, 0)),
                      pl.BlockSpec((None, 6, D_MODEL), lambda i, *_: (i // per_b, 0, 0)),
                      pl.BlockSpec((1, D_MODEL), lambda i, *_: (0, 0)),
                      pl.BlockSpec(memory_space=pl.ANY)],
            out_specs=pl.BlockSpec((tm, D_MODEL), lambda i, *_: (i, 0)),
            scratch_shapes=[pltpu.VMEM((2, tm * ROW_TILES, LANES), F32), pltpu.SemaphoreType.DMA(())]),
        out_shape=jax.ShapeDtypeStruct((N_TOK, D_MODEL), F32),
        compiler_params=_params("arbitrary"),
        name="final",
    )(d0, d1, h1, meta, mod3, final_norm, ys)


def _slot_layout(counts):
    cnt = counts[0, N_GROUPS:N_GROUPS + N_EXPERTS].astype(jnp.int32)
    tile_end = jnp.cumsum((cnt + TE - 1) // TE)
    seg = jnp.concatenate([jnp.zeros((1,), jnp.int32), tile_end * TE])
    off_row = jnp.zeros((1, LANES), F32).at[0, N_GROUPS:N_GROUPS + N_EXPERTS].set(seg[:-1].astype(F32))
    tile_ids = jnp.arange(N_TILES, dtype=jnp.int32)
    tile_expert = jnp.sum((tile_end[None, :] <= tile_ids[:, None]).astype(jnp.int32), axis=1)
    tile_expert = jnp.minimum(tile_expert, N_EXPERTS - 1)
    return seg, off_row, tile_expert, tile_end[-1:]


def _rope_tables(dim):
    pos = jnp.arange(SEQ, dtype=F32)
    inv = ROPE_THETA ** (-jnp.arange(0, dim, 2, dtype=F32) / dim)
    ang = pos[:, None] * inv[None, :]
    return jnp.cos(ang), jnp.sin(ang)


def _decay_tables():
    c = RET_CHUNK
    log_gamma = jnp.log1p(-jnp.exp2(-5.0 - jnp.arange(RET_HEADS, dtype=F32)))
    idx = jnp.arange(c, dtype=F32)
    rel = idx[:, None] - idx[None, :]
    dec = jnp.where(rel[None] >= 0, jnp.exp(log_gamma[:, None, None] * jnp.maximum(rel, 0.0)[None]), 0.0)
    xi = jnp.exp(log_gamma[:, None] * (idx[None, :] + 1.0))[:, :, None]
    zeta = jnp.exp(log_gamma[:, None] * (c - 1.0 - idx[None, :]))[:, :, None]
    cd = jnp.exp(log_gamma * c)[:, None, None]
    return dec, xi, zeta, cd


def _rotate_half_cols(w):
    half = w.shape[-1] // 2
    return jnp.concatenate([-w[..., half:], w[..., :half]], axis=-1)


def kernel(x, c, w_ada, b_ada, norm1, norm2, w_in, w_ret_o, q_norm, kv_norm, w_uq, w_ukv, w_mla_o, w_out,
           w_grp, b_grp, w_exp, b_exp, w1, w3, w2, final_norm):
    assert x.shape == (BATCH, SEQ, D_MODEL) and w_ada.shape[0] == 1
    x2 = x.reshape(N_TOK, D_MODEL)

    o_lat = 4 * RET_W
    o_pe = o_lat + MLA_Q_LORA + MLA_KV_LORA
    o_gate = o_pe + MLA_ROPE
    wi = w_in[0]
    w_ret = wi[:, :o_lat].astype(BF16)
    w_lat = jnp.concatenate([wi[:, o_lat:o_gate], _rotate_half_cols(wi[:, o_pe:o_gate])], axis=1).astype(BF16)
    w_gate = wi[:, o_gate:].astype(BF16)
    wq = w_uq[0].reshape(MLA_Q_LORA, MLA_HEADS, MLA_QK)
    wq = jnp.concatenate([wq, _rotate_half_cols(wq[..., MLA_NOPE:])], axis=-1)
    wq = wq.transpose(1, 0, 2).astype(BF16)
    wkv = w_ukv[0].reshape(MLA_KV_LORA, MLA_HEADS, MLA_NOPE + MLA_V).transpose(1, 0, 2).astype(BF16)
    pad = LANES - N_GROUPS - N_EXPERTS
    w_rt = jnp.concatenate([w_grp[0], w_exp[0], jnp.zeros((D_MODEL, pad), F32)], axis=1)
    b_rt = jnp.concatenate([b_grp[0], b_exp[0], jnp.zeros((pad,), F32)])[None, :]

    ret_cos, ret_sin = _rope_tables(RET_DK)
    mla_cos, mla_sin = _rope_tables(MLA_ROPE)
    mla_cos = jnp.concatenate([mla_cos, mla_cos], axis=-1)
    mla_sin = jnp.concatenate([mla_sin, mla_sin], axis=-1)
    dec, xi, zeta, cd = _decay_tables()

    mod3 = _ada(c, w_ada[0], b_ada[0]).reshape(BATCH, 6, D_MODEL)
    ret, lat, gates = _inproj(x2, mod3, norm1, w_ret, w_lat, w_gate)
    y_ret = _retention(ret.reshape(BATCH, SEQ, 4 * RET_W), ret_cos, ret_sin, dec, xi, zeta, cd)
    attn = _mla(lat.reshape(BATCH, SEQ, MLA_LAT_W), q_norm, kv_norm, wq, wkv, mla_cos, mla_sin)
    h1, u2t, meta, counts = _merge(y_ret.reshape(N_TOK, D_MODEL), attn.reshape(N_TOK, D_MODEL), gates, x2, mod3,
                                   norm2, w_ret_o[0].astype(BF16), w_mla_o[0].astype(BF16),
                                   w_out[0].astype(BF16), w_rt, b_rt)
    seg, off_row, tile_expert, n_valid = _slot_layout(counts)
    dst = _plan(meta, off_row)
    d0, d1 = dst[:, 0], dst[:, 1]
    xs = _dispatch(d0, d1, seg, u2t)
    e_shape = (N_EXPERTS, D_MODEL, D_EXPERT)
    ys = _experts(tile_expert, n_valid, xs, w1[0].reshape(e_shape), w3[0].reshape(e_shape),
                  w2[0].reshape(N_EXPERTS, D_EXPERT, D_MODEL))
    out = _final(d0, d1, h1, meta, mod3, final_norm.reshape(1, D_MODEL), ys)
    return out.reshape(BATCH, SEQ, D_MODEL)
```

```python
import functools

import numpy as np
import jax
import jax.numpy as jnp
from jax import lax
from jax.experimental import pallas as pl
from jax.experimental.pallas import tpu as pltpu

D_MODEL = 1024
BATCH = 8
SEQ = 2048
N_TOK = BATCH * SEQ

RET_HEADS = 4
RET_DK = 256
RET_DV = 256
RET_CHUNK = 256
RET_BLK = 512
RET_W = RET_HEADS * RET_DK

MLA_HEADS = 8
MLA_NOPE = 128
MLA_ROPE = 64
MLA_V = 128
MLA_Q_LORA = 384
MLA_KV_LORA = 256
MLA_LAT_W = MLA_Q_LORA + MLA_KV_LORA + 2 * MLA_ROPE
MLA_QK = MLA_NOPE + MLA_ROPE
ROPE_THETA = 10000.0

N_GROUPS = 4
EXPERTS_PER_GROUP = 8
N_EXPERTS = N_GROUPS * EXPERTS_PER_GROUP
D_EXPERT = 256
EPS = 1e-6
LOG2_E = 1.4426950408889634

LANES = 128
SUBLANES = 8
ROW_TILES = D_MODEL // LANES
VMEM_LIMIT = 56 * 1024 * 1024

TM_PROJ = 512
TQ = 256
MLA_HPS = 2
TE = 256
TOP_K = 2
N_TILES = N_TOK * TOP_K // TE + N_EXPERTS
N_SLOTS = N_TILES * TE

F32 = jnp.float32
BF16 = jnp.bfloat16


def _sigmoid(x):
    return 1.0 / (1.0 + jnp.exp(-x))


def _rms(x):
    return x * lax.rsqrt(jnp.mean(x * x, axis=-1, keepdims=True) + EPS)


def _dot(a, b):
    return jnp.dot(a, b, preferred_element_type=F32)


def _dot_nt(a, b):
    return lax.dot_general(a, b, (((1,), (1,)), ((), ())), preferred_element_type=F32)


def _dot_tn(a, b):
    return lax.dot_general(a, b, (((0,), (0,)), ((), ())), preferred_element_type=F32)


def _params(*sem):
    return pltpu.CompilerParams(dimension_semantics=sem, vmem_limit_bytes=VMEM_LIMIT)


def _resident(shape):
    nd = len(shape)
    return pl.BlockSpec(shape, lambda *_: (0,) * nd, pipeline_mode=pl.Buffered(1))


def _ada_kernel(c_ref, w_ref, b_ref, o_ref):
    c = c_ref[...]
    act = (c * _sigmoid(c)).astype(BF16)
    o_ref[...] = _dot(act, w_ref[...].astype(BF16)) + b_ref[...]


def _ada(c, w_ada, b_ada):
    n = w_ada.shape[1]
    tn = D_MODEL
    return pl.pallas_call(
        _ada_kernel,
        grid=(n // tn,),
        in_specs=[pl.BlockSpec((BATCH, D_MODEL), lambda j: (0, 0)),
                  pl.BlockSpec((D_MODEL, tn), lambda j: (0, j)),
                  pl.BlockSpec((1, tn), lambda j: (0, j))],
        out_specs=pl.BlockSpec((BATCH, tn), lambda j: (0, j)),
        out_shape=jax.ShapeDtypeStruct((BATCH, n), F32),
        compiler_params=_params("arbitrary"),
        name="ada",
    )(c, w_ada, b_ada.reshape(1, n))


def _inproj_kernel(x_ref, mod_ref, n1_ref, wr_ref, wm_ref, wg_ref, ret_ref, lat_ref, gate_ref):
    y = _rms(x_ref[...]) * n1_ref[...]
    u = (y * (1.0 + mod_ref[1:2, :]) + mod_ref[0:1, :]).astype(BF16)
    step = 512
    for n in range(0, 4 * RET_W, step):
        ret_ref[:, n:n + step] = _dot(u, wr_ref[:, n:n + step]).astype(BF16)
    lat_ref[...] = _dot(u, wm_ref[...]).astype(BF16)
    for n in range(0, 2 * D_MODEL, step):
        gate_ref[:, n:n + step] = _dot(u, wg_ref[:, n:n + step]).astype(BF16)


def _inproj(x2, mod3, norm1, w_ret, w_lat, w_gate):
    tm = TM_PROJ
    per_b = SEQ // tm
    return pl.pallas_call(
        _inproj_kernel,
        grid=(N_TOK // tm,),
        in_specs=[pl.BlockSpec((tm, D_MODEL), lambda i: (i, 0)),
                  pl.BlockSpec((None, 6, D_MODEL), lambda i: (i // per_b, 0, 0)),
                  _resident((1, D_MODEL)),
                  _resident(w_ret.shape), _resident(w_lat.shape), _resident(w_gate.shape)],
        out_specs=[pl.BlockSpec((tm, 4 * RET_W), lambda i: (i, 0)),
                   pl.BlockSpec((tm, MLA_LAT_W), lambda i: (i, 0)),
                   pl.BlockSpec((tm, 2 * D_MODEL), lambda i: (i, 0))],
        out_shape=[jax.ShapeDtypeStruct((N_TOK, 4 * RET_W), BF16),
                   jax.ShapeDtypeStruct((N_TOK, MLA_LAT_W), BF16),
                   jax.ShapeDtypeStruct((N_TOK, 2 * D_MODEL), BF16)],
        compiler_params=_params("arbitrary"),
        name="inproj",
    )(x2, mod3, norm1, w_ret, w_lat, w_gate)


def _ret_kernel(q_ref, k_ref, v_ref, g_ref, cos_ref, sin_ref, dec_ref, xi_ref, zeta_ref, cd_ref,
                o_ref, state_ref):
    half = RET_DK // 2

    @pl.when(pl.program_id(1) == 0)
    def _():
        state_ref[...] = jnp.zeros_like(state_ref)

    def rope(x, cos, sin):
        x1, x2 = x[:, :half], x[:, half:]
        return jnp.concatenate([x1 * cos - x2 * sin, x2 * cos + x1 * sin], axis=-1)

    for c in range(RET_BLK // RET_CHUNK):
        rows = slice(c * RET_CHUNK, (c + 1) * RET_CHUNK)
        cos, sin = cos_ref[rows, :], sin_ref[rows, :]
        for h in range(RET_HEADS):
            cols = slice(h * RET_DK, (h + 1) * RET_DK)
            q = rope(q_ref[rows, cols].astype(F32), cos, sin)
            k = rope(k_ref[rows, cols].astype(F32), cos, sin) * (RET_DK ** -0.5)
            v = v_ref[rows, cols]
            scores = _dot_nt(q.astype(BF16), k.astype(BF16)) * dec_ref[h]
            state = state_ref[h]
            y = _dot(scores.astype(BF16), v) + _dot((q * xi_ref[h]).astype(BF16), state.astype(BF16))
            state_ref[h] = state * cd_ref[h] + _dot_tn((k * zeta_ref[h]).astype(BF16), v)
            mu = jnp.mean(y, axis=-1, keepdims=True)
            yc = y - mu
            yn = yc * lax.rsqrt(jnp.mean(yc * yc, axis=-1, keepdims=True) + EPS)
            g = g_ref[rows, cols].astype(F32)
            o_ref[rows, cols] = (g * _sigmoid(g) * yn).astype(BF16)


def _retention(ret3, cos, sin, dec, xi, zeta, cd):
    blk = lambda part: pl.BlockSpec((None, RET_BLK, RET_W), lambda b, j: (b, j, part))
    rope_tab = pl.BlockSpec((RET_BLK, RET_DK // 2), lambda b, j: (j, 0))
    whole = lambda a: pl.BlockSpec(a.shape, lambda b, j: (0,) * a.ndim)
    return pl.pallas_call(
        _ret_kernel,
        grid=(BATCH, SEQ // RET_BLK),
        in_specs=[blk(0), blk(1), blk(2), blk(3), rope_tab, rope_tab,
                  whole(dec), whole(xi), whole(zeta), whole(cd)],
        out_specs=pl.BlockSpec((None, RET_BLK, RET_HEADS * RET_DV), lambda b, j: (b, j, 0)),
        out_shape=jax.ShapeDtypeStruct((BATCH, SEQ, RET_HEADS * RET_DV), BF16),
        scratch_shapes=[pltpu.VMEM((RET_HEADS, RET_DK, RET_DV), F32)],
        compiler_params=_params("arbitrary", "arbitrary"),
        name="retention",
    )(ret3, ret3, ret3, ret3, cos, sin, dec, xi, zeta, cd)


def _mla_kernel(lat_ref, qn_ref, kvn_ref, wq_ref, wkv_ref, cos_ref, sin_ref, o_ref,
                cq_s, ckv_s, kpe_s, q_s, k_s, v_s):
    h = pl.program_id(1)
    o_q, o_kv, o_pe, o_rot = 0, MLA_Q_LORA, MLA_Q_LORA + MLA_KV_LORA, MLA_Q_LORA + MLA_KV_LORA + MLA_ROPE
    cos, sin = cos_ref[...], sin_ref[...]

    @pl.when(h == 0)
    def _():
        cq_s[...] = (_rms(lat_ref[:, o_q:o_kv].astype(F32)) * qn_ref[...]).astype(BF16)
        ckv_s[...] = (_rms(lat_ref[:, o_kv:o_pe].astype(F32)) * kvn_ref[...]).astype(BF16)
        pe = lat_ref[:, o_pe:o_rot].astype(F32)
        rot = lat_ref[:, o_rot:o_rot + MLA_ROPE].astype(F32)
        kpe_s[...] = (pe * cos + rot * sin).astype(BF16)

    scale = (MLA_QK ** -0.5) * LOG2_E
    for g in range(MLA_HPS):
        qf = _dot(cq_s[...], wq_ref[g])
        q_s[g, :, :MLA_NOPE] = (qf[:, :MLA_NOPE] * scale).astype(BF16)
        q_pe = qf[:, MLA_NOPE:MLA_QK] * cos + qf[:, MLA_QK:] * sin
        q_s[g, :, MLA_NOPE:] = (q_pe * scale).astype(BF16)
        kvf = _dot(ckv_s[...], wkv_ref[g])
        k_s[g, :, :MLA_NOPE] = kvf[:, :MLA_NOPE].astype(BF16)
        k_s[g, :, MLA_NOPE:] = kpe_s[...]
        v_s[g, :, :MLA_V] = kvf[:, MLA_NOPE:].astype(BF16)
        v_s[g, :, MLA_V:] = jnp.ones((SEQ, MLA_V), BF16)

    causal = lax.broadcasted_iota(jnp.int32, (TQ, TQ), 0) >= lax.broadcasted_iota(jnp.int32, (TQ, TQ), 1)
    for i in range(SEQ // TQ):
        lo, hi = i * TQ, (i + 1) * TQ
        for g in range(MLA_HPS):
            q = q_s[g, lo:hi, :]
            s_diag = jnp.where(causal, _dot_nt(q, k_s[g, lo:hi, :]), -jnp.inf)
            m = jnp.max(s_diag, axis=-1, keepdims=True)
            if i > 0:
                s_past = _dot_nt(q, k_s[g, :lo, :])
                m = jnp.maximum(m, jnp.max(s_past, axis=-1, keepdims=True))
                acc = _dot(jnp.exp2(s_past - m).astype(BF16), v_s[g, :lo, :])
                acc = acc + _dot(jnp.exp2(s_diag - m).astype(BF16), v_s[g, lo:hi, :])
            else:
                acc = _dot(jnp.exp2(s_diag - m).astype(BF16), v_s[g, lo:hi, :])
            o_ref[lo:hi, g * MLA_V:(g + 1) * MLA_V] = (acc[:, :MLA_V] / acc[:, MLA_V:]).astype(BF16)


def _mla(lat3, q_norm, kv_norm, wq, wkv, cos, sin):
    hps = MLA_HPS
    return pl.pallas_call(
        _mla_kernel,
        grid=(BATCH, MLA_HEADS // hps),
        in_specs=[pl.BlockSpec((None, SEQ, MLA_LAT_W), lambda b, h: (b, 0, 0)),
                  pl.BlockSpec((1, MLA_Q_LORA), lambda b, h: (0, 0)),
                  pl.BlockSpec((1, MLA_KV_LORA), lambda b, h: (0, 0)),
                  pl.BlockSpec((hps, MLA_Q_LORA, MLA_QK + MLA_ROPE), lambda b, h: (h, 0, 0)),
                  pl.BlockSpec((hps, MLA_KV_LORA, MLA_NOPE + MLA_V), lambda b, h: (h, 0, 0)),
                  pl.BlockSpec((SEQ, MLA_ROPE), lambda b, h: (0, 0)),
                  pl.BlockSpec((SEQ, MLA_ROPE), lambda b, h: (0, 0))],
        out_specs=pl.BlockSpec((None, SEQ, hps * MLA_V), lambda b, h: (b, 0, h)),
        out_shape=jax.ShapeDtypeStruct((BATCH, SEQ, MLA_HEADS * MLA_V), BF16),
        scratch_shapes=[pltpu.VMEM((SEQ, MLA_Q_LORA), BF16),
                        pltpu.VMEM((SEQ, MLA_KV_LORA), BF16),
                        pltpu.VMEM((SEQ, MLA_ROPE), BF16),
                        pltpu.VMEM((hps, SEQ, MLA_QK), BF16),
                        pltpu.VMEM((hps, SEQ, MLA_QK), BF16),
                        pltpu.VMEM((hps, SEQ, 2 * MLA_V), BF16)],
        compiler_params=_params("arbitrary", "arbitrary"),
        name="mla",
    )(lat3, q_norm, kv_norm, wq, wkv, cos, sin)


def _route(logits):
    lane = lax.broadcasted_iota(jnp.int32, logits.shape, 1)
    neg = -jnp.inf
    gl = jnp.where(lane < N_GROUPS, logits, neg)
    gmax = jnp.max(gl, axis=-1, keepdims=True)
    gsel = jnp.min(jnp.where(gl == gmax, lane, LANES), axis=-1, keepdims=True)
    p_grp = 1.0 / jnp.sum(jnp.exp(gl - gmax), axis=-1, keepdims=True)
    e_lane = lane - N_GROUPS
    in_grp = (e_lane >= 0) & (e_lane < N_EXPERTS) & ((e_lane >> 3) == gsel)
    el = jnp.where(in_grp, logits, neg)
    v0 = jnp.max(el, axis=-1, keepdims=True)
    i0 = jnp.min(jnp.where(el == v0, lane, LANES), axis=-1, keepdims=True)
    el1 = jnp.where(lane == i0, neg, el)
    v1 = jnp.max(el1, axis=-1, keepdims=True)
    i1 = jnp.min(jnp.where(el1 == v1, lane, LANES), axis=-1, keepdims=True)
    t = jnp.exp(v1 - v0)
    w0 = p_grp / (1.0 + t)
    w1 = p_grp * t / (1.0 + t)
    return i0, i1, w0, w1


def _lane_pack(shape, cols):
    lane = lax.broadcasted_iota(jnp.int32, shape, 1)
    out = jnp.zeros(shape, F32)
    for k, col in enumerate(cols):
        out = jnp.where(lane == k, col, out)
    return out


def _to_token_tiles(ref, val):
    n = val.shape[0]
    for s in range(ROW_TILES):
        ref[pl.ds(s, n, stride=ROW_TILES), :] = val[:, s * LANES:(s + 1) * LANES]


def _from_token_tiles(ref):
    n = ref.shape[0] // ROW_TILES
    return jnp.concatenate([ref[pl.ds(s, n, stride=ROW_TILES), :] for s in range(ROW_TILES)], axis=-1)


def _token_rows(ref, t):
    return ref.at[pl.ds(pl.multiple_of(t * ROW_TILES, ROW_TILES), ROW_TILES)]


def _merge_kernel(yr_ref, at_ref, gr_ref, gm_ref, x_ref, mod_ref, n2_ref, wro_ref, wmo_ref, wo_ref,
                  wrt_ref, brt_ref, h1_ref, u2_ref, meta_ref, cnt_ref, carry_ref):
    tm = x_ref.shape[0]

    @pl.when(pl.program_id(0) == 0)
    def _():
        carry_ref[...] = jnp.zeros_like(carry_ref)

    y_ret = _dot(yr_ref[...], wro_ref[...])
    y_mla = _dot(at_ref[...], wmo_ref[...])
    merged = _sigmoid(gr_ref[...].astype(F32)) * y_ret + _sigmoid(gm_ref[...].astype(F32)) * y_mla
    o = _dot(merged.astype(BF16), wo_ref[...])
    h1 = x_ref[...] + mod_ref[2:3, :] * o
    h1_ref[...] = h1
    u2 = _rms(h1) * n2_ref[...] * (1.0 + mod_ref[4:5, :]) + mod_ref[3:4, :]
    _to_token_tiles(u2_ref, u2)
    u_hi = u2.astype(BF16)
    u_lo = (u2 - u_hi.astype(F32)).astype(BF16)
    w = wrt_ref[...]
    w_hi = w.astype(BF16)
    w_lo = (w - w_hi.astype(F32)).astype(BF16)
    logits = _dot(u_hi, w_hi) + _dot(u_lo, w_hi) + _dot(u_hi, w_lo) + brt_ref[...]
    i0, i1, w0, w1 = _route(logits)
    lane = lax.broadcasted_iota(jnp.int32, (tm, LANES), 1)
    m0, m1 = lane == i0, lane == i1
    member = jnp.where(m0 | m1, 1.0, 0.0)
    tri = jnp.where(lax.broadcasted_iota(jnp.int32, (tm, tm), 0) > lax.broadcasted_iota(jnp.int32, (tm, tm), 1),
                    1.0, 0.0).astype(BF16)
    prefix = _dot(tri, member.astype(BF16)) + carry_ref[0:1, :]
    rank0 = jnp.sum(jnp.where(m0, prefix, 0.0), axis=-1, keepdims=True)
    rank1 = jnp.sum(jnp.where(m1, prefix, 0.0), axis=-1, keepdims=True)
    carry_ref[...] = carry_ref[...] + jnp.sum(member, axis=0, keepdims=True)
    cnt_ref[...] = carry_ref[...]
    meta_ref[...] = _lane_pack((tm, LANES), [i0.astype(F32), i1.astype(F32), rank0, rank1, w0, w1])


def _merge(y_ret, attn, gates, x2, mod3, norm2, w_ret_o, w_mla_o, w_out, w_rt, b_rt):
    tm = TM_PROJ
    per_b = SEQ // tm
    row = lambda j: pl.BlockSpec((tm, D_MODEL), lambda i: (i, j))
    return pl.pallas_call(
        _merge_kernel,
        grid=(N_TOK // tm,),
        in_specs=[row(0), row(0), row(0), row(1), row(0),
                  pl.BlockSpec((None, 6, D_MODEL), lambda i: (i // per_b, 0, 0)),
                  _resident((1, D_MODEL)),
                  _resident(w_ret_o.shape), _resident(w_mla_o.shape), _resident(w_out.shape),
                  _resident(w_rt.shape), _resident(b_rt.shape)],
        out_specs=[row(0),
                   pl.BlockSpec((tm * ROW_TILES, LANES), lambda i: (i, 0)),
                   pl.BlockSpec((tm, LANES), lambda i: (i, 0)),
                   pl.BlockSpec((SUBLANES, LANES), lambda i: (0, 0))],
        out_shape=[jax.ShapeDtypeStruct((N_TOK, D_MODEL), F32),
                   jax.ShapeDtypeStruct((N_TOK * ROW_TILES, LANES), F32),
                   jax.ShapeDtypeStruct((N_TOK, LANES), F32),
                   jax.ShapeDtypeStruct((SUBLANES, LANES), F32)],
        scratch_shapes=[pltpu.VMEM((SUBLANES, LANES), F32)],
        compiler_params=_params("arbitrary"),
        name="merge",
    )(y_ret, attn, gates, gates, x2, mod3, norm2, w_ret_o, w_mla_o, w_out, w_rt, b_rt)


def _plan_kernel(meta_ref, off_ref, dst_ref):
    m = meta_ref[...]
    lane = lax.broadcasted_iota(jnp.int32, m.shape, 1)
    off = off_ref[...]
    i0 = m[:, 0:1].astype(jnp.int32)
    i1 = m[:, 1:2].astype(jnp.int32)
    d0 = jnp.sum(jnp.where(lane == i0, off, 0.0), axis=-1, keepdims=True) + m[:, 2:3]
    d1 = jnp.sum(jnp.where(lane == i1, off, 0.0), axis=-1, keepdims=True) + m[:, 3:4]
    dst_ref[...] = _lane_pack(m.shape, [d0, d1]).astype(jnp.int32)


def _plan(meta, off_row):
    tm = 2048
    return pl.pallas_call(
        _plan_kernel,
        grid=(N_TOK // tm,),
        in_specs=[pl.BlockSpec((tm, LANES), lambda i: (i, 0)),
                  pl.BlockSpec((1, LANES), lambda i: (0, 0))],
        out_specs=pl.BlockSpec((tm, LANES), lambda i: (i, 0)),
        out_shape=jax.ShapeDtypeStruct((N_TOK, LANES), jnp.int32),
        compiler_params=_params("arbitrary"),
        name="plan",
    )(meta, off_row)


def _row_copy_wait(src_like, dst_like, sem):
    pltpu.make_async_copy(src_like, dst_like, sem).wait()


def _dispatch_kernel(d0_ref, d1_ref, seg_ref, cnt_ref, u_ref, xs_ref, inv_ref, zero_ref, sem, zsem):
    i = pl.program_id(0)
    tm = u_ref.shape[0] // ROW_TILES
    tile_rows = TE * ROW_TILES

    def slot_tile(j):
        return xs_ref.at[pl.ds(pl.multiple_of(j * tile_rows, tile_rows), tile_rows)]

    @pl.when(i == 0)
    def _():
        def clear(s, carry):
            inv_ref[s] = 0
            return carry

        for e in range(N_EXPERTS):
            lax.fori_loop(seg_ref[e] + cnt_ref[e], seg_ref[e + 1], clear, 0)
        lax.fori_loop(seg_ref[N_EXPERTS], N_SLOTS, clear, 0)

        zero_ref[...] = jnp.zeros_like(zero_ref)

        def tail(e):
            end = seg_ref[e + 1]
            return end > seg_ref[e], slot_tile(end // TE - 1)

        for e in range(N_EXPERTS):
            nonempty, dst = tail(e)

            @pl.when(nonempty)
            def _():
                pltpu.make_async_copy(zero_ref, dst, zsem).start()

        for e in range(N_EXPERTS):
            nonempty, dst = tail(e)

            @pl.when(nonempty)
            def _():
                pltpu.make_async_copy(zero_ref, dst, zsem).wait()

        def unused(j):
            return pltpu.make_async_copy(zero_ref, slot_tile(j), zsem)

        first_unused = seg_ref[N_EXPERTS] // TE
        lax.fori_loop(first_unused, N_TILES, lambda j, c: (unused(j).start(), c)[1], 0)
        lax.fori_loop(first_unused, N_TILES, lambda j, c: (unused(j).wait(), c)[1], 0)

    base = i * tm

    def body(r, carry):
        src = _token_rows(u_ref, r)
        d0, d1 = d0_ref[base + r], d1_ref[base + r]
        pltpu.make_async_copy(src, _token_rows(xs_ref, d0), sem).start(priority=0)
        pltpu.make_async_copy(src, _token_rows(xs_ref, d1), sem).start(priority=1)
        inv_ref[d0] = base + r
        inv_ref[d1] = N_TOK + base + r
        return carry

    lax.fori_loop(0, tm, body, 0, unroll=8)
    _row_copy_wait(u_ref, xs_ref.at[pl.ds(0, tm * ROW_TILES)], sem)
    _row_copy_wait(u_ref, xs_ref.at[pl.ds(0, tm * ROW_TILES)], sem)


def _dispatch(d0, d1, seg, cnt, u2t):
    tm = TM_PROJ
    return pl.pallas_call(
        _dispatch_kernel,
        grid_spec=pltpu.PrefetchScalarGridSpec(
            num_scalar_prefetch=4,
            grid=(N_TOK // tm,),
            in_specs=[pl.BlockSpec((tm * ROW_TILES, LANES), lambda i, *_: (i, 0))],
            out_specs=[pl.BlockSpec(memory_space=pl.ANY), pl.BlockSpec(memory_space=pltpu.SMEM)],
            scratch_shapes=[pltpu.VMEM((TE * ROW_TILES, LANES), F32),
                            pltpu.SemaphoreType.DMA(()), pltpu.SemaphoreType.DMA(())]),
        out_shape=[jax.ShapeDtypeStruct((N_SLOTS * ROW_TILES, LANES), F32),
                   jax.ShapeDtypeStruct((N_SLOTS,), jnp.int32)],
        compiler_params=_params("arbitrary"),
        name="dispatch",
    )(d0, d1, seg, cnt, u2t)


def _expert_kernel(te_ref, nv_ref, fill_ref, inv_ref, x_ref, w1_ref, w3_ref, w2_ref, yk_ref,
                   w1_s, w3_s, w2_s, ybuf, sem):
    j = pl.program_id(0)
    nv = nv_ref[0]

    def drain(t):
        n = fill_ref[t] * ROW_TILES
        buf = ybuf.at[t % 2]
        pltpu.make_async_copy(buf.at[pl.ds(0, n)], yk_ref.at[pl.ds(0, n)], sem.at[t % 2]).wait()

    @pl.when(j < nv)
    def _():
        @pl.when((j == 0) | (te_ref[j] != te_ref[jnp.maximum(j - 1, 0)]))
        def _():
            w1_s[...] = w1_ref[...].astype(BF16)
            w3_s[...] = w3_ref[...].astype(BF16)
            w2_s[...] = w2_ref[...].astype(BF16)

        x = _from_token_tiles(x_ref).astype(BF16)
        a = _dot(x, w1_s[...])
        b = _dot(x, w3_s[...])
        hid = (a * _sigmoid(a) * b).astype(BF16)
        buf = ybuf.at[j % 2]
        _to_token_tiles(buf, _dot(hid, w2_s[...]))

        base = j * TE
        n = fill_ref[j]

        def row(r, priority):
            dst = _token_rows(yk_ref, inv_ref[base + r])
            pltpu.make_async_copy(_token_rows(buf, r), dst, sem.at[j % 2]).start(priority=priority)

        def pair(p, carry):
            row(2 * p, 0)
            row(2 * p + 1, 1)
            return carry

        lax.fori_loop(0, n // 2, pair, 0)

        @pl.when(n % 2 == 1)
        def _():
            row(n - 1, 0)

    @pl.when((j >= 1) & (j - 1 < nv))
    def _():
        drain(j - 1)

    @pl.when((j == N_TILES - 1) & (j < nv))
    def _():
        drain(j)


def _experts(tile_expert, n_valid, tile_fill, inv, xs, w1, w3, w2):
    tile = lambda j, te, nv: jnp.minimum(j, nv[0] - 1)
    wspec = lambda shape: pl.BlockSpec((None,) + shape, lambda j, te, nv, *_: (te[tile(j, te, nv)], 0, 0))
    return pl.pallas_call(
        _expert_kernel,
        grid_spec=pltpu.PrefetchScalarGridSpec(
            num_scalar_prefetch=4,
            grid=(N_TILES,),
            in_specs=[pl.BlockSpec((TE * ROW_TILES, LANES), lambda j, te, nv, *_: (tile(j, te, nv), 0)),
                      wspec((D_MODEL, D_EXPERT)), wspec((D_MODEL, D_EXPERT)), wspec((D_EXPERT, D_MODEL))],
            out_specs=pl.BlockSpec(memory_space=pl.ANY),
            scratch_shapes=[pltpu.VMEM((D_MODEL, D_EXPERT), BF16), pltpu.VMEM((D_MODEL, D_EXPERT), BF16),
                            pltpu.VMEM((D_EXPERT, D_MODEL), BF16),
                            pltpu.VMEM((2, TE * ROW_TILES, LANES), F32), pltpu.SemaphoreType.DMA((2,))]),
        out_shape=jax.ShapeDtypeStruct((TOP_K * N_TOK * ROW_TILES, LANES), F32),
        compiler_params=_params("arbitrary"),
        name="experts",
    )(tile_expert, n_valid, tile_fill, inv, xs, w1, w3, w2)


def _final_kernel(h1_ref, y0_ref, y1_ref, meta_ref, mod_ref, fn_ref, o_ref):
    m = meta_ref[...]
    moe = m[:, 4:5] * _from_token_tiles(y0_ref) + m[:, 5:6] * _from_token_tiles(y1_ref)
    h2 = h1_ref[...] + mod_ref[5:6, :] * moe
    o_ref[...] = _rms(h2) * fn_ref[...]


def _final(h1, yk, meta, mod3, final_norm):
    tm = TM_PROJ
    per_b = SEQ // tm
    n_blk = N_TOK // tm
    return pl.pallas_call(
        _final_kernel,
        grid=(n_blk,),
        in_specs=[pl.BlockSpec((tm, D_MODEL), lambda i: (i, 0)),
                  pl.BlockSpec((tm * ROW_TILES, LANES), lambda i: (i, 0)),
                  pl.BlockSpec((tm * ROW_TILES, LANES), lambda i: (n_blk + i, 0)),
                  pl.BlockSpec((tm, LANES), lambda i: (i, 0)),
                  pl.BlockSpec((None, 6, D_MODEL), lambda i: (i // per_b, 0, 0)),
                  pl.BlockSpec((1, D_MODEL), lambda i: (0, 0))],
        out_specs=pl.BlockSpec((tm, D_MODEL), lambda i: (i, 0)),
        out_shape=jax.ShapeDtypeStruct((N_TOK, D_MODEL), F32),
        compiler_params=_params("arbitrary"),
        name="final",
    )(h1, yk, yk, meta, mod3, final_norm)


def _slot_layout(counts):
    cnt = counts[0, N_GROUPS:N_GROUPS + N_EXPERTS].astype(jnp.int32)
    tile_end = jnp.cumsum((cnt + TE - 1) // TE)
    seg = jnp.concatenate([jnp.zeros((1,), jnp.int32), tile_end * TE])
    off_row = jnp.zeros((1, LANES), F32).at[0, N_GROUPS:N_GROUPS + N_EXPERTS].set(seg[:-1].astype(F32))
    tile_ids = jnp.arange(N_TILES, dtype=jnp.int32)
    tile_expert = jnp.sum((tile_end[None, :] <= tile_ids[:, None]).astype(jnp.int32), axis=1)
    tile_expert = jnp.minimum(tile_expert, N_EXPERTS - 1)
    real_end = (seg[:-1] + cnt)[tile_expert]
    tile_fill = jnp.where(tile_ids < tile_end[-1], jnp.clip(real_end - tile_ids * TE, 0, TE), 0)
    return cnt, seg, off_row, tile_expert, tile_end[-1:], tile_fill


def _rope_tables(dim):
    pos = jnp.arange(SEQ, dtype=F32)
    inv = ROPE_THETA ** (-jnp.arange(0, dim, 2, dtype=F32) / dim)
    ang = pos[:, None] * inv[None, :]
    return jnp.cos(ang), jnp.sin(ang)


def _decay_tables():
    c = RET_CHUNK
    log_gamma = jnp.log1p(-jnp.exp2(-5.0 - jnp.arange(RET_HEADS, dtype=F32)))
    idx = jnp.arange(c, dtype=F32)
    rel = idx[:, None] - idx[None, :]
    dec = jnp.where(rel[None] >= 0, jnp.exp(log_gamma[:, None, None] * jnp.maximum(rel, 0.0)[None]), 0.0)
    xi = jnp.exp(log_gamma[:, None] * (idx[None, :] + 1.0))[:, :, None]
    zeta = jnp.exp(log_gamma[:, None] * (c - 1.0 - idx[None, :]))[:, :, None]
    cd = jnp.exp(log_gamma * c)[:, None, None]
    return dec, xi, zeta, cd


def _rotate_half_cols(w):
    half = w.shape[-1] // 2
    return jnp.concatenate([-w[..., half:], w[..., :half]], axis=-1)


def kernel(x, c, w_ada, b_ada, norm1, norm2, w_in, w_ret_o, q_norm, kv_norm, w_uq, w_ukv, w_mla_o, w_out,
           w_grp, b_grp, w_exp, b_exp, w1, w3, w2, final_norm):
    assert x.shape == (BATCH, SEQ, D_MODEL) and w_ada.shape[0] == 1
    x2 = x.reshape(N_TOK, D_MODEL)

    o_lat = 4 * RET_W
    o_pe = o_lat + MLA_Q_LORA + MLA_KV_LORA
    o_gate = o_pe + MLA_ROPE
    wi = w_in[0]
    w_ret = wi[:, :o_lat].astype(BF16)
    w_lat = jnp.concatenate([wi[:, o_lat:o_gate], _rotate_half_cols(wi[:, o_pe:o_gate])], axis=1).astype(BF16)
    w_gate = wi[:, o_gate:].astype(BF16)
    wq = w_uq[0].reshape(MLA_Q_LORA, MLA_HEADS, MLA_QK)
    wq = jnp.concatenate([wq, _rotate_half_cols(wq[..., MLA_NOPE:])], axis=-1)
    wq = wq.transpose(1, 0, 2).astype(BF16)
    wkv = w_ukv[0].reshape(MLA_KV_LORA, MLA_HEADS, MLA_NOPE + MLA_V).transpose(1, 0, 2).astype(BF16)
    pad = LANES - N_GROUPS - N_EXPERTS
    w_rt = jnp.concatenate([w_grp[0], w_exp[0], jnp.zeros((D_MODEL, pad), F32)], axis=1)
    b_rt = jnp.concatenate([b_grp[0], b_exp[0], jnp.zeros((pad,), F32)])[None, :]

    ret_cos, ret_sin = _rope_tables(RET_DK)
    mla_cos, mla_sin = _rope_tables(MLA_ROPE)
    mla_cos = jnp.concatenate([mla_cos, mla_cos], axis=-1)
    mla_sin = jnp.concatenate([mla_sin, mla_sin], axis=-1)
    dec, xi, zeta, cd = _decay_tables()

    mod3 = _ada(c, w_ada[0], b_ada[0]).reshape(BATCH, 6, D_MODEL)
    ret, lat, gates = _inproj(x2, mod3, norm1, w_ret, w_lat, w_gate)
    y_ret = _retention(ret.reshape(BATCH, SEQ, 4 * RET_W), ret_cos, ret_sin, dec, xi, zeta, cd)
    attn = _mla(lat.reshape(BATCH, SEQ, MLA_LAT_W), q_norm, kv_norm, wq, wkv, mla_cos, mla_sin)
    h1, u2t, meta, counts = _merge(y_ret.reshape(N_TOK, D_MODEL), attn.reshape(N_TOK, D_MODEL), gates, x2, mod3,
                                   norm2, w_ret_o[0].astype(BF16), w_mla_o[0].astype(BF16),
                                   w_out[0].astype(BF16), w_rt, b_rt)
    cnt, seg, off_row, tile_expert, n_valid, tile_fill = _slot_layout(counts)
    dst = _plan(meta, off_row)
    xs, inv = _dispatch(dst[:, 0], dst[:, 1], seg, cnt, u2t)
    e_shape = (N_EXPERTS, D_MODEL, D_EXPERT)
    yk = _experts(tile_expert, n_valid, tile_fill, inv, xs, w1[0].reshape(e_shape), w3[0].reshape(e_shape),
                  w2[0].reshape(N_EXPERTS, D_EXPERT, D_MODEL))
    out = _final(h1, yk, meta, mod3, final_norm.reshape(1, D_MODEL))
    return out.reshape(BATCH, SEQ, D_MODEL)
```

```python
import jax
import jax.numpy as jnp
from jax import lax
from jax.experimental import pallas as pl
from jax.experimental.pallas import tpu as pltpu

D_MODEL = 1024
BATCH = 8
SEQ = 2048
N_TOK = BATCH * SEQ

RET_HEADS = 4
RET_DK = 256
RET_DV = 256
RET_CHUNK = 256
RET_BLK = 512
RET_W = RET_HEADS * RET_DK

MLA_HEADS = 8
MLA_NOPE = 128
MLA_ROPE = 64
MLA_V = 128
MLA_Q_LORA = 384
MLA_KV_LORA = 256
MLA_LAT_W = MLA_Q_LORA + MLA_KV_LORA + 2 * MLA_ROPE
MLA_QK = MLA_NOPE + MLA_ROPE
ROPE_THETA = 10000.0

N_GROUPS = 4
EXPERTS_PER_GROUP = 8
N_EXPERTS = N_GROUPS * EXPERTS_PER_GROUP
D_EXPERT = 256
EPS = 1e-6
LOG2_E = 1.4426950408889634

LANES = 128
SUBLANES = 8
ROW_TILES = D_MODEL // LANES
VMEM_LIMIT = 56 * 1024 * 1024

TM_PROJ = 512
TQ = 256
MLA_HPS = 2
TE = 256
TOP_K = 2
N_TILES = N_TOK * TOP_K // TE + N_EXPERTS
N_SLOTS = N_TILES * TE

F32 = jnp.float32
BF16 = jnp.bfloat16


def _sigmoid(x):
    return 1.0 / (1.0 + jnp.exp(-x))


def _rms(x):
    return x * lax.rsqrt(jnp.mean(x * x, axis=-1, keepdims=True) + EPS)


def _dot(a, b):
    return jnp.dot(a, b, preferred_element_type=F32)


def _dot_nt(a, b):
    return lax.dot_general(a, b, (((1,), (1,)), ((), ())), preferred_element_type=F32)


def _dot_tn(a, b):
    return lax.dot_general(a, b, (((0,), (0,)), ((), ())), preferred_element_type=F32)


def _params(*sem):
    return pltpu.CompilerParams(dimension_semantics=sem, vmem_limit_bytes=VMEM_LIMIT)


def _resident(shape):
    nd = len(shape)
    return pl.BlockSpec(shape, lambda *_: (0,) * nd, pipeline_mode=pl.Buffered(1))


def _ada_kernel(c_ref, w_ref, b_ref, o_ref):
    c = c_ref[...]
    act = (c * _sigmoid(c)).astype(BF16)
    o_ref[...] = _dot(act, w_ref[...].astype(BF16)) + b_ref[...]


def _ada(c, w_ada, b_ada):
    n = w_ada.shape[1]
    tn = D_MODEL
    return pl.pallas_call(
        _ada_kernel,
        grid=(n // tn,),
        in_specs=[pl.BlockSpec((BATCH, D_MODEL), lambda j: (0, 0)),
                  pl.BlockSpec((D_MODEL, tn), lambda j: (0, j)),
                  pl.BlockSpec((1, tn), lambda j: (0, j))],
        out_specs=pl.BlockSpec((BATCH, tn), lambda j: (0, j)),
        out_shape=jax.ShapeDtypeStruct((BATCH, n), F32),
        compiler_params=_params("arbitrary"),
        name="ada",
    )(c, w_ada, b_ada.reshape(1, n))


def _inproj_kernel(x_ref, mod_ref, n1_ref, cos_ref, sin_ref, wr_ref, wm_ref, wg_ref, ret_ref, lat_ref, gate_ref):
    y = _rms(x_ref[...]) * n1_ref[...]
    u = (y * (1.0 + mod_ref[1:2, :]) + mod_ref[0:1, :]).astype(BF16)
    cos, sin = cos_ref[...], sin_ref[...]
    half = RET_DK // 2
    for n in range(0, 2 * RET_W, RET_DK):
        p = _dot(u, wr_ref[:, n:n + RET_DK])
        x1, x2 = p[:, :half], p[:, half:]
        ret_ref[:, n:n + half] = (x1 * cos - x2 * sin).astype(BF16)
        ret_ref[:, n + half:n + RET_DK] = (x2 * cos + x1 * sin).astype(BF16)
    step = 512
    for n in range(2 * RET_W, 3 * RET_W, step):
        ret_ref[:, n:n + step] = _dot(u, wr_ref[:, n:n + step]).astype(BF16)
    for n in range(3 * RET_W, 4 * RET_W, step):
        p = _dot(u, wr_ref[:, n:n + step])
        ret_ref[:, n:n + step] = (p * _sigmoid(p)).astype(BF16)
    lat_ref[...] = _dot(u, wm_ref[...]).astype(BF16)
    for n in range(0, 2 * D_MODEL, step):
        gate_ref[:, n:n + step] = _sigmoid(_dot(u, wg_ref[:, n:n + step])).astype(BF16)


def _inproj(x2, mod3, norm1, cos, sin, w_ret, w_lat, w_gate):
    tm = TM_PROJ
    per_b = SEQ // tm
    rope_tab = pl.BlockSpec((tm, RET_DK // 2), lambda i: (i % per_b, 0))
    return pl.pallas_call(
        _inproj_kernel,
        grid=(N_TOK // tm,),
        in_specs=[pl.BlockSpec((tm, D_MODEL), lambda i: (i, 0)),
                  pl.BlockSpec((None, 6, D_MODEL), lambda i: (i // per_b, 0, 0)),
                  _resident((1, D_MODEL)), rope_tab, rope_tab,
                  _resident(w_ret.shape), _resident(w_lat.shape), _resident(w_gate.shape)],
        out_specs=[pl.BlockSpec((tm, 4 * RET_W), lambda i: (i, 0)),
                   pl.BlockSpec((tm, MLA_LAT_W), lambda i: (i, 0)),
                   pl.BlockSpec((tm, 2 * D_MODEL), lambda i: (i, 0))],
        out_shape=[jax.ShapeDtypeStruct((N_TOK, 4 * RET_W), BF16),
                   jax.ShapeDtypeStruct((N_TOK, MLA_LAT_W), BF16),
                   jax.ShapeDtypeStruct((N_TOK, 2 * D_MODEL), BF16)],
        compiler_params=_params("arbitrary"),
        name="inproj",
    )(x2, mod3, norm1, cos, sin, w_ret, w_lat, w_gate)


def _ret_kernel(q_ref, k_ref, v_ref, g_ref, dec_ref, xi_ref, zeta_ref, cd_ref, o_ref, state_ref):
    @pl.when(pl.program_id(1) == 0)
    def _():
        state_ref[...] = jnp.zeros_like(state_ref)

    heads = range(RET_HEADS)
    col = lambda h: slice(h * RET_DK, (h + 1) * RET_DK)
    for c in range(RET_BLK // RET_CHUNK):
        rows = slice(c * RET_CHUNK, (c + 1) * RET_CHUNK)
        scores = [(_dot_nt(q_ref[rows, col(h)], k_ref[rows, col(h)]) * dec_ref[h]).astype(BF16) for h in heads]
        carried = [xi_ref[h] * _dot(q_ref[rows, col(h)], state_ref[h].astype(BF16)) for h in heads]
        y = [_dot(scores[h], v_ref[rows, col(h)]) + carried[h] for h in heads]
        for h in heads:
            k_dec = (k_ref[rows, col(h)].astype(F32) * zeta_ref[h]).astype(BF16)
            state_ref[h] = state_ref[h] * cd_ref[h] + _dot_tn(k_dec, v_ref[rows, col(h)])
        yc = [y[h] - jnp.mean(y[h], axis=-1, keepdims=True) for h in heads]
        inv = [lax.rsqrt(jnp.mean(yc[h] * yc[h], axis=-1, keepdims=True) + EPS) for h in heads]
        for h in heads:
            o_ref[rows, col(h)] = (g_ref[rows, col(h)].astype(F32) * (yc[h] * inv[h])).astype(BF16)


def _retention(ret3, dec, xi, zeta, cd):
    blk = lambda part: pl.BlockSpec((None, RET_BLK, RET_W), lambda b, j: (b, j, part))
    whole = lambda a: pl.BlockSpec(a.shape, lambda b, j: (0,) * a.ndim)
    return pl.pallas_call(
        _ret_kernel,
        grid=(BATCH, SEQ // RET_BLK),
        in_specs=[blk(0), blk(1), blk(2), blk(3), whole(dec), whole(xi), whole(zeta), whole(cd)],
        out_specs=pl.BlockSpec((None, RET_BLK, RET_HEADS * RET_DV), lambda b, j: (b, j, 0)),
        out_shape=jax.ShapeDtypeStruct((BATCH, SEQ, RET_HEADS * RET_DV), BF16),
        scratch_shapes=[pltpu.VMEM((RET_HEADS, RET_DK, RET_DV), F32)],
        compiler_params=_params("arbitrary", "arbitrary"),
        name="retention",
    )(ret3, ret3, ret3, ret3, dec, xi, zeta, cd)


def _mla_kernel(lat_ref, qn_ref, kvn_ref, wq_ref, wkv_ref, cos_ref, sin_ref, o_ref,
                cq_s, ckv_s, kpe_s, q_s, k_s, v_s):
    h = pl.program_id(1)
    o_q, o_kv, o_pe, o_rot = 0, MLA_Q_LORA, MLA_Q_LORA + MLA_KV_LORA, MLA_Q_LORA + MLA_KV_LORA + MLA_ROPE
    cos, sin = cos_ref[...], sin_ref[...]

    @pl.when(h == 0)
    def _():
        cq_s[...] = (_rms(lat_ref[:, o_q:o_kv].astype(F32)) * qn_ref[...]).astype(BF16)
        ckv_s[...] = (_rms(lat_ref[:, o_kv:o_pe].astype(F32)) * kvn_ref[...]).astype(BF16)
        pe = lat_ref[:, o_pe:o_rot].astype(F32)
        rot = lat_ref[:, o_rot:o_rot + MLA_ROPE].astype(F32)
        kpe_s[...] = (pe * cos + rot * sin).astype(BF16)

    scale = (MLA_QK ** -0.5) * LOG2_E
    for g in range(MLA_HPS):
        qf = _dot(cq_s[...], wq_ref[g])
        q_s[g, :, :MLA_NOPE] = (qf[:, :MLA_NOPE] * scale).astype(BF16)
        q_pe = qf[:, MLA_NOPE:MLA_QK] * cos + qf[:, MLA_QK:] * sin
        q_s[g, :, MLA_NOPE:] = (q_pe * scale).astype(BF16)
        kvf = _dot(ckv_s[...], wkv_ref[g])
        k_s[g, :, :MLA_NOPE] = kvf[:, :MLA_NOPE].astype(BF16)
        k_s[g, :, MLA_NOPE:] = kpe_s[...]
        v_s[g, :, :MLA_V] = kvf[:, MLA_NOPE:].astype(BF16)
        v_s[g, :, MLA_V:] = jnp.ones((SEQ, MLA_V), BF16)

    causal = lax.broadcasted_iota(jnp.int32, (TQ, TQ), 0) >= lax.broadcasted_iota(jnp.int32, (TQ, TQ), 1)
    heads = range(MLA_HPS)
    n_blk = SEQ // TQ

    def scores(i):
        lo, hi = i * TQ, (i + 1) * TQ
        diag = [jnp.where(causal, _dot_nt(q_s[g, lo:hi, :], k_s[g, lo:hi, :]), -jnp.inf) for g in heads]
        past = [_dot_nt(q_s[g, lo:hi, :], k_s[g, :lo, :]) if i > 0 else None for g in heads]
        return diag, past

    pending = scores(0)
    for i in range(n_blk):
        lo, hi = i * TQ, (i + 1) * TQ
        diag, past = pending
        if i + 1 < n_blk:
            pending = scores(i + 1)
        m = [jnp.max(diag[g], axis=-1, keepdims=True) for g in heads]
        if i > 0:
            m = [jnp.maximum(m[g], jnp.max(past[g], axis=-1, keepdims=True)) for g in heads]
        acc = [_dot(jnp.exp2(diag[g] - m[g]).astype(BF16), v_s[g, lo:hi, :]) for g in heads]
        if i > 0:
            acc = [acc[g] + _dot(jnp.exp2(past[g] - m[g]).astype(BF16), v_s[g, :lo, :]) for g in heads]
        for g in heads:
            o_ref[lo:hi, g * MLA_V:(g + 1) * MLA_V] = (acc[g][:, :MLA_V] / acc[g][:, MLA_V:]).astype(BF16)


def _mla(lat3, q_norm, kv_norm, wq, wkv, cos, sin):
    hps = MLA_HPS
    return pl.pallas_call(
        _mla_kernel,
        grid=(BATCH, MLA_HEADS // hps),
        in_specs=[pl.BlockSpec((None, SEQ, MLA_LAT_W), lambda b, h: (b, 0, 0)),
                  pl.BlockSpec((1, MLA_Q_LORA), lambda b, h: (0, 0)),
                  pl.BlockSpec((1, MLA_KV_LORA), lambda b, h: (0, 0)),
                  pl.BlockSpec((hps, MLA_Q_LORA, MLA_QK + MLA_ROPE), lambda b, h: (h, 0, 0)),
                  pl.BlockSpec((hps, MLA_KV_LORA, MLA_NOPE + MLA_V), lambda b, h: (h, 0, 0)),
                  pl.BlockSpec((SEQ, MLA_ROPE), lambda b, h: (0, 0)),
                  pl.BlockSpec((SEQ, MLA_ROPE), lambda b, h: (0, 0))],
        out_specs=pl.BlockSpec((None, SEQ, hps * MLA_V), lambda b, h: (b, 0, h)),
        out_shape=jax.ShapeDtypeStruct((BATCH, SEQ, MLA_HEADS * MLA_V), BF16),
        scratch_shapes=[pltpu.VMEM((SEQ, MLA_Q_LORA), BF16),
                        pltpu.VMEM((SEQ, MLA_KV_LORA), BF16),
                        pltpu.VMEM((SEQ, MLA_ROPE), BF16),
                        pltpu.VMEM((hps, SEQ, MLA_QK), BF16),
                        pltpu.VMEM((hps, SEQ, MLA_QK), BF16),
                        pltpu.VMEM((hps, SEQ, 2 * MLA_V), BF16)],
        compiler_params=_params("arbitrary", "arbitrary"),
        name="mla",
    )(lat3, q_norm, kv_norm, wq, wkv, cos, sin)


def _route(logits):
    lane = lax.broadcasted_iota(jnp.int32, logits.shape, 1)
    neg = -jnp.inf
    gl = jnp.where(lane < N_GROUPS, logits, neg)
    gmax = jnp.max(gl, axis=-1, keepdims=True)
    gsel = jnp.min(jnp.where(gl == gmax, lane, LANES), axis=-1, keepdims=True)
    p_grp = 1.0 / jnp.sum(jnp.exp(gl - gmax), axis=-1, keepdims=True)
    e_lane = lane - N_GROUPS
    in_grp = (e_lane >= 0) & (e_lane < N_EXPERTS) & ((e_lane >> 3) == gsel)
    el = jnp.where(in_grp, logits, neg)
    v0 = jnp.max(el, axis=-1, keepdims=True)
    i0 = jnp.min(jnp.where(el == v0, lane, LANES), axis=-1, keepdims=True)
    el1 = jnp.where(lane == i0, neg, el)
    v1 = jnp.max(el1, axis=-1, keepdims=True)
    i1 = jnp.min(jnp.where(el1 == v1, lane, LANES), axis=-1, keepdims=True)
    t = jnp.exp(v1 - v0)
    w0 = p_grp / (1.0 + t)
    w1 = p_grp * t / (1.0 + t)
    return i0, i1, w0, w1


def _lane_pack(shape, cols):
    lane = lax.broadcasted_iota(jnp.int32, shape, 1)
    out = jnp.zeros(shape, F32)
    for k, col in enumerate(cols):
        out = jnp.where(lane == k, col, out)
    return out


def _to_token_tiles(ref, val):
    n = val.shape[0]
    for s in range(ROW_TILES):
        ref[pl.ds(s, n, stride=ROW_TILES), :] = val[:, s * LANES:(s + 1) * LANES]


def _from_token_tiles(ref):
    n = ref.shape[0] // ROW_TILES
    return jnp.concatenate([ref[pl.ds(s, n, stride=ROW_TILES), :] for s in range(ROW_TILES)], axis=-1)


def _token_rows(ref, t):
    return ref.at[pl.ds(pl.multiple_of(t * ROW_TILES, ROW_TILES), ROW_TILES)]


def _merge_kernel(yr_ref, at_ref, gr_ref, gm_ref, x_ref, mod_ref, n2_ref, wro_ref, wmo_ref, wo_ref,
                  wrt_ref, brt_ref, h1_ref, u2_ref, meta_ref, cnt_ref, carry_ref):
    tm = x_ref.shape[0]

    @pl.when(pl.program_id(0) == 0)
    def _():
        carry_ref[...] = jnp.zeros_like(carry_ref)

    y_ret = _dot(yr_ref[...], wro_ref[...])
    y_mla = _dot(at_ref[...], wmo_ref[...])
    merged = gr_ref[...].astype(F32) * y_ret + gm_ref[...].astype(F32) * y_mla
    o = _dot(merged.astype(BF16), wo_ref[...])
    h1 = x_ref[...] + mod_ref[2:3, :] * o
    h1_ref[...] = h1
    u2 = _rms(h1) * n2_ref[...] * (1.0 + mod_ref[4:5, :]) + mod_ref[3:4, :]
    _to_token_tiles(u2_ref, u2)
    u_hi = u2.astype(BF16)
    u_lo = (u2 - u_hi.astype(F32)).astype(BF16)
    w = wrt_ref[...]
    w_hi = w.astype(BF16)
    w_lo = (w - w_hi.astype(F32)).astype(BF16)
    logits = _dot(u_hi, w_hi) + _dot(u_lo, w_hi) + _dot(u_hi, w_lo) + brt_ref[...]
    i0, i1, w0, w1 = _route(logits)
    lane = lax.broadcasted_iota(jnp.int32, (tm, LANES), 1)
    m0, m1 = lane == i0, lane == i1
    member = jnp.where(m0 | m1, 1.0, 0.0)
    tri = jnp.where(lax.broadcasted_iota(jnp.int32, (tm, tm), 0) > lax.broadcasted_iota(jnp.int32, (tm, tm), 1),
                    1.0, 0.0).astype(BF16)
    prefix = _dot(tri, member.astype(BF16)) + carry_ref[0:1, :]
    rank0 = jnp.sum(jnp.where(m0, prefix, 0.0), axis=-1, keepdims=True)
    rank1 = jnp.sum(jnp.where(m1, prefix, 0.0), axis=-1, keepdims=True)
    carry_ref[...] = carry_ref[...] + jnp.sum(member, axis=0, keepdims=True)
    cnt_ref[...] = carry_ref[...]
    meta_ref[...] = _lane_pack((tm, LANES), [i0.astype(F32), i1.astype(F32), rank0, rank1, w0, w1])


def _merge(y_ret, attn, gates, x2, mod3, norm2, w_ret_o, w_mla_o, w_out, w_rt, b_rt):
    tm = TM_PROJ
    per_b = SEQ // tm
    row = lambda j: pl.BlockSpec((tm, D_MODEL), lambda i: (i, j))
    return pl.pallas_call(
        _merge_kernel,
        grid=(N_TOK // tm,),
        in_specs=[row(0), row(0), row(0), row(1), row(0),
                  pl.BlockSpec((None, 6, D_MODEL), lambda i: (i // per_b, 0, 0)),
                  _resident((1, D_MODEL)),
                  _resident(w_ret_o.shape), _resident(w_mla_o.shape), _resident(w_out.shape),
                  _resident(w_rt.shape), _resident(b_rt.shape)],
        out_specs=[row(0),
                   pl.BlockSpec((tm * ROW_TILES, LANES), lambda i: (i, 0)),
                   pl.BlockSpec((tm, LANES), lambda i: (i, 0)),
                   pl.BlockSpec((SUBLANES, LANES), lambda i: (0, 0))],
        out_shape=[jax.ShapeDtypeStruct((N_TOK, D_MODEL), F32),
                   jax.ShapeDtypeStruct((N_TOK * ROW_TILES, LANES), F32),
                   jax.ShapeDtypeStruct((N_TOK, LANES), F32),
                   jax.ShapeDtypeStruct((SUBLANES, LANES), F32)],
        scratch_shapes=[pltpu.VMEM((SUBLANES, LANES), F32)],
        compiler_params=_params("arbitrary"),
        name="merge",
    )(y_ret, attn, gates, gates, x2, mod3, norm2, w_ret_o, w_mla_o, w_out, w_rt, b_rt)


def _plan_kernel(meta_ref, off_ref, dst_ref):
    m = meta_ref[...]
    lane = lax.broadcasted_iota(jnp.int32, m.shape, 1)
    off = off_ref[...]
    i0 = m[:, 0:1].astype(jnp.int32)
    i1 = m[:, 1:2].astype(jnp.int32)
    d0 = jnp.sum(jnp.where(lane == i0, off, 0.0), axis=-1, keepdims=True) + m[:, 2:3]
    d1 = jnp.sum(jnp.where(lane == i1, off, 0.0), axis=-1, keepdims=True) + m[:, 3:4]
    dst_ref[...] = _lane_pack(m.shape, [d0, d1]).astype(jnp.int32)


def _plan(meta, off_row):
    tm = 2048
    return pl.pallas_call(
        _plan_kernel,
        grid=(N_TOK // tm,),
        in_specs=[pl.BlockSpec((tm, LANES), lambda i: (i, 0)),
                  pl.BlockSpec((1, LANES), lambda i: (0, 0))],
        out_specs=pl.BlockSpec((tm, LANES), lambda i: (i, 0)),
        out_shape=jax.ShapeDtypeStruct((N_TOK, LANES), jnp.int32),
        compiler_params=_params("arbitrary"),
        name="plan",
    )(meta, off_row)


def _row_copy_wait(src_like, dst_like, sem):
    pltpu.make_async_copy(src_like, dst_like, sem).wait()


def _dispatch_kernel(d0_ref, d1_ref, seg_ref, u_ref, xs_ref, zero_ref, sem, zsem):
    i = pl.program_id(0)
    tm = u_ref.shape[0] // ROW_TILES
    tile_rows = TE * ROW_TILES

    def slot_tile(j):
        return xs_ref.at[pl.ds(pl.multiple_of(j * tile_rows, tile_rows), tile_rows)]

    @pl.when(i == 0)
    def _():
        zero_ref[...] = jnp.zeros_like(zero_ref)

        def tail(e):
            end = seg_ref[e + 1]
            return end > seg_ref[e], slot_tile(end // TE - 1)

        for e in range(N_EXPERTS):
            nonempty, dst = tail(e)

            @pl.when(nonempty)
            def _():
                pltpu.make_async_copy(zero_ref, dst, zsem).start()

        for e in range(N_EXPERTS):
            nonempty, dst = tail(e)

            @pl.when(nonempty)
            def _():
                pltpu.make_async_copy(zero_ref, dst, zsem).wait()

        def unused(j):
            return pltpu.make_async_copy(zero_ref, slot_tile(j), zsem)

        first_unused = seg_ref[N_EXPERTS] // TE
        lax.fori_loop(first_unused, N_TILES, lambda j, c: (unused(j).start(), c)[1], 0)
        lax.fori_loop(first_unused, N_TILES, lambda j, c: (unused(j).wait(), c)[1], 0)

    base = i * tm

    def body(r, carry):
        src = _token_rows(u_ref, r)
        pltpu.make_async_copy(src, _token_rows(xs_ref, d0_ref[base + r]), sem).start(priority=0)
        pltpu.make_async_copy(src, _token_rows(xs_ref, d1_ref[base + r]), sem).start(priority=1)
        return carry

    lax.fori_loop(0, tm, body, 0, unroll=8)
    _row_copy_wait(u_ref, xs_ref.at[pl.ds(0, tm * ROW_TILES)], sem)
    _row_copy_wait(u_ref, xs_ref.at[pl.ds(0, tm * ROW_TILES)], sem)


def _dispatch(d0, d1, seg, u2t):
    tm = TM_PROJ
    return pl.pallas_call(
        _dispatch_kernel,
        grid_spec=pltpu.PrefetchScalarGridSpec(
            num_scalar_prefetch=3,
            grid=(N_TOK // tm,),
            in_specs=[pl.BlockSpec((tm * ROW_TILES, LANES), lambda i, *_: (i, 0))],
            out_specs=pl.BlockSpec(memory_space=pl.ANY),
            scratch_shapes=[pltpu.VMEM((TE * ROW_TILES, LANES), F32),
                            pltpu.SemaphoreType.DMA(()), pltpu.SemaphoreType.DMA(())]),
        out_shape=jax.ShapeDtypeStruct((N_SLOTS * ROW_TILES, LANES), F32),
        compiler_params=_params("arbitrary"),
        name="dispatch",
    )(d0, d1, seg, u2t)


def _expert_kernel(te_ref, nv_ref, x_ref, w1_ref, w3_ref, w2_ref, y_ref, w1_s, w3_s, w2_s):
    j = pl.program_id(0)

    @pl.when(j < nv_ref[0])
    def _():
        @pl.when((j == 0) | (te_ref[j] != te_ref[jnp.maximum(j - 1, 0)]))
        def _():
            w1_s[...] = w1_ref[...].astype(BF16)
            w3_s[...] = w3_ref[...].astype(BF16)
            w2_s[...] = w2_ref[...].astype(BF16)

        x = _from_token_tiles(x_ref).astype(BF16)
        a = _dot(x, w1_s[...])
        b = _dot(x, w3_s[...])
        hid = (a * _sigmoid(a) * b).astype(BF16)
        _to_token_tiles(y_ref, _dot(hid, w2_s[...]))

    @pl.when(j >= nv_ref[0])
    def _():
        y_ref[...] = jnp.zeros_like(y_ref)


def _experts(tile_expert, n_valid, xs, w1, w3, w2):
    tile = lambda j, te, nv: jnp.minimum(j, nv[0] - 1)
    wspec = lambda shape: pl.BlockSpec((None,) + shape, lambda j, te, nv: (te[tile(j, te, nv)], 0, 0))
    slots = pl.BlockSpec((TE * ROW_TILES, LANES), lambda j, te, nv: (tile(j, te, nv), 0))
    return pl.pallas_call(
        _expert_kernel,
        grid_spec=pltpu.PrefetchScalarGridSpec(
            num_scalar_prefetch=2,
            grid=(N_TILES,),
            in_specs=[slots, wspec((D_MODEL, D_EXPERT)), wspec((D_MODEL, D_EXPERT)), wspec((D_EXPERT, D_MODEL))],
            out_specs=pl.BlockSpec((TE * ROW_TILES, LANES), lambda j, te, nv: (j, 0)),
            scratch_shapes=[pltpu.VMEM((D_MODEL, D_EXPERT), BF16), pltpu.VMEM((D_MODEL, D_EXPERT), BF16),
                            pltpu.VMEM((D_EXPERT, D_MODEL), BF16)]),
        out_shape=jax.ShapeDtypeStruct((N_SLOTS * ROW_TILES, LANES), F32),
        compiler_params=_params("arbitrary"),
        name="experts",
    )(tile_expert, n_valid, xs, w1, w3, w2)


def _final_kernel(d0_ref, d1_ref, h1_ref, meta_ref, mod_ref, fn_ref, ys_ref, o_ref, ybuf, sem):
    i = pl.program_id(0)
    tm = h1_ref.shape[0]
    base = i * tm

    def body(r, carry):
        pltpu.make_async_copy(_token_rows(ys_ref, d0_ref[base + r]), _token_rows(ybuf.at[0], r), sem).start(priority=0)
        pltpu.make_async_copy(_token_rows(ys_ref, d1_ref[base + r]), _token_rows(ybuf.at[1], r), sem).start(priority=1)
        return carry

    lax.fori_loop(0, tm, body, 0, unroll=8)
    _row_copy_wait(ys_ref.at[pl.ds(0, tm * ROW_TILES)], ybuf.at[0], sem)
    _row_copy_wait(ys_ref.at[pl.ds(0, tm * ROW_TILES)], ybuf.at[1], sem)
    m = meta_ref[...]
    moe = m[:, 4:5] * _from_token_tiles(ybuf.at[0]) + m[:, 5:6] * _from_token_tiles(ybuf.at[1])
    h2 = h1_ref[...] + mod_ref[5:6, :] * moe
    o_ref[...] = _rms(h2) * fn_ref[...]


def _final(d0, d1, h1, meta, mod3, final_norm, ys):
    tm = TM_PROJ
    per_b = SEQ // tm
    return pl.pallas_call(
        _final_kernel,
        grid_spec=pltpu.PrefetchScalarGridSpec(
            num_scalar_prefetch=2,
            grid=(N_TOK // tm,),
            in_specs=[pl.BlockSpec((tm, D_MODEL), lambda i, *_: (i, 0)),
                      pl.BlockSpec((tm, LANES), lambda i, *_: (i, 0)),
                      pl.BlockSpec((None, 6, D_MODEL), lambda i, *_: (i // per_b, 0, 0)),
                      pl.BlockSpec((1, D_MODEL), lambda i, *_: (0, 0)),
                      pl.BlockSpec(memory_space=pl.ANY)],
            out_specs=pl.BlockSpec((tm, D_MODEL), lambda i, *_: (i, 0)),
            scratch_shapes=[pltpu.VMEM((2, tm * ROW_TILES, LANES), F32), pltpu.SemaphoreType.DMA(())]),
        out_shape=jax.ShapeDtypeStruct((N_TOK, D_MODEL), F32),
        compiler_params=_params("arbitrary"),
        name="final",
    )(d0, d1, h1, meta, mod3, final_norm, ys)


def _slot_layout(counts):
    cnt = counts[0, N_GROUPS:N_GROUPS + N_EXPERTS].astype(jnp.int32)
    tile_end = jnp.cumsum((cnt + TE - 1) // TE)
    seg = jnp.concatenate([jnp.zeros((1,), jnp.int32), tile_end * TE])
    off_row = jnp.zeros((1, LANES), F32).at[0, N_GROUPS:N_GROUPS + N_EXPERTS].set(seg[:-1].astype(F32))
    tile_ids = jnp.arange(N_TILES, dtype=jnp.int32)
    tile_expert = jnp.sum((tile_end[None, :] <= tile_ids[:, None]).astype(jnp.int32), axis=1)
    tile_expert = jnp.minimum(tile_expert, N_EXPERTS - 1)
    return seg, off_row, tile_expert, tile_end[-1:]


def _rope_tables(dim):
    pos = jnp.arange(SEQ, dtype=F32)
    inv = ROPE_THETA ** (-jnp.arange(0, dim, 2, dtype=F32) / dim)
    ang = pos[:, None] * inv[None, :]
    return jnp.cos(ang), jnp.sin(ang)


def _decay_tables():
    c = RET_CHUNK
    log_gamma = jnp.log1p(-jnp.exp2(-5.0 - jnp.arange(RET_HEADS, dtype=F32)))
    idx = jnp.arange(c, dtype=F32)
    rel = idx[:, None] - idx[None, :]
    dec = jnp.where(rel[None] >= 0, jnp.exp(log_gamma[:, None, None] * jnp.maximum(rel, 0.0)[None]), 0.0)
    xi = jnp.exp(log_gamma[:, None] * (idx[None, :] + 1.0))[:, :, None]
    zeta = jnp.exp(log_gamma[:, None] * (c - 1.0 - idx[None, :]))[:, :, None]
    cd = jnp.exp(log_gamma * c)[:, None, None]
    return dec, xi, zeta, cd


def _rotate_half_cols(w):
    half = w.shape[-1] // 2
    return jnp.concatenate([-w[..., half:], w[..., :half]], axis=-1)


def kernel(x, c, w_ada, b_ada, norm1, norm2, w_in, w_ret_o, q_norm, kv_norm, w_uq, w_ukv, w_mla_o, w_out,
           w_grp, b_grp, w_exp, b_exp, w1, w3, w2, final_norm):
    assert x.shape == (BATCH, SEQ, D_MODEL) and w_ada.shape[0] == 1
    x2 = x.reshape(N_TOK, D_MODEL)

    o_lat = 4 * RET_W
    o_pe = o_lat + MLA_Q_LORA + MLA_KV_LORA
    o_gate = o_pe + MLA_ROPE
    wi = w_in[0]
    col = jnp.arange(o_lat)
    k_scale = jnp.where((col >= RET_W) & (col < 2 * RET_W), RET_DK ** -0.5, 1.0).astype(F32)
    w_ret = (wi[:, :o_lat] * k_scale[None, :]).astype(BF16)
    w_lat = jnp.concatenate([wi[:, o_lat:o_gate], _rotate_half_cols(wi[:, o_pe:o_gate])], axis=1).astype(BF16)
    w_gate = wi[:, o_gate:].astype(BF16)
    wq = w_uq[0].reshape(MLA_Q_LORA, MLA_HEADS, MLA_QK)
    wq = jnp.concatenate([wq, _rotate_half_cols(wq[..., MLA_NOPE:])], axis=-1)
    wq = wq.transpose(1, 0, 2).astype(BF16)
    wkv = w_ukv[0].reshape(MLA_KV_LORA, MLA_HEADS, MLA_NOPE + MLA_V).transpose(1, 0, 2).astype(BF16)
    pad = LANES - N_GROUPS - N_EXPERTS
    w_rt = jnp.concatenate([w_grp[0], w_exp[0], jnp.zeros((D_MODEL, pad), F32)], axis=1)
    b_rt = jnp.concatenate([b_grp[0], b_exp[0], jnp.zeros((pad,), F32)])[None, :]

    ret_cos, ret_sin = _rope_tables(RET_DK)
    mla_cos, mla_sin = _rope_tables(MLA_ROPE)
    mla_cos = jnp.concatenate([mla_cos, mla_cos], axis=-1)
    mla_sin = jnp.concatenate([mla_sin, mla_sin], axis=-1)
    dec, xi, zeta, cd = _decay_tables()

    mod3 = _ada(c, w_ada[0], b_ada[0]).reshape(BATCH, 6, D_MODEL)
    ret, lat, gates = _inproj(x2, mod3, norm1, ret_cos, ret_sin, w_ret, w_lat, w_gate)
    y_ret = _retention(ret.reshape(BATCH, SEQ, 4 * RET_W), dec, xi, zeta, cd)
    attn = _mla(lat.reshape(BATCH, SEQ, MLA_LAT_W), q_norm, kv_norm, wq, wkv, mla_cos, mla_sin)
    h1, u2t, meta, counts = _merge(y_ret.reshape(N_TOK, D_MODEL), attn.reshape(N_TOK, D_MODEL), gates, x2, mod3,
                                   norm2, w_ret_o[0].astype(BF16), w_mla_o[0].astype(BF16),
                                   w_out[0].astype(BF16), w_rt, b_rt)
    seg, off_row, tile_expert, n_valid = _slot_layout(counts)
    dst = _plan(meta, off_row)
    d0, d1 = dst[:, 0], dst[:, 1]
    xs = _dispatch(d0, d1, seg, u2t)
    e_shape = (N_EXPERTS, D_MODEL, D_EXPERT)
    ys = _experts(tile_expert, n_valid, xs, w1[0].reshape(e_shape), w3[0].reshape(e_shape),
                  w2[0].reshape(N_EXPERTS, D_EXPERT, D_MODEL))
    out = _final(d0, d1, h1, meta, mod3, final_norm.reshape(1, D_MODEL), ys)
    return out.reshape(BATCH, SEQ, D_MODEL)
```

```python
import jax
import jax.numpy as jnp
from jax import lax
from jax.experimental import pallas as pl
from jax.experimental.pallas import tpu as pltpu

D_MODEL = 1024
BATCH = 8
SEQ = 2048
N_TOK = BATCH * SEQ

RET_HEADS = 4
RET_DK = 256
RET_DV = 256
RET_CHUNK = 256
RET_BLK = 512
RET_W = RET_HEADS * RET_DK

MLA_HEADS = 8
MLA_NOPE = 128
MLA_ROPE = 64
MLA_V = 128
MLA_Q_LORA = 384
MLA_KV_LORA = 256
MLA_LAT_W = MLA_Q_LORA + MLA_KV_LORA + 2 * MLA_ROPE
MLA_QK = MLA_NOPE + MLA_ROPE
ROPE_THETA = 10000.0

N_GROUPS = 4
EXPERTS_PER_GROUP = 8
N_EXPERTS = N_GROUPS * EXPERTS_PER_GROUP
D_EXPERT = 256
EPS = 1e-6
LOG2_E = 1.4426950408889634

LANES = 128
SUBLANES = 8
ROW_TILES = D_MODEL // LANES
VMEM_LIMIT = 56 * 1024 * 1024

TM_PROJ = 512
TQ = 256
MLA_HPS = 2
TE = 256
TOP_K = 2
N_TILES = N_TOK * TOP_K // TE + N_EXPERTS
N_SLOTS = N_TILES * TE

F32 = jnp.float32
BF16 = jnp.bfloat16


def _sigmoid(x):
    return 1.0 / (1.0 + jnp.exp(-x))


def _rms(x):
    return x * lax.rsqrt(jnp.mean(x * x, axis=-1, keepdims=True) + EPS)


def _dot(a, b):
    return jnp.dot(a, b, preferred_element_type=F32)


def _dot_nt(a, b):
    return lax.dot_general(a, b, (((1,), (1,)), ((), ())), preferred_element_type=F32)


def _dot_tn(a, b):
    return lax.dot_general(a, b, (((0,), (0,)), ((), ())), preferred_element_type=F32)


def _params(*sem):
    return pltpu.CompilerParams(dimension_semantics=sem, vmem_limit_bytes=VMEM_LIMIT)


def _resident(shape):
    nd = len(shape)
    return pl.BlockSpec(shape, lambda *_: (0,) * nd, pipeline_mode=pl.Buffered(1))


def _ada_kernel(c_ref, w_ref, b_ref, o_ref):
    c = c_ref[...]
    act = (c * _sigmoid(c)).astype(BF16)
    o_ref[...] = _dot(act, w_ref[...].astype(BF16)) + b_ref[...]


def _ada(c, w_ada, b_ada):
    n = w_ada.shape[1]
    tn = D_MODEL
    return pl.pallas_call(
        _ada_kernel,
        grid=(n // tn,),
        in_specs=[pl.BlockSpec((BATCH, D_MODEL), lambda j: (0, 0)),
                  pl.BlockSpec((D_MODEL, tn), lambda j: (0, j)),
                  pl.BlockSpec((1, tn), lambda j: (0, j))],
        out_specs=pl.BlockSpec((BATCH, tn), lambda j: (0, j)),
        out_shape=jax.ShapeDtypeStruct((BATCH, n), F32),
        compiler_params=_params("arbitrary"),
        name="ada",
    )(c, w_ada, b_ada.reshape(1, n))


def _inproj_kernel(x_ref, mod_ref, n1_ref, cos_ref, sin_ref, wr_ref, wm_ref, wg_ref, ret_ref, lat_ref, gate_ref):
    y = _rms(x_ref[...]) * n1_ref[...]
    u = (y * (1.0 + mod_ref[1:2, :]) + mod_ref[0:1, :]).astype(BF16)
    cos, sin = cos_ref[...], sin_ref[...]
    half = RET_DK // 2
    for n in range(0, 2 * RET_W, RET_DK):
        p = _dot(u, wr_ref[:, n:n + RET_DK])
        x1, x2 = p[:, :half], p[:, half:]
        ret_ref[:, n:n + half] = (x1 * cos - x2 * sin).astype(BF16)
        ret_ref[:, n + half:n + RET_DK] = (x2 * cos + x1 * sin).astype(BF16)
    step = 512
    for n in range(2 * RET_W, 3 * RET_W, step):
        ret_ref[:, n:n + step] = _dot(u, wr_ref[:, n:n + step]).astype(BF16)
    for n in range(3 * RET_W, 4 * RET_W, step):
        p = _dot(u, wr_ref[:, n:n + step])
        ret_ref[:, n:n + step] = (p * _sigmoid(p)).astype(BF16)
    lat_ref[...] = _dot(u, wm_ref[...]).astype(BF16)
    for n in range(0, 2 * D_MODEL, step):
        gate_ref[:, n:n + step] = _sigmoid(_dot(u, wg_ref[:, n:n + step])).astype(BF16)


def _inproj(x2, mod3, norm1, cos, sin, w_ret, w_lat, w_gate):
    tm = TM_PROJ
    per_b = SEQ // tm
    rope_tab = pl.BlockSpec((tm, RET_DK // 2), lambda i: (i % per_b, 0))
    return pl.pallas_call(
        _inproj_kernel,
        grid=(N_TOK // tm,),
        in_specs=[pl.BlockSpec((tm, D_MODEL), lambda i: (i, 0)),
                  pl.BlockSpec((None, 6, D_MODEL), lambda i: (i // per_b, 0, 0)),
                  _resident((1, D_MODEL)), rope_tab, rope_tab,
                  _resident(w_ret.shape), _resident(w_lat.shape), _resident(w_gate.shape)],
        out_specs=[pl.BlockSpec((tm, 4 * RET_W), lambda i: (i, 0)),
                   pl.BlockSpec((tm, MLA_LAT_W), lambda i: (i, 0)),
                   pl.BlockSpec((tm, 2 * D_MODEL), lambda i: (i, 0))],
        out_shape=[jax.ShapeDtypeStruct((N_TOK, 4 * RET_W), BF16),
                   jax.ShapeDtypeStruct((N_TOK, MLA_LAT_W), BF16),
                   jax.ShapeDtypeStruct((N_TOK, 2 * D_MODEL), BF16)],
        compiler_params=_params("arbitrary"),
        name="inproj",
    )(x2, mod3, norm1, cos, sin, w_ret, w_lat, w_gate)


def _ret_kernel(q_ref, k_ref, v_ref, g_ref, dec_ref, xi_ref, zeta_ref, cd_ref, o_ref, state_ref):
    @pl.when(pl.program_id(1) == 0)
    def _():
        state_ref[...] = jnp.zeros_like(state_ref)

    heads = range(RET_HEADS)
    col = lambda h: slice(h * RET_DK, (h + 1) * RET_DK)
    for c in range(RET_BLK // RET_CHUNK):
        rows = slice(c * RET_CHUNK, (c + 1) * RET_CHUNK)
        scores = [(_dot_nt(q_ref[rows, col(h)], k_ref[rows, col(h)]) * dec_ref[h]).astype(BF16) for h in heads]
        carried = [xi_ref[h] * _dot(q_ref[rows, col(h)], state_ref[h].astype(BF16)) for h in heads]
        y = [_dot(scores[h], v_ref[rows, col(h)]) + carried[h] for h in heads]
        for h in heads:
            k_dec = (k_ref[rows, col(h)].astype(F32) * zeta_ref[h]).astype(BF16)
            state_ref[h] = state_ref[h] * cd_ref[h] + _dot_tn(k_dec, v_ref[rows, col(h)])
        yc = [y[h] - jnp.mean(y[h], axis=-1, keepdims=True) for h in heads]
        inv = [lax.rsqrt(jnp.mean(yc[h] * yc[h], axis=-1, keepdims=True) + EPS) for h in heads]
        for h in heads:
            o_ref[rows, col(h)] = (g_ref[rows, col(h)].astype(F32) * (yc[h] * inv[h])).astype(BF16)


def _retention(ret3, dec, xi, zeta, cd):
    blk = lambda part: pl.BlockSpec((None, RET_BLK, RET_W), lambda b, j: (b, j, part))
    whole = lambda a: pl.BlockSpec(a.shape, lambda b, j: (0,) * a.ndim)
    return pl.pallas_call(
        _ret_kernel,
        grid=(BATCH, SEQ // RET_BLK),
        in_specs=[blk(0), blk(1), blk(2), blk(3), whole(dec), whole(xi), whole(zeta), whole(cd)],
        out_specs=pl.BlockSpec((None, RET_BLK, RET_HEADS * RET_DV), lambda b, j: (b, j, 0)),
        out_shape=jax.ShapeDtypeStruct((BATCH, SEQ, RET_HEADS * RET_DV), BF16),
        scratch_shapes=[pltpu.VMEM((RET_HEADS, RET_DK, RET_DV), F32)],
        compiler_params=_params("arbitrary", "arbitrary"),
        name="retention",
    )(ret3, ret3, ret3, ret3, dec, xi, zeta, cd)


def _mla_kernel(lat_ref, qn_ref, kvn_ref, wq_ref, wkv_ref, cos_ref, sin_ref, o_ref,
                cq_s, ckv_s, kpe_s, q_s, k_s, v_s):
    h = pl.program_id(1)
    o_q, o_kv, o_pe, o_rot = 0, MLA_Q_LORA, MLA_Q_LORA + MLA_KV_LORA, MLA_Q_LORA + MLA_KV_LORA + MLA_ROPE
    cos, sin = cos_ref[...], sin_ref[...]

    @pl.when(h == 0)
    def _():
        cq_s[...] = (_rms(lat_ref[:, o_q:o_kv].astype(F32)) * qn_ref[...]).astype(BF16)
        ckv_s[...] = (_rms(lat_ref[:, o_kv:o_pe].astype(F32)) * kvn_ref[...]).astype(BF16)
        pe = lat_ref[:, o_pe:o_rot].astype(F32)
        rot = lat_ref[:, o_rot:o_rot + MLA_ROPE].astype(F32)
        kpe_s[...] = (pe * cos + rot * sin).astype(BF16)

    scale = (MLA_QK ** -0.5) * LOG2_E
    for g in range(MLA_HPS):
        qf = _dot(cq_s[...], wq_ref[g])
        q_s[g, :, :MLA_NOPE] = (qf[:, :MLA_NOPE] * scale).astype(BF16)
        q_pe = qf[:, MLA_NOPE:MLA_QK] * cos + qf[:, MLA_QK:] * sin
        q_s[g, :, MLA_NOPE:] = (q_pe * scale).astype(BF16)
        kvf = _dot(ckv_s[...], wkv_ref[g])
        k_s[g, :, :MLA_NOPE] = kvf[:, :MLA_NOPE].astype(BF16)
        k_s[g, :, MLA_NOPE:] = kpe_s[...]
        v_s[g, :, :MLA_V] = kvf[:, MLA_NOPE:].astype(BF16)
        v_s[g, :, MLA_V:] = jnp.ones((SEQ, MLA_V), BF16)

    causal = lax.broadcasted_iota(jnp.int32, (TQ, TQ), 0) >= lax.broadcasted_iota(jnp.int32, (TQ, TQ), 1)
    heads = range(MLA_HPS)
    n_blk = SEQ // TQ

    def scores(i):
        lo, hi = i * TQ, (i + 1) * TQ
        diag = [jnp.where(causal, _dot_nt(q_s[g, lo:hi, :], k_s[g, lo:hi, :]), -jnp.inf) for g in heads]
        past = [_dot_nt(q_s[g, lo:hi, :], k_s[g, :lo, :]) if i > 0 else None for g in heads]
        return diag, past

    pending = scores(0)
    for i in range(n_blk):
        lo, hi = i * TQ, (i + 1) * TQ
        diag, past = pending
        if i + 1 < n_blk:
            pending = scores(i + 1)
        m = [jnp.max(diag[g], axis=-1, keepdims=True) for g in heads]
        if i > 0:
            m = [jnp.maximum(m[g], jnp.max(past[g], axis=-1, keepdims=True)) for g in heads]
        acc = [_dot(jnp.exp2(diag[g] - m[g]).astype(BF16), v_s[g, lo:hi, :]) for g in heads]
        if i > 0:
            acc = [acc[g] + _dot(jnp.exp2(past[g] - m[g]).astype(BF16), v_s[g, :lo, :]) for g in heads]
        for g in heads:
            o_ref[lo:hi, g * MLA_V:(g + 1) * MLA_V] = (acc[g][:, :MLA_V] / acc[g][:, MLA_V:]).astype(BF16)


def _mla(lat3, q_norm, kv_norm, wq, wkv, cos, sin):
    hps = MLA_HPS
    return pl.pallas_call(
        _mla_kernel,
        grid=(BATCH, MLA_HEADS // hps),
        in_specs=[pl.BlockSpec((None, SEQ, MLA_LAT_W), lambda b, h: (b, 0, 0)),
                  pl.BlockSpec((1, MLA_Q_LORA), lambda b, h: (0, 0)),
                  pl.BlockSpec((1, MLA_KV_LORA), lambda b, h: (0, 0)),
                  pl.BlockSpec((hps, MLA_Q_LORA, MLA_QK + MLA_ROPE), lambda b, h: (h, 0, 0)),
                  pl.BlockSpec((hps, MLA_KV_LORA, MLA_NOPE + MLA_V), lambda b, h: (h, 0, 0)),
                  pl.BlockSpec((SEQ, MLA_ROPE), lambda b, h: (0, 0)),
                  pl.BlockSpec((SEQ, MLA_ROPE), lambda b, h: (0, 0))],
        out_specs=pl.BlockSpec((None, SEQ, hps * MLA_V), lambda b, h: (b, 0, h)),
        out_shape=jax.ShapeDtypeStruct((BATCH, SEQ, MLA_HEADS * MLA_V), BF16),
        scratch_shapes=[pltpu.VMEM((SEQ, MLA_Q_LORA), BF16),
                        pltpu.VMEM((SEQ, MLA_KV_LORA), BF16),
                        pltpu.VMEM((SEQ, MLA_ROPE), BF16),
                        pltpu.VMEM((hps, SEQ, MLA_QK), BF16),
                        pltpu.VMEM((hps, SEQ, MLA_QK), BF16),
                        pltpu.VMEM((hps, SEQ, 2 * MLA_V), BF16)],
        compiler_params=_params("arbitrary", "arbitrary"),
        name="mla",
    )(lat3, q_norm, kv_norm, wq, wkv, cos, sin)


def _route(logits):
    lane = lax.broadcasted_iota(jnp.int32, logits.shape, 1)
    neg = -jnp.inf
    gl = jnp.where(lane < N_GROUPS, logits, neg)
    gmax = jnp.max(gl, axis=-1, keepdims=True)
    gsel = jnp.min(jnp.where(gl == gmax, lane, LANES), axis=-1, keepdims=True)
    p_grp = 1.0 / jnp.sum(jnp.exp(gl - gmax), axis=-1, keepdims=True)
    e_lane = lane - N_GROUPS
    in_grp = (e_lane >= 0) & (e_lane < N_EXPERTS) & ((e_lane >> 3) == gsel)
    el = jnp.where(in_grp, logits, neg)
    v0 = jnp.max(el, axis=-1, keepdims=True)
    i0 = jnp.min(jnp.where(el == v0, lane, LANES), axis=-1, keepdims=True)
    el1 = jnp.where(lane == i0, neg, el)
    v1 = jnp.max(el1, axis=-1, keepdims=True)
    i1 = jnp.min(jnp.where(el1 == v1, lane, LANES), axis=-1, keepdims=True)
    t = jnp.exp(v1 - v0)
    w0 = p_grp / (1.0 + t)
    w1 = p_grp * t / (1.0 + t)
    return i0, i1, w0, w1


def _lane_pack(shape, cols):
    lane = lax.broadcasted_iota(jnp.int32, shape, 1)
    out = jnp.zeros(shape, F32)
    for k, col in enumerate(cols):
        out = jnp.where(lane == k, col, out)
    return out


def _to_token_tiles(ref, val):
    n = val.shape[0]
    for s in range(ROW_TILES):
        ref[pl.ds(s, n, stride=ROW_TILES), :] = val[:, s * LANES:(s + 1) * LANES]


def _from_token_tiles(ref):
    n = ref.shape[0] // ROW_TILES
    return jnp.concatenate([ref[pl.ds(s, n, stride=ROW_TILES), :] for s in range(ROW_TILES)], axis=-1)


def _token_rows(ref, t):
    return ref.at[pl.ds(pl.multiple_of(t * ROW_TILES, ROW_TILES), ROW_TILES)]


def _merge_kernel(yr_ref, at_ref, gr_ref, gm_ref, x_ref, mod_ref, n2_ref, wro_ref, wmo_ref, wo_ref,
                  wrt_ref, brt_ref, h1_ref, u2_ref, meta_ref, cnt_ref, carry_ref):
    tm = x_ref.shape[0]

    @pl.when(pl.program_id(0) == 0)
    def _():
        carry_ref[...] = jnp.zeros_like(carry_ref)

    y_ret = _dot(yr_ref[...], wro_ref[...])
    y_mla = _dot(at_ref[...], wmo_ref[...])
    merged = gr_ref[...].astype(F32) * y_ret + gm_ref[...].astype(F32) * y_mla
    o = _dot(merged.astype(BF16), wo_ref[...])
    h1 = x_ref[...] + mod_ref[2:3, :] * o
    h1_ref[...] = h1
    u2 = _rms(h1) * n2_ref[...] * (1.0 + mod_ref[4:5, :]) + mod_ref[3:4, :]
    _to_token_tiles(u2_ref, u2)
    u_hi = u2.astype(BF16)
    u_lo = (u2 - u_hi.astype(F32)).astype(BF16)
    w = wrt_ref[...]
    w_hi = w.astype(BF16)
    w_lo = (w - w_hi.astype(F32)).astype(BF16)
    logits = _dot(u_hi, w_hi) + _dot(u_lo, w_hi) + _dot(u_hi, w_lo) + brt_ref[...]
    i0, i1, w0, w1 = _route(logits)
    lane = lax.broadcasted_iota(jnp.int32, (tm, LANES), 1)
    m0, m1 = lane == i0, lane == i1
    member = jnp.where(m0 | m1, 1.0, 0.0)
    tri = jnp.where(lax.broadcasted_iota(jnp.int32, (tm, tm), 0) > lax.broadcasted_iota(jnp.int32, (tm, tm), 1),
                    1.0, 0.0).astype(BF16)
    prefix = _dot(tri, member.astype(BF16)) + carry_ref[0:1, :]
    rank0 = jnp.sum(jnp.where(m0, prefix, 0.0), axis=-1, keepdims=True)
    rank1 = jnp.sum(jnp.where(m1, prefix, 0.0), axis=-1, keepdims=True)
    carry_ref[...] = carry_ref[...] + jnp.sum(member, axis=0, keepdims=True)
    cnt_ref[...] = carry_ref[...]
    meta_ref[...] = _lane_pack((tm, LANES), [i0.astype(F32), i1.astype(F32), rank0, rank1, w0, w1])


def _merge(y_ret, attn, gates, x2, mod3, norm2, w_ret_o, w_mla_o, w_out, w_rt, b_rt):
    tm = TM_PROJ
    per_b = SEQ // tm
    row = lambda j: pl.BlockSpec((tm, D_MODEL), lambda i: (i, j))
    return pl.pallas_call(
        _merge_kernel,
        grid=(N_TOK // tm,),
        in_specs=[row(0), row(0), row(0), row(1), row(0),
                  pl.BlockSpec((None, 6, D_MODEL), lambda i: (i // per_b, 0, 0)),
                  _resident((1, D_MODEL)),
                  _resident(w_ret_o.shape), _resident(w_mla_o.shape), _resident(w_out.shape),
                  _resident(w_rt.shape), _resident(b_rt.shape)],
        out_specs=[row(0),
                   pl.BlockSpec((tm * ROW_TILES, LANES), lambda i: (i, 0)),
                   pl.BlockSpec((tm, LANES), lambda i: (i, 0)),
                   pl.BlockSpec((SUBLANES, LANES), lambda i: (0, 0))],
        out_shape=[jax.ShapeDtypeStruct((N_TOK, D_MODEL), F32),
                   jax.ShapeDtypeStruct((N_TOK * ROW_TILES, LANES), F32),
                   jax.ShapeDtypeStruct((N_TOK, LANES), F32),
                   jax.ShapeDtypeStruct((SUBLANES, LANES), F32)],
        scratch_shapes=[pltpu.VMEM((SUBLANES, LANES), F32)],
        compiler_params=_params("arbitrary"),
        name="merge",
    )(y_ret, attn, gates, gates, x2, mod3, norm2, w_ret_o, w_mla_o, w_out, w_rt, b_rt)


def _plan_kernel(meta_ref, off_ref, dst_ref):
    m = meta_ref[...]
    lane = lax.broadcasted_iota(jnp.int32, m.shape, 1)
    off = off_ref[...]
    i0 = m[:, 0:1].astype(jnp.int32)
    i1 = m[:, 1:2].astype(jnp.int32)
    d0 = jnp.sum(jnp.where(lane == i0, off, 0.0), axis=-1, keepdims=True) + m[:, 2:3]
    d1 = jnp.sum(jnp.where(lane == i1, off, 0.0), axis=-1, keepdims=True) + m[:, 3:4]
    dst_ref[...] = _lane_pack(m.shape, [d0, d1]).astype(jnp.int32)


def _plan(meta, off_row):
    tm = 2048
    return pl.pallas_call(
        _plan_kernel,
        grid=(N_TOK // tm,),
        in_specs=[pl.BlockSpec((tm, LANES), lambda i: (i, 0)),
                  pl.BlockSpec((1, LANES), lambda i: (0, 0))],
        out_specs=pl.BlockSpec((tm, LANES), lambda i: (i, 0)),
        out_shape=jax.ShapeDtypeStruct((N_TOK, LANES), jnp.int32),
        compiler_params=_params("arbitrary"),
        name="plan",
    )(meta, off_row)


def _row_copy_wait(src_like, dst_like, sem):
    pltpu.make_async_copy(src_like, dst_like, sem).wait()


def _invert_kernel(d0_ref, d1_ref, seg_ref, cnt_ref, inv_ref):
    def clear(s, carry):
        inv_ref[s] = 0
        return carry

    for e in range(N_EXPERTS):
        lax.fori_loop(seg_ref[e] + cnt_ref[e], seg_ref[e + 1], clear, 0)
    lax.fori_loop(seg_ref[N_EXPERTS], N_SLOTS, clear, 0)

    def body(t, carry):
        inv_ref[d0_ref[t]] = t
        inv_ref[d1_ref[t]] = t
        return carry

    lax.fori_loop(0, N_TOK, body, 0, unroll=8)


def _invert(d0, d1, seg, cnt):
    return pl.pallas_call(
        _invert_kernel,
        grid_spec=pltpu.PrefetchScalarGridSpec(
            num_scalar_prefetch=4, grid=(1,), in_specs=[],
            out_specs=pl.BlockSpec(memory_space=pltpu.SMEM)),
        out_shape=jax.ShapeDtypeStruct((N_SLOTS,), jnp.int32),
        compiler_params=_params("arbitrary"),
        name="invert",
    )(d0, d1, seg, cnt)


def _expert_kernel(te_ref, nv_ref, inv_ref, u_ref, w1_ref, w3_ref, w2_ref, y_ref, w1_s, w3_s, w2_s, xbuf, sem):
    j = pl.program_id(0)
    nv = nv_ref[0]
    tile_rows = TE * ROW_TILES

    def gather(t):
        buf, s = xbuf.at[t % 2], sem.at[t % 2]
        base = t * TE

        def pair(p, carry):
            for k in range(2):
                r = 2 * p + k
                pltpu.make_async_copy(_token_rows(u_ref, inv_ref[base + r]), _token_rows(buf, r), s).start(priority=k)
            return carry

        lax.fori_loop(0, TE // 2, pair, 0, unroll=4)

    @pl.when(j == 0)
    def _():
        gather(0)

    @pl.when(j + 1 < nv)
    def _():
        gather(j + 1)

    @pl.when(j < nv)
    def _():
        @pl.when((j == 0) | (te_ref[j] != te_ref[jnp.maximum(j - 1, 0)]))
        def _():
            w1_s[...] = w1_ref[...].astype(BF16)
            w3_s[...] = w3_ref[...].astype(BF16)
            w2_s[...] = w2_ref[...].astype(BF16)

        buf = xbuf.at[j % 2]
        _row_copy_wait(u_ref.at[pl.ds(0, tile_rows)], buf, sem.at[j % 2])
        x = _from_token_tiles(buf).astype(BF16)
        a = _dot(x, w1_s[...])
        b = _dot(x, w3_s[...])
        hid = (a * _sigmoid(a) * b).astype(BF16)
        _to_token_tiles(y_ref, _dot(hid, w2_s[...]))

    @pl.when(j >= nv)
    def _():
        y_ref[...] = jnp.zeros_like(y_ref)


def _experts(tile_expert, n_valid, inv, u2t, w1, w3, w2):
    wspec = lambda shape: pl.BlockSpec(
        (None,) + shape, lambda j, te, nv, inv: (te[jnp.minimum(j, nv[0] - 1)], 0, 0))
    return pl.pallas_call(
        _expert_kernel,
        grid_spec=pltpu.PrefetchScalarGridSpec(
            num_scalar_prefetch=3,
            grid=(N_TILES,),
            in_specs=[pl.BlockSpec(memory_space=pl.ANY),
                      wspec((D_MODEL, D_EXPERT)), wspec((D_MODEL, D_EXPERT)), wspec((D_EXPERT, D_MODEL))],
            out_specs=pl.BlockSpec((TE * ROW_TILES, LANES), lambda j, *_: (j, 0)),
            scratch_shapes=[pltpu.VMEM((D_MODEL, D_EXPERT), BF16), pltpu.VMEM((D_MODEL, D_EXPERT), BF16),
                            pltpu.VMEM((D_EXPERT, D_MODEL), BF16),
                            pltpu.VMEM((2, TE * ROW_TILES, LANES), F32), pltpu.SemaphoreType.DMA((2,))]),
        out_shape=jax.ShapeDtypeStruct((N_SLOTS * ROW_TILES, LANES), F32),
        compiler_params=_params("arbitrary"),
        name="experts",
    )(tile_expert, n_valid, inv, u2t, w1, w3, w2)


def _final_kernel(d0_ref, d1_ref, h1_ref, meta_ref, mod_ref, fn_ref, ys_ref, o_ref, ybuf, sem):
    i = pl.program_id(0)
    tm = h1_ref.shape[0]
    base = i * tm

    def body(r, carry):
        pltpu.make_async_copy(_token_rows(ys_ref, d0_ref[base + r]), _token_rows(ybuf.at[0], r), sem).start(priority=0)
        pltpu.make_async_copy(_token_rows(ys_ref, d1_ref[base + r]), _token_rows(ybuf.at[1], r), sem).start(priority=1)
        return carry

    lax.fori_loop(0, tm, body, 0, unroll=8)
    _row_copy_wait(ys_ref.at[pl.ds(0, tm * ROW_TILES)], ybuf.at[0], sem)
    _row_copy_wait(ys_ref.at[pl.ds(0, tm * ROW_TILES)], ybuf.at[1], sem)
    m = meta_ref[...]
    moe = m[:, 4:5] * _from_token_tiles(ybuf.at[0]) + m[:, 5:6] * _from_token_tiles(ybuf.at[1])
    h2 = h1_ref[...] + mod_ref[5:6, :] * moe
    o_ref[...] = _rms(h2) * fn_ref[...]


def _final(d0, d1, h1, meta, mod3, final_norm, ys):
    tm = TM_PROJ
    per_b = SEQ // tm
    return pl.pallas_call(
        _final_kernel,
        grid_spec=pltpu.PrefetchScalarGridSpec(
            num_scalar_prefetch=2,
            grid=(N_TOK // tm,),
            in_specs=[pl.BlockSpec((tm, D_MODEL), lambda i, *_: (i, 0)),
                      pl.BlockSpec((tm, LANES), lambda i, *_: (i, 0)),
                      pl.BlockSpec((None, 6, D_MODEL), lambda i, *_: (i // per_b, 0, 0)),
                      pl.BlockSpec((1, D_MODEL), lambda i, *_: (0, 0)),
                      pl.BlockSpec(memory_space=pl.ANY)],
            out_specs=pl.BlockSpec((tm, D_MODEL), lambda i, *_: (i, 0)),
            scratch_shapes=[pltpu.VMEM((2, tm * ROW_TILES, LANES), F32), pltpu.SemaphoreType.DMA(())]),
        out_shape=jax.ShapeDtypeStruct((N_TOK, D_MODEL), F32),
        compiler_params=_params("arbitrary"),
        name="final",
    )(d0, d1, h1, meta, mod3, final_norm, ys)


def _slot_layout(counts):
    cnt = counts[0, N_GROUPS:N_GROUPS + N_EXPERTS].astype(jnp.int32)
    tile_end = jnp.cumsum((cnt + TE - 1) // TE)
    seg = jnp.concatenate([jnp.zeros((1,), jnp.int32), tile_end * TE])
    off_row = jnp.zeros((1, LANES), F32).at[0, N_GROUPS:N_GROUPS + N_EXPERTS].set(seg[:-1].astype(F32))
    tile_ids = jnp.arange(N_TILES, dtype=jnp.int32)
    tile_expert = jnp.sum((tile_end[None, :] <= tile_ids[:, None]).astype(jnp.int32), axis=1)
    tile_expert = jnp.minimum(tile_expert, N_EXPERTS - 1)
    return cnt, seg, off_row, tile_expert, tile_end[-1:]


def _rope_tables(dim):
    pos = jnp.arange(SEQ, dtype=F32)
    inv = ROPE_THETA ** (-jnp.arange(0, dim, 2, dtype=F32) / dim)
    ang = pos[:, None] * inv[None, :]
    return jnp.cos(ang), jnp.sin(ang)


def _decay_tables():
    c = RET_CHUNK
    log_gamma = jnp.log1p(-jnp.exp2(-5.0 - jnp.arange(RET_HEADS, dtype=F32)))
    idx = jnp.arange(c, dtype=F32)
    rel = idx[:, None] - idx[None, :]
    dec = jnp.where(rel[None] >= 0, jnp.exp(log_gamma[:, None, None] * jnp.maximum(rel, 0.0)[None]), 0.0)
    xi = jnp.exp(log_gamma[:, None] * (idx[None, :] + 1.0))[:, :, None]
    zeta = jnp.exp(log_gamma[:, None] * (c - 1.0 - idx[None, :]))[:, :, None]
    cd = jnp.exp(log_gamma * c)[:, None, None]
    return dec, xi, zeta, cd


def _rotate_half_cols(w):
    half = w.shape[-1] // 2
    return jnp.concatenate([-w[..., half:], w[..., :half]], axis=-1)


def kernel(x, c, w_ada, b_ada, norm1, norm2, w_in, w_ret_o, q_norm, kv_norm, w_uq, w_ukv, w_mla_o, w_out,
           w_grp, b_grp, w_exp, b_exp, w1, w3, w2, final_norm):
    assert x.shape == (BATCH, SEQ, D_MODEL) and w_ada.shape[0] == 1
    x2 = x.reshape(N_TOK, D_MODEL)

    o_lat = 4 * RET_W
    o_pe = o_lat + MLA_Q_LORA + MLA_KV_LORA
    o_gate = o_pe + MLA_ROPE
    wi = w_in[0]
    col = jnp.arange(o_lat)
    k_scale = jnp.where((col >= RET_W) & (col < 2 * RET_W), RET_DK ** -0.5, 1.0).astype(F32)
    w_ret = (wi[:, :o_lat] * k_scale[None, :]).astype(BF16)
    w_lat = jnp.concatenate([wi[:, o_lat:o_gate], _rotate_half_cols(wi[:, o_pe:o_gate])], axis=1).astype(BF16)
    w_gate = wi[:, o_gate:].astype(BF16)
    wq = w_uq[0].reshape(MLA_Q_LORA, MLA_HEADS, MLA_QK)
    wq = jnp.concatenate([wq, _rotate_half_cols(wq[..., MLA_NOPE:])], axis=-1)
    wq = wq.transpose(1, 0, 2).astype(BF16)
    wkv = w_ukv[0].reshape(MLA_KV_LORA, MLA_HEADS, MLA_NOPE + MLA_V).transpose(1, 0, 2).astype(BF16)
    pad = LANES - N_GROUPS - N_EXPERTS
    w_rt = jnp.concatenate([w_grp[0], w_exp[0], jnp.zeros((D_MODEL, pad), F32)], axis=1)
    b_rt = jnp.concatenate([b_grp[0], b_exp[0], jnp.zeros((pad,), F32)])[None, :]

    ret_cos, ret_sin = _rope_tables(RET_DK)
    mla_cos, mla_sin = _rope_tables(MLA_ROPE)
    mla_cos = jnp.concatenate([mla_cos, mla_cos], axis=-1)
    mla_sin = jnp.concatenate([mla_sin, mla_sin], axis=-1)
    dec, xi, zeta, cd = _decay_tables()

    mod3 = _ada(c, w_ada[0], b_ada[0]).reshape(BATCH, 6, D_MODEL)
    ret, lat, gates = _inproj(x2, mod3, norm1, ret_cos, ret_sin, w_ret, w_lat, w_gate)
    y_ret = _retention(ret.reshape(BATCH, SEQ, 4 * RET_W), dec, xi, zeta, cd)
    attn = _mla(lat.reshape(BATCH, SEQ, MLA_LAT_W), q_norm, kv_norm, wq, wkv, mla_cos, mla_sin)
    h1, u2t, meta, counts = _merge(y_ret.reshape(N_TOK, D_MODEL), attn.reshape(N_TOK, D_MODEL), gates, x2, mod3,
                                   norm2, w_ret_o[0].astype(BF16), w_mla_o[0].astype(BF16),
                                   w_out[0].astype(BF16), w_rt, b_rt)
    cnt, seg, off_row, tile_expert, n_valid = _slot_layout(counts)
    dst = _plan(meta, off_row)
    d0, d1 = dst[:, 0], dst[:, 1]
    inv = _invert(d0, d1, seg, cnt)
    e_shape = (N_EXPERTS, D_MODEL, D_EXPERT)
    ys = _experts(tile_expert, n_valid, inv, u2t, w1[0].reshape(e_shape), w3[0].reshape(e_shape),
                  w2[0].reshape(N_EXPERTS, D_EXPERT, D_MODEL))
    out = _final(d0, d1, h1, meta, mod3, final_norm.reshape(1, D_MODEL), ys)
    return out.reshape(BATCH, SEQ, D_MODEL)
```

```python
import jax
import jax.numpy as jnp
from jax import lax
from jax.experimental import pallas as pl
from jax.experimental.pallas import tpu as pltpu

D_MODEL = 1024
BATCH = 8
SEQ = 2048
N_TOK = BATCH * SEQ

RET_HEADS = 4
RET_DK = 256
RET_DV = 256
RET_CHUNK = 256
RET_BLK = 512
RET_W = RET_HEADS * RET_DK

MLA_HEADS = 8
MLA_NOPE = 128
MLA_ROPE = 64
MLA_V = 128
MLA_Q_LORA = 384
MLA_KV_LORA = 256
MLA_LAT_W = MLA_Q_LORA + MLA_KV_LORA + 2 * MLA_ROPE
MLA_QK = MLA_NOPE + MLA_ROPE
ROPE_THETA = 10000.0

N_GROUPS = 4
EXPERTS_PER_GROUP = 8
N_EXPERTS = N_GROUPS * EXPERTS_PER_GROUP
D_EXPERT = 256
EPS = 1e-6
LOG2_E = 1.4426950408889634

LANES = 128
SUBLANES = 8
ROW_TILES = D_MODEL // LANES
VMEM_LIMIT = 56 * 1024 * 1024

TM_PROJ = 512
TQ = 256
MLA_HPS = 2
TE = 512
TE_CHAINS = 2
TOP_K = 2
N_TILES = N_TOK * TOP_K // TE + N_EXPERTS
N_SLOTS = N_TILES * TE

F32 = jnp.float32
BF16 = jnp.bfloat16


def _sigmoid(x):
    return 1.0 / (1.0 + jnp.exp(-x))


def _rms(x):
    return x * lax.rsqrt(jnp.mean(x * x, axis=-1, keepdims=True) + EPS)


def _dot(a, b):
    return jnp.dot(a, b, preferred_element_type=F32)


def _dot_nt(a, b):
    return lax.dot_general(a, b, (((1,), (1,)), ((), ())), preferred_element_type=F32)


def _dot_tn(a, b):
    return lax.dot_general(a, b, (((0,), (0,)), ((), ())), preferred_element_type=F32)


def _params(*sem):
    return pltpu.CompilerParams(dimension_semantics=sem, vmem_limit_bytes=VMEM_LIMIT)


def _resident(shape):
    nd = len(shape)
    return pl.BlockSpec(shape, lambda *_: (0,) * nd, pipeline_mode=pl.Buffered(1))


def _ada_kernel(c_ref, w_ref, b_ref, o_ref):
    c = c_ref[...]
    act = (c * _sigmoid(c)).astype(BF16)
    o_ref[...] = _dot(act, w_ref[...].astype(BF16)) + b_ref[...]


def _ada(c, w_ada, b_ada):
    n = w_ada.shape[1]
    tn = D_MODEL
    return pl.pallas_call(
        _ada_kernel,
        grid=(n // tn,),
        in_specs=[pl.BlockSpec((BATCH, D_MODEL), lambda j: (0, 0)),
                  pl.BlockSpec((D_MODEL, tn), lambda j: (0, j)),
                  pl.BlockSpec((1, tn), lambda j: (0, j))],
        out_specs=pl.BlockSpec((BATCH, tn), lambda j: (0, j)),
        out_shape=jax.ShapeDtypeStruct((BATCH, n), F32),
        compiler_params=_params("arbitrary"),
        name="ada",
    )(c, w_ada, b_ada.reshape(1, n))


def _inproj_kernel(x_ref, mod_ref, n1_ref, cos_ref, sin_ref, wr_ref, wm_ref, wg_ref, ret_ref, lat_ref, gate_ref):
    y = _rms(x_ref[...]) * n1_ref[...]
    u = (y * (1.0 + mod_ref[1:2, :]) + mod_ref[0:1, :]).astype(BF16)
    cos, sin = cos_ref[...], sin_ref[...]
    half = RET_DK // 2
    for n in range(0, 2 * RET_W, RET_DK):
        p = _dot(u, wr_ref[:, n:n + RET_DK])
        x1, x2 = p[:, :half], p[:, half:]
        ret_ref[:, n:n + half] = (x1 * cos - x2 * sin).astype(BF16)
        ret_ref[:, n + half:n + RET_DK] = (x2 * cos + x1 * sin).astype(BF16)
    step = 512
    for n in range(2 * RET_W, 3 * RET_W, step):
        ret_ref[:, n:n + step] = _dot(u, wr_ref[:, n:n + step]).astype(BF16)
    for n in range(3 * RET_W, 4 * RET_W, step):
        p = _dot(u, wr_ref[:, n:n + step])
        ret_ref[:, n:n + step] = (p * _sigmoid(p)).astype(BF16)
    lat_ref[...] = _dot(u, wm_ref[...]).astype(BF16)
    for n in range(0, 2 * D_MODEL, step):
        gate_ref[:, n:n + step] = _sigmoid(_dot(u, wg_ref[:, n:n + step])).astype(BF16)


def _inproj(x2, mod3, norm1, cos, sin, w_ret, w_lat, w_gate):
    tm = TM_PROJ
    per_b = SEQ // tm
    rope_tab = pl.BlockSpec((tm, RET_DK // 2), lambda i: (i % per_b, 0))
    return pl.pallas_call(
        _inproj_kernel,
        grid=(N_TOK // tm,),
        in_specs=[pl.BlockSpec((tm, D_MODEL), lambda i: (i, 0)),
                  pl.BlockSpec((None, 6, D_MODEL), lambda i: (i // per_b, 0, 0)),
                  _resident((1, D_MODEL)), rope_tab, rope_tab,
                  _resident(w_ret.shape), _resident(w_lat.shape), _resident(w_gate.shape)],
        out_specs=[pl.BlockSpec((tm, 4 * RET_W), lambda i: (i, 0)),
                   pl.BlockSpec((tm, MLA_LAT_W), lambda i: (i, 0)),
                   pl.BlockSpec((tm, 2 * D_MODEL), lambda i: (i, 0))],
        out_shape=[jax.ShapeDtypeStruct((N_TOK, 4 * RET_W), BF16),
                   jax.ShapeDtypeStruct((N_TOK, MLA_LAT_W), BF16),
                   jax.ShapeDtypeStruct((N_TOK, 2 * D_MODEL), BF16)],
        compiler_params=_params("arbitrary"),
        name="inproj",
    )(x2, mod3, norm1, cos, sin, w_ret, w_lat, w_gate)


def _ret_kernel(q_ref, k_ref, v_ref, g_ref, dec_ref, xi_ref, zeta_ref, cd_ref, o_ref, state_ref):
    @pl.when(pl.program_id(1) == 0)
    def _():
        state_ref[...] = jnp.zeros_like(state_ref)

    heads = range(RET_HEADS)
    col = lambda h: slice(h * RET_DK, (h + 1) * RET_DK)
    for c in range(RET_BLK // RET_CHUNK):
        rows = slice(c * RET_CHUNK, (c + 1) * RET_CHUNK)
        scores = [(_dot_nt(q_ref[rows, col(h)], k_ref[rows, col(h)]) * dec_ref[h]).astype(BF16) for h in heads]
        carried = [xi_ref[h] * _dot(q_ref[rows, col(h)], state_ref[h].astype(BF16)) for h in heads]
        y = [_dot(scores[h], v_ref[rows, col(h)]) + carried[h] for h in heads]
        for h in heads:
            k_dec = (k_ref[rows, col(h)].astype(F32) * zeta_ref[h]).astype(BF16)
            state_ref[h] = state_ref[h] * cd_ref[h] + _dot_tn(k_dec, v_ref[rows, col(h)])
        yc = [y[h] - jnp.mean(y[h], axis=-1, keepdims=True) for h in heads]
        inv = [lax.rsqrt(jnp.mean(yc[h] * yc[h], axis=-1, keepdims=True) + EPS) for h in heads]
        for h in heads:
            o_ref[rows, col(h)] = (g_ref[rows, col(h)].astype(F32) * (yc[h] * inv[h])).astype(BF16)


def _retention(ret3, dec, xi, zeta, cd):
    blk = lambda part: pl.BlockSpec((None, RET_BLK, RET_W), lambda b, j: (b, j, part))
    whole = lambda a: pl.BlockSpec(a.shape, lambda b, j: (0,) * a.ndim)
    return pl.pallas_call(
        _ret_kernel,
        grid=(BATCH, SEQ // RET_BLK),
        in_specs=[blk(0), blk(1), blk(2), blk(3), whole(dec), whole(xi), whole(zeta), whole(cd)],
        out_specs=pl.BlockSpec((None, RET_BLK, RET_HEADS * RET_DV), lambda b, j: (b, j, 0)),
        out_shape=jax.ShapeDtypeStruct((BATCH, SEQ, RET_HEADS * RET_DV), BF16),
        scratch_shapes=[pltpu.VMEM((RET_HEADS, RET_DK, RET_DV), F32)],
        compiler_params=_params("arbitrary", "arbitrary"),
        name="retention",
    )(ret3, ret3, ret3, ret3, dec, xi, zeta, cd)


def _mla_kernel(lat_ref, qn_ref, kvn_ref, wq_ref, wkv_ref, cos_ref, sin_ref, o_ref,
                cq_s, ckv_s, kpe_s, q_s, k_s, v_s):
    h = pl.program_id(1)
    o_q, o_kv, o_pe, o_rot = 0, MLA_Q_LORA, MLA_Q_LORA + MLA_KV_LORA, MLA_Q_LORA + MLA_KV_LORA + MLA_ROPE
    cos, sin = cos_ref[...], sin_ref[...]

    @pl.when(h == 0)
    def _():
        cq_s[...] = (_rms(lat_ref[:, o_q:o_kv].astype(F32)) * qn_ref[...]).astype(BF16)
        ckv_s[...] = (_rms(lat_ref[:, o_kv:o_pe].astype(F32)) * kvn_ref[...]).astype(BF16)
        pe = lat_ref[:, o_pe:o_rot].astype(F32)
        rot = lat_ref[:, o_rot:o_rot + MLA_ROPE].astype(F32)
        kpe_s[...] = (pe * cos + rot * sin).astype(BF16)

    scale = (MLA_QK ** -0.5) * LOG2_E
    for g in range(MLA_HPS):
        qf = _dot(cq_s[...], wq_ref[g])
        q_s[g, :, :MLA_NOPE] = (qf[:, :MLA_NOPE] * scale).astype(BF16)
        q_pe = qf[:, MLA_NOPE:MLA_QK] * cos + qf[:, MLA_QK:] * sin
        q_s[g, :, MLA_NOPE:] = (q_pe * scale).astype(BF16)
        kvf = _dot(ckv_s[...], wkv_ref[g])
        k_s[g, :, :MLA_NOPE] = kvf[:, :MLA_NOPE].astype(BF16)
        k_s[g, :, MLA_NOPE:] = kpe_s[...]
        v_s[g, :, :MLA_V] = kvf[:, MLA_NOPE:].astype(BF16)
        v_s[g, :, MLA_V:] = jnp.ones((SEQ, MLA_V), BF16)

    causal = lax.broadcasted_iota(jnp.int32, (TQ, TQ), 0) >= lax.broadcasted_iota(jnp.int32, (TQ, TQ), 1)
    heads = range(MLA_HPS)
    n_blk = SEQ // TQ

    def scores(i):
        lo, hi = i * TQ, (i + 1) * TQ
        diag = [jnp.where(causal, _dot_nt(q_s[g, lo:hi, :], k_s[g, lo:hi, :]), -jnp.inf) for g in heads]
        past = [_dot_nt(q_s[g, lo:hi, :], k_s[g, :lo, :]) if i > 0 else None for g in heads]
        return diag, past

    pending = scores(0)
    for i in range(n_blk):
        lo, hi = i * TQ, (i + 1) * TQ
        diag, past = pending
        if i + 1 < n_blk:
            pending = scores(i + 1)
        m = [jnp.max(diag[g], axis=-1, keepdims=True) for g in heads]
        if i > 0:
            m = [jnp.maximum(m[g], jnp.max(past[g], axis=-1, keepdims=True)) for g in heads]
        acc = [_dot(jnp.exp2(diag[g] - m[g]).astype(BF16), v_s[g, lo:hi, :]) for g in heads]
        if i > 0:
            acc = [acc[g] + _dot(jnp.exp2(past[g] - m[g]).astype(BF16), v_s[g, :lo, :]) for g in heads]
        for g in heads:
            o_ref[lo:hi, g * MLA_V:(g + 1) * MLA_V] = (acc[g][:, :MLA_V] / acc[g][:, MLA_V:]).astype(BF16)


def _mla(lat3, q_norm, kv_norm, wq, wkv, cos, sin):
    hps = MLA_HPS
    return pl.pallas_call(
        _mla_kernel,
        grid=(BATCH, MLA_HEADS // hps),
        in_specs=[pl.BlockSpec((None, SEQ, MLA_LAT_W), lambda b, h: (b, 0, 0)),
                  pl.BlockSpec((1, MLA_Q_LORA), lambda b, h: (0, 0)),
                  pl.BlockSpec((1, MLA_KV_LORA), lambda b, h: (0, 0)),
                  pl.BlockSpec((hps, MLA_Q_LORA, MLA_QK + MLA_ROPE), lambda b, h: (h, 0, 0)),
                  pl.BlockSpec((hps, MLA_KV_LORA, MLA_NOPE + MLA_V), lambda b, h: (h, 0, 0)),
                  pl.BlockSpec((SEQ, MLA_ROPE), lambda b, h: (0, 0)),
                  pl.BlockSpec((SEQ, MLA_ROPE), lambda b, h: (0, 0))],
        out_specs=pl.BlockSpec((None, SEQ, hps * MLA_V), lambda b, h: (b, 0, h)),
        out_shape=jax.ShapeDtypeStruct((BATCH, SEQ, MLA_HEADS * MLA_V), BF16),
        scratch_shapes=[pltpu.VMEM((SEQ, MLA_Q_LORA), BF16),
                        pltpu.VMEM((SEQ, MLA_KV_LORA), BF16),
                        pltpu.VMEM((SEQ, MLA_ROPE), BF16),
                        pltpu.VMEM((hps, SEQ, MLA_QK), BF16),
                        pltpu.VMEM((hps, SEQ, MLA_QK), BF16),
                        pltpu.VMEM((hps, SEQ, 2 * MLA_V), BF16)],
        compiler_params=_params("arbitrary", "arbitrary"),
        name="mla",
    )(lat3, q_norm, kv_norm, wq, wkv, cos, sin)


def _route(logits):
    lane = lax.broadcasted_iota(jnp.int32, logits.shape, 1)
    neg = -jnp.inf
    gl = jnp.where(lane < N_GROUPS, logits, neg)
    gmax = jnp.max(gl, axis=-1, keepdims=True)
    gsel = jnp.min(jnp.where(gl == gmax, lane, LANES), axis=-1, keepdims=True)
    p_grp = 1.0 / jnp.sum(jnp.exp(gl - gmax), axis=-1, keepdims=True)
    e_lane = lane - N_GROUPS
    in_grp = (e_lane >= 0) & (e_lane < N_EXPERTS) & ((e_lane >> 3) == gsel)
    el = jnp.where(in_grp, logits, neg)
    v0 = jnp.max(el, axis=-1, keepdims=True)
    i0 = jnp.min(jnp.where(el == v0, lane, LANES), axis=-1, keepdims=True)
    el1 = jnp.where(lane == i0, neg, el)
    v1 = jnp.max(el1, axis=-1, keepdims=True)
    i1 = jnp.min(jnp.where(el1 == v1, lane, LANES), axis=-1, keepdims=True)
    t = jnp.exp(v1 - v0)
    w0 = p_grp / (1.0 + t)
    w1 = p_grp * t / (1.0 + t)
    return i0, i1, w0, w1


def _lane_pack(shape, cols):
    lane = lax.broadcasted_iota(jnp.int32, shape, 1)
    out = jnp.zeros(shape, F32)
    for k, col in enumerate(cols):
        out = jnp.where(lane == k, col, out)
    return out


def _to_token_tiles(ref, val):
    n = val.shape[0]
    for s in range(ROW_TILES):
        ref[pl.ds(s, n, stride=ROW_TILES), :] = val[:, s * LANES:(s + 1) * LANES]


def _from_token_tiles(ref):
    n = ref.shape[0] // ROW_TILES
    return jnp.concatenate([ref[pl.ds(s, n, stride=ROW_TILES), :] for s in range(ROW_TILES)], axis=-1)


def _token_rows(ref, t):
    return ref.at[pl.ds(pl.multiple_of(t * ROW_TILES, ROW_TILES), ROW_TILES)]


def _merge_kernel(yr_ref, at_ref, gr_ref, gm_ref, x_ref, mod_ref, n2_ref, wro_ref, wmo_ref, wo_ref,
                  wrt_ref, brt_ref, h1_ref, u2_ref, meta_ref, cnt_ref, carry_ref):
    tm = x_ref.shape[0]

    @pl.when(pl.program_id(0) == 0)
    def _():
        carry_ref[...] = jnp.zeros_like(carry_ref)

    y_ret = _dot(yr_ref[...], wro_ref[...])
    y_mla = _dot(at_ref[...], wmo_ref[...])
    merged = gr_ref[...].astype(F32) * y_ret + gm_ref[...].astype(F32) * y_mla
    o = _dot(merged.astype(BF16), wo_ref[...])
    h1 = x_ref[...] + mod_ref[2:3, :] * o
    h1_ref[...] = h1
    u2 = _rms(h1) * n2_ref[...] * (1.0 + mod_ref[4:5, :]) + mod_ref[3:4, :]
    _to_token_tiles(u2_ref, u2)
    u_hi = u2.astype(BF16)
    u_lo = (u2 - u_hi.astype(F32)).astype(BF16)
    w = wrt_ref[...]
    w_hi = w.astype(BF16)
    w_lo = (w - w_hi.astype(F32)).astype(BF16)
    logits = _dot(u_hi, w_hi) + _dot(u_lo, w_hi) + _dot(u_hi, w_lo) + brt_ref[...]
    i0, i1, w0, w1 = _route(logits)
    lane = lax.broadcasted_iota(jnp.int32, (tm, LANES), 1)
    m0, m1 = lane == i0, lane == i1
    member = jnp.where(m0 | m1, 1.0, 0.0)
    tri = jnp.where(lax.broadcasted_iota(jnp.int32, (tm, tm), 0) > lax.broadcasted_iota(jnp.int32, (tm, tm), 1),
                    1.0, 0.0).astype(BF16)
    prefix = _dot(tri, member.astype(BF16)) + carry_ref[0:1, :]
    rank0 = jnp.sum(jnp.where(m0, prefix, 0.0), axis=-1, keepdims=True)
    rank1 = jnp.sum(jnp.where(m1, prefix, 0.0), axis=-1, keepdims=True)
    carry_ref[...] = carry_ref[...] + jnp.sum(member, axis=0, keepdims=True)
    cnt_ref[...] = carry_ref[...]
    meta_ref[...] = _lane_pack((tm, LANES), [i0.astype(F32), i1.astype(F32), rank0, rank1, w0, w1])


def _merge(y_ret, attn, gates, x2, mod3, norm2, w_ret_o, w_mla_o, w_out, w_rt, b_rt):
    tm = TM_PROJ
    per_b = SEQ // tm
    row = lambda j: pl.BlockSpec((tm, D_MODEL), lambda i: (i, j))
    return pl.pallas_call(
        _merge_kernel,
        grid=(N_TOK // tm,),
        in_specs=[row(0), row(0), row(0), row(1), row(0),
                  pl.BlockSpec((None, 6, D_MODEL), lambda i: (i // per_b, 0, 0)),
                  _resident((1, D_MODEL)),
                  _resident(w_ret_o.shape), _resident(w_mla_o.shape), _resident(w_out.shape),
                  _resident(w_rt.shape), _resident(b_rt.shape)],
        out_specs=[row(0),
                   pl.BlockSpec((tm * ROW_TILES, LANES), lambda i: (i, 0)),
                   pl.BlockSpec((tm, LANES), lambda i: (i, 0)),
                   pl.BlockSpec((SUBLANES, LANES), lambda i: (0, 0))],
        out_shape=[jax.ShapeDtypeStruct((N_TOK, D_MODEL), F32),
                   jax.ShapeDtypeStruct((N_TOK * ROW_TILES, LANES), F32),
                   jax.ShapeDtypeStruct((N_TOK, LANES), F32),
                   jax.ShapeDtypeStruct((SUBLANES, LANES), F32)],
        scratch_shapes=[pltpu.VMEM((SUBLANES, LANES), F32)],
        compiler_params=_params("arbitrary"),
        name="merge",
    )(y_ret, attn, gates, gates, x2, mod3, norm2, w_ret_o, w_mla_o, w_out, w_rt, b_rt)


def _plan_kernel(meta_ref, off_ref, dst_ref):
    m = meta_ref[...]
    lane = lax.broadcasted_iota(jnp.int32, m.shape, 1)
    off = off_ref[...]
    i0 = m[:, 0:1].astype(jnp.int32)
    i1 = m[:, 1:2].astype(jnp.int32)
    d0 = jnp.sum(jnp.where(lane == i0, off, 0.0), axis=-1, keepdims=True) + m[:, 2:3]
    d1 = jnp.sum(jnp.where(lane == i1, off, 0.0), axis=-1, keepdims=True) + m[:, 3:4]
    dst_ref[...] = _lane_pack(m.shape, [d0, d1]).astype(jnp.int32)


def _plan(meta, off_row):
    tm = 2048
    return pl.pallas_call(
        _plan_kernel,
        grid=(N_TOK // tm,),
        in_specs=[pl.BlockSpec((tm, LANES), lambda i: (i, 0)),
                  pl.BlockSpec((1, LANES), lambda i: (0, 0))],
        out_specs=pl.BlockSpec((tm, LANES), lambda i: (i, 0)),
        out_shape=jax.ShapeDtypeStruct((N_TOK, LANES), jnp.int32),
        compiler_params=_params("arbitrary"),
        name="plan",
    )(meta, off_row)


def _row_copy_wait(src_like, dst_like, sem):
    pltpu.make_async_copy(src_like, dst_like, sem).wait()


def _dispatch_kernel(d0_ref, d1_ref, seg_ref, cnt_ref, u_ref, xs_ref, zero_ref, sem, zsem):
    i = pl.program_id(0)
    tm = u_ref.shape[0] // ROW_TILES
    tile_rows = TE * ROW_TILES

    def slot_tile(j):
        return xs_ref.at[pl.ds(pl.multiple_of(j * tile_rows, tile_rows), tile_rows)]

    @pl.when(i == 0)
    def _():
        zero_ref[...] = jnp.zeros_like(zero_ref)

        def pad(e):
            first = seg_ref[e] + cnt_ref[e]
            rows = (seg_ref[e + 1] - first) * ROW_TILES
            start = pl.multiple_of(first * ROW_TILES, ROW_TILES)
            return rows > 0, pltpu.make_async_copy(zero_ref.at[pl.ds(0, rows)], xs_ref.at[pl.ds(start, rows)], zsem)

        for e in range(N_EXPERTS):
            nonempty, copy = pad(e)
            pl.when(nonempty)(copy.start)
        for e in range(N_EXPERTS):
            nonempty, copy = pad(e)
            pl.when(nonempty)(copy.wait)

        def unused(j):
            return pltpu.make_async_copy(zero_ref, slot_tile(j), zsem)

        first_unused = seg_ref[N_EXPERTS] // TE
        lax.fori_loop(first_unused, N_TILES, lambda j, c: (unused(j).start(), c)[1], 0)
        lax.fori_loop(first_unused, N_TILES, lambda j, c: (unused(j).wait(), c)[1], 0)

    base = i * tm

    def body(r, carry):
        src = _token_rows(u_ref, r)
        pltpu.make_async_copy(src, _token_rows(xs_ref, d0_ref[base + r]), sem).start(priority=0)
        pltpu.make_async_copy(src, _token_rows(xs_ref, d1_ref[base + r]), sem).start(priority=1)
        return carry

    lax.fori_loop(0, tm, body, 0, unroll=8)
    _row_copy_wait(u_ref, xs_ref.at[pl.ds(0, tm * ROW_TILES)], sem)
    _row_copy_wait(u_ref, xs_ref.at[pl.ds(0, tm * ROW_TILES)], sem)


def _dispatch(d0, d1, seg, cnt, u2t):
    tm = TM_PROJ
    return pl.pallas_call(
        _dispatch_kernel,
        grid_spec=pltpu.PrefetchScalarGridSpec(
            num_scalar_prefetch=4,
            grid=(N_TOK // tm,),
            in_specs=[pl.BlockSpec((tm * ROW_TILES, LANES), lambda i, *_: (i, 0))],
            out_specs=pl.BlockSpec(memory_space=pl.ANY),
            scratch_shapes=[pltpu.VMEM((TE * ROW_TILES, LANES), F32),
                            pltpu.SemaphoreType.DMA(()), pltpu.SemaphoreType.DMA(())]),
        out_shape=jax.ShapeDtypeStruct((N_SLOTS * ROW_TILES, LANES), F32),
        compiler_params=_params("arbitrary"),
        name="dispatch",
    )(d0, d1, seg, cnt, u2t)


def _expert_kernel(te_ref, nv_ref, x_ref, w1_ref, w3_ref, w2_ref, y_ref, w1_s, w3_s, w2_s):
    j = pl.program_id(0)

    @pl.when(j < nv_ref[0])
    def _():
        @pl.when((j == 0) | (te_ref[j] != te_ref[jnp.maximum(j - 1, 0)]))
        def _():
            w1_s[...] = w1_ref[...].astype(BF16)
            w3_s[...] = w3_ref[...].astype(BF16)
            w2_s[...] = w2_ref[...].astype(BF16)

        sub = TE // TE_CHAINS * ROW_TILES
        part = lambda ref, c: ref.at[pl.ds(c * sub, sub)]
        chains = range(TE_CHAINS)
        x = [_from_token_tiles(part(x_ref, c)).astype(BF16) for c in chains]
        a = [_dot(x[c], w1_s[...]) for c in chains]
        b = [_dot(x[c], w3_s[...]) for c in chains]
        hid = [(a[c] * _sigmoid(a[c]) * b[c]).astype(BF16) for c in chains]
        for c in chains:
            _to_token_tiles(part(y_ref, c), _dot(hid[c], w2_s[...]))

    @pl.when(j >= nv_ref[0])
    def _():
        y_ref[...] = jnp.zeros_like(y_ref)


def _experts(tile_expert, n_valid, xs, w1, w3, w2):
    tile = lambda j, te, nv: jnp.minimum(j, nv[0] - 1)
    wspec = lambda shape: pl.BlockSpec((None,) + shape, lambda j, te, nv: (te[tile(j, te, nv)], 0, 0))
    slots = pl.BlockSpec((TE * ROW_TILES, LANES), lambda j, te, nv: (tile(j, te, nv), 0))
    return pl.pallas_call(
        _expert_kernel,
        grid_spec=pltpu.PrefetchScalarGridSpec(
            num_scalar_prefetch=2,
            grid=(N_TILES,),
            in_specs=[slots, wspec((D_MODEL, D_EXPERT)), wspec((D_MODEL, D_EXPERT)), wspec((D_EXPERT, D_MODEL))],
            out_specs=pl.BlockSpec((TE * ROW_TILES, LANES), lambda j, te, nv: (j, 0)),
            scratch_shapes=[pltpu.VMEM((D_MODEL, D_EXPERT), BF16), pltpu.VMEM((D_MODEL, D_EXPERT), BF16),
                            pltpu.VMEM((D_EXPERT, D_MODEL), BF16)]),
        out_shape=jax.ShapeDtypeStruct((N_SLOTS * ROW_TILES, LANES), F32),
        compiler_params=_params("arbitrary"),
        name="experts",
    )(tile_expert, n_valid, xs, w1, w3, w2)


def _final_kernel(d0_ref, d1_ref, h1_ref, meta_ref, mod_ref, fn_ref, ys_ref, o_ref, ybuf, sem):
    i = pl.program_id(0)
    tm = h1_ref.shape[0]
    base = i * tm

    def body(r, carry):
        pltpu.make_async_copy(_token_rows(ys_ref, d0_ref[base + r]), _token_rows(ybuf.at[0], r), sem).start(priority=0)
        pltpu.make_async_copy(_token_rows(ys_ref, d1_ref[base + r]), _token_rows(ybuf.at[1], r), sem).start(priority=1)
        return carry

    lax.fori_loop(0, tm, body, 0, unroll=8)
    _row_copy_wait(ys_ref.at[pl.ds(0, tm * ROW_TILES)], ybuf.at[0], sem)
    _row_copy_wait(ys_ref.at[pl.ds(0, tm * ROW_TILES)], ybuf.at[1], sem)
    m = meta_ref[...]
    moe = m[:, 4:5] * _from_token_tiles(ybuf.at[0]) + m[:, 5:6] * _from_token_tiles(ybuf.at[1])
    h2 = h1_ref[...] + mod_ref[5:6, :] * moe
    o_ref[...] = _rms(h2) * fn_ref[...]


def _final(d0, d1, h1, meta, mod3, final_norm, ys):
    tm = TM_PROJ
    per_b = SEQ // tm
    return pl.pallas_call(
        _final_kernel,
        grid_spec=pltpu.PrefetchScalarGridSpec(
            num_scalar_prefetch=2,
            grid=(N_TOK // tm,),
            in_specs=[pl.BlockSpec((tm, D_MODEL), lambda i, *_: (i, 0)),
                      pl.BlockSpec((tm, LANES), lambda i, *_: (i, 0)),
                      pl.BlockSpec((None, 6, D_MODEL), lambda i, *_: (i // per_b, 0, 0)),
                      pl.BlockSpec((1, D_MODEL), lambda i, *_: (0, 0)),
                      pl.BlockSpec(memory_space=pl.ANY)],
            out_specs=pl.BlockSpec((tm, D_MODEL), lambda i, *_: (i, 0)),
            scratch_shapes=[pltpu.VMEM((2, tm * ROW_TILES, LANES), F32), pltpu.SemaphoreType.DMA(())]),
        out_shape=jax.ShapeDtypeStruct((N_TOK, D_MODEL), F32),
        compiler_params=_params("arbitrary"),
        name="final",
    )(d0, d1, h1, meta, mod3, final_norm, ys)


def _slot_layout(counts):
    cnt = counts[0, N_GROUPS:N_GROUPS + N_EXPERTS].astype(jnp.int32)
    tile_end = jnp.cumsum((cnt + TE - 1) // TE)
    seg = jnp.concatenate([jnp.zeros((1,), jnp.int32), tile_end * TE])
    off_row = jnp.zeros((1, LANES), F32).at[0, N_GROUPS:N_GROUPS + N_EXPERTS].set(seg[:-1].astype(F32))
    tile_ids = jnp.arange(N_TILES, dtype=jnp.int32)
    tile_expert = jnp.sum((tile_end[None, :] <= tile_ids[:, None]).astype(jnp.int32), axis=1)
    tile_expert = jnp.minimum(tile_expert, N_EXPERTS - 1)
    return cnt, seg, off_row, tile_expert, tile_end[-1:]


def _rope_tables(dim):
    pos = jnp.arange(SEQ, dtype=F32)
    inv = ROPE_THETA ** (-jnp.arange(0, dim, 2, dtype=F32) / dim)
    ang = pos[:, None] * inv[None, :]
    return jnp.cos(ang), jnp.sin(ang)


def _decay_tables():
    c = RET_CHUNK
    log_gamma = jnp.log1p(-jnp.exp2(-5.0 - jnp.arange(RET_HEADS, dtype=F32)))
    idx = jnp.arange(c, dtype=F32)
    rel = idx[:, None] - idx[None, :]
    dec = jnp.where(rel[None] >= 0, jnp.exp(log_gamma[:, None, None] * jnp.maximum(rel, 0.0)[None]), 0.0)
    xi = jnp.exp(log_gamma[:, None] * (idx[None, :] + 1.0))[:, :, None]
    zeta = jnp.exp(log_gamma[:, None] * (c - 1.0 - idx[None, :]))[:, :, None]
    cd = jnp.exp(log_gamma * c)[:, None, None]
    return dec, xi, zeta, cd


def _rotate_half_cols(w):
    half = w.shape[-1] // 2
    return jnp.concatenate([-w[..., half:], w[..., :half]], axis=-1)


def kernel(x, c, w_ada, b_ada, norm1, norm2, w_in, w_ret_o, q_norm, kv_norm, w_uq, w_ukv, w_mla_o, w_out,
           w_grp, b_grp, w_exp, b_exp, w1, w3, w2, final_norm):
    assert x.shape == (BATCH, SEQ, D_MODEL) and w_ada.shape[0] == 1
    x2 = x.reshape(N_TOK, D_MODEL)

    o_lat = 4 * RET_W
    o_pe = o_lat + MLA_Q_LORA + MLA_KV_LORA
    o_gate = o_pe + MLA_ROPE
    wi = w_in[0]
    col = jnp.arange(o_lat)
    k_scale = jnp.where((col >= RET_W) & (col < 2 * RET_W), RET_DK ** -0.5, 1.0).astype(F32)
    w_ret = (wi[:, :o_lat] * k_scale[None, :]).astype(BF16)
    w_lat = jnp.concatenate([wi[:, o_lat:o_gate], _rotate_half_cols(wi[:, o_pe:o_gate])], axis=1).astype(BF16)
    w_gate = wi[:, o_gate:].astype(BF16)
    wq = w_uq[0].reshape(MLA_Q_LORA, MLA_HEADS, MLA_QK)
    wq = jnp.concatenate([wq, _rotate_half_cols(wq[..., MLA_NOPE:])], axis=-1)
    wq = wq.transpose(1, 0, 2).astype(BF16)
    wkv = w_ukv[0].reshape(MLA_KV_LORA, MLA_HEADS, MLA_NOPE + MLA_V).transpose(1, 0, 2).astype(BF16)
    pad = LANES - N_GROUPS - N_EXPERTS
    w_rt = jnp.concatenate([w_grp[0], w_exp[0], jnp.zeros((D_MODEL, pad), F32)], axis=1)
    b_rt = jnp.concatenate([b_grp[0], b_exp[0], jnp.zeros((pad,), F32)])[None, :]

    ret_cos, ret_sin = _rope_tables(RET_DK)
    mla_cos, mla_sin = _rope_tables(MLA_ROPE)
    mla_cos = jnp.concatenate([mla_cos, mla_cos], axis=-1)
    mla_sin = jnp.concatenate([mla_sin, mla_sin], axis=-1)
    dec, xi, zeta, cd = _decay_tables()

    mod3 = _ada(c, w_ada[0], b_ada[0]).reshape(BATCH, 6, D_MODEL)
    ret, lat, gates = _inproj(x2, mod3, norm1, ret_cos, ret_sin, w_ret, w_lat, w_gate)
    y_ret = _retention(ret.reshape(BATCH, SEQ, 4 * RET_W), dec, xi, zeta, cd)
    attn = _mla(lat.reshape(BATCH, SEQ, MLA_LAT_W), q_norm, kv_norm, wq, wkv, mla_cos, mla_sin)
    h1, u2t, meta, counts = _merge(y_ret.reshape(N_TOK, D_MODEL), attn.reshape(N_TOK, D_MODEL), gates, x2, mod3,
                                   norm2, w_ret_o[0].astype(BF16), w_mla_o[0].astype(BF16),
                                   w_out[0].astype(BF16), w_rt, b_rt)
    cnt, seg, off_row, tile_expert, n_valid = _slot_layout(counts)
    dst = _plan(meta, off_row)
    d0, d1 = dst[:, 0], dst[:, 1]
    xs = _dispatch(d0, d1, seg, cnt, u2t)
    e_shape = (N_EXPERTS, D_MODEL, D_EXPERT)
    ys = _experts(tile_expert, n_valid, xs, w1[0].reshape(e_shape), w3[0].reshape(e_shape),
                  w2[0].reshape(N_EXPERTS, D_EXPERT, D_MODEL))
    out = _final(d0, d1, h1, meta, mod3, final_norm.reshape(1, D_MODEL), ys)
    return out.reshape(BATCH, SEQ, D_MODEL)
```

```python
import jax
import jax.numpy as jnp
from jax import lax
from jax.experimental import pallas as pl
from jax.experimental.pallas import tpu as pltpu

D_MODEL = 1024
BATCH = 8
SEQ = 2048
N_TOK = BATCH * SEQ

RET_HEADS = 4
RET_DK = 256
RET_DV = 256
RET_CHUNK = 256
RET_BLK = 512
RET_W = RET_HEADS * RET_DK

MLA_HEADS = 8
MLA_NOPE = 128
MLA_ROPE = 64
MLA_V = 128
MLA_Q_LORA = 384
MLA_KV_LORA = 256
MLA_LAT_W = MLA_Q_LORA + MLA_KV_LORA + 2 * MLA_ROPE
MLA_QK = MLA_NOPE + MLA_ROPE
ROPE_THETA = 10000.0

N_GROUPS = 4
EXPERTS_PER_GROUP = 8
N_EXPERTS = N_GROUPS * EXPERTS_PER_GROUP
D_EXPERT = 256
EPS = 1e-6
LOG2_E = 1.4426950408889634

LANES = 128
SUBLANES = 8
ROW_TILES = D_MODEL // LANES
VMEM_LIMIT = 56 * 1024 * 1024

TM_PROJ = 512
TQ = 256
MLA_HPS = 2
TE = 512
TE_CHAINS = 2
TOP_K = 2
N_TILES = N_TOK * TOP_K // TE + N_EXPERTS
N_SLOTS = N_TILES * TE

F32 = jnp.float32
BF16 = jnp.bfloat16


def _sigmoid(x):
    return 1.0 / (1.0 + jnp.exp(-x))


def _rms(x):
    return x * lax.rsqrt(jnp.mean(x * x, axis=-1, keepdims=True) + EPS)


def _dot(a, b):
    return jnp.dot(a, b, preferred_element_type=F32)


def _dot_nt(a, b):
    return lax.dot_general(a, b, (((1,), (1,)), ((), ())), preferred_element_type=F32)


def _dot_tn(a, b):
    return lax.dot_general(a, b, (((0,), (0,)), ((), ())), preferred_element_type=F32)


def _params(*sem):
    return pltpu.CompilerParams(dimension_semantics=sem, vmem_limit_bytes=VMEM_LIMIT)


def _resident(shape):
    nd = len(shape)
    return pl.BlockSpec(shape, lambda *_: (0,) * nd, pipeline_mode=pl.Buffered(1))


def _ada_kernel(c_ref, w_ref, b_ref, o_ref):
    c = c_ref[...]
    act = (c * _sigmoid(c)).astype(BF16)
    o_ref[...] = _dot(act, w_ref[...].astype(BF16)) + b_ref[...]


def _ada(c, w_ada, b_ada):
    n = w_ada.shape[1]
    tn = D_MODEL
    return pl.pallas_call(
        _ada_kernel,
        grid=(n // tn,),
        in_specs=[pl.BlockSpec((BATCH, D_MODEL), lambda j: (0, 0)),
                  pl.BlockSpec((D_MODEL, tn), lambda j: (0, j)),
                  pl.BlockSpec((1, tn), lambda j: (0, j))],
        out_specs=pl.BlockSpec((BATCH, tn), lambda j: (0, j)),
        out_shape=jax.ShapeDtypeStruct((BATCH, n), F32),
        compiler_params=_params("arbitrary"),
        name="ada",
    )(c, w_ada, b_ada.reshape(1, n))


def _inproj_kernel(x_ref, mod_ref, n1_ref, cos_ref, sin_ref, wr_ref, wm_ref, wg_ref, ret_ref, lat_ref, gate_ref):
    y = _rms(x_ref[...]) * n1_ref[...]
    u = (y * (1.0 + mod_ref[1:2, :]) + mod_ref[0:1, :]).astype(BF16)
    cos, sin = cos_ref[...], sin_ref[...]
    half = RET_DK // 2
    for n in range(0, 2 * RET_W, RET_DK):
        p = _dot(u, wr_ref[:, n:n + RET_DK])
        x1, x2 = p[:, :half], p[:, half:]
        ret_ref[:, n:n + half] = (x1 * cos - x2 * sin).astype(BF16)
        ret_ref[:, n + half:n + RET_DK] = (x2 * cos + x1 * sin).astype(BF16)
    step = 512
    for n in range(2 * RET_W, 3 * RET_W, step):
        ret_ref[:, n:n + step] = _dot(u, wr_ref[:, n:n + step]).astype(BF16)
    for n in range(3 * RET_W, 4 * RET_W, step):
        p = _dot(u, wr_ref[:, n:n + step])
        ret_ref[:, n:n + step] = (p * _sigmoid(p)).astype(BF16)
    lat_ref[...] = _dot(u, wm_ref[...]).astype(BF16)
    for n in range(0, 2 * D_MODEL, step):
        gate_ref[:, n:n + step] = _dot(u, wg_ref[:, n:n + step]).astype(BF16)


def _inproj(x2, mod3, norm1, cos, sin, w_ret, w_lat, w_gate):
    tm = TM_PROJ
    per_b = SEQ // tm
    rope_tab = pl.BlockSpec((tm, RET_DK // 2), lambda i: (i % per_b, 0))
    return pl.pallas_call(
        _inproj_kernel,
        grid=(N_TOK // tm,),
        in_specs=[pl.BlockSpec((tm, D_MODEL), lambda i: (i, 0)),
                  pl.BlockSpec((None, 6, D_MODEL), lambda i: (i // per_b, 0, 0)),
                  _resident((1, D_MODEL)), rope_tab, rope_tab,
                  _resident(w_ret.shape), _resident(w_lat.shape), _resident(w_gate.shape)],
        out_specs=[pl.BlockSpec((tm, 4 * RET_W), lambda i: (i, 0)),
                   pl.BlockSpec((tm, MLA_LAT_W), lambda i: (i, 0)),
                   pl.BlockSpec((tm, 2 * D_MODEL), lambda i: (i, 0))],
        out_shape=[jax.ShapeDtypeStruct((N_TOK, 4 * RET_W), BF16),
                   jax.ShapeDtypeStruct((N_TOK, MLA_LAT_W), BF16),
                   jax.ShapeDtypeStruct((N_TOK, 2 * D_MODEL), BF16)],
        compiler_params=_params("arbitrary"),
        name="inproj",
    )(x2, mod3, norm1, cos, sin, w_ret, w_lat, w_gate)


def _ret_kernel(q_ref, k_ref, v_ref, g_ref, dec_ref, xi_ref, zeta_ref, cd_ref, o_ref, state_ref):
    @pl.when(pl.program_id(1) == 0)
    def _():
        state_ref[...] = jnp.zeros_like(state_ref)

    heads = range(RET_HEADS)
    col = lambda h: slice(h * RET_DK, (h + 1) * RET_DK)
    for c in range(RET_BLK // RET_CHUNK):
        rows = slice(c * RET_CHUNK, (c + 1) * RET_CHUNK)
        scores = [(_dot_nt(q_ref[rows, col(h)], k_ref[rows, col(h)]) * dec_ref[h]).astype(BF16) for h in heads]
        carried = [xi_ref[h] * _dot(q_ref[rows, col(h)], state_ref[h].astype(BF16)) for h in heads]
        y = [_dot(scores[h], v_ref[rows, col(h)]) + carried[h] for h in heads]
        for h in heads:
            k_dec = (k_ref[rows, col(h)].astype(F32) * zeta_ref[h]).astype(BF16)
            state_ref[h] = state_ref[h] * cd_ref[h] + _dot_tn(k_dec, v_ref[rows, col(h)])
        yc = [y[h] - jnp.mean(y[h], axis=-1, keepdims=True) for h in heads]
        inv = [lax.rsqrt(jnp.mean(yc[h] * yc[h], axis=-1, keepdims=True) + EPS) for h in heads]
        for h in heads:
            o_ref[rows, col(h)] = (g_ref[rows, col(h)].astype(F32) * (yc[h] * inv[h])).astype(BF16)


def _retention(ret3, dec, xi, zeta, cd):
    blk = lambda part: pl.BlockSpec((None, RET_BLK, RET_W), lambda b, j: (b, j, part))
    whole = lambda a: pl.BlockSpec(a.shape, lambda b, j: (0,) * a.ndim)
    return pl.pallas_call(
        _ret_kernel,
        grid=(BATCH, SEQ // RET_BLK),
        in_specs=[blk(0), blk(1), blk(2), blk(3), whole(dec), whole(xi), whole(zeta), whole(cd)],
        out_specs=pl.BlockSpec((None, RET_BLK, RET_HEADS * RET_DV), lambda b, j: (b, j, 0)),
        out_shape=jax.ShapeDtypeStruct((BATCH, SEQ, RET_HEADS * RET_DV), BF16),
        scratch_shapes=[pltpu.VMEM((RET_HEADS, RET_DK, RET_DV), F32)],
        compiler_params=_params("arbitrary", "arbitrary"),
        name="retention",
    )(ret3, ret3, ret3, ret3, dec, xi, zeta, cd)


def _mla_kernel(lat_ref, qn_ref, kvn_ref, wq_ref, wkv_ref, cos_ref, sin_ref, o_ref,
                cq_s, ckv_s, kpe_s, q_s, k_s, v_s):
    h = pl.program_id(1)
    o_q, o_kv, o_pe, o_rot = 0, MLA_Q_LORA, MLA_Q_LORA + MLA_KV_LORA, MLA_Q_LORA + MLA_KV_LORA + MLA_ROPE
    cos, sin = cos_ref[...], sin_ref[...]

    @pl.when(h == 0)
    def _():
        cq_s[...] = (_rms(lat_ref[:, o_q:o_kv].astype(F32)) * qn_ref[...]).astype(BF16)
        ckv_s[...] = (_rms(lat_ref[:, o_kv:o_pe].astype(F32)) * kvn_ref[...]).astype(BF16)
        pe = lat_ref[:, o_pe:o_rot].astype(F32)
        rot = lat_ref[:, o_rot:o_rot + MLA_ROPE].astype(F32)
        kpe_s[...] = (pe * cos + rot * sin).astype(BF16)

    scale = (MLA_QK ** -0.5) * LOG2_E
    for g in range(MLA_HPS):
        qf = _dot(cq_s[...], wq_ref[g])
        q_s[g, :, :MLA_NOPE] = (qf[:, :MLA_NOPE] * scale).astype(BF16)
        q_pe = qf[:, MLA_NOPE:MLA_QK] * cos + qf[:, MLA_QK:] * sin
        q_s[g, :, MLA_NOPE:] = (q_pe * scale).astype(BF16)
        kvf = _dot(ckv_s[...], wkv_ref[g])
        k_s[g, :, :MLA_NOPE] = kvf[:, :MLA_NOPE].astype(BF16)
        k_s[g, :, MLA_NOPE:] = kpe_s[...]
        v_s[g, :, :MLA_V] = kvf[:, MLA_NOPE:].astype(BF16)
        v_s[g, :, MLA_V:] = jnp.ones((SEQ, MLA_V), BF16)

    causal = lax.broadcasted_iota(jnp.int32, (TQ, TQ), 0) >= lax.broadcasted_iota(jnp.int32, (TQ, TQ), 1)
    heads = range(MLA_HPS)
    n_blk = SEQ // TQ

    def scores(i):
        lo, hi = i * TQ, (i + 1) * TQ
        diag = [jnp.where(causal, _dot_nt(q_s[g, lo:hi, :], k_s[g, lo:hi, :]), -jnp.inf) for g in heads]
        past = [_dot_nt(q_s[g, lo:hi, :], k_s[g, :lo, :]) if i > 0 else None for g in heads]
        return diag, past

    pending = scores(0)
    for i in range(n_blk):
        lo, hi = i * TQ, (i + 1) * TQ
        diag, past = pending
        if i + 1 < n_blk:
            pending = scores(i + 1)
        m = [jnp.max(diag[g], axis=-1, keepdims=True) for g in heads]
        if i > 0:
            m = [jnp.maximum(m[g], jnp.max(past[g], axis=-1, keepdims=True)) for g in heads]
        acc = [_dot(jnp.exp2(diag[g] - m[g]).astype(BF16), v_s[g, lo:hi, :]) for g in heads]
        if i > 0:
            acc = [acc[g] + _dot(jnp.exp2(past[g] - m[g]).astype(BF16), v_s[g, :lo, :]) for g in heads]
        for g in heads:
            o_ref[lo:hi, g * MLA_V:(g + 1) * MLA_V] = (acc[g][:, :MLA_V] / acc[g][:, MLA_V:]).astype(BF16)


def _mla(lat3, q_norm, kv_norm, wq, wkv, cos, sin):
    hps = MLA_HPS
    return pl.pallas_call(
        _mla_kernel,
        grid=(BATCH, MLA_HEADS // hps),
        in_specs=[pl.BlockSpec((None, SEQ, MLA_LAT_W), lambda b, h: (b, 0, 0)),
                  pl.BlockSpec((1, MLA_Q_LORA), lambda b, h: (0, 0)),
                  pl.BlockSpec((1, MLA_KV_LORA), lambda b, h: (0, 0)),
                  pl.BlockSpec((hps, MLA_Q_LORA, MLA_QK + MLA_ROPE), lambda b, h: (h, 0, 0)),
                  pl.BlockSpec((hps, MLA_KV_LORA, MLA_NOPE + MLA_V), lambda b, h: (h, 0, 0)),
                  pl.BlockSpec((SEQ, MLA_ROPE), lambda b, h: (0, 0)),
                  pl.BlockSpec((SEQ, MLA_ROPE), lambda b, h: (0, 0))],
        out_specs=pl.BlockSpec((None, SEQ, hps * MLA_V), lambda b, h: (b, 0, h)),
        out_shape=jax.ShapeDtypeStruct((BATCH, SEQ, MLA_HEADS * MLA_V), BF16),
        scratch_shapes=[pltpu.VMEM((SEQ, MLA_Q_LORA), BF16),
                        pltpu.VMEM((SEQ, MLA_KV_LORA), BF16),
                        pltpu.VMEM((SEQ, MLA_ROPE), BF16),
                        pltpu.VMEM((hps, SEQ, MLA_QK), BF16),
                        pltpu.VMEM((hps, SEQ, MLA_QK), BF16),
                        pltpu.VMEM((hps, SEQ, 2 * MLA_V), BF16)],
        compiler_params=_params("arbitrary", "arbitrary"),
        name="mla",
    )(lat3, q_norm, kv_norm, wq, wkv, cos, sin)


def _route(logits):
    lane = lax.broadcasted_iota(jnp.int32, logits.shape, 1)
    neg = -jnp.inf
    gl = jnp.where(lane < N_GROUPS, logits, neg)
    gmax = jnp.max(gl, axis=-1, keepdims=True)
    gsel = jnp.min(jnp.where(gl == gmax, lane, LANES), axis=-1, keepdims=True)
    p_grp = 1.0 / jnp.sum(jnp.exp(gl - gmax), axis=-1, keepdims=True)
    e_lane = lane - N_GROUPS
    in_grp = (e_lane >= 0) & (e_lane < N_EXPERTS) & ((e_lane >> 3) == gsel)
    el = jnp.where(in_grp, logits, neg)
    v0 = jnp.max(el, axis=-1, keepdims=True)
    i0 = jnp.min(jnp.where(el == v0, lane, LANES), axis=-1, keepdims=True)
    el1 = jnp.where(lane == i0, neg, el)
    v1 = jnp.max(el1, axis=-1, keepdims=True)
    i1 = jnp.min(jnp.where(el1 == v1, lane, LANES), axis=-1, keepdims=True)
    t = jnp.exp(v1 - v0)
    w0 = p_grp / (1.0 + t)
    w1 = p_grp * t / (1.0 + t)
    return i0, i1, w0, w1


def _lane_pack(shape, cols):
    lane = lax.broadcasted_iota(jnp.int32, shape, 1)
    out = jnp.zeros(shape, F32)
    for k, col in enumerate(cols):
        out = jnp.where(lane == k, col, out)
    return out


def _to_token_tiles(ref, val):
    n = val.shape[0]
    for s in range(ROW_TILES):
        ref[pl.ds(s, n, stride=ROW_TILES), :] = val[:, s * LANES:(s + 1) * LANES]


def _from_token_tiles(ref):
    n = ref.shape[0] // ROW_TILES
    return jnp.concatenate([ref[pl.ds(s, n, stride=ROW_TILES), :] for s in range(ROW_TILES)], axis=-1)


def _token_rows(ref, t):
    return ref.at[pl.ds(pl.multiple_of(t * ROW_TILES, ROW_TILES), ROW_TILES)]


def _merge_kernel(yr_ref, at_ref, gr_ref, gm_ref, x_ref, mod_ref, n2_ref, wro_ref, wmo_ref, wo_ref,
                  wrt_ref, brt_ref, h1_ref, u2_ref, meta_ref, cnt_ref, carry_ref):
    tm = x_ref.shape[0]

    @pl.when(pl.program_id(0) == 0)
    def _():
        carry_ref[...] = jnp.zeros_like(carry_ref)

    y_ret = _dot(yr_ref[...], wro_ref[...])
    y_mla = _dot(at_ref[...], wmo_ref[...])
    merged = _sigmoid(gr_ref[...].astype(F32)) * y_ret + _sigmoid(gm_ref[...].astype(F32)) * y_mla
    o = _dot(merged.astype(BF16), wo_ref[...])
    h1 = x_ref[...] + mod_ref[2:3, :] * o
    h1_ref[...] = h1
    u2 = _rms(h1) * n2_ref[...] * (1.0 + mod_ref[4:5, :]) + mod_ref[3:4, :]
    _to_token_tiles(u2_ref, u2)
    u_hi = u2.astype(BF16)
    u_lo = (u2 - u_hi.astype(F32)).astype(BF16)
    w = wrt_ref[...]
    w_hi = w.astype(BF16)
    w_lo = (w - w_hi.astype(F32)).astype(BF16)
    logits = _dot(u_hi, w_hi) + _dot(u_lo, w_hi) + _dot(u_hi, w_lo) + brt_ref[...]
    i0, i1, w0, w1 = _route(logits)
    lane = lax.broadcasted_iota(jnp.int32, (tm, LANES), 1)
    m0, m1 = lane == i0, lane == i1
    member = jnp.where(m0 | m1, 1.0, 0.0)
    tri = jnp.where(lax.broadcasted_iota(jnp.int32, (tm, tm), 0) > lax.broadcasted_iota(jnp.int32, (tm, tm), 1),
                    1.0, 0.0).astype(BF16)
    prefix = _dot(tri, member.astype(BF16)) + carry_ref[0:1, :]
    rank0 = jnp.sum(jnp.where(m0, prefix, 0.0), axis=-1, keepdims=True)
    rank1 = jnp.sum(jnp.where(m1, prefix, 0.0), axis=-1, keepdims=True)
    carry_ref[...] = carry_ref[...] + jnp.sum(member, axis=0, keepdims=True)
    cnt_ref[...] = carry_ref[...]
    meta_ref[...] = _lane_pack((tm, LANES), [i0.astype(F32), i1.astype(F32), rank0, rank1, w0, w1])


def _merge(y_ret, attn, gates, x2, mod3, norm2, w_ret_o, w_mla_o, w_out, w_rt, b_rt):
    tm = TM_PROJ
    per_b = SEQ // tm
    row = lambda j: pl.BlockSpec((tm, D_MODEL), lambda i: (i, j))
    return pl.pallas_call(
        _merge_kernel,
        grid=(N_TOK // tm,),
        in_specs=[row(0), row(0), row(0), row(1), row(0),
                  pl.BlockSpec((None, 6, D_MODEL), lambda i: (i // per_b, 0, 0)),
                  _resident((1, D_MODEL)),
                  _resident(w_ret_o.shape), _resident(w_mla_o.shape), _resident(w_out.shape),
                  _resident(w_rt.shape), _resident(b_rt.shape)],
        out_specs=[row(0),
                   pl.BlockSpec((tm * ROW_TILES, LANES), lambda i: (i, 0)),
                   pl.BlockSpec((tm, LANES), lambda i: (i, 0)),
                   pl.BlockSpec((SUBLANES, LANES), lambda i: (0, 0))],
        out_shape=[jax.ShapeDtypeStruct((N_TOK, D_MODEL), F32),
                   jax.ShapeDtypeStruct((N_TOK * ROW_TILES, LANES), F32),
                   jax.ShapeDtypeStruct((N_TOK, LANES), F32),
                   jax.ShapeDtypeStruct((SUBLANES, LANES), F32)],
        scratch_shapes=[pltpu.VMEM((SUBLANES, LANES), F32)],
        compiler_params=_params("arbitrary"),
        name="merge",
    )(y_ret, attn, gates, gates, x2, mod3, norm2, w_ret_o, w_mla_o, w_out, w_rt, b_rt)


def _plan_kernel(meta_ref, off_ref, dst_ref):
    m = meta_ref[...]
    lane = lax.broadcasted_iota(jnp.int32, m.shape, 1)
    off = off_ref[...]
    i0 = m[:, 0:1].astype(jnp.int32)
    i1 = m[:, 1:2].astype(jnp.int32)
    d0 = jnp.sum(jnp.where(lane == i0, off, 0.0), axis=-1, keepdims=True) + m[:, 2:3]
    d1 = jnp.sum(jnp.where(lane == i1, off, 0.0), axis=-1, keepdims=True) + m[:, 3:4]
    dst_ref[...] = _lane_pack(m.shape, [d0, d1]).astype(jnp.int32)


def _plan(meta, off_row):
    tm = 2048
    return pl.pallas_call(
        _plan_kernel,
        grid=(N_TOK // tm,),
        in_specs=[pl.BlockSpec((tm, LANES), lambda i: (i, 0)),
                  pl.BlockSpec((1, LANES), lambda i: (0, 0))],
        out_specs=pl.BlockSpec((tm, LANES), lambda i: (i, 0)),
        out_shape=jax.ShapeDtypeStruct((N_TOK, LANES), jnp.int32),
        compiler_params=_params("arbitrary"),
        name="plan",
    )(meta, off_row)


def _row_copy_wait(src_like, dst_like, sem):
    pltpu.make_async_copy(src_like, dst_like, sem).wait()


def _dispatch_kernel(d0_ref, d1_ref, seg_ref, cnt_ref, u_ref, xs_ref, zero_ref, sem, zsem):
    i = pl.program_id(0)
    tm = u_ref.shape[0] // ROW_TILES
    tile_rows = TE * ROW_TILES

    def slot_tile(j):
        return xs_ref.at[pl.ds(pl.multiple_of(j * tile_rows, tile_rows), tile_rows)]

    @pl.when(i == 0)
    def _():
        zero_ref[...] = jnp.zeros_like(zero_ref)

        def pad(e):
            first = seg_ref[e] + cnt_ref[e]
            rows = (seg_ref[e + 1] - first) * ROW_TILES
            start = pl.multiple_of(first * ROW_TILES, ROW_TILES)
            return rows > 0, pltpu.make_async_copy(zero_ref.at[pl.ds(0, rows)], xs_ref.at[pl.ds(start, rows)], zsem)

        for e in range(N_EXPERTS):
            nonempty, copy = pad(e)
            pl.when(nonempty)(copy.start)
        for e in range(N_EXPERTS):
            nonempty, copy = pad(e)
            pl.when(nonempty)(copy.wait)

        def unused(j):
            return pltpu.make_async_copy(zero_ref, slot_tile(j), zsem)

        first_unused = seg_ref[N_EXPERTS] // TE
        lax.fori_loop(first_unused, N_TILES, lambda j, c: (unused(j).start(), c)[1], 0)
        lax.fori_loop(first_unused, N_TILES, lambda j, c: (unused(j).wait(), c)[1], 0)

    base = i * tm

    def body(r, carry):
        src = _token_rows(u_ref, r)
        pltpu.make_async_copy(src, _token_rows(xs_ref, d0_ref[base + r]), sem).start(priority=0)
        pltpu.make_async_copy(src, _token_rows(xs_ref, d1_ref[base + r]), sem).start(priority=1)
        return carry

    lax.fori_loop(0, tm, body, 0, unroll=8)
    _row_copy_wait(u_ref, xs_ref.at[pl.ds(0, tm * ROW_TILES)], sem)
    _row_copy_wait(u_ref, xs_ref.at[pl.ds(0, tm * ROW_TILES)], sem)


def _dispatch(d0, d1, seg, cnt, u2t):
    tm = TM_PROJ
    return pl.pallas_call(
        _dispatch_kernel,
        grid_spec=pltpu.PrefetchScalarGridSpec(
            num_scalar_prefetch=4,
            grid=(N_TOK // tm,),
            in_specs=[pl.BlockSpec((tm * ROW_TILES, LANES), lambda i, *_: (i, 0))],
            out_specs=pl.BlockSpec(memory_space=pl.ANY),
            scratch_shapes=[pltpu.VMEM((TE * ROW_TILES, LANES), F32),
                            pltpu.SemaphoreType.DMA(()), pltpu.SemaphoreType.DMA(())]),
        out_shape=jax.ShapeDtypeStruct((N_SLOTS * ROW_TILES, LANES), F32),
        compiler_params=_params("arbitrary"),
        name="dispatch",
    )(d0, d1, seg, cnt, u2t)


def _expert_kernel(te_ref, nv_ref, x_ref, w1_ref, w3_ref, w2_ref, y_ref, w1_s, w3_s, w2_s):
    j = pl.program_id(0)

    @pl.when(j < nv_ref[0])
    def _():
        @pl.when((j == 0) | (te_ref[j] != te_ref[jnp.maximum(j - 1, 0)]))
        def _():
            w1_s[...] = w1_ref[...].astype(BF16)
            w3_s[...] = w3_ref[...].astype(BF16)
            w2_s[...] = w2_ref[...].astype(BF16)

        sub = TE // TE_CHAINS * ROW_TILES
        part = lambda ref, c: ref.at[pl.ds(c * sub, sub)]
        chains = range(TE_CHAINS)
        x = [_from_token_tiles(part(x_ref, c)).astype(BF16) for c in chains]
        a = [_dot(x[c], w1_s[...]) for c in chains]
        b = [_dot(x[c], w3_s[...]) for c in chains]
        hid = [(a[c] * _sigmoid(a[c]) * b[c]).astype(BF16) for c in chains]
        for c in chains:
            _to_token_tiles(part(y_ref, c), _dot(hid[c], w2_s[...]))

    @pl.when(j >= nv_ref[0])
    def _():
        y_ref[...] = jnp.zeros_like(y_ref)


def _experts(tile_expert, n_valid, xs, w1, w3, w2):
    tile = lambda j, te, nv: jnp.minimum(j, nv[0] - 1)
    wspec = lambda shape: pl.BlockSpec((None,) + shape, lambda j, te, nv: (te[tile(j, te, nv)], 0, 0))
    slots = pl.BlockSpec((TE * ROW_TILES, LANES), lambda j, te, nv: (tile(j, te, nv), 0))
    return pl.pallas_call(
        _expert_kernel,
        grid_spec=pltpu.PrefetchScalarGridSpec(
            num_scalar_prefetch=2,
            grid=(N_TILES,),
            in_specs=[slots, wspec((D_MODEL, D_EXPERT)), wspec((D_MODEL, D_EXPERT)), wspec((D_EXPERT, D_MODEL))],
            out_specs=pl.BlockSpec((TE * ROW_TILES, LANES), lambda j, te, nv: (j, 0)),
            scratch_shapes=[pltpu.VMEM((D_MODEL, D_EXPERT), BF16), pltpu.VMEM((D_MODEL, D_EXPERT), BF16),
                            pltpu.VMEM((D_EXPERT, D_MODEL), BF16)]),
        out_shape=jax.ShapeDtypeStruct((N_SLOTS * ROW_TILES, LANES), F32),
        compiler_params=_params("arbitrary"),
        name="experts",
    )(tile_expert, n_valid, xs, w1, w3, w2)


def _final_kernel(d0_ref, d1_ref, h1_ref, meta_ref, mod_ref, fn_ref, ys_ref, o_ref, ybuf, sem):
    i = pl.program_id(0)
    tm = h1_ref.shape[0]

    def gather(t):
        buf, s = ybuf.at[t % 2], sem.at[t % 2]
        base = t * tm

        def body(r, carry):
            pltpu.make_async_copy(_token_rows(ys_ref, d0_ref[base + r]), _token_rows(buf.at[0], r), s).start(priority=0)
            pltpu.make_async_copy(_token_rows(ys_ref, d1_ref[base + r]), _token_rows(buf.at[1], r), s).start(priority=1)
            return carry

        lax.fori_loop(0, tm, body, 0, unroll=8)

    @pl.when(i == 0)
    def _():
        gather(0)

    @pl.when(i + 1 < pl.num_programs(0))
    def _():
        gather(i + 1)

    buf = ybuf.at[i % 2]
    for k in range(TOP_K):
        _row_copy_wait(ys_ref.at[pl.ds(0, tm * ROW_TILES)], buf.at[k], sem.at[i % 2])
    m = meta_ref[...]
    moe = m[:, 4:5] * _from_token_tiles(buf.at[0]) + m[:, 5:6] * _from_token_tiles(buf.at[1])
    h2 = h1_ref[...] + mod_ref[5:6, :] * moe
    o_ref[...] = _rms(h2) * fn_ref[...]


def _final(d0, d1, h1, meta, mod3, final_norm, ys):
    tm = TM_PROJ
    per_b = SEQ // tm
    return pl.pallas_call(
        _final_kernel,
        grid_spec=pltpu.PrefetchScalarGridSpec(
            num_scalar_prefetch=2,
            grid=(N_TOK // tm,),
            in_specs=[pl.BlockSpec((tm, D_MODEL), lambda i, *_: (i, 0)),
                      pl.BlockSpec((tm, LANES), lambda i, *_: (i, 0)),
                      pl.BlockSpec((None, 6, D_MODEL), lambda i, *_: (i // per_b, 0, 0)),
                      pl.BlockSpec((1, D_MODEL), lambda i, *_: (0, 0)),
                      pl.BlockSpec(memory_space=pl.ANY)],
            out_specs=pl.BlockSpec((tm, D_MODEL), lambda i, *_: (i, 0)),
            scratch_shapes=[pltpu.VMEM((2, TOP_K, tm * ROW_TILES, LANES), F32), pltpu.SemaphoreType.DMA((2,))]),
        out_shape=jax.ShapeDtypeStruct((N_TOK, D_MODEL), F32),
        compiler_params=_params("arbitrary"),
        name="final",
    )(d0, d1, h1, meta, mod3, final_norm, ys)


def _slot_layout(counts):
    cnt = counts[0, N_GROUPS:N_GROUPS + N_EXPERTS].astype(jnp.int32)
    tile_end = jnp.cumsum((cnt + TE - 1) // TE)
    seg = jnp.concatenate([jnp.zeros((1,), jnp.int32), tile_end * TE])
    off_row = jnp.zeros((1, LANES), F32).at[0, N_GROUPS:N_GROUPS + N_EXPERTS].set(seg[:-1].astype(F32))
    tile_ids = jnp.arange(N_TILES, dtype=jnp.int32)
    tile_expert = jnp.sum((tile_end[None, :] <= tile_ids[:, None]).astype(jnp.int32), axis=1)
    tile_expert = jnp.minimum(tile_expert, N_EXPERTS - 1)
    return cnt, seg, off_row, tile_expert, tile_end[-1:]


def _rope_tables(dim):
    pos = jnp.arange(SEQ, dtype=F32)
    inv = ROPE_THETA ** (-jnp.arange(0, dim, 2, dtype=F32) / dim)
    ang = pos[:, None] * inv[None, :]
    return jnp.cos(ang), jnp.sin(ang)


def _decay_tables():
    c = RET_CHUNK
    log_gamma = jnp.log1p(-jnp.exp2(-5.0 - jnp.arange(RET_HEADS, dtype=F32)))
    idx = jnp.arange(c, dtype=F32)
    rel = idx[:, None] - idx[None, :]
    dec = jnp.where(rel[None] >= 0, jnp.exp(log_gamma[:, None, None] * jnp.maximum(rel, 0.0)[None]), 0.0)
    xi = jnp.exp(log_gamma[:, None] * (idx[None, :] + 1.0))[:, :, None]
    zeta = jnp.exp(log_gamma[:, None] * (c - 1.0 - idx[None, :]))[:, :, None]
    cd = jnp.exp(log_gamma * c)[:, None, None]
    return dec, xi, zeta, cd


def _rotate_half_cols(w):
    half = w.shape[-1] // 2
    return jnp.concatenate([-w[..., half:], w[..., :half]], axis=-1)


def kernel(x, c, w_ada, b_ada, norm1, norm2, w_in, w_ret_o, q_norm, kv_norm, w_uq, w_ukv, w_mla_o, w_out,
           w_grp, b_grp, w_exp, b_exp, w1, w3, w2, final_norm):
    assert x.shape == (BATCH, SEQ, D_MODEL) and w_ada.shape[0] == 1
    x2 = x.reshape(N_TOK, D_MODEL)

    o_lat = 4 * RET_W
    o_pe = o_lat + MLA_Q_LORA + MLA_KV_LORA
    o_gate = o_pe + MLA_ROPE
    wi = w_in[0]
    col = jnp.arange(o_lat)
    k_scale = jnp.where((col >= RET_W) & (col < 2 * RET_W), RET_DK ** -0.5, 1.0).astype(F32)
    w_ret = (wi[:, :o_lat] * k_scale[None, :]).astype(BF16)
    w_lat = jnp.concatenate([wi[:, o_lat:o_gate], _rotate_half_cols(wi[:, o_pe:o_gate])], axis=1).astype(BF16)
    w_gate = wi[:, o_gate:].astype(BF16)
    wq = w_uq[0].reshape(MLA_Q_LORA, MLA_HEADS, MLA_QK)
    wq = jnp.concatenate([wq, _rotate_half_cols(wq[..., MLA_NOPE:])], axis=-1)
    wq = wq.transpose(1, 0, 2).astype(BF16)
    wkv = w_ukv[0].reshape(MLA_KV_LORA, MLA_HEADS, MLA_NOPE + MLA_V).transpose(1, 0, 2).astype(BF16)
    pad = LANES - N_GROUPS - N_EXPERTS
    w_rt = jnp.concatenate([w_grp[0], w_exp[0], jnp.zeros((D_MODEL, pad), F32)], axis=1)
    b_rt = jnp.concatenate([b_grp[0], b_exp[0], jnp.zeros((pad,), F32)])[None, :]

    ret_cos, ret_sin = _rope_tables(RET_DK)
    mla_cos, mla_sin = _rope_tables(MLA_ROPE)
    mla_cos = jnp.concatenate([mla_cos, mla_cos], axis=-1)
    mla_sin = jnp.concatenate([mla_sin, mla_sin], axis=-1)
    dec, xi, zeta, cd = _decay_tables()

    mod3 = _ada(c, w_ada[0], b_ada[0]).reshape(BATCH, 6, D_MODEL)
    ret, lat, gates = _inproj(x2, mod3, norm1, ret_cos, ret_sin, w_ret, w_lat, w_gate)
    y_ret = _retention(ret.reshape(BATCH, SEQ, 4 * RET_W), dec, xi, zeta, cd)
    attn = _mla(lat.reshape(BATCH, SEQ, MLA_LAT_W), q_norm, kv_norm, wq, wkv, mla_cos, mla_sin)
    h1, u2t, meta, counts = _merge(y_ret.reshape(N_TOK, D_MODEL), attn.reshape(N_TOK, D_MODEL), gates, x2, mod3,
                                   norm2, w_ret_o[0].astype(BF16), w_mla_o[0].astype(BF16),
                                   w_out[0].astype(BF16), w_rt, b_rt)
    cnt, seg, off_row, tile_expert, n_valid = _slot_layout(counts)
    dst = _plan(meta, off_row)
    d0, d1 = dst[:, 0], dst[:, 1]
    xs = _dispatch(d0, d1, seg, cnt, u2t)
    e_shape = (N_EXPERTS, D_MODEL, D_EXPERT)
    ys = _experts(tile_expert, n_valid, xs, w1[0].reshape(e_shape), w3[0].reshape(e_shape),
                  w2[0].reshape(N_EXPERTS, D_EXPERT, D_MODEL))
    out = _final(d0, d1, h1, meta, mod3, final_norm.reshape(1, D_MODEL), ys)
    return out.reshape(BATCH, SEQ, D_MODEL)
```

```python
import jax
import jax.numpy as jnp
from jax import lax
from jax.experimental import pallas as pl
from jax.experimental.pallas import tpu as pltpu

D_MODEL = 1024
BATCH = 8
SEQ = 2048
N_TOK = BATCH * SEQ

RET_HEADS = 4
RET_DK = 256
RET_DV = 256
RET_CHUNK = 256
RET_BLK = 512
RET_W = RET_HEADS * RET_DK

MLA_HEADS = 8
MLA_NOPE = 128
MLA_ROPE = 64
MLA_V = 128
MLA_Q_LORA = 384
MLA_KV_LORA = 256
MLA_LAT_W = MLA_Q_LORA + MLA_KV_LORA + 2 * MLA_ROPE
MLA_QK = MLA_NOPE + MLA_ROPE
ROPE_THETA = 10000.0

N_GROUPS = 4
EXPERTS_PER_GROUP = 8
N_EXPERTS = N_GROUPS * EXPERTS_PER_GROUP
D_EXPERT = 256
EPS = 1e-6
LOG2_E = 1.4426950408889634

LANES = 128
SUBLANES = 8
ROUTER_ROWS = 48
ROW_TILES = D_MODEL // LANES
VMEM_LIMIT = 56 * 1024 * 1024

TM_PROJ = 512
TQ = 256
MLA_HPS = 2
TE = 512
TE_CHAINS = 2
TOP_K = 2
N_TILES = N_TOK * TOP_K // TE + N_EXPERTS
N_SLOTS = N_TILES * TE

F32 = jnp.float32
BF16 = jnp.bfloat16


def _sigmoid(x):
    return 1.0 / (1.0 + jnp.exp(-x))


def _rms(x):
    return x * lax.rsqrt(jnp.mean(x * x, axis=-1, keepdims=True) + EPS)


def _dot(a, b):
    return jnp.dot(a, b, preferred_element_type=F32)


def _dot_nt(a, b):
    return lax.dot_general(a, b, (((1,), (1,)), ((), ())), preferred_element_type=F32)


def _dot_tn(a, b):
    return lax.dot_general(a, b, (((0,), (0,)), ((), ())), preferred_element_type=F32)


def _params(*sem):
    return pltpu.CompilerParams(dimension_semantics=sem, vmem_limit_bytes=VMEM_LIMIT)


def _resident(shape):
    nd = len(shape)
    return pl.BlockSpec(shape, lambda *_: (0,) * nd, pipeline_mode=pl.Buffered(1))


def _ada_kernel(c_ref, w_ref, b_ref, o_ref):
    c = c_ref[...]
    act = (c * _sigmoid(c)).astype(BF16)
    o_ref[...] = _dot(act, w_ref[...].astype(BF16)) + b_ref[...]


def _ada(c, w_ada, b_ada):
    n = w_ada.shape[1]
    tn = D_MODEL
    return pl.pallas_call(
        _ada_kernel,
        grid=(n // tn,),
        in_specs=[pl.BlockSpec((BATCH, D_MODEL), lambda j: (0, 0)),
                  pl.BlockSpec((D_MODEL, tn), lambda j: (0, j)),
                  pl.BlockSpec((1, tn), lambda j: (0, j))],
        out_specs=pl.BlockSpec((BATCH, tn), lambda j: (0, j)),
        out_shape=jax.ShapeDtypeStruct((BATCH, n), F32),
        compiler_params=_params("arbitrary"),
        name="ada",
    )(c, w_ada, b_ada.reshape(1, n))


def _inproj_kernel(x_ref, mod_ref, n1_ref, cos_ref, sin_ref, wr_ref, wm_ref, wg_ref, ret_ref, lat_ref, gate_ref):
    y = _rms(x_ref[...]) * n1_ref[...]
    u = (y * (1.0 + mod_ref[1:2, :]) + mod_ref[0:1, :]).astype(BF16)
    cos, sin = cos_ref[...], sin_ref[...]
    half = RET_DK // 2
    for n in range(0, 2 * RET_W, RET_DK):
        p = _dot(u, wr_ref[:, n:n + RET_DK])
        x1, x2 = p[:, :half], p[:, half:]
        ret_ref[:, n:n + half] = (x1 * cos - x2 * sin).astype(BF16)
        ret_ref[:, n + half:n + RET_DK] = (x2 * cos + x1 * sin).astype(BF16)
    step = 512
    for n in range(2 * RET_W, 3 * RET_W, step):
        ret_ref[:, n:n + step] = _dot(u, wr_ref[:, n:n + step]).astype(BF16)
    for n in range(3 * RET_W, 4 * RET_W, step):
        p = _dot(u, wr_ref[:, n:n + step])
        ret_ref[:, n:n + step] = (p * _sigmoid(p)).astype(BF16)
    lat_ref[...] = _dot(u, wm_ref[...]).astype(BF16)
    for n in range(0, 2 * D_MODEL, step):
        gate_ref[:, n:n + step] = _dot(u, wg_ref[:, n:n + step]).astype(BF16)


def _inproj(x2, mod3, norm1, cos, sin, w_ret, w_lat, w_gate):
    tm = TM_PROJ
    per_b = SEQ // tm
    rope_tab = pl.BlockSpec((tm, RET_DK // 2), lambda i: (i % per_b, 0))
    return pl.pallas_call(
        _inproj_kernel,
        grid=(N_TOK // tm,),
        in_specs=[pl.BlockSpec((tm, D_MODEL), lambda i: (i, 0)),
                  pl.BlockSpec((None, 6, D_MODEL), lambda i: (i // per_b, 0, 0)),
                  _resident((1, D_MODEL)), rope_tab, rope_tab,
                  _resident(w_ret.shape), _resident(w_lat.shape), _resident(w_gate.shape)],
        out_specs=[pl.BlockSpec((tm, 4 * RET_W), lambda i: (i, 0)),
                   pl.BlockSpec((tm, MLA_LAT_W), lambda i: (i, 0)),
                   pl.BlockSpec((tm, 2 * D_MODEL), lambda i: (i, 0))],
        out_shape=[jax.ShapeDtypeStruct((N_TOK, 4 * RET_W), BF16),
                   jax.ShapeDtypeStruct((N_TOK, MLA_LAT_W), BF16),
                   jax.ShapeDtypeStruct((N_TOK, 2 * D_MODEL), BF16)],
        compiler_params=_params("arbitrary"),
        name="inproj",
    )(x2, mod3, norm1, cos, sin, w_ret, w_lat, w_gate)


def _ret_kernel(q_ref, k_ref, v_ref, g_ref, dec_ref, xi_ref, zeta_ref, cd_ref, o_ref, state_ref):
    @pl.when(pl.program_id(1) == 0)
    def _():
        state_ref[...] = jnp.zeros_like(state_ref)

    heads = range(RET_HEADS)
    col = lambda h: slice(h * RET_DK, (h + 1) * RET_DK)
    for c in range(RET_BLK // RET_CHUNK):
        rows = slice(c * RET_CHUNK, (c + 1) * RET_CHUNK)
        scores = [(_dot_nt(q_ref[rows, col(h)], k_ref[rows, col(h)]) * dec_ref[h]).astype(BF16) for h in heads]
        carried = [xi_ref[h] * _dot(q_ref[rows, col(h)], state_ref[h].astype(BF16)) for h in heads]
        y = [_dot(scores[h], v_ref[rows, col(h)]) + carried[h] for h in heads]
        for h in heads:
            k_dec = (k_ref[rows, col(h)].astype(F32) * zeta_ref[h]).astype(BF16)
            state_ref[h] = state_ref[h] * cd_ref[h] + _dot_tn(k_dec, v_ref[rows, col(h)])
        yc = [y[h] - jnp.mean(y[h], axis=-1, keepdims=True) for h in heads]
        inv = [lax.rsqrt(jnp.mean(yc[h] * yc[h], axis=-1, keepdims=True) + EPS) for h in heads]
        for h in heads:
            o_ref[rows, col(h)] = (g_ref[rows, col(h)].astype(F32) * (yc[h] * inv[h])).astype(BF16)


def _retention(ret3, dec, xi, zeta, cd):
    blk = lambda part: pl.BlockSpec((None, RET_BLK, RET_W), lambda b, j: (b, j, part))
    whole = lambda a: pl.BlockSpec(a.shape, lambda b, j: (0,) * a.ndim)
    return pl.pallas_call(
        _ret_kernel,
        grid=(BATCH, SEQ // RET_BLK),
        in_specs=[blk(0), blk(1), blk(2), blk(3), whole(dec), whole(xi), whole(zeta), whole(cd)],
        out_specs=pl.BlockSpec((None, RET_BLK, RET_HEADS * RET_DV), lambda b, j: (b, j, 0)),
        out_shape=jax.ShapeDtypeStruct((BATCH, SEQ, RET_HEADS * RET_DV), BF16),
        scratch_shapes=[pltpu.VMEM((RET_HEADS, RET_DK, RET_DV), F32)],
        compiler_params=_params("arbitrary", "arbitrary"),
        name="retention",
    )(ret3, ret3, ret3, ret3, dec, xi, zeta, cd)


def _mla_kernel(lat_ref, qn_ref, kvn_ref, wq_ref, wkv_ref, cos_ref, sin_ref, o_ref,
                cq_s, ckv_s, kpe_s, q_s, k_s, v_s):
    h = pl.program_id(1)
    o_q, o_kv, o_pe, o_rot = 0, MLA_Q_LORA, MLA_Q_LORA + MLA_KV_LORA, MLA_Q_LORA + MLA_KV_LORA + MLA_ROPE
    cos, sin = cos_ref[...], sin_ref[...]

    @pl.when(h == 0)
    def _():
        cq_s[...] = (_rms(lat_ref[:, o_q:o_kv].astype(F32)) * qn_ref[...]).astype(BF16)
        ckv_s[...] = (_rms(lat_ref[:, o_kv:o_pe].astype(F32)) * kvn_ref[...]).astype(BF16)
        pe = lat_ref[:, o_pe:o_rot].astype(F32)
        rot = lat_ref[:, o_rot:o_rot + MLA_ROPE].astype(F32)
        kpe_s[...] = (pe * cos + rot * sin).astype(BF16)

    scale = (MLA_QK ** -0.5) * LOG2_E
    for g in range(MLA_HPS):
        qf = _dot(cq_s[...], wq_ref[g])
        q_s[g, :, :MLA_NOPE] = (qf[:, :MLA_NOPE] * scale).astype(BF16)
        q_pe = qf[:, MLA_NOPE:MLA_QK] * cos + qf[:, MLA_QK:] * sin
        q_s[g, :, MLA_NOPE:] = (q_pe * scale).astype(BF16)
        kvf = _dot(ckv_s[...], wkv_ref[g])
        k_s[g, :, :MLA_NOPE] = kvf[:, :MLA_NOPE].astype(BF16)
        k_s[g, :, MLA_NOPE:] = kpe_s[...]
        v_s[g, :, :MLA_V] = kvf[:, MLA_NOPE:].astype(BF16)
        v_s[g, :, MLA_V:] = jnp.ones((SEQ, MLA_V), BF16)

    causal = lax.broadcasted_iota(jnp.int32, (TQ, TQ), 0) >= lax.broadcasted_iota(jnp.int32, (TQ, TQ), 1)
    heads = range(MLA_HPS)
    n_blk = SEQ // TQ

    def scores(i):
        lo, hi = i * TQ, (i + 1) * TQ
        diag = [jnp.where(causal, _dot_nt(q_s[g, lo:hi, :], k_s[g, lo:hi, :]), -jnp.inf) for g in heads]
        past = [_dot_nt(q_s[g, lo:hi, :], k_s[g, :lo, :]) if i > 0 else None for g in heads]
        return diag, past

    pending = scores(0)
    for i in range(n_blk):
        lo, hi = i * TQ, (i + 1) * TQ
        diag, past = pending
        if i + 1 < n_blk:
            pending = scores(i + 1)
        m = [jnp.max(diag[g], axis=-1, keepdims=True) for g in heads]
        if i > 0:
            m = [jnp.maximum(m[g], jnp.max(past[g], axis=-1, keepdims=True)) for g in heads]
        acc = [_dot(jnp.exp2(diag[g] - m[g]).astype(BF16), v_s[g, lo:hi, :]) for g in heads]
        if i > 0:
            acc = [acc[g] + _dot(jnp.exp2(past[g] - m[g]).astype(BF16), v_s[g, :lo, :]) for g in heads]
        for g in heads:
            o_ref[lo:hi, g * MLA_V:(g + 1) * MLA_V] = (acc[g][:, :MLA_V] / acc[g][:, MLA_V:]).astype(BF16)


def _mla(lat3, q_norm, kv_norm, wq, wkv, cos, sin):
    hps = MLA_HPS
    return pl.pallas_call(
        _mla_kernel,
        grid=(BATCH, MLA_HEADS // hps),
        in_specs=[pl.BlockSpec((None, SEQ, MLA_LAT_W), lambda b, h: (b, 0, 0)),
                  pl.BlockSpec((1, MLA_Q_LORA), lambda b, h: (0, 0)),
                  pl.BlockSpec((1, MLA_KV_LORA), lambda b, h: (0, 0)),
                  pl.BlockSpec((hps, MLA_Q_LORA, MLA_QK + MLA_ROPE), lambda b, h: (h, 0, 0)),
                  pl.BlockSpec((hps, MLA_KV_LORA, MLA_NOPE + MLA_V), lambda b, h: (h, 0, 0)),
                  pl.BlockSpec((SEQ, MLA_ROPE), lambda b, h: (0, 0)),
                  pl.BlockSpec((SEQ, MLA_ROPE), lambda b, h: (0, 0))],
        out_specs=pl.BlockSpec((None, SEQ, hps * MLA_V), lambda b, h: (b, 0, h)),
        out_shape=jax.ShapeDtypeStruct((BATCH, SEQ, MLA_HEADS * MLA_V), BF16),
        scratch_shapes=[pltpu.VMEM((SEQ, MLA_Q_LORA), BF16),
                        pltpu.VMEM((SEQ, MLA_KV_LORA), BF16),
                        pltpu.VMEM((SEQ, MLA_ROPE), BF16),
                        pltpu.VMEM((hps, SEQ, MLA_QK), BF16),
                        pltpu.VMEM((hps, SEQ, MLA_QK), BF16),
                        pltpu.VMEM((hps, SEQ, 2 * MLA_V), BF16)],
        compiler_params=_params("arbitrary", "arbitrary"),
        name="mla",
    )(lat3, q_norm, kv_norm, wq, wkv, cos, sin)


def _route(logits_t):
    tm = logits_t.shape[1]
    row = lax.broadcasted_iota(jnp.int32, (SUBLANES, tm), 0)
    neg = -jnp.inf
    gl = jnp.where(row < N_GROUPS, logits_t[:SUBLANES], neg)
    gmax = jnp.max(gl, axis=0, keepdims=True)
    gsel = jnp.min(jnp.where(gl == gmax, row, SUBLANES), axis=0, keepdims=True)
    p_grp = 1.0 / jnp.sum(jnp.exp(gl - gmax), axis=0, keepdims=True)
    el = logits_t[SUBLANES * N_GROUPS:SUBLANES * (N_GROUPS + 1)]
    for g in reversed(range(N_GROUPS - 1)):
        el = jnp.where(gsel == g, logits_t[SUBLANES * (g + 1):SUBLANES * (g + 2)], el)
    v0 = jnp.max(el, axis=0, keepdims=True)
    i0 = jnp.min(jnp.where(el == v0, row, SUBLANES), axis=0, keepdims=True)
    el1 = jnp.where(row == i0, neg, el)
    v1 = jnp.max(el1, axis=0, keepdims=True)
    i1 = jnp.min(jnp.where(el1 == v1, row, SUBLANES), axis=0, keepdims=True)
    t = jnp.exp(v1 - v0)
    w0 = p_grp / (1.0 + t)
    w1 = p_grp * t / (1.0 + t)
    return gsel * EXPERTS_PER_GROUP + i0, gsel * EXPERTS_PER_GROUP + i1, w0, w1


def _stack_rows(rows, n):
    tm = rows[0].shape[1]
    row = lax.broadcasted_iota(jnp.int32, (n, tm), 0)
    out = jnp.zeros((n, tm), F32)
    for k, r in enumerate(rows):
        out = jnp.where(row == k, r, out)
    return out


def _to_token_tiles(ref, val):
    n = val.shape[0]
    for s in range(ROW_TILES):
        ref[pl.ds(s, n, stride=ROW_TILES), :] = val[:, s * LANES:(s + 1) * LANES]


def _from_token_tiles(ref):
    n = ref.shape[0] // ROW_TILES
    return jnp.concatenate([ref[pl.ds(s, n, stride=ROW_TILES), :] for s in range(ROW_TILES)], axis=-1)


def _token_rows(ref, t):
    return ref.at[pl.ds(pl.multiple_of(t * ROW_TILES, ROW_TILES), ROW_TILES)]


def _merge_kernel(yr_ref, at_ref, gr_ref, gm_ref, x_ref, mod_ref, n2_ref, wro_ref, wmo_ref, wo_ref,
                  wrt_ref, brt_ref, h1_ref, u2_ref, meta_ref, wtok_ref, cnt_ref, carry_ref):
    tm = x_ref.shape[0]

    @pl.when(pl.program_id(0) == 0)
    def _():
        carry_ref[...] = jnp.zeros_like(carry_ref)

    y_ret = _dot(yr_ref[...], wro_ref[...])
    y_mla = _dot(at_ref[...], wmo_ref[...])
    merged = _sigmoid(gr_ref[...].astype(F32)) * y_ret + _sigmoid(gm_ref[...].astype(F32)) * y_mla
    o = _dot(merged.astype(BF16), wo_ref[...])
    h1 = x_ref[...] + mod_ref[2:3, :] * o
    h1_ref[...] = h1
    u2 = _rms(h1) * n2_ref[...] * (1.0 + mod_ref[4:5, :]) + mod_ref[3:4, :]
    _to_token_tiles(u2_ref, u2)
    u_hi = u2.astype(BF16)
    u_lo = (u2 - u_hi.astype(F32)).astype(BF16)
    w = wrt_ref[...]
    w_hi = w.astype(BF16)
    w_lo = (w - w_hi.astype(F32)).astype(BF16)
    logits_t = _dot_nt(w_hi, u_hi) + _dot_nt(w_hi, u_lo) + _dot_nt(w_lo, u_hi) + brt_ref[:, 0:1]
    e0, e1, w0, w1 = _route(logits_t)
    erow = lax.broadcasted_iota(jnp.int32, (N_EXPERTS, tm), 0)
    m0, m1 = erow == e0, erow == e1
    member = jnp.where(m0 | m1, 1.0, 0.0)
    earlier = jnp.where(lax.broadcasted_iota(jnp.int32, (tm, tm), 0) < lax.broadcasted_iota(jnp.int32, (tm, tm), 1),
                        1.0, 0.0).astype(BF16)
    prefix = _dot(member.astype(BF16), earlier) + carry_ref[:, 0:1]
    rank0 = jnp.sum(jnp.where(m0, prefix, 0.0), axis=0, keepdims=True)
    rank1 = jnp.sum(jnp.where(m1, prefix, 0.0), axis=0, keepdims=True)
    carry_ref[...] = carry_ref[...] + jnp.sum(member, axis=1, keepdims=True)
    cnt_ref[...] = carry_ref[...]
    meta_ref[...] = _stack_rows([e0.astype(F32), e1.astype(F32), rank0, rank1], SUBLANES)
    wt = _stack_rows([w0, w1], 2 * SUBLANES)
    wt_hi = wt.astype(BF16)
    wt_lo = (wt - wt_hi.astype(F32)).astype(BF16)
    place = jnp.where(lax.broadcasted_iota(jnp.int32, (2 * SUBLANES, LANES), 0)
                      == lax.broadcasted_iota(jnp.int32, (2 * SUBLANES, LANES), 1), 1.0, 0.0).astype(BF16)
    wtok_ref[...] = _dot_tn(wt_hi, place) + _dot_tn(wt_lo, place)


def _merge(y_ret, attn, gates, x2, mod3, norm2, w_ret_o, w_mla_o, w_out, w_rt, b_rt):
    tm = TM_PROJ
    per_b = SEQ // tm
    row = lambda j: pl.BlockSpec((tm, D_MODEL), lambda i: (i, j))
    return pl.pallas_call(
        _merge_kernel,
        grid=(N_TOK // tm,),
        in_specs=[row(0), row(0), row(0), row(1), row(0),
                  pl.BlockSpec((None, 6, D_MODEL), lambda i: (i // per_b, 0, 0)),
                  _resident((1, D_MODEL)),
                  _resident(w_ret_o.shape), _resident(w_mla_o.shape), _resident(w_out.shape),
                  _resident(w_rt.shape), _resident(b_rt.shape)],
        out_specs=[row(0),
                   pl.BlockSpec((tm * ROW_TILES, LANES), lambda i: (i, 0)),
                   pl.BlockSpec((SUBLANES, tm), lambda i: (0, i)),
                   pl.BlockSpec((tm, LANES), lambda i: (i, 0)),
                   pl.BlockSpec((N_EXPERTS, LANES), lambda i: (0, 0))],
        out_shape=[jax.ShapeDtypeStruct((N_TOK, D_MODEL), F32),
                   jax.ShapeDtypeStruct((N_TOK * ROW_TILES, LANES), F32),
                   jax.ShapeDtypeStruct((SUBLANES, N_TOK), F32),
                   jax.ShapeDtypeStruct((N_TOK, LANES), F32),
                   jax.ShapeDtypeStruct((N_EXPERTS, LANES), F32)],
        scratch_shapes=[pltpu.VMEM((N_EXPERTS, LANES), F32)],
        compiler_params=_params("arbitrary"),
        name="merge",
    )(y_ret, attn, gates, gates, x2, mod3, norm2, w_ret_o, w_mla_o, w_out, w_rt, b_rt)


def _plan_kernel(meta_ref, off_ref, dst_ref):
    m = meta_ref[...]
    erow = lax.broadcasted_iota(jnp.int32, (N_EXPERTS, m.shape[1]), 0)
    off = off_ref[:, 0:1]
    d = [jnp.sum(jnp.where(erow == m[k:k + 1].astype(jnp.int32), off, 0.0), axis=0, keepdims=True) + m[k + 2:k + 3]
         for k in range(TOP_K)]
    dst_ref[...] = _stack_rows(d, SUBLANES).astype(jnp.int32)


def _plan(meta_t, off_col):
    tm = 2048
    return pl.pallas_call(
        _plan_kernel,
        grid=(N_TOK // tm,),
        in_specs=[pl.BlockSpec((SUBLANES, tm), lambda i: (0, i)),
                  pl.BlockSpec((N_EXPERTS, LANES), lambda i: (0, 0))],
        out_specs=pl.BlockSpec((SUBLANES, tm), lambda i: (0, i)),
        out_shape=jax.ShapeDtypeStruct((SUBLANES, N_TOK), jnp.int32),
        compiler_params=_params("arbitrary"),
        name="plan",
    )(meta_t, off_col)


def _row_copy_wait(src_like, dst_like, sem):
    pltpu.make_async_copy(src_like, dst_like, sem).wait()


def _dispatch_kernel(d0_ref, d1_ref, seg_ref, cnt_ref, u_ref, xs_ref, zero_ref, sem, zsem):
    i = pl.program_id(0)
    tm = u_ref.shape[0] // ROW_TILES
    tile_rows = TE * ROW_TILES

    def slot_tile(j):
        return xs_ref.at[pl.ds(pl.multiple_of(j * tile_rows, tile_rows), tile_rows)]

    @pl.when(i == 0)
    def _():
        zero_ref[...] = jnp.zeros_like(zero_ref)

        def pad(e):
            first = seg_ref[e] + cnt_ref[e]
            rows = (seg_ref[e + 1] - first) * ROW_TILES
            start = pl.multiple_of(first * ROW_TILES, ROW_TILES)
            return rows > 0, pltpu.make_async_copy(zero_ref.at[pl.ds(0, rows)], xs_ref.at[pl.ds(start, rows)], zsem)

        for e in range(N_EXPERTS):
            nonempty, copy = pad(e)
            pl.when(nonempty)(copy.start)
        for e in range(N_EXPERTS):
            nonempty, copy = pad(e)
            pl.when(nonempty)(copy.wait)

        def unused(j):
            return pltpu.make_async_copy(zero_ref, slot_tile(j), zsem)

        first_unused = seg_ref[N_EXPERTS] // TE
        lax.fori_loop(first_unused, N_TILES, lambda j, c: (unused(j).start(), c)[1], 0)
        lax.fori_loop(first_unused, N_TILES, lambda j, c: (unused(j).wait(), c)[1], 0)

    base = i * tm

    def body(r, carry):
        src = _token_rows(u_ref, r)
        pltpu.make_async_copy(src, _token_rows(xs_ref, d0_ref[base + r]), sem).start(priority=0)
        pltpu.make_async_copy(src, _token_rows(xs_ref, d1_ref[base + r]), sem).start(priority=1)
        return carry

    lax.fori_loop(0, tm, body, 0, unroll=8)
    _row_copy_wait(u_ref, xs_ref.at[pl.ds(0, tm * ROW_TILES)], sem)
    _row_copy_wait(u_ref, xs_ref.at[pl.ds(0, tm * ROW_TILES)], sem)


def _dispatch(d0, d1, seg, cnt, u2t):
    tm = TM_PROJ
    return pl.pallas_call(
        _dispatch_kernel,
        grid_spec=pltpu.PrefetchScalarGridSpec(
            num_scalar_prefetch=4,
            grid=(N_TOK // tm,),
            in_specs=[pl.BlockSpec((tm * ROW_TILES, LANES), lambda i, *_: (i, 0))],
            out_specs=pl.BlockSpec(memory_space=pl.ANY),
            scratch_shapes=[pltpu.VMEM((TE * ROW_TILES, LANES), F32),
                            pltpu.SemaphoreType.DMA(()), pltpu.SemaphoreType.DMA(())]),
        out_shape=jax.ShapeDtypeStruct((N_SLOTS * ROW_TILES, LANES), F32),
        compiler_params=_params("arbitrary"),
        name="dispatch",
    )(d0, d1, seg, cnt, u2t)


def _expert_kernel(te_ref, nv_ref, x_ref, w1_ref, w3_ref, w2_ref, y_ref, w1_s, w3_s, w2_s):
    j = pl.program_id(0)

    @pl.when(j < nv_ref[0])
    def _():
        @pl.when((j == 0) | (te_ref[j] != te_ref[jnp.maximum(j - 1, 0)]))
        def _():
            w1_s[...] = w1_ref[...].astype(BF16)
            w3_s[...] = w3_ref[...].astype(BF16)
            w2_s[...] = w2_ref[...].astype(BF16)

        sub = TE // TE_CHAINS * ROW_TILES
        part = lambda ref, c: ref.at[pl.ds(c * sub, sub)]
        chains = range(TE_CHAINS)
        x = [_from_token_tiles(part(x_ref, c)).astype(BF16) for c in chains]
        a = [_dot(x[c], w1_s[...]) for c in chains]
        b = [_dot(x[c], w3_s[...]) for c in chains]
        hid = [(a[c] * _sigmoid(a[c]) * b[c]).astype(BF16) for c in chains]
        for c in chains:
            _to_token_tiles(part(y_ref, c), _dot(hid[c], w2_s[...]))

    @pl.when(j >= nv_ref[0])
    def _():
        y_ref[...] = jnp.zeros_like(y_ref)


def _experts(tile_expert, n_valid, xs, w1, w3, w2):
    tile = lambda j, te, nv: jnp.minimum(j, nv[0] - 1)
    wspec = lambda shape: pl.BlockSpec((None,) + shape, lambda j, te, nv: (te[tile(j, te, nv)], 0, 0))
    slots = pl.BlockSpec((TE * ROW_TILES, LANES), lambda j, te, nv: (tile(j, te, nv), 0))
    return pl.pallas_call(
        _expert_kernel,
        grid_spec=pltpu.PrefetchScalarGridSpec(
            num_scalar_prefetch=2,
            grid=(N_TILES,),
            in_specs=[slots, wspec((D_MODEL, D_EXPERT)), wspec((D_MODEL, D_EXPERT)), wspec((D_EXPERT, D_MODEL))],
            out_specs=pl.BlockSpec((TE * ROW_TILES, LANES), lambda j, te, nv: (j, 0)),
            scratch_shapes=[pltpu.VMEM((D_MODEL, D_EXPERT), BF16), pltpu.VMEM((D_MODEL, D_EXPERT), BF16),
                            pltpu.VMEM((D_EXPERT, D_MODEL), BF16)]),
        out_shape=jax.ShapeDtypeStruct((N_SLOTS * ROW_TILES, LANES), F32),
        compiler_params=_params("arbitrary"),
        name="experts",
    )(tile_expert, n_valid, xs, w1, w3, w2)


def _final_kernel(d0_ref, d1_ref, h1_ref, meta_ref, mod_ref, fn_ref, ys_ref, o_ref, ybuf, sem):
    i = pl.program_id(0)
    tm = h1_ref.shape[0]

    def gather(t):
        buf, s = ybuf.at[t % 2], sem.at[t % 2]
        base = t * tm

        def body(r, carry):
            pltpu.make_async_copy(_token_rows(ys_ref, d0_ref[base + r]), _token_rows(buf.at[0], r), s).start(priority=0)
            pltpu.make_async_copy(_token_rows(ys_ref, d1_ref[base + r]), _token_rows(buf.at[1], r), s).start(priority=1)
            return carry

        lax.fori_loop(0, tm, body, 0, unroll=8)

    @pl.when(i == 0)
    def _():
        gather(0)

    @pl.when(i + 1 < pl.num_programs(0))
    def _():
        gather(i + 1)

    buf = ybuf.at[i % 2]
    for k in range(TOP_K):
        _row_copy_wait(ys_ref.at[pl.ds(0, tm * ROW_TILES)], buf.at[k], sem.at[i % 2])
    m = meta_ref[...]
    moe = m[:, 0:1] * _from_token_tiles(buf.at[0]) + m[:, 1:2] * _from_token_tiles(buf.at[1])
    h2 = h1_ref[...] + mod_ref[5:6, :] * moe
    o_ref[...] = _rms(h2) * fn_ref[...]


def _final(d0, d1, h1, meta, mod3, final_norm, ys):
    tm = TM_PROJ
    per_b = SEQ // tm
    return pl.pallas_call(
        _final_kernel,
        grid_spec=pltpu.PrefetchScalarGridSpec(
            num_scalar_prefetch=2,
            grid=(N_TOK // tm,),
            in_specs=[pl.BlockSpec((tm, D_MODEL), lambda i, *_: (i, 0)),
                      pl.BlockSpec((tm, LANES), lambda i, *_: (i, 0)),
                      pl.BlockSpec((None, 6, D_MODEL), lambda i, *_: (i // per_b, 0, 0)),
                      pl.BlockSpec((1, D_MODEL), lambda i, *_: (0, 0)),
                      pl.BlockSpec(memory_space=pl.ANY)],
            out_specs=pl.BlockSpec((tm, D_MODEL), lambda i, *_: (i, 0)),
            scratch_shapes=[pltpu.VMEM((2, TOP_K, tm * ROW_TILES, LANES), F32), pltpu.SemaphoreType.DMA((2,))]),
        out_shape=jax.ShapeDtypeStruct((N_TOK, D_MODEL), F32),
        compiler_params=_params("arbitrary"),
        name="final",
    )(d0, d1, h1, meta, mod3, final_norm, ys)


def _slot_layout(counts):
    cnt = counts[:, 0].astype(jnp.int32)
    tile_end = jnp.cumsum((cnt + TE - 1) // TE)
    seg = jnp.concatenate([jnp.zeros((1,), jnp.int32), tile_end * TE])
    off_col = jnp.broadcast_to(seg[:-1].astype(F32)[:, None], (N_EXPERTS, LANES))
    tile_ids = jnp.arange(N_TILES, dtype=jnp.int32)
    tile_expert = jnp.sum((tile_end[None, :] <= tile_ids[:, None]).astype(jnp.int32), axis=1)
    tile_expert = jnp.minimum(tile_expert, N_EXPERTS - 1)
    return cnt, seg, off_col, tile_expert, tile_end[-1:]


def _rope_tables(dim):
    pos = jnp.arange(SEQ, dtype=F32)
    inv = ROPE_THETA ** (-jnp.arange(0, dim, 2, dtype=F32) / dim)
    ang = pos[:, None] * inv[None, :]
    return jnp.cos(ang), jnp.sin(ang)


def _decay_tables():
    c = RET_CHUNK
    log_gamma = jnp.log1p(-jnp.exp2(-5.0 - jnp.arange(RET_HEADS, dtype=F32)))
    idx = jnp.arange(c, dtype=F32)
    rel = idx[:, None] - idx[None, :]
    dec = jnp.where(rel[None] >= 0, jnp.exp(log_gamma[:, None, None] * jnp.maximum(rel, 0.0)[None]), 0.0)
    xi = jnp.exp(log_gamma[:, None] * (idx[None, :] + 1.0))[:, :, None]
    zeta = jnp.exp(log_gamma[:, None] * (c - 1.0 - idx[None, :]))[:, :, None]
    cd = jnp.exp(log_gamma * c)[:, None, None]
    return dec, xi, zeta, cd


def _rotate_half_cols(w):
    half = w.shape[-1] // 2
    return jnp.concatenate([-w[..., half:], w[..., :half]], axis=-1)


def kernel(x, c, w_ada, b_ada, norm1, norm2, w_in, w_ret_o, q_norm, kv_norm, w_uq, w_ukv, w_mla_o, w_out,
           w_grp, b_grp, w_exp, b_exp, w1, w3, w2, final_norm):
    assert x.shape == (BATCH, SEQ, D_MODEL) and w_ada.shape[0] == 1
    x2 = x.reshape(N_TOK, D_MODEL)

    o_lat = 4 * RET_W
    o_pe = o_lat + MLA_Q_LORA + MLA_KV_LORA
    o_gate = o_pe + MLA_ROPE
    wi = w_in[0]
    col = jnp.arange(o_lat)
    k_scale = jnp.where((col >= RET_W) & (col < 2 * RET_W), RET_DK ** -0.5, 1.0).astype(F32)
    w_ret = (wi[:, :o_lat] * k_scale[None, :]).astype(BF16)
    w_lat = jnp.concatenate([wi[:, o_lat:o_gate], _rotate_half_cols(wi[:, o_pe:o_gate])], axis=1).astype(BF16)
    w_gate = wi[:, o_gate:].astype(BF16)
    wq = w_uq[0].reshape(MLA_Q_LORA, MLA_HEADS, MLA_QK)
    wq = jnp.concatenate([wq, _rotate_half_cols(wq[..., MLA_NOPE:])], axis=-1)
    wq = wq.transpose(1, 0, 2).astype(BF16)
    wkv = w_ukv[0].reshape(MLA_KV_LORA, MLA_HEADS, MLA_NOPE + MLA_V).transpose(1, 0, 2).astype(BF16)
    gap = jnp.zeros((SUBLANES - N_GROUPS, D_MODEL), F32)
    tail = jnp.zeros((ROUTER_ROWS - SUBLANES - N_EXPERTS, D_MODEL), F32)
    w_rt = jnp.concatenate([w_grp[0].T, gap, w_exp[0].T, tail], axis=0)
    b_rt = jnp.concatenate([b_grp[0], gap[:, 0], b_exp[0], tail[:, 0]])
    b_rt = jnp.broadcast_to(b_rt[:, None], (ROUTER_ROWS, LANES))

    ret_cos, ret_sin = _rope_tables(RET_DK)
    mla_cos, mla_sin = _rope_tables(MLA_ROPE)
    mla_cos = jnp.concatenate([mla_cos, mla_cos], axis=-1)
    mla_sin = jnp.concatenate([mla_sin, mla_sin], axis=-1)
    dec, xi, zeta, cd = _decay_tables()

    mod3 = _ada(c, w_ada[0], b_ada[0]).reshape(BATCH, 6, D_MODEL)
    ret, lat, gates = _inproj(x2, mod3, norm1, ret_cos, ret_sin, w_ret, w_lat, w_gate)
    y_ret = _retention(ret.reshape(BATCH, SEQ, 4 * RET_W), dec, xi, zeta, cd)
    attn = _mla(lat.reshape(BATCH, SEQ, MLA_LAT_W), q_norm, kv_norm, wq, wkv, mla_cos, mla_sin)
    h1, u2t, meta_t, wtok, counts = _merge(y_ret.reshape(N_TOK, D_MODEL), attn.reshape(N_TOK, D_MODEL), gates, x2,
                                           mod3, norm2, w_ret_o[0].astype(BF16), w_mla_o[0].astype(BF16),
                                           w_out[0].astype(BF16), w_rt, b_rt)
    cnt, seg, off_col, tile_expert, n_valid = _slot_layout(counts)
    dst = _plan(meta_t, off_col)
    d0, d1 = dst[0], dst[1]
    xs = _dispatch(d0, d1, seg, cnt, u2t)
    e_shape = (N_EXPERTS, D_MODEL, D_EXPERT)
    ys = _experts(tile_expert, n_valid, xs, w1[0].reshape(e_shape), w3[0].reshape(e_shape),
                  w2[0].reshape(N_EXPERTS, D_EXPERT, D_MODEL))
    out = _final(d0, d1, h1, wtok, mod3, final_norm.reshape(1, D_MODEL), ys)
    return out.reshape(BATCH, SEQ, D_MODEL)
```

```python
import jax
import jax.numpy as jnp
from jax import lax
from jax.experimental import pallas as pl
from jax.experimental.pallas import tpu as pltpu

D_MODEL = 1024
BATCH = 8
SEQ = 2048
N_TOK = BATCH * SEQ

RET_HEADS = 4
RET_DK = 256
RET_DV = 256
RET_CHUNK = 256
RET_BLK = 512
RET_W = RET_HEADS * RET_DK

MLA_HEADS = 8
MLA_NOPE = 128
MLA_ROPE = 64
MLA_V = 128
MLA_Q_LORA = 384
MLA_KV_LORA = 256
MLA_LAT_W = MLA_Q_LORA + MLA_KV_LORA + 2 * MLA_ROPE
MLA_QK = MLA_NOPE + MLA_ROPE
ROPE_THETA = 10000.0

N_GROUPS = 4
EXPERTS_PER_GROUP = 8
N_EXPERTS = N_GROUPS * EXPERTS_PER_GROUP
D_EXPERT = 256
EPS = 1e-6
LOG2_E = 1.4426950408889634

LANES = 128
SUBLANES = 8
ROUTER_ROWS = 48
ROW_TILES = D_MODEL // LANES
VMEM_LIMIT = 56 * 1024 * 1024

TM_PROJ = 512
TQ = 256
MLA_HPS = 4
TE = 512
TE_CHAINS = 2
MERGE_CHAINS = 2
TOP_K = 2
N_TILES = N_TOK * TOP_K // TE + N_EXPERTS
N_SLOTS = N_TILES * TE

F32 = jnp.float32
BF16 = jnp.bfloat16


def _sigmoid(x):
    return 1.0 / (1.0 + jnp.exp(-x))


def _rms(x):
    return x * lax.rsqrt(jnp.mean(x * x, axis=-1, keepdims=True) + EPS)


def _dot(a, b):
    return jnp.dot(a, b, preferred_element_type=F32)


def _dot_nt(a, b):
    return lax.dot_general(a, b, (((1,), (1,)), ((), ())), preferred_element_type=F32)


def _dot_tn(a, b):
    return lax.dot_general(a, b, (((0,), (0,)), ((), ())), preferred_element_type=F32)


def _params(*sem):
    return pltpu.CompilerParams(dimension_semantics=sem, vmem_limit_bytes=VMEM_LIMIT)


def _resident(shape):
    nd = len(shape)
    return pl.BlockSpec(shape, lambda *_: (0,) * nd, pipeline_mode=pl.Buffered(1))


def _ada_kernel(c_ref, w_ref, b_ref, o_ref):
    c = c_ref[...]
    act = (c * _sigmoid(c)).astype(BF16)
    o_ref[...] = _dot(act, w_ref[...].astype(BF16)) + b_ref[...]


def _ada(c, w_ada, b_ada):
    n = w_ada.shape[1]
    tn = D_MODEL
    return pl.pallas_call(
        _ada_kernel,
        grid=(n // tn,),
        in_specs=[pl.BlockSpec((BATCH, D_MODEL), lambda j: (0, 0)),
                  pl.BlockSpec((D_MODEL, tn), lambda j: (0, j)),
                  pl.BlockSpec((1, tn), lambda j: (0, j))],
        out_specs=pl.BlockSpec((BATCH, tn), lambda j: (0, j)),
        out_shape=jax.ShapeDtypeStruct((BATCH, n), F32),
        compiler_params=_params("arbitrary"),
        name="ada",
    )(c, w_ada, b_ada.reshape(1, n))


O_LAT = 4 * RET_W
O_PE = O_LAT + MLA_Q_LORA + MLA_KV_LORA
O_GATE = O_PE + MLA_ROPE


def _wprep_kernel(w_ref, ret_ref, lat_ref, gate_ref):
    ret_ref[:, :RET_W] = w_ref[:, :RET_W].astype(BF16)
    ret_ref[:, RET_W:2 * RET_W] = (w_ref[:, RET_W:2 * RET_W] * (RET_DK ** -0.5)).astype(BF16)
    ret_ref[:, 2 * RET_W:] = w_ref[:, 2 * RET_W:O_LAT].astype(BF16)
    lat_ref[:, :O_GATE - O_LAT] = w_ref[:, O_LAT:O_GATE].astype(BF16)
    half = MLA_ROPE // 2
    rot = jnp.concatenate([-w_ref[:, O_PE + half:O_GATE], w_ref[:, O_PE:O_PE + half]], axis=-1)
    lat_ref[:, O_GATE - O_LAT:] = rot.astype(BF16)
    gate_ref[...] = w_ref[:, O_GATE:].astype(BF16)


def _wprep(w_in):
    rows = 256
    n_in = w_in.shape[1]
    out_w = (O_LAT, MLA_LAT_W, 2 * D_MODEL)
    return pl.pallas_call(
        _wprep_kernel,
        grid=(D_MODEL // rows,),
        in_specs=[pl.BlockSpec((rows, n_in), lambda i: (i, 0))],
        out_specs=[pl.BlockSpec((rows, n), lambda i: (i, 0)) for n in out_w],
        out_shape=[jax.ShapeDtypeStruct((D_MODEL, n), BF16) for n in out_w],
        compiler_params=_params("arbitrary"),
        name="wprep",
    )(w_in)


def _inproj_kernel(x_ref, mod_ref, n1_ref, cos_ref, sin_ref, wr_ref, wm_ref, wg_ref, ret_ref, lat_ref, gate_ref):
    y = _rms(x_ref[...]) * n1_ref[...]
    u = (y * (1.0 + mod_ref[1:2, :]) + mod_ref[0:1, :]).astype(BF16)
    cos, sin = cos_ref[...], sin_ref[...]
    half = RET_DK // 2
    for n in range(0, 2 * RET_W, RET_DK):
        p = _dot(u, wr_ref[:, n:n + RET_DK])
        x1, x2 = p[:, :half], p[:, half:]
        ret_ref[:, n:n + half] = (x1 * cos - x2 * sin).astype(BF16)
        ret_ref[:, n + half:n + RET_DK] = (x2 * cos + x1 * sin).astype(BF16)
    step = 512
    for n in range(2 * RET_W, 3 * RET_W, step):
        ret_ref[:, n:n + step] = _dot(u, wr_ref[:, n:n + step]).astype(BF16)
    for n in range(3 * RET_W, 4 * RET_W, step):
        p = _dot(u, wr_ref[:, n:n + step])
        ret_ref[:, n:n + step] = (p * _sigmoid(p)).astype(BF16)
    lat_ref[...] = _dot(u, wm_ref[...]).astype(BF16)
    for n in range(0, 2 * D_MODEL, step):
        gate_ref[:, n:n + step] = _dot(u, wg_ref[:, n:n + step]).astype(BF16)


def _inproj(x2, mod3, norm1, cos, sin, w_ret, w_lat, w_gate):
    tm = TM_PROJ
    per_b = SEQ // tm
    rope_tab = pl.BlockSpec((tm, RET_DK // 2), lambda i: (i % per_b, 0))
    return pl.pallas_call(
        _inproj_kernel,
        grid=(N_TOK // tm,),
        in_specs=[pl.BlockSpec((tm, D_MODEL), lambda i: (i, 0)),
                  pl.BlockSpec((None, 6, D_MODEL), lambda i: (i // per_b, 0, 0)),
                  _resident((1, D_MODEL)), rope_tab, rope_tab,
                  _resident(w_ret.shape), _resident(w_lat.shape), _resident(w_gate.shape)],
        out_specs=[pl.BlockSpec((tm, 4 * RET_W), lambda i: (i, 0)),
                   pl.BlockSpec((tm, MLA_LAT_W), lambda i: (i, 0)),
                   pl.BlockSpec((tm, 2 * D_MODEL), lambda i: (i, 0))],
        out_shape=[jax.ShapeDtypeStruct((N_TOK, 4 * RET_W), BF16),
                   jax.ShapeDtypeStruct((N_TOK, MLA_LAT_W), BF16),
                   jax.ShapeDtypeStruct((N_TOK, 2 * D_MODEL), BF16)],
        compiler_params=_params("arbitrary"),
        name="inproj",
    )(x2, mod3, norm1, cos, sin, w_ret, w_lat, w_gate)


def _ret_kernel(q_ref, k_ref, v_ref, g_ref, dec_ref, xi_ref, zeta_ref, cd_ref, o_ref, state_ref):
    @pl.when(pl.program_id(1) == 0)
    def _():
        state_ref[...] = jnp.zeros_like(state_ref)

    heads = range(RET_HEADS)
    col = lambda h: slice(h * RET_DK, (h + 1) * RET_DK)
    for c in range(RET_BLK // RET_CHUNK):
        rows = slice(c * RET_CHUNK, (c + 1) * RET_CHUNK)
        scores = [(_dot_nt(q_ref[rows, col(h)], k_ref[rows, col(h)]) * dec_ref[h]).astype(BF16) for h in heads]
        carried = [xi_ref[h] * _dot(q_ref[rows, col(h)], state_ref[h].astype(BF16)) for h in heads]
        y = [_dot(scores[h], v_ref[rows, col(h)]) + carried[h] for h in heads]
        for h in heads:
            k_dec = (k_ref[rows, col(h)].astype(F32) * zeta_ref[h]).astype(BF16)
            state_ref[h] = state_ref[h] * cd_ref[h] + _dot_tn(k_dec, v_ref[rows, col(h)])
        yc = [y[h] - jnp.mean(y[h], axis=-1, keepdims=True) for h in heads]
        inv = [lax.rsqrt(jnp.mean(yc[h] * yc[h], axis=-1, keepdims=True) + EPS) for h in heads]
        for h in heads:
            o_ref[rows, col(h)] = (g_ref[rows, col(h)].astype(F32) * (yc[h] * inv[h])).astype(BF16)


def _retention(ret3, dec, xi, zeta, cd):
    blk = lambda part: pl.BlockSpec((None, RET_BLK, RET_W), lambda b, j: (b, j, part))
    whole = lambda a: pl.BlockSpec(a.shape, lambda b, j: (0,) * a.ndim)
    return pl.pallas_call(
        _ret_kernel,
        grid=(BATCH, SEQ // RET_BLK),
        in_specs=[blk(0), blk(1), blk(2), blk(3), whole(dec), whole(xi), whole(zeta), whole(cd)],
        out_specs=pl.BlockSpec((None, RET_BLK, RET_HEADS * RET_DV), lambda b, j: (b, j, 0)),
        out_shape=jax.ShapeDtypeStruct((BATCH, SEQ, RET_HEADS * RET_DV), BF16),
        scratch_shapes=[pltpu.VMEM((RET_HEADS, RET_DK, RET_DV), F32)],
        compiler_params=_params("arbitrary", "arbitrary"),
        name="retention",
    )(ret3, ret3, ret3, ret3, dec, xi, zeta, cd)


def _mla_kernel(lat_ref, qn_ref, kvn_ref, wq_ref, wkv_ref, cos_ref, sin_ref, o_ref,
                cq_s, ckv_s, kpe_s, q_s, k_s, v_s):
    h = pl.program_id(1)
    o_q, o_kv, o_pe, o_rot = 0, MLA_Q_LORA, MLA_Q_LORA + MLA_KV_LORA, MLA_Q_LORA + MLA_KV_LORA + MLA_ROPE
    cos, sin = cos_ref[...], sin_ref[...]

    @pl.when(h == 0)
    def _():
        cq_s[...] = (_rms(lat_ref[:, o_q:o_kv].astype(F32)) * qn_ref[...]).astype(BF16)
        ckv_s[...] = (_rms(lat_ref[:, o_kv:o_pe].astype(F32)) * kvn_ref[...]).astype(BF16)
        pe = lat_ref[:, o_pe:o_rot].astype(F32)
        rot = lat_ref[:, o_rot:o_rot + MLA_ROPE].astype(F32)
        kpe_s[...] = (pe * cos + rot * sin).astype(BF16)

    scale = (MLA_QK ** -0.5) * LOG2_E
    for g in range(MLA_HPS):
        qf = _dot(cq_s[...], wq_ref[g])
        q_s[g, :, :MLA_NOPE] = (qf[:, :MLA_NOPE] * scale).astype(BF16)
        q_pe = qf[:, MLA_NOPE:MLA_QK] * cos + qf[:, MLA_QK:] * sin
        q_s[g, :, MLA_NOPE:] = (q_pe * scale).astype(BF16)
        kvf = _dot(ckv_s[...], wkv_ref[g])
        k_s[g, :, :MLA_NOPE] = kvf[:, :MLA_NOPE].astype(BF16)
        k_s[g, :, MLA_NOPE:] = kpe_s[...]
        v_s[g, :, :MLA_V] = kvf[:, MLA_NOPE:].astype(BF16)
        v_s[g, :, MLA_V:] = jnp.ones((SEQ, MLA_V), BF16)

    causal = lax.broadcasted_iota(jnp.int32, (TQ, TQ), 0) >= lax.broadcasted_iota(jnp.int32, (TQ, TQ), 1)
    heads = range(MLA_HPS)
    n_blk = SEQ // TQ

    def scores(i):
        lo, hi = i * TQ, (i + 1) * TQ
        diag = [jnp.where(causal, _dot_nt(q_s[g, lo:hi, :], k_s[g, lo:hi, :]), -jnp.inf) for g in heads]
        past = [_dot_nt(q_s[g, lo:hi, :], k_s[g, :lo, :]) if i > 0 else None for g in heads]
        return diag, past

    pending = scores(0)
    for i in range(n_blk):
        lo, hi = i * TQ, (i + 1) * TQ
        diag, past = pending
        if i + 1 < n_blk:
            pending = scores(i + 1)
        m = [jnp.max(diag[g], axis=-1, keepdims=True) for g in heads]
        if i > 0:
            m = [jnp.maximum(m[g], jnp.max(past[g], axis=-1, keepdims=True)) for g in heads]
        acc = [_dot(jnp.exp2(diag[g] - m[g]).astype(BF16), v_s[g, lo:hi, :]) for g in heads]
        if i > 0:
            acc = [acc[g] + _dot(jnp.exp2(past[g] - m[g]).astype(BF16), v_s[g, :lo, :]) for g in heads]
        for g in heads:
            o_ref[lo:hi, g * MLA_V:(g + 1) * MLA_V] = (acc[g][:, :MLA_V] / acc[g][:, MLA_V:]).astype(BF16)


def _mla(lat3, q_norm, kv_norm, wq, wkv, cos, sin):
    hps = MLA_HPS
    return pl.pallas_call(
        _mla_kernel,
        grid=(BATCH, MLA_HEADS // hps),
        in_specs=[pl.BlockSpec((None, SEQ, MLA_LAT_W), lambda b, h: (b, 0, 0)),
                  pl.BlockSpec((1, MLA_Q_LORA), lambda b, h: (0, 0)),
                  pl.BlockSpec((1, MLA_KV_LORA), lambda b, h: (0, 0)),
                  pl.BlockSpec((hps, MLA_Q_LORA, MLA_QK + MLA_ROPE), lambda b, h: (h, 0, 0)),
                  pl.BlockSpec((hps, MLA_KV_LORA, MLA_NOPE + MLA_V), lambda b, h: (h, 0, 0)),
                  pl.BlockSpec((SEQ, MLA_ROPE), lambda b, h: (0, 0)),
                  pl.BlockSpec((SEQ, MLA_ROPE), lambda b, h: (0, 0))],
        out_specs=pl.BlockSpec((None, SEQ, hps * MLA_V), lambda b, h: (b, 0, h)),
        out_shape=jax.ShapeDtypeStruct((BATCH, SEQ, MLA_HEADS * MLA_V), BF16),
        scratch_shapes=[pltpu.VMEM((SEQ, MLA_Q_LORA), BF16),
                        pltpu.VMEM((SEQ, MLA_KV_LORA), BF16),
                        pltpu.VMEM((SEQ, MLA_ROPE), BF16),
                        pltpu.VMEM((hps, SEQ, MLA_QK), BF16),
                        pltpu.VMEM((hps, SEQ, MLA_QK), BF16),
                        pltpu.VMEM((hps, SEQ, 2 * MLA_V), BF16)],
        compiler_params=_params("arbitrary", "arbitrary"),
        name="mla",
    )(lat3, q_norm, kv_norm, wq, wkv, cos, sin)


def _route(logits_t):
    tm = logits_t.shape[1]
    row = lax.broadcasted_iota(jnp.int32, (SUBLANES, tm), 0)
    neg = -jnp.inf
    gl = jnp.where(row < N_GROUPS, logits_t[:SUBLANES], neg)
    gmax = jnp.max(gl, axis=0, keepdims=True)
    gsel = jnp.min(jnp.where(gl == gmax, row, SUBLANES), axis=0, keepdims=True)
    p_grp = 1.0 / jnp.sum(jnp.exp(gl - gmax), axis=0, keepdims=True)
    el = logits_t[SUBLANES * N_GROUPS:SUBLANES * (N_GROUPS + 1)]
    for g in reversed(range(N_GROUPS - 1)):
        el = jnp.where(gsel == g, logits_t[SUBLANES * (g + 1):SUBLANES * (g + 2)], el)
    v0 = jnp.max(el, axis=0, keepdims=True)
    i0 = jnp.min(jnp.where(el == v0, row, SUBLANES), axis=0, keepdims=True)
    el1 = jnp.where(row == i0, neg, el)
    v1 = jnp.max(el1, axis=0, keepdims=True)
    i1 = jnp.min(jnp.where(el1 == v1, row, SUBLANES), axis=0, keepdims=True)
    t = jnp.exp(v1 - v0)
    w0 = p_grp / (1.0 + t)
    w1 = p_grp * t / (1.0 + t)
    return gsel * EXPERTS_PER_GROUP + i0, gsel * EXPERTS_PER_GROUP + i1, w0, w1


def _stack_rows(rows, n):
    tm = rows[0].shape[1]
    row = lax.broadcasted_iota(jnp.int32, (n, tm), 0)
    out = jnp.zeros((n, tm), F32)
    for k, r in enumerate(rows):
        out = jnp.where(row == k, r, out)
    return out


def _to_token_tiles(ref, val):
    n = val.shape[0]
    for s in range(ROW_TILES):
        ref[pl.ds(s, n, stride=ROW_TILES), :] = val[:, s * LANES:(s + 1) * LANES]


def _from_token_tiles(ref):
    n = ref.shape[0] // ROW_TILES
    return jnp.concatenate([ref[pl.ds(s, n, stride=ROW_TILES), :] for s in range(ROW_TILES)], axis=-1)


def _token_rows(ref, t):
    return ref.at[pl.ds(pl.multiple_of(t * ROW_TILES, ROW_TILES), ROW_TILES)]


def _merge_kernel(yr_ref, at_ref, gr_ref, gm_ref, x_ref, mod_ref, n2_ref, wro_ref, wmo_ref, wo_ref,
                  wrt_ref, brt_ref, h1_ref, u2_ref, meta_ref, wtok_ref, cnt_ref, carry_ref):
    tm = x_ref.shape[0]

    @pl.when(pl.program_id(0) == 0)
    def _():
        carry_ref[...] = jnp.zeros_like(carry_ref)

    chunks = range(MERGE_CHAINS)
    sub = tm // MERGE_CHAINS
    rows = [slice(c * sub, (c + 1) * sub) for c in chunks]
    y_ret = [_dot(yr_ref[r, :], wro_ref[...]) for r in rows]
    y_mla = [_dot(at_ref[r, :], wmo_ref[...]) for r in rows]
    merged = [(_sigmoid(gr_ref[rows[c], :].astype(F32)) * y_ret[c]
               + _sigmoid(gm_ref[rows[c], :].astype(F32)) * y_mla[c]).astype(BF16) for c in chunks]
    o = [_dot(merged[c], wo_ref[...]) for c in chunks]
    h1 = [x_ref[rows[c], :] + mod_ref[2:3, :] * o[c] for c in chunks]
    for c in chunks:
        h1_ref[rows[c], :] = h1[c]
    u2 = [_rms(h1[c]) * n2_ref[...] * (1.0 + mod_ref[4:5, :]) + mod_ref[3:4, :] for c in chunks]
    for c in chunks:
        _to_token_tiles(u2_ref.at[pl.ds(c * sub * ROW_TILES, sub * ROW_TILES)], u2[c])
    w = wrt_ref[...]
    w_hi = w.astype(BF16)
    w_lo = (w - w_hi.astype(F32)).astype(BF16)
    u_hi = [u2[c].astype(BF16) for c in chunks]
    u_lo = [(u2[c] - u_hi[c].astype(F32)).astype(BF16) for c in chunks]
    logits_t = [_dot_nt(w_hi, u_hi[c]) + _dot_nt(w_hi, u_lo[c]) + _dot_nt(w_lo, u_hi[c]) + brt_ref[:, 0:1]
                for c in chunks]
    routed = [_route(logits_t[c]) for c in chunks]
    e0, e1, w0, w1 = [jnp.concatenate([routed[c][k] for c in chunks], axis=1) for k in range(4)]
    erow = lax.broadcasted_iota(jnp.int32, (N_EXPERTS, tm), 0)
    m0, m1 = erow == e0, erow == e1
    member = jnp.where(m0 | m1, 1.0, 0.0)
    earlier = jnp.where(lax.broadcasted_iota(jnp.int32, (tm, tm), 0) < lax.broadcasted_iota(jnp.int32, (tm, tm), 1),
                        1.0, 0.0).astype(BF16)
    prefix = _dot(member.astype(BF16), earlier) + carry_ref[:, 0:1]
    rank0 = jnp.sum(jnp.where(m0, prefix, 0.0), axis=0, keepdims=True)
    rank1 = jnp.sum(jnp.where(m1, prefix, 0.0), axis=0, keepdims=True)
    carry_ref[...] = carry_ref[...] + jnp.sum(member, axis=1, keepdims=True)
    cnt_ref[...] = carry_ref[...]
    meta_ref[...] = _stack_rows([e0.astype(F32), e1.astype(F32), rank0, rank1], SUBLANES)
    wt = _stack_rows([w0, w1], 2 * SUBLANES)
    wt_hi = wt.astype(BF16)
    wt_lo = (wt - wt_hi.astype(F32)).astype(BF16)
    place = jnp.where(lax.broadcasted_iota(jnp.int32, (2 * SUBLANES, LANES), 0)
                      == lax.broadcasted_iota(jnp.int32, (2 * SUBLANES, LANES), 1), 1.0, 0.0).astype(BF16)
    wtok_ref[...] = _dot_tn(wt_hi, place) + _dot_tn(wt_lo, place)


def _merge(y_ret, attn, gates, x2, mod3, norm2, w_ret_o, w_mla_o, w_out, w_rt, b_rt):
    tm = TM_PROJ
    per_b = SEQ // tm
    row = lambda j: pl.BlockSpec((tm, D_MODEL), lambda i: (i, j))
    return pl.pallas_call(
        _merge_kernel,
        grid=(N_TOK // tm,),
        in_specs=[row(0), row(0), row(0), row(1), row(0),
                  pl.BlockSpec((None, 6, D_MODEL), lambda i: (i // per_b, 0, 0)),
                  _resident((1, D_MODEL)),
                  _resident(w_ret_o.shape), _resident(w_mla_o.shape), _resident(w_out.shape),
                  _resident(w_rt.shape), _resident(b_rt.shape)],
        out_specs=[row(0),
                   pl.BlockSpec((tm * ROW_TILES, LANES), lambda i: (i, 0)),
                   pl.BlockSpec((SUBLANES, tm), lambda i: (0, i)),
                   pl.BlockSpec((tm, LANES), lambda i: (i, 0)),
                   pl.BlockSpec((N_EXPERTS, LANES), lambda i: (0, 0))],
        out_shape=[jax.ShapeDtypeStruct((N_TOK, D_MODEL), F32),
                   jax.ShapeDtypeStruct((N_TOK * ROW_TILES, LANES), F32),
                   jax.ShapeDtypeStruct((SUBLANES, N_TOK), F32),
                   jax.ShapeDtypeStruct((N_TOK, LANES), F32),
                   jax.ShapeDtypeStruct((N_EXPERTS, LANES), F32)],
        scratch_shapes=[pltpu.VMEM((N_EXPERTS, LANES), F32)],
        compiler_params=_params("arbitrary"),
        name="merge",
    )(y_ret, attn, gates, gates, x2, mod3, norm2, w_ret_o, w_mla_o, w_out, w_rt, b_rt)


def _plan_kernel(meta_ref, off_ref, dst_ref):
    m = meta_ref[...]
    erow = lax.broadcasted_iota(jnp.int32, (N_EXPERTS, m.shape[1]), 0)
    off = off_ref[:, 0:1]
    d = [jnp.sum(jnp.where(erow == m[k:k + 1].astype(jnp.int32), off, 0.0), axis=0, keepdims=True) + m[k + 2:k + 3]
         for k in range(TOP_K)]
    dst_ref[...] = _stack_rows(d, SUBLANES).astype(jnp.int32)


def _plan(meta_t, off_col):
    tm = 2048
    return pl.pallas_call(
        _plan_kernel,
        grid=(N_TOK // tm,),
        in_specs=[pl.BlockSpec((SUBLANES, tm), lambda i: (0, i)),
                  pl.BlockSpec((N_EXPERTS, LANES), lambda i: (0, 0))],
        out_specs=pl.BlockSpec((SUBLANES, tm), lambda i: (0, i)),
        out_shape=jax.ShapeDtypeStruct((SUBLANES, N_TOK), jnp.int32),
        compiler_params=_params("arbitrary"),
        name="plan",
    )(meta_t, off_col)


def _row_copy_wait(src_like, dst_like, sem):
    pltpu.make_async_copy(src_like, dst_like, sem).wait()


def _dispatch_kernel(d0_ref, d1_ref, seg_ref, cnt_ref, u_ref, xs_ref, zero_ref, sem, zsem):
    i = pl.program_id(0)
    tm = u_ref.shape[0] // ROW_TILES
    tile_rows = TE * ROW_TILES

    def slot_tile(j):
        return xs_ref.at[pl.ds(pl.multiple_of(j * tile_rows, tile_rows), tile_rows)]

    @pl.when(i == 0)
    def _():
        zero_ref[...] = jnp.zeros_like(zero_ref)

        def pad(e):
            first = seg_ref[e] + cnt_ref[e]
            rows = (seg_ref[e + 1] - first) * ROW_TILES
            start = pl.multiple_of(first * ROW_TILES, ROW_TILES)
            return rows > 0, pltpu.make_async_copy(zero_ref.at[pl.ds(0, rows)], xs_ref.at[pl.ds(start, rows)], zsem)

        for e in range(N_EXPERTS):
            nonempty, copy = pad(e)
            pl.when(nonempty)(copy.start)
        for e in range(N_EXPERTS):
            nonempty, copy = pad(e)
            pl.when(nonempty)(copy.wait)

        def unused(j):
            return pltpu.make_async_copy(zero_ref, slot_tile(j), zsem)

        first_unused = seg_ref[N_EXPERTS] // TE
        lax.fori_loop(first_unused, N_TILES, lambda j, c: (unused(j).start(), c)[1], 0)
        lax.fori_loop(first_unused, N_TILES, lambda j, c: (unused(j).wait(), c)[1], 0)

    base = i * tm

    def body(r, carry):
        src = _token_rows(u_ref, r)
        pltpu.make_async_copy(src, _token_rows(xs_ref, d0_ref[base + r]), sem).start(priority=0)
        pltpu.make_async_copy(src, _token_rows(xs_ref, d1_ref[base + r]), sem).start(priority=1)
        return carry

    lax.fori_loop(0, tm, body, 0, unroll=8)
    _row_copy_wait(u_ref, xs_ref.at[pl.ds(0, tm * ROW_TILES)], sem)
    _row_copy_wait(u_ref, xs_ref.at[pl.ds(0, tm * ROW_TILES)], sem)


def _dispatch(d0, d1, seg, cnt, u2t):
    tm = TM_PROJ
    return pl.pallas_call(
        _dispatch_kernel,
        grid_spec=pltpu.PrefetchScalarGridSpec(
            num_scalar_prefetch=4,
            grid=(N_TOK // tm,),
            in_specs=[pl.BlockSpec((tm * ROW_TILES, LANES), lambda i, *_: (i, 0))],
            out_specs=pl.BlockSpec(memory_space=pl.ANY),
            scratch_shapes=[pltpu.VMEM((TE * ROW_TILES, LANES), F32),
                            pltpu.SemaphoreType.DMA(()), pltpu.SemaphoreType.DMA(())]),
        out_shape=jax.ShapeDtypeStruct((N_SLOTS * ROW_TILES, LANES), F32),
        compiler_params=_params("arbitrary"),
        name="dispatch",
    )(d0, d1, seg, cnt, u2t)


def _expert_kernel(te_ref, nv_ref, x_ref, w1_ref, w3_ref, w2_ref, y_ref, w1_s, w3_s, w2_s):
    j = pl.program_id(0)

    @pl.when(j < nv_ref[0])
    def _():
        @pl.when((j == 0) | (te_ref[j] != te_ref[jnp.maximum(j - 1, 0)]))
        def _():
            w1_s[...] = w1_ref[...].astype(BF16)
            w3_s[...] = w3_ref[...].astype(BF16)
            w2_s[...] = w2_ref[...].astype(BF16)

        sub = TE // TE_CHAINS * ROW_TILES
        part = lambda ref, c: ref.at[pl.ds(c * sub, sub)]
        chains = range(TE_CHAINS)
        x = [_from_token_tiles(part(x_ref, c)).astype(BF16) for c in chains]
        a = [_dot(x[c], w1_s[...]) for c in chains]
        b = [_dot(x[c], w3_s[...]) for c in chains]
        hid = [(a[c] * _sigmoid(a[c]) * b[c]).astype(BF16) for c in chains]
        for c in chains:
            _to_token_tiles(part(y_ref, c), _dot(hid[c], w2_s[...]))

    @pl.when(j >= nv_ref[0])
    def _():
        y_ref[...] = jnp.zeros_like(y_ref)


def _experts(tile_expert, n_valid, xs, w1, w3, w2):
    tile = lambda j, te, nv: jnp.minimum(j, nv[0] - 1)
    wspec = lambda shape: pl.BlockSpec((None,) + shape, lambda j, te, nv: (te[tile(j, te, nv)], 0, 0))
    slots = pl.BlockSpec((TE * ROW_TILES, LANES), lambda j, te, nv: (tile(j, te, nv), 0))
    return pl.pallas_call(
        _expert_kernel,
        grid_spec=pltpu.PrefetchScalarGridSpec(
            num_scalar_prefetch=2,
            grid=(N_TILES,),
            in_specs=[slots, wspec((D_MODEL, D_EXPERT)), wspec((D_MODEL, D_EXPERT)), wspec((D_EXPERT, D_MODEL))],
            out_specs=pl.BlockSpec((TE * ROW_TILES, LANES), lambda j, te, nv: (j, 0)),
            scratch_shapes=[pltpu.VMEM((D_MODEL, D_EXPERT), BF16), pltpu.VMEM((D_MODEL, D_EXPERT), BF16),
                            pltpu.VMEM((D_EXPERT, D_MODEL), BF16)]),
        out_shape=jax.ShapeDtypeStruct((N_SLOTS * ROW_TILES, LANES), F32),
        compiler_params=_params("arbitrary"),
        name="experts",
    )(tile_expert, n_valid, xs, w1, w3, w2)


def _final_kernel(d0_ref, d1_ref, h1_ref, meta_ref, mod_ref, fn_ref, ys_ref, o_ref, ybuf, sem):
    i = pl.program_id(0)
    tm = h1_ref.shape[0]

    def gather(t):
        buf, s = ybuf.at[t % 2], sem.at[t % 2]
        base = t * tm

        def body(r, carry):
            pltpu.make_async_copy(_token_rows(ys_ref, d0_ref[base + r]), _token_rows(buf.at[0], r), s).start(priority=0)
            pltpu.make_async_copy(_token_rows(ys_ref, d1_ref[base + r]), _token_rows(buf.at[1], r), s).start(priority=1)
            return carry

        lax.fori_loop(0, tm, body, 0, unroll=8)

    @pl.when(i == 0)
    def _():
        gather(0)

    @pl.when(i + 1 < pl.num_programs(0))
    def _():
        gather(i + 1)

    buf = ybuf.at[i % 2]
    for k in range(TOP_K):
        _row_copy_wait(ys_ref.at[pl.ds(0, tm * ROW_TILES)], buf.at[k], sem.at[i % 2])
    m = meta_ref[...]
    moe = m[:, 0:1] * _from_token_tiles(buf.at[0]) + m[:, 1:2] * _from_token_tiles(buf.at[1])
    h2 = h1_ref[...] + mod_ref[5:6, :] * moe
    o_ref[...] = _rms(h2) * fn_ref[...]


def _final(d0, d1, h1, meta, mod3, final_norm, ys):
    tm = TM_PROJ
    per_b = SEQ // tm
    return pl.pallas_call(
        _final_kernel,
        grid_spec=pltpu.PrefetchScalarGridSpec(
            num_scalar_prefetch=2,
            grid=(N_TOK // tm,),
            in_specs=[pl.BlockSpec((tm, D_MODEL), lambda i, *_: (i, 0)),
                      pl.BlockSpec((tm, LANES), lambda i, *_: (i, 0)),
                      pl.BlockSpec((None, 6, D_MODEL), lambda i, *_: (i // per_b, 0, 0)),
                      pl.BlockSpec((1, D_MODEL), lambda i, *_: (0, 0)),
                      pl.BlockSpec(memory_space=pl.ANY)],
            out_specs=pl.BlockSpec((tm, D_MODEL), lambda i, *_: (i, 0)),
            scratch_shapes=[pltpu.VMEM((2, TOP_K, tm * ROW_TILES, LANES), F32), pltpu.SemaphoreType.DMA((2,))]),
        out_shape=jax.ShapeDtypeStruct((N_TOK, D_MODEL), F32),
        compiler_params=_params("arbitrary"),
        name="final",
    )(d0, d1, h1, meta, mod3, final_norm, ys)


def _slot_layout(counts):
    cnt = counts[:, 0].astype(jnp.int32)
    tile_end = jnp.cumsum((cnt + TE - 1) // TE)
    seg = jnp.concatenate([jnp.zeros((1,), jnp.int32), tile_end * TE])
    off_col = jnp.broadcast_to(seg[:-1].astype(F32)[:, None], (N_EXPERTS, LANES))
    tile_ids = jnp.arange(N_TILES, dtype=jnp.int32)
    tile_expert = jnp.sum((tile_end[None, :] <= tile_ids[:, None]).astype(jnp.int32), axis=1)
    tile_expert = jnp.minimum(tile_expert, N_EXPERTS - 1)
    return cnt, seg, off_col, tile_expert, tile_end[-1:]


def _rope_tables(dim):
    pos = jnp.arange(SEQ, dtype=F32)
    inv = ROPE_THETA ** (-jnp.arange(0, dim, 2, dtype=F32) / dim)
    ang = pos[:, None] * inv[None, :]
    return jnp.cos(ang), jnp.sin(ang)


def _decay_tables():
    c = RET_CHUNK
    log_gamma = jnp.log1p(-jnp.exp2(-5.0 - jnp.arange(RET_HEADS, dtype=F32)))
    idx = jnp.arange(c, dtype=F32)
    rel = idx[:, None] - idx[None, :]
    dec = jnp.where(rel[None] >= 0, jnp.exp(log_gamma[:, None, None] * jnp.maximum(rel, 0.0)[None]), 0.0)
    xi = jnp.exp(log_gamma[:, None] * (idx[None, :] + 1.0))[:, :, None]
    zeta = jnp.exp(log_gamma[:, None] * (c - 1.0 - idx[None, :]))[:, :, None]
    cd = jnp.exp(log_gamma * c)[:, None, None]
    return dec, xi, zeta, cd


def _rotate_half_cols(w):
    half = w.shape[-1] // 2
    return jnp.concatenate([-w[..., half:], w[..., :half]], axis=-1)


def kernel(x, c, w_ada, b_ada, norm1, norm2, w_in, w_ret_o, q_norm, kv_norm, w_uq, w_ukv, w_mla_o, w_out,
           w_grp, b_grp, w_exp, b_exp, w1, w3, w2, final_norm):
    assert x.shape == (BATCH, SEQ, D_MODEL) and w_ada.shape[0] == 1
    x2 = x.reshape(N_TOK, D_MODEL)

    w_ret, w_lat, w_gate = _wprep(w_in[0])
    wq = w_uq[0].reshape(MLA_Q_LORA, MLA_HEADS, MLA_QK)
    wq = jnp.concatenate([wq, _rotate_half_cols(wq[..., MLA_NOPE:])], axis=-1)
    wq = wq.transpose(1, 0, 2).astype(BF16)
    wkv = w_ukv[0].reshape(MLA_KV_LORA, MLA_HEADS, MLA_NOPE + MLA_V).transpose(1, 0, 2).astype(BF16)
    gap = jnp.zeros((SUBLANES - N_GROUPS, D_MODEL), F32)
    tail = jnp.zeros((ROUTER_ROWS - SUBLANES - N_EXPERTS, D_MODEL), F32)
    w_rt = jnp.concatenate([w_grp[0].T, gap, w_exp[0].T, tail], axis=0)
    b_rt = jnp.concatenate([b_grp[0], gap[:, 0], b_exp[0], tail[:, 0]])
    b_rt = jnp.broadcast_to(b_rt[:, None], (ROUTER_ROWS, LANES))

    ret_cos, ret_sin = _rope_tables(RET_DK)
    mla_cos, mla_sin = _rope_tables(MLA_ROPE)
    mla_cos = jnp.concatenate([mla_cos, mla_cos], axis=-1)
    mla_sin = jnp.concatenate([mla_sin, mla_sin], axis=-1)
    dec, xi, zeta, cd = _decay_tables()

    mod3 = _ada(c, w_ada[0], b_ada[0]).reshape(BATCH, 6, D_MODEL)
    ret, lat, gates = _inproj(x2, mod3, norm1, ret_cos, ret_sin, w_ret, w_lat, w_gate)
    y_ret = _retention(ret.reshape(BATCH, SEQ, 4 * RET_W), dec, xi, zeta, cd)
    attn = _mla(lat.reshape(BATCH, SEQ, MLA_LAT_W), q_norm, kv_norm, wq, wkv, mla_cos, mla_sin)
    h1, u2t, meta_t, wtok, counts = _merge(y_ret.reshape(N_TOK, D_MODEL), attn.reshape(N_TOK, D_MODEL), gates, x2,
                                           mod3, norm2, w_ret_o[0].astype(BF16), w_mla_o[0].astype(BF16),
                                           w_out[0].astype(BF16), w_rt, b_rt)
    cnt, seg, off_col, tile_expert, n_valid = _slot_layout(counts)
    dst = _plan(meta_t, off_col)
    d0, d1 = dst[0], dst[1]
    xs = _dispatch(d0, d1, seg, cnt, u2t)
    e_shape = (N_EXPERTS, D_MODEL, D_EXPERT)
    ys = _experts(tile_expert, n_valid, xs, w1[0].reshape(e_shape), w3[0].reshape(e_shape),
                  w2[0].reshape(N_EXPERTS, D_EXPERT, D_MODEL))
    out = _final(d0, d1, h1, wtok, mod3, final_norm.reshape(1, D_MODEL), ys)
    return out.reshape(BATCH, SEQ, D_MODEL)
```

```python
import jax
import jax.numpy as jnp
from jax import lax
from jax.experimental import pallas as pl
from jax.experimental.pallas import tpu as pltpu

D_MODEL = 1024
BATCH = 8
SEQ = 2048
N_TOK = BATCH * SEQ

RET_HEADS = 4
RET_DK = 256
RET_DV = 256
RET_CHUNK = 256
RET_BLK = 512
RET_W = RET_HEADS * RET_DK

MLA_HEADS = 8
MLA_NOPE = 128
MLA_ROPE = 64
MLA_V = 128
MLA_Q_LORA = 384
MLA_KV_LORA = 256
MLA_LAT_W = MLA_Q_LORA + MLA_KV_LORA + 2 * MLA_ROPE
MLA_QK = MLA_NOPE + MLA_ROPE
ROPE_THETA = 10000.0

N_GROUPS = 4
EXPERTS_PER_GROUP = 8
N_EXPERTS = N_GROUPS * EXPERTS_PER_GROUP
D_EXPERT = 256
EPS = 1e-6
LOG2_E = 1.4426950408889634

LANES = 128
SUBLANES = 8
ROUTER_ROWS = 48
ROW_TILES = D_MODEL // LANES
VMEM_LIMIT = 56 * 1024 * 1024

TM_PROJ = 512
TQ = 256
MLA_HPS = 2
TE = 512
TE_CHAINS = 2
MERGE_CHAINS = 2
TOP_K = 2
N_TILES = N_TOK * TOP_K // TE + N_EXPERTS
N_SLOTS = N_TILES * TE

F32 = jnp.float32
BF16 = jnp.bfloat16


def _sigmoid(x):
    return 1.0 / (1.0 + jnp.exp(-x))


def _rms(x):
    return x * lax.rsqrt(jnp.mean(x * x, axis=-1, keepdims=True) + EPS)


def _dot(a, b):
    return jnp.dot(a, b, preferred_element_type=F32)


def _dot_nt(a, b):
    return lax.dot_general(a, b, (((1,), (1,)), ((), ())), preferred_element_type=F32)


def _dot_tn(a, b):
    return lax.dot_general(a, b, (((0,), (0,)), ((), ())), preferred_element_type=F32)


def _params(*sem):
    return pltpu.CompilerParams(dimension_semantics=sem, vmem_limit_bytes=VMEM_LIMIT)


def _resident(shape):
    nd = len(shape)
    return pl.BlockSpec(shape, lambda *_: (0,) * nd, pipeline_mode=pl.Buffered(1))


def _ada_kernel(c_ref, w_ref, b_ref, o_ref):
    c = c_ref[...]
    act = (c * _sigmoid(c)).astype(BF16)
    o_ref[...] = _dot(act, w_ref[...].astype(BF16)) + b_ref[...]


def _ada(c, w_ada, b_ada):
    n = w_ada.shape[1]
    tn = D_MODEL
    return pl.pallas_call(
        _ada_kernel,
        grid=(n // tn,),
        in_specs=[pl.BlockSpec((BATCH, D_MODEL), lambda j: (0, 0)),
                  pl.BlockSpec((D_MODEL, tn), lambda j: (0, j)),
                  pl.BlockSpec((1, tn), lambda j: (0, j))],
        out_specs=pl.BlockSpec((BATCH, tn), lambda j: (0, j)),
        out_shape=jax.ShapeDtypeStruct((BATCH, n), F32),
        compiler_params=_params("arbitrary"),
        name="ada",
    )(c, w_ada, b_ada.reshape(1, n))


O_LAT = 4 * RET_W
O_PE = O_LAT + MLA_Q_LORA + MLA_KV_LORA
O_GATE = O_PE + MLA_ROPE


N_IN = O_GATE + 2 * D_MODEL
WPREP_ROWS = 512


def _wprep_kernel(w_hbm, ret_ref, lat_ref, gate_ref, buf, sem):
    chunks = [(s, min(WPREP_ROWS, N_IN - s)) for s in range(0, N_IN, WPREP_ROWS)]

    def copy(i):
        s, n = chunks[i]
        return pltpu.make_async_copy(w_hbm.at[pl.ds(s, n)], buf.at[i % 2, pl.ds(0, n)], sem.at[i % 2])

    groups = [(0, RET_W, ret_ref, 0, 1.0), (RET_W, 2 * RET_W, ret_ref, RET_W, RET_DK ** -0.5),
              (2 * RET_W, O_LAT, ret_ref, 2 * RET_W, 1.0), (O_LAT, O_GATE, lat_ref, 0, 1.0),
              (O_GATE, N_IN, gate_ref, 0, 1.0)]
    half = MLA_ROPE // 2
    rot = [(O_PE + half, O_GATE, O_GATE - O_LAT, -1.0), (O_PE, O_PE + half, O_GATE - O_LAT + half, 1.0)]

    copy(0).start()
    for i, (s, n) in enumerate(chunks):
        if i + 1 < len(chunks):
            copy(i + 1).start()
        copy(i).wait()
        for lo, hi, dst, dst_lo, scale in groups + [(a, b, lat_ref, d, sc) for a, b, d, sc in rot]:
            a, b = max(lo, s), min(hi, s + n)
            if a < b:
                x = buf[i % 2, a - s:b - s, :]
                dst[dst_lo + a - lo:dst_lo + b - lo, :] = (x if scale == 1.0 else x * scale).astype(BF16)


def _wprep(w_in_t):
    out_rows = (O_LAT, MLA_LAT_W, 2 * D_MODEL)
    whole = lambda n: pl.BlockSpec((n, D_MODEL), lambda i: (0, 0))
    return pl.pallas_call(
        _wprep_kernel,
        grid=(1,),
        in_specs=[pl.BlockSpec(memory_space=pl.ANY)],
        out_specs=[whole(n) for n in out_rows],
        out_shape=[jax.ShapeDtypeStruct((n, D_MODEL), BF16) for n in out_rows],
        scratch_shapes=[pltpu.VMEM((2, WPREP_ROWS, D_MODEL), F32), pltpu.SemaphoreType.DMA((2,))],
        compiler_params=_params("arbitrary"),
        name="wprep",
    )(w_in_t)


def _inproj_kernel(x_ref, mod_ref, n1_ref, cos_ref, sin_ref, wr_ref, wm_ref, wg_ref, ret_ref, lat_ref, gate_ref):
    y = _rms(x_ref[...]) * n1_ref[...]
    u = (y * (1.0 + mod_ref[1:2, :]) + mod_ref[0:1, :]).astype(BF16)
    cos, sin = cos_ref[...], sin_ref[...]
    half = RET_DK // 2
    for n in range(0, 2 * RET_W, RET_DK):
        p = _dot_nt(u, wr_ref[n:n + RET_DK, :])
        x1, x2 = p[:, :half], p[:, half:]
        ret_ref[:, n:n + half] = (x1 * cos - x2 * sin).astype(BF16)
        ret_ref[:, n + half:n + RET_DK] = (x2 * cos + x1 * sin).astype(BF16)
    step = 512
    for n in range(2 * RET_W, 3 * RET_W, step):
        ret_ref[:, n:n + step] = _dot_nt(u, wr_ref[n:n + step, :]).astype(BF16)
    for n in range(3 * RET_W, 4 * RET_W, step):
        p = _dot_nt(u, wr_ref[n:n + step, :])
        ret_ref[:, n:n + step] = (p * _sigmoid(p)).astype(BF16)
    lat_ref[...] = _dot_nt(u, wm_ref[...]).astype(BF16)
    for n in range(0, 2 * D_MODEL, step):
        gate_ref[:, n:n + step] = _dot_nt(u, wg_ref[n:n + step, :]).astype(BF16)


def _inproj(x2, mod3, norm1, cos, sin, w_ret, w_lat, w_gate):
    tm = TM_PROJ
    per_b = SEQ // tm
    rope_tab = pl.BlockSpec((tm, RET_DK // 2), lambda i: (i % per_b, 0))
    return pl.pallas_call(
        _inproj_kernel,
        grid=(N_TOK // tm,),
        in_specs=[pl.BlockSpec((tm, D_MODEL), lambda i: (i, 0)),
                  pl.BlockSpec((None, 6, D_MODEL), lambda i: (i // per_b, 0, 0)),
                  _resident((1, D_MODEL)), rope_tab, rope_tab,
                  _resident(w_ret.shape), _resident(w_lat.shape), _resident(w_gate.shape)],
        out_specs=[pl.BlockSpec((tm, 4 * RET_W), lambda i: (i, 0)),
                   pl.BlockSpec((tm, MLA_LAT_W), lambda i: (i, 0)),
                   pl.BlockSpec((tm, 2 * D_MODEL), lambda i: (i, 0))],
        out_shape=[jax.ShapeDtypeStruct((N_TOK, 4 * RET_W), BF16),
                   jax.ShapeDtypeStruct((N_TOK, MLA_LAT_W), BF16),
                   jax.ShapeDtypeStruct((N_TOK, 2 * D_MODEL), BF16)],
        compiler_params=_params("arbitrary"),
        name="inproj",
    )(x2, mod3, norm1, cos, sin, w_ret, w_lat, w_gate)


def _ret_kernel(q_ref, k_ref, v_ref, g_ref, dec_ref, xi_ref, zeta_ref, cd_ref, o_ref, state_ref):
    @pl.when(pl.program_id(1) == 0)
    def _():
        state_ref[...] = jnp.zeros_like(state_ref)

    heads = range(RET_HEADS)
    col = lambda h: slice(h * RET_DK, (h + 1) * RET_DK)
    for c in range(RET_BLK // RET_CHUNK):
        rows = slice(c * RET_CHUNK, (c + 1) * RET_CHUNK)
        scores = [(_dot_nt(q_ref[rows, col(h)], k_ref[rows, col(h)]) * dec_ref[h]).astype(BF16) for h in heads]
        carried = [xi_ref[h] * _dot(q_ref[rows, col(h)], state_ref[h].astype(BF16)) for h in heads]
        y = [_dot(scores[h], v_ref[rows, col(h)]) + carried[h] for h in heads]
        for h in heads:
            k_dec = (k_ref[rows, col(h)].astype(F32) * zeta_ref[h]).astype(BF16)
            state_ref[h] = state_ref[h] * cd_ref[h] + _dot_tn(k_dec, v_ref[rows, col(h)])
        yc = [y[h] - jnp.mean(y[h], axis=-1, keepdims=True) for h in heads]
        inv = [lax.rsqrt(jnp.mean(yc[h] * yc[h], axis=-1, keepdims=True) + EPS) for h in heads]
        for h in heads:
            o_ref[rows, col(h)] = (g_ref[rows, col(h)].astype(F32) * (yc[h] * inv[h])).astype(BF16)


def _retention(ret3, dec, xi, zeta, cd):
    blk = lambda part: pl.BlockSpec((None, RET_BLK, RET_W), lambda b, j: (b, j, part))
    whole = lambda a: pl.BlockSpec(a.shape, lambda b, j: (0,) * a.ndim)
    return pl.pallas_call(
        _ret_kernel,
        grid=(BATCH, SEQ // RET_BLK),
        in_specs=[blk(0), blk(1), blk(2), blk(3), whole(dec), whole(xi), whole(zeta), whole(cd)],
        out_specs=pl.BlockSpec((None, RET_BLK, RET_HEADS * RET_DV), lambda b, j: (b, j, 0)),
        out_shape=jax.ShapeDtypeStruct((BATCH, SEQ, RET_HEADS * RET_DV), BF16),
        scratch_shapes=[pltpu.VMEM((RET_HEADS, RET_DK, RET_DV), F32)],
        compiler_params=_params("arbitrary", "arbitrary"),
        name="retention",
    )(ret3, ret3, ret3, ret3, dec, xi, zeta, cd)


def _mla_kernel(lat_ref, qn_ref, kvn_ref, wq_ref, wkv_ref, cos_ref, sin_ref, o_ref,
                cq_s, ckv_s, kpe_s, q_s, k_s, v_s):
    h = pl.program_id(1)
    o_q, o_kv, o_pe, o_rot = 0, MLA_Q_LORA, MLA_Q_LORA + MLA_KV_LORA, MLA_Q_LORA + MLA_KV_LORA + MLA_ROPE
    cos, sin = cos_ref[...], sin_ref[...]

    @pl.when(h == 0)
    def _():
        cq_s[...] = (_rms(lat_ref[:, o_q:o_kv].astype(F32)) * qn_ref[...]).astype(BF16)
        ckv_s[...] = (_rms(lat_ref[:, o_kv:o_pe].astype(F32)) * kvn_ref[...]).astype(BF16)
        pe = lat_ref[:, o_pe:o_rot].astype(F32)
        rot = lat_ref[:, o_rot:o_rot + MLA_ROPE].astype(F32)
        kpe_s[...] = (pe * cos + rot * sin).astype(BF16)

    scale = (MLA_QK ** -0.5) * LOG2_E
    for g in range(MLA_HPS):
        qf = _dot(cq_s[...], wq_ref[g])
        q_s[g, :, :MLA_NOPE] = (qf[:, :MLA_NOPE] * scale).astype(BF16)
        q_pe = qf[:, MLA_NOPE:MLA_QK] * cos + qf[:, MLA_QK:] * sin
        q_s[g, :, MLA_NOPE:] = (q_pe * scale).astype(BF16)
        kvf = _dot(ckv_s[...], wkv_ref[g])
        k_s[g, :, :MLA_NOPE] = kvf[:, :MLA_NOPE].astype(BF16)
        k_s[g, :, MLA_NOPE:] = kpe_s[...]
        v_s[g, :, :MLA_V] = kvf[:, MLA_NOPE:].astype(BF16)
        v_s[g, :, MLA_V:] = jnp.ones((SEQ, MLA_V), BF16)

    causal = lax.broadcasted_iota(jnp.int32, (TQ, TQ), 0) >= lax.broadcasted_iota(jnp.int32, (TQ, TQ), 1)
    heads = range(MLA_HPS)
    n_blk = SEQ // TQ

    def scores(i):
        lo, hi = i * TQ, (i + 1) * TQ
        diag = [jnp.where(causal, _dot_nt(q_s[g, lo:hi, :], k_s[g, lo:hi, :]), -jnp.inf) for g in heads]
        past = [_dot_nt(q_s[g, lo:hi, :], k_s[g, :lo, :]) if i > 0 else None for g in heads]
        return diag, past

    pending = scores(0)
    for i in range(n_blk):
        lo, hi = i * TQ, (i + 1) * TQ
        diag, past = pending
        if i + 1 < n_blk:
            pending = scores(i + 1)
        m = [jnp.max(diag[g], axis=-1, keepdims=True) for g in heads]
        if i > 0:
            m = [jnp.maximum(m[g], jnp.max(past[g], axis=-1, keepdims=True)) for g in heads]
        acc = [_dot(jnp.exp2(diag[g] - m[g]).astype(BF16), v_s[g, lo:hi, :]) for g in heads]
        if i > 0:
            acc = [acc[g] + _dot(jnp.exp2(past[g] - m[g]).astype(BF16), v_s[g, :lo, :]) for g in heads]
        for g in heads:
            o_ref[lo:hi, g * MLA_V:(g + 1) * MLA_V] = (acc[g][:, :MLA_V] / acc[g][:, MLA_V:]).astype(BF16)


def _mla(lat3, q_norm, kv_norm, wq, wkv, cos, sin):
    hps = MLA_HPS
    return pl.pallas_call(
        _mla_kernel,
        grid=(BATCH, MLA_HEADS // hps),
        in_specs=[pl.BlockSpec((None, SEQ, MLA_LAT_W), lambda b, h: (b, 0, 0)),
                  pl.BlockSpec((1, MLA_Q_LORA), lambda b, h: (0, 0)),
                  pl.BlockSpec((1, MLA_KV_LORA), lambda b, h: (0, 0)),
                  pl.BlockSpec((hps, MLA_Q_LORA, MLA_QK + MLA_ROPE), lambda b, h: (h, 0, 0)),
                  pl.BlockSpec((hps, MLA_KV_LORA, MLA_NOPE + MLA_V), lambda b, h: (h, 0, 0)),
                  pl.BlockSpec((SEQ, MLA_ROPE), lambda b, h: (0, 0)),
                  pl.BlockSpec((SEQ, MLA_ROPE), lambda b, h: (0, 0))],
        out_specs=pl.BlockSpec((None, SEQ, hps * MLA_V), lambda b, h: (b, 0, h)),
        out_shape=jax.ShapeDtypeStruct((BATCH, SEQ, MLA_HEADS * MLA_V), BF16),
        scratch_shapes=[pltpu.VMEM((SEQ, MLA_Q_LORA), BF16),
                        pltpu.VMEM((SEQ, MLA_KV_LORA), BF16),
                        pltpu.VMEM((SEQ, MLA_ROPE), BF16),
                        pltpu.VMEM((hps, SEQ, MLA_QK), BF16),
                        pltpu.VMEM((hps, SEQ, MLA_QK), BF16),
                        pltpu.VMEM((hps, SEQ, 2 * MLA_V), BF16)],
        compiler_params=_params("arbitrary", "arbitrary"),
        name="mla",
    )(lat3, q_norm, kv_norm, wq, wkv, cos, sin)


def _route(logits_t):
    tm = logits_t.shape[1]
    row = lax.broadcasted_iota(jnp.int32, (SUBLANES, tm), 0)
    neg = -jnp.inf
    gl = jnp.where(row < N_GROUPS, logits_t[:SUBLANES], neg)
    gmax = jnp.max(gl, axis=0, keepdims=True)
    gsel = jnp.min(jnp.where(gl == gmax, row, SUBLANES), axis=0, keepdims=True)
    p_grp = 1.0 / jnp.sum(jnp.exp(gl - gmax), axis=0, keepdims=True)
    el = logits_t[SUBLANES * N_GROUPS:SUBLANES * (N_GROUPS + 1)]
    for g in reversed(range(N_GROUPS - 1)):
        el = jnp.where(gsel == g, logits_t[SUBLANES * (g + 1):SUBLANES * (g + 2)], el)
    v0 = jnp.max(el, axis=0, keepdims=True)
    i0 = jnp.min(jnp.where(el == v0, row, SUBLANES), axis=0, keepdims=True)
    el1 = jnp.where(row == i0, neg, el)
    v1 = jnp.max(el1, axis=0, keepdims=True)
    i1 = jnp.min(jnp.where(el1 == v1, row, SUBLANES), axis=0, keepdims=True)
    t = jnp.exp(v1 - v0)
    w0 = p_grp / (1.0 + t)
    w1 = p_grp * t / (1.0 + t)
    return gsel * EXPERTS_PER_GROUP + i0, gsel * EXPERTS_PER_GROUP + i1, w0, w1


def _stack_rows(rows, n):
    tm = rows[0].shape[1]
    row = lax.broadcasted_iota(jnp.int32, (n, tm), 0)
    out = jnp.zeros((n, tm), F32)
    for k, r in enumerate(rows):
        out = jnp.where(row == k, r, out)
    return out


def _to_token_tiles(ref, val):
    n = val.shape[0]
    for s in range(ROW_TILES):
        ref[pl.ds(s, n, stride=ROW_TILES), :] = val[:, s * LANES:(s + 1) * LANES]


def _from_token_tiles(ref):
    n = ref.shape[0] // ROW_TILES
    return jnp.concatenate([ref[pl.ds(s, n, stride=ROW_TILES), :] for s in range(ROW_TILES)], axis=-1)


def _token_rows(ref, t):
    return ref.at[pl.ds(pl.multiple_of(t * ROW_TILES, ROW_TILES), ROW_TILES)]


def _merge_kernel(yr_ref, at_ref, gr_ref, gm_ref, x_ref, mod_ref, n2_ref, wro_ref, wmo_ref, wo_ref,
                  wrt_ref, brt_ref, h1_ref, u2_ref, meta_ref, wtok_ref, cnt_ref, carry_ref):
    tm = x_ref.shape[0]

    @pl.when(pl.program_id(0) == 0)
    def _():
        carry_ref[...] = jnp.zeros_like(carry_ref)

    chunks = range(MERGE_CHAINS)
    sub = tm // MERGE_CHAINS
    rows = [slice(c * sub, (c + 1) * sub) for c in chunks]
    y_ret = [_dot(yr_ref[r, :], wro_ref[...]) for r in rows]
    y_mla = [_dot(at_ref[r, :], wmo_ref[...]) for r in rows]
    merged = [(_sigmoid(gr_ref[rows[c], :].astype(F32)) * y_ret[c]
               + _sigmoid(gm_ref[rows[c], :].astype(F32)) * y_mla[c]).astype(BF16) for c in chunks]
    o = [_dot(merged[c], wo_ref[...]) for c in chunks]
    h1 = [x_ref[rows[c], :] + mod_ref[2:3, :] * o[c] for c in chunks]
    for c in chunks:
        h1_ref[rows[c], :] = h1[c]
    u2 = [_rms(h1[c]) * n2_ref[...] * (1.0 + mod_ref[4:5, :]) + mod_ref[3:4, :] for c in chunks]
    for c in chunks:
        _to_token_tiles(u2_ref.at[pl.ds(c * sub * ROW_TILES, sub * ROW_TILES)], u2[c])
    w = wrt_ref[...]
    w_hi = w.astype(BF16)
    w_lo = (w - w_hi.astype(F32)).astype(BF16)
    u_hi = [u2[c].astype(BF16) for c in chunks]
    u_lo = [(u2[c] - u_hi[c].astype(F32)).astype(BF16) for c in chunks]
    logits_t = [_dot_nt(w_hi, u_hi[c]) + _dot_nt(w_hi, u_lo[c]) + _dot_nt(w_lo, u_hi[c]) + brt_ref[:, 0:1]
                for c in chunks]
    routed = [_route(logits_t[c]) for c in chunks]
    e0, e1, w0, w1 = [jnp.concatenate([routed[c][k] for c in chunks], axis=1) for k in range(4)]
    erow = lax.broadcasted_iota(jnp.int32, (N_EXPERTS, tm), 0)
    m0, m1 = erow == e0, erow == e1
    member = jnp.where(m0 | m1, 1.0, 0.0)
    earlier = jnp.where(lax.broadcasted_iota(jnp.int32, (tm, tm), 0) < lax.broadcasted_iota(jnp.int32, (tm, tm), 1),
                        1.0, 0.0).astype(BF16)
    prefix = _dot(member.astype(BF16), earlier) + carry_ref[:, 0:1]
    rank0 = jnp.sum(jnp.where(m0, prefix, 0.0), axis=0, keepdims=True)
    rank1 = jnp.sum(jnp.where(m1, prefix, 0.0), axis=0, keepdims=True)
    carry_ref[...] = carry_ref[...] + jnp.sum(member, axis=1, keepdims=True)
    cnt_ref[...] = carry_ref[...]
    meta_ref[...] = _stack_rows([e0.astype(F32), e1.astype(F32), rank0, rank1], SUBLANES)
    wt = _stack_rows([w0, w1], 2 * SUBLANES)
    wt_hi = wt.astype(BF16)
    wt_lo = (wt - wt_hi.astype(F32)).astype(BF16)
    place = jnp.where(lax.broadcasted_iota(jnp.int32, (2 * SUBLANES, LANES), 0)
                      == lax.broadcasted_iota(jnp.int32, (2 * SUBLANES, LANES), 1), 1.0, 0.0).astype(BF16)
    wtok_ref[...] = _dot_tn(wt_hi, place) + _dot_tn(wt_lo, place)


def _merge(y_ret, attn, gates, x2, mod3, norm2, w_ret_o, w_mla_o, w_out, w_rt, b_rt):
    tm = TM_PROJ
    per_b = SEQ // tm
    row = lambda j: pl.BlockSpec((tm, D_MODEL), lambda i: (i, j))
    return pl.pallas_call(
        _merge_kernel,
        grid=(N_TOK // tm,),
        in_specs=[row(0), row(0), row(0), row(1), row(0),
                  pl.BlockSpec((None, 6, D_MODEL), lambda i: (i // per_b, 0, 0)),
                  _resident((1, D_MODEL)),
                  _resident(w_ret_o.shape), _resident(w_mla_o.shape), _resident(w_out.shape),
                  _resident(w_rt.shape), _resident(b_rt.shape)],
        out_specs=[row(0),
                   pl.BlockSpec((tm * ROW_TILES, LANES), lambda i: (i, 0)),
                   pl.BlockSpec((SUBLANES, tm), lambda i: (0, i)),
                   pl.BlockSpec((tm, LANES), lambda i: (i, 0)),
                   pl.BlockSpec((N_EXPERTS, LANES), lambda i: (0, 0))],
        out_shape=[jax.ShapeDtypeStruct((N_TOK, D_MODEL), F32),
                   jax.ShapeDtypeStruct((N_TOK * ROW_TILES, LANES), F32),
                   jax.ShapeDtypeStruct((SUBLANES, N_TOK), F32),
                   jax.ShapeDtypeStruct((N_TOK, LANES), F32),
                   jax.ShapeDtypeStruct((N_EXPERTS, LANES), F32)],
        scratch_shapes=[pltpu.VMEM((N_EXPERTS, LANES), F32)],
        compiler_params=_params("arbitrary"),
        name="merge",
    )(y_ret, attn, gates, gates, x2, mod3, norm2, w_ret_o, w_mla_o, w_out, w_rt, b_rt)


def _plan_kernel(meta_ref, off_ref, dst_ref):
    m = meta_ref[...]
    erow = lax.broadcasted_iota(jnp.int32, (N_EXPERTS, m.shape[1]), 0)
    off = off_ref[:, 0:1]
    d = [jnp.sum(jnp.where(erow == m[k:k + 1].astype(jnp.int32), off, 0.0), axis=0, keepdims=True) + m[k + 2:k + 3]
         for k in range(TOP_K)]
    dst_ref[...] = _stack_rows(d, SUBLANES).astype(jnp.int32)


def _plan(meta_t, off_col):
    tm = 2048
    return pl.pallas_call(
        _plan_kernel,
        grid=(N_TOK // tm,),
        in_specs=[pl.BlockSpec((SUBLANES, tm), lambda i: (0, i)),
                  pl.BlockSpec((N_EXPERTS, LANES), lambda i: (0, 0))],
        out_specs=pl.BlockSpec((SUBLANES, tm), lambda i: (0, i)),
        out_shape=jax.ShapeDtypeStruct((SUBLANES, N_TOK), jnp.int32),
        compiler_params=_params("arbitrary"),
        name="plan",
    )(meta_t, off_col)


def _row_copy_wait(src_like, dst_like, sem):
    pltpu.make_async_copy(src_like, dst_like, sem).wait()


def _dispatch_kernel(d0_ref, d1_ref, seg_ref, cnt_ref, u_ref, xs_ref, zero_ref, sem, zsem):
    i = pl.program_id(0)
    tm = u_ref.shape[0] // ROW_TILES
    tile_rows = TE * ROW_TILES

    def slot_tile(j):
        return xs_ref.at[pl.ds(pl.multiple_of(j * tile_rows, tile_rows), tile_rows)]

    @pl.when(i == 0)
    def _():
        zero_ref[...] = jnp.zeros_like(zero_ref)

        def pad(e):
            first = seg_ref[e] + cnt_ref[e]
            rows = (seg_ref[e + 1] - first) * ROW_TILES
            start = pl.multiple_of(first * ROW_TILES, ROW_TILES)
            return rows > 0, pltpu.make_async_copy(zero_ref.at[pl.ds(0, rows)], xs_ref.at[pl.ds(start, rows)], zsem)

        for e in range(N_EXPERTS):
            nonempty, copy = pad(e)
            pl.when(nonempty)(copy.start)
        for e in range(N_EXPERTS):
            nonempty, copy = pad(e)
            pl.when(nonempty)(copy.wait)

        def unused(j):
            return pltpu.make_async_copy(zero_ref, slot_tile(j), zsem)

        first_unused = seg_ref[N_EXPERTS] // TE
        lax.fori_loop(first_unused, N_TILES, lambda j, c: (unused(j).start(), c)[1], 0)
        lax.fori_loop(first_unused, N_TILES, lambda j, c: (unused(j).wait(), c)[1], 0)

    base = i * tm

    def body(r, carry):
        src = _token_rows(u_ref, r)
        pltpu.make_async_copy(src, _token_rows(xs_ref, d0_ref[base + r]), sem).start(priority=0)
        pltpu.make_async_copy(src, _token_rows(xs_ref, d1_ref[base + r]), sem).start(priority=1)
        return carry

    lax.fori_loop(0, tm, body, 0, unroll=8)
    _row_copy_wait(u_ref, xs_ref.at[pl.ds(0, tm * ROW_TILES)], sem)
    _row_copy_wait(u_ref, xs_ref.at[pl.ds(0, tm * ROW_TILES)], sem)


def _dispatch(d0, d1, seg, cnt, u2t):
    tm = TM_PROJ
    return pl.pallas_call(
        _dispatch_kernel,
        grid_spec=pltpu.PrefetchScalarGridSpec(
            num_scalar_prefetch=4,
            grid=(N_TOK // tm,),
            in_specs=[pl.BlockSpec((tm * ROW_TILES, LANES), lambda i, *_: (i, 0))],
            out_specs=pl.BlockSpec(memory_space=pl.ANY),
            scratch_shapes=[pltpu.VMEM((TE * ROW_TILES, LANES), F32),
                            pltpu.SemaphoreType.DMA(()), pltpu.SemaphoreType.DMA(())]),
        out_shape=jax.ShapeDtypeStruct((N_SLOTS * ROW_TILES, LANES), F32),
        compiler_params=_params("arbitrary"),
        name="dispatch",
    )(d0, d1, seg, cnt, u2t)


def _expert_kernel(te_ref, nv_ref, x_ref, w1_ref, w3_ref, w2_ref, y_ref, w1_s, w3_s, w2_s):
    j = pl.program_id(0)

    @pl.when(j < nv_ref[0])
    def _():
        @pl.when((j == 0) | (te_ref[j] != te_ref[jnp.maximum(j - 1, 0)]))
        def _():
            w1_s[...] = w1_ref[...].astype(BF16)
            w3_s[...] = w3_ref[...].astype(BF16)
            w2_s[...] = w2_ref[...].astype(BF16)

        sub = TE // TE_CHAINS * ROW_TILES
        part = lambda ref, c: ref.at[pl.ds(c * sub, sub)]
        chains = range(TE_CHAINS)
        x = [_from_token_tiles(part(x_ref, c)).astype(BF16) for c in chains]
        a = [_dot(x[c], w1_s[...]) for c in chains]
        b = [_dot(x[c], w3_s[...]) for c in chains]
        hid = [(a[c] * _sigmoid(a[c]) * b[c]).astype(BF16) for c in chains]
        for c in chains:
            y_ref[c * (TE // TE_CHAINS):(c + 1) * (TE // TE_CHAINS), :] = _dot(hid[c], w2_s[...])

    @pl.when(j >= nv_ref[0])
    def _():
        y_ref[...] = jnp.zeros_like(y_ref)


def _experts(tile_expert, n_valid, xs, w1, w3, w2):
    tile = lambda j, te, nv: jnp.minimum(j, nv[0] - 1)
    wspec = lambda shape: pl.BlockSpec((None,) + shape, lambda j, te, nv: (te[tile(j, te, nv)], 0, 0))
    slots = pl.BlockSpec((TE * ROW_TILES, LANES), lambda j, te, nv: (tile(j, te, nv), 0))
    return pl.pallas_call(
        _expert_kernel,
        grid_spec=pltpu.PrefetchScalarGridSpec(
            num_scalar_prefetch=2,
            grid=(N_TILES,),
            in_specs=[slots, wspec((D_MODEL, D_EXPERT)), wspec((D_MODEL, D_EXPERT)), wspec((D_EXPERT, D_MODEL))],
            out_specs=pl.BlockSpec((TE, D_MODEL), lambda j, te, nv: (j, 0)),
            scratch_shapes=[pltpu.VMEM((D_MODEL, D_EXPERT), BF16), pltpu.VMEM((D_MODEL, D_EXPERT), BF16),
                            pltpu.VMEM((D_EXPERT, D_MODEL), BF16)]),
        out_shape=jax.ShapeDtypeStruct((N_SLOTS, D_MODEL), F32),
        compiler_params=_params("arbitrary"),
        name="experts",
    )(tile_expert, n_valid, xs, w1, w3, w2)


def _final_kernel(d0_ref, d1_ref, h1_ref, meta_ref, mod_ref, fn_ref, ys_ref, o_ref, ybuf, sem):
    i = pl.program_id(0)
    tm = h1_ref.shape[0]

    def gather(t):
        buf, s = ybuf.at[t % 2], sem.at[t % 2]
        base = t * tm

        def body(r, carry):
            row = lambda k: buf.at[k, pl.ds(r, 1)]
            pltpu.make_async_copy(ys_ref.at[pl.ds(d0_ref[base + r], 1)], row(0), s).start(priority=0)
            pltpu.make_async_copy(ys_ref.at[pl.ds(d1_ref[base + r], 1)], row(1), s).start(priority=1)
            return carry

        lax.fori_loop(0, tm, body, 0, unroll=8)

    @pl.when(i == 0)
    def _():
        gather(0)

    @pl.when(i + 1 < pl.num_programs(0))
    def _():
        gather(i + 1)

    buf = ybuf.at[i % 2]
    for k in range(TOP_K):
        _row_copy_wait(ys_ref.at[pl.ds(0, tm)], buf.at[k], sem.at[i % 2])
    m = meta_ref[...]
    moe = m[:, 0:1] * buf[0] + m[:, 1:2] * buf[1]
    h2 = h1_ref[...] + mod_ref[5:6, :] * moe
    o_ref[...] = _rms(h2) * fn_ref[...]


def _final(d0, d1, h1, meta, mod3, final_norm, ys):
    tm = TM_PROJ
    per_b = SEQ // tm
    return pl.pallas_call(
        _final_kernel,
        grid_spec=pltpu.PrefetchScalarGridSpec(
            num_scalar_prefetch=2,
            grid=(N_TOK // tm,),
            in_specs=[pl.BlockSpec((tm, D_MODEL), lambda i, *_: (i, 0)),
                      pl.BlockSpec((tm, LANES), lambda i, *_: (i, 0)),
                      pl.BlockSpec((None, 6, D_MODEL), lambda i, *_: (i // per_b, 0, 0)),
                      pl.BlockSpec((1, D_MODEL), lambda i, *_: (0, 0)),
                      pl.BlockSpec(memory_space=pl.ANY)],
            out_specs=pl.BlockSpec((tm, D_MODEL), lambda i, *_: (i, 0)),
            scratch_shapes=[pltpu.VMEM((2, TOP_K, tm, D_MODEL), F32), pltpu.SemaphoreType.DMA((2,))]),
        out_shape=jax.ShapeDtypeStruct((N_TOK, D_MODEL), F32),
        compiler_params=_params("arbitrary"),
        name="final",
    )(d0, d1, h1, meta, mod3, final_norm, ys)


def _slot_layout(counts):
    cnt = counts[:, 0].astype(jnp.int32)
    tile_end = jnp.cumsum((cnt + TE - 1) // TE)
    seg = jnp.concatenate([jnp.zeros((1,), jnp.int32), tile_end * TE])
    off_col = jnp.broadcast_to(seg[:-1].astype(F32)[:, None], (N_EXPERTS, LANES))
    tile_ids = jnp.arange(N_TILES, dtype=jnp.int32)
    tile_expert = jnp.sum((tile_end[None, :] <= tile_ids[:, None]).astype(jnp.int32), axis=1)
    tile_expert = jnp.minimum(tile_expert, N_EXPERTS - 1)
    return cnt, seg, off_col, tile_expert, tile_end[-1:]


def _rope_tables(dim):
    pos = jnp.arange(SEQ, dtype=F32)
    inv = ROPE_THETA ** (-jnp.arange(0, dim, 2, dtype=F32) / dim)
    ang = pos[:, None] * inv[None, :]
    return jnp.cos(ang), jnp.sin(ang)


def _decay_tables():
    c = RET_CHUNK
    log_gamma = jnp.log1p(-jnp.exp2(-5.0 - jnp.arange(RET_HEADS, dtype=F32)))
    idx = jnp.arange(c, dtype=F32)
    rel = idx[:, None] - idx[None, :]
    dec = jnp.where(rel[None] >= 0, jnp.exp(log_gamma[:, None, None] * jnp.maximum(rel, 0.0)[None]), 0.0)
    xi = jnp.exp(log_gamma[:, None] * (idx[None, :] + 1.0))[:, :, None]
    zeta = jnp.exp(log_gamma[:, None] * (c - 1.0 - idx[None, :]))[:, :, None]
    cd = jnp.exp(log_gamma * c)[:, None, None]
    return dec, xi, zeta, cd


def _rotate_half_cols(w):
    half = w.shape[-1] // 2
    return jnp.concatenate([-w[..., half:], w[..., :half]], axis=-1)


def kernel(x, c, w_ada, b_ada, norm1, norm2, w_in, w_ret_o, q_norm, kv_norm, w_uq, w_ukv, w_mla_o, w_out,
           w_grp, b_grp, w_exp, b_exp, w1, w3, w2, final_norm):
    assert x.shape == (BATCH, SEQ, D_MODEL) and w_ada.shape[0] == 1
    x2 = x.reshape(N_TOK, D_MODEL)

    w_ret, w_lat, w_gate = _wprep(jnp.transpose(w_in[0]))
    wq = w_uq[0].reshape(MLA_Q_LORA, MLA_HEADS, MLA_QK)
    wq = jnp.concatenate([wq, _rotate_half_cols(wq[..., MLA_NOPE:])], axis=-1)
    wq = wq.transpose(1, 0, 2).astype(BF16)
    wkv = w_ukv[0].reshape(MLA_KV_LORA, MLA_HEADS, MLA_NOPE + MLA_V).transpose(1, 0, 2).astype(BF16)
    gap = jnp.zeros((SUBLANES - N_GROUPS, D_MODEL), F32)
    tail = jnp.zeros((ROUTER_ROWS - SUBLANES - N_EXPERTS, D_MODEL), F32)
    w_rt = jnp.concatenate([w_grp[0].T, gap, w_exp[0].T, tail], axis=0)
    b_rt = jnp.concatenate([b_grp[0], gap[:, 0], b_exp[0], tail[:, 0]])
    b_rt = jnp.broadcast_to(b_rt[:, None], (ROUTER_ROWS, LANES))

    ret_cos, ret_sin = _rope_tables(RET_DK)
    mla_cos, mla_sin = _rope_tables(MLA_ROPE)
    mla_cos = jnp.concatenate([mla_cos, mla_cos], axis=-1)
    mla_sin = jnp.concatenate([mla_sin, mla_sin], axis=-1)
    dec, xi, zeta, cd = _decay_tables()

    mod3 = _ada(c, w_ada[0], b_ada[0]).reshape(BATCH, 6, D_MODEL)
    ret, lat, gates = _inproj(x2, mod3, norm1, ret_cos, ret_sin, w_ret, w_lat, w_gate)
    y_ret = _retention(ret.reshape(BATCH, SEQ, 4 * RET_W), dec, xi, zeta, cd)
    attn = _mla(lat.reshape(BATCH, SEQ, MLA_LAT_W), q_norm, kv_norm, wq, wkv, mla_cos, mla_sin)
    h1, u2t, meta_t, wtok, counts = _merge(y_ret.reshape(N_TOK, D_MODEL), attn.reshape(N_TOK, D_MODEL), gates, x2,
                                           mod3, norm2, w_ret_o[0].astype(BF16), w_mla_o[0].astype(BF16),
                                           w_out[0].astype(BF16), w_rt, b_rt)
    cnt, seg, off_col, tile_expert, n_valid = _slot_layout(counts)
    dst = _plan(meta_t, off_col)
    d0, d1 = dst[0], dst[1]
    xs = _dispatch(d0, d1, seg, cnt, u2t)
    e_shape = (N_EXPERTS, D_MODEL, D_EXPERT)
    ys = _experts(tile_expert, n_valid, xs, w1[0].reshape(e_shape), w3[0].reshape(e_shape),
                  w2[0].reshape(N_EXPERTS, D_EXPERT, D_MODEL))
    out = _final(d0, d1, h1, wtok, mod3, final_norm.reshape(1, D_MODEL), ys)
    return out.reshape(BATCH, SEQ, D_MODEL)
```

```python
import numpy as np
import jax
import jax.numpy as jnp
from jax import lax
from jax.experimental import pallas as pl
from jax.experimental.pallas import tpu as pltpu

D_MODEL = 1024
BATCH = 8
SEQ = 2048
N_TOK = BATCH * SEQ

RET_HEADS = 4
RET_DK = 256
RET_DV = 256
RET_CHUNK = 256
RET_BLK = 512
RET_W = RET_HEADS * RET_DK

MLA_HEADS = 8
MLA_NOPE = 128
MLA_ROPE = 64
MLA_V = 128
MLA_Q_LORA = 384
MLA_KV_LORA = 256
MLA_LAT_W = MLA_Q_LORA + MLA_KV_LORA + 2 * MLA_ROPE
MLA_QK = MLA_NOPE + MLA_ROPE
ROPE_THETA = 10000.0

N_GROUPS = 4
EXPERTS_PER_GROUP = 8
N_EXPERTS = N_GROUPS * EXPERTS_PER_GROUP
D_EXPERT = 256
EPS = 1e-6
LOG2_E = 1.4426950408889634

LANES = 128
SUBLANES = 8
ROUTER_ROWS = 48
ROW_TILES = D_MODEL // LANES
VMEM_LIMIT = 56 * 1024 * 1024

TM_PROJ = 512
TQ = 256
MLA_HPS = 2
TE = 512
TE_CHAINS = 2
MERGE_CHAINS = 2
TOP_K = 2
N_TILES = N_TOK * TOP_K // TE + N_EXPERTS
N_SLOTS = N_TILES * TE

F32 = jnp.float32
BF16 = jnp.bfloat16


def _sigmoid(x):
    return 1.0 / (1.0 + jnp.exp(-x))


def _rms(x):
    return x * lax.rsqrt(jnp.mean(x * x, axis=-1, keepdims=True) + EPS)


def _dot(a, b):
    return jnp.dot(a, b, preferred_element_type=F32)


def _dot_nt(a, b):
    return lax.dot_general(a, b, (((1,), (1,)), ((), ())), preferred_element_type=F32)


def _dot_tn(a, b):
    return lax.dot_general(a, b, (((0,), (0,)), ((), ())), preferred_element_type=F32)


def _params(*sem):
    return pltpu.CompilerParams(dimension_semantics=sem, vmem_limit_bytes=VMEM_LIMIT)


def _resident(shape):
    nd = len(shape)
    return pl.BlockSpec(shape, lambda *_: (0,) * nd, pipeline_mode=pl.Buffered(1))


def _ada_kernel(c_ref, w_ref, b_ref, o_ref):
    c = c_ref[...]
    act = (c * _sigmoid(c)).astype(BF16)
    o_ref[...] = _dot(act, w_ref[...].astype(BF16)) + b_ref[...]


def _ada(c, w_ada, b_ada):
    n = w_ada.shape[1]
    tn = D_MODEL
    return pl.pallas_call(
        _ada_kernel,
        grid=(n // tn,),
        in_specs=[pl.BlockSpec((BATCH, D_MODEL), lambda j: (0, 0)),
                  pl.BlockSpec((D_MODEL, tn), lambda j: (0, j)),
                  pl.BlockSpec((1, tn), lambda j: (0, j))],
        out_specs=pl.BlockSpec((BATCH, tn), lambda j: (0, j)),
        out_shape=jax.ShapeDtypeStruct((BATCH, n), F32),
        compiler_params=_params("arbitrary"),
        name="ada",
    )(c, w_ada, b_ada.reshape(1, n))


O_LAT = 4 * RET_W
O_PE = O_LAT + MLA_Q_LORA + MLA_KV_LORA
O_GATE = O_PE + MLA_ROPE


N_IN = O_GATE + 2 * D_MODEL
WPREP_ROWS = 512


def _wprep_kernel(w_hbm, ret_ref, lat_ref, gate_ref, buf, sem):
    chunks = [(s, min(WPREP_ROWS, N_IN - s)) for s in range(0, N_IN, WPREP_ROWS)]

    def copy(i):
        s, n = chunks[i]
        return pltpu.make_async_copy(w_hbm.at[pl.ds(s, n)], buf.at[i % 2, pl.ds(0, n)], sem.at[i % 2])

    groups = [(0, RET_W, ret_ref, 0, 1.0), (RET_W, 2 * RET_W, ret_ref, RET_W, RET_DK ** -0.5),
              (2 * RET_W, O_LAT, ret_ref, 2 * RET_W, 1.0), (O_LAT, O_GATE, lat_ref, 0, 1.0),
              (O_GATE, N_IN, gate_ref, 0, 1.0)]
    half = MLA_ROPE // 2
    rot = [(O_PE + half, O_GATE, O_GATE - O_LAT, -1.0), (O_PE, O_PE + half, O_GATE - O_LAT + half, 1.0)]

    copy(0).start()
    for i, (s, n) in enumerate(chunks):
        if i + 1 < len(chunks):
            copy(i + 1).start()
        copy(i).wait()
        for lo, hi, dst, dst_lo, scale in groups + [(a, b, lat_ref, d, sc) for a, b, d, sc in rot]:
            a, b = max(lo, s), min(hi, s + n)
            if a < b:
                x = buf[i % 2, a - s:b - s, :]
                dst[dst_lo + a - lo:dst_lo + b - lo, :] = (x if scale == 1.0 else x * scale).astype(BF16)


def _wprep(w_in_t):
    out_rows = (O_LAT, MLA_LAT_W, 2 * D_MODEL)
    whole = lambda n: pl.BlockSpec((n, D_MODEL), lambda i: (0, 0))
    return pl.pallas_call(
        _wprep_kernel,
        grid=(1,),
        in_specs=[pl.BlockSpec(memory_space=pl.ANY)],
        out_specs=[whole(n) for n in out_rows],
        out_shape=[jax.ShapeDtypeStruct((n, D_MODEL), BF16) for n in out_rows],
        scratch_shapes=[pltpu.VMEM((2, WPREP_ROWS, D_MODEL), F32), pltpu.SemaphoreType.DMA((2,))],
        compiler_params=_params("arbitrary"),
        name="wprep",
    )(w_in_t)


def _inproj_kernel(x_ref, mod_ref, n1_ref, cos_ref, sin_ref, wr_ref, wm_ref, wg_ref, ret_ref, lat_ref, gate_ref):
    y = _rms(x_ref[...]) * n1_ref[...]
    u = (y * (1.0 + mod_ref[1:2, :]) + mod_ref[0:1, :]).astype(BF16)
    cos, sin = cos_ref[...], sin_ref[...]
    half = RET_DK // 2
    for n in range(0, 2 * RET_W, RET_DK):
        p = _dot_nt(u, wr_ref[n:n + RET_DK, :])
        x1, x2 = p[:, :half], p[:, half:]
        ret_ref[:, n:n + half] = (x1 * cos - x2 * sin).astype(BF16)
        ret_ref[:, n + half:n + RET_DK] = (x2 * cos + x1 * sin).astype(BF16)
    step = 512
    for n in range(2 * RET_W, 3 * RET_W, step):
        ret_ref[:, n:n + step] = _dot_nt(u, wr_ref[n:n + step, :]).astype(BF16)
    for n in range(3 * RET_W, 4 * RET_W, step):
        p = _dot_nt(u, wr_ref[n:n + step, :])
        ret_ref[:, n:n + step] = (p * _sigmoid(p)).astype(BF16)
    lat_ref[...] = _dot_nt(u, wm_ref[...]).astype(BF16)
    for n in range(0, 2 * D_MODEL, step):
        gate_ref[:, n:n + step] = _dot_nt(u, wg_ref[n:n + step, :]).astype(BF16)


def _inproj(x2, mod3, norm1, cos, sin, w_ret, w_lat, w_gate):
    tm = TM_PROJ
    per_b = SEQ // tm
    rope_tab = pl.BlockSpec((tm, RET_DK // 2), lambda i: (i % per_b, 0))
    return pl.pallas_call(
        _inproj_kernel,
        grid=(N_TOK // tm,),
        in_specs=[pl.BlockSpec((tm, D_MODEL), lambda i: (i, 0)),
                  pl.BlockSpec((None, 6, D_MODEL), lambda i: (i // per_b, 0, 0)),
                  _resident((1, D_MODEL)), rope_tab, rope_tab,
                  _resident(w_ret.shape), _resident(w_lat.shape), _resident(w_gate.shape)],
        out_specs=[pl.BlockSpec((tm, 4 * RET_W), lambda i: (i, 0)),
                   pl.BlockSpec((tm, MLA_LAT_W), lambda i: (i, 0)),
                   pl.BlockSpec((tm, 2 * D_MODEL), lambda i: (i, 0))],
        out_shape=[jax.ShapeDtypeStruct((N_TOK, 4 * RET_W), BF16),
                   jax.ShapeDtypeStruct((N_TOK, MLA_LAT_W), BF16),
                   jax.ShapeDtypeStruct((N_TOK, 2 * D_MODEL), BF16)],
        compiler_params=_params("arbitrary"),
        name="inproj",
    )(x2, mod3, norm1, cos, sin, w_ret, w_lat, w_gate)


def _ret_kernel(q_ref, k_ref, v_ref, g_ref, dec_ref, xi_ref, zeta_ref, cd_ref, o_ref, state_ref):
    @pl.when(pl.program_id(1) == 0)
    def _():
        state_ref[...] = jnp.zeros_like(state_ref)

    heads = range(RET_HEADS)
    col = lambda h: slice(h * RET_DK, (h + 1) * RET_DK)
    for c in range(RET_BLK // RET_CHUNK):
        rows = slice(c * RET_CHUNK, (c + 1) * RET_CHUNK)
        scores = [(_dot_nt(q_ref[rows, col(h)], k_ref[rows, col(h)]) * dec_ref[h]).astype(BF16) for h in heads]
        carried = [xi_ref[h] * _dot(q_ref[rows, col(h)], state_ref[h].astype(BF16)) for h in heads]
        y = [_dot(scores[h], v_ref[rows, col(h)]) + carried[h] for h in heads]
        for h in heads:
            k_dec = (k_ref[rows, col(h)].astype(F32) * zeta_ref[h]).astype(BF16)
            state_ref[h] = state_ref[h] * cd_ref[h] + _dot_tn(k_dec, v_ref[rows, col(h)])
        yc = [y[h] - jnp.mean(y[h], axis=-1, keepdims=True) for h in heads]
        inv = [lax.rsqrt(jnp.mean(yc[h] * yc[h], axis=-1, keepdims=True) + EPS) for h in heads]
        for h in heads:
            o_ref[rows, col(h)] = (g_ref[rows, col(h)].astype(F32) * (yc[h] * inv[h])).astype(BF16)


def _retention(ret3, dec, xi, zeta, cd):
    blk = lambda part: pl.BlockSpec((None, RET_BLK, RET_W), lambda b, j: (b, j, part))
    whole = lambda a: pl.BlockSpec(a.shape, lambda b, j: (0,) * a.ndim)
    return pl.pallas_call(
        _ret_kernel,
        grid=(BATCH, SEQ // RET_BLK),
        in_specs=[blk(0), blk(1), blk(2), blk(3), whole(dec), whole(xi), whole(zeta), whole(cd)],
        out_specs=pl.BlockSpec((None, RET_BLK, RET_HEADS * RET_DV), lambda b, j: (b, j, 0)),
        out_shape=jax.ShapeDtypeStruct((BATCH, SEQ, RET_HEADS * RET_DV), BF16),
        scratch_shapes=[pltpu.VMEM((RET_HEADS, RET_DK, RET_DV), F32)],
        compiler_params=_params("arbitrary", "arbitrary"),
        name="retention",
    )(ret3, ret3, ret3, ret3, dec, xi, zeta, cd)


def _mla_kernel(lat_ref, qn_ref, kvn_ref, wq_ref, wkv_ref, cos_ref, sin_ref, o_ref,
                cq_s, ckv_s, kpe_s, q_s, k_s, v_s):
    h = pl.program_id(1)
    o_q, o_kv, o_pe, o_rot = 0, MLA_Q_LORA, MLA_Q_LORA + MLA_KV_LORA, MLA_Q_LORA + MLA_KV_LORA + MLA_ROPE
    cos, sin = cos_ref[...], sin_ref[...]

    @pl.when(h == 0)
    def _():
        cq_s[...] = (_rms(lat_ref[:, o_q:o_kv].astype(F32)) * qn_ref[...]).astype(BF16)
        ckv_s[...] = (_rms(lat_ref[:, o_kv:o_pe].astype(F32)) * kvn_ref[...]).astype(BF16)
        pe = lat_ref[:, o_pe:o_rot].astype(F32)
        rot = lat_ref[:, o_rot:o_rot + MLA_ROPE].astype(F32)
        kpe_s[...] = (pe * cos + rot * sin).astype(BF16)

    scale = (MLA_QK ** -0.5) * LOG2_E
    for g in range(MLA_HPS):
        qf = _dot(cq_s[...], wq_ref[g])
        q_s[g, :, :MLA_NOPE] = (qf[:, :MLA_NOPE] * scale).astype(BF16)
        q_pe = qf[:, MLA_NOPE:MLA_QK] * cos + qf[:, MLA_QK:] * sin
        q_s[g, :, MLA_NOPE:] = (q_pe * scale).astype(BF16)
        kvf = _dot(ckv_s[...], wkv_ref[g])
        k_s[g, :, :MLA_NOPE] = kvf[:, :MLA_NOPE].astype(BF16)
        k_s[g, :, MLA_NOPE:] = kpe_s[...]
        v_s[g, :, :MLA_V] = kvf[:, MLA_NOPE:].astype(BF16)
        v_s[g, :, MLA_V:] = jnp.ones((SEQ, MLA_V), BF16)

    causal = lax.broadcasted_iota(jnp.int32, (TQ, TQ), 0) >= lax.broadcasted_iota(jnp.int32, (TQ, TQ), 1)
    heads = range(MLA_HPS)
    n_blk = SEQ // TQ

    def scores(i):
        lo, hi = i * TQ, (i + 1) * TQ
        diag = [jnp.where(causal, _dot_nt(q_s[g, lo:hi, :], k_s[g, lo:hi, :]), -jnp.inf) for g in heads]
        past = [_dot_nt(q_s[g, lo:hi, :], k_s[g, :lo, :]) if i > 0 else None for g in heads]
        return diag, past

    pending = scores(0)
    for i in range(n_blk):
        lo, hi = i * TQ, (i + 1) * TQ
        diag, past = pending
        if i + 1 < n_blk:
            pending = scores(i + 1)
        m = [jnp.max(diag[g], axis=-1, keepdims=True) for g in heads]
        if i > 0:
            m = [jnp.maximum(m[g], jnp.max(past[g], axis=-1, keepdims=True)) for g in heads]
        acc = [_dot(jnp.exp2(diag[g] - m[g]).astype(BF16), v_s[g, lo:hi, :]) for g in heads]
        if i > 0:
            acc = [acc[g] + _dot(jnp.exp2(past[g] - m[g]).astype(BF16), v_s[g, :lo, :]) for g in heads]
        for g in heads:
            o_ref[lo:hi, g * MLA_V:(g + 1) * MLA_V] = (acc[g][:, :MLA_V] / acc[g][:, MLA_V:]).astype(BF16)


def _mla(lat3, q_norm, kv_norm, wq, wkv, cos, sin):
    hps = MLA_HPS
    return pl.pallas_call(
        _mla_kernel,
        grid=(BATCH, MLA_HEADS // hps),
        in_specs=[pl.BlockSpec((None, SEQ, MLA_LAT_W), lambda b, h: (b, 0, 0)),
                  pl.BlockSpec((1, MLA_Q_LORA), lambda b, h: (0, 0)),
                  pl.BlockSpec((1, MLA_KV_LORA), lambda b, h: (0, 0)),
                  pl.BlockSpec((hps, MLA_Q_LORA, MLA_QK + MLA_ROPE), lambda b, h: (h, 0, 0)),
                  pl.BlockSpec((hps, MLA_KV_LORA, MLA_NOPE + MLA_V), lambda b, h: (h, 0, 0)),
                  pl.BlockSpec((SEQ, MLA_ROPE), lambda b, h: (0, 0)),
                  pl.BlockSpec((SEQ, MLA_ROPE), lambda b, h: (0, 0))],
        out_specs=pl.BlockSpec((None, SEQ, hps * MLA_V), lambda b, h: (b, 0, h)),
        out_shape=jax.ShapeDtypeStruct((BATCH, SEQ, MLA_HEADS * MLA_V), BF16),
        scratch_shapes=[pltpu.VMEM((SEQ, MLA_Q_LORA), BF16),
                        pltpu.VMEM((SEQ, MLA_KV_LORA), BF16),
                        pltpu.VMEM((SEQ, MLA_ROPE), BF16),
                        pltpu.VMEM((hps, SEQ, MLA_QK), BF16),
                        pltpu.VMEM((hps, SEQ, MLA_QK), BF16),
                        pltpu.VMEM((hps, SEQ, 2 * MLA_V), BF16)],
        compiler_params=_params("arbitrary", "arbitrary"),
        name="mla",
    )(lat3, q_norm, kv_norm, wq, wkv, cos, sin)


def _route(logits_t):
    tm = logits_t.shape[1]
    row = lax.broadcasted_iota(jnp.int32, (SUBLANES, tm), 0)
    neg = -jnp.inf
    gl = jnp.where(row < N_GROUPS, logits_t[:SUBLANES], neg)
    gmax = jnp.max(gl, axis=0, keepdims=True)
    gsel = jnp.min(jnp.where(gl == gmax, row, SUBLANES), axis=0, keepdims=True)
    p_grp = 1.0 / jnp.sum(jnp.exp(gl - gmax), axis=0, keepdims=True)
    el = logits_t[SUBLANES * N_GROUPS:SUBLANES * (N_GROUPS + 1)]
    for g in reversed(range(N_GROUPS - 1)):
        el = jnp.where(gsel == g, logits_t[SUBLANES * (g + 1):SUBLANES * (g + 2)], el)
    v0 = jnp.max(el, axis=0, keepdims=True)
    i0 = jnp.min(jnp.where(el == v0, row, SUBLANES), axis=0, keepdims=True)
    el1 = jnp.where(row == i0, neg, el)
    v1 = jnp.max(el1, axis=0, keepdims=True)
    i1 = jnp.min(jnp.where(el1 == v1, row, SUBLANES), axis=0, keepdims=True)
    t = jnp.exp(v1 - v0)
    w0 = p_grp / (1.0 + t)
    w1 = p_grp * t / (1.0 + t)
    return gsel * EXPERTS_PER_GROUP + i0, gsel * EXPERTS_PER_GROUP + i1, w0, w1


def _stack_rows(rows, n):
    tm = rows[0].shape[1]
    row = lax.broadcasted_iota(jnp.int32, (n, tm), 0)
    out = jnp.zeros((n, tm), F32)
    for k, r in enumerate(rows):
        out = jnp.where(row == k, r, out)
    return out


def _to_token_tiles(ref, val):
    n = val.shape[0]
    for s in range(ROW_TILES):
        ref[pl.ds(s, n, stride=ROW_TILES), :] = val[:, s * LANES:(s + 1) * LANES]


def _from_token_tiles(ref):
    n = ref.shape[0] // ROW_TILES
    return jnp.concatenate([ref[pl.ds(s, n, stride=ROW_TILES), :] for s in range(ROW_TILES)], axis=-1)


def _token_rows(ref, t):
    return ref.at[pl.ds(pl.multiple_of(t * ROW_TILES, ROW_TILES), ROW_TILES)]


def _merge_kernel(yr_ref, at_ref, gr_ref, gm_ref, x_ref, mod_ref, n2_ref, wro_ref, wmo_ref, wo_ref,
                  wrt_ref, brt_ref, h1_ref, u2_ref, meta_ref, wtok_ref, cnt_ref, carry_ref):
    tm = x_ref.shape[0]

    @pl.when(pl.program_id(0) == 0)
    def _():
        carry_ref[...] = jnp.zeros_like(carry_ref)

    chunks = range(MERGE_CHAINS)
    sub = tm // MERGE_CHAINS
    rows = [slice(c * sub, (c + 1) * sub) for c in chunks]
    y_ret = [_dot(yr_ref[r, :], wro_ref[...]) for r in rows]
    y_mla = [_dot(at_ref[r, :], wmo_ref[...]) for r in rows]
    merged = [(_sigmoid(gr_ref[rows[c], :].astype(F32)) * y_ret[c]
               + _sigmoid(gm_ref[rows[c], :].astype(F32)) * y_mla[c]).astype(BF16) for c in chunks]
    o = [_dot(merged[c], wo_ref[...]) for c in chunks]
    h1 = [x_ref[rows[c], :] + mod_ref[2:3, :] * o[c] for c in chunks]
    for c in chunks:
        h1_ref[rows[c], :] = h1[c]
    u2 = [_rms(h1[c]) * n2_ref[...] * (1.0 + mod_ref[4:5, :]) + mod_ref[3:4, :] for c in chunks]
    for c in chunks:
        _to_token_tiles(u2_ref.at[pl.ds(c * sub * ROW_TILES, sub * ROW_TILES)], u2[c])
    w = wrt_ref[...]
    w_hi = w.astype(BF16)
    w_lo = (w - w_hi.astype(F32)).astype(BF16)
    u_hi = [u2[c].astype(BF16) for c in chunks]
    u_lo = [(u2[c] - u_hi[c].astype(F32)).astype(BF16) for c in chunks]
    logits_t = [_dot_nt(w_hi, u_hi[c]) + _dot_nt(w_hi, u_lo[c]) + _dot_nt(w_lo, u_hi[c]) + brt_ref[:, 0:1]
                for c in chunks]
    routed = [_route(logits_t[c]) for c in chunks]
    e0, e1, w0, w1 = [jnp.concatenate([routed[c][k] for c in chunks], axis=1) for k in range(4)]
    erow = lax.broadcasted_iota(jnp.int32, (N_EXPERTS, tm), 0)
    m0, m1 = erow == e0, erow == e1
    member = jnp.where(m0 | m1, 1.0, 0.0)
    earlier = jnp.where(lax.broadcasted_iota(jnp.int32, (tm, tm), 0) < lax.broadcasted_iota(jnp.int32, (tm, tm), 1),
                        1.0, 0.0).astype(BF16)
    prefix = _dot(member.astype(BF16), earlier) + carry_ref[:, 0:1]
    rank0 = jnp.sum(jnp.where(m0, prefix, 0.0), axis=0, keepdims=True)
    rank1 = jnp.sum(jnp.where(m1, prefix, 0.0), axis=0, keepdims=True)
    carry_ref[...] = carry_ref[...] + jnp.sum(member, axis=1, keepdims=True)
    cnt_ref[...] = carry_ref[...]
    meta_ref[...] = _stack_rows([e0.astype(F32), e1.astype(F32), rank0, rank1], SUBLANES)
    wt = _stack_rows([w0, w1], 2 * SUBLANES)
    wt_hi = wt.astype(BF16)
    wt_lo = (wt - wt_hi.astype(F32)).astype(BF16)
    place = jnp.where(lax.broadcasted_iota(jnp.int32, (2 * SUBLANES, LANES), 0)
                      == lax.broadcasted_iota(jnp.int32, (2 * SUBLANES, LANES), 1), 1.0, 0.0).astype(BF16)
    wtok_ref[...] = _dot_tn(wt_hi, place) + _dot_tn(wt_lo, place)


def _merge(y_ret, attn, gates, x2, mod3, norm2, w_ret_o, w_mla_o, w_out, w_rt, b_rt):
    tm = TM_PROJ
    per_b = SEQ // tm
    row = lambda j: pl.BlockSpec((tm, D_MODEL), lambda i: (i, j))
    return pl.pallas_call(
        _merge_kernel,
        grid=(N_TOK // tm,),
        in_specs=[row(0), row(0), row(0), row(1), row(0),
                  pl.BlockSpec((None, 6, D_MODEL), lambda i: (i // per_b, 0, 0)),
                  _resident((1, D_MODEL)),
                  _resident(w_ret_o.shape), _resident(w_mla_o.shape), _resident(w_out.shape),
                  _resident(w_rt.shape), _resident(b_rt.shape)],
        out_specs=[row(0),
                   pl.BlockSpec((tm * ROW_TILES, LANES), lambda i: (i, 0)),
                   pl.BlockSpec((SUBLANES, tm), lambda i: (0, i)),
                   pl.BlockSpec((tm, LANES), lambda i: (i, 0)),
                   pl.BlockSpec((N_EXPERTS, LANES), lambda i: (0, 0))],
        out_shape=[jax.ShapeDtypeStruct((N_TOK, D_MODEL), F32),
                   jax.ShapeDtypeStruct((N_TOK * ROW_TILES, LANES), F32),
                   jax.ShapeDtypeStruct((SUBLANES, N_TOK), F32),
                   jax.ShapeDtypeStruct((N_TOK, LANES), F32),
                   jax.ShapeDtypeStruct((N_EXPERTS, LANES), F32)],
        scratch_shapes=[pltpu.VMEM((N_EXPERTS, LANES), F32)],
        compiler_params=_params("arbitrary"),
        name="merge",
    )(y_ret, attn, gates, gates, x2, mod3, norm2, w_ret_o, w_mla_o, w_out, w_rt, b_rt)


def _plan_kernel(meta_ref, off_ref, dst_ref):
    m = meta_ref[...]
    erow = lax.broadcasted_iota(jnp.int32, (N_EXPERTS, m.shape[1]), 0)
    off = off_ref[:, 0:1]
    d = [jnp.sum(jnp.where(erow == m[k:k + 1].astype(jnp.int32), off, 0.0), axis=0, keepdims=True) + m[k + 2:k + 3]
         for k in range(TOP_K)]
    dst_ref[...] = _stack_rows(d, SUBLANES).astype(jnp.int32)


def _plan(meta_t, off_col):
    tm = 2048
    return pl.pallas_call(
        _plan_kernel,
        grid=(N_TOK // tm,),
        in_specs=[pl.BlockSpec((SUBLANES, tm), lambda i: (0, i)),
                  pl.BlockSpec((N_EXPERTS, LANES), lambda i: (0, 0))],
        out_specs=pl.BlockSpec((SUBLANES, tm), lambda i: (0, i)),
        out_shape=jax.ShapeDtypeStruct((SUBLANES, N_TOK), jnp.int32),
        compiler_params=_params("arbitrary"),
        name="plan",
    )(meta_t, off_col)


def _row_copy_wait(src_like, dst_like, sem):
    pltpu.make_async_copy(src_like, dst_like, sem).wait()


def _dispatch_kernel(d0_ref, d1_ref, seg_ref, cnt_ref, u_ref, xs_ref, zero_ref, sem, zsem):
    i = pl.program_id(0)
    tm = u_ref.shape[0] // ROW_TILES
    tile_rows = TE * ROW_TILES

    def slot_tile(j):
        return xs_ref.at[pl.ds(pl.multiple_of(j * tile_rows, tile_rows), tile_rows)]

    @pl.when(i == 0)
    def _():
        zero_ref[...] = jnp.zeros_like(zero_ref)

        def pad(e):
            first = seg_ref[e] + cnt_ref[e]
            rows = (seg_ref[e + 1] - first) * ROW_TILES
            start = pl.multiple_of(first * ROW_TILES, ROW_TILES)
            return rows > 0, pltpu.make_async_copy(zero_ref.at[pl.ds(0, rows)], xs_ref.at[pl.ds(start, rows)], zsem)

        for e in range(N_EXPERTS):
            nonempty, copy = pad(e)
            pl.when(nonempty)(copy.start)
        for e in range(N_EXPERTS):
            nonempty, copy = pad(e)
            pl.when(nonempty)(copy.wait)

        def unused(j):
            return pltpu.make_async_copy(zero_ref, slot_tile(j), zsem)

        first_unused = seg_ref[N_EXPERTS] // TE
        lax.fori_loop(first_unused, N_TILES, lambda j, c: (unused(j).start(), c)[1], 0)
        lax.fori_loop(first_unused, N_TILES, lambda j, c: (unused(j).wait(), c)[1], 0)

    base = i * tm

    def body(r, carry):
        src = _token_rows(u_ref, r)
        pltpu.make_async_copy(src, _token_rows(xs_ref, d0_ref[base + r]), sem).start(priority=0)
        pltpu.make_async_copy(src, _token_rows(xs_ref, d1_ref[base + r]), sem).start(priority=1)
        return carry

    lax.fori_loop(0, tm, body, 0, unroll=8)
    _row_copy_wait(u_ref, xs_ref.at[pl.ds(0, tm * ROW_TILES)], sem)
    _row_copy_wait(u_ref, xs_ref.at[pl.ds(0, tm * ROW_TILES)], sem)


def _dispatch(d0, d1, seg, cnt, u2t):
    tm = TM_PROJ
    return pl.pallas_call(
        _dispatch_kernel,
        grid_spec=pltpu.PrefetchScalarGridSpec(
            num_scalar_prefetch=4,
            grid=(N_TOK // tm,),
            in_specs=[pl.BlockSpec((tm * ROW_TILES, LANES), lambda i, *_: (i, 0))],
            out_specs=pl.BlockSpec(memory_space=pl.ANY),
            scratch_shapes=[pltpu.VMEM((TE * ROW_TILES, LANES), F32),
                            pltpu.SemaphoreType.DMA(()), pltpu.SemaphoreType.DMA(())]),
        out_shape=jax.ShapeDtypeStruct((N_SLOTS * ROW_TILES, LANES), F32),
        compiler_params=_params("arbitrary"),
        name="dispatch",
    )(d0, d1, seg, cnt, u2t)


def _expert_kernel(te_ref, nv_ref, x_ref, w1_ref, w3_ref, w2_ref, y_ref, w1_s, w3_s, w2_s):
    j = pl.program_id(0)

    @pl.when(j < nv_ref[0])
    def _():
        @pl.when((j == 0) | (te_ref[j] != te_ref[jnp.maximum(j - 1, 0)]))
        def _():
            w1_s[...] = w1_ref[...].astype(BF16)
            w3_s[...] = w3_ref[...].astype(BF16)
            w2_s[...] = w2_ref[...].astype(BF16)

        sub = TE // TE_CHAINS * ROW_TILES
        part = lambda ref, c: ref.at[pl.ds(c * sub, sub)]
        chains = range(TE_CHAINS)
        x = [_from_token_tiles(part(x_ref, c)).astype(BF16) for c in chains]
        a = [_dot(x[c], w1_s[...]) for c in chains]
        b = [_dot(x[c], w3_s[...]) for c in chains]
        hid = [(a[c] * _sigmoid(a[c]) * b[c]).astype(BF16) for c in chains]
        for c in chains:
            _to_token_tiles(part(y_ref, c), _dot(hid[c], w2_s[...]))


def _experts(tile_expert, n_valid, xs, w1, w3, w2):
    tile = lambda j, te, nv: jnp.minimum(j, nv[0] - 1)
    wspec = lambda shape: pl.BlockSpec((None,) + shape, lambda j, te, nv: (te[tile(j, te, nv)], 0, 0))
    slots = pl.BlockSpec((TE * ROW_TILES, LANES), lambda j, te, nv: (tile(j, te, nv), 0))
    return pl.pallas_call(
        _expert_kernel,
        grid_spec=pltpu.PrefetchScalarGridSpec(
            num_scalar_prefetch=2,
            grid=(N_TILES,),
            in_specs=[slots, wspec((D_MODEL, D_EXPERT)), wspec((D_MODEL, D_EXPERT)), wspec((D_EXPERT, D_MODEL))],
            out_specs=slots,
            scratch_shapes=[pltpu.VMEM((D_MODEL, D_EXPERT), BF16), pltpu.VMEM((D_MODEL, D_EXPERT), BF16),
                            pltpu.VMEM((D_EXPERT, D_MODEL), BF16)]),
        out_shape=jax.ShapeDtypeStruct((N_SLOTS * ROW_TILES, LANES), F32),
        input_output_aliases={2: 0},
        compiler_params=_params("arbitrary"),
        name="experts",
    )(tile_expert, n_valid, xs, w1, w3, w2)


def _final_kernel(d0_ref, d1_ref, h1_ref, meta_ref, mod_ref, fn_ref, ys_ref, o_ref, ybuf, sem):
    i = pl.program_id(0)
    tm = h1_ref.shape[0]

    def gather(t):
        buf, s = ybuf.at[t % 2], sem.at[t % 2]
        base = t * tm

        def body(r, carry):
            pltpu.make_async_copy(_token_rows(ys_ref, d0_ref[base + r]), _token_rows(buf.at[0], r), s).start(priority=0)
            pltpu.make_async_copy(_token_rows(ys_ref, d1_ref[base + r]), _token_rows(buf.at[1], r), s).start(priority=1)
            return carry

        lax.fori_loop(0, tm, body, 0, unroll=8)

    @pl.when(i == 0)
    def _():
        gather(0)

    @pl.when(i + 1 < pl.num_programs(0))
    def _():
        gather(i + 1)

    buf = ybuf.at[i % 2]
    for k in range(TOP_K):
        _row_copy_wait(ys_ref.at[pl.ds(0, tm * ROW_TILES)], buf.at[k], sem.at[i % 2])
    m = meta_ref[...]
    moe = m[:, 0:1] * _from_token_tiles(buf.at[0]) + m[:, 1:2] * _from_token_tiles(buf.at[1])
    h2 = h1_ref[...] + mod_ref[5:6, :] * moe
    o_ref[...] = _rms(h2) * fn_ref[...]


def _final(d0, d1, h1, meta, mod3, final_norm, ys):
    tm = TM_PROJ
    per_b = SEQ // tm
    return pl.pallas_call(
        _final_kernel,
        grid_spec=pltpu.PrefetchScalarGridSpec(
            num_scalar_prefetch=2,
            grid=(N_TOK // tm,),
            in_specs=[pl.BlockSpec((tm, D_MODEL), lambda i, *_: (i, 0)),
                      pl.BlockSpec((tm, LANES), lambda i, *_: (i, 0)),
                      pl.BlockSpec((None, 6, D_MODEL), lambda i, *_: (i // per_b, 0, 0)),
                      pl.BlockSpec((1, D_MODEL), lambda i, *_: (0, 0)),
                      pl.BlockSpec(memory_space=pl.ANY)],
            out_specs=pl.BlockSpec((tm, D_MODEL), lambda i, *_: (i, 0)),
            scratch_shapes=[pltpu.VMEM((2, TOP_K, tm * ROW_TILES, LANES), F32), pltpu.SemaphoreType.DMA((2,))]),
        out_shape=jax.ShapeDtypeStruct((N_TOK, D_MODEL), F32),
        compiler_params=_params("arbitrary"),
        name="final",
    )(d0, d1, h1, meta, mod3, final_norm, ys)


def _slot_layout(counts):
    cnt = counts[:, 0].astype(jnp.int32)
    tile_end = jnp.cumsum((cnt + TE - 1) // TE)
    seg = jnp.concatenate([jnp.zeros((1,), jnp.int32), tile_end * TE])
    off_col = jnp.broadcast_to(seg[:-1].astype(F32)[:, None], (N_EXPERTS, LANES))
    tile_ids = jnp.arange(N_TILES, dtype=jnp.int32)
    tile_expert = jnp.sum((tile_end[None, :] <= tile_ids[:, None]).astype(jnp.int32), axis=1)
    tile_expert = jnp.minimum(tile_expert, N_EXPERTS - 1)
    return cnt, seg, off_col, tile_expert, tile_end[-1:]


def _rope_tables(dim):
    pos = np.arange(SEQ, dtype=np.float64)
    inv = ROPE_THETA ** (-np.arange(0, dim, 2, dtype=np.float64) / dim)
    ang = pos[:, None] * inv[None, :]
    return np.cos(ang).astype(np.float32), np.sin(ang).astype(np.float32)


def _decay_tables():
    c = RET_CHUNK
    log_gamma = np.log1p(-np.exp2(-5.0 - np.arange(RET_HEADS, dtype=np.float64)))
    idx = np.arange(c, dtype=np.float64)
    rel = idx[:, None] - idx[None, :]
    dec = np.where(rel[None] >= 0, np.exp(log_gamma[:, None, None] * np.maximum(rel, 0.0)[None]), 0.0)
    xi = np.exp(log_gamma[:, None] * (idx[None, :] + 1.0))[:, :, None]
    zeta = np.exp(log_gamma[:, None] * (c - 1.0 - idx[None, :]))[:, :, None]
    cd = np.exp(log_gamma * c)[:, None, None]
    return tuple(jnp.asarray(t.astype(np.float32)) for t in (dec, xi, zeta, cd))


def _rotate_half_cols(w):
    half = w.shape[-1] // 2
    return jnp.concatenate([-w[..., half:], w[..., :half]], axis=-1)


def kernel(x, c, w_ada, b_ada, norm1, norm2, w_in, w_ret_o, q_norm, kv_norm, w_uq, w_ukv, w_mla_o, w_out,
           w_grp, b_grp, w_exp, b_exp, w1, w3, w2, final_norm):
    assert x.shape == (BATCH, SEQ, D_MODEL) and w_ada.shape[0] == 1
    x2 = x.reshape(N_TOK, D_MODEL)

    w_ret, w_lat, w_gate = _wprep(jnp.transpose(w_in[0]))
    wq = w_uq[0].reshape(MLA_Q_LORA, MLA_HEADS, MLA_QK)
    wq = jnp.concatenate([wq, _rotate_half_cols(wq[..., MLA_NOPE:])], axis=-1)
    wq = wq.transpose(1, 0, 2).astype(BF16)
    wkv = w_ukv[0].reshape(MLA_KV_LORA, MLA_HEADS, MLA_NOPE + MLA_V).transpose(1, 0, 2).astype(BF16)
    gap = jnp.zeros((SUBLANES - N_GROUPS, D_MODEL), F32)
    tail = jnp.zeros((ROUTER_ROWS - SUBLANES - N_EXPERTS, D_MODEL), F32)
    w_rt = jnp.concatenate([w_grp[0].T, gap, w_exp[0].T, tail], axis=0)
    b_rt = jnp.concatenate([b_grp[0], gap[:, 0], b_exp[0], tail[:, 0]])
    b_rt = jnp.broadcast_to(b_rt[:, None], (ROUTER_ROWS, LANES))

    ret_cos, ret_sin = (jnp.asarray(t) for t in _rope_tables(RET_DK))
    mla_cos, mla_sin = (jnp.asarray(np.concatenate([t, t], axis=-1)) for t in _rope_tables(MLA_ROPE))
    dec, xi, zeta, cd = _decay_tables()

    mod3 = _ada(c, w_ada[0], b_ada[0]).reshape(BATCH, 6, D_MODEL)
    ret, lat, gates = _inproj(x2, mod3, norm1, ret_cos, ret_sin, w_ret, w_lat, w_gate)
    y_ret = _retention(ret.reshape(BATCH, SEQ, 4 * RET_W), dec, xi, zeta, cd)
    attn = _mla(lat.reshape(BATCH, SEQ, MLA_LAT_W), q_norm, kv_norm, wq, wkv, mla_cos, mla_sin)
    h1, u2t, meta_t, wtok, counts = _merge(y_ret.reshape(N_TOK, D_MODEL), attn.reshape(N_TOK, D_MODEL), gates, x2,
                                           mod3, norm2, w_ret_o[0].astype(BF16), w_mla_o[0].astype(BF16),
                                           w_out[0].astype(BF16), w_rt, b_rt)
    cnt, seg, off_col, tile_expert, n_valid = _slot_layout(counts)
    dst = _plan(meta_t, off_col)
    d0, d1 = dst[0], dst[1]
    xs = _dispatch(d0, d1, seg, cnt, u2t)
    e_shape = (N_EXPERTS, D_MODEL, D_EXPERT)
    ys = _experts(tile_expert, n_valid, xs, w1[0].reshape(e_shape), w3[0].reshape(e_shape),
                  w2[0].reshape(N_EXPERTS, D_EXPERT, D_MODEL))
    out = _final(d0, d1, h1, wtok, mod3, final_norm.reshape(1, D_MODEL), ys)
    return out.reshape(BATCH, SEQ, D_MODEL)
```

```python
import numpy as np
import jax
import jax.numpy as jnp
from jax import lax
from jax.experimental import pallas as pl
from jax.experimental.pallas import tpu as pltpu

D_MODEL = 1024
BATCH = 8
SEQ = 2048
N_TOK = BATCH * SEQ

RET_HEADS = 4
RET_DK = 256
RET_DV = 256
RET_CHUNK = 256
RET_BLK = 512
RET_W = RET_HEADS * RET_DK

MLA_HEADS = 8
MLA_NOPE = 128
MLA_ROPE = 64
MLA_V = 128
MLA_Q_LORA = 384
MLA_KV_LORA = 256
MLA_LAT_W = MLA_Q_LORA + MLA_KV_LORA + 2 * MLA_ROPE
MLA_QK = MLA_NOPE + MLA_ROPE
ROPE_THETA = 10000.0

N_GROUPS = 4
EXPERTS_PER_GROUP = 8
N_EXPERTS = N_GROUPS * EXPERTS_PER_GROUP
D_EXPERT = 256
EPS = 1e-6
LOG2_E = 1.4426950408889634

LANES = 128
SUBLANES = 8
ROUTER_ROWS = 48
ROW_TILES = D_MODEL // LANES
VMEM_LIMIT = 56 * 1024 * 1024

TM_PROJ = 512
TQ = 256
MLA_HPS = 2
TE = 512
TE_CHAINS = 2
MERGE_CHAINS = 2
TOP_K = 2
N_TILES = N_TOK * TOP_K // TE + N_EXPERTS
N_SLOTS = N_TILES * TE

F32 = jnp.float32
BF16 = jnp.bfloat16


def _sigmoid(x):
    return 1.0 / (1.0 + jnp.exp(-x))


def _rms(x):
    return x * lax.rsqrt(jnp.mean(x * x, axis=-1, keepdims=True) + EPS)


def _dot(a, b):
    return jnp.dot(a, b, preferred_element_type=F32)


def _dot_nt(a, b):
    return lax.dot_general(a, b, (((1,), (1,)), ((), ())), preferred_element_type=F32)


def _dot_tn(a, b):
    return lax.dot_general(a, b, (((0,), (0,)), ((), ())), preferred_element_type=F32)


def _params(*sem):
    return pltpu.CompilerParams(dimension_semantics=sem, vmem_limit_bytes=VMEM_LIMIT)


def _resident(shape):
    nd = len(shape)
    return pl.BlockSpec(shape, lambda *_: (0,) * nd, pipeline_mode=pl.Buffered(1))


def _ada_kernel(c_ref, w_ref, b_ref, o_ref):
    c = c_ref[...]
    act = (c * _sigmoid(c)).astype(BF16)
    o_ref[...] = _dot(act, w_ref[...].astype(BF16)) + b_ref[...]


def _ada(c, w_ada, b_ada):
    n = w_ada.shape[1]
    tn = D_MODEL
    return pl.pallas_call(
        _ada_kernel,
        grid=(n // tn,),
        in_specs=[pl.BlockSpec((BATCH, D_MODEL), lambda j: (0, 0)),
                  pl.BlockSpec((D_MODEL, tn), lambda j: (0, j)),
                  pl.BlockSpec((1, tn), lambda j: (0, j))],
        out_specs=pl.BlockSpec((BATCH, tn), lambda j: (0, j)),
        out_shape=jax.ShapeDtypeStruct((BATCH, n), F32),
        compiler_params=_params("arbitrary"),
        name="ada",
    )(c, w_ada, b_ada.reshape(1, n))


O_LAT = 4 * RET_W
O_PE = O_LAT + MLA_Q_LORA + MLA_KV_LORA
O_GATE = O_PE + MLA_ROPE


N_IN = O_GATE + 2 * D_MODEL
WPREP_ROWS = 512


def _wprep_kernel(w_hbm, ret_ref, lat_ref, gate_ref, buf, sem):
    chunks = [(s, min(WPREP_ROWS, N_IN - s)) for s in range(0, N_IN, WPREP_ROWS)]

    def copy(i):
        s, n = chunks[i]
        return pltpu.make_async_copy(w_hbm.at[pl.ds(s, n)], buf.at[i % 2, pl.ds(0, n)], sem.at[i % 2])

    groups = [(0, RET_W, ret_ref, 0, 1.0), (RET_W, 2 * RET_W, ret_ref, RET_W, RET_DK ** -0.5),
              (2 * RET_W, O_LAT, ret_ref, 2 * RET_W, 1.0), (O_LAT, O_GATE, lat_ref, 0, 1.0),
              (O_GATE, N_IN, gate_ref, 0, 1.0)]
    half = MLA_ROPE // 2
    rot = [(O_PE + half, O_GATE, O_GATE - O_LAT, -1.0), (O_PE, O_PE + half, O_GATE - O_LAT + half, 1.0)]

    copy(0).start()
    for i, (s, n) in enumerate(chunks):
        if i + 1 < len(chunks):
            copy(i + 1).start()
        copy(i).wait()
        for lo, hi, dst, dst_lo, scale in groups + [(a, b, lat_ref, d, sc) for a, b, d, sc in rot]:
            a, b = max(lo, s), min(hi, s + n)
            if a < b:
                x = buf[i % 2, a - s:b - s, :]
                dst[dst_lo + a - lo:dst_lo + b - lo, :] = (x if scale == 1.0 else x * scale).astype(BF16)


def _wprep(w_in_t):
    out_rows = (O_LAT, MLA_LAT_W, 2 * D_MODEL)
    whole = lambda n: pl.BlockSpec((n, D_MODEL), lambda i: (0, 0))
    return pl.pallas_call(
        _wprep_kernel,
        grid=(1,),
        in_specs=[pl.BlockSpec(memory_space=pl.ANY)],
        out_specs=[whole(n) for n in out_rows],
        out_shape=[jax.ShapeDtypeStruct((n, D_MODEL), BF16) for n in out_rows],
        scratch_shapes=[pltpu.VMEM((2, WPREP_ROWS, D_MODEL), F32), pltpu.SemaphoreType.DMA((2,))],
        compiler_params=_params("arbitrary"),
        name="wprep",
    )(w_in_t)


def _inproj_kernel(x_ref, mod_ref, n1_ref, cos_ref, sin_ref, wr_ref, wm_ref, wg_ref, ret_ref, lat_ref, gate_ref):
    y = _rms(x_ref[...]) * n1_ref[...]
    u = (y * (1.0 + mod_ref[1:2, :]) + mod_ref[0:1, :]).astype(BF16)
    cos, sin = cos_ref[...], sin_ref[...]
    half = RET_DK // 2
    for n in range(0, 2 * RET_W, RET_DK):
        p = _dot_nt(u, wr_ref[n:n + RET_DK, :])
        x1, x2 = p[:, :half], p[:, half:]
        ret_ref[:, n:n + half] = (x1 * cos - x2 * sin).astype(BF16)
        ret_ref[:, n + half:n + RET_DK] = (x2 * cos + x1 * sin).astype(BF16)
    step = 512
    for n in range(2 * RET_W, 3 * RET_W, step):
        ret_ref[:, n:n + step] = _dot_nt(u, wr_ref[n:n + step, :]).astype(BF16)
    for n in range(3 * RET_W, 4 * RET_W, step):
        p = _dot_nt(u, wr_ref[n:n + step, :])
        ret_ref[:, n:n + step] = (p * _sigmoid(p)).astype(BF16)
    lat_ref[...] = _dot_nt(u, wm_ref[...]).astype(BF16)
    for n in range(0, 2 * D_MODEL, step):
        gate_ref[:, n:n + step] = _dot_nt(u, wg_ref[n:n + step, :]).astype(BF16)


def _inproj(x2, mod3, norm1, cos, sin, w_ret, w_lat, w_gate):
    tm = TM_PROJ
    per_b = SEQ // tm
    rope_tab = pl.BlockSpec((tm, RET_DK // 2), lambda i: (i % per_b, 0))
    return pl.pallas_call(
        _inproj_kernel,
        grid=(N_TOK // tm,),
        in_specs=[pl.BlockSpec((tm, D_MODEL), lambda i: (i, 0)),
                  pl.BlockSpec((None, 6, D_MODEL), lambda i: (i // per_b, 0, 0)),
                  _resident((1, D_MODEL)), rope_tab, rope_tab,
                  _resident(w_ret.shape), _resident(w_lat.shape), _resident(w_gate.shape)],
        out_specs=[pl.BlockSpec((tm, 4 * RET_W), lambda i: (i, 0)),
                   pl.BlockSpec((tm, MLA_LAT_W), lambda i: (i, 0)),
                   pl.BlockSpec((tm, 2 * D_MODEL), lambda i: (i, 0))],
        out_shape=[jax.ShapeDtypeStruct((N_TOK, 4 * RET_W), BF16),
                   jax.ShapeDtypeStruct((N_TOK, MLA_LAT_W), BF16),
                   jax.ShapeDtypeStruct((N_TOK, 2 * D_MODEL), BF16)],
        compiler_params=_params("arbitrary"),
        name="inproj",
    )(x2, mod3, norm1, cos, sin, w_ret, w_lat, w_gate)


def _ret_kernel(q_ref, k_ref, v_ref, g_ref, dec_ref, xi_ref, zeta_ref, cd_ref, o_ref, state_ref):
    @pl.when(pl.program_id(1) == 0)
    def _():
        state_ref[...] = jnp.zeros_like(state_ref)

    heads = range(RET_HEADS)
    col = lambda h: slice(h * RET_DK, (h + 1) * RET_DK)
    for c in range(RET_BLK // RET_CHUNK):
        rows = slice(c * RET_CHUNK, (c + 1) * RET_CHUNK)
        scores = [(_dot_nt(q_ref[rows, col(h)], k_ref[rows, col(h)]) * dec_ref[h]).astype(BF16) for h in heads]
        carried = [xi_ref[h] * _dot(q_ref[rows, col(h)], state_ref[h].astype(BF16)) for h in heads]
        y = [_dot(scores[h], v_ref[rows, col(h)]) + carried[h] for h in heads]
        for h in heads:
            k_dec = (k_ref[rows, col(h)].astype(F32) * zeta_ref[h]).astype(BF16)
            state_ref[h] = state_ref[h] * cd_ref[h] + _dot_tn(k_dec, v_ref[rows, col(h)])
        yc = [y[h] - jnp.mean(y[h], axis=-1, keepdims=True) for h in heads]
        inv = [lax.rsqrt(jnp.mean(yc[h] * yc[h], axis=-1, keepdims=True) + EPS) for h in heads]
        for h in heads:
            o_ref[rows, col(h)] = (g_ref[rows, col(h)].astype(F32) * (yc[h] * inv[h])).astype(BF16)


def _retention(ret3, dec, xi, zeta, cd):
    blk = lambda part: pl.BlockSpec((None, RET_BLK, RET_W), lambda b, j: (b, j, part))
    whole = lambda a: pl.BlockSpec(a.shape, lambda b, j: (0,) * a.ndim)
    return pl.pallas_call(
        _ret_kernel,
        grid=(BATCH, SEQ // RET_BLK),
        in_specs=[blk(0), blk(1), blk(2), blk(3), whole(dec), whole(xi), whole(zeta), whole(cd)],
        out_specs=pl.BlockSpec((None, RET_BLK, RET_HEADS * RET_DV), lambda b, j: (b, j, 0)),
        out_shape=jax.ShapeDtypeStruct((BATCH, SEQ, RET_HEADS * RET_DV), BF16),
        scratch_shapes=[pltpu.VMEM((RET_HEADS, RET_DK, RET_DV), F32)],
        compiler_params=_params("arbitrary", "arbitrary"),
        name="retention",
    )(ret3, ret3, ret3, ret3, dec, xi, zeta, cd)


def _mla_kernel(lat_ref, qn_ref, kvn_ref, wq_ref, wkv_ref, cos_ref, sin_ref, o_ref,
                cq_s, ckv_s, kpe_s, q_s, k_s, v_s):
    h = pl.program_id(1)
    o_q, o_kv, o_pe, o_rot = 0, MLA_Q_LORA, MLA_Q_LORA + MLA_KV_LORA, MLA_Q_LORA + MLA_KV_LORA + MLA_ROPE
    cos, sin = cos_ref[...], sin_ref[...]

    @pl.when(h == 0)
    def _():
        cq_s[...] = (_rms(lat_ref[:, o_q:o_kv].astype(F32)) * qn_ref[...]).astype(BF16)
        ckv_s[...] = (_rms(lat_ref[:, o_kv:o_pe].astype(F32)) * kvn_ref[...]).astype(BF16)
        pe = lat_ref[:, o_pe:o_rot].astype(F32)
        rot = lat_ref[:, o_rot:o_rot + MLA_ROPE].astype(F32)
        kpe_s[...] = (pe * cos + rot * sin).astype(BF16)

    scale = (MLA_QK ** -0.5) * LOG2_E
    for g in range(MLA_HPS):
        qf = _dot(cq_s[...], wq_ref[g])
        q_s[g, :, :MLA_NOPE] = (qf[:, :MLA_NOPE] * scale).astype(BF16)
        q_pe = qf[:, MLA_NOPE:MLA_QK] * cos + qf[:, MLA_QK:] * sin
        q_s[g, :, MLA_NOPE:] = (q_pe * scale).astype(BF16)
        kvf = _dot(ckv_s[...], wkv_ref[g])
        k_s[g, :, :MLA_NOPE] = kvf[:, :MLA_NOPE].astype(BF16)
        k_s[g, :, MLA_NOPE:] = kpe_s[...]
        v_s[g, :, :MLA_V] = kvf[:, MLA_NOPE:].astype(BF16)
        v_s[g, :, MLA_V:] = jnp.ones((SEQ, MLA_V), BF16)

    causal = lax.broadcasted_iota(jnp.int32, (TQ, TQ), 0) >= lax.broadcasted_iota(jnp.int32, (TQ, TQ), 1)
    heads = range(MLA_HPS)
    n_blk = SEQ // TQ

    def scores(i):
        lo, hi = i * TQ, (i + 1) * TQ
        diag = [jnp.where(causal, _dot_nt(q_s[g, lo:hi, :], k_s[g, lo:hi, :]), -jnp.inf) for g in heads]
        past = [_dot_nt(q_s[g, lo:hi, :], k_s[g, :lo, :]) if i > 0 else None for g in heads]
        return diag, past

    pending = scores(0)
    for i in range(n_blk):
        lo, hi = i * TQ, (i + 1) * TQ
        diag, past = pending
        if i + 1 < n_blk:
            pending = scores(i + 1)
        m = [jnp.max(diag[g], axis=-1, keepdims=True) for g in heads]
        if i > 0:
            m = [jnp.maximum(m[g], jnp.max(past[g], axis=-1, keepdims=True)) for g in heads]
        acc = [_dot(jnp.exp2(diag[g] - m[g]).astype(BF16), v_s[g, lo:hi, :]) for g in heads]
        if i > 0:
            acc = [acc[g] + _dot(jnp.exp2(past[g] - m[g]).astype(BF16), v_s[g, :lo, :]) for g in heads]
        for g in heads:
            o_ref[lo:hi, g * MLA_V:(g + 1) * MLA_V] = (acc[g][:, :MLA_V] / acc[g][:, MLA_V:]).astype(BF16)


def _mla(lat3, q_norm, kv_norm, wq, wkv, cos, sin):
    hps = MLA_HPS
    return pl.pallas_call(
        _mla_kernel,
        grid=(BATCH, MLA_HEADS // hps),
        in_specs=[pl.BlockSpec((None, SEQ, MLA_LAT_W), lambda b, h: (b, 0, 0)),
                  pl.BlockSpec((1, MLA_Q_LORA), lambda b, h: (0, 0)),
                  pl.BlockSpec((1, MLA_KV_LORA), lambda b, h: (0, 0)),
                  pl.BlockSpec((hps, MLA_Q_LORA, MLA_QK + MLA_ROPE), lambda b, h: (h, 0, 0)),
                  pl.BlockSpec((hps, MLA_KV_LORA, MLA_NOPE + MLA_V), lambda b, h: (h, 0, 0)),
                  pl.BlockSpec((SEQ, MLA_ROPE), lambda b, h: (0, 0)),
                  pl.BlockSpec((SEQ, MLA_ROPE), lambda b, h: (0, 0))],
        out_specs=pl.BlockSpec((None, SEQ, hps * MLA_V), lambda b, h: (b, 0, h)),
        out_shape=jax.ShapeDtypeStruct((BATCH, SEQ, MLA_HEADS * MLA_V), BF16),
        scratch_shapes=[pltpu.VMEM((SEQ, MLA_Q_LORA), BF16),
                        pltpu.VMEM((SEQ, MLA_KV_LORA), BF16),
                        pltpu.VMEM((SEQ, MLA_ROPE), BF16),
                        pltpu.VMEM((hps, SEQ, MLA_QK), BF16),
                        pltpu.VMEM((hps, SEQ, MLA_QK), BF16),
                        pltpu.VMEM((hps, SEQ, 2 * MLA_V), BF16)],
        compiler_params=_params("arbitrary", "arbitrary"),
        name="mla",
    )(lat3, q_norm, kv_norm, wq, wkv, cos, sin)


def _route(logits_t):
    tm = logits_t.shape[1]
    row = lax.broadcasted_iota(jnp.int32, (SUBLANES, tm), 0)
    neg = -jnp.inf
    gl = jnp.where(row < N_GROUPS, logits_t[:SUBLANES], neg)
    gmax = jnp.max(gl, axis=0, keepdims=True)
    gsel = jnp.min(jnp.where(gl == gmax, row, SUBLANES), axis=0, keepdims=True)
    p_grp = 1.0 / jnp.sum(jnp.exp(gl - gmax), axis=0, keepdims=True)
    el = logits_t[SUBLANES * N_GROUPS:SUBLANES * (N_GROUPS + 1)]
    for g in reversed(range(N_GROUPS - 1)):
        el = jnp.where(gsel == g, logits_t[SUBLANES * (g + 1):SUBLANES * (g + 2)], el)
    v0 = jnp.max(el, axis=0, keepdims=True)
    i0 = jnp.min(jnp.where(el == v0, row, SUBLANES), axis=0, keepdims=True)
    el1 = jnp.where(row == i0, neg, el)
    v1 = jnp.max(el1, axis=0, keepdims=True)
    i1 = jnp.min(jnp.where(el1 == v1, row, SUBLANES), axis=0, keepdims=True)
    t = jnp.exp(v1 - v0)
    w0 = p_grp / (1.0 + t)
    w1 = p_grp * t / (1.0 + t)
    return gsel * EXPERTS_PER_GROUP + i0, gsel * EXPERTS_PER_GROUP + i1, w0, w1


def _stack_rows(rows, n):
    tm = rows[0].shape[1]
    row = lax.broadcasted_iota(jnp.int32, (n, tm), 0)
    out = jnp.zeros((n, tm), F32)
    for k, r in enumerate(rows):
        out = jnp.where(row == k, r, out)
    return out


def _to_token_tiles(ref, val):
    n = val.shape[0]
    for s in range(ROW_TILES):
        ref[pl.ds(s, n, stride=ROW_TILES), :] = val[:, s * LANES:(s + 1) * LANES]


def _from_token_tiles(ref):
    n = ref.shape[0] // ROW_TILES
    return jnp.concatenate([ref[pl.ds(s, n, stride=ROW_TILES), :] for s in range(ROW_TILES)], axis=-1)


def _token_rows(ref, t):
    return ref.at[pl.ds(pl.multiple_of(t * ROW_TILES, ROW_TILES), ROW_TILES)]


def _merge_kernel(yr_ref, at_ref, gr_ref, gm_ref, x_ref, mod_ref, n2_ref, wro_ref, wmo_ref, wo_ref,
                  wrt_ref, brt_ref, h1_ref, u2_ref, meta_ref, wtok_ref, cnt_ref, carry_ref):
    tm = x_ref.shape[0]

    @pl.when(pl.program_id(0) == 0)
    def _():
        carry_ref[...] = jnp.zeros_like(carry_ref)

    chunks = range(MERGE_CHAINS)
    sub = tm // MERGE_CHAINS
    rows = [slice(c * sub, (c + 1) * sub) for c in chunks]
    y_ret = [_dot(yr_ref[r, :], wro_ref[...]) for r in rows]
    y_mla = [_dot(at_ref[r, :], wmo_ref[...]) for r in rows]
    merged = [(_sigmoid(gr_ref[rows[c], :].astype(F32)) * y_ret[c]
               + _sigmoid(gm_ref[rows[c], :].astype(F32)) * y_mla[c]).astype(BF16) for c in chunks]
    o = [_dot(merged[c], wo_ref[...]) for c in chunks]
    h1 = [x_ref[rows[c], :] + mod_ref[2:3, :] * o[c] for c in chunks]
    for c in chunks:
        h1_ref[rows[c], :] = h1[c]
    u2 = [_rms(h1[c]) * n2_ref[...] * (1.0 + mod_ref[4:5, :]) + mod_ref[3:4, :] for c in chunks]
    for c in chunks:
        _to_token_tiles(u2_ref.at[pl.ds(c * sub * ROW_TILES, sub * ROW_TILES)], u2[c])
    w = wrt_ref[...]
    w_hi = w.astype(BF16)
    w_lo = (w - w_hi.astype(F32)).astype(BF16)
    w_both = jnp.concatenate([w_hi, w_lo], axis=0)
    u_hi = [u2[c].astype(BF16) for c in chunks]
    u_lo = [(u2[c] - u_hi[c].astype(F32)).astype(BF16) for c in chunks]
    by_hi = [_dot_nt(w_both, u_hi[c]) for c in chunks]
    logits_t = [by_hi[c][:ROUTER_ROWS] + by_hi[c][ROUTER_ROWS:] + _dot_nt(w_hi, u_lo[c]) + brt_ref[:, 0:1]
                for c in chunks]
    routed = [_route(logits_t[c]) for c in chunks]
    e0, e1, w0, w1 = [jnp.concatenate([routed[c][k] for c in chunks], axis=1) for k in range(4)]
    erow = lax.broadcasted_iota(jnp.int32, (N_EXPERTS, tm), 0)
    m0, m1 = erow == e0, erow == e1
    member = jnp.where(m0 | m1, 1.0, 0.0)
    earlier = jnp.where(lax.broadcasted_iota(jnp.int32, (tm, tm), 0) < lax.broadcasted_iota(jnp.int32, (tm, tm), 1),
                        1.0, 0.0).astype(BF16)
    prefix = _dot(member.astype(BF16), earlier) + carry_ref[:, 0:1]
    rank0 = jnp.sum(jnp.where(m0, prefix, 0.0), axis=0, keepdims=True)
    rank1 = jnp.sum(jnp.where(m1, prefix, 0.0), axis=0, keepdims=True)
    carry_ref[...] = carry_ref[...] + jnp.sum(member, axis=1, keepdims=True)
    cnt_ref[...] = carry_ref[...]
    meta_ref[...] = _stack_rows([e0.astype(F32), e1.astype(F32), rank0, rank1], SUBLANES)
    wt = _stack_rows([w0, w1], 2 * SUBLANES)
    wt_hi = wt.astype(BF16)
    wt_lo = (wt - wt_hi.astype(F32)).astype(BF16)
    place = jnp.where(lax.broadcasted_iota(jnp.int32, (2 * SUBLANES, LANES), 0)
                      == lax.broadcasted_iota(jnp.int32, (2 * SUBLANES, LANES), 1), 1.0, 0.0).astype(BF16)
    wtok_ref[...] = _dot_tn(wt_hi, place) + _dot_tn(wt_lo, place)


def _merge(y_ret, attn, gates, x2, mod3, norm2, w_ret_o, w_mla_o, w_out, w_rt, b_rt):
    tm = TM_PROJ
    per_b = SEQ // tm
    row = lambda j: pl.BlockSpec((tm, D_MODEL), lambda i: (i, j))
    return pl.pallas_call(
        _merge_kernel,
        grid=(N_TOK // tm,),
        in_specs=[row(0), row(0), row(0), row(1), row(0),
                  pl.BlockSpec((None, 6, D_MODEL), lambda i: (i // per_b, 0, 0)),
                  _resident((1, D_MODEL)),
                  _resident(w_ret_o.shape), _resident(w_mla_o.shape), _resident(w_out.shape),
                  _resident(w_rt.shape), _resident(b_rt.shape)],
        out_specs=[row(0),
                   pl.BlockSpec((tm * ROW_TILES, LANES), lambda i: (i, 0)),
                   pl.BlockSpec((SUBLANES, tm), lambda i: (0, i)),
                   pl.BlockSpec((tm, LANES), lambda i: (i, 0)),
                   pl.BlockSpec((N_EXPERTS, LANES), lambda i: (0, 0))],
        out_shape=[jax.ShapeDtypeStruct((N_TOK, D_MODEL), F32),
                   jax.ShapeDtypeStruct((N_TOK * ROW_TILES, LANES), F32),
                   jax.ShapeDtypeStruct((SUBLANES, N_TOK), F32),
                   jax.ShapeDtypeStruct((N_TOK, LANES), F32),
                   jax.ShapeDtypeStruct((N_EXPERTS, LANES), F32)],
        scratch_shapes=[pltpu.VMEM((N_EXPERTS, LANES), F32)],
        compiler_params=_params("arbitrary"),
        name="merge",
    )(y_ret, attn, gates, gates, x2, mod3, norm2, w_ret_o, w_mla_o, w_out, w_rt, b_rt)


def _plan_kernel(meta_ref, off_ref, dst_ref):
    m = meta_ref[...]
    erow = lax.broadcasted_iota(jnp.int32, (N_EXPERTS, m.shape[1]), 0)
    off = off_ref[:, 0:1]
    d = [jnp.sum(jnp.where(erow == m[k:k + 1].astype(jnp.int32), off, 0.0), axis=0, keepdims=True) + m[k + 2:k + 3]
         for k in range(TOP_K)]
    dst_ref[...] = _stack_rows(d, SUBLANES).astype(jnp.int32)


def _plan(meta_t, off_col):
    tm = 2048
    return pl.pallas_call(
        _plan_kernel,
        grid=(N_TOK // tm,),
        in_specs=[pl.BlockSpec((SUBLANES, tm), lambda i: (0, i)),
                  pl.BlockSpec((N_EXPERTS, LANES), lambda i: (0, 0))],
        out_specs=pl.BlockSpec((SUBLANES, tm), lambda i: (0, i)),
        out_shape=jax.ShapeDtypeStruct((SUBLANES, N_TOK), jnp.int32),
        compiler_params=_params("arbitrary"),
        name="plan",
    )(meta_t, off_col)


def _row_copy_wait(src_like, dst_like, sem):
    pltpu.make_async_copy(src_like, dst_like, sem).wait()


def _dispatch_kernel(d0_ref, d1_ref, seg_ref, cnt_ref, u_ref, xs_ref, zero_ref, sem, zsem):
    i = pl.program_id(0)
    tm = u_ref.shape[0] // ROW_TILES
    tile_rows = TE * ROW_TILES

    def slot_tile(j):
        return xs_ref.at[pl.ds(pl.multiple_of(j * tile_rows, tile_rows), tile_rows)]

    @pl.when(i == 0)
    def _():
        zero_ref[...] = jnp.zeros_like(zero_ref)

        def pad(e):
            first = seg_ref[e] + cnt_ref[e]
            rows = (seg_ref[e + 1] - first) * ROW_TILES
            start = pl.multiple_of(first * ROW_TILES, ROW_TILES)
            return rows > 0, pltpu.make_async_copy(zero_ref.at[pl.ds(0, rows)], xs_ref.at[pl.ds(start, rows)], zsem)

        for e in range(N_EXPERTS):
            nonempty, copy = pad(e)
            pl.when(nonempty)(copy.start)
        for e in range(N_EXPERTS):
            nonempty, copy = pad(e)
            pl.when(nonempty)(copy.wait)

        def unused(j):
            return pltpu.make_async_copy(zero_ref, slot_tile(j), zsem)

        first_unused = seg_ref[N_EXPERTS] // TE
        lax.fori_loop(first_unused, N_TILES, lambda j, c: (unused(j).start(), c)[1], 0)
        lax.fori_loop(first_unused, N_TILES, lambda j, c: (unused(j).wait(), c)[1], 0)

    base = i * tm

    def body(r, carry):
        src = _token_rows(u_ref, r)
        pltpu.make_async_copy(src, _token_rows(xs_ref, d0_ref[base + r]), sem).start(priority=0)
        pltpu.make_async_copy(src, _token_rows(xs_ref, d1_ref[base + r]), sem).start(priority=1)
        return carry

    lax.fori_loop(0, tm, body, 0, unroll=8)
    _row_copy_wait(u_ref, xs_ref.at[pl.ds(0, tm * ROW_TILES)], sem)
    _row_copy_wait(u_ref, xs_ref.at[pl.ds(0, tm * ROW_TILES)], sem)


def _dispatch(d0, d1, seg, cnt, u2t):
    tm = TM_PROJ
    return pl.pallas_call(
        _dispatch_kernel,
        grid_spec=pltpu.PrefetchScalarGridSpec(
            num_scalar_prefetch=4,
            grid=(N_TOK // tm,),
            in_specs=[pl.BlockSpec((tm * ROW_TILES, LANES), lambda i, *_: (i, 0))],
            out_specs=pl.BlockSpec(memory_space=pl.ANY),
            scratch_shapes=[pltpu.VMEM((TE * ROW_TILES, LANES), F32),
                            pltpu.SemaphoreType.DMA(()), pltpu.SemaphoreType.DMA(())]),
        out_shape=jax.ShapeDtypeStruct((N_SLOTS * ROW_TILES, LANES), F32),
        compiler_params=_params("arbitrary"),
        name="dispatch",
    )(d0, d1, seg, cnt, u2t)


def _expert_kernel(te_ref, nv_ref, x_ref, w1_ref, w3_ref, w2_ref, y_ref, w1_s, w3_s, w2_s):
    j = pl.program_id(0)

    @pl.when(j < nv_ref[0])
    def _():
        @pl.when((j == 0) | (te_ref[j] != te_ref[jnp.maximum(j - 1, 0)]))
        def _():
            w1_s[...] = w1_ref[...].astype(BF16)
            w3_s[...] = w3_ref[...].astype(BF16)
            w2_s[...] = w2_ref[...].astype(BF16)

        sub = TE // TE_CHAINS * ROW_TILES
        part = lambda ref, c: ref.at[pl.ds(c * sub, sub)]
        chains = range(TE_CHAINS)
        x = [_from_token_tiles(part(x_ref, c)).astype(BF16) for c in chains]
        a = [_dot(x[c], w1_s[...]) for c in chains]
        b = [_dot(x[c], w3_s[...]) for c in chains]
        hid = [(a[c] * _sigmoid(a[c]) * b[c]).astype(BF16) for c in chains]
        for c in chains:
            _to_token_tiles(part(y_ref, c), _dot(hid[c], w2_s[...]))


def _experts(tile_expert, n_valid, xs, w1, w3, w2):
    tile = lambda j, te, nv: jnp.minimum(j, nv[0] - 1)
    wspec = lambda shape: pl.BlockSpec((None,) + shape, lambda j, te, nv: (te[tile(j, te, nv)], 0, 0))
    slots = pl.BlockSpec((TE * ROW_TILES, LANES), lambda j, te, nv: (tile(j, te, nv), 0))
    return pl.pallas_call(
        _expert_kernel,
        grid_spec=pltpu.PrefetchScalarGridSpec(
            num_scalar_prefetch=2,
            grid=(N_TILES,),
            in_specs=[slots, wspec((D_MODEL, D_EXPERT)), wspec((D_MODEL, D_EXPERT)), wspec((D_EXPERT, D_MODEL))],
            out_specs=slots,
            scratch_shapes=[pltpu.VMEM((D_MODEL, D_EXPERT), BF16), pltpu.VMEM((D_MODEL, D_EXPERT), BF16),
                            pltpu.VMEM((D_EXPERT, D_MODEL), BF16)]),
        out_shape=jax.ShapeDtypeStruct((N_SLOTS * ROW_TILES, LANES), F32),
        input_output_aliases={2: 0},
        compiler_params=_params("arbitrary"),
        name="experts",
    )(tile_expert, n_valid, xs, w1, w3, w2)


def _final_kernel(d0_ref, d1_ref, h1_ref, meta_ref, mod_ref, fn_ref, ys_ref, o_ref, ybuf, sem):
    i = pl.program_id(0)
    tm = h1_ref.shape[0]

    def gather(t):
        buf, s = ybuf.at[t % 2], sem.at[t % 2]
        base = t * tm

        def body(r, carry):
            pltpu.make_async_copy(_token_rows(ys_ref, d0_ref[base + r]), _token_rows(buf.at[0], r), s).start(priority=0)
            pltpu.make_async_copy(_token_rows(ys_ref, d1_ref[base + r]), _token_rows(buf.at[1], r), s).start(priority=1)
            return carry

        lax.fori_loop(0, tm, body, 0, unroll=8)

    @pl.when(i == 0)
    def _():
        gather(0)

    @pl.when(i + 1 < pl.num_programs(0))
    def _():
        gather(i + 1)

    buf = ybuf.at[i % 2]
    for k in range(TOP_K):
        _row_copy_wait(ys_ref.at[pl.ds(0, tm * ROW_TILES)], buf.at[k], sem.at[i % 2])
    m = meta_ref[...]
    moe = m[:, 0:1] * _from_token_tiles(buf.at[0]) + m[:, 1:2] * _from_token_tiles(buf.at[1])
    h2 = h1_ref[...] + mod_ref[5:6, :] * moe
    o_ref[...] = _rms(h2) * fn_ref[...]


def _final(d0, d1, h1, meta, mod3, final_norm, ys):
    tm = TM_PROJ
    per_b = SEQ // tm
    return pl.pallas_call(
        _final_kernel,
        grid_spec=pltpu.PrefetchScalarGridSpec(
            num_scalar_prefetch=2,
            grid=(N_TOK // tm,),
            in_specs=[pl.BlockSpec((tm, D_MODEL), lambda i, *_: (i, 0)),
                      pl.BlockSpec((tm, LANES), lambda i, *_: (i, 0)),
                      pl.BlockSpec((None, 6, D_MODEL), lambda i, *_: (i // per_b, 0, 0)),
                      pl.BlockSpec((1, D_MODEL), lambda i, *_: (0, 0)),
                      pl.BlockSpec(memory_space=pl.ANY)],
            out_specs=pl.BlockSpec((tm, D_MODEL), lambda i, *_: (i, 0)),
            scratch_shapes=[pltpu.VMEM((2, TOP_K, tm * ROW_TILES, LANES), F32), pltpu.SemaphoreType.DMA((2,))]),
        out_shape=jax.ShapeDtypeStruct((N_TOK, D_MODEL), F32),
        compiler_params=_params("arbitrary"),
        name="final",
    )(d0, d1, h1, meta, mod3, final_norm, ys)


def _slot_layout(counts):
    cnt = counts[:, 0].astype(jnp.int32)
    tile_end = jnp.cumsum((cnt + TE - 1) // TE)
    seg = jnp.concatenate([jnp.zeros((1,), jnp.int32), tile_end * TE])
    off_col = jnp.broadcast_to(seg[:-1].astype(F32)[:, None], (N_EXPERTS, LANES))
    tile_ids = jnp.arange(N_TILES, dtype=jnp.int32)
    tile_expert = jnp.sum((tile_end[None, :] <= tile_ids[:, None]).astype(jnp.int32), axis=1)
    tile_expert = jnp.minimum(tile_expert, N_EXPERTS - 1)
    return cnt, seg, off_col, tile_expert, tile_end[-1:]


def _rope_tables(dim):
    pos = np.arange(SEQ, dtype=np.float64)
    inv = ROPE_THETA ** (-np.arange(0, dim, 2, dtype=np.float64) / dim)
    ang = pos[:, None] * inv[None, :]
    return np.cos(ang).astype(np.float32), np.sin(ang).astype(np.float32)


def _decay_tables():
    c = RET_CHUNK
    log_gamma = np.log1p(-np.exp2(-5.0 - np.arange(RET_HEADS, dtype=np.float64)))
    idx = np.arange(c, dtype=np.float64)
    rel = idx[:, None] - idx[None, :]
    dec = np.where(rel[None] >= 0, np.exp(log_gamma[:, None, None] * np.maximum(rel, 0.0)[None]), 0.0)
    xi = np.exp(log_gamma[:, None] * (idx[None, :] + 1.0))[:, :, None]
    zeta = np.exp(log_gamma[:, None] * (c - 1.0 - idx[None, :]))[:, :, None]
    cd = np.exp(log_gamma * c)[:, None, None]
    return tuple(jnp.asarray(t.astype(np.float32)) for t in (dec, xi, zeta, cd))


def _rotate_half_cols(w):
    half = w.shape[-1] // 2
    return jnp.concatenate([-w[..., half:], w[..., :half]], axis=-1)


def kernel(x, c, w_ada, b_ada, norm1, norm2, w_in, w_ret_o, q_norm, kv_norm, w_uq, w_ukv, w_mla_o, w_out,
           w_grp, b_grp, w_exp, b_exp, w1, w3, w2, final_norm):
    assert x.shape == (BATCH, SEQ, D_MODEL) and w_ada.shape[0] == 1
    x2 = x.reshape(N_TOK, D_MODEL)

    w_ret, w_lat, w_gate = _wprep(jnp.transpose(w_in[0]))
    wq = w_uq[0].reshape(MLA_Q_LORA, MLA_HEADS, MLA_QK)
    wq = jnp.concatenate([wq, _rotate_half_cols(wq[..., MLA_NOPE:])], axis=-1)
    wq = wq.transpose(1, 0, 2).astype(BF16)
    wkv = w_ukv[0].reshape(MLA_KV_LORA, MLA_HEADS, MLA_NOPE + MLA_V).transpose(1, 0, 2).astype(BF16)
    gap = jnp.zeros((SUBLANES - N_GROUPS, D_MODEL), F32)
    tail = jnp.zeros((ROUTER_ROWS - SUBLANES - N_EXPERTS, D_MODEL), F32)
    w_rt = jnp.concatenate([w_grp[0].T, gap, w_exp[0].T, tail], axis=0)
    b_rt = jnp.concatenate([b_grp[0], gap[:, 0], b_exp[0], tail[:, 0]])
    b_rt = jnp.broadcast_to(b_rt[:, None], (ROUTER_ROWS, LANES))

    ret_cos, ret_sin = (jnp.asarray(t) for t in _rope_tables(RET_DK))
    mla_cos, mla_sin = (jnp.asarray(np.concatenate([t, t], axis=-1)) for t in _rope_tables(MLA_ROPE))
    dec, xi, zeta, cd = _decay_tables()

    mod3 = _ada(c, w_ada[0], b_ada[0]).reshape(BATCH, 6, D_MODEL)
    ret, lat, gates = _inproj(x2, mod3, norm1, ret_cos, ret_sin, w_ret, w_lat, w_gate)
    y_ret = _retention(ret.reshape(BATCH, SEQ, 4 * RET_W), dec, xi, zeta, cd)
    attn = _mla(lat.reshape(BATCH, SEQ, MLA_LAT_W), q_norm, kv_norm, wq, wkv, mla_cos, mla_sin)
    h1, u2t, meta_t, wtok, counts = _merge(y_ret.reshape(N_TOK, D_MODEL), attn.reshape(N_TOK, D_MODEL), gates, x2,
                                           mod3, norm2, w_ret_o[0].astype(BF16), w_mla_o[0].astype(BF16),
                                           w_out[0].astype(BF16), w_rt, b_rt)
    cnt, seg, off_col, tile_expert, n_valid = _slot_layout(counts)
    dst = _plan(meta_t, off_col)
    d0, d1 = dst[0], dst[1]
    xs = _dispatch(d0, d1, seg, cnt, u2t)
    e_shape = (N_EXPERTS, D_MODEL, D_EXPERT)
    ys = _experts(tile_expert, n_valid, xs, w1[0].reshape(e_shape), w3[0].reshape(e_shape),
                  w2[0].reshape(N_EXPERTS, D_EXPERT, D_MODEL))
    out = _final(d0, d1, h1, wtok, mod3, final_norm.reshape(1, D_MODEL), ys)
    return out.reshape(BATCH, SEQ, D_MODEL)
```

```python
import numpy as np
import jax
import jax.numpy as jnp
from jax import lax
from jax.experimental import pallas as pl
from jax.experimental.pallas import tpu as pltpu

D_MODEL = 1024
BATCH = 8
SEQ = 2048
N_TOK = BATCH * SEQ

RET_HEADS = 4
RET_DK = 256
RET_DV = 256
RET_CHUNK = 256
RET_BLK = 512
RET_W = RET_HEADS * RET_DK

MLA_HEADS = 8
MLA_NOPE = 128
MLA_ROPE = 64
MLA_V = 128
MLA_Q_LORA = 384
MLA_KV_LORA = 256
MLA_LAT_W = MLA_Q_LORA + MLA_KV_LORA + 2 * MLA_ROPE
MLA_QK = MLA_NOPE + MLA_ROPE
ROPE_THETA = 10000.0

N_GROUPS = 4
EXPERTS_PER_GROUP = 8
N_EXPERTS = N_GROUPS * EXPERTS_PER_GROUP
D_EXPERT = 256
EPS = 1e-6
LOG2_E = 1.4426950408889634

LANES = 128
SUBLANES = 8
ROUTER_ROWS = 48
ROW_TILES = D_MODEL // LANES
VMEM_LIMIT = 56 * 1024 * 1024

TM_PROJ = 512
TM_DISPATCH = 2048
TQ = 256
MLA_HPS = 2
TE = 512
TE_CHAINS = 2
MERGE_CHAINS = 2
TOP_K = 2
N_TILES = N_TOK * TOP_K // TE + N_EXPERTS
N_SLOTS = N_TILES * TE

F32 = jnp.float32
BF16 = jnp.bfloat16


def _sigmoid(x):
    return 1.0 / (1.0 + jnp.exp(-x))


def _rms(x):
    return x * lax.rsqrt(jnp.mean(x * x, axis=-1, keepdims=True) + EPS)


def _dot(a, b):
    return jnp.dot(a, b, preferred_element_type=F32)


def _dot_nt(a, b):
    return lax.dot_general(a, b, (((1,), (1,)), ((), ())), preferred_element_type=F32)


def _dot_tn(a, b):
    return lax.dot_general(a, b, (((0,), (0,)), ((), ())), preferred_element_type=F32)


def _params(*sem):
    return pltpu.CompilerParams(dimension_semantics=sem, vmem_limit_bytes=VMEM_LIMIT)


def _resident(shape):
    nd = len(shape)
    return pl.BlockSpec(shape, lambda *_: (0,) * nd, pipeline_mode=pl.Buffered(1))


def _ada_kernel(c_ref, w_ref, b_ref, o_ref):
    c = c_ref[...]
    act = (c * _sigmoid(c)).astype(BF16)
    o_ref[...] = _dot(act, w_ref[...].astype(BF16)) + b_ref[...]


def _ada(c, w_ada, b_ada):
    n = w_ada.shape[1]
    tn = D_MODEL
    return pl.pallas_call(
        _ada_kernel,
        grid=(n // tn,),
        in_specs=[pl.BlockSpec((BATCH, D_MODEL), lambda j: (0, 0)),
                  pl.BlockSpec((D_MODEL, tn), lambda j: (0, j)),
                  pl.BlockSpec((1, tn), lambda j: (0, j))],
        out_specs=pl.BlockSpec((BATCH, tn), lambda j: (0, j)),
        out_shape=jax.ShapeDtypeStruct((BATCH, n), F32),
        compiler_params=_params("arbitrary"),
        name="ada",
    )(c, w_ada, b_ada.reshape(1, n))


O_LAT = 4 * RET_W
O_PE = O_LAT + MLA_Q_LORA + MLA_KV_LORA
O_GATE = O_PE + MLA_ROPE


N_IN = O_GATE + 2 * D_MODEL
WPREP_ROWS = 512


def _wprep_kernel(w_hbm, ret_ref, lat_ref, gate_ref, buf, sem):
    chunks = [(s, min(WPREP_ROWS, N_IN - s)) for s in range(0, N_IN, WPREP_ROWS)]

    def copy(i):
        s, n = chunks[i]
        return pltpu.make_async_copy(w_hbm.at[pl.ds(s, n)], buf.at[i % 2, pl.ds(0, n)], sem.at[i % 2])

    groups = [(0, RET_W, ret_ref, 0, 1.0), (RET_W, 2 * RET_W, ret_ref, RET_W, RET_DK ** -0.5),
              (2 * RET_W, O_LAT, ret_ref, 2 * RET_W, 1.0), (O_LAT, O_GATE, lat_ref, 0, 1.0),
              (O_GATE, N_IN, gate_ref, 0, 1.0)]
    half = MLA_ROPE // 2
    rot = [(O_PE + half, O_GATE, O_GATE - O_LAT, -1.0), (O_PE, O_PE + half, O_GATE - O_LAT + half, 1.0)]

    copy(0).start()
    for i, (s, n) in enumerate(chunks):
        if i + 1 < len(chunks):
            copy(i + 1).start()
        copy(i).wait()
        for lo, hi, dst, dst_lo, scale in groups + [(a, b, lat_ref, d, sc) for a, b, d, sc in rot]:
            a, b = max(lo, s), min(hi, s + n)
            if a < b:
                x = buf[i % 2, a - s:b - s, :]
                dst[dst_lo + a - lo:dst_lo + b - lo, :] = (x if scale == 1.0 else x * scale).astype(BF16)


def _wprep(w_in_t):
    out_rows = (O_LAT, MLA_LAT_W, 2 * D_MODEL)
    whole = lambda n: pl.BlockSpec((n, D_MODEL), lambda i: (0, 0))
    return pl.pallas_call(
        _wprep_kernel,
        grid=(1,),
        in_specs=[pl.BlockSpec(memory_space=pl.ANY)],
        out_specs=[whole(n) for n in out_rows],
        out_shape=[jax.ShapeDtypeStruct((n, D_MODEL), BF16) for n in out_rows],
        scratch_shapes=[pltpu.VMEM((2, WPREP_ROWS, D_MODEL), F32), pltpu.SemaphoreType.DMA((2,))],
        compiler_params=_params("arbitrary"),
        name="wprep",
    )(w_in_t)


def _inproj_kernel(x_ref, mod_ref, n1_ref, cos_ref, sin_ref, wr_ref, wm_ref, wg_ref, ret_ref, lat_ref, gate_ref):
    y = _rms(x_ref[...]) * n1_ref[...]
    u = (y * (1.0 + mod_ref[1:2, :]) + mod_ref[0:1, :]).astype(BF16)
    cos, sin = cos_ref[...], sin_ref[...]
    half = RET_DK // 2
    for n in range(0, 2 * RET_W, RET_DK):
        p = _dot_nt(u, wr_ref[n:n + RET_DK, :])
        x1, x2 = p[:, :half], p[:, half:]
        ret_ref[:, n:n + half] = (x1 * cos - x2 * sin).astype(BF16)
        ret_ref[:, n + half:n + RET_DK] = (x2 * cos + x1 * sin).astype(BF16)
    step = 512
    for n in range(2 * RET_W, 3 * RET_W, step):
        ret_ref[:, n:n + step] = _dot_nt(u, wr_ref[n:n + step, :]).astype(BF16)
    for n in range(3 * RET_W, 4 * RET_W, step):
        p = _dot_nt(u, wr_ref[n:n + step, :])
        ret_ref[:, n:n + step] = (p * _sigmoid(p)).astype(BF16)
    lat_ref[...] = _dot_nt(u, wm_ref[...]).astype(BF16)
    for n in range(0, 2 * D_MODEL, step):
        gate_ref[:, n:n + step] = _dot_nt(u, wg_ref[n:n + step, :]).astype(BF16)


def _inproj(x2, mod3, norm1, cos, sin, w_ret, w_lat, w_gate):
    tm = TM_PROJ
    per_b = SEQ // tm
    rope_tab = pl.BlockSpec((tm, RET_DK // 2), lambda i: (i % per_b, 0))
    return pl.pallas_call(
        _inproj_kernel,
        grid=(N_TOK // tm,),
        in_specs=[pl.BlockSpec((tm, D_MODEL), lambda i: (i, 0)),
                  pl.BlockSpec((None, 6, D_MODEL), lambda i: (i // per_b, 0, 0)),
                  _resident((1, D_MODEL)), rope_tab, rope_tab,
                  _resident(w_ret.shape), _resident(w_lat.shape), _resident(w_gate.shape)],
        out_specs=[pl.BlockSpec((tm, 4 * RET_W), lambda i: (i, 0)),
                   pl.BlockSpec((tm, MLA_LAT_W), lambda i: (i, 0)),
                   pl.BlockSpec((tm, 2 * D_MODEL), lambda i: (i, 0))],
        out_shape=[jax.ShapeDtypeStruct((N_TOK, 4 * RET_W), BF16),
                   jax.ShapeDtypeStruct((N_TOK, MLA_LAT_W), BF16),
                   jax.ShapeDtypeStruct((N_TOK, 2 * D_MODEL), BF16)],
        compiler_params=_params("arbitrary"),
        name="inproj",
    )(x2, mod3, norm1, cos, sin, w_ret, w_lat, w_gate)


def _ret_kernel(q_ref, k_ref, v_ref, g_ref, dec_ref, xi_ref, zeta_ref, cd_ref, o_ref, state_ref):
    @pl.when(pl.program_id(1) == 0)
    def _():
        state_ref[...] = jnp.zeros_like(state_ref)

    heads = range(RET_HEADS)
    col = lambda h: slice(h * RET_DK, (h + 1) * RET_DK)
    for c in range(RET_BLK // RET_CHUNK):
        rows = slice(c * RET_CHUNK, (c + 1) * RET_CHUNK)
        scores = [(_dot_nt(q_ref[rows, col(h)], k_ref[rows, col(h)]) * dec_ref[h]).astype(BF16) for h in heads]
        carried = [xi_ref[h] * _dot(q_ref[rows, col(h)], state_ref[h].astype(BF16)) for h in heads]
        y = [_dot(scores[h], v_ref[rows, col(h)]) + carried[h] for h in heads]
        for h in heads:
            k_dec = (k_ref[rows, col(h)].astype(F32) * zeta_ref[h]).astype(BF16)
            state_ref[h] = state_ref[h] * cd_ref[h] + _dot_tn(k_dec, v_ref[rows, col(h)])
        yc = [y[h] - jnp.mean(y[h], axis=-1, keepdims=True) for h in heads]
        inv = [lax.rsqrt(jnp.mean(yc[h] * yc[h], axis=-1, keepdims=True) + EPS) for h in heads]
        for h in heads:
            o_ref[rows, col(h)] = (g_ref[rows, col(h)].astype(F32) * (yc[h] * inv[h])).astype(BF16)


def _retention(ret3, dec, xi, zeta, cd):
    blk = lambda part: pl.BlockSpec((None, RET_BLK, RET_W), lambda b, j: (b, j, part))
    whole = lambda a: pl.BlockSpec(a.shape, lambda b, j: (0,) * a.ndim)
    return pl.pallas_call(
        _ret_kernel,
        grid=(BATCH, SEQ // RET_BLK),
        in_specs=[blk(0), blk(1), blk(2), blk(3), whole(dec), whole(xi), whole(zeta), whole(cd)],
        out_specs=pl.BlockSpec((None, RET_BLK, RET_HEADS * RET_DV), lambda b, j: (b, j, 0)),
        out_shape=jax.ShapeDtypeStruct((BATCH, SEQ, RET_HEADS * RET_DV), BF16),
        scratch_shapes=[pltpu.VMEM((RET_HEADS, RET_DK, RET_DV), F32)],
        compiler_params=_params("arbitrary", "arbitrary"),
        name="retention",
    )(ret3, ret3, ret3, ret3, dec, xi, zeta, cd)


def _mla_kernel(lat_ref, qn_ref, kvn_ref, wq_ref, wkv_ref, cos_ref, sin_ref,
                rq_ref, rk_ref, rv_ref, rg_ref, dec_ref, xi_ref, zeta_ref, cd_ref, o_ref, yr_ref,
                cq_s, ckv_s, kpe_s, q_s, k_s, v_s, state_ref):
    h = pl.program_id(1)
    o_q, o_kv, o_pe, o_rot = 0, MLA_Q_LORA, MLA_Q_LORA + MLA_KV_LORA, MLA_Q_LORA + MLA_KV_LORA + MLA_ROPE
    cos, sin = cos_ref[...], sin_ref[...]

    @pl.when(h == 0)
    def _():
        cq_s[...] = (_rms(lat_ref[:, o_q:o_kv].astype(F32)) * qn_ref[...]).astype(BF16)
        ckv_s[...] = (_rms(lat_ref[:, o_kv:o_pe].astype(F32)) * kvn_ref[...]).astype(BF16)
        pe = lat_ref[:, o_pe:o_rot].astype(F32)
        rot = lat_ref[:, o_rot:o_rot + MLA_ROPE].astype(F32)
        kpe_s[...] = (pe * cos + rot * sin).astype(BF16)

    scale = (MLA_QK ** -0.5) * LOG2_E
    for g in range(MLA_HPS):
        qf = _dot(cq_s[...], wq_ref[g])
        q_s[g, :, :MLA_NOPE] = (qf[:, :MLA_NOPE] * scale).astype(BF16)
        q_pe = qf[:, MLA_NOPE:MLA_QK] * cos + qf[:, MLA_QK:] * sin
        q_s[g, :, MLA_NOPE:] = (q_pe * scale).astype(BF16)
        kvf = _dot(ckv_s[...], wkv_ref[g])
        k_s[g, :, :MLA_NOPE] = kvf[:, :MLA_NOPE].astype(BF16)
        k_s[g, :, MLA_NOPE:] = kpe_s[...]
        v_s[g, :, :MLA_V] = kvf[:, MLA_NOPE:].astype(BF16)
        v_s[g, :, MLA_V:] = jnp.ones((SEQ, MLA_V), BF16)

    causal = lax.broadcasted_iota(jnp.int32, (TQ, TQ), 0) >= lax.broadcasted_iota(jnp.int32, (TQ, TQ), 1)
    heads = range(MLA_HPS)
    n_blk = SEQ // TQ

    def scores(i):
        lo, hi = i * TQ, (i + 1) * TQ
        diag = [jnp.where(causal, _dot_nt(q_s[g, lo:hi, :], k_s[g, lo:hi, :]), -jnp.inf) for g in heads]
        past = [_dot_nt(q_s[g, lo:hi, :], k_s[g, :lo, :]) if i > 0 else None for g in heads]
        return diag, past

    state_ref[...] = jnp.zeros_like(state_ref)
    pending = scores(0)
    for i in range(n_blk):
        lo, hi = i * TQ, (i + 1) * TQ
        diag, past = pending
        if i + 1 < n_blk:
            pending = scores(i + 1)
        rq, rk, rv = rq_ref[lo:hi, :], rk_ref[lo:hi, :], rv_ref[lo:hi, :]
        r_scores = (_dot_nt(rq, rk) * dec_ref[h]).astype(BF16)
        carried = xi_ref[h] * _dot(rq, state_ref[...].astype(BF16))
        m = [jnp.max(diag[g], axis=-1, keepdims=True) for g in heads]
        if i > 0:
            m = [jnp.maximum(m[g], jnp.max(past[g], axis=-1, keepdims=True)) for g in heads]
        ry = _dot(r_scores, rv) + carried
        k_dec = (rk.astype(F32) * zeta_ref[h]).astype(BF16)
        state_ref[...] = state_ref[...] * cd_ref[h] + _dot_tn(k_dec, rv)
        acc = [_dot(jnp.exp2(diag[g] - m[g]).astype(BF16), v_s[g, lo:hi, :]) for g in heads]
        if i > 0:
            acc = [acc[g] + _dot(jnp.exp2(past[g] - m[g]).astype(BF16), v_s[g, :lo, :]) for g in heads]
        ryc = ry - jnp.mean(ry, axis=-1, keepdims=True)
        r_inv = lax.rsqrt(jnp.mean(ryc * ryc, axis=-1, keepdims=True) + EPS)
        yr_ref[lo:hi, :] = (rg_ref[lo:hi, :].astype(F32) * (ryc * r_inv)).astype(BF16)
        for g in heads:
            o_ref[lo:hi, g * MLA_V:(g + 1) * MLA_V] = (acc[g][:, :MLA_V] / acc[g][:, MLA_V:]).astype(BF16)


def _mla(lat3, q_norm, kv_norm, wq, wkv, cos, sin, ret3, dec, xi, zeta, cd):
    hps = MLA_HPS
    assert MLA_HEADS // hps == RET_HEADS and TQ == RET_CHUNK
    ret_part = lambda part: pl.BlockSpec((None, SEQ, RET_DK), lambda b, h: (b, 0, part * RET_HEADS + h))
    whole = lambda a: pl.BlockSpec(a.shape, lambda b, h: (0,) * a.ndim)
    return pl.pallas_call(
        _mla_kernel,
        grid=(BATCH, MLA_HEADS // hps),
        in_specs=[pl.BlockSpec((None, SEQ, MLA_LAT_W), lambda b, h: (b, 0, 0)),
                  pl.BlockSpec((1, MLA_Q_LORA), lambda b, h: (0, 0)),
                  pl.BlockSpec((1, MLA_KV_LORA), lambda b, h: (0, 0)),
                  pl.BlockSpec((hps, MLA_Q_LORA, MLA_QK + MLA_ROPE), lambda b, h: (h, 0, 0)),
                  pl.BlockSpec((hps, MLA_KV_LORA, MLA_NOPE + MLA_V), lambda b, h: (h, 0, 0)),
                  pl.BlockSpec((SEQ, MLA_ROPE), lambda b, h: (0, 0)),
                  pl.BlockSpec((SEQ, MLA_ROPE), lambda b, h: (0, 0)),
                  ret_part(0), ret_part(1), ret_part(2), ret_part(3),
                  whole(dec), whole(xi), whole(zeta), whole(cd)],
        out_specs=[pl.BlockSpec((None, SEQ, hps * MLA_V), lambda b, h: (b, 0, h)),
                   pl.BlockSpec((None, SEQ, RET_DV), lambda b, h: (b, 0, h))],
        out_shape=[jax.ShapeDtypeStruct((BATCH, SEQ, MLA_HEADS * MLA_V), BF16),
                   jax.ShapeDtypeStruct((BATCH, SEQ, RET_HEADS * RET_DV), BF16)],
        scratch_shapes=[pltpu.VMEM((SEQ, MLA_Q_LORA), BF16),
                        pltpu.VMEM((SEQ, MLA_KV_LORA), BF16),
                        pltpu.VMEM((SEQ, MLA_ROPE), BF16),
                        pltpu.VMEM((hps, SEQ, MLA_QK), BF16),
                        pltpu.VMEM((hps, SEQ, MLA_QK), BF16),
                        pltpu.VMEM((hps, SEQ, 2 * MLA_V), BF16),
                        pltpu.VMEM((RET_DK, RET_DV), F32)],
        compiler_params=_params("arbitrary", "arbitrary"),
        name="mla",
    )(lat3, q_norm, kv_norm, wq, wkv, cos, sin, ret3, ret3, ret3, ret3, dec, xi, zeta, cd)


def _route(logits_t):
    tm = logits_t.shape[1]
    row = lax.broadcasted_iota(jnp.int32, (SUBLANES, tm), 0)
    neg = -jnp.inf
    gl = jnp.where(row < N_GROUPS, logits_t[:SUBLANES], neg)
    gmax = jnp.max(gl, axis=0, keepdims=True)
    gsel = jnp.min(jnp.where(gl == gmax, row, SUBLANES), axis=0, keepdims=True)
    p_grp = 1.0 / jnp.sum(jnp.exp(gl - gmax), axis=0, keepdims=True)
    el = logits_t[SUBLANES * N_GROUPS:SUBLANES * (N_GROUPS + 1)]
    for g in reversed(range(N_GROUPS - 1)):
        el = jnp.where(gsel == g, logits_t[SUBLANES * (g + 1):SUBLANES * (g + 2)], el)
    v0 = jnp.max(el, axis=0, keepdims=True)
    i0 = jnp.min(jnp.where(el == v0, row, SUBLANES), axis=0, keepdims=True)
    el1 = jnp.where(row == i0, neg, el)
    v1 = jnp.max(el1, axis=0, keepdims=True)
    i1 = jnp.min(jnp.where(el1 == v1, row, SUBLANES), axis=0, keepdims=True)
    t = jnp.exp(v1 - v0)
    w0 = p_grp / (1.0 + t)
    w1 = p_grp * t / (1.0 + t)
    return gsel * EXPERTS_PER_GROUP + i0, gsel * EXPERTS_PER_GROUP + i1, w0, w1


def _stack_rows(rows, n):
    tm = rows[0].shape[1]
    row = lax.broadcasted_iota(jnp.int32, (n, tm), 0)
    out = jnp.zeros((n, tm), F32)
    for k, r in enumerate(rows):
        out = jnp.where(row == k, r, out)
    return out


def _to_token_tiles(ref, val):
    n = val.shape[0]
    for s in range(ROW_TILES):
        ref[pl.ds(s, n, stride=ROW_TILES), :] = val[:, s * LANES:(s + 1) * LANES]


def _from_token_tiles(ref):
    n = ref.shape[0] // ROW_TILES
    return jnp.concatenate([ref[pl.ds(s, n, stride=ROW_TILES), :] for s in range(ROW_TILES)], axis=-1)


def _token_rows(ref, t):
    return ref.at[pl.ds(pl.multiple_of(t * ROW_TILES, ROW_TILES), ROW_TILES)]


def _merge_kernel(yr_ref, at_ref, gr_ref, gm_ref, x_ref, mod_ref, n2_ref, wro_ref, wmo_ref, wo_ref,
                  wrt_ref, brt_ref, h1_ref, u2_ref, meta_ref, wtok_ref, cnt_ref, carry_ref):
    tm = x_ref.shape[0]

    @pl.when(pl.program_id(0) == 0)
    def _():
        carry_ref[...] = jnp.zeros_like(carry_ref)

    chunks = range(MERGE_CHAINS)
    sub = tm // MERGE_CHAINS
    rows = [slice(c * sub, (c + 1) * sub) for c in chunks]
    y_ret = [_dot(yr_ref[r, :], wro_ref[...]) for r in rows]
    y_mla = [_dot(at_ref[r, :], wmo_ref[...]) for r in rows]
    merged = [(_sigmoid(gr_ref[rows[c], :].astype(F32)) * y_ret[c]
               + _sigmoid(gm_ref[rows[c], :].astype(F32)) * y_mla[c]).astype(BF16) for c in chunks]
    o = [_dot(merged[c], wo_ref[...]) for c in chunks]
    h1 = [x_ref[rows[c], :] + mod_ref[2:3, :] * o[c] for c in chunks]
    for c in chunks:
        h1_ref[rows[c], :] = h1[c]
    u2 = [_rms(h1[c]) * n2_ref[...] * (1.0 + mod_ref[4:5, :]) + mod_ref[3:4, :] for c in chunks]
    for c in chunks:
        _to_token_tiles(u2_ref.at[pl.ds(c * sub * ROW_TILES, sub * ROW_TILES)], u2[c])
    w = wrt_ref[...]
    w_hi = w.astype(BF16)
    w_lo = (w - w_hi.astype(F32)).astype(BF16)
    w_both = jnp.concatenate([w_hi, w_lo], axis=0)
    u_hi = [u2[c].astype(BF16) for c in chunks]
    u_lo = [(u2[c] - u_hi[c].astype(F32)).astype(BF16) for c in chunks]
    by_hi = [_dot_nt(w_both, u_hi[c]) for c in chunks]
    logits_t = [by_hi[c][:ROUTER_ROWS] + by_hi[c][ROUTER_ROWS:] + _dot_nt(w_hi, u_lo[c]) + brt_ref[:, 0:1]
                for c in chunks]
    routed = [_route(logits_t[c]) for c in chunks]
    e0, e1, w0, w1 = [jnp.concatenate([routed[c][k] for c in chunks], axis=1) for k in range(4)]
    erow = lax.broadcasted_iota(jnp.int32, (N_EXPERTS, tm), 0)
    m0, m1 = erow == e0, erow == e1
    member = jnp.where(m0 | m1, 1.0, 0.0)
    earlier = jnp.where(lax.broadcasted_iota(jnp.int32, (tm, tm), 0) < lax.broadcasted_iota(jnp.int32, (tm, tm), 1),
                        1.0, 0.0).astype(BF16)
    prefix = _dot(member.astype(BF16), earlier) + carry_ref[:, 0:1]
    rank0 = jnp.sum(jnp.where(m0, prefix, 0.0), axis=0, keepdims=True)
    rank1 = jnp.sum(jnp.where(m1, prefix, 0.0), axis=0, keepdims=True)
    carry_ref[...] = carry_ref[...] + jnp.sum(member, axis=1, keepdims=True)
    cnt_ref[...] = carry_ref[...]
    meta_ref[...] = _stack_rows([e0.astype(F32), e1.astype(F32), rank0, rank1], SUBLANES)
    wt = _stack_rows([w0, w1], 2 * SUBLANES)
    wt_hi = wt.astype(BF16)
    wt_lo = (wt - wt_hi.astype(F32)).astype(BF16)
    place = jnp.where(lax.broadcasted_iota(jnp.int32, (2 * SUBLANES, LANES), 0)
                      == lax.broadcasted_iota(jnp.int32, (2 * SUBLANES, LANES), 1), 1.0, 0.0).astype(BF16)
    wtok_ref[...] = _dot_tn(wt_hi, place) + _dot_tn(wt_lo, place)


def _merge(y_ret, attn, gates, x2, mod3, norm2, w_ret_o, w_mla_o, w_out, w_rt, b_rt):
    tm = TM_PROJ
    per_b = SEQ // tm
    row = lambda j: pl.BlockSpec((tm, D_MODEL), lambda i: (i, j))
    return pl.pallas_call(
        _merge_kernel,
        grid=(N_TOK // tm,),
        in_specs=[row(0), row(0), row(0), row(1), row(0),
                  pl.BlockSpec((None, 6, D_MODEL), lambda i: (i // per_b, 0, 0)),
                  _resident((1, D_MODEL)),
                  _resident(w_ret_o.shape), _resident(w_mla_o.shape), _resident(w_out.shape),
                  _resident(w_rt.shape), _resident(b_rt.shape)],
        out_specs=[row(0),
                   pl.BlockSpec((tm * ROW_TILES, LANES), lambda i: (i, 0)),
                   pl.BlockSpec((SUBLANES, tm), lambda i: (0, i)),
                   pl.BlockSpec((tm, LANES), lambda i: (i, 0)),
                   pl.BlockSpec((N_EXPERTS, LANES), lambda i: (0, 0))],
        out_shape=[jax.ShapeDtypeStruct((N_TOK, D_MODEL), F32),
                   jax.ShapeDtypeStruct((N_TOK * ROW_TILES, LANES), F32),
                   jax.ShapeDtypeStruct((SUBLANES, N_TOK), F32),
                   jax.ShapeDtypeStruct((N_TOK, LANES), F32),
                   jax.ShapeDtypeStruct((N_EXPERTS, LANES), F32)],
        scratch_shapes=[pltpu.VMEM((N_EXPERTS, LANES), F32)],
        compiler_params=_params("arbitrary"),
        name="merge",
    )(y_ret, attn, gates, gates, x2, mod3, norm2, w_ret_o, w_mla_o, w_out, w_rt, b_rt)


def _plan_kernel(meta_ref, off_ref, dst_ref):
    m = meta_ref[...]
    erow = lax.broadcasted_iota(jnp.int32, (N_EXPERTS, m.shape[1]), 0)
    off = off_ref[:, 0:1]
    d = [jnp.sum(jnp.where(erow == m[k:k + 1].astype(jnp.int32), off, 0.0), axis=0, keepdims=True) + m[k + 2:k + 3]
         for k in range(TOP_K)]
    dst_ref[...] = _stack_rows(d, SUBLANES).astype(jnp.int32)


def _plan(meta_t, off_col):
    tm = 2048
    return pl.pallas_call(
        _plan_kernel,
        grid=(N_TOK // tm,),
        in_specs=[pl.BlockSpec((SUBLANES, tm), lambda i: (0, i)),
                  pl.BlockSpec((N_EXPERTS, LANES), lambda i: (0, 0))],
        out_specs=pl.BlockSpec((SUBLANES, tm), lambda i: (0, i)),
        out_shape=jax.ShapeDtypeStruct((SUBLANES, N_TOK), jnp.int32),
        compiler_params=_params("arbitrary"),
        name="plan",
    )(meta_t, off_col)


def _row_copy_wait(src_like, dst_like, sem):
    pltpu.make_async_copy(src_like, dst_like, sem).wait()


def _dispatch_kernel(d0_ref, d1_ref, seg_ref, cnt_ref, u_ref, xs_ref, zero_ref, sem, zsem):
    i = pl.program_id(0)
    tm = u_ref.shape[0] // ROW_TILES
    tile_rows = TE * ROW_TILES

    def slot_tile(j):
        return xs_ref.at[pl.ds(pl.multiple_of(j * tile_rows, tile_rows), tile_rows)]

    @pl.when(i == 0)
    def _():
        zero_ref[...] = jnp.zeros_like(zero_ref)

        def pad(e):
            first = seg_ref[e] + cnt_ref[e]
            rows = (seg_ref[e + 1] - first) * ROW_TILES
            start = pl.multiple_of(first * ROW_TILES, ROW_TILES)
            return rows > 0, pltpu.make_async_copy(zero_ref.at[pl.ds(0, rows)], xs_ref.at[pl.ds(start, rows)], zsem)

        for e in range(N_EXPERTS):
            nonempty, copy = pad(e)
            pl.when(nonempty)(copy.start)
        for e in range(N_EXPERTS):
            nonempty, copy = pad(e)
            pl.when(nonempty)(copy.wait)

        def unused(j):
            return pltpu.make_async_copy(zero_ref, slot_tile(j), zsem)

        first_unused = seg_ref[N_EXPERTS] // TE
        lax.fori_loop(first_unused, N_TILES, lambda j, c: (unused(j).start(), c)[1], 0)
        lax.fori_loop(first_unused, N_TILES, lambda j, c: (unused(j).wait(), c)[1], 0)

    base = i * tm

    def body(r, carry):
        src = _token_rows(u_ref, r)
        pltpu.make_async_copy(src, _token_rows(xs_ref, d0_ref[base + r]), sem).start(priority=0)
        pltpu.make_async_copy(src, _token_rows(xs_ref, d1_ref[base + r]), sem).start(priority=1)
        return carry

    lax.fori_loop(0, tm, body, 0, unroll=8)
    _row_copy_wait(u_ref, xs_ref.at[pl.ds(0, tm * ROW_TILES)], sem)
    _row_copy_wait(u_ref, xs_ref.at[pl.ds(0, tm * ROW_TILES)], sem)


def _dispatch(d0, d1, seg, cnt, u2t):
    tm = TM_DISPATCH
    return pl.pallas_call(
        _dispatch_kernel,
        grid_spec=pltpu.PrefetchScalarGridSpec(
            num_scalar_prefetch=4,
            grid=(N_TOK // tm,),
            in_specs=[pl.BlockSpec((tm * ROW_TILES, LANES), lambda i, *_: (i, 0))],
            out_specs=pl.BlockSpec(memory_space=pl.ANY),
            scratch_shapes=[pltpu.VMEM((TE * ROW_TILES, LANES), F32),
                            pltpu.SemaphoreType.DMA(()), pltpu.SemaphoreType.DMA(())]),
        out_shape=jax.ShapeDtypeStruct((N_SLOTS * ROW_TILES, LANES), F32),
        compiler_params=_params("arbitrary"),
        name="dispatch",
    )(d0, d1, seg, cnt, u2t)


def _expert_kernel(te_ref, nv_ref, x_ref, w1_ref, w3_ref, w2_ref, y_ref, w1_s, w3_s, w2_s):
    j = pl.program_id(0)

    @pl.when(j < nv_ref[0])
    def _():
        @pl.when((j == 0) | (te_ref[j] != te_ref[jnp.maximum(j - 1, 0)]))
        def _():
            w1_s[...] = w1_ref[...].astype(BF16)
            w3_s[...] = w3_ref[...].astype(BF16)
            w2_s[...] = w2_ref[...].astype(BF16)

        sub = TE // TE_CHAINS * ROW_TILES
        part = lambda ref, c: ref.at[pl.ds(c * sub, sub)]
        chains = range(TE_CHAINS)
        x = [_from_token_tiles(part(x_ref, c)).astype(BF16) for c in chains]
        a = [_dot(x[c], w1_s[...]) for c in chains]
        b = [_dot(x[c], w3_s[...]) for c in chains]
        hid = [(a[c] * _sigmoid(a[c]) * b[c]).astype(BF16) for c in chains]
        for c in chains:
            _to_token_tiles(part(y_ref, c), _dot(hid[c], w2_s[...]))


def _experts(tile_expert, n_valid, xs, w1, w3, w2):
    tile = lambda j, te, nv: jnp.minimum(j, nv[0] - 1)
    wspec = lambda shape: pl.BlockSpec((None,) + shape, lambda j, te, nv: (te[tile(j, te, nv)], 0, 0))
    slots = pl.BlockSpec((TE * ROW_TILES, LANES), lambda j, te, nv: (tile(j, te, nv), 0))
    return pl.pallas_call(
        _expert_kernel,
        grid_spec=pltpu.PrefetchScalarGridSpec(
            num_scalar_prefetch=2,
            grid=(N_TILES,),
            in_specs=[slots, wspec((D_MODEL, D_EXPERT)), wspec((D_MODEL, D_EXPERT)), wspec((D_EXPERT, D_MODEL))],
            out_specs=slots,
            scratch_shapes=[pltpu.VMEM((D_MODEL, D_EXPERT), BF16), pltpu.VMEM((D_MODEL, D_EXPERT), BF16),
                            pltpu.VMEM((D_EXPERT, D_MODEL), BF16)]),
        out_shape=jax.ShapeDtypeStruct((N_SLOTS * ROW_TILES, LANES), F32),
        input_output_aliases={2: 0},
        compiler_params=_params("arbitrary"),
        name="experts",
    )(tile_expert, n_valid, xs, w1, w3, w2)


def _final_kernel(d0_ref, d1_ref, h1_ref, meta_ref, mod_ref, fn_ref, ys_ref, o_ref, ybuf, sem):
    i = pl.program_id(0)
    tm = h1_ref.shape[0]

    def gather(t):
        buf, s = ybuf.at[t % 2], sem.at[t % 2]
        base = t * tm

        def body(r, carry):
            pltpu.make_async_copy(_token_rows(ys_ref, d0_ref[base + r]), _token_rows(buf.at[0], r), s).start(priority=0)
            pltpu.make_async_copy(_token_rows(ys_ref, d1_ref[base + r]), _token_rows(buf.at[1], r), s).start(priority=1)
            return carry

        lax.fori_loop(0, tm, body, 0, unroll=8)

    @pl.when(i == 0)
    def _():
        gather(0)

    @pl.when(i + 1 < pl.num_programs(0))
    def _():
        gather(i + 1)

    buf = ybuf.at[i % 2]
    for k in range(TOP_K):
        _row_copy_wait(ys_ref.at[pl.ds(0, tm * ROW_TILES)], buf.at[k], sem.at[i % 2])
    m = meta_ref[...]
    moe = m[:, 0:1] * _from_token_tiles(buf.at[0]) + m[:, 1:2] * _from_token_tiles(buf.at[1])
    h2 = h1_ref[...] + mod_ref[5:6, :] * moe
    o_ref[...] = _rms(h2) * fn_ref[...]


def _final(d0, d1, h1, meta, mod3, final_norm, ys):
    tm = TM_PROJ
    per_b = SEQ // tm
    return pl.pallas_call(
        _final_kernel,
        grid_spec=pltpu.PrefetchScalarGridSpec(
            num_scalar_prefetch=2,
            grid=(N_TOK // tm,),
            in_specs=[pl.BlockSpec((tm, D_MODEL), lambda i, *_: (i, 0)),
                      pl.BlockSpec((tm, LANES), lambda i, *_: (i, 0)),
                      pl.BlockSpec((None, 6, D_MODEL), lambda i, *_: (i // per_b, 0, 0)),
                      pl.BlockSpec((1, D_MODEL), lambda i, *_: (0, 0)),
                      pl.BlockSpec(memory_space=pl.ANY)],
            out_specs=pl.BlockSpec((tm, D_MODEL), lambda i, *_: (i, 0)),
            scratch_shapes=[pltpu.VMEM((2, TOP_K, tm * ROW_TILES, LANES), F32), pltpu.SemaphoreType.DMA((2,))]),
        out_shape=jax.ShapeDtypeStruct((N_TOK, D_MODEL), F32),
        compiler_params=_params("arbitrary"),
        name="final",
    )(d0, d1, h1, meta, mod3, final_norm, ys)


def _slot_layout(counts):
    cnt = counts[:, 0].astype(jnp.int32)
    tile_end = jnp.cumsum((cnt + TE - 1) // TE)
    seg = jnp.concatenate([jnp.zeros((1,), jnp.int32), tile_end * TE])
    off_col = jnp.broadcast_to(seg[:-1].astype(F32)[:, None], (N_EXPERTS, LANES))
    tile_ids = jnp.arange(N_TILES, dtype=jnp.int32)
    tile_expert = jnp.sum((tile_end[None, :] <= tile_ids[:, None]).astype(jnp.int32), axis=1)
    tile_expert = jnp.minimum(tile_expert, N_EXPERTS - 1)
    return cnt, seg, off_col, tile_expert, tile_end[-1:]


def _rope_tables(dim):
    pos = np.arange(SEQ, dtype=np.float64)
    inv = ROPE_THETA ** (-np.arange(0, dim, 2, dtype=np.float64) / dim)
    ang = pos[:, None] * inv[None, :]
    return np.cos(ang).astype(np.float32), np.sin(ang).astype(np.float32)


def _decay_tables():
    c = RET_CHUNK
    log_gamma = np.log1p(-np.exp2(-5.0 - np.arange(RET_HEADS, dtype=np.float64)))
    idx = np.arange(c, dtype=np.float64)
    rel = idx[:, None] - idx[None, :]
    dec = np.where(rel[None] >= 0, np.exp(log_gamma[:, None, None] * np.maximum(rel, 0.0)[None]), 0.0)
    xi = np.exp(log_gamma[:, None] * (idx[None, :] + 1.0))[:, :, None]
    zeta = np.exp(log_gamma[:, None] * (c - 1.0 - idx[None, :]))[:, :, None]
    cd = np.exp(log_gamma * c)[:, None, None]
    return tuple(jnp.asarray(t.astype(np.float32)) for t in (dec, xi, zeta, cd))


def _rotate_half_cols(w):
    half = w.shape[-1] // 2
    return jnp.concatenate([-w[..., half:], w[..., :half]], axis=-1)


def kernel(x, c, w_ada, b_ada, norm1, norm2, w_in, w_ret_o, q_norm, kv_norm, w_uq, w_ukv, w_mla_o, w_out,
           w_grp, b_grp, w_exp, b_exp, w1, w3, w2, final_norm):
    assert x.shape == (BATCH, SEQ, D_MODEL) and w_ada.shape[0] == 1
    x2 = x.reshape(N_TOK, D_MODEL)

    w_ret, w_lat, w_gate = _wprep(jnp.transpose(w_in[0]))
    wq = w_uq[0].reshape(MLA_Q_LORA, MLA_HEADS, MLA_QK)
    wq = jnp.concatenate([wq, _rotate_half_cols(wq[..., MLA_NOPE:])], axis=-1)
    wq = wq.transpose(1, 0, 2).astype(BF16)
    wkv = w_ukv[0].reshape(MLA_KV_LORA, MLA_HEADS, MLA_NOPE + MLA_V).transpose(1, 0, 2).astype(BF16)
    gap = jnp.zeros((SUBLANES - N_GROUPS, D_MODEL), F32)
    tail = jnp.zeros((ROUTER_ROWS - SUBLANES - N_EXPERTS, D_MODEL), F32)
    w_rt = jnp.concatenate([w_grp[0].T, gap, w_exp[0].T, tail], axis=0)
    b_rt = jnp.concatenate([b_grp[0], gap[:, 0], b_exp[0], tail[:, 0]])
    b_rt = jnp.broadcast_to(b_rt[:, None], (ROUTER_ROWS, LANES))

    ret_cos, ret_sin = (jnp.asarray(t) for t in _rope_tables(RET_DK))
    mla_cos, mla_sin = (jnp.asarray(np.concatenate([t, t], axis=-1)) for t in _rope_tables(MLA_ROPE))
    dec, xi, zeta, cd = _decay_tables()

    mod3 = _ada(c, w_ada[0], b_ada[0]).reshape(BATCH, 6, D_MODEL)
    ret, lat, gates = _inproj(x2, mod3, norm1, ret_cos, ret_sin, w_ret, w_lat, w_gate)
    attn, y_ret = _mla(lat.reshape(BATCH, SEQ, MLA_LAT_W), q_norm, kv_norm, wq, wkv, mla_cos, mla_sin,
                       ret.reshape(BATCH, SEQ, 4 * RET_W), dec, xi, zeta, cd)
    h1, u2t, meta_t, wtok, counts = _merge(y_ret.reshape(N_TOK, D_MODEL), attn.reshape(N_TOK, D_MODEL), gates, x2,
                                           mod3, norm2, w_ret_o[0].astype(BF16), w_mla_o[0].astype(BF16),
                                           w_out[0].astype(BF16), w_rt, b_rt)
    cnt, seg, off_col, tile_expert, n_valid = _slot_layout(counts)
    dst = _plan(meta_t, off_col)
    d0, d1 = dst[0], dst[1]
    xs = _dispatch(d0, d1, seg, cnt, u2t)
    e_shape = (N_EXPERTS, D_MODEL, D_EXPERT)
    ys = _experts(tile_expert, n_valid, xs, w1[0].reshape(e_shape), w3[0].reshape(e_shape),
                  w2[0].reshape(N_EXPERTS, D_EXPERT, D_MODEL))
    out = _final(d0, d1, h1, wtok, mod3, final_norm.reshape(1, D_MODEL), ys)
    return out.reshape(BATCH, SEQ, D_MODEL)
```

```python
import numpy as np
import jax
import jax.numpy as jnp
from jax import lax
from jax.experimental import pallas as pl
from jax.experimental.pallas import tpu as pltpu

D_MODEL = 1024
BATCH = 8
SEQ = 2048
N_TOK = BATCH * SEQ

RET_HEADS = 4
RET_DK = 256
RET_DV = 256
RET_CHUNK = 256
RET_W = RET_HEADS * RET_DK

MLA_HEADS = 8
MLA_NOPE = 128
MLA_ROPE = 64
MLA_V = 128
MLA_Q_LORA = 384
MLA_KV_LORA = 256
MLA_LAT_W = MLA_Q_LORA + MLA_KV_LORA + 2 * MLA_ROPE
MLA_QK = MLA_NOPE + MLA_ROPE
ROPE_THETA = 10000.0

N_GROUPS = 4
EXPERTS_PER_GROUP = 8
N_EXPERTS = N_GROUPS * EXPERTS_PER_GROUP
D_EXPERT = 256
EPS = 1e-6
LOG2_E = 1.4426950408889634

LANES = 128
SUBLANES = 8
ROUTER_ROWS = 48
ROW_TILES = D_MODEL // LANES
VMEM_LIMIT = 56 * 1024 * 1024

TM_PROJ = 512
TM_INPROJ = 1024
TM_FINAL = 1024
TM_DISPATCH = 4096
TQ = 256
MLA_HPS = 2
TE = 512
TE_CHAINS = 2
MERGE_CHAINS = 2
TOP_K = 2
N_TILES = N_TOK * TOP_K // TE + N_EXPERTS
N_SLOTS = N_TILES * TE

F32 = jnp.float32
BF16 = jnp.bfloat16


def _sigmoid(x):
    return 1.0 / (1.0 + jnp.exp(-x))


def _rms(x):
    return x * lax.rsqrt(jnp.mean(x * x, axis=-1, keepdims=True) + EPS)


def _dot(a, b):
    return jnp.dot(a, b, preferred_element_type=F32)


def _dot_nt(a, b):
    return lax.dot_general(a, b, (((1,), (1,)), ((), ())), preferred_element_type=F32)


def _dot_tn(a, b):
    return lax.dot_general(a, b, (((0,), (0,)), ((), ())), preferred_element_type=F32)


def _params(*sem):
    return pltpu.CompilerParams(dimension_semantics=sem, vmem_limit_bytes=VMEM_LIMIT)


def _resident(shape):
    nd = len(shape)
    return pl.BlockSpec(shape, lambda *_: (0,) * nd, pipeline_mode=pl.Buffered(1))


def _ada_kernel(c_ref, w_ref, b_ref, o_ref):
    c = c_ref[...]
    act = (c * _sigmoid(c)).astype(BF16)
    o_ref[...] = _dot(act, w_ref[...].astype(BF16)) + b_ref[...]


def _ada(c, w_ada, b_ada):
    n = w_ada.shape[1]
    tn = D_MODEL
    return pl.pallas_call(
        _ada_kernel,
        grid=(n // tn,),
        in_specs=[pl.BlockSpec((BATCH, D_MODEL), lambda j: (0, 0)),
                  pl.BlockSpec((D_MODEL, tn), lambda j: (0, j)),
                  pl.BlockSpec((1, tn), lambda j: (0, j))],
        out_specs=pl.BlockSpec((BATCH, tn), lambda j: (0, j)),
        out_shape=jax.ShapeDtypeStruct((BATCH, n), F32),
        compiler_params=_params("arbitrary"),
        name="ada",
    )(c, w_ada, b_ada.reshape(1, n))


O_LAT = 4 * RET_W
O_PE = O_LAT + MLA_Q_LORA + MLA_KV_LORA
O_GATE = O_PE + MLA_ROPE


N_IN = O_GATE + 2 * D_MODEL
WPREP_ROWS = 512


def _wprep_kernel(w_hbm, ret_ref, lat_ref, gate_ref, buf, sem):
    chunks = [(s, min(WPREP_ROWS, N_IN - s)) for s in range(0, N_IN, WPREP_ROWS)]

    def copy(i):
        s, n = chunks[i]
        return pltpu.make_async_copy(w_hbm.at[pl.ds(s, n)], buf.at[i % 2, pl.ds(0, n)], sem.at[i % 2])

    groups = [(0, RET_W, ret_ref, 0, 1.0), (RET_W, 2 * RET_W, ret_ref, RET_W, RET_DK ** -0.5),
              (2 * RET_W, O_LAT, ret_ref, 2 * RET_W, 1.0), (O_LAT, O_GATE, lat_ref, 0, 1.0),
              (O_GATE, N_IN, gate_ref, 0, 1.0)]
    half = MLA_ROPE // 2
    rot = [(O_PE + half, O_GATE, O_GATE - O_LAT, -1.0), (O_PE, O_PE + half, O_GATE - O_LAT + half, 1.0)]

    copy(0).start()
    for i, (s, n) in enumerate(chunks):
        if i + 1 < len(chunks):
            copy(i + 1).start()
        copy(i).wait()
        for lo, hi, dst, dst_lo, scale in groups + [(a, b, lat_ref, d, sc) for a, b, d, sc in rot]:
            a, b = max(lo, s), min(hi, s + n)
            if a < b:
                x = buf[i % 2, a - s:b - s, :]
                dst[dst_lo + a - lo:dst_lo + b - lo, :] = (x if scale == 1.0 else x * scale).astype(BF16)


def _wprep(w_in_t):
    out_rows = (O_LAT, MLA_LAT_W, 2 * D_MODEL)
    whole = lambda n: pl.BlockSpec((n, D_MODEL), lambda i: (0, 0))
    return pl.pallas_call(
        _wprep_kernel,
        grid=(1,),
        in_specs=[pl.BlockSpec(memory_space=pl.ANY)],
        out_specs=[whole(n) for n in out_rows],
        out_shape=[jax.ShapeDtypeStruct((n, D_MODEL), BF16) for n in out_rows],
        scratch_shapes=[pltpu.VMEM((2, WPREP_ROWS, D_MODEL), F32), pltpu.SemaphoreType.DMA((2,))],
        compiler_params=_params("arbitrary"),
        name="wprep",
    )(w_in_t)


def _inproj_kernel(x_ref, mod_ref, n1_ref, cos_ref, sin_ref, wr_ref, wm_ref, wg_ref, ret_ref, lat_ref, gate_ref):
    y = _rms(x_ref[...]) * n1_ref[...]
    u = (y * (1.0 + mod_ref[1:2, :]) + mod_ref[0:1, :]).astype(BF16)
    cos, sin = cos_ref[...], sin_ref[...]
    half = RET_DK // 2
    for n in range(0, 2 * RET_W, RET_DK):
        p = _dot_nt(u, wr_ref[n:n + RET_DK, :])
        x1, x2 = p[:, :half], p[:, half:]
        ret_ref[:, n:n + half] = (x1 * cos - x2 * sin).astype(BF16)
        ret_ref[:, n + half:n + RET_DK] = (x2 * cos + x1 * sin).astype(BF16)
    step = 512
    for n in range(2 * RET_W, 3 * RET_W, step):
        ret_ref[:, n:n + step] = _dot_nt(u, wr_ref[n:n + step, :]).astype(BF16)
    for n in range(3 * RET_W, 4 * RET_W, step):
        p = _dot_nt(u, wr_ref[n:n + step, :])
        ret_ref[:, n:n + step] = (p * _sigmoid(p)).astype(BF16)
    lat_ref[...] = _dot_nt(u, wm_ref[...]).astype(BF16)
    for n in range(0, 2 * D_MODEL, step):
        gate_ref[:, n:n + step] = _dot_nt(u, wg_ref[n:n + step, :]).astype(BF16)


def _inproj(x2, mod3, norm1, cos, sin, w_ret, w_lat, w_gate):
    tm = TM_INPROJ
    per_b = SEQ // tm
    rope_tab = pl.BlockSpec((tm, RET_DK // 2), lambda i: (i % per_b, 0))
    return pl.pallas_call(
        _inproj_kernel,
        grid=(N_TOK // tm,),
        in_specs=[pl.BlockSpec((tm, D_MODEL), lambda i: (i, 0)),
                  pl.BlockSpec((None, 6, D_MODEL), lambda i: (i // per_b, 0, 0)),
                  _resident((1, D_MODEL)), rope_tab, rope_tab,
                  _resident(w_ret.shape), _resident(w_lat.shape), _resident(w_gate.shape)],
        out_specs=[pl.BlockSpec((tm, 4 * RET_W), lambda i: (i, 0)),
                   pl.BlockSpec((tm, MLA_LAT_W), lambda i: (i, 0)),
                   pl.BlockSpec((tm, 2 * D_MODEL), lambda i: (i, 0))],
        out_shape=[jax.ShapeDtypeStruct((N_TOK, 4 * RET_W), BF16),
                   jax.ShapeDtypeStruct((N_TOK, MLA_LAT_W), BF16),
                   jax.ShapeDtypeStruct((N_TOK, 2 * D_MODEL), BF16)],
        compiler_params=_params("arbitrary"),
        name="inproj",
    )(x2, mod3, norm1, cos, sin, w_ret, w_lat, w_gate)


def _mla_kernel(lat_ref, qn_ref, kvn_ref, wq_ref, wkv_ref, cos_ref, sin_ref,
                rq_ref, rk_ref, rv_ref, rg_ref, dec_ref, xi_ref, zeta_ref, cd_ref, o_ref, yr_ref,
                cq_s, ckv_s, kpe_s, q_s, k_s, v_s, state_ref):
    h = pl.program_id(1)
    o_q, o_kv, o_pe, o_rot = 0, MLA_Q_LORA, MLA_Q_LORA + MLA_KV_LORA, MLA_Q_LORA + MLA_KV_LORA + MLA_ROPE
    cos, sin = cos_ref[...], sin_ref[...]

    @pl.when(h == 0)
    def _():
        cq_s[...] = (_rms(lat_ref[:, o_q:o_kv].astype(F32)) * qn_ref[...]).astype(BF16)
        ckv_s[...] = (_rms(lat_ref[:, o_kv:o_pe].astype(F32)) * kvn_ref[...]).astype(BF16)
        pe = lat_ref[:, o_pe:o_rot].astype(F32)
        rot = lat_ref[:, o_rot:o_rot + MLA_ROPE].astype(F32)
        kpe_s[...] = (pe * cos + rot * sin).astype(BF16)

    scale = (MLA_QK ** -0.5) * LOG2_E
    for g in range(MLA_HPS):
        qf = _dot(cq_s[...], wq_ref[g])
        q_s[g, :, :MLA_NOPE] = (qf[:, :MLA_NOPE] * scale).astype(BF16)
        q_pe = qf[:, MLA_NOPE:MLA_QK] * cos + qf[:, MLA_QK:] * sin
        q_s[g, :, MLA_NOPE:] = (q_pe * scale).astype(BF16)
        kvf = _dot(ckv_s[...], wkv_ref[g])
        k_s[g, :, :MLA_NOPE] = kvf[:, :MLA_NOPE].astype(BF16)
        k_s[g, :, MLA_NOPE:] = kpe_s[...]
        v_s[g, :, :MLA_V] = kvf[:, MLA_NOPE:].astype(BF16)
        v_s[g, :, MLA_V:] = jnp.ones((SEQ, MLA_V), BF16)

    causal = lax.broadcasted_iota(jnp.int32, (TQ, TQ), 0) >= lax.broadcasted_iota(jnp.int32, (TQ, TQ), 1)
    heads = range(MLA_HPS)
    n_blk = SEQ // TQ

    def scores(i):
        lo, hi = i * TQ, (i + 1) * TQ
        diag = [jnp.where(causal, _dot_nt(q_s[g, lo:hi, :], k_s[g, lo:hi, :]), -jnp.inf) for g in heads]
        past = [_dot_nt(q_s[g, lo:hi, :], k_s[g, :lo, :]) if i > 0 else None for g in heads]
        return diag, past

    state_ref[...] = jnp.zeros_like(state_ref)
    pending = scores(0)
    for i in range(n_blk):
        lo, hi = i * TQ, (i + 1) * TQ
        diag, past = pending
        if i + 1 < n_blk:
            pending = scores(i + 1)
        rq, rk, rv = rq_ref[lo:hi, :], rk_ref[lo:hi, :], rv_ref[lo:hi, :]
        r_scores = (_dot_nt(rq, rk) * dec_ref[h]).astype(BF16)
        carried = xi_ref[h] * _dot(rq, state_ref[...].astype(BF16))
        m = [jnp.max(diag[g], axis=-1, keepdims=True) for g in heads]
        if i > 0:
            m = [jnp.maximum(m[g], jnp.max(past[g], axis=-1, keepdims=True)) for g in heads]
        ry = _dot(r_scores, rv) + carried
        k_dec = (rk.astype(F32) * zeta_ref[h]).astype(BF16)
        state_ref[...] = state_ref[...] * cd_ref[h] + _dot_tn(k_dec, rv)
        acc = [_dot(jnp.exp2(diag[g] - m[g]).astype(BF16), v_s[g, lo:hi, :]) for g in heads]
        if i > 0:
            acc = [acc[g] + _dot(jnp.exp2(past[g] - m[g]).astype(BF16), v_s[g, :lo, :]) for g in heads]
        ryc = ry - jnp.mean(ry, axis=-1, keepdims=True)
        r_inv = lax.rsqrt(jnp.mean(ryc * ryc, axis=-1, keepdims=True) + EPS)
        yr_ref[lo:hi, :] = (rg_ref[lo:hi, :].astype(F32) * (ryc * r_inv)).astype(BF16)
        for g in heads:
            o_ref[lo:hi, g * MLA_V:(g + 1) * MLA_V] = (acc[g][:, :MLA_V] / acc[g][:, MLA_V:]).astype(BF16)


def _mla(lat3, q_norm, kv_norm, wq, wkv, cos, sin, ret3, dec, xi, zeta, cd):
    hps = MLA_HPS
    assert MLA_HEADS // hps == RET_HEADS and TQ == RET_CHUNK
    ret_part = lambda part: pl.BlockSpec((None, SEQ, RET_DK), lambda b, h: (b, 0, part * RET_HEADS + h))
    whole = lambda a: pl.BlockSpec(a.shape, lambda b, h: (0,) * a.ndim)
    return pl.pallas_call(
        _mla_kernel,
        grid=(BATCH, MLA_HEADS // hps),
        in_specs=[pl.BlockSpec((None, SEQ, MLA_LAT_W), lambda b, h: (b, 0, 0)),
                  pl.BlockSpec((1, MLA_Q_LORA), lambda b, h: (0, 0)),
                  pl.BlockSpec((1, MLA_KV_LORA), lambda b, h: (0, 0)),
                  pl.BlockSpec((hps, MLA_Q_LORA, MLA_QK + MLA_ROPE), lambda b, h: (h, 0, 0)),
                  pl.BlockSpec((hps, MLA_KV_LORA, MLA_NOPE + MLA_V), lambda b, h: (h, 0, 0)),
                  pl.BlockSpec((SEQ, MLA_ROPE), lambda b, h: (0, 0)),
                  pl.BlockSpec((SEQ, MLA_ROPE), lambda b, h: (0, 0)),
                  ret_part(0), ret_part(1), ret_part(2), ret_part(3),
                  whole(dec), whole(xi), whole(zeta), whole(cd)],
        out_specs=[pl.BlockSpec((None, SEQ, hps * MLA_V), lambda b, h: (b, 0, h)),
                   pl.BlockSpec((None, SEQ, RET_DV), lambda b, h: (b, 0, h))],
        out_shape=[jax.ShapeDtypeStruct((BATCH, SEQ, MLA_HEADS * MLA_V), BF16),
                   jax.ShapeDtypeStruct((BATCH, SEQ, RET_HEADS * RET_DV), BF16)],
        scratch_shapes=[pltpu.VMEM((SEQ, MLA_Q_LORA), BF16),
                        pltpu.VMEM((SEQ, MLA_KV_LORA), BF16),
                        pltpu.VMEM((SEQ, MLA_ROPE), BF16),
                        pltpu.VMEM((hps, SEQ, MLA_QK), BF16),
                        pltpu.VMEM((hps, SEQ, MLA_QK), BF16),
                        pltpu.VMEM((hps, SEQ, 2 * MLA_V), BF16),
                        pltpu.VMEM((RET_DK, RET_DV), F32)],
        compiler_params=_params("arbitrary", "arbitrary"),
        name="mla",
    )(lat3, q_norm, kv_norm, wq, wkv, cos, sin, ret3, ret3, ret3, ret3, dec, xi, zeta, cd)


def _route(logits_t):
    tm = logits_t.shape[1]
    row = lax.broadcasted_iota(jnp.int32, (SUBLANES, tm), 0)
    neg = -jnp.inf
    gl = jnp.where(row < N_GROUPS, logits_t[:SUBLANES], neg)
    gmax = jnp.max(gl, axis=0, keepdims=True)
    gsel = jnp.min(jnp.where(gl == gmax, row, SUBLANES), axis=0, keepdims=True)
    p_grp = 1.0 / jnp.sum(jnp.exp(gl - gmax), axis=0, keepdims=True)
    el = logits_t[SUBLANES * N_GROUPS:SUBLANES * (N_GROUPS + 1)]
    for g in reversed(range(N_GROUPS - 1)):
        el = jnp.where(gsel == g, logits_t[SUBLANES * (g + 1):SUBLANES * (g + 2)], el)
    v0 = jnp.max(el, axis=0, keepdims=True)
    i0 = jnp.min(jnp.where(el == v0, row, SUBLANES), axis=0, keepdims=True)
    el1 = jnp.where(row == i0, neg, el)
    v1 = jnp.max(el1, axis=0, keepdims=True)
    i1 = jnp.min(jnp.where(el1 == v1, row, SUBLANES), axis=0, keepdims=True)
    t = jnp.exp(v1 - v0)
    w0 = p_grp / (1.0 + t)
    w1 = p_grp * t / (1.0 + t)
    return gsel * EXPERTS_PER_GROUP + i0, gsel * EXPERTS_PER_GROUP + i1, w0, w1


def _stack_rows(rows, n):
    tm = rows[0].shape[1]
    row = lax.broadcasted_iota(jnp.int32, (n, tm), 0)
    out = jnp.zeros((n, tm), F32)
    for k, r in enumerate(rows):
        out = jnp.where(row == k, r, out)
    return out


def _to_token_tiles(ref, val):
    n = val.shape[0]
    for s in range(ROW_TILES):
        ref[pl.ds(s, n, stride=ROW_TILES), :] = val[:, s * LANES:(s + 1) * LANES]


def _from_token_tiles(ref):
    n = ref.shape[0] // ROW_TILES
    return jnp.concatenate([ref[pl.ds(s, n, stride=ROW_TILES), :] for s in range(ROW_TILES)], axis=-1)


def _token_rows(ref, t):
    return ref.at[pl.ds(pl.multiple_of(t * ROW_TILES, ROW_TILES), ROW_TILES)]


def _merge_kernel(yr_ref, at_ref, gr_ref, gm_ref, x_ref, mod_ref, n2_ref, wro_ref, wmo_ref, wo_ref,
                  wrt_ref, brt_ref, h1_ref, u2_ref, meta_ref, wtok_ref, cnt_ref, carry_ref):
    tm = x_ref.shape[0]

    @pl.when(pl.program_id(0) == 0)
    def _():
        carry_ref[...] = jnp.zeros_like(carry_ref)

    chunks = range(MERGE_CHAINS)
    sub = tm // MERGE_CHAINS
    rows = [slice(c * sub, (c + 1) * sub) for c in chunks]
    y_ret = [_dot(yr_ref[r, :], wro_ref[...]) for r in rows]
    y_mla = [_dot(at_ref[r, :], wmo_ref[...]) for r in rows]
    merged = [(_sigmoid(gr_ref[rows[c], :].astype(F32)) * y_ret[c]
               + _sigmoid(gm_ref[rows[c], :].astype(F32)) * y_mla[c]).astype(BF16) for c in chunks]
    o = [_dot(merged[c], wo_ref[...]) for c in chunks]
    h1 = [x_ref[rows[c], :] + mod_ref[2:3, :] * o[c] for c in chunks]
    for c in chunks:
        h1_ref[rows[c], :] = h1[c]
    u2 = [_rms(h1[c]) * n2_ref[...] * (1.0 + mod_ref[4:5, :]) + mod_ref[3:4, :] for c in chunks]
    for c in chunks:
        _to_token_tiles(u2_ref.at[pl.ds(c * sub * ROW_TILES, sub * ROW_TILES)], u2[c])
    w = wrt_ref[...]
    w_hi = w.astype(BF16)
    w_lo = (w - w_hi.astype(F32)).astype(BF16)
    w_both = jnp.concatenate([w_hi, w_lo], axis=0)
    u_hi = [u2[c].astype(BF16) for c in chunks]
    u_lo = [(u2[c] - u_hi[c].astype(F32)).astype(BF16) for c in chunks]
    by_hi = [_dot_nt(w_both, u_hi[c]) for c in chunks]
    logits_t = [by_hi[c][:ROUTER_ROWS] + by_hi[c][ROUTER_ROWS:] + _dot_nt(w_hi, u_lo[c]) + brt_ref[:, 0:1]
                for c in chunks]
    routed = [_route(logits_t[c]) for c in chunks]
    e0, e1, w0, w1 = [jnp.concatenate([routed[c][k] for c in chunks], axis=1) for k in range(4)]
    erow = lax.broadcasted_iota(jnp.int32, (N_EXPERTS, tm), 0)
    m0, m1 = erow == e0, erow == e1
    member = jnp.where(m0 | m1, 1.0, 0.0)
    earlier = jnp.where(lax.broadcasted_iota(jnp.int32, (tm, tm), 0) < lax.broadcasted_iota(jnp.int32, (tm, tm), 1),
                        1.0, 0.0).astype(BF16)
    prefix = _dot(member.astype(BF16), earlier) + carry_ref[:, 0:1]
    rank0 = jnp.sum(jnp.where(m0, prefix, 0.0), axis=0, keepdims=True)
    rank1 = jnp.sum(jnp.where(m1, prefix, 0.0), axis=0, keepdims=True)
    carry_ref[...] = carry_ref[...] + jnp.sum(member, axis=1, keepdims=True)
    cnt_ref[...] = carry_ref[...]
    meta_ref[...] = _stack_rows([e0.astype(F32), e1.astype(F32), rank0, rank1], SUBLANES)
    wt = _stack_rows([w0, w1], 2 * SUBLANES)
    wt_hi = wt.astype(BF16)
    wt_lo = (wt - wt_hi.astype(F32)).astype(BF16)
    place = jnp.where(lax.broadcasted_iota(jnp.int32, (2 * SUBLANES, LANES), 0)
                      == lax.broadcasted_iota(jnp.int32, (2 * SUBLANES, LANES), 1), 1.0, 0.0).astype(BF16)
    wtok_ref[...] = _dot_tn(wt_hi, place) + _dot_tn(wt_lo, place)


def _merge(y_ret, attn, gates, x2, mod3, norm2, w_ret_o, w_mla_o, w_out, w_rt, b_rt):
    tm = TM_PROJ
    per_b = SEQ // tm
    row = lambda j: pl.BlockSpec((tm, D_MODEL), lambda i: (i, j))
    return pl.pallas_call(
        _merge_kernel,
        grid=(N_TOK // tm,),
        in_specs=[row(0), row(0), row(0), row(1), row(0),
                  pl.BlockSpec((None, 6, D_MODEL), lambda i: (i // per_b, 0, 0)),
                  _resident((1, D_MODEL)),
                  _resident(w_ret_o.shape), _resident(w_mla_o.shape), _resident(w_out.shape),
                  _resident(w_rt.shape), _resident(b_rt.shape)],
        out_specs=[row(0),
                   pl.BlockSpec((tm * ROW_TILES, LANES), lambda i: (i, 0)),
                   pl.BlockSpec((SUBLANES, tm), lambda i: (0, i)),
                   pl.BlockSpec((tm, LANES), lambda i: (i, 0)),
                   pl.BlockSpec((N_EXPERTS, LANES), lambda i: (0, 0))],
        out_shape=[jax.ShapeDtypeStruct((N_TOK, D_MODEL), F32),
                   jax.ShapeDtypeStruct((N_TOK * ROW_TILES, LANES), F32),
                   jax.ShapeDtypeStruct((SUBLANES, N_TOK), F32),
                   jax.ShapeDtypeStruct((N_TOK, LANES), F32),
                   jax.ShapeDtypeStruct((N_EXPERTS, LANES), F32)],
        scratch_shapes=[pltpu.VMEM((N_EXPERTS, LANES), F32)],
        compiler_params=_params("arbitrary"),
        name="merge",
    )(y_ret, attn, gates, gates, x2, mod3, norm2, w_ret_o, w_mla_o, w_out, w_rt, b_rt)


def _plan_kernel(meta_ref, off_ref, dst_ref):
    m = meta_ref[...]
    erow = lax.broadcasted_iota(jnp.int32, (N_EXPERTS, m.shape[1]), 0)
    off = off_ref[:, 0:1]
    d = [jnp.sum(jnp.where(erow == m[k:k + 1].astype(jnp.int32), off, 0.0), axis=0, keepdims=True) + m[k + 2:k + 3]
         for k in range(TOP_K)]
    dst_ref[...] = _stack_rows(d, SUBLANES).astype(jnp.int32)


def _plan(meta_t, off_col):
    tm = 2048
    return pl.pallas_call(
        _plan_kernel,
        grid=(N_TOK // tm,),
        in_specs=[pl.BlockSpec((SUBLANES, tm), lambda i: (0, i)),
                  pl.BlockSpec((N_EXPERTS, LANES), lambda i: (0, 0))],
        out_specs=pl.BlockSpec((SUBLANES, tm), lambda i: (0, i)),
        out_shape=jax.ShapeDtypeStruct((SUBLANES, N_TOK), jnp.int32),
        compiler_params=_params("arbitrary"),
        name="plan",
    )(meta_t, off_col)


def _row_copy_wait(src_like, dst_like, sem):
    pltpu.make_async_copy(src_like, dst_like, sem).wait()


def _dispatch_kernel(d0_ref, d1_ref, seg_ref, cnt_ref, u_ref, xs_ref, zero_ref, sem, zsem):
    i = pl.program_id(0)
    tm = u_ref.shape[0] // ROW_TILES
    tile_rows = TE * ROW_TILES

    def slot_tile(j):
        return xs_ref.at[pl.ds(pl.multiple_of(j * tile_rows, tile_rows), tile_rows)]

    @pl.when(i == 0)
    def _():
        zero_ref[...] = jnp.zeros_like(zero_ref)

        def pad(e):
            first = seg_ref[e] + cnt_ref[e]
            rows = (seg_ref[e + 1] - first) * ROW_TILES
            start = pl.multiple_of(first * ROW_TILES, ROW_TILES)
            return rows > 0, pltpu.make_async_copy(zero_ref.at[pl.ds(0, rows)], xs_ref.at[pl.ds(start, rows)], zsem)

        for e in range(N_EXPERTS):
            nonempty, copy = pad(e)
            pl.when(nonempty)(copy.start)
        for e in range(N_EXPERTS):
            nonempty, copy = pad(e)
            pl.when(nonempty)(copy.wait)

        def unused(j):
            return pltpu.make_async_copy(zero_ref, slot_tile(j), zsem)

        first_unused = seg_ref[N_EXPERTS] // TE
        lax.fori_loop(first_unused, N_TILES, lambda j, c: (unused(j).start(), c)[1], 0)
        lax.fori_loop(first_unused, N_TILES, lambda j, c: (unused(j).wait(), c)[1], 0)

    base = i * tm

    def body(r, carry):
        src = _token_rows(u_ref, r)
        pltpu.make_async_copy(src, _token_rows(xs_ref, d0_ref[base + r]), sem).start(priority=0)
        pltpu.make_async_copy(src, _token_rows(xs_ref, d1_ref[base + r]), sem).start(priority=1)
        return carry

    lax.fori_loop(0, tm, body, 0, unroll=8)
    _row_copy_wait(u_ref, xs_ref.at[pl.ds(0, tm * ROW_TILES)], sem)
    _row_copy_wait(u_ref, xs_ref.at[pl.ds(0, tm * ROW_TILES)], sem)


def _dispatch(d0, d1, seg, cnt, u2t):
    tm = TM_DISPATCH
    return pl.pallas_call(
        _dispatch_kernel,
        grid_spec=pltpu.PrefetchScalarGridSpec(
            num_scalar_prefetch=4,
            grid=(N_TOK // tm,),
            in_specs=[pl.BlockSpec((tm * ROW_TILES, LANES), lambda i, *_: (i, 0))],
            out_specs=pl.BlockSpec(memory_space=pl.ANY),
            scratch_shapes=[pltpu.VMEM((TE * ROW_TILES, LANES), F32),
                            pltpu.SemaphoreType.DMA(()), pltpu.SemaphoreType.DMA(())]),
        out_shape=jax.ShapeDtypeStruct((N_SLOTS * ROW_TILES, LANES), F32),
        compiler_params=_params("arbitrary"),
        name="dispatch",
    )(d0, d1, seg, cnt, u2t)


def _expert_kernel(te_ref, nv_ref, x_ref, w1_ref, w3_ref, w2_ref, y_ref, w1_s, w3_s, w2_s):
    j = pl.program_id(0)

    @pl.when(j < nv_ref[0])
    def _():
        @pl.when((j == 0) | (te_ref[j] != te_ref[jnp.maximum(j - 1, 0)]))
        def _():
            w1_s[...] = w1_ref[...].astype(BF16)
            w3_s[...] = w3_ref[...].astype(BF16)
            w2_s[...] = w2_ref[...].astype(BF16)

        sub = TE // TE_CHAINS * ROW_TILES
        part = lambda ref, c: ref.at[pl.ds(c * sub, sub)]
        chains = range(TE_CHAINS)
        x = [_from_token_tiles(part(x_ref, c)).astype(BF16) for c in chains]
        a = [_dot(x[c], w1_s[...]) for c in chains]
        b = [_dot(x[c], w3_s[...]) for c in chains]
        hid = [(a[c] * _sigmoid(a[c]) * b[c]).astype(BF16) for c in chains]
        for c in chains:
            _to_token_tiles(part(y_ref, c), _dot(hid[c], w2_s[...]))


def _experts(tile_expert, n_valid, xs, w1, w3, w2):
    tile = lambda j, te, nv: jnp.minimum(j, nv[0] - 1)
    wspec = lambda shape: pl.BlockSpec((None,) + shape, lambda j, te, nv: (te[tile(j, te, nv)], 0, 0))
    slots = pl.BlockSpec((TE * ROW_TILES, LANES), lambda j, te, nv: (tile(j, te, nv), 0))
    return pl.pallas_call(
        _expert_kernel,
        grid_spec=pltpu.PrefetchScalarGridSpec(
            num_scalar_prefetch=2,
            grid=(N_TILES,),
            in_specs=[slots, wspec((D_MODEL, D_EXPERT)), wspec((D_MODEL, D_EXPERT)), wspec((D_EXPERT, D_MODEL))],
            out_specs=slots,
            scratch_shapes=[pltpu.VMEM((D_MODEL, D_EXPERT), BF16), pltpu.VMEM((D_MODEL, D_EXPERT), BF16),
                            pltpu.VMEM((D_EXPERT, D_MODEL), BF16)]),
        out_shape=jax.ShapeDtypeStruct((N_SLOTS * ROW_TILES, LANES), F32),
        input_output_aliases={2: 0},
        compiler_params=_params("arbitrary"),
        name="experts",
    )(tile_expert, n_valid, xs, w1, w3, w2)


def _final_kernel(d0_ref, d1_ref, h1_ref, meta_ref, mod_ref, fn_ref, ys_ref, o_ref, ybuf, sem):
    i = pl.program_id(0)
    tm = h1_ref.shape[0]

    def gather(t):
        buf, s = ybuf.at[t % 2], sem.at[t % 2]
        base = t * tm

        def body(r, carry):
            pltpu.make_async_copy(_token_rows(ys_ref, d0_ref[base + r]), _token_rows(buf.at[0], r), s).start(priority=0)
            pltpu.make_async_copy(_token_rows(ys_ref, d1_ref[base + r]), _token_rows(buf.at[1], r), s).start(priority=1)
            return carry

        lax.fori_loop(0, tm, body, 0, unroll=8)

    @pl.when(i == 0)
    def _():
        gather(0)

    @pl.when(i + 1 < pl.num_programs(0))
    def _():
        gather(i + 1)

    buf = ybuf.at[i % 2]
    for k in range(TOP_K):
        _row_copy_wait(ys_ref.at[pl.ds(0, tm * ROW_TILES)], buf.at[k], sem.at[i % 2])
    m = meta_ref[...]
    moe = m[:, 0:1] * _from_token_tiles(buf.at[0]) + m[:, 1:2] * _from_token_tiles(buf.at[1])
    h2 = h1_ref[...] + mod_ref[5:6, :] * moe
    o_ref[...] = _rms(h2) * fn_ref[...]


def _final(d0, d1, h1, meta, mod3, final_norm, ys):
    tm = TM_FINAL
    per_b = SEQ // tm
    return pl.pallas_call(
        _final_kernel,
        grid_spec=pltpu.PrefetchScalarGridSpec(
            num_scalar_prefetch=2,
            grid=(N_TOK // tm,),
            in_specs=[pl.BlockSpec((tm, D_MODEL), lambda i, *_: (i, 0)),
                      pl.BlockSpec((tm, LANES), lambda i, *_: (i, 0)),
                      pl.BlockSpec((None, 6, D_MODEL), lambda i, *_: (i // per_b, 0, 0)),
                      pl.BlockSpec((1, D_MODEL), lambda i, *_: (0, 0)),
                      pl.BlockSpec(memory_space=pl.ANY)],
            out_specs=pl.BlockSpec((tm, D_MODEL), lambda i, *_: (i, 0)),
            scratch_shapes=[pltpu.VMEM((2, TOP_K, tm * ROW_TILES, LANES), F32), pltpu.SemaphoreType.DMA((2,))]),
        out_shape=jax.ShapeDtypeStruct((N_TOK, D_MODEL), F32),
        compiler_params=_params("arbitrary"),
        name="final",
    )(d0, d1, h1, meta, mod3, final_norm, ys)


def _slot_layout(counts):
    cnt = counts[:, 0].astype(jnp.int32)
    tile_end = jnp.cumsum((cnt + TE - 1) // TE)
    seg = jnp.concatenate([jnp.zeros((1,), jnp.int32), tile_end * TE])
    off_col = jnp.broadcast_to(seg[:-1].astype(F32)[:, None], (N_EXPERTS, LANES))
    tile_ids = jnp.arange(N_TILES, dtype=jnp.int32)
    tile_expert = jnp.sum((tile_end[None, :] <= tile_ids[:, None]).astype(jnp.int32), axis=1)
    tile_expert = jnp.minimum(tile_expert, N_EXPERTS - 1)
    return cnt, seg, off_col, tile_expert, tile_end[-1:]


def _rope_tables(dim):
    pos = np.arange(SEQ, dtype=np.float64)
    inv = ROPE_THETA ** (-np.arange(0, dim, 2, dtype=np.float64) / dim)
    ang = pos[:, None] * inv[None, :]
    return np.cos(ang).astype(np.float32), np.sin(ang).astype(np.float32)


def _decay_tables():
    c = RET_CHUNK
    log_gamma = np.log1p(-np.exp2(-5.0 - np.arange(RET_HEADS, dtype=np.float64)))
    idx = np.arange(c, dtype=np.float64)
    rel = idx[:, None] - idx[None, :]
    dec = np.where(rel[None] >= 0, np.exp(log_gamma[:, None, None] * np.maximum(rel, 0.0)[None]), 0.0)
    xi = np.exp(log_gamma[:, None] * (idx[None, :] + 1.0))[:, :, None]
    zeta = np.exp(log_gamma[:, None] * (c - 1.0 - idx[None, :]))[:, :, None]
    cd = np.exp(log_gamma * c)[:, None, None]
    return tuple(jnp.asarray(t.astype(np.float32)) for t in (dec, xi, zeta, cd))


def _rotate_half_cols(w):
    half = w.shape[-1] // 2
    return jnp.concatenate([-w[..., half:], w[..., :half]], axis=-1)


def kernel(x, c, w_ada, b_ada, norm1, norm2, w_in, w_ret_o, q_norm, kv_norm, w_uq, w_ukv, w_mla_o, w_out,
           w_grp, b_grp, w_exp, b_exp, w1, w3, w2, final_norm):
    assert x.shape == (BATCH, SEQ, D_MODEL) and w_ada.shape[0] == 1
    x2 = x.reshape(N_TOK, D_MODEL)

    w_ret, w_lat, w_gate = _wprep(jnp.transpose(w_in[0]))
    wq = w_uq[0].reshape(MLA_Q_LORA, MLA_HEADS, MLA_QK)
    wq = jnp.concatenate([wq, _rotate_half_cols(wq[..., MLA_NOPE:])], axis=-1)
    wq = wq.transpose(1, 0, 2).astype(BF16)
    wkv = w_ukv[0].reshape(MLA_KV_LORA, MLA_HEADS, MLA_NOPE + MLA_V).transpose(1, 0, 2).astype(BF16)
    gap = jnp.zeros((SUBLANES - N_GROUPS, D_MODEL), F32)
    tail = jnp.zeros((ROUTER_ROWS - SUBLANES - N_EXPERTS, D_MODEL), F32)
    w_rt = jnp.concatenate([w_grp[0].T, gap, w_exp[0].T, tail], axis=0)
    b_rt = jnp.concatenate([b_grp[0], gap[:, 0], b_exp[0], tail[:, 0]])
    b_rt = jnp.broadcast_to(b_rt[:, None], (ROUTER_ROWS, LANES))

    ret_cos, ret_sin = (jnp.asarray(t) for t in _rope_tables(RET_DK))
    mla_cos, mla_sin = (jnp.asarray(np.concatenate([t, t], axis=-1)) for t in _rope_tables(MLA_ROPE))
    dec, xi, zeta, cd = _decay_tables()

    mod3 = _ada(c, w_ada[0], b_ada[0]).reshape(BATCH, 6, D_MODEL)
    ret, lat, gates = _inproj(x2, mod3, norm1, ret_cos, ret_sin, w_ret, w_lat, w_gate)
    attn, y_ret = _mla(lat.reshape(BATCH, SEQ, MLA_LAT_W), q_norm, kv_norm, wq, wkv, mla_cos, mla_sin,
                       ret.reshape(BATCH, SEQ, 4 * RET_W), dec, xi, zeta, cd)
    h1, u2t, meta_t, wtok, counts = _merge(y_ret.reshape(N_TOK, D_MODEL), attn.reshape(N_TOK, D_MODEL), gates, x2,
                                           mod3, norm2, w_ret_o[0].astype(BF16), w_mla_o[0].astype(BF16),
                                           w_out[0].astype(BF16), w_rt, b_rt)
    cnt, seg, off_col, tile_expert, n_valid = _slot_layout(counts)
    dst = _plan(meta_t, off_col)
    d0, d1 = dst[0], dst[1]
    xs = _dispatch(d0, d1, seg, cnt, u2t)
    e_shape = (N_EXPERTS, D_MODEL, D_EXPERT)
    ys = _experts(tile_expert, n_valid, xs, w1[0].reshape(e_shape), w3[0].reshape(e_shape),
                  w2[0].reshape(N_EXPERTS, D_EXPERT, D_MODEL))
    out = _final(d0, d1, h1, wtok, mod3, final_norm.reshape(1, D_MODEL), ys)
    return out.reshape(BATCH, SEQ, D_MODEL)
```

```python
import numpy as np
import jax
import jax.numpy as jnp
from jax import lax
from jax.experimental import pallas as pl
from jax.experimental.pallas import tpu as pltpu

D_MODEL = 1024
BATCH = 8
SEQ = 2048
N_TOK = BATCH * SEQ

RET_HEADS = 4
RET_DK = 256
RET_DV = 256
RET_CHUNK = 256
RET_W = RET_HEADS * RET_DK

MLA_HEADS = 8
MLA_NOPE = 128
MLA_ROPE = 64
MLA_V = 128
MLA_Q_LORA = 384
MLA_KV_LORA = 256
MLA_LAT_W = MLA_Q_LORA + MLA_KV_LORA + 2 * MLA_ROPE
MLA_QK = MLA_NOPE + MLA_ROPE
ROPE_THETA = 10000.0

N_GROUPS = 4
EXPERTS_PER_GROUP = 8
N_EXPERTS = N_GROUPS * EXPERTS_PER_GROUP
D_EXPERT = 256
EPS = 1e-6
LOG2_E = 1.4426950408889634

LANES = 128
SUBLANES = 8
ROUTER_ROWS = 48
ROW_TILES = D_MODEL // LANES
VMEM_LIMIT = 56 * 1024 * 1024

TM_PROJ = 512
TM_INPROJ = 1024
TM_FINAL = 512
TM_DISPATCH = 4096
TQ = 256
MLA_HPS = 2
TE = 512
TE_CHAINS = 2
MERGE_CHAINS = 2
TOP_K = 2
N_TILES = N_TOK * TOP_K // TE + N_EXPERTS
N_SLOTS = N_TILES * TE

F32 = jnp.float32
BF16 = jnp.bfloat16


def _sigmoid(x):
    return 1.0 / (1.0 + jnp.exp(-x))


def _rms(x):
    return x * lax.rsqrt(jnp.mean(x * x, axis=-1, keepdims=True) + EPS)


def _dot(a, b):
    return jnp.dot(a, b, preferred_element_type=F32)


def _dot_nt(a, b):
    return lax.dot_general(a, b, (((1,), (1,)), ((), ())), preferred_element_type=F32)


def _dot_tn(a, b):
    return lax.dot_general(a, b, (((0,), (0,)), ((), ())), preferred_element_type=F32)


def _params(*sem):
    return pltpu.CompilerParams(dimension_semantics=sem, vmem_limit_bytes=VMEM_LIMIT)


def _resident(shape):
    nd = len(shape)
    return pl.BlockSpec(shape, lambda *_: (0,) * nd, pipeline_mode=pl.Buffered(1))


def _ada_kernel(c_ref, w_ref, b_ref, o_ref):
    c = c_ref[...]
    act = (c * _sigmoid(c)).astype(BF16)
    o_ref[...] = _dot(act, w_ref[...].astype(BF16)) + b_ref[...]


def _ada(c, w_ada, b_ada):
    n = w_ada.shape[1]
    tn = D_MODEL
    return pl.pallas_call(
        _ada_kernel,
        grid=(n // tn,),
        in_specs=[pl.BlockSpec((BATCH, D_MODEL), lambda j: (0, 0)),
                  pl.BlockSpec((D_MODEL, tn), lambda j: (0, j)),
                  pl.BlockSpec((1, tn), lambda j: (0, j))],
        out_specs=pl.BlockSpec((BATCH, tn), lambda j: (0, j)),
        out_shape=jax.ShapeDtypeStruct((BATCH, n), F32),
        compiler_params=_params("arbitrary"),
        name="ada",
    )(c, w_ada, b_ada.reshape(1, n))


O_LAT = 4 * RET_W
O_PE = O_LAT + MLA_Q_LORA + MLA_KV_LORA
O_GATE = O_PE + MLA_ROPE


N_IN = O_GATE + 2 * D_MODEL
WPREP_ROWS = 512


def _wprep_kernel(w_hbm, ret_ref, lat_ref, gate_ref, buf, sem):
    chunks = [(s, min(WPREP_ROWS, N_IN - s)) for s in range(0, N_IN, WPREP_ROWS)]

    def copy(i):
        s, n = chunks[i]
        return pltpu.make_async_copy(w_hbm.at[pl.ds(s, n)], buf.at[i % 2, pl.ds(0, n)], sem.at[i % 2])

    groups = [(0, RET_W, ret_ref, 0, 1.0), (RET_W, 2 * RET_W, ret_ref, RET_W, RET_DK ** -0.5),
              (2 * RET_W, O_LAT, ret_ref, 2 * RET_W, 1.0), (O_LAT, O_GATE, lat_ref, 0, 1.0),
              (O_GATE, N_IN, gate_ref, 0, 1.0)]
    half = MLA_ROPE // 2
    rot = [(O_PE + half, O_GATE, O_GATE - O_LAT, -1.0), (O_PE, O_PE + half, O_GATE - O_LAT + half, 1.0)]

    copy(0).start()
    for i, (s, n) in enumerate(chunks):
        if i + 1 < len(chunks):
            copy(i + 1).start()
        copy(i).wait()
        for lo, hi, dst, dst_lo, scale in groups + [(a, b, lat_ref, d, sc) for a, b, d, sc in rot]:
            a, b = max(lo, s), min(hi, s + n)
            if a < b:
                x = buf[i % 2, a - s:b - s, :]
                dst[dst_lo + a - lo:dst_lo + b - lo, :] = (x if scale == 1.0 else x * scale).astype(BF16)


def _wprep(w_in_t):
    out_rows = (O_LAT, MLA_LAT_W, 2 * D_MODEL)
    whole = lambda n: pl.BlockSpec((n, D_MODEL), lambda i: (0, 0))
    return pl.pallas_call(
        _wprep_kernel,
        grid=(1,),
        in_specs=[pl.BlockSpec(memory_space=pl.ANY)],
        out_specs=[whole(n) for n in out_rows],
        out_shape=[jax.ShapeDtypeStruct((n, D_MODEL), BF16) for n in out_rows],
        scratch_shapes=[pltpu.VMEM((2, WPREP_ROWS, D_MODEL), F32), pltpu.SemaphoreType.DMA((2,))],
        compiler_params=_params("arbitrary"),
        name="wprep",
    )(w_in_t)


def _inproj_kernel(x_ref, mod_ref, n1_ref, cos_ref, sin_ref, wr_ref, wm_ref, wg_ref, ret_ref, lat_ref, gate_ref):
    y = _rms(x_ref[...]) * n1_ref[...]
    u = (y * (1.0 + mod_ref[1:2, :]) + mod_ref[0:1, :]).astype(BF16)
    cos, sin = cos_ref[...], sin_ref[...]
    half = RET_DK // 2
    for n in range(0, 2 * RET_W, RET_DK):
        p = _dot_nt(u, wr_ref[n:n + RET_DK, :])
        x1, x2 = p[:, :half], p[:, half:]
        ret_ref[:, n:n + half] = (x1 * cos - x2 * sin).astype(BF16)
        ret_ref[:, n + half:n + RET_DK] = (x2 * cos + x1 * sin).astype(BF16)
    step = 512
    for n in range(2 * RET_W, 3 * RET_W, step):
        ret_ref[:, n:n + step] = _dot_nt(u, wr_ref[n:n + step, :]).astype(BF16)
    for n in range(3 * RET_W, 4 * RET_W, step):
        p = _dot_nt(u, wr_ref[n:n + step, :])
        ret_ref[:, n:n + step] = (p * _sigmoid(p)).astype(BF16)
    lat_ref[...] = _dot_nt(u, wm_ref[...]).astype(BF16)
    for n in range(0, 2 * D_MODEL, step):
        gate_ref[:, n:n + step] = _dot_nt(u, wg_ref[n:n + step, :]).astype(BF16)


def _inproj(x2, mod3, norm1, cos, sin, w_ret, w_lat, w_gate):
    tm = TM_INPROJ
    per_b = SEQ // tm
    rope_tab = pl.BlockSpec((tm, RET_DK // 2), lambda i: (i % per_b, 0))
    return pl.pallas_call(
        _inproj_kernel,
        grid=(N_TOK // tm,),
        in_specs=[pl.BlockSpec((tm, D_MODEL), lambda i: (i, 0)),
                  pl.BlockSpec((None, 6, D_MODEL), lambda i: (i // per_b, 0, 0)),
                  _resident((1, D_MODEL)), rope_tab, rope_tab,
                  _resident(w_ret.shape), _resident(w_lat.shape), _resident(w_gate.shape)],
        out_specs=[pl.BlockSpec((tm, 4 * RET_W), lambda i: (i, 0)),
                   pl.BlockSpec((tm, MLA_LAT_W), lambda i: (i, 0)),
                   pl.BlockSpec((tm, 2 * D_MODEL), lambda i: (i, 0))],
        out_shape=[jax.ShapeDtypeStruct((N_TOK, 4 * RET_W), BF16),
                   jax.ShapeDtypeStruct((N_TOK, MLA_LAT_W), BF16),
                   jax.ShapeDtypeStruct((N_TOK, 2 * D_MODEL), BF16)],
        compiler_params=_params("arbitrary"),
        name="inproj",
    )(x2, mod3, norm1, cos, sin, w_ret, w_lat, w_gate)


def _mla_kernel(lat_ref, qn_ref, kvn_ref, wq_ref, wkv_ref, cos_ref, sin_ref,
                rq_ref, rk_ref, rv_ref, rg_ref, dec_ref, xi_ref, zeta_ref, cd_ref, o_ref, yr_ref,
                cq_s, ckv_s, kpe_s, q_s, k_s, v_s, state_ref):
    h = pl.program_id(1)
    o_q, o_kv, o_pe, o_rot = 0, MLA_Q_LORA, MLA_Q_LORA + MLA_KV_LORA, MLA_Q_LORA + MLA_KV_LORA + MLA_ROPE
    cos, sin = cos_ref[...], sin_ref[...]

    @pl.when(h == 0)
    def _():
        cq_s[...] = (_rms(lat_ref[:, o_q:o_kv].astype(F32)) * qn_ref[...]).astype(BF16)
        ckv_s[...] = (_rms(lat_ref[:, o_kv:o_pe].astype(F32)) * kvn_ref[...]).astype(BF16)
        pe = lat_ref[:, o_pe:o_rot].astype(F32)
        rot = lat_ref[:, o_rot:o_rot + MLA_ROPE].astype(F32)
        kpe_s[...] = (pe * cos + rot * sin).astype(BF16)

    scale = (MLA_QK ** -0.5) * LOG2_E
    for g in range(MLA_HPS):
        qf = _dot(cq_s[...], wq_ref[g])
        q_s[g, :, :MLA_NOPE] = (qf[:, :MLA_NOPE] * scale).astype(BF16)
        q_pe = qf[:, MLA_NOPE:MLA_QK] * cos + qf[:, MLA_QK:] * sin
        q_s[g, :, MLA_NOPE:] = (q_pe * scale).astype(BF16)
        kvf = _dot(ckv_s[...], wkv_ref[g])
        k_s[g, :, :MLA_NOPE] = kvf[:, :MLA_NOPE].astype(BF16)
        k_s[g, :, MLA_NOPE:] = kpe_s[...]
        v_s[g, :, :MLA_V] = kvf[:, MLA_NOPE:].astype(BF16)
        v_s[g, :, MLA_V:] = jnp.ones((SEQ, MLA_V), BF16)

    causal = lax.broadcasted_iota(jnp.int32, (TQ, TQ), 0) >= lax.broadcasted_iota(jnp.int32, (TQ, TQ), 1)
    heads = range(MLA_HPS)
    n_blk = SEQ // TQ

    def scores(i):
        lo, hi = i * TQ, (i + 1) * TQ
        diag = [jnp.where(causal, _dot_nt(q_s[g, lo:hi, :], k_s[g, lo:hi, :]), -jnp.inf) for g in heads]
        past = [_dot_nt(q_s[g, lo:hi, :], k_s[g, :lo, :]) if i > 0 else None for g in heads]
        return diag, past

    state_ref[...] = jnp.zeros_like(state_ref)
    pending = scores(0)
    for i in range(n_blk):
        lo, hi = i * TQ, (i + 1) * TQ
        diag, past = pending
        if i + 1 < n_blk:
            pending = scores(i + 1)
        rq, rk, rv = rq_ref[lo:hi, :], rk_ref[lo:hi, :], rv_ref[lo:hi, :]
        r_scores = (_dot_nt(rq, rk) * dec_ref[h]).astype(BF16)
        carried = xi_ref[h] * _dot(rq, state_ref[...].astype(BF16))
        m = [jnp.max(diag[g], axis=-1, keepdims=True) for g in heads]
        if i > 0:
            m = [jnp.maximum(m[g], jnp.max(past[g], axis=-1, keepdims=True)) for g in heads]
        ry = _dot(r_scores, rv) + carried
        k_dec = (rk.astype(F32) * zeta_ref[h]).astype(BF16)
        state_ref[...] = state_ref[...] * cd_ref[h] + _dot_tn(k_dec, rv)
        acc = [_dot(jnp.exp2(diag[g] - m[g]).astype(BF16), v_s[g, lo:hi, :]) for g in heads]
        if i > 0:
            acc = [acc[g] + _dot(jnp.exp2(past[g] - m[g]).astype(BF16), v_s[g, :lo, :]) for g in heads]
        ryc = ry - jnp.mean(ry, axis=-1, keepdims=True)
        r_inv = lax.rsqrt(jnp.mean(ryc * ryc, axis=-1, keepdims=True) + EPS)
        yr_ref[lo:hi, :] = (rg_ref[lo:hi, :].astype(F32) * (ryc * r_inv)).astype(BF16)
        for g in heads:
            o_ref[lo:hi, g * MLA_V:(g + 1) * MLA_V] = (acc[g][:, :MLA_V] / acc[g][:, MLA_V:]).astype(BF16)


def _mla(lat3, q_norm, kv_norm, wq, wkv, cos, sin, ret3, dec, xi, zeta, cd):
    hps = MLA_HPS
    assert MLA_HEADS // hps == RET_HEADS and TQ == RET_CHUNK
    ret_part = lambda part: pl.BlockSpec((None, SEQ, RET_DK), lambda b, h: (b, 0, part * RET_HEADS + h))
    whole = lambda a: pl.BlockSpec(a.shape, lambda b, h: (0,) * a.ndim)
    return pl.pallas_call(
        _mla_kernel,
        grid=(BATCH, MLA_HEADS // hps),
        in_specs=[pl.BlockSpec((None, SEQ, MLA_LAT_W), lambda b, h: (b, 0, 0)),
                  pl.BlockSpec((1, MLA_Q_LORA), lambda b, h: (0, 0)),
                  pl.BlockSpec((1, MLA_KV_LORA), lambda b, h: (0, 0)),
                  pl.BlockSpec((hps, MLA_Q_LORA, MLA_QK + MLA_ROPE), lambda b, h: (h, 0, 0)),
                  pl.BlockSpec((hps, MLA_KV_LORA, MLA_NOPE + MLA_V), lambda b, h: (h, 0, 0)),
                  pl.BlockSpec((SEQ, MLA_ROPE), lambda b, h: (0, 0)),
                  pl.BlockSpec((SEQ, MLA_ROPE), lambda b, h: (0, 0)),
                  ret_part(0), ret_part(1), ret_part(2), ret_part(3),
                  whole(dec), whole(xi), whole(zeta), whole(cd)],
        out_specs=[pl.BlockSpec((None, SEQ, hps * MLA_V), lambda b, h: (b, 0, h)),
                   pl.BlockSpec((None, SEQ, RET_DV), lambda b, h: (b, 0, h))],
        out_shape=[jax.ShapeDtypeStruct((BATCH, SEQ, MLA_HEADS * MLA_V), BF16),
                   jax.ShapeDtypeStruct((BATCH, SEQ, RET_HEADS * RET_DV), BF16)],
        scratch_shapes=[pltpu.VMEM((SEQ, MLA_Q_LORA), BF16),
                        pltpu.VMEM((SEQ, MLA_KV_LORA), BF16),
                        pltpu.VMEM((SEQ, MLA_ROPE), BF16),
                        pltpu.VMEM((hps, SEQ, MLA_QK), BF16),
                        pltpu.VMEM((hps, SEQ, MLA_QK), BF16),
                        pltpu.VMEM((hps, SEQ, 2 * MLA_V), BF16),
                        pltpu.VMEM((RET_DK, RET_DV), F32)],
        compiler_params=_params("arbitrary", "arbitrary"),
        name="mla",
    )(lat3, q_norm, kv_norm, wq, wkv, cos, sin, ret3, ret3, ret3, ret3, dec, xi, zeta, cd)


def _route(logits_t):
    tm = logits_t.shape[1]
    row = lax.broadcasted_iota(jnp.int32, (SUBLANES, tm), 0)
    neg = -jnp.inf
    gl = jnp.where(row < N_GROUPS, logits_t[:SUBLANES], neg)
    gmax = jnp.max(gl, axis=0, keepdims=True)
    gsel = jnp.min(jnp.where(gl == gmax, row, SUBLANES), axis=0, keepdims=True)
    p_grp = 1.0 / jnp.sum(jnp.exp(gl - gmax), axis=0, keepdims=True)
    el = logits_t[SUBLANES * N_GROUPS:SUBLANES * (N_GROUPS + 1)]
    for g in reversed(range(N_GROUPS - 1)):
        el = jnp.where(gsel == g, logits_t[SUBLANES * (g + 1):SUBLANES * (g + 2)], el)
    v0 = jnp.max(el, axis=0, keepdims=True)
    i0 = jnp.min(jnp.where(el == v0, row, SUBLANES), axis=0, keepdims=True)
    el1 = jnp.where(row == i0, neg, el)
    v1 = jnp.max(el1, axis=0, keepdims=True)
    i1 = jnp.min(jnp.where(el1 == v1, row, SUBLANES), axis=0, keepdims=True)
    t = jnp.exp(v1 - v0)
    w0 = p_grp / (1.0 + t)
    w1 = p_grp * t / (1.0 + t)
    return gsel * EXPERTS_PER_GROUP + i0, gsel * EXPERTS_PER_GROUP + i1, w0, w1


def _stack_rows(rows, n):
    tm = rows[0].shape[1]
    row = lax.broadcasted_iota(jnp.int32, (n, tm), 0)
    out = jnp.zeros((n, tm), F32)
    for k, r in enumerate(rows):
        out = jnp.where(row == k, r, out)
    return out


def _to_token_tiles(ref, val):
    n = val.shape[0]
    for s in range(ROW_TILES):
        ref[pl.ds(s, n, stride=ROW_TILES), :] = val[:, s * LANES:(s + 1) * LANES]


def _from_token_tiles(ref):
    n = ref.shape[0] // ROW_TILES
    return jnp.concatenate([ref[pl.ds(s, n, stride=ROW_TILES), :] for s in range(ROW_TILES)], axis=-1)


def _token_rows(ref, t):
    return ref.at[pl.ds(pl.multiple_of(t * ROW_TILES, ROW_TILES), ROW_TILES)]


def _merge_kernel(yr_ref, at_ref, gr_ref, gm_ref, x_ref, mod_ref, n2_ref, wro_ref, wmo_ref, wo_ref,
                  wrt_ref, brt_ref, h1_ref, u2_ref, meta_ref, wtok_ref, cnt_ref, carry_ref):
    tm = x_ref.shape[0]

    @pl.when(pl.program_id(0) == 0)
    def _():
        carry_ref[...] = jnp.zeros_like(carry_ref)

    chunks = range(MERGE_CHAINS)
    sub = tm // MERGE_CHAINS
    rows = [slice(c * sub, (c + 1) * sub) for c in chunks]
    y_ret = [_dot(yr_ref[r, :], wro_ref[...]) for r in rows]
    y_mla = [_dot(at_ref[r, :], wmo_ref[...]) for r in rows]
    merged = [(_sigmoid(gr_ref[rows[c], :].astype(F32)) * y_ret[c]
               + _sigmoid(gm_ref[rows[c], :].astype(F32)) * y_mla[c]).astype(BF16) for c in chunks]
    o = [_dot(merged[c], wo_ref[...]) for c in chunks]
    h1 = [x_ref[rows[c], :] + mod_ref[2:3, :] * o[c] for c in chunks]
    for c in chunks:
        h1_ref[rows[c], :] = h1[c]
    u2 = [_rms(h1[c]) * n2_ref[...] * (1.0 + mod_ref[4:5, :]) + mod_ref[3:4, :] for c in chunks]
    for c in chunks:
        _to_token_tiles(u2_ref.at[pl.ds(c * sub * ROW_TILES, sub * ROW_TILES)], u2[c])
    w = wrt_ref[...]
    w_hi = w.astype(BF16)
    w_lo = (w - w_hi.astype(F32)).astype(BF16)
    w_both = jnp.concatenate([w_hi, w_lo], axis=0)
    u_hi = [u2[c].astype(BF16) for c in chunks]
    u_lo = [(u2[c] - u_hi[c].astype(F32)).astype(BF16) for c in chunks]
    by_hi = [_dot_nt(w_both, u_hi[c]) for c in chunks]
    logits_t = [by_hi[c][:ROUTER_ROWS] + by_hi[c][ROUTER_ROWS:] + _dot_nt(w_hi, u_lo[c]) + brt_ref[:, 0:1]
                for c in chunks]
    routed = [_route(logits_t[c]) for c in chunks]
    e0, e1, w0, w1 = [jnp.concatenate([routed[c][k] for c in chunks], axis=1) for k in range(4)]
    erow = lax.broadcasted_iota(jnp.int32, (N_EXPERTS, tm), 0)
    m0, m1 = erow == e0, erow == e1
    member = jnp.where(m0 | m1, 1.0, 0.0)
    earlier = jnp.where(lax.broadcasted_iota(jnp.int32, (tm, tm), 0) < lax.broadcasted_iota(jnp.int32, (tm, tm), 1),
                        1.0, 0.0).astype(BF16)
    prefix = _dot(member.astype(BF16), earlier) + carry_ref[:, 0:1]
    rank0 = jnp.sum(jnp.where(m0, prefix, 0.0), axis=0, keepdims=True)
    rank1 = jnp.sum(jnp.where(m1, prefix, 0.0), axis=0, keepdims=True)
    carry_ref[...] = carry_ref[...] + jnp.sum(member, axis=1, keepdims=True)
    cnt_ref[...] = carry_ref[...]
    meta_ref[...] = _stack_rows([e0.astype(F32), e1.astype(F32), rank0, rank1], SUBLANES)
    wt = _stack_rows([w0, w1], 2 * SUBLANES)
    wt_hi = wt.astype(BF16)
    wt_lo = (wt - wt_hi.astype(F32)).astype(BF16)
    place = jnp.where(lax.broadcasted_iota(jnp.int32, (2 * SUBLANES, LANES), 0)
                      == lax.broadcasted_iota(jnp.int32, (2 * SUBLANES, LANES), 1), 1.0, 0.0).astype(BF16)
    wtok_ref[...] = _dot_tn(wt_hi, place) + _dot_tn(wt_lo, place)


def _merge(y_ret, attn, gates, x2, mod3, norm2, w_ret_o, w_mla_o, w_out, w_rt, b_rt):
    tm = TM_PROJ
    per_b = SEQ // tm
    row = lambda j: pl.BlockSpec((tm, D_MODEL), lambda i: (i, j))
    return pl.pallas_call(
        _merge_kernel,
        grid=(N_TOK // tm,),
        in_specs=[row(0), row(0), row(0), row(1), row(0),
                  pl.BlockSpec((None, 6, D_MODEL), lambda i: (i // per_b, 0, 0)),
                  _resident((1, D_MODEL)),
                  _resident(w_ret_o.shape), _resident(w_mla_o.shape), _resident(w_out.shape),
                  _resident(w_rt.shape), _resident(b_rt.shape)],
        out_specs=[row(0),
                   pl.BlockSpec((tm * ROW_TILES, LANES), lambda i: (i, 0)),
                   pl.BlockSpec((SUBLANES, tm), lambda i: (0, i)),
                   pl.BlockSpec((tm, LANES), lambda i: (i, 0)),
                   pl.BlockSpec((N_EXPERTS, LANES), lambda i: (0, 0))],
        out_shape=[jax.ShapeDtypeStruct((N_TOK, D_MODEL), F32),
                   jax.ShapeDtypeStruct((N_TOK * ROW_TILES, LANES), F32),
                   jax.ShapeDtypeStruct((SUBLANES, N_TOK), F32),
                   jax.ShapeDtypeStruct((N_TOK, LANES), F32),
                   jax.ShapeDtypeStruct((N_EXPERTS, LANES), F32)],
        scratch_shapes=[pltpu.VMEM((N_EXPERTS, LANES), F32)],
        compiler_params=_params("arbitrary"),
        name="merge",
    )(y_ret, attn, gates, gates, x2, mod3, norm2, w_ret_o, w_mla_o, w_out, w_rt, b_rt)


def _plan_kernel(meta_ref, off_ref, dst_ref):
    m = meta_ref[...]
    erow = lax.broadcasted_iota(jnp.int32, (N_EXPERTS, m.shape[1]), 0)
    off = off_ref[:, 0:1]
    d = [jnp.sum(jnp.where(erow == m[k:k + 1].astype(jnp.int32), off, 0.0), axis=0, keepdims=True) + m[k + 2:k + 3]
         for k in range(TOP_K)]
    dst_ref[...] = _stack_rows(d, SUBLANES).astype(jnp.int32)


def _plan(meta_t, off_col):
    tm = 2048
    return pl.pallas_call(
        _plan_kernel,
        grid=(N_TOK // tm,),
        in_specs=[pl.BlockSpec((SUBLANES, tm), lambda i: (0, i)),
                  pl.BlockSpec((N_EXPERTS, LANES), lambda i: (0, 0))],
        out_specs=pl.BlockSpec((SUBLANES, tm), lambda i: (0, i)),
        out_shape=jax.ShapeDtypeStruct((SUBLANES, N_TOK), jnp.int32),
        compiler_params=_params("arbitrary"),
        name="plan",
    )(meta_t, off_col)


def _row_copy_wait(src_like, dst_like, sem):
    pltpu.make_async_copy(src_like, dst_like, sem).wait()


def _dispatch_kernel(d0_ref, d1_ref, seg_ref, cnt_ref, u_ref, xs_ref, zero_ref, sem, zsem):
    i = pl.program_id(0)
    tm = u_ref.shape[0] // ROW_TILES
    tile_rows = TE * ROW_TILES

    def slot_tile(j):
        return xs_ref.at[pl.ds(pl.multiple_of(j * tile_rows, tile_rows), tile_rows)]

    @pl.when(i == 0)
    def _():
        zero_ref[...] = jnp.zeros_like(zero_ref)

        def pad(e):
            first = seg_ref[e] + cnt_ref[e]
            rows = (seg_ref[e + 1] - first) * ROW_TILES
            start = pl.multiple_of(first * ROW_TILES, ROW_TILES)
            return rows > 0, pltpu.make_async_copy(zero_ref.at[pl.ds(0, rows)], xs_ref.at[pl.ds(start, rows)], zsem)

        for e in range(N_EXPERTS):
            nonempty, copy = pad(e)
            pl.when(nonempty)(copy.start)
        for e in range(N_EXPERTS):
            nonempty, copy = pad(e)
            pl.when(nonempty)(copy.wait)

        def unused(j):
            return pltpu.make_async_copy(zero_ref, slot_tile(j), zsem)

        first_unused = seg_ref[N_EXPERTS] // TE
        lax.fori_loop(first_unused, N_TILES, lambda j, c: (unused(j).start(), c)[1], 0)
        lax.fori_loop(first_unused, N_TILES, lambda j, c: (unused(j).wait(), c)[1], 0)

    base = i * tm

    def body(r, carry):
        src = _token_rows(u_ref, r)
        pltpu.make_async_copy(src, _token_rows(xs_ref, d0_ref[base + r]), sem).start(priority=0)
        pltpu.make_async_copy(src, _token_rows(xs_ref, d1_ref[base + r]), sem).start(priority=1)
        return carry

    lax.fori_loop(0, tm, body, 0, unroll=8)
    _row_copy_wait(u_ref, xs_ref.at[pl.ds(0, tm * ROW_TILES)], sem)
    _row_copy_wait(u_ref, xs_ref.at[pl.ds(0, tm * ROW_TILES)], sem)


def _dispatch(d0, d1, seg, cnt, u2t):
    tm = TM_DISPATCH
    return pl.pallas_call(
        _dispatch_kernel,
        grid_spec=pltpu.PrefetchScalarGridSpec(
            num_scalar_prefetch=4,
            grid=(N_TOK // tm,),
            in_specs=[pl.BlockSpec((tm * ROW_TILES, LANES), lambda i, *_: (i, 0))],
            out_specs=pl.BlockSpec(memory_space=pl.ANY),
            scratch_shapes=[pltpu.VMEM((TE * ROW_TILES, LANES), F32),
                            pltpu.SemaphoreType.DMA(()), pltpu.SemaphoreType.DMA(())]),
        out_shape=jax.ShapeDtypeStruct((N_SLOTS * ROW_TILES, LANES), F32),
        compiler_params=_params("arbitrary"),
        name="dispatch",
    )(d0, d1, seg, cnt, u2t)


def _expert_kernel(te_ref, nv_ref, x_ref, w1_ref, w3_ref, w2_ref, y_ref, w1_s, w3_s, w2_s):
    j = pl.program_id(0)

    @pl.when(j < nv_ref[0])
    def _():
        @pl.when((j == 0) | (te_ref[j] != te_ref[jnp.maximum(j - 1, 0)]))
        def _():
            w1_s[...] = w1_ref[...].astype(BF16)
            w3_s[...] = w3_ref[...].astype(BF16)
            w2_s[...] = w2_ref[...].astype(BF16)

        sub = TE // TE_CHAINS * ROW_TILES
        part = lambda ref, c: ref.at[pl.ds(c * sub, sub)]
        chains = range(TE_CHAINS)
        x = [_from_token_tiles(part(x_ref, c)).astype(BF16) for c in chains]
        a = [_dot(x[c], w1_s[...]) for c in chains]
        b = [_dot(x[c], w3_s[...]) for c in chains]
        hid = [(a[c] * _sigmoid(a[c]) * b[c]).astype(BF16) for c in chains]
        for c in chains:
            _to_token_tiles(part(y_ref, c), _dot(hid[c], w2_s[...]))


def _experts(tile_expert, n_valid, xs, w1, w3, w2):
    tile = lambda j, te, nv: jnp.minimum(j, nv[0] - 1)
    wspec = lambda shape: pl.BlockSpec((None,) + shape, lambda j, te, nv: (te[tile(j, te, nv)], 0, 0))
    slots = pl.BlockSpec((TE * ROW_TILES, LANES), lambda j, te, nv: (tile(j, te, nv), 0))
    return pl.pallas_call(
        _expert_kernel,
        grid_spec=pltpu.PrefetchScalarGridSpec(
            num_scalar_prefetch=2,
            grid=(N_TILES,),
            in_specs=[slots, wspec((D_MODEL, D_EXPERT)), wspec((D_MODEL, D_EXPERT)), wspec((D_EXPERT, D_MODEL))],
            out_specs=slots,
            scratch_shapes=[pltpu.VMEM((D_MODEL, D_EXPERT), BF16), pltpu.VMEM((D_MODEL, D_EXPERT), BF16),
                            pltpu.VMEM((D_EXPERT, D_MODEL), BF16)]),
        out_shape=jax.ShapeDtypeStruct((N_SLOTS * ROW_TILES, LANES), F32),
        input_output_aliases={2: 0},
        compiler_params=_params("arbitrary"),
        name="experts",
    )(tile_expert, n_valid, xs, w1, w3, w2)


def _final_kernel(d0_ref, d1_ref, h1_ref, meta_ref, mod_ref, fn_ref, ys_ref, o_ref, ybuf, sem):
    i = pl.program_id(0)
    tm = h1_ref.shape[0]

    def gather(t):
        buf, s = ybuf.at[t % 2], sem.at[t % 2]
        base = t * tm

        def body(r, carry):
            pltpu.make_async_copy(_token_rows(ys_ref, d0_ref[base + r]), _token_rows(buf.at[0], r), s).start(priority=0)
            pltpu.make_async_copy(_token_rows(ys_ref, d1_ref[base + r]), _token_rows(buf.at[1], r), s).start(priority=1)
            return carry

        lax.fori_loop(0, tm, body, 0, unroll=8)

    @pl.when(i == 0)
    def _():
        gather(0)

    @pl.when(i + 1 < pl.num_programs(0))
    def _():
        gather(i + 1)

    buf = ybuf.at[i % 2]
    for k in range(TOP_K):
        _row_copy_wait(ys_ref.at[pl.ds(0, tm * ROW_TILES)], buf.at[k], sem.at[i % 2])
    m = meta_ref[...]
    moe = m[:, 0:1] * _from_token_tiles(buf.at[0]) + m[:, 1:2] * _from_token_tiles(buf.at[1])
    h2 = h1_ref[...] + mod_ref[5:6, :] * moe
    o_ref[...] = _rms(h2) * fn_ref[...]


def _final(d0, d1, h1, meta, mod3, final_norm, ys):
    tm = TM_FINAL
    per_b = SEQ // tm
    return pl.pallas_call(
        _final_kernel,
        grid_spec=pltpu.PrefetchScalarGridSpec(
            num_scalar_prefetch=2,
            grid=(N_TOK // tm,),
            in_specs=[pl.BlockSpec((tm, D_MODEL), lambda i, *_: (i, 0)),
                      pl.BlockSpec((tm, LANES), lambda i, *_: (i, 0)),
                      pl.BlockSpec((None, 6, D_MODEL), lambda i, *_: (i // per_b, 0, 0)),
                      pl.BlockSpec((1, D_MODEL), lambda i, *_: (0, 0)),
                      pl.BlockSpec(memory_space=pl.ANY)],
            out_specs=pl.BlockSpec((tm, D_MODEL), lambda i, *_: (i, 0)),
            scratch_shapes=[pltpu.VMEM((2, TOP_K, tm * ROW_TILES, LANES), F32), pltpu.SemaphoreType.DMA((2,))]),
        out_shape=jax.ShapeDtypeStruct((N_TOK, D_MODEL), F32),
        compiler_params=_params("arbitrary"),
        name="final",
    )(d0, d1, h1, meta, mod3, final_norm, ys)


def _slot_layout(counts):
    cnt = counts[:, 0].astype(jnp.int32)
    tile_end = jnp.cumsum((cnt + TE - 1) // TE)
    seg = jnp.concatenate([jnp.zeros((1,), jnp.int32), tile_end * TE])
    off_col = jnp.broadcast_to(seg[:-1].astype(F32)[:, None], (N_EXPERTS, LANES))
    tile_ids = jnp.arange(N_TILES, dtype=jnp.int32)
    tile_expert = jnp.sum((tile_end[None, :] <= tile_ids[:, None]).astype(jnp.int32), axis=1)
    tile_expert = jnp.minimum(tile_expert, N_EXPERTS - 1)
    return cnt, seg, off_col, tile_expert, tile_end[-1:]


def _rope_tables(dim):
    pos = np.arange(SEQ, dtype=np.float64)
    inv = ROPE_THETA ** (-np.arange(0, dim, 2, dtype=np.float64) / dim)
    ang = pos[:, None] * inv[None, :]
    return np.cos(ang).astype(np.float32), np.sin(ang).astype(np.float32)


def _decay_tables():
    c = RET_CHUNK
    log_gamma = np.log1p(-np.exp2(-5.0 - np.arange(RET_HEADS, dtype=np.float64)))
    idx = np.arange(c, dtype=np.float64)
    rel = idx[:, None] - idx[None, :]
    dec = np.where(rel[None] >= 0, np.exp(log_gamma[:, None, None] * np.maximum(rel, 0.0)[None]), 0.0)
    xi = np.exp(log_gamma[:, None] * (idx[None, :] + 1.0))[:, :, None]
    zeta = np.exp(log_gamma[:, None] * (c - 1.0 - idx[None, :]))[:, :, None]
    cd = np.exp(log_gamma * c)[:, None, None]
    return tuple(jnp.asarray(t.astype(np.float32)) for t in (dec, xi, zeta, cd))


def _rotate_half_cols(w):
    half = w.shape[-1] // 2
    return jnp.concatenate([-w[..., half:], w[..., :half]], axis=-1)


def kernel(x, c, w_ada, b_ada, norm1, norm2, w_in, w_ret_o, q_norm, kv_norm, w_uq, w_ukv, w_mla_o, w_out,
           w_grp, b_grp, w_exp, b_exp, w1, w3, w2, final_norm):
    assert x.shape == (BATCH, SEQ, D_MODEL) and w_ada.shape[0] == 1
    x2 = x.reshape(N_TOK, D_MODEL)

    w_ret, w_lat, w_gate = _wprep(jnp.transpose(w_in[0]))
    wq = w_uq[0].reshape(MLA_Q_LORA, MLA_HEADS, MLA_QK)
    wq = jnp.concatenate([wq, _rotate_half_cols(wq[..., MLA_NOPE:])], axis=-1)
    wq = wq.transpose(1, 0, 2).astype(BF16)
    wkv = w_ukv[0].reshape(MLA_KV_LORA, MLA_HEADS, MLA_NOPE + MLA_V).transpose(1, 0, 2).astype(BF16)
    gap = jnp.zeros((SUBLANES - N_GROUPS, D_MODEL), F32)
    tail = jnp.zeros((ROUTER_ROWS - SUBLANES - N_EXPERTS, D_MODEL), F32)
    w_rt = jnp.concatenate([w_grp[0].T, gap, w_exp[0].T, tail], axis=0)
    b_rt = jnp.concatenate([b_grp[0], gap[:, 0], b_exp[0], tail[:, 0]])
    b_rt = jnp.broadcast_to(b_rt[:, None], (ROUTER_ROWS, LANES))

    ret_cos, ret_sin = (jnp.asarray(t) for t in _rope_tables(RET_DK))
    mla_cos, mla_sin = (jnp.asarray(np.concatenate([t, t], axis=-1)) for t in _rope_tables(MLA_ROPE))
    dec, xi, zeta, cd = _decay_tables()

    mod3 = _ada(c, w_ada[0], b_ada[0]).reshape(BATCH, 6, D_MODEL)
    ret, lat, gates = _inproj(x2, mod3, norm1, ret_cos, ret_sin, w_ret, w_lat, w_gate)
    attn, y_ret = _mla(lat.reshape(BATCH, SEQ, MLA_LAT_W), q_norm, kv_norm, wq, wkv, mla_cos, mla_sin,
                       ret.reshape(BATCH, SEQ, 4 * RET_W), dec, xi, zeta, cd)
    h1, u2t, meta_t, wtok, counts = _merge(y_ret.reshape(N_TOK, D_MODEL), attn.reshape(N_TOK, D_MODEL), gates, x2,
                                           mod3, norm2, w_ret_o[0].astype(BF16), w_mla_o[0].astype(BF16),
                                           w_out[0].astype(BF16), w_rt, b_rt)
    cnt, seg, off_col, tile_expert, n_valid = _slot_layout(counts)
    dst = _plan(meta_t, off_col)
    d0, d1 = dst[0], dst[1]
    xs = _dispatch(d0, d1, seg, cnt, u2t)
    e_shape = (N_EXPERTS, D_MODEL, D_EXPERT)
    ys = _experts(tile_expert, n_valid, xs, w1[0].reshape(e_shape), w3[0].reshape(e_shape),
                  w2[0].reshape(N_EXPERTS, D_EXPERT, D_MODEL))
    out = _final(d0, d1, h1, wtok, mod3, final_norm.reshape(1, D_MODEL), ys)
    return out.reshape(BATCH, SEQ, D_MODEL)
```

```python
import numpy as np
import jax
import jax.numpy as jnp
from jax import lax
from jax.experimental import pallas as pl
from jax.experimental.pallas import tpu as pltpu

D_MODEL = 1024
BATCH = 8
SEQ = 2048
N_TOK = BATCH * SEQ

RET_HEADS = 4
RET_DK = 256
RET_DV = 256
RET_CHUNK = 256
RET_W = RET_HEADS * RET_DK

MLA_HEADS = 8
MLA_NOPE = 128
MLA_ROPE = 64
MLA_V = 128
MLA_Q_LORA = 384
MLA_KV_LORA = 256
MLA_LAT_W = MLA_Q_LORA + MLA_KV_LORA + 2 * MLA_ROPE
MLA_QK = MLA_NOPE + MLA_ROPE
ROPE_THETA = 10000.0

N_GROUPS = 4
EXPERTS_PER_GROUP = 8
N_EXPERTS = N_GROUPS * EXPERTS_PER_GROUP
D_EXPERT = 256
EPS = 1e-6
LOG2_E = 1.4426950408889634

LANES = 128
SUBLANES = 8
ROUTER_ROWS = 48
ROW_TILES = D_MODEL // LANES
VMEM_LIMIT = 56 * 1024 * 1024

TM_PROJ = 512
TM_INPROJ = 512
TM_FINAL = 512
TM_DISPATCH = 4096
TQ = 256
MLA_HPS = 2
TE = 512
TE_CHAINS = 2
MERGE_CHAINS = 2
TOP_K = 2
N_TILES = N_TOK * TOP_K // TE + N_EXPERTS
N_SLOTS = N_TILES * TE

F32 = jnp.float32
BF16 = jnp.bfloat16


def _sigmoid(x):
    return 1.0 / (1.0 + jnp.exp(-x))


def _rms(x):
    return x * lax.rsqrt(jnp.mean(x * x, axis=-1, keepdims=True) + EPS)


def _dot(a, b):
    return jnp.dot(a, b, preferred_element_type=F32)


def _dot_nt(a, b):
    return lax.dot_general(a, b, (((1,), (1,)), ((), ())), preferred_element_type=F32)


def _dot_tn(a, b):
    return lax.dot_general(a, b, (((0,), (0,)), ((), ())), preferred_element_type=F32)


def _params(*sem):
    return pltpu.CompilerParams(dimension_semantics=sem, vmem_limit_bytes=VMEM_LIMIT)


def _resident(shape):
    nd = len(shape)
    return pl.BlockSpec(shape, lambda *_: (0,) * nd, pipeline_mode=pl.Buffered(1))


def _ada_kernel(c_ref, w_ref, b_ref, o_ref):
    c = c_ref[...]
    act = (c * _sigmoid(c)).astype(BF16)
    o_ref[...] = _dot(act, w_ref[...].astype(BF16)) + b_ref[...]


def _ada(c, w_ada, b_ada):
    n = w_ada.shape[1]
    tn = D_MODEL
    return pl.pallas_call(
        _ada_kernel,
        grid=(n // tn,),
        in_specs=[pl.BlockSpec((BATCH, D_MODEL), lambda j: (0, 0)),
                  pl.BlockSpec((D_MODEL, tn), lambda j: (0, j)),
                  pl.BlockSpec((1, tn), lambda j: (0, j))],
        out_specs=pl.BlockSpec((BATCH, tn), lambda j: (0, j)),
        out_shape=jax.ShapeDtypeStruct((BATCH, n), F32),
        compiler_params=_params("arbitrary"),
        name="ada",
    )(c, w_ada, b_ada.reshape(1, n))


O_LAT = 4 * RET_W
O_PE = O_LAT + MLA_Q_LORA + MLA_KV_LORA
O_GATE = O_PE + MLA_ROPE


N_IN = O_GATE + 2 * D_MODEL
WPREP_ROWS = 512


def _wprep_kernel(w_hbm, ret_ref, lat_ref, gate_ref, buf, sem):
    chunks = [(s, min(WPREP_ROWS, N_IN - s)) for s in range(0, N_IN, WPREP_ROWS)]

    def copy(i):
        s, n = chunks[i]
        return pltpu.make_async_copy(w_hbm.at[pl.ds(s, n)], buf.at[i % 2, pl.ds(0, n)], sem.at[i % 2])

    groups = [(0, RET_W, ret_ref, 0, 1.0), (RET_W, 2 * RET_W, ret_ref, RET_W, RET_DK ** -0.5),
              (2 * RET_W, O_LAT, ret_ref, 2 * RET_W, 1.0), (O_LAT, O_GATE, lat_ref, 0, 1.0),
              (O_GATE, N_IN, gate_ref, 0, 1.0)]
    half = MLA_ROPE // 2
    rot = [(O_PE + half, O_GATE, O_GATE - O_LAT, -1.0), (O_PE, O_PE + half, O_GATE - O_LAT + half, 1.0)]

    copy(0).start()
    for i, (s, n) in enumerate(chunks):
        if i + 1 < len(chunks):
            copy(i + 1).start()
        copy(i).wait()
        for lo, hi, dst, dst_lo, scale in groups + [(a, b, lat_ref, d, sc) for a, b, d, sc in rot]:
            a, b = max(lo, s), min(hi, s + n)
            if a < b:
                x = buf[i % 2, a - s:b - s, :]
                dst[dst_lo + a - lo:dst_lo + b - lo, :] = (x if scale == 1.0 else x * scale).astype(BF16)


def _wprep(w_in_t):
    out_rows = (O_LAT, MLA_LAT_W, 2 * D_MODEL)
    whole = lambda n: pl.BlockSpec((n, D_MODEL), lambda i: (0, 0))
    return pl.pallas_call(
        _wprep_kernel,
        grid=(1,),
        in_specs=[pl.BlockSpec(memory_space=pl.ANY)],
        out_specs=[whole(n) for n in out_rows],
        out_shape=[jax.ShapeDtypeStruct((n, D_MODEL), BF16) for n in out_rows],
        scratch_shapes=[pltpu.VMEM((2, WPREP_ROWS, D_MODEL), F32), pltpu.SemaphoreType.DMA((2,))],
        compiler_params=_params("arbitrary"),
        name="wprep",
    )(w_in_t)


def _inproj_kernel(x_ref, mod_ref, n1_ref, cos_ref, sin_ref, mcos_ref, msin_ref, qn_ref, kvn_ref,
                   wr_ref, wm_ref, wg_ref, ret_ref, lat_ref, gate_ref):
    y = _rms(x_ref[...]) * n1_ref[...]
    u = (y * (1.0 + mod_ref[1:2, :]) + mod_ref[0:1, :]).astype(BF16)
    cos, sin = cos_ref[...], sin_ref[...]
    half = RET_DK // 2
    for n in range(0, 2 * RET_W, RET_DK):
        p = _dot_nt(u, wr_ref[n:n + RET_DK, :])
        x1, x2 = p[:, :half], p[:, half:]
        ret_ref[:, n:n + half] = (x1 * cos - x2 * sin).astype(BF16)
        ret_ref[:, n + half:n + RET_DK] = (x2 * cos + x1 * sin).astype(BF16)
    step = 512
    for n in range(2 * RET_W, 3 * RET_W, step):
        ret_ref[:, n:n + step] = _dot_nt(u, wr_ref[n:n + step, :]).astype(BF16)
    for n in range(3 * RET_W, 4 * RET_W, step):
        p = _dot_nt(u, wr_ref[n:n + step, :])
        ret_ref[:, n:n + step] = (p * _sigmoid(p)).astype(BF16)
    lat = _dot_nt(u, wm_ref[...])
    o_kv, o_pe, o_rot = MLA_Q_LORA, MLA_Q_LORA + MLA_KV_LORA, MLA_Q_LORA + MLA_KV_LORA + MLA_ROPE
    lat_ref[:, :o_kv] = (_rms(lat[:, :o_kv]) * qn_ref[...]).astype(BF16)
    lat_ref[:, o_kv:o_pe] = (_rms(lat[:, o_kv:o_pe]) * kvn_ref[...]).astype(BF16)
    lat_ref[:, o_pe:o_rot] = (lat[:, o_pe:o_rot] * mcos_ref[...] + lat[:, o_rot:] * msin_ref[...]).astype(BF16)
    lat_ref[:, o_rot:] = jnp.zeros((lat.shape[0], MLA_ROPE), BF16)
    for n in range(0, 2 * D_MODEL, step):
        gate_ref[:, n:n + step] = _dot_nt(u, wg_ref[n:n + step, :]).astype(BF16)


def _inproj(x2, mod3, norm1, cos, sin, mla_cos, mla_sin, q_norm, kv_norm, w_ret, w_lat, w_gate):
    tm = TM_INPROJ
    per_b = SEQ // tm
    rope_tab = pl.BlockSpec((tm, RET_DK // 2), lambda i: (i % per_b, 0))
    mla_tab = pl.BlockSpec((tm, MLA_ROPE), lambda i: (i % per_b, 0))
    return pl.pallas_call(
        _inproj_kernel,
        grid=(N_TOK // tm,),
        in_specs=[pl.BlockSpec((tm, D_MODEL), lambda i: (i, 0)),
                  pl.BlockSpec((None, 6, D_MODEL), lambda i: (i // per_b, 0, 0)),
                  _resident((1, D_MODEL)), rope_tab, rope_tab, mla_tab, mla_tab,
                  _resident(q_norm.shape), _resident(kv_norm.shape),
                  _resident(w_ret.shape), _resident(w_lat.shape), _resident(w_gate.shape)],
        out_specs=[pl.BlockSpec((tm, 4 * RET_W), lambda i: (i, 0)),
                   pl.BlockSpec((tm, MLA_LAT_W), lambda i: (i, 0)),
                   pl.BlockSpec((tm, 2 * D_MODEL), lambda i: (i, 0))],
        out_shape=[jax.ShapeDtypeStruct((N_TOK, 4 * RET_W), BF16),
                   jax.ShapeDtypeStruct((N_TOK, MLA_LAT_W), BF16),
                   jax.ShapeDtypeStruct((N_TOK, 2 * D_MODEL), BF16)],
        compiler_params=_params("arbitrary"),
        name="inproj",
    )(x2, mod3, norm1, cos, sin, mla_cos, mla_sin, q_norm, kv_norm, w_ret, w_lat, w_gate)


def _mla_kernel(lat_ref, wq_ref, wkv_ref, cos_ref, sin_ref,
                rq_ref, rk_ref, rv_ref, rg_ref, dec_ref, xi_ref, zeta_ref, cd_ref, o_ref, yr_ref,
                q_s, k_s, v_s, state_ref):
    h = pl.program_id(1)
    o_kv, o_pe, o_rot = MLA_Q_LORA, MLA_Q_LORA + MLA_KV_LORA, MLA_Q_LORA + MLA_KV_LORA + MLA_ROPE
    cos, sin = cos_ref[...], sin_ref[...]

    scale = (MLA_QK ** -0.5) * LOG2_E
    for g in range(MLA_HPS):
        qf = _dot(lat_ref[:, :o_kv], wq_ref[g])
        q_s[g, :, :MLA_NOPE] = (qf[:, :MLA_NOPE] * scale).astype(BF16)
        q_pe = qf[:, MLA_NOPE:MLA_QK] * cos + qf[:, MLA_QK:] * sin
        q_s[g, :, MLA_NOPE:] = (q_pe * scale).astype(BF16)
        kvf = _dot(lat_ref[:, o_kv:o_pe], wkv_ref[g])
        k_s[g, :, :MLA_NOPE] = kvf[:, :MLA_NOPE].astype(BF16)
        k_s[g, :, MLA_NOPE:] = lat_ref[:, o_pe:o_rot]
        v_s[g, :, :MLA_V] = kvf[:, MLA_NOPE:].astype(BF16)
        v_s[g, :, MLA_V:] = jnp.ones((SEQ, MLA_V), BF16)

    causal = lax.broadcasted_iota(jnp.int32, (TQ, TQ), 0) >= lax.broadcasted_iota(jnp.int32, (TQ, TQ), 1)
    heads = range(MLA_HPS)
    n_blk = SEQ // TQ

    def scores(i):
        lo, hi = i * TQ, (i + 1) * TQ
        diag = [jnp.where(causal, _dot_nt(q_s[g, lo:hi, :], k_s[g, lo:hi, :]), -jnp.inf) for g in heads]
        past = [_dot_nt(q_s[g, lo:hi, :], k_s[g, :lo, :]) if i > 0 else None for g in heads]
        return diag, past

    state_ref[...] = jnp.zeros_like(state_ref)
    pending = scores(0)
    for i in range(n_blk):
        lo, hi = i * TQ, (i + 1) * TQ
        diag, past = pending
        if i + 1 < n_blk:
            pending = scores(i + 1)
        rq, rk, rv = rq_ref[lo:hi, :], rk_ref[lo:hi, :], rv_ref[lo:hi, :]
        r_scores = (_dot_nt(rq, rk) * dec_ref[h]).astype(BF16)
        carried = xi_ref[h] * _dot(rq, state_ref[...].astype(BF16))
        m = [jnp.max(diag[g], axis=-1, keepdims=True) for g in heads]
        if i > 0:
            m = [jnp.maximum(m[g], jnp.max(past[g], axis=-1, keepdims=True)) for g in heads]
        ry = _dot(r_scores, rv) + carried
        k_dec = (rk.astype(F32) * zeta_ref[h]).astype(BF16)
        state_ref[...] = state_ref[...] * cd_ref[h] + _dot_tn(k_dec, rv)
        acc = [_dot(jnp.exp2(diag[g] - m[g]).astype(BF16), v_s[g, lo:hi, :]) for g in heads]
        if i > 0:
            acc = [acc[g] + _dot(jnp.exp2(past[g] - m[g]).astype(BF16), v_s[g, :lo, :]) for g in heads]
        ryc = ry - jnp.mean(ry, axis=-1, keepdims=True)
        r_inv = lax.rsqrt(jnp.mean(ryc * ryc, axis=-1, keepdims=True) + EPS)
        yr_ref[lo:hi, :] = (rg_ref[lo:hi, :].astype(F32) * (ryc * r_inv)).astype(BF16)
        for g in heads:
            o_ref[lo:hi, g * MLA_V:(g + 1) * MLA_V] = (acc[g][:, :MLA_V] / acc[g][:, MLA_V:]).astype(BF16)


def _mla(lat3, wq, wkv, cos, sin, ret3, dec, xi, zeta, cd):
    hps = MLA_HPS
    assert MLA_HEADS // hps == RET_HEADS and TQ == RET_CHUNK
    ret_part = lambda part: pl.BlockSpec((None, SEQ, RET_DK), lambda b, h: (b, 0, part * RET_HEADS + h))
    whole = lambda a: pl.BlockSpec(a.shape, lambda b, h: (0,) * a.ndim)
    return pl.pallas_call(
        _mla_kernel,
        grid=(BATCH, MLA_HEADS // hps),
        in_specs=[pl.BlockSpec((None, SEQ, MLA_LAT_W), lambda b, h: (b, 0, 0)),
                  pl.BlockSpec((hps, MLA_Q_LORA, MLA_QK + MLA_ROPE), lambda b, h: (h, 0, 0)),
                  pl.BlockSpec((hps, MLA_KV_LORA, MLA_NOPE + MLA_V), lambda b, h: (h, 0, 0)),
                  pl.BlockSpec((SEQ, MLA_ROPE), lambda b, h: (0, 0)),
                  pl.BlockSpec((SEQ, MLA_ROPE), lambda b, h: (0, 0)),
                  ret_part(0), ret_part(1), ret_part(2), ret_part(3),
                  whole(dec), whole(xi), whole(zeta), whole(cd)],
        out_specs=[pl.BlockSpec((None, SEQ, hps * MLA_V), lambda b, h: (b, 0, h)),
                   pl.BlockSpec((None, SEQ, RET_DV), lambda b, h: (b, 0, h))],
        out_shape=[jax.ShapeDtypeStruct((BATCH, SEQ, MLA_HEADS * MLA_V), BF16),
                   jax.ShapeDtypeStruct((BATCH, SEQ, RET_HEADS * RET_DV), BF16)],
        scratch_shapes=[pltpu.VMEM((hps, SEQ, MLA_QK), BF16),
                        pltpu.VMEM((hps, SEQ, MLA_QK), BF16),
                        pltpu.VMEM((hps, SEQ, 2 * MLA_V), BF16),
                        pltpu.VMEM((RET_DK, RET_DV), F32)],
        compiler_params=_params("arbitrary", "arbitrary"),
        name="mla",
    )(lat3, wq, wkv, cos, sin, ret3, ret3, ret3, ret3, dec, xi, zeta, cd)


def _route(logits_t):
    tm = logits_t.shape[1]
    row = lax.broadcasted_iota(jnp.int32, (SUBLANES, tm), 0)
    neg = -jnp.inf
    gl = jnp.where(row < N_GROUPS, logits_t[:SUBLANES], neg)
    gmax = jnp.max(gl, axis=0, keepdims=True)
    gsel = jnp.min(jnp.where(gl == gmax, row, SUBLANES), axis=0, keepdims=True)
    p_grp = 1.0 / jnp.sum(jnp.exp(gl - gmax), axis=0, keepdims=True)
    el = logits_t[SUBLANES * N_GROUPS:SUBLANES * (N_GROUPS + 1)]
    for g in reversed(range(N_GROUPS - 1)):
        el = jnp.where(gsel == g, logits_t[SUBLANES * (g + 1):SUBLANES * (g + 2)], el)
    v0 = jnp.max(el, axis=0, keepdims=True)
    i0 = jnp.min(jnp.where(el == v0, row, SUBLANES), axis=0, keepdims=True)
    el1 = jnp.where(row == i0, neg, el)
    v1 = jnp.max(el1, axis=0, keepdims=True)
    i1 = jnp.min(jnp.where(el1 == v1, row, SUBLANES), axis=0, keepdims=True)
    t = jnp.exp(v1 - v0)
    w0 = p_grp / (1.0 + t)
    w1 = p_grp * t / (1.0 + t)
    return gsel * EXPERTS_PER_GROUP + i0, gsel * EXPERTS_PER_GROUP + i1, w0, w1


def _stack_rows(rows, n):
    tm = rows[0].shape[1]
    row = lax.broadcasted_iota(jnp.int32, (n, tm), 0)
    out = jnp.zeros((n, tm), F32)
    for k, r in enumerate(rows):
        out = jnp.where(row == k, r, out)
    return out


def _to_token_tiles(ref, val):
    n = val.shape[0]
    for s in range(ROW_TILES):
        ref[pl.ds(s, n, stride=ROW_TILES), :] = val[:, s * LANES:(s + 1) * LANES]


def _from_token_tiles(ref):
    n = ref.shape[0] // ROW_TILES
    return jnp.concatenate([ref[pl.ds(s, n, stride=ROW_TILES), :] for s in range(ROW_TILES)], axis=-1)


def _token_rows(ref, t):
    return ref.at[pl.ds(pl.multiple_of(t * ROW_TILES, ROW_TILES), ROW_TILES)]


def _merge_kernel(yr_ref, at_ref, gr_ref, gm_ref, x_ref, mod_ref, n2_ref, wro_ref, wmo_ref, wo_ref,
                  wrt_ref, brt_ref, h1_ref, u2_ref, meta_ref, wtok_ref, cnt_ref, carry_ref):
    tm = x_ref.shape[0]

    @pl.when(pl.program_id(0) == 0)
    def _():
        carry_ref[...] = jnp.zeros_like(carry_ref)

    chunks = range(MERGE_CHAINS)
    sub = tm // MERGE_CHAINS
    rows = [slice(c * sub, (c + 1) * sub) for c in chunks]
    y_ret = [_dot(yr_ref[r, :], wro_ref[...]) for r in rows]
    y_mla = [_dot(at_ref[r, :], wmo_ref[...]) for r in rows]
    merged = [(_sigmoid(gr_ref[rows[c], :].astype(F32)) * y_ret[c]
               + _sigmoid(gm_ref[rows[c], :].astype(F32)) * y_mla[c]).astype(BF16) for c in chunks]
    o = [_dot(merged[c], wo_ref[...]) for c in chunks]
    h1 = [x_ref[rows[c], :] + mod_ref[2:3, :] * o[c] for c in chunks]
    for c in chunks:
        h1_ref[rows[c], :] = h1[c]
    u2 = [_rms(h1[c]) * n2_ref[...] * (1.0 + mod_ref[4:5, :]) + mod_ref[3:4, :] for c in chunks]
    for c in chunks:
        _to_token_tiles(u2_ref.at[pl.ds(c * sub * ROW_TILES, sub * ROW_TILES)], u2[c])
    w = wrt_ref[...]
    w_hi = w.astype(BF16)
    w_lo = (w - w_hi.astype(F32)).astype(BF16)
    w_both = jnp.concatenate([w_hi, w_lo], axis=0)
    u_hi = [u2[c].astype(BF16) for c in chunks]
    u_lo = [(u2[c] - u_hi[c].astype(F32)).astype(BF16) for c in chunks]
    by_hi = [_dot_nt(w_both, u_hi[c]) for c in chunks]
    logits_t = [by_hi[c][:ROUTER_ROWS] + by_hi[c][ROUTER_ROWS:] + _dot_nt(w_hi, u_lo[c]) + brt_ref[:, 0:1]
                for c in chunks]
    routed = [_route(logits_t[c]) for c in chunks]
    e0, e1, w0, w1 = [jnp.concatenate([routed[c][k] for c in chunks], axis=1) for k in range(4)]
    erow = lax.broadcasted_iota(jnp.int32, (N_EXPERTS, tm), 0)
    m0, m1 = erow == e0, erow == e1
    member = jnp.where(m0 | m1, 1.0, 0.0)
    earlier = jnp.where(lax.broadcasted_iota(jnp.int32, (tm, tm), 0) < lax.broadcasted_iota(jnp.int32, (tm, tm), 1),
                        1.0, 0.0).astype(BF16)
    prefix = _dot(member.astype(BF16), earlier) + carry_ref[:, 0:1]
    rank0 = jnp.sum(jnp.where(m0, prefix, 0.0), axis=0, keepdims=True)
    rank1 = jnp.sum(jnp.where(m1, prefix, 0.0), axis=0, keepdims=True)
    carry_ref[...] = carry_ref[...] + jnp.sum(member, axis=1, keepdims=True)
    cnt_ref[...] = carry_ref[...]
    meta_ref[...] = _stack_rows([e0.astype(F32), e1.astype(F32), rank0, rank1], SUBLANES)
    wt = _stack_rows([w0, w1], 2 * SUBLANES)
    wt_hi = wt.astype(BF16)
    wt_lo = (wt - wt_hi.astype(F32)).astype(BF16)
    place = jnp.where(lax.broadcasted_iota(jnp.int32, (2 * SUBLANES, LANES), 0)
                      == lax.broadcasted_iota(jnp.int32, (2 * SUBLANES, LANES), 1), 1.0, 0.0).astype(BF16)
    wtok_ref[...] = _dot_tn(wt_hi, place) + _dot_tn(wt_lo, place)


def _merge(y_ret, attn, gates, x2, mod3, norm2, w_ret_o, w_mla_o, w_out, w_rt, b_rt):
    tm = TM_PROJ
    per_b = SEQ // tm
    row = lambda j: pl.BlockSpec((tm, D_MODEL), lambda i: (i, j))
    return pl.pallas_call(
        _merge_kernel,
        grid=(N_TOK // tm,),
        in_specs=[row(0), row(0), row(0), row(1), row(0),
                  pl.BlockSpec((None, 6, D_MODEL), lambda i: (i // per_b, 0, 0)),
                  _resident((1, D_MODEL)),
                  _resident(w_ret_o.shape), _resident(w_mla_o.shape), _resident(w_out.shape),
                  _resident(w_rt.shape), _resident(b_rt.shape)],
        out_specs=[row(0),
                   pl.BlockSpec((tm * ROW_TILES, LANES), lambda i: (i, 0)),
                   pl.BlockSpec((SUBLANES, tm), lambda i: (0, i)),
                   pl.BlockSpec((tm, LANES), lambda i: (i, 0)),
                   pl.BlockSpec((N_EXPERTS, LANES), lambda i: (0, 0))],
        out_shape=[jax.ShapeDtypeStruct((N_TOK, D_MODEL), F32),
                   jax.ShapeDtypeStruct((N_TOK * ROW_TILES, LANES), F32),
                   jax.ShapeDtypeStruct((SUBLANES, N_TOK), F32),
                   jax.ShapeDtypeStruct((N_TOK, LANES), F32),
                   jax.ShapeDtypeStruct((N_EXPERTS, LANES), F32)],
        scratch_shapes=[pltpu.VMEM((N_EXPERTS, LANES), F32)],
        compiler_params=_params("arbitrary"),
        name="merge",
    )(y_ret, attn, gates, gates, x2, mod3, norm2, w_ret_o, w_mla_o, w_out, w_rt, b_rt)


def _plan_kernel(meta_ref, off_ref, dst_ref):
    m = meta_ref[...]
    erow = lax.broadcasted_iota(jnp.int32, (N_EXPERTS, m.shape[1]), 0)
    off = off_ref[:, 0:1]
    d = [jnp.sum(jnp.where(erow == m[k:k + 1].astype(jnp.int32), off, 0.0), axis=0, keepdims=True) + m[k + 2:k + 3]
         for k in range(TOP_K)]
    dst_ref[...] = _stack_rows(d, SUBLANES).astype(jnp.int32)


def _plan(meta_t, off_col):
    tm = 2048
    return pl.pallas_call(
        _plan_kernel,
        grid=(N_TOK // tm,),
        in_specs=[pl.BlockSpec((SUBLANES, tm), lambda i: (0, i)),
                  pl.BlockSpec((N_EXPERTS, LANES), lambda i: (0, 0))],
        out_specs=pl.BlockSpec((SUBLANES, tm), lambda i: (0, i)),
        out_shape=jax.ShapeDtypeStruct((SUBLANES, N_TOK), jnp.int32),
        compiler_params=_params("arbitrary"),
        name="plan",
    )(meta_t, off_col)


def _row_copy_wait(src_like, dst_like, sem):
    pltpu.make_async_copy(src_like, dst_like, sem).wait()


def _dispatch_kernel(d0_ref, d1_ref, seg_ref, cnt_ref, u_ref, xs_ref, zero_ref, sem, zsem):
    i = pl.program_id(0)
    tm = u_ref.shape[0] // ROW_TILES
    tile_rows = TE * ROW_TILES

    def slot_tile(j):
        return xs_ref.at[pl.ds(pl.multiple_of(j * tile_rows, tile_rows), tile_rows)]

    @pl.when(i == 0)
    def _():
        zero_ref[...] = jnp.zeros_like(zero_ref)

        def pad(e):
            first = seg_ref[e] + cnt_ref[e]
            rows = (seg_ref[e + 1] - first) * ROW_TILES
            start = pl.multiple_of(first * ROW_TILES, ROW_TILES)
            return rows > 0, pltpu.make_async_copy(zero_ref.at[pl.ds(0, rows)], xs_ref.at[pl.ds(start, rows)], zsem)

        for e in range(N_EXPERTS):
            nonempty, copy = pad(e)
            pl.when(nonempty)(copy.start)
        for e in range(N_EXPERTS):
            nonempty, copy = pad(e)
            pl.when(nonempty)(copy.wait)

        def unused(j):
            return pltpu.make_async_copy(zero_ref, slot_tile(j), zsem)

        first_unused = seg_ref[N_EXPERTS] // TE
        lax.fori_loop(first_unused, N_TILES, lambda j, c: (unused(j).start(), c)[1], 0)
        lax.fori_loop(first_unused, N_TILES, lambda j, c: (unused(j).wait(), c)[1], 0)

    base = i * tm

    def body(r, carry):
        src = _token_rows(u_ref, r)
        pltpu.make_async_copy(src, _token_rows(xs_ref, d0_ref[base + r]), sem).start(priority=0)
        pltpu.make_async_copy(src, _token_rows(xs_ref, d1_ref[base + r]), sem).start(priority=1)
        return carry

    lax.fori_loop(0, tm, body, 0, unroll=8)
    _row_copy_wait(u_ref, xs_ref.at[pl.ds(0, tm * ROW_TILES)], sem)
    _row_copy_wait(u_ref, xs_ref.at[pl.ds(0, tm * ROW_TILES)], sem)


def _dispatch(d0, d1, seg, cnt, u2t):
    tm = TM_DISPATCH
    return pl.pallas_call(
        _dispatch_kernel,
        grid_spec=pltpu.PrefetchScalarGridSpec(
            num_scalar_prefetch=4,
            grid=(N_TOK // tm,),
            in_specs=[pl.BlockSpec((tm * ROW_TILES, LANES), lambda i, *_: (i, 0))],
            out_specs=pl.BlockSpec(memory_space=pl.ANY),
            scratch_shapes=[pltpu.VMEM((TE * ROW_TILES, LANES), F32),
                            pltpu.SemaphoreType.DMA(()), pltpu.SemaphoreType.DMA(())]),
        out_shape=jax.ShapeDtypeStruct((N_SLOTS * ROW_TILES, LANES), F32),
        compiler_params=_params("arbitrary"),
        name="dispatch",
    )(d0, d1, seg, cnt, u2t)


def _expert_kernel(te_ref, nv_ref, x_ref, w1_ref, w3_ref, w2_ref, y_ref, w1_s, w3_s, w2_s):
    j = pl.program_id(0)

    @pl.when(j < nv_ref[0])
    def _():
        @pl.when((j == 0) | (te_ref[j] != te_ref[jnp.maximum(j - 1, 0)]))
        def _():
            w1_s[...] = w1_ref[...].astype(BF16)
            w3_s[...] = w3_ref[...].astype(BF16)
            w2_s[...] = w2_ref[...].astype(BF16)

        sub = TE // TE_CHAINS * ROW_TILES
        part = lambda ref, c: ref.at[pl.ds(c * sub, sub)]
        chains = range(TE_CHAINS)
        x = [_from_token_tiles(part(x_ref, c)).astype(BF16) for c in chains]
        a = [_dot(x[c], w1_s[...]) for c in chains]
        b = [_dot(x[c], w3_s[...]) for c in chains]
        hid = [(a[c] * _sigmoid(a[c]) * b[c]).astype(BF16) for c in chains]
        for c in chains:
            _to_token_tiles(part(y_ref, c), _dot(hid[c], w2_s[...]))


def _experts(tile_expert, n_valid, xs, w1, w3, w2):
    tile = lambda j, te, nv: jnp.minimum(j, nv[0] - 1)
    wspec = lambda shape: pl.BlockSpec((None,) + shape, lambda j, te, nv: (te[tile(j, te, nv)], 0, 0))
    slots = pl.BlockSpec((TE * ROW_TILES, LANES), lambda j, te, nv: (tile(j, te, nv), 0))
    return pl.pallas_call(
        _expert_kernel,
        grid_spec=pltpu.PrefetchScalarGridSpec(
            num_scalar_prefetch=2,
            grid=(N_TILES,),
            in_specs=[slots, wspec((D_MODEL, D_EXPERT)), wspec((D_MODEL, D_EXPERT)), wspec((D_EXPERT, D_MODEL))],
            out_specs=slots,
            scratch_shapes=[pltpu.VMEM((D_MODEL, D_EXPERT), BF16), pltpu.VMEM((D_MODEL, D_EXPERT), BF16),
                            pltpu.VMEM((D_EXPERT, D_MODEL), BF16)]),
        out_shape=jax.ShapeDtypeStruct((N_SLOTS * ROW_TILES, LANES), F32),
        input_output_aliases={2: 0},
        compiler_params=_params("arbitrary"),
        name="experts",
    )(tile_expert, n_valid, xs, w1, w3, w2)


def _final_kernel(d0_ref, d1_ref, h1_ref, meta_ref, mod_ref, fn_ref, ys_ref, o_ref, ybuf, sem):
    i = pl.program_id(0)
    tm = h1_ref.shape[0]

    def gather(t):
        buf, s = ybuf.at[t % 2], sem.at[t % 2]
        base = t * tm

        def body(r, carry):
            pltpu.make_async_copy(_token_rows(ys_ref, d0_ref[base + r]), _token_rows(buf.at[0], r), s).start(priority=0)
            pltpu.make_async_copy(_token_rows(ys_ref, d1_ref[base + r]), _token_rows(buf.at[1], r), s).start(priority=1)
            return carry

        lax.fori_loop(0, tm, body, 0, unroll=8)

    @pl.when(i == 0)
    def _():
        gather(0)

    @pl.when(i + 1 < pl.num_programs(0))
    def _():
        gather(i + 1)

    buf = ybuf.at[i % 2]
    for k in range(TOP_K):
        _row_copy_wait(ys_ref.at[pl.ds(0, tm * ROW_TILES)], buf.at[k], sem.at[i % 2])
    m = meta_ref[...]
    moe = m[:, 0:1] * _from_token_tiles(buf.at[0]) + m[:, 1:2] * _from_token_tiles(buf.at[1])
    h2 = h1_ref[...] + mod_ref[5:6, :] * moe
    o_ref[...] = _rms(h2) * fn_ref[...]


def _final(d0, d1, h1, meta, mod3, final_norm, ys):
    tm = TM_FINAL
    per_b = SEQ // tm
    return pl.pallas_call(
        _final_kernel,
        grid_spec=pltpu.PrefetchScalarGridSpec(
            num_scalar_prefetch=2,
            grid=(N_TOK // tm,),
            in_specs=[pl.BlockSpec((tm, D_MODEL), lambda i, *_: (i, 0)),
                      pl.BlockSpec((tm, LANES), lambda i, *_: (i, 0)),
                      pl.BlockSpec((None, 6, D_MODEL), lambda i, *_: (i // per_b, 0, 0)),
                      pl.BlockSpec((1, D_MODEL), lambda i, *_: (0, 0)),
                      pl.BlockSpec(memory_space=pl.ANY)],
            out_specs=pl.BlockSpec((tm, D_MODEL), lambda i, *_: (i, 0)),
            scratch_shapes=[pltpu.VMEM((2, TOP_K, tm * ROW_TILES, LANES), F32), pltpu.SemaphoreType.DMA((2,))]),
        out_shape=jax.ShapeDtypeStruct((N_TOK, D_MODEL), F32),
        compiler_params=_params("arbitrary"),
        name="final",
    )(d0, d1, h1, meta, mod3, final_norm, ys)


def _slot_layout(counts):
    cnt = counts[:, 0].astype(jnp.int32)
    tile_end = jnp.cumsum((cnt + TE - 1) // TE)
    seg = jnp.concatenate([jnp.zeros((1,), jnp.int32), tile_end * TE])
    off_col = jnp.broadcast_to(seg[:-1].astype(F32)[:, None], (N_EXPERTS, LANES))
    tile_ids = jnp.arange(N_TILES, dtype=jnp.int32)
    tile_expert = jnp.sum((tile_end[None, :] <= tile_ids[:, None]).astype(jnp.int32), axis=1)
    tile_expert = jnp.minimum(tile_expert, N_EXPERTS - 1)
    return cnt, seg, off_col, tile_expert, tile_end[-1:]


def _rope_tables(dim):
    pos = np.arange(SEQ, dtype=np.float64)
    inv = ROPE_THETA ** (-np.arange(0, dim, 2, dtype=np.float64) / dim)
    ang = pos[:, None] * inv[None, :]
    return np.cos(ang).astype(np.float32), np.sin(ang).astype(np.float32)


def _decay_tables():
    c = RET_CHUNK
    log_gamma = np.log1p(-np.exp2(-5.0 - np.arange(RET_HEADS, dtype=np.float64)))
    idx = np.arange(c, dtype=np.float64)
    rel = idx[:, None] - idx[None, :]
    dec = np.where(rel[None] >= 0, np.exp(log_gamma[:, None, None] * np.maximum(rel, 0.0)[None]), 0.0)
    xi = np.exp(log_gamma[:, None] * (idx[None, :] + 1.0))[:, :, None]
    zeta = np.exp(log_gamma[:, None] * (c - 1.0 - idx[None, :]))[:, :, None]
    cd = np.exp(log_gamma * c)[:, None, None]
    return tuple(jnp.asarray(t.astype(np.float32)) for t in (dec, xi, zeta, cd))


def _rotate_half_cols(w):
    half = w.shape[-1] // 2
    return jnp.concatenate([-w[..., half:], w[..., :half]], axis=-1)


def kernel(x, c, w_ada, b_ada, norm1, norm2, w_in, w_ret_o, q_norm, kv_norm, w_uq, w_ukv, w_mla_o, w_out,
           w_grp, b_grp, w_exp, b_exp, w1, w3, w2, final_norm):
    assert x.shape == (BATCH, SEQ, D_MODEL) and w_ada.shape[0] == 1
    x2 = x.reshape(N_TOK, D_MODEL)

    w_ret, w_lat, w_gate = _wprep(jnp.transpose(w_in[0]))
    wq = w_uq[0].reshape(MLA_Q_LORA, MLA_HEADS, MLA_QK)
    wq = jnp.concatenate([wq, _rotate_half_cols(wq[..., MLA_NOPE:])], axis=-1)
    wq = wq.transpose(1, 0, 2).astype(BF16)
    wkv = w_ukv[0].reshape(MLA_KV_LORA, MLA_HEADS, MLA_NOPE + MLA_V).transpose(1, 0, 2).astype(BF16)
    gap = jnp.zeros((SUBLANES - N_GROUPS, D_MODEL), F32)
    tail = jnp.zeros((ROUTER_ROWS - SUBLANES - N_EXPERTS, D_MODEL), F32)
    w_rt = jnp.concatenate([w_grp[0].T, gap, w_exp[0].T, tail], axis=0)
    b_rt = jnp.concatenate([b_grp[0], gap[:, 0], b_exp[0], tail[:, 0]])
    b_rt = jnp.broadcast_to(b_rt[:, None], (ROUTER_ROWS, LANES))

    ret_cos, ret_sin = (jnp.asarray(t) for t in _rope_tables(RET_DK))
    mla_cos, mla_sin = (jnp.asarray(np.concatenate([t, t], axis=-1)) for t in _rope_tables(MLA_ROPE))
    dec, xi, zeta, cd = _decay_tables()

    mod3 = _ada(c, w_ada[0], b_ada[0]).reshape(BATCH, 6, D_MODEL)
    ret, lat, gates = _inproj(x2, mod3, norm1, ret_cos, ret_sin, mla_cos, mla_sin, q_norm, kv_norm,
                              w_ret, w_lat, w_gate)
    attn, y_ret = _mla(lat.reshape(BATCH, SEQ, MLA_LAT_W), wq, wkv, mla_cos, mla_sin,
                       ret.reshape(BATCH, SEQ, 4 * RET_W), dec, xi, zeta, cd)
    h1, u2t, meta_t, wtok, counts = _merge(y_ret.reshape(N_TOK, D_MODEL), attn.reshape(N_TOK, D_MODEL), gates, x2,
                                           mod3, norm2, w_ret_o[0].astype(BF16), w_mla_o[0].astype(BF16),
                                           w_out[0].astype(BF16), w_rt, b_rt)
    cnt, seg, off_col, tile_expert, n_valid = _slot_layout(counts)
    dst = _plan(meta_t, off_col)
    d0, d1 = dst[0], dst[1]
    xs = _dispatch(d0, d1, seg, cnt, u2t)
    e_shape = (N_EXPERTS, D_MODEL, D_EXPERT)
    ys = _experts(tile_expert, n_valid, xs, w1[0].reshape(e_shape), w3[0].reshape(e_shape),
                  w2[0].reshape(N_EXPERTS, D_EXPERT, D_MODEL))
    out = _final(d0, d1, h1, wtok, mod3, final_norm.reshape(1, D_MODEL), ys)
    return out.reshape(BATCH, SEQ, D_MODEL)
```

```python
import numpy as np
import jax
import jax.numpy as jnp
from jax import lax
from jax.experimental import pallas as pl
from jax.experimental.pallas import tpu as pltpu

D_MODEL = 1024
BATCH = 8
SEQ = 2048
N_TOK = BATCH * SEQ

RET_HEADS = 4
RET_DK = 256
RET_DV = 256
RET_CHUNK = 256
RET_W = RET_HEADS * RET_DK

MLA_HEADS = 8
MLA_NOPE = 128
MLA_ROPE = 64
MLA_V = 128
MLA_Q_LORA = 384
MLA_KV_LORA = 256
MLA_LAT_W = MLA_Q_LORA + MLA_KV_LORA + 2 * MLA_ROPE
MLA_QK = MLA_NOPE + MLA_ROPE
ROPE_THETA = 10000.0

N_GROUPS = 4
EXPERTS_PER_GROUP = 8
N_EXPERTS = N_GROUPS * EXPERTS_PER_GROUP
D_EXPERT = 256
EPS = 1e-6
LOG2_E = 1.4426950408889634

LANES = 128
SUBLANES = 8
ROUTER_ROWS = 48
ROW_TILES = D_MODEL // LANES
VMEM_LIMIT = 56 * 1024 * 1024

TM_PROJ = 512
TM_INPROJ = 512
TM_FINAL = 512
TM_DISPATCH = 4096
TQ = 256
MLA_HPS = 2
TE = 512
TE_CHAINS = 2
MERGE_CHAINS = 2
TOP_K = 2
N_TILES = N_TOK * TOP_K // TE + N_EXPERTS
N_SLOTS = N_TILES * TE

F32 = jnp.float32
BF16 = jnp.bfloat16


def _sigmoid(x):
    return 1.0 / (1.0 + jnp.exp(-x))


def _rms(x):
    return x * lax.rsqrt(jnp.mean(x * x, axis=-1, keepdims=True) + EPS)


def _dot(a, b):
    return jnp.dot(a, b, preferred_element_type=F32)


def _dot_nt(a, b):
    return lax.dot_general(a, b, (((1,), (1,)), ((), ())), preferred_element_type=F32)


def _dot_tn(a, b):
    return lax.dot_general(a, b, (((0,), (0,)), ((), ())), preferred_element_type=F32)


def _params(*sem):
    return pltpu.CompilerParams(dimension_semantics=sem, vmem_limit_bytes=VMEM_LIMIT)


def _resident(shape):
    nd = len(shape)
    return pl.BlockSpec(shape, lambda *_: (0,) * nd, pipeline_mode=pl.Buffered(1))


def _ada_kernel(c_ref, w_ref, b_ref, o_ref):
    c = c_ref[...]
    act = (c * _sigmoid(c)).astype(BF16)
    o_ref[...] = _dot(act, w_ref[...].astype(BF16)) + b_ref[...]


def _ada(c, w_ada, b_ada):
    n = w_ada.shape[1]
    tn = D_MODEL
    return pl.pallas_call(
        _ada_kernel,
        grid=(n // tn,),
        in_specs=[pl.BlockSpec((BATCH, D_MODEL), lambda j: (0, 0)),
                  pl.BlockSpec((D_MODEL, tn), lambda j: (0, j)),
                  pl.BlockSpec((1, tn), lambda j: (0, j))],
        out_specs=pl.BlockSpec((BATCH, tn), lambda j: (0, j)),
        out_shape=jax.ShapeDtypeStruct((BATCH, n), F32),
        compiler_params=_params("arbitrary"),
        name="ada",
    )(c, w_ada, b_ada.reshape(1, n))


O_LAT = 4 * RET_W
O_PE = O_LAT + MLA_Q_LORA + MLA_KV_LORA
O_GATE = O_PE + MLA_ROPE


N_IN = O_GATE + 2 * D_MODEL
WPREP_ROWS = 512


def _stage_weights(w_hbm, ret_ref, lat_ref, gate_ref, buf, sem):
    chunks = [(s, min(WPREP_ROWS, N_IN - s)) for s in range(0, N_IN, WPREP_ROWS)]

    def copy(i):
        s, n = chunks[i]
        return pltpu.make_async_copy(w_hbm.at[pl.ds(s, n)], buf.at[i % 2, pl.ds(0, n)], sem.at[i % 2])

    groups = [(0, RET_W, ret_ref, 0, 1.0), (RET_W, 2 * RET_W, ret_ref, RET_W, RET_DK ** -0.5),
              (2 * RET_W, O_LAT, ret_ref, 2 * RET_W, 1.0), (O_LAT, O_GATE, lat_ref, 0, 1.0),
              (O_GATE, N_IN, gate_ref, 0, 1.0)]
    half = MLA_ROPE // 2
    rot = [(O_PE + half, O_GATE, O_GATE - O_LAT, -1.0), (O_PE, O_PE + half, O_GATE - O_LAT + half, 1.0)]

    copy(0).start()
    for i, (s, n) in enumerate(chunks):
        if i + 1 < len(chunks):
            copy(i + 1).start()
        copy(i).wait()
        for lo, hi, dst, dst_lo, scale in groups + [(a, b, lat_ref, d, sc) for a, b, d, sc in rot]:
            a, b = max(lo, s), min(hi, s + n)
            if a < b:
                x = buf[i % 2, a - s:b - s, :]
                dst[dst_lo + a - lo:dst_lo + b - lo, :] = (x if scale == 1.0 else x * scale).astype(BF16)


def _inproj_kernel(x_ref, mod_ref, n1_ref, cos_ref, sin_ref, mcos_ref, msin_ref, qn_ref, kvn_ref, w_hbm,
                   ret_ref, lat_ref, gate_ref, wr_ref, wm_ref, wg_ref, stage_buf, stage_sem):
    @pl.when(pl.program_id(0) == 0)
    def _():
        _stage_weights(w_hbm, wr_ref, wm_ref, wg_ref, stage_buf, stage_sem)

    y = _rms(x_ref[...]) * n1_ref[...]
    u = (y * (1.0 + mod_ref[1:2, :]) + mod_ref[0:1, :]).astype(BF16)
    cos, sin = cos_ref[...], sin_ref[...]
    half = RET_DK // 2
    for n in range(0, 2 * RET_W, RET_DK):
        p = _dot_nt(u, wr_ref[n:n + RET_DK, :])
        x1, x2 = p[:, :half], p[:, half:]
        ret_ref[:, n:n + half] = (x1 * cos - x2 * sin).astype(BF16)
        ret_ref[:, n + half:n + RET_DK] = (x2 * cos + x1 * sin).astype(BF16)
    step = 512
    for n in range(2 * RET_W, 3 * RET_W, step):
        ret_ref[:, n:n + step] = _dot_nt(u, wr_ref[n:n + step, :]).astype(BF16)
    for n in range(3 * RET_W, 4 * RET_W, step):
        p = _dot_nt(u, wr_ref[n:n + step, :])
        ret_ref[:, n:n + step] = (p * _sigmoid(p)).astype(BF16)
    lat = _dot_nt(u, wm_ref[...])
    o_kv, o_pe, o_rot = MLA_Q_LORA, MLA_Q_LORA + MLA_KV_LORA, MLA_Q_LORA + MLA_KV_LORA + MLA_ROPE
    lat_ref[:, :o_kv] = (_rms(lat[:, :o_kv]) * qn_ref[...]).astype(BF16)
    lat_ref[:, o_kv:o_pe] = (_rms(lat[:, o_kv:o_pe]) * kvn_ref[...]).astype(BF16)
    lat_ref[:, o_pe:o_rot] = (lat[:, o_pe:o_rot] * mcos_ref[...] + lat[:, o_rot:] * msin_ref[...]).astype(BF16)
    lat_ref[:, o_rot:] = jnp.zeros((lat.shape[0], MLA_ROPE), BF16)
    for n in range(0, 2 * D_MODEL, step):
        gate_ref[:, n:n + step] = _dot_nt(u, wg_ref[n:n + step, :]).astype(BF16)


def _inproj(x2, mod3, norm1, cos, sin, mla_cos, mla_sin, q_norm, kv_norm, w_in_t):
    tm = TM_INPROJ
    per_b = SEQ // tm
    rope_tab = pl.BlockSpec((tm, RET_DK // 2), lambda i: (i % per_b, 0))
    mla_tab = pl.BlockSpec((tm, MLA_ROPE), lambda i: (i % per_b, 0))
    return pl.pallas_call(
        _inproj_kernel,
        grid=(N_TOK // tm,),
        in_specs=[pl.BlockSpec((tm, D_MODEL), lambda i: (i, 0)),
                  pl.BlockSpec((None, 6, D_MODEL), lambda i: (i // per_b, 0, 0)),
                  _resident((1, D_MODEL)), rope_tab, rope_tab, mla_tab, mla_tab,
                  _resident(q_norm.shape), _resident(kv_norm.shape),
                  pl.BlockSpec(memory_space=pl.ANY)],
        out_specs=[pl.BlockSpec((tm, 4 * RET_W), lambda i: (i, 0)),
                   pl.BlockSpec((tm, MLA_LAT_W), lambda i: (i, 0)),
                   pl.BlockSpec((tm, 2 * D_MODEL), lambda i: (i, 0))],
        out_shape=[jax.ShapeDtypeStruct((N_TOK, 4 * RET_W), BF16),
                   jax.ShapeDtypeStruct((N_TOK, MLA_LAT_W), BF16),
                   jax.ShapeDtypeStruct((N_TOK, 2 * D_MODEL), BF16)],
        scratch_shapes=[pltpu.VMEM((O_LAT, D_MODEL), BF16), pltpu.VMEM((MLA_LAT_W, D_MODEL), BF16),
                        pltpu.VMEM((2 * D_MODEL, D_MODEL), BF16),
                        pltpu.VMEM((2, WPREP_ROWS, D_MODEL), F32), pltpu.SemaphoreType.DMA((2,))],
        compiler_params=_params("arbitrary"),
        name="inproj",
    )(x2, mod3, norm1, cos, sin, mla_cos, mla_sin, q_norm, kv_norm, w_in_t)


def _mla_kernel(lat_ref, wq_ref, wkv_ref, cos_ref, sin_ref,
                rq_ref, rk_ref, rv_ref, rg_ref, dec_ref, xi_ref, zeta_ref, cd_ref, o_ref, yr_ref,
                q_s, k_s, v_s, state_ref):
    h = pl.program_id(1)
    o_kv, o_pe, o_rot = MLA_Q_LORA, MLA_Q_LORA + MLA_KV_LORA, MLA_Q_LORA + MLA_KV_LORA + MLA_ROPE
    cos, sin = cos_ref[...], sin_ref[...]

    scale = (MLA_QK ** -0.5) * LOG2_E
    for g in range(MLA_HPS):
        qf = _dot(lat_ref[:, :o_kv], wq_ref[g])
        q_s[g, :, :MLA_NOPE] = (qf[:, :MLA_NOPE] * scale).astype(BF16)
        q_pe = qf[:, MLA_NOPE:MLA_QK] * cos + qf[:, MLA_QK:] * sin
        q_s[g, :, MLA_NOPE:] = (q_pe * scale).astype(BF16)
        kvf = _dot(lat_ref[:, o_kv:o_pe], wkv_ref[g])
        k_s[g, :, :MLA_NOPE] = kvf[:, :MLA_NOPE].astype(BF16)
        k_s[g, :, MLA_NOPE:] = lat_ref[:, o_pe:o_rot]
        v_s[g, :, :MLA_V] = kvf[:, MLA_NOPE:].astype(BF16)
        v_s[g, :, MLA_V:] = jnp.ones((SEQ, MLA_V), BF16)

    causal = lax.broadcasted_iota(jnp.int32, (TQ, TQ), 0) >= lax.broadcasted_iota(jnp.int32, (TQ, TQ), 1)
    heads = range(MLA_HPS)
    n_blk = SEQ // TQ

    def scores(i):
        lo, hi = i * TQ, (i + 1) * TQ
        diag = [jnp.where(causal, _dot_nt(q_s[g, lo:hi, :], k_s[g, lo:hi, :]), -jnp.inf) for g in heads]
        past = [_dot_nt(q_s[g, lo:hi, :], k_s[g, :lo, :]) if i > 0 else None for g in heads]
        return diag, past

    state_ref[...] = jnp.zeros_like(state_ref)
    pending = scores(0)
    for i in range(n_blk):
        lo, hi = i * TQ, (i + 1) * TQ
        diag, past = pending
        if i + 1 < n_blk:
            pending = scores(i + 1)
        rq, rk, rv = rq_ref[lo:hi, :], rk_ref[lo:hi, :], rv_ref[lo:hi, :]
        r_scores = (_dot_nt(rq, rk) * dec_ref[h]).astype(BF16)
        carried = xi_ref[h] * _dot(rq, state_ref[...].astype(BF16))
        m = [jnp.max(diag[g], axis=-1, keepdims=True) for g in heads]
        if i > 0:
            m = [jnp.maximum(m[g], jnp.max(past[g], axis=-1, keepdims=True)) for g in heads]
        ry = _dot(r_scores, rv) + carried
        k_dec = (rk.astype(F32) * zeta_ref[h]).astype(BF16)
        state_ref[...] = state_ref[...] * cd_ref[h] + _dot_tn(k_dec, rv)
        acc = [_dot(jnp.exp2(diag[g] - m[g]).astype(BF16), v_s[g, lo:hi, :]) for g in heads]
        if i > 0:
            acc = [acc[g] + _dot(jnp.exp2(past[g] - m[g]).astype(BF16), v_s[g, :lo, :]) for g in heads]
        ryc = ry - jnp.mean(ry, axis=-1, keepdims=True)
        r_inv = lax.rsqrt(jnp.mean(ryc * ryc, axis=-1, keepdims=True) + EPS)
        yr_ref[lo:hi, :] = (rg_ref[lo:hi, :].astype(F32) * (ryc * r_inv)).astype(BF16)
        for g in heads:
            o_ref[lo:hi, g * MLA_V:(g + 1) * MLA_V] = (acc[g][:, :MLA_V] / acc[g][:, MLA_V:]).astype(BF16)


def _mla(lat3, wq, wkv, cos, sin, ret3, dec, xi, zeta, cd):
    hps = MLA_HPS
    assert MLA_HEADS // hps == RET_HEADS and TQ == RET_CHUNK
    ret_part = lambda part: pl.BlockSpec((None, SEQ, RET_DK), lambda b, h: (b, 0, part * RET_HEADS + h))
    whole = lambda a: pl.BlockSpec(a.shape, lambda b, h: (0,) * a.ndim)
    return pl.pallas_call(
        _mla_kernel,
        grid=(BATCH, MLA_HEADS // hps),
        in_specs=[pl.BlockSpec((None, SEQ, MLA_LAT_W), lambda b, h: (b, 0, 0)),
                  pl.BlockSpec((hps, MLA_Q_LORA, MLA_QK + MLA_ROPE), lambda b, h: (h, 0, 0)),
                  pl.BlockSpec((hps, MLA_KV_LORA, MLA_NOPE + MLA_V), lambda b, h: (h, 0, 0)),
                  pl.BlockSpec((SEQ, MLA_ROPE), lambda b, h: (0, 0)),
                  pl.BlockSpec((SEQ, MLA_ROPE), lambda b, h: (0, 0)),
                  ret_part(0), ret_part(1), ret_part(2), ret_part(3),
                  whole(dec), whole(xi), whole(zeta), whole(cd)],
        out_specs=[pl.BlockSpec((None, SEQ, hps * MLA_V), lambda b, h: (b, 0, h)),
                   pl.BlockSpec((None, SEQ, RET_DV), lambda b, h: (b, 0, h))],
        out_shape=[jax.ShapeDtypeStruct((BATCH, SEQ, MLA_HEADS * MLA_V), BF16),
                   jax.ShapeDtypeStruct((BATCH, SEQ, RET_HEADS * RET_DV), BF16)],
        scratch_shapes=[pltpu.VMEM((hps, SEQ, MLA_QK), BF16),
                        pltpu.VMEM((hps, SEQ, MLA_QK), BF16),
                        pltpu.VMEM((hps, SEQ, 2 * MLA_V), BF16),
                        pltpu.VMEM((RET_DK, RET_DV), F32)],
        compiler_params=_params("arbitrary", "arbitrary"),
        name="mla",
    )(lat3, wq, wkv, cos, sin, ret3, ret3, ret3, ret3, dec, xi, zeta, cd)


def _route(logits_t):
    tm = logits_t.shape[1]
    row = lax.broadcasted_iota(jnp.int32, (SUBLANES, tm), 0)
    neg = -jnp.inf
    gl = jnp.where(row < N_GROUPS, logits_t[:SUBLANES], neg)
    gmax = jnp.max(gl, axis=0, keepdims=True)
    gsel = jnp.min(jnp.where(gl == gmax, row, SUBLANES), axis=0, keepdims=True)
    p_grp = 1.0 / jnp.sum(jnp.exp(gl - gmax), axis=0, keepdims=True)
    el = logits_t[SUBLANES * N_GROUPS:SUBLANES * (N_GROUPS + 1)]
    for g in reversed(range(N_GROUPS - 1)):
        el = jnp.where(gsel == g, logits_t[SUBLANES * (g + 1):SUBLANES * (g + 2)], el)
    v0 = jnp.max(el, axis=0, keepdims=True)
    i0 = jnp.min(jnp.where(el == v0, row, SUBLANES), axis=0, keepdims=True)
    el1 = jnp.where(row == i0, neg, el)
    v1 = jnp.max(el1, axis=0, keepdims=True)
    i1 = jnp.min(jnp.where(el1 == v1, row, SUBLANES), axis=0, keepdims=True)
    t = jnp.exp(v1 - v0)
    w0 = p_grp / (1.0 + t)
    w1 = p_grp * t / (1.0 + t)
    return gsel * EXPERTS_PER_GROUP + i0, gsel * EXPERTS_PER_GROUP + i1, w0, w1


def _stack_rows(rows, n):
    tm = rows[0].shape[1]
    row = lax.broadcasted_iota(jnp.int32, (n, tm), 0)
    out = jnp.zeros((n, tm), F32)
    for k, r in enumerate(rows):
        out = jnp.where(row == k, r, out)
    return out


def _to_token_tiles(ref, val):
    n = val.shape[0]
    for s in range(ROW_TILES):
        ref[pl.ds(s, n, stride=ROW_TILES), :] = val[:, s * LANES:(s + 1) * LANES]


def _from_token_tiles(ref):
    n = ref.shape[0] // ROW_TILES
    return jnp.concatenate([ref[pl.ds(s, n, stride=ROW_TILES), :] for s in range(ROW_TILES)], axis=-1)


def _token_rows(ref, t):
    return ref.at[pl.ds(pl.multiple_of(t * ROW_TILES, ROW_TILES), ROW_TILES)]


def _merge_kernel(yr_ref, at_ref, gr_ref, gm_ref, x_ref, mod_ref, n2_ref, wro_ref, wmo_ref, wo_ref,
                  wrt_ref, brt_ref, h1_ref, u2_ref, meta_ref, wtok_ref, cnt_ref, carry_ref):
    tm = x_ref.shape[0]

    @pl.when(pl.program_id(0) == 0)
    def _():
        carry_ref[...] = jnp.zeros_like(carry_ref)

    chunks = range(MERGE_CHAINS)
    sub = tm // MERGE_CHAINS
    rows = [slice(c * sub, (c + 1) * sub) for c in chunks]
    y_ret = [_dot(yr_ref[r, :], wro_ref[...]) for r in rows]
    y_mla = [_dot(at_ref[r, :], wmo_ref[...]) for r in rows]
    merged = [(_sigmoid(gr_ref[rows[c], :].astype(F32)) * y_ret[c]
               + _sigmoid(gm_ref[rows[c], :].astype(F32)) * y_mla[c]).astype(BF16) for c in chunks]
    o = [_dot(merged[c], wo_ref[...]) for c in chunks]
    h1 = [x_ref[rows[c], :] + mod_ref[2:3, :] * o[c] for c in chunks]
    for c in chunks:
        h1_ref[rows[c], :] = h1[c]
    u2 = [_rms(h1[c]) * n2_ref[...] * (1.0 + mod_ref[4:5, :]) + mod_ref[3:4, :] for c in chunks]
    for c in chunks:
        _to_token_tiles(u2_ref.at[pl.ds(c * sub * ROW_TILES, sub * ROW_TILES)], u2[c])
    w = wrt_ref[...]
    w_hi = w.astype(BF16)
    w_lo = (w - w_hi.astype(F32)).astype(BF16)
    w_both = jnp.concatenate([w_hi, w_lo], axis=0)
    u_hi = [u2[c].astype(BF16) for c in chunks]
    u_lo = [(u2[c] - u_hi[c].astype(F32)).astype(BF16) for c in chunks]
    by_hi = [_dot_nt(w_both, u_hi[c]) for c in chunks]
    logits_t = [by_hi[c][:ROUTER_ROWS] + by_hi[c][ROUTER_ROWS:] + _dot_nt(w_hi, u_lo[c]) + brt_ref[:, 0:1]
                for c in chunks]
    routed = [_route(logits_t[c]) for c in chunks]
    e0, e1, w0, w1 = [jnp.concatenate([routed[c][k] for c in chunks], axis=1) for k in range(4)]
    erow = lax.broadcasted_iota(jnp.int32, (N_EXPERTS, tm), 0)
    m0, m1 = erow == e0, erow == e1
    member = jnp.where(m0 | m1, 1.0, 0.0)
    earlier = jnp.where(lax.broadcasted_iota(jnp.int32, (tm, tm), 0) < lax.broadcasted_iota(jnp.int32, (tm, tm), 1),
                        1.0, 0.0).astype(BF16)
    prefix = _dot(member.astype(BF16), earlier) + carry_ref[:, 0:1]
    rank0 = jnp.sum(jnp.where(m0, prefix, 0.0), axis=0, keepdims=True)
    rank1 = jnp.sum(jnp.where(m1, prefix, 0.0), axis=0, keepdims=True)
    carry_ref[...] = carry_ref[...] + jnp.sum(member, axis=1, keepdims=True)
    cnt_ref[...] = carry_ref[...]
    meta_ref[...] = _stack_rows([e0.astype(F32), e1.astype(F32), rank0, rank1], SUBLANES)
    wt = _stack_rows([w0, w1], 2 * SUBLANES)
    wt_hi = wt.astype(BF16)
    wt_lo = (wt - wt_hi.astype(F32)).astype(BF16)
    place = jnp.where(lax.broadcasted_iota(jnp.int32, (2 * SUBLANES, LANES), 0)
                      == lax.broadcasted_iota(jnp.int32, (2 * SUBLANES, LANES), 1), 1.0, 0.0).astype(BF16)
    wtok_ref[...] = _dot_tn(wt_hi, place) + _dot_tn(wt_lo, place)


def _merge(y_ret, attn, gates, x2, mod3, norm2, w_ret_o, w_mla_o, w_out, w_rt, b_rt):
    tm = TM_PROJ
    per_b = SEQ // tm
    row = lambda j: pl.BlockSpec((tm, D_MODEL), lambda i: (i, j))
    return pl.pallas_call(
        _merge_kernel,
        grid=(N_TOK // tm,),
        in_specs=[row(0), row(0), row(0), row(1), row(0),
                  pl.BlockSpec((None, 6, D_MODEL), lambda i: (i // per_b, 0, 0)),
                  _resident((1, D_MODEL)),
                  _resident(w_ret_o.shape), _resident(w_mla_o.shape), _resident(w_out.shape),
                  _resident(w_rt.shape), _resident(b_rt.shape)],
        out_specs=[row(0),
                   pl.BlockSpec((tm * ROW_TILES, LANES), lambda i: (i, 0)),
                   pl.BlockSpec((SUBLANES, tm), lambda i: (0, i)),
                   pl.BlockSpec((tm, LANES), lambda i: (i, 0)),
                   pl.BlockSpec((N_EXPERTS, LANES), lambda i: (0, 0))],
        out_shape=[jax.ShapeDtypeStruct((N_TOK, D_MODEL), F32),
                   jax.ShapeDtypeStruct((N_TOK * ROW_TILES, LANES), F32),
                   jax.ShapeDtypeStruct((SUBLANES, N_TOK), F32),
                   jax.ShapeDtypeStruct((N_TOK, LANES), F32),
                   jax.ShapeDtypeStruct((N_EXPERTS, LANES), F32)],
        scratch_shapes=[pltpu.VMEM((N_EXPERTS, LANES), F32)],
        compiler_params=_params("arbitrary"),
        name="merge",
    )(y_ret, attn, gates, gates, x2, mod3, norm2, w_ret_o, w_mla_o, w_out, w_rt, b_rt)


def _plan_kernel(meta_ref, off_ref, dst_ref):
    m = meta_ref[...]
    erow = lax.broadcasted_iota(jnp.int32, (N_EXPERTS, m.shape[1]), 0)
    off = off_ref[:, 0:1]
    d = [jnp.sum(jnp.where(erow == m[k:k + 1].astype(jnp.int32), off, 0.0), axis=0, keepdims=True) + m[k + 2:k + 3]
         for k in range(TOP_K)]
    dst_ref[...] = _stack_rows(d, SUBLANES).astype(jnp.int32)


def _plan(meta_t, off_col):
    tm = 2048
    return pl.pallas_call(
        _plan_kernel,
        grid=(N_TOK // tm,),
        in_specs=[pl.BlockSpec((SUBLANES, tm), lambda i: (0, i)),
                  pl.BlockSpec((N_EXPERTS, LANES), lambda i: (0, 0))],
        out_specs=pl.BlockSpec((SUBLANES, tm), lambda i: (0, i)),
        out_shape=jax.ShapeDtypeStruct((SUBLANES, N_TOK), jnp.int32),
        compiler_params=_params("arbitrary"),
        name="plan",
    )(meta_t, off_col)


def _row_copy_wait(src_like, dst_like, sem):
    pltpu.make_async_copy(src_like, dst_like, sem).wait()


def _dispatch_kernel(d0_ref, d1_ref, seg_ref, cnt_ref, u_ref, xs_ref, zero_ref, sem, zsem):
    i = pl.program_id(0)
    tm = u_ref.shape[0] // ROW_TILES
    tile_rows = TE * ROW_TILES

    def slot_tile(j):
        return xs_ref.at[pl.ds(pl.multiple_of(j * tile_rows, tile_rows), tile_rows)]

    @pl.when(i == 0)
    def _():
        zero_ref[...] = jnp.zeros_like(zero_ref)

        def pad(e):
            first = seg_ref[e] + cnt_ref[e]
            rows = (seg_ref[e + 1] - first) * ROW_TILES
            start = pl.multiple_of(first * ROW_TILES, ROW_TILES)
            return rows > 0, pltpu.make_async_copy(zero_ref.at[pl.ds(0, rows)], xs_ref.at[pl.ds(start, rows)], zsem)

        for e in range(N_EXPERTS):
            nonempty, copy = pad(e)
            pl.when(nonempty)(copy.start)
        for e in range(N_EXPERTS):
            nonempty, copy = pad(e)
            pl.when(nonempty)(copy.wait)

        def unused(j):
            return pltpu.make_async_copy(zero_ref, slot_tile(j), zsem)

        first_unused = seg_ref[N_EXPERTS] // TE
        lax.fori_loop(first_unused, N_TILES, lambda j, c: (unused(j).start(), c)[1], 0)
        lax.fori_loop(first_unused, N_TILES, lambda j, c: (unused(j).wait(), c)[1], 0)

    base = i * tm

    def body(r, carry):
        src = _token_rows(u_ref, r)
        pltpu.make_async_copy(src, _token_rows(xs_ref, d0_ref[base + r]), sem).start(priority=0)
        pltpu.make_async_copy(src, _token_rows(xs_ref, d1_ref[base + r]), sem).start(priority=1)
        return carry

    lax.fori_loop(0, tm, body, 0, unroll=8)
    _row_copy_wait(u_ref, xs_ref.at[pl.ds(0, tm * ROW_TILES)], sem)
    _row_copy_wait(u_ref, xs_ref.at[pl.ds(0, tm * ROW_TILES)], sem)


def _dispatch(d0, d1, seg, cnt, u2t):
    tm = TM_DISPATCH
    return pl.pallas_call(
        _dispatch_kernel,
        grid_spec=pltpu.PrefetchScalarGridSpec(
            num_scalar_prefetch=4,
            grid=(N_TOK // tm,),
            in_specs=[pl.BlockSpec((tm * ROW_TILES, LANES), lambda i, *_: (i, 0))],
            out_specs=pl.BlockSpec(memory_space=pl.ANY),
            scratch_shapes=[pltpu.VMEM((TE * ROW_TILES, LANES), F32),
                            pltpu.SemaphoreType.DMA(()), pltpu.SemaphoreType.DMA(())]),
        out_shape=jax.ShapeDtypeStruct((N_SLOTS * ROW_TILES, LANES), F32),
        compiler_params=_params("arbitrary"),
        name="dispatch",
    )(d0, d1, seg, cnt, u2t)


def _expert_kernel(te_ref, nv_ref, x_ref, w1_ref, w3_ref, w2_ref, y_ref, w1_s, w3_s, w2_s):
    j = pl.program_id(0)

    @pl.when(j < nv_ref[0])
    def _():
        @pl.when((j == 0) | (te_ref[j] != te_ref[jnp.maximum(j - 1, 0)]))
        def _():
            w1_s[...] = w1_ref[...].astype(BF16)
            w3_s[...] = w3_ref[...].astype(BF16)
            w2_s[...] = w2_ref[...].astype(BF16)

        sub = TE // TE_CHAINS * ROW_TILES
        part = lambda ref, c: ref.at[pl.ds(c * sub, sub)]
        chains = range(TE_CHAINS)
        x = [_from_token_tiles(part(x_ref, c)).astype(BF16) for c in chains]
        a = [_dot(x[c], w1_s[...]) for c in chains]
        b = [_dot(x[c], w3_s[...]) for c in chains]
        hid = [(a[c] * _sigmoid(a[c]) * b[c]).astype(BF16) for c in chains]
        for c in chains:
            _to_token_tiles(part(y_ref, c), _dot(hid[c], w2_s[...]))


def _experts(tile_expert, n_valid, xs, w1, w3, w2):
    tile = lambda j, te, nv: jnp.minimum(j, nv[0] - 1)
    wspec = lambda shape: pl.BlockSpec((None,) + shape, lambda j, te, nv: (te[tile(j, te, nv)], 0, 0))
    slots = pl.BlockSpec((TE * ROW_TILES, LANES), lambda j, te, nv: (tile(j, te, nv), 0))
    return pl.pallas_call(
        _expert_kernel,
        grid_spec=pltpu.PrefetchScalarGridSpec(
            num_scalar_prefetch=2,
            grid=(N_TILES,),
            in_specs=[slots, wspec((D_MODEL, D_EXPERT)), wspec((D_MODEL, D_EXPERT)), wspec((D_EXPERT, D_MODEL))],
            out_specs=slots,
            scratch_shapes=[pltpu.VMEM((D_MODEL, D_EXPERT), BF16), pltpu.VMEM((D_MODEL, D_EXPERT), BF16),
                            pltpu.VMEM((D_EXPERT, D_MODEL), BF16)]),
        out_shape=jax.ShapeDtypeStruct((N_SLOTS * ROW_TILES, LANES), F32),
        input_output_aliases={2: 0},
        compiler_params=_params("arbitrary"),
        name="experts",
    )(tile_expert, n_valid, xs, w1, w3, w2)


def _final_kernel(d0_ref, d1_ref, h1_ref, meta_ref, mod_ref, fn_ref, ys_ref, o_ref, ybuf, sem):
    i = pl.program_id(0)
    tm = h1_ref.shape[0]

    def gather(t):
        buf, s = ybuf.at[t % 2], sem.at[t % 2]
        base = t * tm

        def body(r, carry):
            pltpu.make_async_copy(_token_rows(ys_ref, d0_ref[base + r]), _token_rows(buf.at[0], r), s).start(priority=0)
            pltpu.make_async_copy(_token_rows(ys_ref, d1_ref[base + r]), _token_rows(buf.at[1], r), s).start(priority=1)
            return carry

        lax.fori_loop(0, tm, body, 0, unroll=8)

    @pl.when(i == 0)
    def _():
        gather(0)

    @pl.when(i + 1 < pl.num_programs(0))
    def _():
        gather(i + 1)

    buf = ybuf.at[i % 2]
    for k in range(TOP_K):
        _row_copy_wait(ys_ref.at[pl.ds(0, tm * ROW_TILES)], buf.at[k], sem.at[i % 2])
    m = meta_ref[...]
    moe = m[:, 0:1] * _from_token_tiles(buf.at[0]) + m[:, 1:2] * _from_token_tiles(buf.at[1])
    h2 = h1_ref[...] + mod_ref[5:6, :] * moe
    o_ref[...] = _rms(h2) * fn_ref[...]


def _final(d0, d1, h1, meta, mod3, final_norm, ys):
    tm = TM_FINAL
    per_b = SEQ // tm
    return pl.pallas_call(
        _final_kernel,
        grid_spec=pltpu.PrefetchScalarGridSpec(
            num_scalar_prefetch=2,
            grid=(N_TOK // tm,),
            in_specs=[pl.BlockSpec((tm, D_MODEL), lambda i, *_: (i, 0)),
                      pl.BlockSpec((tm, LANES), lambda i, *_: (i, 0)),
                      pl.BlockSpec((None, 6, D_MODEL), lambda i, *_: (i // per_b, 0, 0)),
                      pl.BlockSpec((1, D_MODEL), lambda i, *_: (0, 0)),
                      pl.BlockSpec(memory_space=pl.ANY)],
            out_specs=pl.BlockSpec((tm, D_MODEL), lambda i, *_: (i, 0)),
            scratch_shapes=[pltpu.VMEM((2, TOP_K, tm * ROW_TILES, LANES), F32), pltpu.SemaphoreType.DMA((2,))]),
        out_shape=jax.ShapeDtypeStruct((N_TOK, D_MODEL), F32),
        compiler_params=_params("arbitrary"),
        name="final",
    )(d0, d1, h1, meta, mod3, final_norm, ys)


def _slot_layout(counts):
    cnt = counts[:, 0].astype(jnp.int32)
    tile_end = jnp.cumsum((cnt + TE - 1) // TE)
    seg = jnp.concatenate([jnp.zeros((1,), jnp.int32), tile_end * TE])
    off_col = jnp.broadcast_to(seg[:-1].astype(F32)[:, None], (N_EXPERTS, LANES))
    tile_ids = jnp.arange(N_TILES, dtype=jnp.int32)
    tile_expert = jnp.sum((tile_end[None, :] <= tile_ids[:, None]).astype(jnp.int32), axis=1)
    tile_expert = jnp.minimum(tile_expert, N_EXPERTS - 1)
    return cnt, seg, off_col, tile_expert, tile_end[-1:]


def _rope_tables(dim):
    pos = np.arange(SEQ, dtype=np.float64)
    inv = ROPE_THETA ** (-np.arange(0, dim, 2, dtype=np.float64) / dim)
    ang = pos[:, None] * inv[None, :]
    return np.cos(ang).astype(np.float32), np.sin(ang).astype(np.float32)


def _decay_tables():
    c = RET_CHUNK
    log_gamma = np.log1p(-np.exp2(-5.0 - np.arange(RET_HEADS, dtype=np.float64)))
    idx = np.arange(c, dtype=np.float64)
    rel = idx[:, None] - idx[None, :]
    dec = np.where(rel[None] >= 0, np.exp(log_gamma[:, None, None] * np.maximum(rel, 0.0)[None]), 0.0)
    xi = np.exp(log_gamma[:, None] * (idx[None, :] + 1.0))[:, :, None]
    zeta = np.exp(log_gamma[:, None] * (c - 1.0 - idx[None, :]))[:, :, None]
    cd = np.exp(log_gamma * c)[:, None, None]
    return tuple(jnp.asarray(t.astype(np.float32)) for t in (dec, xi, zeta, cd))


def _rotate_half_cols(w):
    half = w.shape[-1] // 2
    return jnp.concatenate([-w[..., half:], w[..., :half]], axis=-1)


def kernel(x, c, w_ada, b_ada, norm1, norm2, w_in, w_ret_o, q_norm, kv_norm, w_uq, w_ukv, w_mla_o, w_out,
           w_grp, b_grp, w_exp, b_exp, w1, w3, w2, final_norm):
    assert x.shape == (BATCH, SEQ, D_MODEL) and w_ada.shape[0] == 1
    x2 = x.reshape(N_TOK, D_MODEL)

    w_in_t = jnp.transpose(w_in[0])
    wq = w_uq[0].reshape(MLA_Q_LORA, MLA_HEADS, MLA_QK)
    wq = jnp.concatenate([wq, _rotate_half_cols(wq[..., MLA_NOPE:])], axis=-1)
    wq = wq.transpose(1, 0, 2).astype(BF16)
    wkv = w_ukv[0].reshape(MLA_KV_LORA, MLA_HEADS, MLA_NOPE + MLA_V).transpose(1, 0, 2).astype(BF16)
    gap = jnp.zeros((SUBLANES - N_GROUPS, D_MODEL), F32)
    tail = jnp.zeros((ROUTER_ROWS - SUBLANES - N_EXPERTS, D_MODEL), F32)
    w_rt = jnp.concatenate([w_grp[0].T, gap, w_exp[0].T, tail], axis=0)
    b_rt = jnp.concatenate([b_grp[0], gap[:, 0], b_exp[0], tail[:, 0]])
    b_rt = jnp.broadcast_to(b_rt[:, None], (ROUTER_ROWS, LANES))

    ret_cos, ret_sin = (jnp.asarray(t) for t in _rope_tables(RET_DK))
    mla_cos, mla_sin = (jnp.asarray(np.concatenate([t, t], axis=-1)) for t in _rope_tables(MLA_ROPE))
    dec, xi, zeta, cd = _decay_tables()

    mod3 = _ada(c, w_ada[0], b_ada[0]).reshape(BATCH, 6, D_MODEL)
    ret, lat, gates = _inproj(x2, mod3, norm1, ret_cos, ret_sin, mla_cos, mla_sin, q_norm, kv_norm, w_in_t)
    attn, y_ret = _mla(lat.reshape(BATCH, SEQ, MLA_LAT_W), wq, wkv, mla_cos, mla_sin,
                       ret.reshape(BATCH, SEQ, 4 * RET_W), dec, xi, zeta, cd)
    h1, u2t, meta_t, wtok, counts = _merge(y_ret.reshape(N_TOK, D_MODEL), attn.reshape(N_TOK, D_MODEL), gates, x2,
                                           mod3, norm2, w_ret_o[0].astype(BF16), w_mla_o[0].astype(BF16),
                                           w_out[0].astype(BF16), w_rt, b_rt)
    cnt, seg, off_col, tile_expert, n_valid = _slot_layout(counts)
    dst = _plan(meta_t, off_col)
    d0, d1 = dst[0], dst[1]
    xs = _dispatch(d0, d1, seg, cnt, u2t)
    e_shape = (N_EXPERTS, D_MODEL, D_EXPERT)
    ys = _experts(tile_expert, n_valid, xs, w1[0].reshape(e_shape), w3[0].reshape(e_shape),
                  w2[0].reshape(N_EXPERTS, D_EXPERT, D_MODEL))
    out = _final(d0, d1, h1, wtok, mod3, final_norm.reshape(1, D_MODEL), ys)
    return out.reshape(BATCH, SEQ, D_MODEL)
```

```python
import numpy as np
import jax
import jax.numpy as jnp
from jax import lax
from jax.experimental import pallas as pl
from jax.experimental.pallas import tpu as pltpu

D_MODEL = 1024
BATCH = 8
SEQ = 2048
N_TOK = BATCH * SEQ

RET_HEADS = 4
RET_DK = 256
RET_DV = 256
RET_CHUNK = 256
RET_W = RET_HEADS * RET_DK

MLA_HEADS = 8
MLA_NOPE = 128
MLA_ROPE = 64
MLA_V = 128
MLA_Q_LORA = 384
MLA_KV_LORA = 256
MLA_LAT_W = MLA_Q_LORA + MLA_KV_LORA + 2 * MLA_ROPE
MLA_QK = MLA_NOPE + MLA_ROPE
ROPE_THETA = 10000.0

N_GROUPS = 4
EXPERTS_PER_GROUP = 8
N_EXPERTS = N_GROUPS * EXPERTS_PER_GROUP
D_EXPERT = 256
EPS = 1e-6
LOG2_E = 1.4426950408889634

LANES = 128
SUBLANES = 8
ROUTER_ROWS = 48
ROW_TILES = D_MODEL // LANES
VMEM_LIMIT = 56 * 1024 * 1024

TM_PROJ = 512
TM_INPROJ = 512
TM_FINAL = 512
TM_DISPATCH = 4096
TQ = 256
MLA_HPS = 2
TE = 512
TE_CHAINS = 2
MERGE_CHAINS = 4
TOP_K = 2
N_TILES = N_TOK * TOP_K // TE + N_EXPERTS
N_SLOTS = N_TILES * TE

F32 = jnp.float32
BF16 = jnp.bfloat16


def _sigmoid(x):
    return 1.0 / (1.0 + jnp.exp(-x))


def _rms(x):
    return x * lax.rsqrt(jnp.mean(x * x, axis=-1, keepdims=True) + EPS)


def _dot(a, b):
    return jnp.dot(a, b, preferred_element_type=F32)


def _dot_nt(a, b):
    return lax.dot_general(a, b, (((1,), (1,)), ((), ())), preferred_element_type=F32)


def _dot_tn(a, b):
    return lax.dot_general(a, b, (((0,), (0,)), ((), ())), preferred_element_type=F32)


def _params(*sem):
    return pltpu.CompilerParams(dimension_semantics=sem, vmem_limit_bytes=VMEM_LIMIT)


def _resident(shape):
    nd = len(shape)
    return pl.BlockSpec(shape, lambda *_: (0,) * nd, pipeline_mode=pl.Buffered(1))


def _layer_weight(w):
    return pl.BlockSpec((None,) + w.shape[1:], lambda *_: (0,) * w.ndim, pipeline_mode=pl.Buffered(1))


def _ada_kernel(c_ref, w_ref, b_ref, o_ref):
    c = c_ref[...]
    act = (c * _sigmoid(c)).astype(BF16)
    o_ref[...] = _dot(act, w_ref[...].astype(BF16)) + b_ref[...]


def _ada(c, w_ada, b_ada):
    n = w_ada.shape[1]
    tn = D_MODEL
    return pl.pallas_call(
        _ada_kernel,
        grid=(n // tn,),
        in_specs=[pl.BlockSpec((BATCH, D_MODEL), lambda j: (0, 0)),
                  pl.BlockSpec((D_MODEL, tn), lambda j: (0, j)),
                  pl.BlockSpec((1, tn), lambda j: (0, j))],
        out_specs=pl.BlockSpec((BATCH, tn), lambda j: (0, j)),
        out_shape=jax.ShapeDtypeStruct((BATCH, n), F32),
        compiler_params=_params("arbitrary"),
        name="ada",
    )(c, w_ada, b_ada.reshape(1, n))


O_LAT = 4 * RET_W
O_PE = O_LAT + MLA_Q_LORA + MLA_KV_LORA
O_GATE = O_PE + MLA_ROPE


N_IN = O_GATE + 2 * D_MODEL
WPREP_ROWS = 512


def _stage_weights(w_hbm, ret_ref, lat_ref, gate_ref, buf, sem):
    chunks = [(s, min(WPREP_ROWS, N_IN - s)) for s in range(0, N_IN, WPREP_ROWS)]

    def copy(i):
        s, n = chunks[i]
        return pltpu.make_async_copy(w_hbm.at[pl.ds(s, n)], buf.at[i % 2, pl.ds(0, n)], sem.at[i % 2])

    groups = [(0, RET_W, ret_ref, 0, 1.0), (RET_W, 2 * RET_W, ret_ref, RET_W, RET_DK ** -0.5),
              (2 * RET_W, O_LAT, ret_ref, 2 * RET_W, 1.0), (O_LAT, O_GATE, lat_ref, 0, 1.0),
              (O_GATE, N_IN, gate_ref, 0, 1.0)]
    half = MLA_ROPE // 2
    rot = [(O_PE + half, O_GATE, O_GATE - O_LAT, -1.0), (O_PE, O_PE + half, O_GATE - O_LAT + half, 1.0)]

    copy(0).start()
    for i, (s, n) in enumerate(chunks):
        if i + 1 < len(chunks):
            copy(i + 1).start()
        copy(i).wait()
        for lo, hi, dst, dst_lo, scale in groups + [(a, b, lat_ref, d, sc) for a, b, d, sc in rot]:
            a, b = max(lo, s), min(hi, s + n)
            if a < b:
                x = buf[i % 2, a - s:b - s, :]
                dst[dst_lo + a - lo:dst_lo + b - lo, :] = (x if scale == 1.0 else x * scale).astype(BF16)


def _inproj_kernel(x_ref, mod_ref, n1_ref, cos_ref, sin_ref, mcos_ref, msin_ref, qn_ref, kvn_ref, w_hbm,
                   ret_ref, lat_ref, gate_ref, wr_ref, wm_ref, wg_ref, stage_buf, stage_sem):
    @pl.when(pl.program_id(0) == 0)
    def _():
        _stage_weights(w_hbm, wr_ref, wm_ref, wg_ref, stage_buf, stage_sem)

    y = _rms(x_ref[...]) * n1_ref[...]
    u = (y * (1.0 + mod_ref[1:2, :]) + mod_ref[0:1, :]).astype(BF16)
    cos, sin = cos_ref[...], sin_ref[...]
    half = RET_DK // 2
    for n in range(0, 2 * RET_W, RET_DK):
        p = _dot_nt(u, wr_ref[n:n + RET_DK, :])
        x1, x2 = p[:, :half], p[:, half:]
        ret_ref[:, n:n + half] = (x1 * cos - x2 * sin).astype(BF16)
        ret_ref[:, n + half:n + RET_DK] = (x2 * cos + x1 * sin).astype(BF16)
    step = 512
    for n in range(2 * RET_W, 3 * RET_W, step):
        ret_ref[:, n:n + step] = _dot_nt(u, wr_ref[n:n + step, :]).astype(BF16)
    for n in range(3 * RET_W, 4 * RET_W, step):
        p = _dot_nt(u, wr_ref[n:n + step, :])
        ret_ref[:, n:n + step] = (p * _sigmoid(p)).astype(BF16)
    lat = _dot_nt(u, wm_ref[...])
    o_kv, o_pe, o_rot = MLA_Q_LORA, MLA_Q_LORA + MLA_KV_LORA, MLA_Q_LORA + MLA_KV_LORA + MLA_ROPE
    lat_ref[:, :o_kv] = (_rms(lat[:, :o_kv]) * qn_ref[...]).astype(BF16)
    lat_ref[:, o_kv:o_pe] = (_rms(lat[:, o_kv:o_pe]) * kvn_ref[...]).astype(BF16)
    lat_ref[:, o_pe:o_rot] = (lat[:, o_pe:o_rot] * mcos_ref[...] + lat[:, o_rot:] * msin_ref[...]).astype(BF16)
    lat_ref[:, o_rot:] = jnp.zeros((lat.shape[0], MLA_ROPE), BF16)
    for n in range(0, 2 * D_MODEL, step):
        gate_ref[:, n:n + step] = _dot_nt(u, wg_ref[n:n + step, :]).astype(BF16)


def _inproj(x2, mod3, norm1, cos, sin, mla_cos, mla_sin, q_norm, kv_norm, w_in_t):
    tm = TM_INPROJ
    per_b = SEQ // tm
    rope_tab = pl.BlockSpec((tm, RET_DK // 2), lambda i: (i % per_b, 0))
    mla_tab = pl.BlockSpec((tm, MLA_ROPE), lambda i: (i % per_b, 0))
    return pl.pallas_call(
        _inproj_kernel,
        grid=(N_TOK // tm,),
        in_specs=[pl.BlockSpec((tm, D_MODEL), lambda i: (i, 0)),
                  pl.BlockSpec((None, 6, D_MODEL), lambda i: (i // per_b, 0, 0)),
                  _resident((1, D_MODEL)), rope_tab, rope_tab, mla_tab, mla_tab,
                  _resident(q_norm.shape), _resident(kv_norm.shape),
                  pl.BlockSpec(memory_space=pl.ANY)],
        out_specs=[pl.BlockSpec((tm, 4 * RET_W), lambda i: (i, 0)),
                   pl.BlockSpec((tm, MLA_LAT_W), lambda i: (i, 0)),
                   pl.BlockSpec((tm, 2 * D_MODEL), lambda i: (i, 0))],
        out_shape=[jax.ShapeDtypeStruct((N_TOK, 4 * RET_W), BF16),
                   jax.ShapeDtypeStruct((N_TOK, MLA_LAT_W), BF16),
                   jax.ShapeDtypeStruct((N_TOK, 2 * D_MODEL), BF16)],
        scratch_shapes=[pltpu.VMEM((O_LAT, D_MODEL), BF16), pltpu.VMEM((MLA_LAT_W, D_MODEL), BF16),
                        pltpu.VMEM((2 * D_MODEL, D_MODEL), BF16),
                        pltpu.VMEM((2, WPREP_ROWS, D_MODEL), F32), pltpu.SemaphoreType.DMA((2,))],
        compiler_params=_params("arbitrary"),
        name="inproj",
    )(x2, mod3, norm1, cos, sin, mla_cos, mla_sin, q_norm, kv_norm, w_in_t)


def _mla_kernel(lat_ref, wq_ref, wkv_ref, cos_ref, sin_ref,
                rq_ref, rk_ref, rv_ref, rg_ref, dec_ref, xi_ref, zeta_ref, cd_ref, o_ref, yr_ref,
                q_s, k_s, v_s, state_ref):
    h = pl.program_id(1)
    o_kv, o_pe, o_rot = MLA_Q_LORA, MLA_Q_LORA + MLA_KV_LORA, MLA_Q_LORA + MLA_KV_LORA + MLA_ROPE
    cos, sin = cos_ref[...], sin_ref[...]

    scale = (MLA_QK ** -0.5) * LOG2_E
    for g in range(MLA_HPS):
        qf = _dot(lat_ref[:, :o_kv], wq_ref[g])
        q_s[g, :, :MLA_NOPE] = (qf[:, :MLA_NOPE] * scale).astype(BF16)
        q_pe = qf[:, MLA_NOPE:MLA_QK] * cos + qf[:, MLA_QK:] * sin
        q_s[g, :, MLA_NOPE:] = (q_pe * scale).astype(BF16)
        kvf = _dot(lat_ref[:, o_kv:o_pe], wkv_ref[g])
        k_s[g, :, :MLA_NOPE] = kvf[:, :MLA_NOPE].astype(BF16)
        k_s[g, :, MLA_NOPE:] = lat_ref[:, o_pe:o_rot]
        v_s[g, :, :MLA_V] = kvf[:, MLA_NOPE:].astype(BF16)
        v_s[g, :, MLA_V:] = jnp.ones((SEQ, MLA_V), BF16)

    causal = lax.broadcasted_iota(jnp.int32, (TQ, TQ), 0) >= lax.broadcasted_iota(jnp.int32, (TQ, TQ), 1)
    heads = range(MLA_HPS)
    n_blk = SEQ // TQ

    def scores(i):
        lo, hi = i * TQ, (i + 1) * TQ
        diag = [jnp.where(causal, _dot_nt(q_s[g, lo:hi, :], k_s[g, lo:hi, :]), -jnp.inf) for g in heads]
        past = [_dot_nt(q_s[g, lo:hi, :], k_s[g, :lo, :]) if i > 0 else None for g in heads]
        return diag, past

    state_ref[...] = jnp.zeros_like(state_ref)
    pending = scores(0)
    for i in range(n_blk):
        lo, hi = i * TQ, (i + 1) * TQ
        diag, past = pending
        if i + 1 < n_blk:
            pending = scores(i + 1)
        rq, rk, rv = rq_ref[lo:hi, :], rk_ref[lo:hi, :], rv_ref[lo:hi, :]
        r_scores = (_dot_nt(rq, rk) * dec_ref[h]).astype(BF16)
        carried = xi_ref[h] * _dot(rq, state_ref[...].astype(BF16))
        m = [jnp.max(diag[g], axis=-1, keepdims=True) for g in heads]
        if i > 0:
            m = [jnp.maximum(m[g], jnp.max(past[g], axis=-1, keepdims=True)) for g in heads]
        ry = _dot(r_scores, rv) + carried
        k_dec = (rk.astype(F32) * zeta_ref[h]).astype(BF16)
        state_ref[...] = state_ref[...] * cd_ref[h] + _dot_tn(k_dec, rv)
        acc = [_dot(jnp.exp2(diag[g] - m[g]).astype(BF16), v_s[g, lo:hi, :]) for g in heads]
        if i > 0:
            acc = [acc[g] + _dot(jnp.exp2(past[g] - m[g]).astype(BF16), v_s[g, :lo, :]) for g in heads]
        ryc = ry - jnp.mean(ry, axis=-1, keepdims=True)
        r_inv = lax.rsqrt(jnp.mean(ryc * ryc, axis=-1, keepdims=True) + EPS)
        yr_ref[lo:hi, :] = (rg_ref[lo:hi, :].astype(F32) * (ryc * r_inv)).astype(BF16)
        for g in heads:
            o_ref[lo:hi, g * MLA_V:(g + 1) * MLA_V] = (acc[g][:, :MLA_V] / acc[g][:, MLA_V:]).astype(BF16)


def _mla(lat3, wq, wkv, cos, sin, ret3, dec, xi, zeta, cd):
    hps = MLA_HPS
    assert MLA_HEADS // hps == RET_HEADS and TQ == RET_CHUNK
    ret_part = lambda part: pl.BlockSpec((None, SEQ, RET_DK), lambda b, h: (b, 0, part * RET_HEADS + h))
    whole = lambda a: pl.BlockSpec(a.shape, lambda b, h: (0,) * a.ndim)
    return pl.pallas_call(
        _mla_kernel,
        grid=(BATCH, MLA_HEADS // hps),
        in_specs=[pl.BlockSpec((None, SEQ, MLA_LAT_W), lambda b, h: (b, 0, 0)),
                  pl.BlockSpec((hps, MLA_Q_LORA, MLA_QK + MLA_ROPE), lambda b, h: (h, 0, 0)),
                  pl.BlockSpec((hps, MLA_KV_LORA, MLA_NOPE + MLA_V), lambda b, h: (h, 0, 0)),
                  pl.BlockSpec((SEQ, MLA_ROPE), lambda b, h: (0, 0)),
                  pl.BlockSpec((SEQ, MLA_ROPE), lambda b, h: (0, 0)),
                  ret_part(0), ret_part(1), ret_part(2), ret_part(3),
                  whole(dec), whole(xi), whole(zeta), whole(cd)],
        out_specs=[pl.BlockSpec((None, SEQ, hps * MLA_V), lambda b, h: (b, 0, h)),
                   pl.BlockSpec((None, SEQ, RET_DV), lambda b, h: (b, 0, h))],
        out_shape=[jax.ShapeDtypeStruct((BATCH, SEQ, MLA_HEADS * MLA_V), BF16),
                   jax.ShapeDtypeStruct((BATCH, SEQ, RET_HEADS * RET_DV), BF16)],
        scratch_shapes=[pltpu.VMEM((hps, SEQ, MLA_QK), BF16),
                        pltpu.VMEM((hps, SEQ, MLA_QK), BF16),
                        pltpu.VMEM((hps, SEQ, 2 * MLA_V), BF16),
                        pltpu.VMEM((RET_DK, RET_DV), F32)],
        compiler_params=_params("arbitrary", "arbitrary"),
        name="mla",
    )(lat3, wq, wkv, cos, sin, ret3, ret3, ret3, ret3, dec, xi, zeta, cd)


def _route(logits_t):
    tm = logits_t.shape[1]
    row = lax.broadcasted_iota(jnp.int32, (SUBLANES, tm), 0)
    neg = -jnp.inf
    gl = jnp.where(row < N_GROUPS, logits_t[:SUBLANES], neg)
    gmax = jnp.max(gl, axis=0, keepdims=True)
    gsel = jnp.min(jnp.where(gl == gmax, row, SUBLANES), axis=0, keepdims=True)
    p_grp = 1.0 / jnp.sum(jnp.exp(gl - gmax), axis=0, keepdims=True)
    el = logits_t[SUBLANES * N_GROUPS:SUBLANES * (N_GROUPS + 1)]
    for g in reversed(range(N_GROUPS - 1)):
        el = jnp.where(gsel == g, logits_t[SUBLANES * (g + 1):SUBLANES * (g + 2)], el)
    v0 = jnp.max(el, axis=0, keepdims=True)
    i0 = jnp.min(jnp.where(el == v0, row, SUBLANES), axis=0, keepdims=True)
    el1 = jnp.where(row == i0, neg, el)
    v1 = jnp.max(el1, axis=0, keepdims=True)
    i1 = jnp.min(jnp.where(el1 == v1, row, SUBLANES), axis=0, keepdims=True)
    t = jnp.exp(v1 - v0)
    w0 = p_grp / (1.0 + t)
    w1 = p_grp * t / (1.0 + t)
    return gsel * EXPERTS_PER_GROUP + i0, gsel * EXPERTS_PER_GROUP + i1, w0, w1


def _stack_rows(rows, n):
    tm = rows[0].shape[1]
    row = lax.broadcasted_iota(jnp.int32, (n, tm), 0)
    out = jnp.zeros((n, tm), F32)
    for k, r in enumerate(rows):
        out = jnp.where(row == k, r, out)
    return out


def _to_token_tiles(ref, val):
    n = val.shape[0]
    for s in range(ROW_TILES):
        ref[pl.ds(s, n, stride=ROW_TILES), :] = val[:, s * LANES:(s + 1) * LANES]


def _from_token_tiles(ref):
    n = ref.shape[0] // ROW_TILES
    return jnp.concatenate([ref[pl.ds(s, n, stride=ROW_TILES), :] for s in range(ROW_TILES)], axis=-1)


def _token_rows(ref, t):
    return ref.at[pl.ds(pl.multiple_of(t * ROW_TILES, ROW_TILES), ROW_TILES)]


def _merge_kernel(yr_ref, at_ref, gr_ref, gm_ref, x_ref, mod_ref, n2_ref, wro_ref, wmo_ref, wo_ref,
                  wrt_ref, brt_ref, h1_ref, u2_ref, meta_ref, wtok_ref, cnt_ref, carry_ref, wro_s, wmo_s, wo_s):
    tm = x_ref.shape[0]

    @pl.when(pl.program_id(0) == 0)
    def _():
        carry_ref[...] = jnp.zeros_like(carry_ref)
        wro_s[...] = wro_ref[...].astype(BF16)
        wmo_s[...] = wmo_ref[...].astype(BF16)
        wo_s[...] = wo_ref[...].astype(BF16)

    chunks = range(MERGE_CHAINS)
    sub = tm // MERGE_CHAINS
    rows = [slice(c * sub, (c + 1) * sub) for c in chunks]
    y_ret = [_dot(yr_ref[r, :], wro_s[...]) for r in rows]
    y_mla = [_dot(at_ref[r, :], wmo_s[...]) for r in rows]
    merged = [(_sigmoid(gr_ref[rows[c], :].astype(F32)) * y_ret[c]
               + _sigmoid(gm_ref[rows[c], :].astype(F32)) * y_mla[c]).astype(BF16) for c in chunks]
    o = [_dot(merged[c], wo_s[...]) for c in chunks]
    h1 = [x_ref[rows[c], :] + mod_ref[2:3, :] * o[c] for c in chunks]
    for c in chunks:
        h1_ref[rows[c], :] = h1[c]
    u2 = [_rms(h1[c]) * n2_ref[...] * (1.0 + mod_ref[4:5, :]) + mod_ref[3:4, :] for c in chunks]
    for c in chunks:
        _to_token_tiles(u2_ref.at[pl.ds(c * sub * ROW_TILES, sub * ROW_TILES)], u2[c])
    w = wrt_ref[...]
    w_hi = w.astype(BF16)
    w_lo = (w - w_hi.astype(F32)).astype(BF16)
    w_both = jnp.concatenate([w_hi, w_lo], axis=0)
    u_hi = [u2[c].astype(BF16) for c in chunks]
    u_lo = [(u2[c] - u_hi[c].astype(F32)).astype(BF16) for c in chunks]
    by_hi = [_dot_nt(w_both, u_hi[c]) for c in chunks]
    logits_t = [by_hi[c][:ROUTER_ROWS] + by_hi[c][ROUTER_ROWS:] + _dot_nt(w_hi, u_lo[c]) + brt_ref[:, 0:1]
                for c in chunks]
    routed = [_route(logits_t[c]) for c in chunks]
    e0, e1, w0, w1 = [jnp.concatenate([routed[c][k] for c in chunks], axis=1) for k in range(4)]
    erow = lax.broadcasted_iota(jnp.int32, (N_EXPERTS, tm), 0)
    m0, m1 = erow == e0, erow == e1
    member = jnp.where(m0 | m1, 1.0, 0.0)
    earlier = jnp.where(lax.broadcasted_iota(jnp.int32, (tm, tm), 0) < lax.broadcasted_iota(jnp.int32, (tm, tm), 1),
                        1.0, 0.0).astype(BF16)
    prefix = _dot(member.astype(BF16), earlier) + carry_ref[:, 0:1]
    rank0 = jnp.sum(jnp.where(m0, prefix, 0.0), axis=0, keepdims=True)
    rank1 = jnp.sum(jnp.where(m1, prefix, 0.0), axis=0, keepdims=True)
    carry_ref[...] = carry_ref[...] + jnp.sum(member, axis=1, keepdims=True)
    cnt_ref[...] = carry_ref[...]
    meta_ref[...] = _stack_rows([e0.astype(F32), e1.astype(F32), rank0, rank1], SUBLANES)
    wt = _stack_rows([w0, w1], 2 * SUBLANES)
    wt_hi = wt.astype(BF16)
    wt_lo = (wt - wt_hi.astype(F32)).astype(BF16)
    place = jnp.where(lax.broadcasted_iota(jnp.int32, (2 * SUBLANES, LANES), 0)
                      == lax.broadcasted_iota(jnp.int32, (2 * SUBLANES, LANES), 1), 1.0, 0.0).astype(BF16)
    wtok_ref[...] = _dot_tn(wt_hi, place) + _dot_tn(wt_lo, place)


def _merge(y_ret, attn, gates, x2, mod3, norm2, w_ret_o, w_mla_o, w_out, w_rt, b_rt):
    tm = TM_PROJ
    per_b = SEQ // tm
    row = lambda j: pl.BlockSpec((tm, D_MODEL), lambda i: (i, j))
    return pl.pallas_call(
        _merge_kernel,
        grid=(N_TOK // tm,),
        in_specs=[row(0), row(0), row(0), row(1), row(0),
                  pl.BlockSpec((None, 6, D_MODEL), lambda i: (i // per_b, 0, 0)),
                  _resident((1, D_MODEL)),
                  _layer_weight(w_ret_o), _layer_weight(w_mla_o), _layer_weight(w_out),
                  _resident(w_rt.shape), _resident(b_rt.shape)],
        out_specs=[row(0),
                   pl.BlockSpec((tm * ROW_TILES, LANES), lambda i: (i, 0)),
                   pl.BlockSpec((SUBLANES, tm), lambda i: (0, i)),
                   pl.BlockSpec((tm, LANES), lambda i: (i, 0)),
                   pl.BlockSpec((N_EXPERTS, LANES), lambda i: (0, 0))],
        out_shape=[jax.ShapeDtypeStruct((N_TOK, D_MODEL), F32),
                   jax.ShapeDtypeStruct((N_TOK * ROW_TILES, LANES), F32),
                   jax.ShapeDtypeStruct((SUBLANES, N_TOK), F32),
                   jax.ShapeDtypeStruct((N_TOK, LANES), F32),
                   jax.ShapeDtypeStruct((N_EXPERTS, LANES), F32)],
        scratch_shapes=[pltpu.VMEM((N_EXPERTS, LANES), F32)] + [pltpu.VMEM((D_MODEL, D_MODEL), BF16)] * 3,
        compiler_params=_params("arbitrary"),
        name="merge",
    )(y_ret, attn, gates, gates, x2, mod3, norm2, w_ret_o, w_mla_o, w_out, w_rt, b_rt)


def _plan_kernel(meta_ref, off_ref, dst_ref):
    m = meta_ref[...]
    erow = lax.broadcasted_iota(jnp.int32, (N_EXPERTS, m.shape[1]), 0)
    off = off_ref[:, 0:1]
    d = [jnp.sum(jnp.where(erow == m[k:k + 1].astype(jnp.int32), off, 0.0), axis=0, keepdims=True) + m[k + 2:k + 3]
         for k in range(TOP_K)]
    dst_ref[...] = _stack_rows(d, SUBLANES).astype(jnp.int32)


def _plan(meta_t, off_col):
    tm = 2048
    return pl.pallas_call(
        _plan_kernel,
        grid=(N_TOK // tm,),
        in_specs=[pl.BlockSpec((SUBLANES, tm), lambda i: (0, i)),
                  pl.BlockSpec((N_EXPERTS, LANES), lambda i: (0, 0))],
        out_specs=pl.BlockSpec((SUBLANES, tm), lambda i: (0, i)),
        out_shape=jax.ShapeDtypeStruct((SUBLANES, N_TOK), jnp.int32),
        compiler_params=_params("arbitrary"),
        name="plan",
    )(meta_t, off_col)


def _row_copy_wait(src_like, dst_like, sem):
    pltpu.make_async_copy(src_like, dst_like, sem).wait()


def _dispatch_kernel(d0_ref, d1_ref, seg_ref, cnt_ref, u_ref, xs_ref, zero_ref, sem, zsem):
    i = pl.program_id(0)
    tm = u_ref.shape[0] // ROW_TILES
    tile_rows = TE * ROW_TILES

    def slot_tile(j):
        return xs_ref.at[pl.ds(pl.multiple_of(j * tile_rows, tile_rows), tile_rows)]

    @pl.when(i == 0)
    def _():
        zero_ref[...] = jnp.zeros_like(zero_ref)

        def pad(e):
            first = seg_ref[e] + cnt_ref[e]
            rows = (seg_ref[e + 1] - first) * ROW_TILES
            start = pl.multiple_of(first * ROW_TILES, ROW_TILES)
            return rows > 0, pltpu.make_async_copy(zero_ref.at[pl.ds(0, rows)], xs_ref.at[pl.ds(start, rows)], zsem)

        for e in range(N_EXPERTS):
            nonempty, copy = pad(e)
            pl.when(nonempty)(copy.start)
        for e in range(N_EXPERTS):
            nonempty, copy = pad(e)
            pl.when(nonempty)(copy.wait)

        def unused(j):
            return pltpu.make_async_copy(zero_ref, slot_tile(j), zsem)

        first_unused = seg_ref[N_EXPERTS] // TE
        lax.fori_loop(first_unused, N_TILES, lambda j, c: (unused(j).start(), c)[1], 0)
        lax.fori_loop(first_unused, N_TILES, lambda j, c: (unused(j).wait(), c)[1], 0)

    base = i * tm

    def body(r, carry):
        src = _token_rows(u_ref, r)
        pltpu.make_async_copy(src, _token_rows(xs_ref, d0_ref[base + r]), sem).start(priority=0)
        pltpu.make_async_copy(src, _token_rows(xs_ref, d1_ref[base + r]), sem).start(priority=1)
        return carry

    lax.fori_loop(0, tm, body, 0, unroll=8)
    _row_copy_wait(u_ref, xs_ref.at[pl.ds(0, tm * ROW_TILES)], sem)
    _row_copy_wait(u_ref, xs_ref.at[pl.ds(0, tm * ROW_TILES)], sem)


def _dispatch(d0, d1, seg, cnt, u2t):
    tm = TM_DISPATCH
    return pl.pallas_call(
        _dispatch_kernel,
        grid_spec=pltpu.PrefetchScalarGridSpec(
            num_scalar_prefetch=4,
            grid=(N_TOK // tm,),
            in_specs=[pl.BlockSpec((tm * ROW_TILES, LANES), lambda i, *_: (i, 0))],
            out_specs=pl.BlockSpec(memory_space=pl.ANY),
            scratch_shapes=[pltpu.VMEM((TE * ROW_TILES, LANES), F32),
                            pltpu.SemaphoreType.DMA(()), pltpu.SemaphoreType.DMA(())]),
        out_shape=jax.ShapeDtypeStruct((N_SLOTS * ROW_TILES, LANES), F32),
        compiler_params=_params("arbitrary"),
        name="dispatch",
    )(d0, d1, seg, cnt, u2t)


def _expert_kernel(te_ref, nv_ref, x_ref, w1_ref, w3_ref, w2_ref, y_ref, w1_s, w3_s, w2_s):
    j = pl.program_id(0)

    @pl.when(j < nv_ref[0])
    def _():
        @pl.when((j == 0) | (te_ref[j] != te_ref[jnp.maximum(j - 1, 0)]))
        def _():
            w1_s[...] = w1_ref[...].astype(BF16)
            w3_s[...] = w3_ref[...].astype(BF16)
            w2_s[...] = w2_ref[...].astype(BF16)

        sub = TE // TE_CHAINS * ROW_TILES
        part = lambda ref, c: ref.at[pl.ds(c * sub, sub)]
        chains = range(TE_CHAINS)
        x = [_from_token_tiles(part(x_ref, c)).astype(BF16) for c in chains]
        a = [_dot(x[c], w1_s[...]) for c in chains]
        b = [_dot(x[c], w3_s[...]) for c in chains]
        hid = [(a[c] * _sigmoid(a[c]) * b[c]).astype(BF16) for c in chains]
        for c in chains:
            _to_token_tiles(part(y_ref, c), _dot(hid[c], w2_s[...]))


def _experts(tile_expert, n_valid, xs, w1, w3, w2):
    tile = lambda j, te, nv: jnp.minimum(j, nv[0] - 1)
    wspec = lambda shape: pl.BlockSpec((None,) + shape, lambda j, te, nv: (te[tile(j, te, nv)], 0, 0))
    slots = pl.BlockSpec((TE * ROW_TILES, LANES), lambda j, te, nv: (tile(j, te, nv), 0))
    return pl.pallas_call(
        _expert_kernel,
        grid_spec=pltpu.PrefetchScalarGridSpec(
            num_scalar_prefetch=2,
            grid=(N_TILES,),
            in_specs=[slots, wspec((D_MODEL, D_EXPERT)), wspec((D_MODEL, D_EXPERT)), wspec((D_EXPERT, D_MODEL))],
            out_specs=slots,
            scratch_shapes=[pltpu.VMEM((D_MODEL, D_EXPERT), BF16), pltpu.VMEM((D_MODEL, D_EXPERT), BF16),
                            pltpu.VMEM((D_EXPERT, D_MODEL), BF16)]),
        out_shape=jax.ShapeDtypeStruct((N_SLOTS * ROW_TILES, LANES), F32),
        input_output_aliases={2: 0},
        compiler_params=_params("arbitrary"),
        name="experts",
    )(tile_expert, n_valid, xs, w1, w3, w2)


def _final_kernel(d0_ref, d1_ref, h1_ref, meta_ref, mod_ref, fn_ref, ys_ref, o_ref, ybuf, sem):
    i = pl.program_id(0)
    tm = h1_ref.shape[0]

    def gather(t):
        buf, s = ybuf.at[t % 2], sem.at[t % 2]
        base = t * tm

        def body(r, carry):
            pltpu.make_async_copy(_token_rows(ys_ref, d0_ref[base + r]), _token_rows(buf.at[0], r), s).start(priority=0)
            pltpu.make_async_copy(_token_rows(ys_ref, d1_ref[base + r]), _token_rows(buf.at[1], r), s).start(priority=1)
            return carry

        lax.fori_loop(0, tm, body, 0, unroll=8)

    @pl.when(i == 0)
    def _():
        gather(0)

    @pl.when(i + 1 < pl.num_programs(0))
    def _():
        gather(i + 1)

    buf = ybuf.at[i % 2]
    for k in range(TOP_K):
        _row_copy_wait(ys_ref.at[pl.ds(0, tm * ROW_TILES)], buf.at[k], sem.at[i % 2])
    m = meta_ref[...]
    moe = m[:, 0:1] * _from_token_tiles(buf.at[0]) + m[:, 1:2] * _from_token_tiles(buf.at[1])
    h2 = h1_ref[...] + mod_ref[5:6, :] * moe
    o_ref[...] = _rms(h2) * fn_ref[...]


def _final(d0, d1, h1, meta, mod3, final_norm, ys):
    tm = TM_FINAL
    per_b = SEQ // tm
    return pl.pallas_call(
        _final_kernel,
        grid_spec=pltpu.PrefetchScalarGridSpec(
            num_scalar_prefetch=2,
            grid=(N_TOK // tm,),
            in_specs=[pl.BlockSpec((tm, D_MODEL), lambda i, *_: (i, 0)),
                      pl.BlockSpec((tm, LANES), lambda i, *_: (i, 0)),
                      pl.BlockSpec((None, 6, D_MODEL), lambda i, *_: (i // per_b, 0, 0)),
                      pl.BlockSpec((1, D_MODEL), lambda i, *_: (0, 0)),
                      pl.BlockSpec(memory_space=pl.ANY)],
            out_specs=pl.BlockSpec((tm, D_MODEL), lambda i, *_: (i, 0)),
            scratch_shapes=[pltpu.VMEM((2, TOP_K, tm * ROW_TILES, LANES), F32), pltpu.SemaphoreType.DMA((2,))]),
        out_shape=jax.ShapeDtypeStruct((N_TOK, D_MODEL), F32),
        compiler_params=_params("arbitrary"),
        name="final",
    )(d0, d1, h1, meta, mod3, final_norm, ys)


def _slot_layout(counts):
    cnt = counts[:, 0].astype(jnp.int32)
    tile_end = jnp.cumsum((cnt + TE - 1) // TE)
    seg = jnp.concatenate([jnp.zeros((1,), jnp.int32), tile_end * TE])
    off_col = jnp.broadcast_to(seg[:-1].astype(F32)[:, None], (N_EXPERTS, LANES))
    tile_ids = jnp.arange(N_TILES, dtype=jnp.int32)
    tile_expert = jnp.sum((tile_end[None, :] <= tile_ids[:, None]).astype(jnp.int32), axis=1)
    tile_expert = jnp.minimum(tile_expert, N_EXPERTS - 1)
    return cnt, seg, off_col, tile_expert, tile_end[-1:]


def _rope_tables(dim):
    pos = np.arange(SEQ, dtype=np.float64)
    inv = ROPE_THETA ** (-np.arange(0, dim, 2, dtype=np.float64) / dim)
    ang = pos[:, None] * inv[None, :]
    return np.cos(ang).astype(np.float32), np.sin(ang).astype(np.float32)


def _decay_tables():
    c = RET_CHUNK
    log_gamma = np.log1p(-np.exp2(-5.0 - np.arange(RET_HEADS, dtype=np.float64)))
    idx = np.arange(c, dtype=np.float64)
    rel = idx[:, None] - idx[None, :]
    dec = np.where(rel[None] >= 0, np.exp(log_gamma[:, None, None] * np.maximum(rel, 0.0)[None]), 0.0)
    xi = np.exp(log_gamma[:, None] * (idx[None, :] + 1.0))[:, :, None]
    zeta = np.exp(log_gamma[:, None] * (c - 1.0 - idx[None, :]))[:, :, None]
    cd = np.exp(log_gamma * c)[:, None, None]
    return tuple(jnp.asarray(t.astype(np.float32)) for t in (dec, xi, zeta, cd))


def _rotate_half_cols(w):
    half = w.shape[-1] // 2
    return jnp.concatenate([-w[..., half:], w[..., :half]], axis=-1)


def kernel(x, c, w_ada, b_ada, norm1, norm2, w_in, w_ret_o, q_norm, kv_norm, w_uq, w_ukv, w_mla_o, w_out,
           w_grp, b_grp, w_exp, b_exp, w1, w3, w2, final_norm):
    assert x.shape == (BATCH, SEQ, D_MODEL) and w_ada.shape[0] == 1
    x2 = x.reshape(N_TOK, D_MODEL)

    w_in_t = jnp.transpose(w_in[0])
    wq = w_uq[0].reshape(MLA_Q_LORA, MLA_HEADS, MLA_QK)
    wq = jnp.concatenate([wq, _rotate_half_cols(wq[..., MLA_NOPE:])], axis=-1)
    wq = wq.transpose(1, 0, 2).astype(BF16)
    wkv = w_ukv[0].reshape(MLA_KV_LORA, MLA_HEADS, MLA_NOPE + MLA_V).transpose(1, 0, 2).astype(BF16)
    gap = jnp.zeros((SUBLANES - N_GROUPS, D_MODEL), F32)
    tail = jnp.zeros((ROUTER_ROWS - SUBLANES - N_EXPERTS, D_MODEL), F32)
    w_rt = jnp.concatenate([w_grp[0].T, gap, w_exp[0].T, tail], axis=0)
    b_rt = jnp.concatenate([b_grp[0], gap[:, 0], b_exp[0], tail[:, 0]])
    b_rt = jnp.broadcast_to(b_rt[:, None], (ROUTER_ROWS, LANES))

    ret_cos, ret_sin = (jnp.asarray(t) for t in _rope_tables(RET_DK))
    mla_cos, mla_sin = (jnp.asarray(np.concatenate([t, t], axis=-1)) for t in _rope_tables(MLA_ROPE))
    dec, xi, zeta, cd = _decay_tables()

    mod3 = _ada(c, w_ada[0], b_ada[0]).reshape(BATCH, 6, D_MODEL)
    ret, lat, gates = _inproj(x2, mod3, norm1, ret_cos, ret_sin, mla_cos, mla_sin, q_norm, kv_norm, w_in_t)
    attn, y_ret = _mla(lat.reshape(BATCH, SEQ, MLA_LAT_W), wq, wkv, mla_cos, mla_sin,
                       ret.reshape(BATCH, SEQ, 4 * RET_W), dec, xi, zeta, cd)
    h1, u2t, meta_t, wtok, counts = _merge(y_ret.reshape(N_TOK, D_MODEL), attn.reshape(N_TOK, D_MODEL), gates, x2,
                                           mod3, norm2, w_ret_o, w_mla_o, w_out, w_rt, b_rt)
    cnt, seg, off_col, tile_expert, n_valid = _slot_layout(counts)
    dst = _plan(meta_t, off_col)
    d0, d1 = dst[0], dst[1]
    xs = _dispatch(d0, d1, seg, cnt, u2t)
    e_shape = (N_EXPERTS, D_MODEL, D_EXPERT)
    ys = _experts(tile_expert, n_valid, xs, w1[0].reshape(e_shape), w3[0].reshape(e_shape),
                  w2[0].reshape(N_EXPERTS, D_EXPERT, D_MODEL))
    out = _final(d0, d1, h1, wtok, mod3, final_norm.reshape(1, D_MODEL), ys)
    return out.reshape(BATCH, SEQ, D_MODEL)
```

```python
import numpy as np
import jax
import jax.numpy as jnp
from jax import lax
from jax.experimental import pallas as pl
from jax.experimental.pallas import tpu as pltpu

D_MODEL = 1024
BATCH = 8
SEQ = 2048
N_TOK = BATCH * SEQ

RET_HEADS = 4
RET_DK = 256
RET_DV = 256
RET_CHUNK = 256
RET_W = RET_HEADS * RET_DK

MLA_HEADS = 8
MLA_NOPE = 128
MLA_ROPE = 64
MLA_V = 128
MLA_Q_LORA = 384
MLA_KV_LORA = 256
MLA_LAT_W = MLA_Q_LORA + MLA_KV_LORA + 2 * MLA_ROPE
MLA_QK = MLA_NOPE + MLA_ROPE
ROPE_THETA = 10000.0

N_GROUPS = 4
EXPERTS_PER_GROUP = 8
N_EXPERTS = N_GROUPS * EXPERTS_PER_GROUP
D_EXPERT = 256
EPS = 1e-6
LOG2_E = 1.4426950408889634

LANES = 128
SUBLANES = 8
ROUTER_ROWS = 48
ROW_TILES = D_MODEL // LANES
VMEM_LIMIT = 56 * 1024 * 1024

TM_PROJ = 512
TM_INPROJ = 512
TM_FINAL = 512
TM_DISPATCH = 4096
TQ = 256
MLA_HPS = 2
TE = 512
TE_CHAINS = 2
MERGE_CHAINS = 4
TOP_K = 2
N_TILES = N_TOK * TOP_K // TE + N_EXPERTS
N_SLOTS = N_TILES * TE

F32 = jnp.float32
BF16 = jnp.bfloat16


def _sigmoid(x):
    return 1.0 / (1.0 + jnp.exp(-x))


def _rms(x):
    return x * lax.rsqrt(jnp.mean(x * x, axis=-1, keepdims=True) + EPS)


def _dot(a, b):
    return jnp.dot(a, b, preferred_element_type=F32)


def _dot_nt(a, b):
    return lax.dot_general(a, b, (((1,), (1,)), ((), ())), preferred_element_type=F32)


def _dot_tn(a, b):
    return lax.dot_general(a, b, (((0,), (0,)), ((), ())), preferred_element_type=F32)


def _params(*sem):
    return pltpu.CompilerParams(dimension_semantics=sem, vmem_limit_bytes=VMEM_LIMIT)


def _resident(shape):
    nd = len(shape)
    return pl.BlockSpec(shape, lambda *_: (0,) * nd, pipeline_mode=pl.Buffered(1))


def _layer_weight(w):
    return pl.BlockSpec((None,) + w.shape[1:], lambda *_: (0,) * w.ndim, pipeline_mode=pl.Buffered(1))


def _ada_kernel(c_ref, w_ref, b_ref, o_ref):
    c = c_ref[...]
    act = (c * _sigmoid(c)).astype(BF16)
    o_ref[...] = _dot(act, w_ref[...].astype(BF16)) + b_ref[...]


def _ada(c, w_ada, b_ada):
    n = w_ada.shape[1]
    tn = D_MODEL
    return pl.pallas_call(
        _ada_kernel,
        grid=(n // tn,),
        in_specs=[pl.BlockSpec((BATCH, D_MODEL), lambda j: (0, 0)),
                  pl.BlockSpec((D_MODEL, tn), lambda j: (0, j)),
                  pl.BlockSpec((1, tn), lambda j: (0, j))],
        out_specs=pl.BlockSpec((BATCH, tn), lambda j: (0, j)),
        out_shape=jax.ShapeDtypeStruct((BATCH, n), F32),
        compiler_params=_params("arbitrary"),
        name="ada",
    )(c, w_ada, b_ada.reshape(1, n))


O_LAT = 4 * RET_W
O_PE = O_LAT + MLA_Q_LORA + MLA_KV_LORA
O_GATE = O_PE + MLA_ROPE


N_IN = O_GATE + 2 * D_MODEL
WPREP_ROWS = 512


def _stage_weights(w_hbm, ret_ref, lat_ref, gate_ref, buf, sem):
    chunks = [(s, min(WPREP_ROWS, N_IN - s)) for s in range(0, N_IN, WPREP_ROWS)]

    def copy(i):
        s, n = chunks[i]
        return pltpu.make_async_copy(w_hbm.at[pl.ds(s, n)], buf.at[i % 2, pl.ds(0, n)], sem.at[i % 2])

    groups = [(0, RET_W, ret_ref, 0, 1.0), (RET_W, 2 * RET_W, ret_ref, RET_W, RET_DK ** -0.5),
              (2 * RET_W, O_LAT, ret_ref, 2 * RET_W, 1.0), (O_LAT, O_GATE, lat_ref, 0, 1.0),
              (O_GATE, N_IN, gate_ref, 0, 1.0)]
    half = MLA_ROPE // 2
    rot = [(O_PE + half, O_GATE, O_GATE - O_LAT, -1.0), (O_PE, O_PE + half, O_GATE - O_LAT + half, 1.0)]

    copy(0).start()
    for i, (s, n) in enumerate(chunks):
        if i + 1 < len(chunks):
            copy(i + 1).start()
        copy(i).wait()
        for lo, hi, dst, dst_lo, scale in groups + [(a, b, lat_ref, d, sc) for a, b, d, sc in rot]:
            a, b = max(lo, s), min(hi, s + n)
            if a < b:
                x = buf[i % 2, a - s:b - s, :]
                dst[dst_lo + a - lo:dst_lo + b - lo, :] = (x if scale == 1.0 else x * scale).astype(BF16)


def _inproj_kernel(x_ref, mod_ref, n1_ref, cos_ref, sin_ref, mcos_ref, msin_ref, qn_ref, kvn_ref, w_hbm,
                   ret_ref, lat_ref, gate_ref, wr_ref, wm_ref, wg_ref, stage_buf, stage_sem):
    @pl.when(pl.program_id(0) == 0)
    def _():
        _stage_weights(w_hbm, wr_ref, wm_ref, wg_ref, stage_buf, stage_sem)

    y = _rms(x_ref[...]) * n1_ref[...]
    u = (y * (1.0 + mod_ref[1:2, :]) + mod_ref[0:1, :]).astype(BF16)
    cos, sin = cos_ref[...], sin_ref[...]
    half = RET_DK // 2
    for n in range(0, 2 * RET_W, RET_DK):
        p = _dot_nt(u, wr_ref[n:n + RET_DK, :])
        x1, x2 = p[:, :half], p[:, half:]
        ret_ref[:, n:n + half] = (x1 * cos - x2 * sin).astype(BF16)
        ret_ref[:, n + half:n + RET_DK] = (x2 * cos + x1 * sin).astype(BF16)
    step = 512
    for n in range(2 * RET_W, 3 * RET_W, step):
        ret_ref[:, n:n + step] = _dot_nt(u, wr_ref[n:n + step, :]).astype(BF16)
    for n in range(3 * RET_W, 4 * RET_W, step):
        p = _dot_nt(u, wr_ref[n:n + step, :])
        ret_ref[:, n:n + step] = (p * _sigmoid(p)).astype(BF16)
    lat = _dot_nt(u, wm_ref[...])
    o_kv, o_pe, o_rot = MLA_Q_LORA, MLA_Q_LORA + MLA_KV_LORA, MLA_Q_LORA + MLA_KV_LORA + MLA_ROPE
    lat_ref[:, :o_kv] = (_rms(lat[:, :o_kv]) * qn_ref[...]).astype(BF16)
    lat_ref[:, o_kv:o_pe] = (_rms(lat[:, o_kv:o_pe]) * kvn_ref[...]).astype(BF16)
    lat_ref[:, o_pe:o_rot] = (lat[:, o_pe:o_rot] * mcos_ref[...] + lat[:, o_rot:] * msin_ref[...]).astype(BF16)
    lat_ref[:, o_rot:] = jnp.zeros((lat.shape[0], MLA_ROPE), BF16)
    for n in range(0, 2 * D_MODEL, step):
        gate_ref[:, n:n + step] = _dot_nt(u, wg_ref[n:n + step, :]).astype(BF16)


def _inproj(x2, mod3, norm1, cos, sin, mla_cos, mla_sin, q_norm, kv_norm, w_in_t):
    tm = TM_INPROJ
    per_b = SEQ // tm
    rope_tab = pl.BlockSpec((tm, RET_DK // 2), lambda i: (i % per_b, 0))
    mla_tab = pl.BlockSpec((tm, MLA_ROPE), lambda i: (i % per_b, 0))
    return pl.pallas_call(
        _inproj_kernel,
        grid=(N_TOK // tm,),
        in_specs=[pl.BlockSpec((tm, D_MODEL), lambda i: (i, 0)),
                  pl.BlockSpec((None, 6, D_MODEL), lambda i: (i // per_b, 0, 0)),
                  _resident((1, D_MODEL)), rope_tab, rope_tab, mla_tab, mla_tab,
                  _resident(q_norm.shape), _resident(kv_norm.shape),
                  pl.BlockSpec(memory_space=pl.ANY)],
        out_specs=[pl.BlockSpec((tm, 4 * RET_W), lambda i: (i, 0)),
                   pl.BlockSpec((tm, MLA_LAT_W), lambda i: (i, 0)),
                   pl.BlockSpec((tm, 2 * D_MODEL), lambda i: (i, 0))],
        out_shape=[jax.ShapeDtypeStruct((N_TOK, 4 * RET_W), BF16),
                   jax.ShapeDtypeStruct((N_TOK, MLA_LAT_W), BF16),
                   jax.ShapeDtypeStruct((N_TOK, 2 * D_MODEL), BF16)],
        scratch_shapes=[pltpu.VMEM((O_LAT, D_MODEL), BF16), pltpu.VMEM((MLA_LAT_W, D_MODEL), BF16),
                        pltpu.VMEM((2 * D_MODEL, D_MODEL), BF16),
                        pltpu.VMEM((2, WPREP_ROWS, D_MODEL), F32), pltpu.SemaphoreType.DMA((2,))],
        compiler_params=_params("arbitrary"),
        name="inproj",
    )(x2, mod3, norm1, cos, sin, mla_cos, mla_sin, q_norm, kv_norm, w_in_t)


def _mla_kernel(lat_ref, wq_ref, wkv_ref, cos_ref, sin_ref,
                rq_ref, rk_ref, rv_ref, rg_ref, dec_ref, xi_ref, zeta_ref, cd_ref, o_ref, yr_ref,
                q_s, k_s, v_s, state_ref):
    h = pl.program_id(1)
    o_kv, o_pe, o_rot = MLA_Q_LORA, MLA_Q_LORA + MLA_KV_LORA, MLA_Q_LORA + MLA_KV_LORA + MLA_ROPE
    cos, sin = cos_ref[...], sin_ref[...]

    scale = (MLA_QK ** -0.5) * LOG2_E
    for g in range(MLA_HPS):
        qf = _dot(lat_ref[:, :o_kv], wq_ref[g])
        q_s[g, :, :MLA_NOPE] = (qf[:, :MLA_NOPE] * scale).astype(BF16)
        q_pe = qf[:, MLA_NOPE:MLA_QK] * cos + qf[:, MLA_QK:] * sin
        q_s[g, :, MLA_NOPE:] = (q_pe * scale).astype(BF16)
        kvf = _dot(lat_ref[:, o_kv:o_pe], wkv_ref[g])
        k_s[g, :, :MLA_NOPE] = kvf[:, :MLA_NOPE].astype(BF16)
        k_s[g, :, MLA_NOPE:] = lat_ref[:, o_pe:o_rot]
        v_s[g, :, :MLA_V] = kvf[:, MLA_NOPE:].astype(BF16)
        v_s[g, :, MLA_V:] = jnp.ones((SEQ, MLA_V), BF16)

    causal = lax.broadcasted_iota(jnp.int32, (TQ, TQ), 0) >= lax.broadcasted_iota(jnp.int32, (TQ, TQ), 1)
    heads = range(MLA_HPS)
    n_blk = SEQ // TQ

    def scores(i):
        lo, hi = i * TQ, (i + 1) * TQ
        diag = [jnp.where(causal, _dot_nt(q_s[g, lo:hi, :], k_s[g, lo:hi, :]), -jnp.inf) for g in heads]
        past = [_dot_nt(q_s[g, lo:hi, :], k_s[g, :lo, :]) if i > 0 else None for g in heads]
        return diag, past

    state_ref[...] = jnp.zeros_like(state_ref)
    pending = scores(0)
    for i in range(n_blk):
        lo, hi = i * TQ, (i + 1) * TQ
        diag, past = pending
        if i + 1 < n_blk:
            pending = scores(i + 1)
        rq, rk, rv = rq_ref[lo:hi, :], rk_ref[lo:hi, :], rv_ref[lo:hi, :]
        r_scores = (_dot_nt(rq, rk) * dec_ref[h]).astype(BF16)
        carried = xi_ref[h] * _dot(rq, state_ref[...].astype(BF16))
        m = [jnp.max(diag[g], axis=-1, keepdims=True) for g in heads]
        if i > 0:
            m = [jnp.maximum(m[g], jnp.max(past[g], axis=-1, keepdims=True)) for g in heads]
        ry = _dot(r_scores, rv) + carried
        k_dec = (rk.astype(F32) * zeta_ref[h]).astype(BF16)
        state_ref[...] = state_ref[...] * cd_ref[h] + _dot_tn(k_dec, rv)
        acc = [_dot(jnp.exp2(diag[g] - m[g]).astype(BF16), v_s[g, lo:hi, :]) for g in heads]
        if i > 0:
            acc = [acc[g] + _dot(jnp.exp2(past[g] - m[g]).astype(BF16), v_s[g, :lo, :]) for g in heads]
        ryc = ry - jnp.mean(ry, axis=-1, keepdims=True)
        r_inv = lax.rsqrt(jnp.mean(ryc * ryc, axis=-1, keepdims=True) + EPS)
        yr_ref[lo:hi, :] = (rg_ref[lo:hi, :].astype(F32) * (ryc * r_inv)).astype(BF16)
        for g in heads:
            o_ref[lo:hi, g * MLA_V:(g + 1) * MLA_V] = (acc[g][:, :MLA_V] / acc[g][:, MLA_V:]).astype(BF16)


def _mla(lat3, wq, wkv, cos, sin, ret3, dec, xi, zeta, cd):
    hps = MLA_HPS
    assert MLA_HEADS // hps == RET_HEADS and TQ == RET_CHUNK
    ret_part = lambda part: pl.BlockSpec((None, SEQ, RET_DK), lambda b, h: (b, 0, part * RET_HEADS + h))
    whole = lambda a: pl.BlockSpec(a.shape, lambda b, h: (0,) * a.ndim)
    return pl.pallas_call(
        _mla_kernel,
        grid=(BATCH, MLA_HEADS // hps),
        in_specs=[pl.BlockSpec((None, SEQ, MLA_LAT_W), lambda b, h: (b, 0, 0)),
                  pl.BlockSpec((hps, MLA_Q_LORA, MLA_QK + MLA_ROPE), lambda b, h: (h, 0, 0)),
                  pl.BlockSpec((hps, MLA_KV_LORA, MLA_NOPE + MLA_V), lambda b, h: (h, 0, 0)),
                  pl.BlockSpec((SEQ, MLA_ROPE), lambda b, h: (0, 0)),
                  pl.BlockSpec((SEQ, MLA_ROPE), lambda b, h: (0, 0)),
                  ret_part(0), ret_part(1), ret_part(2), ret_part(3),
                  whole(dec), whole(xi), whole(zeta), whole(cd)],
        out_specs=[pl.BlockSpec((None, SEQ, hps * MLA_V), lambda b, h: (b, 0, h)),
                   pl.BlockSpec((None, SEQ, RET_DV), lambda b, h: (b, 0, h))],
        out_shape=[jax.ShapeDtypeStruct((BATCH, SEQ, MLA_HEADS * MLA_V), BF16),
                   jax.ShapeDtypeStruct((BATCH, SEQ, RET_HEADS * RET_DV), BF16)],
        scratch_shapes=[pltpu.VMEM((hps, SEQ, MLA_QK), BF16),
                        pltpu.VMEM((hps, SEQ, MLA_QK), BF16),
                        pltpu.VMEM((hps, SEQ, 2 * MLA_V), BF16),
                        pltpu.VMEM((RET_DK, RET_DV), F32)],
        compiler_params=_params("arbitrary", "arbitrary"),
        name="mla",
    )(lat3, wq, wkv, cos, sin, ret3, ret3, ret3, ret3, dec, xi, zeta, cd)


def _route(logits_t):
    tm = logits_t.shape[1]
    row = lax.broadcasted_iota(jnp.int32, (SUBLANES, tm), 0)
    neg = -jnp.inf
    gl = jnp.where(row < N_GROUPS, logits_t[:SUBLANES], neg)
    gmax = jnp.max(gl, axis=0, keepdims=True)
    gsel = jnp.min(jnp.where(gl == gmax, row, SUBLANES), axis=0, keepdims=True)
    p_grp = 1.0 / jnp.sum(jnp.exp(gl - gmax), axis=0, keepdims=True)
    el = logits_t[SUBLANES * N_GROUPS:SUBLANES * (N_GROUPS + 1)]
    for g in reversed(range(N_GROUPS - 1)):
        el = jnp.where(gsel == g, logits_t[SUBLANES * (g + 1):SUBLANES * (g + 2)], el)
    v0 = jnp.max(el, axis=0, keepdims=True)
    i0 = jnp.min(jnp.where(el == v0, row, SUBLANES), axis=0, keepdims=True)
    el1 = jnp.where(row == i0, neg, el)
    v1 = jnp.max(el1, axis=0, keepdims=True)
    i1 = jnp.min(jnp.where(el1 == v1, row, SUBLANES), axis=0, keepdims=True)
    t = jnp.exp(v1 - v0)
    w0 = p_grp / (1.0 + t)
    w1 = p_grp * t / (1.0 + t)
    return gsel * EXPERTS_PER_GROUP + i0, gsel * EXPERTS_PER_GROUP + i1, w0, w1


def _stack_rows(rows, n):
    tm = rows[0].shape[1]
    row = lax.broadcasted_iota(jnp.int32, (n, tm), 0)
    out = jnp.zeros((n, tm), F32)
    for k, r in enumerate(rows):
        out = jnp.where(row == k, r, out)
    return out


def _to_token_tiles(ref, val):
    n = val.shape[0]
    for s in range(ROW_TILES):
        ref[pl.ds(s, n, stride=ROW_TILES), :] = val[:, s * LANES:(s + 1) * LANES]


def _from_token_tiles(ref):
    n = ref.shape[0] // ROW_TILES
    return jnp.concatenate([ref[pl.ds(s, n, stride=ROW_TILES), :] for s in range(ROW_TILES)], axis=-1)


def _token_rows(ref, t):
    return ref.at[pl.ds(pl.multiple_of(t * ROW_TILES, ROW_TILES), ROW_TILES)]


def _merge_kernel(yr_ref, at_ref, gr_ref, gm_ref, x_ref, mod_ref, n2_ref, wro_ref, wmo_ref, wo_ref,
                  wrt_ref, brt_ref, h1_ref, u2_ref, meta_ref, wtok_ref, cnt_ref, carry_ref, wro_s, wmo_s, wo_s):
    tm = x_ref.shape[0]

    @pl.when(pl.program_id(0) == 0)
    def _():
        carry_ref[...] = jnp.zeros_like(carry_ref)
        wro_s[...] = wro_ref[...].astype(BF16)
        wmo_s[...] = wmo_ref[...].astype(BF16)
        wo_s[...] = wo_ref[...].astype(BF16)

    chunks = range(MERGE_CHAINS)
    sub = tm // MERGE_CHAINS
    rows = [slice(c * sub, (c + 1) * sub) for c in chunks]
    y_ret = [_dot(yr_ref[r, :], wro_s[...]) for r in rows]
    y_mla = [_dot(at_ref[r, :], wmo_s[...]) for r in rows]
    merged = [(_sigmoid(gr_ref[rows[c], :].astype(F32)) * y_ret[c]
               + _sigmoid(gm_ref[rows[c], :].astype(F32)) * y_mla[c]).astype(BF16) for c in chunks]
    o = [_dot(merged[c], wo_s[...]) for c in chunks]
    h1 = [x_ref[rows[c], :] + mod_ref[2:3, :] * o[c] for c in chunks]
    for c in chunks:
        h1_ref[rows[c], :] = h1[c]
    u2 = [_rms(h1[c]) * n2_ref[...] * (1.0 + mod_ref[4:5, :]) + mod_ref[3:4, :] for c in chunks]
    for c in chunks:
        _to_token_tiles(u2_ref.at[pl.ds(c * sub * ROW_TILES, sub * ROW_TILES)], u2[c])
    w = wrt_ref[...]
    w_hi = w.astype(BF16)
    w_lo = (w - w_hi.astype(F32)).astype(BF16)
    w_both = jnp.concatenate([w_hi, w_lo], axis=0)
    u_hi = [u2[c].astype(BF16) for c in chunks]
    u_lo = [(u2[c] - u_hi[c].astype(F32)).astype(BF16) for c in chunks]
    by_hi = [_dot_nt(w_both, u_hi[c]) for c in chunks]
    logits_t = [by_hi[c][:ROUTER_ROWS] + by_hi[c][ROUTER_ROWS:] + _dot_nt(w_hi, u_lo[c]) + brt_ref[:, 0:1]
                for c in chunks]
    routed = [_route(logits_t[c]) for c in chunks]
    e0, e1, w0, w1 = [jnp.concatenate([routed[c][k] for c in chunks], axis=1) for k in range(4)]
    erow = lax.broadcasted_iota(jnp.int32, (N_EXPERTS, tm), 0)
    m0, m1 = erow == e0, erow == e1
    member = jnp.where(m0 | m1, 1.0, 0.0)
    earlier = jnp.where(lax.broadcasted_iota(jnp.int32, (tm, tm), 0) < lax.broadcasted_iota(jnp.int32, (tm, tm), 1),
                        1.0, 0.0).astype(BF16)
    prefix = _dot(member.astype(BF16), earlier) + carry_ref[:, 0:1]
    rank0 = jnp.sum(jnp.where(m0, prefix, 0.0), axis=0, keepdims=True)
    rank1 = jnp.sum(jnp.where(m1, prefix, 0.0), axis=0, keepdims=True)
    carry_ref[...] = carry_ref[...] + jnp.sum(member, axis=1, keepdims=True)
    cnt_ref[...] = carry_ref[...]
    meta_ref[...] = _stack_rows([e0.astype(F32), e1.astype(F32), rank0, rank1], SUBLANES)
    wt = _stack_rows([w0, w1], 2 * SUBLANES)
    wt_hi = wt.astype(BF16)
    wt_lo = (wt - wt_hi.astype(F32)).astype(BF16)
    place = jnp.where(lax.broadcasted_iota(jnp.int32, (2 * SUBLANES, LANES), 0)
                      == lax.broadcasted_iota(jnp.int32, (2 * SUBLANES, LANES), 1), 1.0, 0.0).astype(BF16)
    wtok_ref[...] = _dot_tn(wt_hi, place) + _dot_tn(wt_lo, place)


def _merge(y_ret, attn, gates, x2, mod3, norm2, w_ret_o, w_mla_o, w_out, w_rt, b_rt):
    tm = TM_PROJ
    per_b = SEQ // tm
    row = lambda j: pl.BlockSpec((tm, D_MODEL), lambda i: (i, j))
    return pl.pallas_call(
        _merge_kernel,
        grid=(N_TOK // tm,),
        in_specs=[row(0), row(0), row(0), row(1), row(0),
                  pl.BlockSpec((None, 6, D_MODEL), lambda i: (i // per_b, 0, 0)),
                  _resident((1, D_MODEL)),
                  _layer_weight(w_ret_o), _layer_weight(w_mla_o), _layer_weight(w_out),
                  _resident(w_rt.shape), _resident(b_rt.shape)],
        out_specs=[row(0),
                   pl.BlockSpec((tm * ROW_TILES, LANES), lambda i: (i, 0)),
                   pl.BlockSpec((SUBLANES, tm), lambda i: (0, i)),
                   pl.BlockSpec((tm, LANES), lambda i: (i, 0)),
                   pl.BlockSpec((N_EXPERTS, LANES), lambda i: (0, 0))],
        out_shape=[jax.ShapeDtypeStruct((N_TOK, D_MODEL), F32),
                   jax.ShapeDtypeStruct((N_TOK * ROW_TILES, LANES), F32),
                   jax.ShapeDtypeStruct((SUBLANES, N_TOK), F32),
                   jax.ShapeDtypeStruct((N_TOK, LANES), F32),
                   jax.ShapeDtypeStruct((N_EXPERTS, LANES), F32)],
        scratch_shapes=[pltpu.VMEM((N_EXPERTS, LANES), F32)] + [pltpu.VMEM((D_MODEL, D_MODEL), BF16)] * 3,
        compiler_params=_params("arbitrary"),
        name="merge",
    )(y_ret, attn, gates, gates, x2, mod3, norm2, w_ret_o, w_mla_o, w_out, w_rt, b_rt)


def _plan_kernel(meta_ref, off_ref, dst_ref):
    m = meta_ref[...]
    erow = lax.broadcasted_iota(jnp.int32, (N_EXPERTS, m.shape[1]), 0)
    off = off_ref[:, 0:1]
    d = [jnp.sum(jnp.where(erow == m[k:k + 1].astype(jnp.int32), off, 0.0), axis=0, keepdims=True) + m[k + 2:k + 3]
         for k in range(TOP_K)]
    dst_ref[...] = _stack_rows(d, SUBLANES).astype(jnp.int32)


def _plan(meta_t, off_col):
    tm = 2048
    return pl.pallas_call(
        _plan_kernel,
        grid=(N_TOK // tm,),
        in_specs=[pl.BlockSpec((SUBLANES, tm), lambda i: (0, i)),
                  pl.BlockSpec((N_EXPERTS, LANES), lambda i: (0, 0))],
        out_specs=pl.BlockSpec((SUBLANES, tm), lambda i: (0, i)),
        out_shape=jax.ShapeDtypeStruct((SUBLANES, N_TOK), jnp.int32),
        compiler_params=_params("arbitrary"),
        name="plan",
    )(meta_t, off_col)


def _row_copy_wait(src_like, dst_like, sem):
    pltpu.make_async_copy(src_like, dst_like, sem).wait()


def _dispatch_kernel(d0_ref, d1_ref, seg_ref, cnt_ref, u_ref, xs_ref, zero_ref, sem, zsem):
    i = pl.program_id(0)
    tm = u_ref.shape[0] // ROW_TILES
    tile_rows = TE * ROW_TILES

    def slot_tile(j):
        return xs_ref.at[pl.ds(pl.multiple_of(j * tile_rows, tile_rows), tile_rows)]

    def pad(e):
        first = seg_ref[e] + cnt_ref[e]
        rows = (seg_ref[e + 1] - first) * ROW_TILES
        start = pl.multiple_of(first * ROW_TILES, ROW_TILES)
        return rows > 0, pltpu.make_async_copy(zero_ref.at[pl.ds(0, rows)], xs_ref.at[pl.ds(start, rows)], zsem)

    def unused(j):
        return pltpu.make_async_copy(zero_ref, slot_tile(j), zsem)

    first_unused = seg_ref[N_EXPERTS] // TE

    @pl.when(i == 0)
    def _():
        zero_ref[...] = jnp.zeros_like(zero_ref)
        for e in range(N_EXPERTS):
            nonempty, copy = pad(e)
            pl.when(nonempty)(copy.start)
        lax.fori_loop(first_unused, N_TILES, lambda j, c: (unused(j).start(), c)[1], 0)

    base = i * tm

    def body(r, carry):
        src = _token_rows(u_ref, r)
        pltpu.make_async_copy(src, _token_rows(xs_ref, d0_ref[base + r]), sem).start(priority=0)
        pltpu.make_async_copy(src, _token_rows(xs_ref, d1_ref[base + r]), sem).start(priority=1)
        return carry

    lax.fori_loop(0, tm, body, 0, unroll=8)
    _row_copy_wait(u_ref, xs_ref.at[pl.ds(0, tm * ROW_TILES)], sem)
    _row_copy_wait(u_ref, xs_ref.at[pl.ds(0, tm * ROW_TILES)], sem)

    @pl.when(i == 0)
    def _():
        for e in range(N_EXPERTS):
            nonempty, copy = pad(e)
            pl.when(nonempty)(copy.wait)
        lax.fori_loop(first_unused, N_TILES, lambda j, c: (unused(j).wait(), c)[1], 0)


def _dispatch(d0, d1, seg, cnt, u2t):
    tm = TM_DISPATCH
    return pl.pallas_call(
        _dispatch_kernel,
        grid_spec=pltpu.PrefetchScalarGridSpec(
            num_scalar_prefetch=4,
            grid=(N_TOK // tm,),
            in_specs=[pl.BlockSpec((tm * ROW_TILES, LANES), lambda i, *_: (i, 0))],
            out_specs=pl.BlockSpec(memory_space=pl.ANY),
            scratch_shapes=[pltpu.VMEM((TE * ROW_TILES, LANES), F32),
                            pltpu.SemaphoreType.DMA(()), pltpu.SemaphoreType.DMA(())]),
        out_shape=jax.ShapeDtypeStruct((N_SLOTS * ROW_TILES, LANES), F32),
        compiler_params=_params("arbitrary"),
        name="dispatch",
    )(d0, d1, seg, cnt, u2t)


def _expert_kernel(te_ref, nv_ref, x_ref, w1_ref, w3_ref, w2_ref, y_ref, w1_s, w3_s, w2_s):
    j = pl.program_id(0)

    @pl.when(j < nv_ref[0])
    def _():
        @pl.when((j == 0) | (te_ref[j] != te_ref[jnp.maximum(j - 1, 0)]))
        def _():
            w1_s[...] = w1_ref[...].astype(BF16)
            w3_s[...] = w3_ref[...].astype(BF16)
            w2_s[...] = w2_ref[...].astype(BF16)

        sub = TE // TE_CHAINS * ROW_TILES
        part = lambda ref, c: ref.at[pl.ds(c * sub, sub)]
        chains = range(TE_CHAINS)
        x = [_from_token_tiles(part(x_ref, c)).astype(BF16) for c in chains]
        a = [_dot(x[c], w1_s[...]) for c in chains]
        b = [_dot(x[c], w3_s[...]) for c in chains]
        hid = [(a[c] * _sigmoid(a[c]) * b[c]).astype(BF16) for c in chains]
        for c in chains:
            _to_token_tiles(part(y_ref, c), _dot(hid[c], w2_s[...]))


def _experts(tile_expert, n_valid, xs, w1, w3, w2):
    tile = lambda j, te, nv: jnp.minimum(j, nv[0] - 1)
    wspec = lambda shape: pl.BlockSpec((None,) + shape, lambda j, te, nv: (te[tile(j, te, nv)], 0, 0))
    slots = pl.BlockSpec((TE * ROW_TILES, LANES), lambda j, te, nv: (tile(j, te, nv), 0))
    return pl.pallas_call(
        _expert_kernel,
        grid_spec=pltpu.PrefetchScalarGridSpec(
            num_scalar_prefetch=2,
            grid=(N_TILES,),
            in_specs=[slots, wspec((D_MODEL, D_EXPERT)), wspec((D_MODEL, D_EXPERT)), wspec((D_EXPERT, D_MODEL))],
            out_specs=slots,
            scratch_shapes=[pltpu.VMEM((D_MODEL, D_EXPERT), BF16), pltpu.VMEM((D_MODEL, D_EXPERT), BF16),
                            pltpu.VMEM((D_EXPERT, D_MODEL), BF16)]),
        out_shape=jax.ShapeDtypeStruct((N_SLOTS * ROW_TILES, LANES), F32),
        input_output_aliases={2: 0},
        compiler_params=_params("arbitrary"),
        name="experts",
    )(tile_expert, n_valid, xs, w1, w3, w2)


def _final_kernel(d0_ref, d1_ref, h1_ref, meta_ref, mod_ref, fn_ref, ys_ref, o_ref, ybuf, sem):
    i = pl.program_id(0)
    tm = h1_ref.shape[0]

    def gather(t):
        buf, s = ybuf.at[t % 2], sem.at[t % 2]
        base = t * tm

        def body(r, carry):
            pltpu.make_async_copy(_token_rows(ys_ref, d0_ref[base + r]), _token_rows(buf.at[0], r), s).start(priority=0)
            pltpu.make_async_copy(_token_rows(ys_ref, d1_ref[base + r]), _token_rows(buf.at[1], r), s).start(priority=1)
            return carry

        lax.fori_loop(0, tm, body, 0, unroll=8)

    @pl.when(i == 0)
    def _():
        gather(0)

    @pl.when(i + 1 < pl.num_programs(0))
    def _():
        gather(i + 1)

    buf = ybuf.at[i % 2]
    for k in range(TOP_K):
        _row_copy_wait(ys_ref.at[pl.ds(0, tm * ROW_TILES)], buf.at[k], sem.at[i % 2])
    m = meta_ref[...]
    moe = m[:, 0:1] * _from_token_tiles(buf.at[0]) + m[:, 1:2] * _from_token_tiles(buf.at[1])
    h2 = h1_ref[...] + mod_ref[5:6, :] * moe
    o_ref[...] = _rms(h2) * fn_ref[...]


def _final(d0, d1, h1, meta, mod3, final_norm, ys):
    tm = TM_FINAL
    per_b = SEQ // tm
    return pl.pallas_call(
        _final_kernel,
        grid_spec=pltpu.PrefetchScalarGridSpec(
            num_scalar_prefetch=2,
            grid=(N_TOK // tm,),
            in_specs=[pl.BlockSpec((tm, D_MODEL), lambda i, *_: (i, 0)),
                      pl.BlockSpec((tm, LANES), lambda i, *_: (i, 0)),
                      pl.BlockSpec((None, 6, D_MODEL), lambda i, *_: (i // per_b, 0, 0)),
                      pl.BlockSpec((1, D_MODEL), lambda i, *_: (0, 0)),
                      pl.BlockSpec(memory_space=pl.ANY)],
            out_specs=pl.BlockSpec((tm, D_MODEL), lambda i, *_: (i, 0)),
            scratch_shapes=[pltpu.VMEM((2, TOP_K, tm * ROW_TILES, LANES), F32), pltpu.SemaphoreType.DMA((2,))]),
        out_shape=jax.ShapeDtypeStruct((N_TOK, D_MODEL), F32),
        compiler_params=_params("arbitrary"),
        name="final",
    )(d0, d1, h1, meta, mod3, final_norm, ys)


def _slot_layout(counts):
    cnt = counts[:, 0].astype(jnp.int32)
    tile_end = jnp.cumsum((cnt + TE - 1) // TE)
    seg = jnp.concatenate([jnp.zeros((1,), jnp.int32), tile_end * TE])
    off_col = jnp.broadcast_to(seg[:-1].astype(F32)[:, None], (N_EXPERTS, LANES))
    tile_ids = jnp.arange(N_TILES, dtype=jnp.int32)
    tile_expert = jnp.sum((tile_end[None, :] <= tile_ids[:, None]).astype(jnp.int32), axis=1)
    tile_expert = jnp.minimum(tile_expert, N_EXPERTS - 1)
    return cnt, seg, off_col, tile_expert, tile_end[-1:]


def _rope_tables(dim):
    pos = np.arange(SEQ, dtype=np.float64)
    inv = ROPE_THETA ** (-np.arange(0, dim, 2, dtype=np.float64) / dim)
    ang = pos[:, None] * inv[None, :]
    return np.cos(ang).astype(np.float32), np.sin(ang).astype(np.float32)


def _decay_tables():
    c = RET_CHUNK
    log_gamma = np.log1p(-np.exp2(-5.0 - np.arange(RET_HEADS, dtype=np.float64)))
    idx = np.arange(c, dtype=np.float64)
    rel = idx[:, None] - idx[None, :]
    dec = np.where(rel[None] >= 0, np.exp(log_gamma[:, None, None] * np.maximum(rel, 0.0)[None]), 0.0)
    xi = np.exp(log_gamma[:, None] * (idx[None, :] + 1.0))[:, :, None]
    zeta = np.exp(log_gamma[:, None] * (c - 1.0 - idx[None, :]))[:, :, None]
    cd = np.exp(log_gamma * c)[:, None, None]
    return tuple(jnp.asarray(t.astype(np.float32)) for t in (dec, xi, zeta, cd))


def _rotate_half_cols(w):
    half = w.shape[-1] // 2
    return jnp.concatenate([-w[..., half:], w[..., :half]], axis=-1)


def kernel(x, c, w_ada, b_ada, norm1, norm2, w_in, w_ret_o, q_norm, kv_norm, w_uq, w_ukv, w_mla_o, w_out,
           w_grp, b_grp, w_exp, b_exp, w1, w3, w2, final_norm):
    assert x.shape == (BATCH, SEQ, D_MODEL) and w_ada.shape[0] == 1
    x2 = x.reshape(N_TOK, D_MODEL)

    w_in_t = jnp.transpose(w_in[0])
    wq = w_uq[0].reshape(MLA_Q_LORA, MLA_HEADS, MLA_QK)
    wq = jnp.concatenate([wq, _rotate_half_cols(wq[..., MLA_NOPE:])], axis=-1)
    wq = wq.transpose(1, 0, 2).astype(BF16)
    wkv = w_ukv[0].reshape(MLA_KV_LORA, MLA_HEADS, MLA_NOPE + MLA_V).transpose(1, 0, 2).astype(BF16)
    gap = jnp.zeros((SUBLANES - N_GROUPS, D_MODEL), F32)
    tail = jnp.zeros((ROUTER_ROWS - SUBLANES - N_EXPERTS, D_MODEL), F32)
    w_rt = jnp.concatenate([w_grp[0].T, gap, w_exp[0].T, tail], axis=0)
    b_rt = jnp.concatenate([b_grp[0], gap[:, 0], b_exp[0], tail[:, 0]])
    b_rt = jnp.broadcast_to(b_rt[:, None], (ROUTER_ROWS, LANES))

    ret_cos, ret_sin = (jnp.asarray(t) for t in _rope_tables(RET_DK))
    mla_cos, mla_sin = (jnp.asarray(np.concatenate([t, t], axis=-1)) for t in _rope_tables(MLA_ROPE))
    dec, xi, zeta, cd = _decay_tables()

    mod3 = _ada(c, w_ada[0], b_ada[0]).reshape(BATCH, 6, D_MODEL)
    ret, lat, gates = _inproj(x2, mod3, norm1, ret_cos, ret_sin, mla_cos, mla_sin, q_norm, kv_norm, w_in_t)
    attn, y_ret = _mla(lat.reshape(BATCH, SEQ, MLA_LAT_W), wq, wkv, mla_cos, mla_sin,
                       ret.reshape(BATCH, SEQ, 4 * RET_W), dec, xi, zeta, cd)
    h1, u2t, meta_t, wtok, counts = _merge(y_ret.reshape(N_TOK, D_MODEL), attn.reshape(N_TOK, D_MODEL), gates, x2,
                                           mod3, norm2, w_ret_o, w_mla_o, w_out, w_rt, b_rt)
    cnt, seg, off_col, tile_expert, n_valid = _slot_layout(counts)
    dst = _plan(meta_t, off_col)
    d0, d1 = dst[0], dst[1]
    xs = _dispatch(d0, d1, seg, cnt, u2t)
    e_shape = (N_EXPERTS, D_MODEL, D_EXPERT)
    ys = _experts(tile_expert, n_valid, xs, w1[0].reshape(e_shape), w3[0].reshape(e_shape),
                  w2[0].reshape(N_EXPERTS, D_EXPERT, D_MODEL))
    out = _final(d0, d1, h1, wtok, mod3, final_norm.reshape(1, D_MODEL), ys)
    return out.reshape(BATCH, SEQ, D_MODEL)
```

```python
import numpy as np
import jax
import jax.numpy as jnp
from jax import lax
from jax.experimental import pallas as pl
from jax.experimental.pallas import tpu as pltpu

D_MODEL = 1024
BATCH = 8
SEQ = 2048
N_TOK = BATCH * SEQ

RET_HEADS = 4
RET_DK = 256
RET_DV = 256
RET_CHUNK = 256
RET_W = RET_HEADS * RET_DK

MLA_HEADS = 8
MLA_NOPE = 128
MLA_ROPE = 64
MLA_V = 128
MLA_Q_LORA = 384
MLA_KV_LORA = 256
MLA_LAT_W = MLA_Q_LORA + MLA_KV_LORA + 2 * MLA_ROPE
MLA_QK = MLA_NOPE + MLA_ROPE
ROPE_THETA = 10000.0

N_GROUPS = 4
EXPERTS_PER_GROUP = 8
N_EXPERTS = N_GROUPS * EXPERTS_PER_GROUP
D_EXPERT = 256
EPS = 1e-6
LOG2_E = 1.4426950408889634

LANES = 128
SUBLANES = 8
ROUTER_ROWS = 48
ROW_TILES = D_MODEL // LANES
VMEM_LIMIT = 56 * 1024 * 1024

TM_PROJ = 512
TM_INPROJ = 512
TM_FINAL = 512
TM_DISPATCH = 4096
TQ = 256
MLA_HPS = 2
TE = 512
TE_CHAINS = 2
MERGE_CHAINS = 4
TOP_K = 2
N_TILES = N_TOK * TOP_K // TE + N_EXPERTS
N_SLOTS = N_TILES * TE

F32 = jnp.float32
BF16 = jnp.bfloat16


def _sigmoid(x):
    return 1.0 / (1.0 + jnp.exp(-x))


def _rms(x):
    return x * lax.rsqrt(jnp.mean(x * x, axis=-1, keepdims=True) + EPS)


def _dot(a, b):
    return jnp.dot(a, b, preferred_element_type=F32)


def _dot_nt(a, b):
    return lax.dot_general(a, b, (((1,), (1,)), ((), ())), preferred_element_type=F32)


def _dot_tn(a, b):
    return lax.dot_general(a, b, (((0,), (0,)), ((), ())), preferred_element_type=F32)


def _params(*sem):
    return pltpu.CompilerParams(dimension_semantics=sem, vmem_limit_bytes=VMEM_LIMIT)


def _resident(shape):
    nd = len(shape)
    return pl.BlockSpec(shape, lambda *_: (0,) * nd, pipeline_mode=pl.Buffered(1))


def _layer_weight(w):
    return pl.BlockSpec((None,) + w.shape[1:], lambda *_: (0,) * w.ndim, pipeline_mode=pl.Buffered(1))


def _ada_kernel(c_ref, w_ref, b_ref, o_ref):
    c = c_ref[...]
    act = (c * _sigmoid(c)).astype(BF16)
    o_ref[...] = _dot(act, w_ref[...].astype(BF16)) + b_ref[...]


def _ada(c, w_ada, b_ada):
    n = w_ada.shape[1]
    tn = D_MODEL
    return pl.pallas_call(
        _ada_kernel,
        grid=(n // tn,),
        in_specs=[pl.BlockSpec((BATCH, D_MODEL), lambda j: (0, 0)),
                  pl.BlockSpec((D_MODEL, tn), lambda j: (0, j)),
                  pl.BlockSpec((1, tn), lambda j: (0, j))],
        out_specs=pl.BlockSpec((BATCH, tn), lambda j: (0, j)),
        out_shape=jax.ShapeDtypeStruct((BATCH, n), F32),
        compiler_params=_params("arbitrary"),
        name="ada",
    )(c, w_ada, b_ada.reshape(1, n))


O_LAT = 4 * RET_W
O_PE = O_LAT + MLA_Q_LORA + MLA_KV_LORA
O_GATE = O_PE + MLA_ROPE


N_IN = O_GATE + 2 * D_MODEL
WPREP_ROWS = 512


def _stage_weights(w_hbm, ret_ref, lat_ref, gate_ref, buf, sem):
    chunks = [(s, min(WPREP_ROWS, N_IN - s)) for s in range(0, N_IN, WPREP_ROWS)]

    def copy(i):
        s, n = chunks[i]
        return pltpu.make_async_copy(w_hbm.at[pl.ds(s, n)], buf.at[i % 2, pl.ds(0, n)], sem.at[i % 2])

    groups = [(0, RET_W, ret_ref, 0, 1.0), (RET_W, 2 * RET_W, ret_ref, RET_W, RET_DK ** -0.5),
              (2 * RET_W, O_LAT, ret_ref, 2 * RET_W, 1.0), (O_LAT, O_GATE, lat_ref, 0, 1.0),
              (O_GATE, N_IN, gate_ref, 0, 1.0)]
    half = MLA_ROPE // 2
    rot = [(O_PE + half, O_GATE, O_GATE - O_LAT, -1.0), (O_PE, O_PE + half, O_GATE - O_LAT + half, 1.0)]

    copy(0).start()
    for i, (s, n) in enumerate(chunks):
        if i + 1 < len(chunks):
            copy(i + 1).start()
        copy(i).wait()
        for lo, hi, dst, dst_lo, scale in groups + [(a, b, lat_ref, d, sc) for a, b, d, sc in rot]:
            a, b = max(lo, s), min(hi, s + n)
            if a < b:
                x = buf[i % 2, a - s:b - s, :]
                dst[dst_lo + a - lo:dst_lo + b - lo, :] = (x if scale == 1.0 else x * scale).astype(BF16)


def _inproj_kernel(x_ref, mod_ref, n1_ref, cos_ref, sin_ref, mcos_ref, msin_ref, qn_ref, kvn_ref, w_hbm,
                   ret_ref, lat_ref, gate_ref, wr_ref, wm_ref, wg_ref, stage_buf, stage_sem):
    @pl.when(pl.program_id(0) == 0)
    def _():
        _stage_weights(w_hbm, wr_ref, wm_ref, wg_ref, stage_buf, stage_sem)

    y = _rms(x_ref[...]) * n1_ref[...]
    u = (y * (1.0 + mod_ref[1:2, :]) + mod_ref[0:1, :]).astype(BF16)
    cos, sin = cos_ref[...], sin_ref[...]
    half = RET_DK // 2
    for n in range(0, 2 * RET_W, RET_DK):
        p = _dot_nt(u, wr_ref[n:n + RET_DK, :])
        x1, x2 = p[:, :half], p[:, half:]
        ret_ref[:, n:n + half] = (x1 * cos - x2 * sin).astype(BF16)
        ret_ref[:, n + half:n + RET_DK] = (x2 * cos + x1 * sin).astype(BF16)
    step = 512
    for n in range(2 * RET_W, 3 * RET_W, step):
        ret_ref[:, n:n + step] = _dot_nt(u, wr_ref[n:n + step, :]).astype(BF16)
    for n in range(3 * RET_W, 4 * RET_W, step):
        p = _dot_nt(u, wr_ref[n:n + step, :])
        ret_ref[:, n:n + step] = (p * _sigmoid(p)).astype(BF16)
    lat = _dot_nt(u, wm_ref[...])
    o_kv, o_pe, o_rot = MLA_Q_LORA, MLA_Q_LORA + MLA_KV_LORA, MLA_Q_LORA + MLA_KV_LORA + MLA_ROPE
    lat_ref[:, :o_kv] = (_rms(lat[:, :o_kv]) * qn_ref[...]).astype(BF16)
    lat_ref[:, o_kv:o_pe] = (_rms(lat[:, o_kv:o_pe]) * kvn_ref[...]).astype(BF16)
    lat_ref[:, o_pe:o_rot] = (lat[:, o_pe:o_rot] * mcos_ref[...] + lat[:, o_rot:] * msin_ref[...]).astype(BF16)
    lat_ref[:, o_rot:] = jnp.zeros((lat.shape[0], MLA_ROPE), BF16)
    for n in range(0, 2 * D_MODEL, step):
        gate_ref[:, n:n + step] = _dot_nt(u, wg_ref[n:n + step, :]).astype(BF16)


def _inproj(x2, mod3, norm1, cos, sin, mla_cos, mla_sin, q_norm, kv_norm, w_in_t):
    tm = TM_INPROJ
    per_b = SEQ // tm
    rope_tab = pl.BlockSpec((tm, RET_DK // 2), lambda i: (i % per_b, 0))
    mla_tab = pl.BlockSpec((tm, MLA_ROPE), lambda i: (i % per_b, 0))
    return pl.pallas_call(
        _inproj_kernel,
        grid=(N_TOK // tm,),
        in_specs=[pl.BlockSpec((tm, D_MODEL), lambda i: (i, 0)),
                  pl.BlockSpec((None, 6, D_MODEL), lambda i: (i // per_b, 0, 0)),
                  _resident((1, D_MODEL)), rope_tab, rope_tab, mla_tab, mla_tab,
                  _resident(q_norm.shape), _resident(kv_norm.shape),
                  pl.BlockSpec(memory_space=pl.ANY)],
        out_specs=[pl.BlockSpec((tm, 4 * RET_W), lambda i: (i, 0)),
                   pl.BlockSpec((tm, MLA_LAT_W), lambda i: (i, 0)),
                   pl.BlockSpec((tm, 2 * D_MODEL), lambda i: (i, 0))],
        out_shape=[jax.ShapeDtypeStruct((N_TOK, 4 * RET_W), BF16),
                   jax.ShapeDtypeStruct((N_TOK, MLA_LAT_W), BF16),
                   jax.ShapeDtypeStruct((N_TOK, 2 * D_MODEL), BF16)],
        scratch_shapes=[pltpu.VMEM((O_LAT, D_MODEL), BF16), pltpu.VMEM((MLA_LAT_W, D_MODEL), BF16),
                        pltpu.VMEM((2 * D_MODEL, D_MODEL), BF16),
                        pltpu.VMEM((2, WPREP_ROWS, D_MODEL), F32), pltpu.SemaphoreType.DMA((2,))],
        compiler_params=_params("arbitrary"),
        name="inproj",
    )(x2, mod3, norm1, cos, sin, mla_cos, mla_sin, q_norm, kv_norm, w_in_t)


def _mla_kernel(lat_ref, wq_ref, wkv_ref, cos_ref, sin_ref,
                rq_ref, rk_ref, rv_ref, rg_ref, dec_ref, xi_ref, zeta_ref, cd_ref, w1_ref, w3_ref, w2_ref,
                o_ref, yr_ref, w1b_ref, w3b_ref, w2b_ref, q_s, k_s, v_s, state_ref):
    h = pl.program_id(1)
    o_kv, o_pe, o_rot = MLA_Q_LORA, MLA_Q_LORA + MLA_KV_LORA, MLA_Q_LORA + MLA_KV_LORA + MLA_ROPE
    cos, sin = cos_ref[...], sin_ref[...]

    scale = (MLA_QK ** -0.5) * LOG2_E
    for g in range(MLA_HPS):
        qf = _dot(lat_ref[:, :o_kv], wq_ref[g])
        q_s[g, :, :MLA_NOPE] = (qf[:, :MLA_NOPE] * scale).astype(BF16)
        q_pe = qf[:, MLA_NOPE:MLA_QK] * cos + qf[:, MLA_QK:] * sin
        q_s[g, :, MLA_NOPE:] = (q_pe * scale).astype(BF16)
        kvf = _dot(lat_ref[:, o_kv:o_pe], wkv_ref[g])
        k_s[g, :, :MLA_NOPE] = kvf[:, :MLA_NOPE].astype(BF16)
        k_s[g, :, MLA_NOPE:] = lat_ref[:, o_pe:o_rot]
        v_s[g, :, :MLA_V] = kvf[:, MLA_NOPE:].astype(BF16)
        v_s[g, :, MLA_V:] = jnp.ones((SEQ, MLA_V), BF16)

    causal = lax.broadcasted_iota(jnp.int32, (TQ, TQ), 0) >= lax.broadcasted_iota(jnp.int32, (TQ, TQ), 1)
    heads = range(MLA_HPS)
    n_blk = SEQ // TQ

    def scores(i):
        lo, hi = i * TQ, (i + 1) * TQ
        diag = [jnp.where(causal, _dot_nt(q_s[g, lo:hi, :], k_s[g, lo:hi, :]), -jnp.inf) for g in heads]
        past = [_dot_nt(q_s[g, lo:hi, :], k_s[g, :lo, :]) if i > 0 else None for g in heads]
        return diag, past

    state_ref[...] = jnp.zeros_like(state_ref)
    pending = scores(0)
    for i in range(n_blk):
        lo, hi = i * TQ, (i + 1) * TQ
        diag, past = pending
        if i + 1 < n_blk:
            pending = scores(i + 1)
        rq, rk, rv = rq_ref[lo:hi, :], rk_ref[lo:hi, :], rv_ref[lo:hi, :]
        r_scores = (_dot_nt(rq, rk) * dec_ref[h]).astype(BF16)
        carried = xi_ref[h] * _dot(rq, state_ref[...].astype(BF16))
        m = [jnp.max(diag[g], axis=-1, keepdims=True) for g in heads]
        if i > 0:
            m = [jnp.maximum(m[g], jnp.max(past[g], axis=-1, keepdims=True)) for g in heads]
        ry = _dot(r_scores, rv) + carried
        k_dec = (rk.astype(F32) * zeta_ref[h]).astype(BF16)
        state_ref[...] = state_ref[...] * cd_ref[h] + _dot_tn(k_dec, rv)
        acc = [_dot(jnp.exp2(diag[g] - m[g]).astype(BF16), v_s[g, lo:hi, :]) for g in heads]
        if i > 0:
            acc = [acc[g] + _dot(jnp.exp2(past[g] - m[g]).astype(BF16), v_s[g, :lo, :]) for g in heads]
        if i < 3:
            src, dst = ((w1_ref, w1b_ref), (w3_ref, w3b_ref), (w2_ref, w2b_ref))[i]
            dst[...] = src[...].astype(BF16)
        ryc = ry - jnp.mean(ry, axis=-1, keepdims=True)
        r_inv = lax.rsqrt(jnp.mean(ryc * ryc, axis=-1, keepdims=True) + EPS)
        yr_ref[lo:hi, :] = (rg_ref[lo:hi, :].astype(F32) * (ryc * r_inv)).astype(BF16)
        for g in heads:
            o_ref[lo:hi, g * MLA_V:(g + 1) * MLA_V] = (acc[g][:, :MLA_V] / acc[g][:, MLA_V:]).astype(BF16)


def _mla(lat3, wq, wkv, cos, sin, ret3, dec, xi, zeta, cd, w1, w3, w2):
    hps = MLA_HPS
    steps = MLA_HEADS // hps
    assert steps == RET_HEADS and TQ == RET_CHUNK and BATCH * steps == N_EXPERTS
    expert = lambda w: pl.BlockSpec((None,) + w.shape[1:], lambda b, h: (b * steps + h, 0, 0))
    ret_part = lambda part: pl.BlockSpec((None, SEQ, RET_DK), lambda b, h: (b, 0, part * RET_HEADS + h))
    whole = lambda a: pl.BlockSpec(a.shape, lambda b, h: (0,) * a.ndim)
    return pl.pallas_call(
        _mla_kernel,
        grid=(BATCH, MLA_HEADS // hps),
        in_specs=[pl.BlockSpec((None, SEQ, MLA_LAT_W), lambda b, h: (b, 0, 0)),
                  pl.BlockSpec((hps, MLA_Q_LORA, MLA_QK + MLA_ROPE), lambda b, h: (h, 0, 0)),
                  pl.BlockSpec((hps, MLA_KV_LORA, MLA_NOPE + MLA_V), lambda b, h: (h, 0, 0)),
                  pl.BlockSpec((SEQ, MLA_ROPE), lambda b, h: (0, 0)),
                  pl.BlockSpec((SEQ, MLA_ROPE), lambda b, h: (0, 0)),
                  ret_part(0), ret_part(1), ret_part(2), ret_part(3),
                  whole(dec), whole(xi), whole(zeta), whole(cd), expert(w1), expert(w3), expert(w2)],
        out_specs=[pl.BlockSpec((None, SEQ, hps * MLA_V), lambda b, h: (b, 0, h)),
                   pl.BlockSpec((None, SEQ, RET_DV), lambda b, h: (b, 0, h)),
                   expert(w1), expert(w3), expert(w2)],
        out_shape=[jax.ShapeDtypeStruct((BATCH, SEQ, MLA_HEADS * MLA_V), BF16),
                   jax.ShapeDtypeStruct((BATCH, SEQ, RET_HEADS * RET_DV), BF16)]
                  + [jax.ShapeDtypeStruct(w.shape, BF16) for w in (w1, w3, w2)],
        scratch_shapes=[pltpu.VMEM((hps, SEQ, MLA_QK), BF16),
                        pltpu.VMEM((hps, SEQ, MLA_QK), BF16),
                        pltpu.VMEM((hps, SEQ, 2 * MLA_V), BF16),
                        pltpu.VMEM((RET_DK, RET_DV), F32)],
        compiler_params=_params("arbitrary", "arbitrary"),
        name="mla",
    )(lat3, wq, wkv, cos, sin, ret3, ret3, ret3, ret3, dec, xi, zeta, cd, w1, w3, w2)


def _route(logits_t):
    tm = logits_t.shape[1]
    row = lax.broadcasted_iota(jnp.int32, (SUBLANES, tm), 0)
    neg = -jnp.inf
    gl = jnp.where(row < N_GROUPS, logits_t[:SUBLANES], neg)
    gmax = jnp.max(gl, axis=0, keepdims=True)
    gsel = jnp.min(jnp.where(gl == gmax, row, SUBLANES), axis=0, keepdims=True)
    p_grp = 1.0 / jnp.sum(jnp.exp(gl - gmax), axis=0, keepdims=True)
    el = logits_t[SUBLANES * N_GROUPS:SUBLANES * (N_GROUPS + 1)]
    for g in reversed(range(N_GROUPS - 1)):
        el = jnp.where(gsel == g, logits_t[SUBLANES * (g + 1):SUBLANES * (g + 2)], el)
    v0 = jnp.max(el, axis=0, keepdims=True)
    i0 = jnp.min(jnp.where(el == v0, row, SUBLANES), axis=0, keepdims=True)
    el1 = jnp.where(row == i0, neg, el)
    v1 = jnp.max(el1, axis=0, keepdims=True)
    i1 = jnp.min(jnp.where(el1 == v1, row, SUBLANES), axis=0, keepdims=True)
    t = jnp.exp(v1 - v0)
    w0 = p_grp / (1.0 + t)
    w1 = p_grp * t / (1.0 + t)
    return gsel * EXPERTS_PER_GROUP + i0, gsel * EXPERTS_PER_GROUP + i1, w0, w1


def _stack_rows(rows, n):
    tm = rows[0].shape[1]
    row = lax.broadcasted_iota(jnp.int32, (n, tm), 0)
    out = jnp.zeros((n, tm), F32)
    for k, r in enumerate(rows):
        out = jnp.where(row == k, r, out)
    return out


def _to_token_tiles(ref, val):
    n = val.shape[0]
    for s in range(ROW_TILES):
        ref[pl.ds(s, n, stride=ROW_TILES), :] = val[:, s * LANES:(s + 1) * LANES]


def _from_token_tiles(ref):
    n = ref.shape[0] // ROW_TILES
    return jnp.concatenate([ref[pl.ds(s, n, stride=ROW_TILES), :] for s in range(ROW_TILES)], axis=-1)


def _token_rows(ref, t):
    return ref.at[pl.ds(pl.multiple_of(t * ROW_TILES, ROW_TILES), ROW_TILES)]


def _merge_kernel(yr_ref, at_ref, gr_ref, gm_ref, x_ref, mod_ref, n2_ref, wro_ref, wmo_ref, wo_ref,
                  wrt_ref, brt_ref, h1_ref, u2_ref, meta_ref, wtok_ref, cnt_ref, carry_ref, wro_s, wmo_s, wo_s):
    tm = x_ref.shape[0]

    @pl.when(pl.program_id(0) == 0)
    def _():
        carry_ref[...] = jnp.zeros_like(carry_ref)
        wro_s[...] = wro_ref[...].astype(BF16)
        wmo_s[...] = wmo_ref[...].astype(BF16)
        wo_s[...] = wo_ref[...].astype(BF16)

    chunks = range(MERGE_CHAINS)
    sub = tm // MERGE_CHAINS
    rows = [slice(c * sub, (c + 1) * sub) for c in chunks]
    y_ret = [_dot(yr_ref[r, :], wro_s[...]) for r in rows]
    y_mla = [_dot(at_ref[r, :], wmo_s[...]) for r in rows]
    merged = [(_sigmoid(gr_ref[rows[c], :].astype(F32)) * y_ret[c]
               + _sigmoid(gm_ref[rows[c], :].astype(F32)) * y_mla[c]).astype(BF16) for c in chunks]
    o = [_dot(merged[c], wo_s[...]) for c in chunks]
    h1 = [x_ref[rows[c], :] + mod_ref[2:3, :] * o[c] for c in chunks]
    for c in chunks:
        h1_ref[rows[c], :] = h1[c]
    u2 = [_rms(h1[c]) * n2_ref[...] * (1.0 + mod_ref[4:5, :]) + mod_ref[3:4, :] for c in chunks]
    for c in chunks:
        _to_token_tiles(u2_ref.at[pl.ds(c * sub * ROW_TILES, sub * ROW_TILES)], u2[c])
    w = wrt_ref[...]
    w_hi = w.astype(BF16)
    w_lo = (w - w_hi.astype(F32)).astype(BF16)
    w_both = jnp.concatenate([w_hi, w_lo], axis=0)
    u_hi = [u2[c].astype(BF16) for c in chunks]
    u_lo = [(u2[c] - u_hi[c].astype(F32)).astype(BF16) for c in chunks]
    by_hi = [_dot_nt(w_both, u_hi[c]) for c in chunks]
    logits_t = [by_hi[c][:ROUTER_ROWS] + by_hi[c][ROUTER_ROWS:] + _dot_nt(w_hi, u_lo[c]) + brt_ref[:, 0:1]
                for c in chunks]
    routed = [_route(logits_t[c]) for c in chunks]
    e0, e1, w0, w1 = [jnp.concatenate([routed[c][k] for c in chunks], axis=1) for k in range(4)]
    erow = lax.broadcasted_iota(jnp.int32, (N_EXPERTS, tm), 0)
    m0, m1 = erow == e0, erow == e1
    member = jnp.where(m0 | m1, 1.0, 0.0)
    earlier = jnp.where(lax.broadcasted_iota(jnp.int32, (tm, tm), 0) < lax.broadcasted_iota(jnp.int32, (tm, tm), 1),
                        1.0, 0.0).astype(BF16)
    prefix = _dot(member.astype(BF16), earlier) + carry_ref[:, 0:1]
    rank0 = jnp.sum(jnp.where(m0, prefix, 0.0), axis=0, keepdims=True)
    rank1 = jnp.sum(jnp.where(m1, prefix, 0.0), axis=0, keepdims=True)
    carry_ref[...] = carry_ref[...] + jnp.sum(member, axis=1, keepdims=True)
    cnt_ref[...] = carry_ref[...]
    meta_ref[...] = _stack_rows([e0.astype(F32), e1.astype(F32), rank0, rank1], SUBLANES)
    wt = _stack_rows([w0, w1], 2 * SUBLANES)
    wt_hi = wt.astype(BF16)
    wt_lo = (wt - wt_hi.astype(F32)).astype(BF16)
    place = jnp.where(lax.broadcasted_iota(jnp.int32, (2 * SUBLANES, LANES), 0)
                      == lax.broadcasted_iota(jnp.int32, (2 * SUBLANES, LANES), 1), 1.0, 0.0).astype(BF16)
    wtok_ref[...] = _dot_tn(wt_hi, place) + _dot_tn(wt_lo, place)


def _merge(y_ret, attn, gates, x2, mod3, norm2, w_ret_o, w_mla_o, w_out, w_rt, b_rt):
    tm = TM_PROJ
    per_b = SEQ // tm
    row = lambda j: pl.BlockSpec((tm, D_MODEL), lambda i: (i, j))
    return pl.pallas_call(
        _merge_kernel,
        grid=(N_TOK // tm,),
        in_specs=[row(0), row(0), row(0), row(1), row(0),
                  pl.BlockSpec((None, 6, D_MODEL), lambda i: (i // per_b, 0, 0)),
                  _resident((1, D_MODEL)),
                  _layer_weight(w_ret_o), _layer_weight(w_mla_o), _layer_weight(w_out),
                  _resident(w_rt.shape), _resident(b_rt.shape)],
        out_specs=[row(0),
                   pl.BlockSpec((tm * ROW_TILES, LANES), lambda i: (i, 0)),
                   pl.BlockSpec((SUBLANES, tm), lambda i: (0, i)),
                   pl.BlockSpec((tm, LANES), lambda i: (i, 0)),
                   pl.BlockSpec((N_EXPERTS, LANES), lambda i: (0, 0))],
        out_shape=[jax.ShapeDtypeStruct((N_TOK, D_MODEL), F32),
                   jax.ShapeDtypeStruct((N_TOK * ROW_TILES, LANES), F32),
                   jax.ShapeDtypeStruct((SUBLANES, N_TOK), F32),
                   jax.ShapeDtypeStruct((N_TOK, LANES), F32),
                   jax.ShapeDtypeStruct((N_EXPERTS, LANES), F32)],
        scratch_shapes=[pltpu.VMEM((N_EXPERTS, LANES), F32)] + [pltpu.VMEM((D_MODEL, D_MODEL), BF16)] * 3,
        compiler_params=_params("arbitrary"),
        name="merge",
    )(y_ret, attn, gates, gates, x2, mod3, norm2, w_ret_o, w_mla_o, w_out, w_rt, b_rt)


def _plan_kernel(meta_ref, off_ref, dst_ref):
    m = meta_ref[...]
    erow = lax.broadcasted_iota(jnp.int32, (N_EXPERTS, m.shape[1]), 0)
    off = off_ref[:, 0:1]
    d = [jnp.sum(jnp.where(erow == m[k:k + 1].astype(jnp.int32), off, 0.0), axis=0, keepdims=True) + m[k + 2:k + 3]
         for k in range(TOP_K)]
    dst_ref[...] = _stack_rows(d, SUBLANES).astype(jnp.int32)


def _plan(meta_t, off_col):
    tm = 2048
    return pl.pallas_call(
        _plan_kernel,
        grid=(N_TOK // tm,),
        in_specs=[pl.BlockSpec((SUBLANES, tm), lambda i: (0, i)),
                  pl.BlockSpec((N_EXPERTS, LANES), lambda i: (0, 0))],
        out_specs=pl.BlockSpec((SUBLANES, tm), lambda i: (0, i)),
        out_shape=jax.ShapeDtypeStruct((SUBLANES, N_TOK), jnp.int32),
        compiler_params=_params("arbitrary"),
        name="plan",
    )(meta_t, off_col)


def _row_copy_wait(src_like, dst_like, sem):
    pltpu.make_async_copy(src_like, dst_like, sem).wait()


def _dispatch_kernel(d0_ref, d1_ref, seg_ref, cnt_ref, u_ref, xs_ref, zero_ref, sem, zsem):
    i = pl.program_id(0)
    tm = u_ref.shape[0] // ROW_TILES
    tile_rows = TE * ROW_TILES

    def slot_tile(j):
        return xs_ref.at[pl.ds(pl.multiple_of(j * tile_rows, tile_rows), tile_rows)]

    def pad(e):
        first = seg_ref[e] + cnt_ref[e]
        rows = (seg_ref[e + 1] - first) * ROW_TILES
        start = pl.multiple_of(first * ROW_TILES, ROW_TILES)
        return rows > 0, pltpu.make_async_copy(zero_ref.at[pl.ds(0, rows)], xs_ref.at[pl.ds(start, rows)], zsem)

    def unused(j):
        return pltpu.make_async_copy(zero_ref, slot_tile(j), zsem)

    first_unused = seg_ref[N_EXPERTS] // TE

    @pl.when(i == 0)
    def _():
        zero_ref[...] = jnp.zeros_like(zero_ref)
        for e in range(N_EXPERTS):
            nonempty, copy = pad(e)
            pl.when(nonempty)(copy.start)
        lax.fori_loop(first_unused, N_TILES, lambda j, c: (unused(j).start(), c)[1], 0)

    base = i * tm

    def body(r, carry):
        src = _token_rows(u_ref, r)
        pltpu.make_async_copy(src, _token_rows(xs_ref, d0_ref[base + r]), sem).start(priority=0)
        pltpu.make_async_copy(src, _token_rows(xs_ref, d1_ref[base + r]), sem).start(priority=1)
        return carry

    lax.fori_loop(0, tm, body, 0, unroll=8)
    _row_copy_wait(u_ref, xs_ref.at[pl.ds(0, tm * ROW_TILES)], sem)
    _row_copy_wait(u_ref, xs_ref.at[pl.ds(0, tm * ROW_TILES)], sem)

    @pl.when(i == 0)
    def _():
        for e in range(N_EXPERTS):
            nonempty, copy = pad(e)
            pl.when(nonempty)(copy.wait)
        lax.fori_loop(first_unused, N_TILES, lambda j, c: (unused(j).wait(), c)[1], 0)


def _dispatch(d0, d1, seg, cnt, u2t):
    tm = TM_DISPATCH
    return pl.pallas_call(
        _dispatch_kernel,
        grid_spec=pltpu.PrefetchScalarGridSpec(
            num_scalar_prefetch=4,
            grid=(N_TOK // tm,),
            in_specs=[pl.BlockSpec((tm * ROW_TILES, LANES), lambda i, *_: (i, 0))],
            out_specs=pl.BlockSpec(memory_space=pl.ANY),
            scratch_shapes=[pltpu.VMEM((TE * ROW_TILES, LANES), F32),
                            pltpu.SemaphoreType.DMA(()), pltpu.SemaphoreType.DMA(())]),
        out_shape=jax.ShapeDtypeStruct((N_SLOTS * ROW_TILES, LANES), F32),
        compiler_params=_params("arbitrary"),
        name="dispatch",
    )(d0, d1, seg, cnt, u2t)


def _expert_kernel(te_ref, nv_ref, x_ref, w1_s, w3_s, w2_s, y_ref):
    j = pl.program_id(0)

    @pl.when(j < nv_ref[0])
    def _():
        sub = TE // TE_CHAINS * ROW_TILES
        part = lambda ref, c: ref.at[pl.ds(c * sub, sub)]
        chains = range(TE_CHAINS)
        x = [_from_token_tiles(part(x_ref, c)).astype(BF16) for c in chains]
        a = [_dot(x[c], w1_s[...]) for c in chains]
        b = [_dot(x[c], w3_s[...]) for c in chains]
        hid = [(a[c] * _sigmoid(a[c]) * b[c]).astype(BF16) for c in chains]
        for c in chains:
            _to_token_tiles(part(y_ref, c), _dot(hid[c], w2_s[...]))


def _experts(tile_expert, n_valid, xs, w1, w3, w2):
    tile = lambda j, te, nv: jnp.minimum(j, nv[0] - 1)
    wspec = lambda shape: pl.BlockSpec((None,) + shape, lambda j, te, nv: (te[tile(j, te, nv)], 0, 0))
    slots = pl.BlockSpec((TE * ROW_TILES, LANES), lambda j, te, nv: (tile(j, te, nv), 0))
    return pl.pallas_call(
        _expert_kernel,
        grid_spec=pltpu.PrefetchScalarGridSpec(
            num_scalar_prefetch=2,
            grid=(N_TILES,),
            in_specs=[slots, wspec((D_MODEL, D_EXPERT)), wspec((D_MODEL, D_EXPERT)), wspec((D_EXPERT, D_MODEL))],
            out_specs=slots),
        out_shape=jax.ShapeDtypeStruct((N_SLOTS * ROW_TILES, LANES), F32),
        input_output_aliases={2: 0},
        compiler_params=_params("arbitrary"),
        name="experts",
    )(tile_expert, n_valid, xs, w1, w3, w2)


def _final_kernel(d0_ref, d1_ref, h1_ref, meta_ref, mod_ref, fn_ref, ys_ref, o_ref, ybuf, sem):
    i = pl.program_id(0)
    tm = h1_ref.shape[0]

    def gather(t):
        buf, s = ybuf.at[t % 2], sem.at[t % 2]
        base = t * tm

        def body(r, carry):
            pltpu.make_async_copy(_token_rows(ys_ref, d0_ref[base + r]), _token_rows(buf.at[0], r), s).start(priority=0)
            pltpu.make_async_copy(_token_rows(ys_ref, d1_ref[base + r]), _token_rows(buf.at[1], r), s).start(priority=1)
            return carry

        lax.fori_loop(0, tm, body, 0, unroll=8)

    @pl.when(i == 0)
    def _():
        gather(0)

    @pl.when(i + 1 < pl.num_programs(0))
    def _():
        gather(i + 1)

    buf = ybuf.at[i % 2]
    for k in range(TOP_K):
        _row_copy_wait(ys_ref.at[pl.ds(0, tm * ROW_TILES)], buf.at[k], sem.at[i % 2])
    m = meta_ref[...]
    moe = m[:, 0:1] * _from_token_tiles(buf.at[0]) + m[:, 1:2] * _from_token_tiles(buf.at[1])
    h2 = h1_ref[...] + mod_ref[5:6, :] * moe
    o_ref[...] = _rms(h2) * fn_ref[...]


def _final(d0, d1, h1, meta, mod3, final_norm, ys):
    tm = TM_FINAL
    per_b = SEQ // tm
    return pl.pallas_call(
        _final_kernel,
        grid_spec=pltpu.PrefetchScalarGridSpec(
            num_scalar_prefetch=2,
            grid=(N_TOK // tm,),
            in_specs=[pl.BlockSpec((tm, D_MODEL), lambda i, *_: (i, 0)),
                      pl.BlockSpec((tm, LANES), lambda i, *_: (i, 0)),
                      pl.BlockSpec((None, 6, D_MODEL), lambda i, *_: (i // per_b, 0, 0)),
                      pl.BlockSpec((1, D_MODEL), lambda i, *_: (0, 0)),
                      pl.BlockSpec(memory_space=pl.ANY)],
            out_specs=pl.BlockSpec((tm, D_MODEL), lambda i, *_: (i, 0)),
            scratch_shapes=[pltpu.VMEM((2, TOP_K, tm * ROW_TILES, LANES), F32), pltpu.SemaphoreType.DMA((2,))]),
        out_shape=jax.ShapeDtypeStruct((N_TOK, D_MODEL), F32),
        compiler_params=_params("arbitrary"),
        name="final",
    )(d0, d1, h1, meta, mod3, final_norm, ys)


def _slot_layout(counts):
    cnt = counts[:, 0].astype(jnp.int32)
    tile_end = jnp.cumsum((cnt + TE - 1) // TE)
    seg = jnp.concatenate([jnp.zeros((1,), jnp.int32), tile_end * TE])
    off_col = jnp.broadcast_to(seg[:-1].astype(F32)[:, None], (N_EXPERTS, LANES))
    tile_ids = jnp.arange(N_TILES, dtype=jnp.int32)
    tile_expert = jnp.sum((tile_end[None, :] <= tile_ids[:, None]).astype(jnp.int32), axis=1)
    tile_expert = jnp.minimum(tile_expert, N_EXPERTS - 1)
    return cnt, seg, off_col, tile_expert, tile_end[-1:]


def _rope_tables(dim):
    pos = np.arange(SEQ, dtype=np.float64)
    inv = ROPE_THETA ** (-np.arange(0, dim, 2, dtype=np.float64) / dim)
    ang = pos[:, None] * inv[None, :]
    return np.cos(ang).astype(np.float32), np.sin(ang).astype(np.float32)


def _decay_tables():
    c = RET_CHUNK
    log_gamma = np.log1p(-np.exp2(-5.0 - np.arange(RET_HEADS, dtype=np.float64)))
    idx = np.arange(c, dtype=np.float64)
    rel = idx[:, None] - idx[None, :]
    dec = np.where(rel[None] >= 0, np.exp(log_gamma[:, None, None] * np.maximum(rel, 0.0)[None]), 0.0)
    xi = np.exp(log_gamma[:, None] * (idx[None, :] + 1.0))[:, :, None]
    zeta = np.exp(log_gamma[:, None] * (c - 1.0 - idx[None, :]))[:, :, None]
    cd = np.exp(log_gamma * c)[:, None, None]
    return tuple(jnp.asarray(t.astype(np.float32)) for t in (dec, xi, zeta, cd))


def _rotate_half_cols(w):
    half = w.shape[-1] // 2
    return jnp.concatenate([-w[..., half:], w[..., :half]], axis=-1)


def kernel(x, c, w_ada, b_ada, norm1, norm2, w_in, w_ret_o, q_norm, kv_norm, w_uq, w_ukv, w_mla_o, w_out,
           w_grp, b_grp, w_exp, b_exp, w1, w3, w2, final_norm):
    assert x.shape == (BATCH, SEQ, D_MODEL) and w_ada.shape[0] == 1
    x2 = x.reshape(N_TOK, D_MODEL)

    w_in_t = jnp.transpose(w_in[0])
    wq = w_uq[0].reshape(MLA_Q_LORA, MLA_HEADS, MLA_QK)
    wq = jnp.concatenate([wq, _rotate_half_cols(wq[..., MLA_NOPE:])], axis=-1)
    wq = wq.transpose(1, 0, 2).astype(BF16)
    wkv = w_ukv[0].reshape(MLA_KV_LORA, MLA_HEADS, MLA_NOPE + MLA_V).transpose(1, 0, 2).astype(BF16)
    gap = jnp.zeros((SUBLANES - N_GROUPS, D_MODEL), F32)
    tail = jnp.zeros((ROUTER_ROWS - SUBLANES - N_EXPERTS, D_MODEL), F32)
    w_rt = jnp.concatenate([w_grp[0].T, gap, w_exp[0].T, tail], axis=0)
    b_rt = jnp.concatenate([b_grp[0], gap[:, 0], b_exp[0], tail[:, 0]])
    b_rt = jnp.broadcast_to(b_rt[:, None], (ROUTER_ROWS, LANES))

    ret_cos, ret_sin = (jnp.asarray(t) for t in _rope_tables(RET_DK))
    mla_cos, mla_sin = (jnp.asarray(np.concatenate([t, t], axis=-1)) for t in _rope_tables(MLA_ROPE))
    dec, xi, zeta, cd = _decay_tables()

    mod3 = _ada(c, w_ada[0], b_ada[0]).reshape(BATCH, 6, D_MODEL)
    ret, lat, gates = _inproj(x2, mod3, norm1, ret_cos, ret_sin, mla_cos, mla_sin, q_norm, kv_norm, w_in_t)
    e_shape = (N_EXPERTS, D_MODEL, D_EXPERT)
    attn, y_ret, w1b, w3b, w2b = _mla(
        lat.reshape(BATCH, SEQ, MLA_LAT_W), wq, wkv, mla_cos, mla_sin, ret.reshape(BATCH, SEQ, 4 * RET_W),
        dec, xi, zeta, cd, w1[0].reshape(e_shape), w3[0].reshape(e_shape), w2[0].reshape(N_EXPERTS, D_EXPERT, D_MODEL))
    h1, u2t, meta_t, wtok, counts = _merge(y_ret.reshape(N_TOK, D_MODEL), attn.reshape(N_TOK, D_MODEL), gates, x2,
                                           mod3, norm2, w_ret_o, w_mla_o, w_out, w_rt, b_rt)
    cnt, seg, off_col, tile_expert, n_valid = _slot_layout(counts)
    dst = _plan(meta_t, off_col)
    d0, d1 = dst[0], dst[1]
    xs = _dispatch(d0, d1, seg, cnt, u2t)
    ys = _experts(tile_expert, n_valid, xs, w1b, w3b, w2b)
    out = _final(d0, d1, h1, wtok, mod3, final_norm.reshape(1, D_MODEL), ys)
    return out.reshape(BATCH, SEQ, D_MODEL)
```

```python
import numpy as np
import jax
import jax.numpy as jnp
from jax import lax
from jax.experimental import pallas as pl
from jax.experimental.pallas import tpu as pltpu

D_MODEL = 1024
BATCH = 8
SEQ = 2048
N_TOK = BATCH * SEQ

RET_HEADS = 4
RET_DK = 256
RET_DV = 256
RET_CHUNK = 256
RET_W = RET_HEADS * RET_DK

MLA_HEADS = 8
MLA_NOPE = 128
MLA_ROPE = 64
MLA_V = 128
MLA_Q_LORA = 384
MLA_KV_LORA = 256
MLA_LAT_W = MLA_Q_LORA + MLA_KV_LORA + 2 * MLA_ROPE
MLA_QK = MLA_NOPE + MLA_ROPE
ROPE_THETA = 10000.0

N_GROUPS = 4
EXPERTS_PER_GROUP = 8
N_EXPERTS = N_GROUPS * EXPERTS_PER_GROUP
D_EXPERT = 256
EPS = 1e-6
LOG2_E = 1.4426950408889634

LANES = 128
SUBLANES = 8
ROUTER_ROWS = 48
ROW_TILES = D_MODEL // LANES
VMEM_LIMIT = 56 * 1024 * 1024

TM_PROJ = 512
TM_INPROJ = 512
TM_FINAL = 512
TM_DISPATCH = 4096
TQ = 256
MLA_HPS = 2
TE = 512
TE_CHAINS = 2
MERGE_CHAINS = 4
TOP_K = 2
N_TILES = N_TOK * TOP_K // TE + N_EXPERTS
N_SLOTS = N_TILES * TE

F32 = jnp.float32
BF16 = jnp.bfloat16


def _sigmoid(x):
    return 1.0 / (1.0 + jnp.exp(-x))


def _rms(x):
    return x * lax.rsqrt(jnp.mean(x * x, axis=-1, keepdims=True) + EPS)


def _dot(a, b):
    return jnp.dot(a, b, preferred_element_type=F32)


def _dot_nt(a, b):
    return lax.dot_general(a, b, (((1,), (1,)), ((), ())), preferred_element_type=F32)


def _dot_tn(a, b):
    return lax.dot_general(a, b, (((0,), (0,)), ((), ())), preferred_element_type=F32)


def _params(*sem):
    return pltpu.CompilerParams(dimension_semantics=sem, vmem_limit_bytes=VMEM_LIMIT)


def _resident(shape):
    nd = len(shape)
    return pl.BlockSpec(shape, lambda *_: (0,) * nd, pipeline_mode=pl.Buffered(1))


def _layer_weight(w):
    return pl.BlockSpec((None,) + w.shape[1:], lambda *_: (0,) * w.ndim, pipeline_mode=pl.Buffered(1))


def _ada_kernel(c_ref, w_ref, b_ref, o_ref):
    c = c_ref[...]
    act = (c * _sigmoid(c)).astype(BF16)
    o_ref[...] = _dot(act, w_ref[...].astype(BF16)) + b_ref[...]


def _ada(c, w_ada, b_ada):
    n = w_ada.shape[1]
    tn = D_MODEL
    return pl.pallas_call(
        _ada_kernel,
        grid=(n // tn,),
        in_specs=[pl.BlockSpec((BATCH, D_MODEL), lambda j: (0, 0)),
                  pl.BlockSpec((D_MODEL, tn), lambda j: (0, j)),
                  pl.BlockSpec((1, tn), lambda j: (0, j))],
        out_specs=pl.BlockSpec((BATCH, tn), lambda j: (0, j)),
        out_shape=jax.ShapeDtypeStruct((BATCH, n), F32),
        compiler_params=_params("arbitrary"),
        name="ada",
    )(c, w_ada, b_ada.reshape(1, n))


O_LAT = 4 * RET_W
O_PE = O_LAT + MLA_Q_LORA + MLA_KV_LORA
O_GATE = O_PE + MLA_ROPE


N_IN = O_GATE + 2 * D_MODEL
WPREP_ROWS = 512


def _stage_weights(w_hbm, ret_ref, lat_ref, gate_ref, buf, sem):
    chunks = [(s, min(WPREP_ROWS, N_IN - s)) for s in range(0, N_IN, WPREP_ROWS)]

    def copy(i):
        s, n = chunks[i]
        return pltpu.make_async_copy(w_hbm.at[pl.ds(s, n)], buf.at[i % 2, pl.ds(0, n)], sem.at[i % 2])

    groups = [(0, RET_W, ret_ref, 0, 1.0), (RET_W, 2 * RET_W, ret_ref, RET_W, RET_DK ** -0.5),
              (2 * RET_W, O_LAT, ret_ref, 2 * RET_W, 1.0), (O_LAT, O_GATE, lat_ref, 0, 1.0),
              (O_GATE, N_IN, gate_ref, 0, 1.0)]
    half = MLA_ROPE // 2
    rot = [(O_PE + half, O_GATE, O_GATE - O_LAT, -1.0), (O_PE, O_PE + half, O_GATE - O_LAT + half, 1.0)]

    copy(0).start()
    for i, (s, n) in enumerate(chunks):
        if i + 1 < len(chunks):
            copy(i + 1).start()
        copy(i).wait()
        for lo, hi, dst, dst_lo, scale in groups + [(a, b, lat_ref, d, sc) for a, b, d, sc in rot]:
            a, b = max(lo, s), min(hi, s + n)
            if a < b:
                x = buf[i % 2, a - s:b - s, :]
                dst[dst_lo + a - lo:dst_lo + b - lo, :] = (x if scale == 1.0 else x * scale).astype(BF16)


def _inproj_kernel(x_ref, mod_ref, n1_ref, cos_ref, sin_ref, mcos_ref, msin_ref, qn_ref, kvn_ref, w_hbm,
                   ret_ref, lat_ref, gate_ref, wr_ref, wm_ref, wg_ref, stage_buf, stage_sem):
    @pl.when(pl.program_id(0) == 0)
    def _():
        _stage_weights(w_hbm, wr_ref, wm_ref, wg_ref, stage_buf, stage_sem)

    y = _rms(x_ref[...]) * n1_ref[...]
    u = (y * (1.0 + mod_ref[1:2, :]) + mod_ref[0:1, :]).astype(BF16)
    cos, sin = cos_ref[...], sin_ref[...]
    half = RET_DK // 2
    for n in range(0, 2 * RET_W, RET_DK):
        p = _dot_nt(u, wr_ref[n:n + RET_DK, :])
        x1, x2 = p[:, :half], p[:, half:]
        ret_ref[:, n:n + half] = (x1 * cos - x2 * sin).astype(BF16)
        ret_ref[:, n + half:n + RET_DK] = (x2 * cos + x1 * sin).astype(BF16)
    step = 512
    for n in range(2 * RET_W, 3 * RET_W, step):
        ret_ref[:, n:n + step] = _dot_nt(u, wr_ref[n:n + step, :]).astype(BF16)
    for n in range(3 * RET_W, 4 * RET_W, step):
        p = _dot_nt(u, wr_ref[n:n + step, :])
        ret_ref[:, n:n + step] = (p * _sigmoid(p)).astype(BF16)
    lat = _dot_nt(u, wm_ref[...])
    o_kv, o_pe, o_rot = MLA_Q_LORA, MLA_Q_LORA + MLA_KV_LORA, MLA_Q_LORA + MLA_KV_LORA + MLA_ROPE
    lat_ref[:, :o_kv] = (_rms(lat[:, :o_kv]) * qn_ref[...]).astype(BF16)
    lat_ref[:, o_kv:o_pe] = (_rms(lat[:, o_kv:o_pe]) * kvn_ref[...]).astype(BF16)
    lat_ref[:, o_pe:o_rot] = (lat[:, o_pe:o_rot] * mcos_ref[...] + lat[:, o_rot:] * msin_ref[...]).astype(BF16)
    lat_ref[:, o_rot:] = jnp.zeros((lat.shape[0], MLA_ROPE), BF16)
    for n in range(0, 2 * D_MODEL, step):
        gate_ref[:, n:n + step] = _dot_nt(u, wg_ref[n:n + step, :]).astype(BF16)


def _inproj(x2, mod3, norm1, cos, sin, mla_cos, mla_sin, q_norm, kv_norm, w_in_t):
    tm = TM_INPROJ
    per_b = SEQ // tm
    rope_tab = pl.BlockSpec((tm, RET_DK // 2), lambda i: (i % per_b, 0))
    mla_tab = pl.BlockSpec((tm, MLA_ROPE), lambda i: (i % per_b, 0))
    return pl.pallas_call(
        _inproj_kernel,
        grid=(N_TOK // tm,),
        in_specs=[pl.BlockSpec((tm, D_MODEL), lambda i: (i, 0)),
                  pl.BlockSpec((None, 6, D_MODEL), lambda i: (i // per_b, 0, 0)),
                  _resident((1, D_MODEL)), rope_tab, rope_tab, mla_tab, mla_tab,
                  _resident(q_norm.shape), _resident(kv_norm.shape),
                  pl.BlockSpec(memory_space=pl.ANY)],
        out_specs=[pl.BlockSpec((tm, 4 * RET_W), lambda i: (i, 0)),
                   pl.BlockSpec((tm, MLA_LAT_W), lambda i: (i, 0)),
                   pl.BlockSpec((tm, 2 * D_MODEL), lambda i: (i, 0))],
        out_shape=[jax.ShapeDtypeStruct((N_TOK, 4 * RET_W), BF16),
                   jax.ShapeDtypeStruct((N_TOK, MLA_LAT_W), BF16),
                   jax.ShapeDtypeStruct((N_TOK, 2 * D_MODEL), BF16)],
        scratch_shapes=[pltpu.VMEM((O_LAT, D_MODEL), BF16), pltpu.VMEM((MLA_LAT_W, D_MODEL), BF16),
                        pltpu.VMEM((2 * D_MODEL, D_MODEL), BF16),
                        pltpu.VMEM((2, WPREP_ROWS, D_MODEL), F32), pltpu.SemaphoreType.DMA((2,))],
        compiler_params=_params("arbitrary"),
        name="inproj",
    )(x2, mod3, norm1, cos, sin, mla_cos, mla_sin, q_norm, kv_norm, w_in_t)


def _mla_kernel(lat_ref, wq_ref, wkv_ref, cos_ref, sin_ref,
                rq_ref, rk_ref, rv_ref, rg_ref, dec_ref, xi_ref, zeta_ref, cd_ref, w1_ref, w3_ref, w2_ref,
                o_ref, yr_ref, w1b_ref, w3b_ref, w2b_ref, slots_ref, q_s, k_s, v_s, state_ref, zero_s, zsem):
    h = pl.program_id(1)
    step = pl.program_id(0) * pl.num_programs(1) + h
    tile_rows = TE * ROW_TILES
    per_step = N_TILES // N_EXPERTS

    def zero_fill(t):
        start = pl.multiple_of((step * per_step + t) * tile_rows, tile_rows)
        return pltpu.make_async_copy(zero_s, slots_ref.at[pl.ds(start, tile_rows)], zsem)

    @pl.when(step == 0)
    def _():
        zero_s[...] = jnp.zeros_like(zero_s)

    for t in range(per_step):
        zero_fill(t).start()
    o_kv, o_pe, o_rot = MLA_Q_LORA, MLA_Q_LORA + MLA_KV_LORA, MLA_Q_LORA + MLA_KV_LORA + MLA_ROPE
    cos, sin = cos_ref[...], sin_ref[...]

    scale = (MLA_QK ** -0.5) * LOG2_E
    for g in range(MLA_HPS):
        qf = _dot(lat_ref[:, :o_kv], wq_ref[g])
        q_s[g, :, :MLA_NOPE] = (qf[:, :MLA_NOPE] * scale).astype(BF16)
        q_pe = qf[:, MLA_NOPE:MLA_QK] * cos + qf[:, MLA_QK:] * sin
        q_s[g, :, MLA_NOPE:] = (q_pe * scale).astype(BF16)
        kvf = _dot(lat_ref[:, o_kv:o_pe], wkv_ref[g])
        k_s[g, :, :MLA_NOPE] = kvf[:, :MLA_NOPE].astype(BF16)
        k_s[g, :, MLA_NOPE:] = lat_ref[:, o_pe:o_rot]
        v_s[g, :, :MLA_V] = kvf[:, MLA_NOPE:].astype(BF16)
        v_s[g, :, MLA_V:] = jnp.ones((SEQ, MLA_V), BF16)

    causal = lax.broadcasted_iota(jnp.int32, (TQ, TQ), 0) >= lax.broadcasted_iota(jnp.int32, (TQ, TQ), 1)
    heads = range(MLA_HPS)
    n_blk = SEQ // TQ

    def scores(i):
        lo, hi = i * TQ, (i + 1) * TQ
        diag = [jnp.where(causal, _dot_nt(q_s[g, lo:hi, :], k_s[g, lo:hi, :]), -jnp.inf) for g in heads]
        past = [_dot_nt(q_s[g, lo:hi, :], k_s[g, :lo, :]) if i > 0 else None for g in heads]
        return diag, past

    state_ref[...] = jnp.zeros_like(state_ref)
    pending = scores(0)
    for i in range(n_blk):
        lo, hi = i * TQ, (i + 1) * TQ
        diag, past = pending
        if i + 1 < n_blk:
            pending = scores(i + 1)
        rq, rk, rv = rq_ref[lo:hi, :], rk_ref[lo:hi, :], rv_ref[lo:hi, :]
        r_scores = (_dot_nt(rq, rk) * dec_ref[h]).astype(BF16)
        carried = xi_ref[h] * _dot(rq, state_ref[...].astype(BF16))
        m = [jnp.max(diag[g], axis=-1, keepdims=True) for g in heads]
        if i > 0:
            m = [jnp.maximum(m[g], jnp.max(past[g], axis=-1, keepdims=True)) for g in heads]
        ry = _dot(r_scores, rv) + carried
        k_dec = (rk.astype(F32) * zeta_ref[h]).astype(BF16)
        state_ref[...] = state_ref[...] * cd_ref[h] + _dot_tn(k_dec, rv)
        acc = [_dot(jnp.exp2(diag[g] - m[g]).astype(BF16), v_s[g, lo:hi, :]) for g in heads]
        if i > 0:
            acc = [acc[g] + _dot(jnp.exp2(past[g] - m[g]).astype(BF16), v_s[g, :lo, :]) for g in heads]
        if i < 3:
            src, dst = ((w1_ref, w1b_ref), (w3_ref, w3b_ref), (w2_ref, w2b_ref))[i]
            dst[...] = src[...].astype(BF16)
        ryc = ry - jnp.mean(ry, axis=-1, keepdims=True)
        r_inv = lax.rsqrt(jnp.mean(ryc * ryc, axis=-1, keepdims=True) + EPS)
        yr_ref[lo:hi, :] = (rg_ref[lo:hi, :].astype(F32) * (ryc * r_inv)).astype(BF16)
        for g in heads:
            o_ref[lo:hi, g * MLA_V:(g + 1) * MLA_V] = (acc[g][:, :MLA_V] / acc[g][:, MLA_V:]).astype(BF16)
    for t in range(per_step):
        zero_fill(t).wait()


def _mla(lat3, wq, wkv, cos, sin, ret3, dec, xi, zeta, cd, w1, w3, w2):
    hps = MLA_HPS
    steps = MLA_HEADS // hps
    assert steps == RET_HEADS and TQ == RET_CHUNK and BATCH * steps == N_EXPERTS and N_TILES % N_EXPERTS == 0
    expert = lambda w: pl.BlockSpec((None,) + w.shape[1:], lambda b, h: (b * steps + h, 0, 0))
    ret_part = lambda part: pl.BlockSpec((None, SEQ, RET_DK), lambda b, h: (b, 0, part * RET_HEADS + h))
    whole = lambda a: pl.BlockSpec(a.shape, lambda b, h: (0,) * a.ndim)
    return pl.pallas_call(
        _mla_kernel,
        grid=(BATCH, MLA_HEADS // hps),
        in_specs=[pl.BlockSpec((None, SEQ, MLA_LAT_W), lambda b, h: (b, 0, 0)),
                  pl.BlockSpec((hps, MLA_Q_LORA, MLA_QK + MLA_ROPE), lambda b, h: (h, 0, 0)),
                  pl.BlockSpec((hps, MLA_KV_LORA, MLA_NOPE + MLA_V), lambda b, h: (h, 0, 0)),
                  pl.BlockSpec((SEQ, MLA_ROPE), lambda b, h: (0, 0)),
                  pl.BlockSpec((SEQ, MLA_ROPE), lambda b, h: (0, 0)),
                  ret_part(0), ret_part(1), ret_part(2), ret_part(3),
                  whole(dec), whole(xi), whole(zeta), whole(cd), expert(w1), expert(w3), expert(w2)],
        out_specs=[pl.BlockSpec((None, SEQ, hps * MLA_V), lambda b, h: (b, 0, h)),
                   pl.BlockSpec((None, SEQ, RET_DV), lambda b, h: (b, 0, h)),
                   expert(w1), expert(w3), expert(w2), pl.BlockSpec(memory_space=pl.ANY)],
        out_shape=[jax.ShapeDtypeStruct((BATCH, SEQ, MLA_HEADS * MLA_V), BF16),
                   jax.ShapeDtypeStruct((BATCH, SEQ, RET_HEADS * RET_DV), BF16)]
                  + [jax.ShapeDtypeStruct(w.shape, BF16) for w in (w1, w3, w2)]
                  + [jax.ShapeDtypeStruct((N_SLOTS * ROW_TILES, LANES), F32)],
        scratch_shapes=[pltpu.VMEM((hps, SEQ, MLA_QK), BF16),
                        pltpu.VMEM((hps, SEQ, MLA_QK), BF16),
                        pltpu.VMEM((hps, SEQ, 2 * MLA_V), BF16),
                        pltpu.VMEM((RET_DK, RET_DV), F32),
                        pltpu.VMEM((TE * ROW_TILES, LANES), F32), pltpu.SemaphoreType.DMA(())],
        compiler_params=_params("arbitrary", "arbitrary"),
        name="mla",
    )(lat3, wq, wkv, cos, sin, ret3, ret3, ret3, ret3, dec, xi, zeta, cd, w1, w3, w2)


def _route(logits_t):
    tm = logits_t.shape[1]
    row = lax.broadcasted_iota(jnp.int32, (SUBLANES, tm), 0)
    neg = -jnp.inf
    gl = jnp.where(row < N_GROUPS, logits_t[:SUBLANES], neg)
    gmax = jnp.max(gl, axis=0, keepdims=True)
    gsel = jnp.min(jnp.where(gl == gmax, row, SUBLANES), axis=0, keepdims=True)
    p_grp = 1.0 / jnp.sum(jnp.exp(gl - gmax), axis=0, keepdims=True)
    el = logits_t[SUBLANES * N_GROUPS:SUBLANES * (N_GROUPS + 1)]
    for g in reversed(range(N_GROUPS - 1)):
        el = jnp.where(gsel == g, logits_t[SUBLANES * (g + 1):SUBLANES * (g + 2)], el)
    v0 = jnp.max(el, axis=0, keepdims=True)
    i0 = jnp.min(jnp.where(el == v0, row, SUBLANES), axis=0, keepdims=True)
    el1 = jnp.where(row == i0, neg, el)
    v1 = jnp.max(el1, axis=0, keepdims=True)
    i1 = jnp.min(jnp.where(el1 == v1, row, SUBLANES), axis=0, keepdims=True)
    t = jnp.exp(v1 - v0)
    w0 = p_grp / (1.0 + t)
    w1 = p_grp * t / (1.0 + t)
    return gsel * EXPERTS_PER_GROUP + i0, gsel * EXPERTS_PER_GROUP + i1, w0, w1


def _stack_rows(rows, n):
    tm = rows[0].shape[1]
    row = lax.broadcasted_iota(jnp.int32, (n, tm), 0)
    out = jnp.zeros((n, tm), F32)
    for k, r in enumerate(rows):
        out = jnp.where(row == k, r, out)
    return out


def _to_token_tiles(ref, val):
    n = val.shape[0]
    for s in range(ROW_TILES):
        ref[pl.ds(s, n, stride=ROW_TILES), :] = val[:, s * LANES:(s + 1) * LANES]


def _from_token_tiles(ref):
    n = ref.shape[0] // ROW_TILES
    return jnp.concatenate([ref[pl.ds(s, n, stride=ROW_TILES), :] for s in range(ROW_TILES)], axis=-1)


def _token_rows(ref, t):
    return ref.at[pl.ds(pl.multiple_of(t * ROW_TILES, ROW_TILES), ROW_TILES)]


def _merge_kernel(yr_ref, at_ref, gr_ref, gm_ref, x_ref, mod_ref, n2_ref, wro_ref, wmo_ref, wo_ref,
                  wrt_ref, brt_ref, h1_ref, u2_ref, meta_ref, wtok_ref, cnt_ref, carry_ref, wro_s, wmo_s, wo_s):
    tm = x_ref.shape[0]

    @pl.when(pl.program_id(0) == 0)
    def _():
        carry_ref[...] = jnp.zeros_like(carry_ref)
        wro_s[...] = wro_ref[...].astype(BF16)
        wmo_s[...] = wmo_ref[...].astype(BF16)
        wo_s[...] = wo_ref[...].astype(BF16)

    chunks = range(MERGE_CHAINS)
    sub = tm // MERGE_CHAINS
    rows = [slice(c * sub, (c + 1) * sub) for c in chunks]
    y_ret = [_dot(yr_ref[r, :], wro_s[...]) for r in rows]
    y_mla = [_dot(at_ref[r, :], wmo_s[...]) for r in rows]
    merged = [(_sigmoid(gr_ref[rows[c], :].astype(F32)) * y_ret[c]
               + _sigmoid(gm_ref[rows[c], :].astype(F32)) * y_mla[c]).astype(BF16) for c in chunks]
    o = [_dot(merged[c], wo_s[...]) for c in chunks]
    h1 = [x_ref[rows[c], :] + mod_ref[2:3, :] * o[c] for c in chunks]
    for c in chunks:
        h1_ref[rows[c], :] = h1[c]
    u2 = [_rms(h1[c]) * n2_ref[...] * (1.0 + mod_ref[4:5, :]) + mod_ref[3:4, :] for c in chunks]
    for c in chunks:
        _to_token_tiles(u2_ref.at[pl.ds(c * sub * ROW_TILES, sub * ROW_TILES)], u2[c])
    w = wrt_ref[...]
    w_hi = w.astype(BF16)
    w_lo = (w - w_hi.astype(F32)).astype(BF16)
    w_both = jnp.concatenate([w_hi, w_lo], axis=0)
    u_hi = [u2[c].astype(BF16) for c in chunks]
    u_lo = [(u2[c] - u_hi[c].astype(F32)).astype(BF16) for c in chunks]
    by_hi = [_dot_nt(w_both, u_hi[c]) for c in chunks]
    logits_t = [by_hi[c][:ROUTER_ROWS] + by_hi[c][ROUTER_ROWS:] + _dot_nt(w_hi, u_lo[c]) + brt_ref[:, 0:1]
                for c in chunks]
    routed = [_route(logits_t[c]) for c in chunks]
    e0, e1, w0, w1 = [jnp.concatenate([routed[c][k] for c in chunks], axis=1) for k in range(4)]
    erow = lax.broadcasted_iota(jnp.int32, (N_EXPERTS, tm), 0)
    m0, m1 = erow == e0, erow == e1
    member = jnp.where(m0 | m1, 1.0, 0.0)
    earlier = jnp.where(lax.broadcasted_iota(jnp.int32, (tm, tm), 0) < lax.broadcasted_iota(jnp.int32, (tm, tm), 1),
                        1.0, 0.0).astype(BF16)
    prefix = _dot(member.astype(BF16), earlier) + carry_ref[:, 0:1]
    rank0 = jnp.sum(jnp.where(m0, prefix, 0.0), axis=0, keepdims=True)
    rank1 = jnp.sum(jnp.where(m1, prefix, 0.0), axis=0, keepdims=True)
    carry_ref[...] = carry_ref[...] + jnp.sum(member, axis=1, keepdims=True)
    cnt_ref[...] = carry_ref[...]
    meta_ref[...] = _stack_rows([e0.astype(F32), e1.astype(F32), rank0, rank1], SUBLANES)
    wt = _stack_rows([w0, w1], 2 * SUBLANES)
    wt_hi = wt.astype(BF16)
    wt_lo = (wt - wt_hi.astype(F32)).astype(BF16)
    place = jnp.where(lax.broadcasted_iota(jnp.int32, (2 * SUBLANES, LANES), 0)
                      == lax.broadcasted_iota(jnp.int32, (2 * SUBLANES, LANES), 1), 1.0, 0.0).astype(BF16)
    wtok_ref[...] = _dot_tn(wt_hi, place) + _dot_tn(wt_lo, place)


def _merge(y_ret, attn, gates, x2, mod3, norm2, w_ret_o, w_mla_o, w_out, w_rt, b_rt):
    tm = TM_PROJ
    per_b = SEQ // tm
    row = lambda j: pl.BlockSpec((tm, D_MODEL), lambda i: (i, j))
    return pl.pallas_call(
        _merge_kernel,
        grid=(N_TOK // tm,),
        in_specs=[row(0), row(0), row(0), row(1), row(0),
                  pl.BlockSpec((None, 6, D_MODEL), lambda i: (i // per_b, 0, 0)),
                  _resident((1, D_MODEL)),
                  _layer_weight(w_ret_o), _layer_weight(w_mla_o), _layer_weight(w_out),
                  _resident(w_rt.shape), _resident(b_rt.shape)],
        out_specs=[row(0),
                   pl.BlockSpec((tm * ROW_TILES, LANES), lambda i: (i, 0)),
                   pl.BlockSpec((SUBLANES, tm), lambda i: (0, i)),
                   pl.BlockSpec((tm, LANES), lambda i: (i, 0)),
                   pl.BlockSpec((N_EXPERTS, LANES), lambda i: (0, 0))],
        out_shape=[jax.ShapeDtypeStruct((N_TOK, D_MODEL), F32),
                   jax.ShapeDtypeStruct((N_TOK * ROW_TILES, LANES), F32),
                   jax.ShapeDtypeStruct((SUBLANES, N_TOK), F32),
                   jax.ShapeDtypeStruct((N_TOK, LANES), F32),
                   jax.ShapeDtypeStruct((N_EXPERTS, LANES), F32)],
        scratch_shapes=[pltpu.VMEM((N_EXPERTS, LANES), F32)] + [pltpu.VMEM((D_MODEL, D_MODEL), BF16)] * 3,
        compiler_params=_params("arbitrary"),
        name="merge",
    )(y_ret, attn, gates, gates, x2, mod3, norm2, w_ret_o, w_mla_o, w_out, w_rt, b_rt)


def _plan_kernel(meta_ref, off_ref, dst_ref):
    m = meta_ref[...]
    erow = lax.broadcasted_iota(jnp.int32, (N_EXPERTS, m.shape[1]), 0)
    off = off_ref[:, 0:1]
    d = [jnp.sum(jnp.where(erow == m[k:k + 1].astype(jnp.int32), off, 0.0), axis=0, keepdims=True) + m[k + 2:k + 3]
         for k in range(TOP_K)]
    dst_ref[...] = _stack_rows(d, SUBLANES).astype(jnp.int32)


def _plan(meta_t, off_col):
    tm = 2048
    return pl.pallas_call(
        _plan_kernel,
        grid=(N_TOK // tm,),
        in_specs=[pl.BlockSpec((SUBLANES, tm), lambda i: (0, i)),
                  pl.BlockSpec((N_EXPERTS, LANES), lambda i: (0, 0))],
        out_specs=pl.BlockSpec((SUBLANES, tm), lambda i: (0, i)),
        out_shape=jax.ShapeDtypeStruct((SUBLANES, N_TOK), jnp.int32),
        compiler_params=_params("arbitrary"),
        name="plan",
    )(meta_t, off_col)


def _row_copy_wait(src_like, dst_like, sem):
    pltpu.make_async_copy(src_like, dst_like, sem).wait()


def _dispatch_kernel(d0_ref, d1_ref, u_ref, zeroed_ref, xs_ref, sem):
    del zeroed_ref
    i = pl.program_id(0)
    tm = u_ref.shape[0] // ROW_TILES
    base = i * tm

    def body(r, carry):
        src = _token_rows(u_ref, r)
        pltpu.make_async_copy(src, _token_rows(xs_ref, d0_ref[base + r]), sem).start(priority=0)
        pltpu.make_async_copy(src, _token_rows(xs_ref, d1_ref[base + r]), sem).start(priority=1)
        return carry

    lax.fori_loop(0, tm, body, 0, unroll=8)
    _row_copy_wait(u_ref, xs_ref.at[pl.ds(0, tm * ROW_TILES)], sem)
    _row_copy_wait(u_ref, xs_ref.at[pl.ds(0, tm * ROW_TILES)], sem)


def _dispatch(d0, d1, u2t, zeroed):
    tm = TM_DISPATCH
    return pl.pallas_call(
        _dispatch_kernel,
        grid_spec=pltpu.PrefetchScalarGridSpec(
            num_scalar_prefetch=2,
            grid=(N_TOK // tm,),
            in_specs=[pl.BlockSpec((tm * ROW_TILES, LANES), lambda i, *_: (i, 0)),
                      pl.BlockSpec(memory_space=pl.ANY)],
            out_specs=pl.BlockSpec(memory_space=pl.ANY),
            scratch_shapes=[pltpu.SemaphoreType.DMA(())]),
        out_shape=jax.ShapeDtypeStruct((N_SLOTS * ROW_TILES, LANES), F32),
        input_output_aliases={3: 0},
        compiler_params=_params("arbitrary"),
        name="dispatch",
    )(d0, d1, u2t, zeroed)


def _expert_kernel(te_ref, nv_ref, x_ref, w1_s, w3_s, w2_s, y_ref):
    j = pl.program_id(0)

    @pl.when(j < nv_ref[0])
    def _():
        sub = TE // TE_CHAINS * ROW_TILES
        part = lambda ref, c: ref.at[pl.ds(c * sub, sub)]
        chains = range(TE_CHAINS)
        x = [_from_token_tiles(part(x_ref, c)).astype(BF16) for c in chains]
        a = [_dot(x[c], w1_s[...]) for c in chains]
        b = [_dot(x[c], w3_s[...]) for c in chains]
        hid = [(a[c] * _sigmoid(a[c]) * b[c]).astype(BF16) for c in chains]
        for c in chains:
            _to_token_tiles(part(y_ref, c), _dot(hid[c], w2_s[...]))


def _experts(tile_expert, n_valid, xs, w1, w3, w2):
    tile = lambda j, te, nv: jnp.minimum(j, nv[0] - 1)
    wspec = lambda shape: pl.BlockSpec((None,) + shape, lambda j, te, nv: (te[tile(j, te, nv)], 0, 0))
    slots = pl.BlockSpec((TE * ROW_TILES, LANES), lambda j, te, nv: (tile(j, te, nv), 0))
    return pl.pallas_call(
        _expert_kernel,
        grid_spec=pltpu.PrefetchScalarGridSpec(
            num_scalar_prefetch=2,
            grid=(N_TILES,),
            in_specs=[slots, wspec((D_MODEL, D_EXPERT)), wspec((D_MODEL, D_EXPERT)), wspec((D_EXPERT, D_MODEL))],
            out_specs=slots),
        out_shape=jax.ShapeDtypeStruct((N_SLOTS * ROW_TILES, LANES), F32),
        input_output_aliases={2: 0},
        compiler_params=_params("arbitrary"),
        name="experts",
    )(tile_expert, n_valid, xs, w1, w3, w2)


def _final_kernel(d0_ref, d1_ref, h1_ref, meta_ref, mod_ref, fn_ref, ys_ref, o_ref, ybuf, sem):
    i = pl.program_id(0)
    tm = h1_ref.shape[0]

    def gather(t):
        buf, s = ybuf.at[t % 2], sem.at[t % 2]
        base = t * tm

        def body(r, carry):
            pltpu.make_async_copy(_token_rows(ys_ref, d0_ref[base + r]), _token_rows(buf.at[0], r), s).start(priority=0)
            pltpu.make_async_copy(_token_rows(ys_ref, d1_ref[base + r]), _token_rows(buf.at[1], r), s).start(priority=1)
            return carry

        lax.fori_loop(0, tm, body, 0, unroll=8)

    @pl.when(i == 0)
    def _():
        gather(0)

    @pl.when(i + 1 < pl.num_programs(0))
    def _():
        gather(i + 1)

    buf = ybuf.at[i % 2]
    for k in range(TOP_K):
        _row_copy_wait(ys_ref.at[pl.ds(0, tm * ROW_TILES)], buf.at[k], sem.at[i % 2])
    m = meta_ref[...]
    moe = m[:, 0:1] * _from_token_tiles(buf.at[0]) + m[:, 1:2] * _from_token_tiles(buf.at[1])
    h2 = h1_ref[...] + mod_ref[5:6, :] * moe
    o_ref[...] = _rms(h2) * fn_ref[...]


def _final(d0, d1, h1, meta, mod3, final_norm, ys):
    tm = TM_FINAL
    per_b = SEQ // tm
    return pl.pallas_call(
        _final_kernel,
        grid_spec=pltpu.PrefetchScalarGridSpec(
            num_scalar_prefetch=2,
            grid=(N_TOK // tm,),
            in_specs=[pl.BlockSpec((tm, D_MODEL), lambda i, *_: (i, 0)),
                      pl.BlockSpec((tm, LANES), lambda i, *_: (i, 0)),
                      pl.BlockSpec((None, 6, D_MODEL), lambda i, *_: (i // per_b, 0, 0)),
                      pl.BlockSpec((1, D_MODEL), lambda i, *_: (0, 0)),
                      pl.BlockSpec(memory_space=pl.ANY)],
            out_specs=pl.BlockSpec((tm, D_MODEL), lambda i, *_: (i, 0)),
            scratch_shapes=[pltpu.VMEM((2, TOP_K, tm * ROW_TILES, LANES), F32), pltpu.SemaphoreType.DMA((2,))]),
        out_shape=jax.ShapeDtypeStruct((N_TOK, D_MODEL), F32),
        compiler_params=_params("arbitrary"),
        name="final",
    )(d0, d1, h1, meta, mod3, final_norm, ys)


def _slot_layout(counts):
    cnt = counts[:, 0].astype(jnp.int32)
    tile_end = jnp.cumsum((cnt + TE - 1) // TE)
    seg = jnp.concatenate([jnp.zeros((1,), jnp.int32), tile_end * TE])
    off_col = jnp.broadcast_to(seg[:-1].astype(F32)[:, None], (N_EXPERTS, LANES))
    tile_ids = jnp.arange(N_TILES, dtype=jnp.int32)
    tile_expert = jnp.sum((tile_end[None, :] <= tile_ids[:, None]).astype(jnp.int32), axis=1)
    tile_expert = jnp.minimum(tile_expert, N_EXPERTS - 1)
    return cnt, seg, off_col, tile_expert, tile_end[-1:]


def _rope_tables(dim):
    pos = np.arange(SEQ, dtype=np.float64)
    inv = ROPE_THETA ** (-np.arange(0, dim, 2, dtype=np.float64) / dim)
    ang = pos[:, None] * inv[None, :]
    return np.cos(ang).astype(np.float32), np.sin(ang).astype(np.float32)


def _decay_tables():
    c = RET_CHUNK
    log_gamma = np.log1p(-np.exp2(-5.0 - np.arange(RET_HEADS, dtype=np.float64)))
    idx = np.arange(c, dtype=np.float64)
    rel = idx[:, None] - idx[None, :]
    dec = np.where(rel[None] >= 0, np.exp(log_gamma[:, None, None] * np.maximum(rel, 0.0)[None]), 0.0)
    xi = np.exp(log_gamma[:, None] * (idx[None, :] + 1.0))[:, :, None]
    zeta = np.exp(log_gamma[:, None] * (c - 1.0 - idx[None, :]))[:, :, None]
    cd = np.exp(log_gamma * c)[:, None, None]
    return tuple(jnp.asarray(t.astype(np.float32)) for t in (dec, xi, zeta, cd))


def _rotate_half_cols(w):
    half = w.shape[-1] // 2
    return jnp.concatenate([-w[..., half:], w[..., :half]], axis=-1)


def kernel(x, c, w_ada, b_ada, norm1, norm2, w_in, w_ret_o, q_norm, kv_norm, w_uq, w_ukv, w_mla_o, w_out,
           w_grp, b_grp, w_exp, b_exp, w1, w3, w2, final_norm):
    assert x.shape == (BATCH, SEQ, D_MODEL) and w_ada.shape[0] == 1
    x2 = x.reshape(N_TOK, D_MODEL)

    w_in_t = jnp.transpose(w_in[0])
    wq = w_uq[0].reshape(MLA_Q_LORA, MLA_HEADS, MLA_QK)
    wq = jnp.concatenate([wq, _rotate_half_cols(wq[..., MLA_NOPE:])], axis=-1)
    wq = wq.transpose(1, 0, 2).astype(BF16)
    wkv = w_ukv[0].reshape(MLA_KV_LORA, MLA_HEADS, MLA_NOPE + MLA_V).transpose(1, 0, 2).astype(BF16)
    gap = jnp.zeros((SUBLANES - N_GROUPS, D_MODEL), F32)
    tail = jnp.zeros((ROUTER_ROWS - SUBLANES - N_EXPERTS, D_MODEL), F32)
    w_rt = jnp.concatenate([w_grp[0].T, gap, w_exp[0].T, tail], axis=0)
    b_rt = jnp.concatenate([b_grp[0], gap[:, 0], b_exp[0], tail[:, 0]])
    b_rt = jnp.broadcast_to(b_rt[:, None], (ROUTER_ROWS, LANES))

    ret_cos, ret_sin = (jnp.asarray(t) for t in _rope_tables(RET_DK))
    mla_cos, mla_sin = (jnp.asarray(np.concatenate([t, t], axis=-1)) for t in _rope_tables(MLA_ROPE))
    dec, xi, zeta, cd = _decay_tables()

    mod3 = _ada(c, w_ada[0], b_ada[0]).reshape(BATCH, 6, D_MODEL)
    ret, lat, gates = _inproj(x2, mod3, norm1, ret_cos, ret_sin, mla_cos, mla_sin, q_norm, kv_norm, w_in_t)
    e_shape = (N_EXPERTS, D_MODEL, D_EXPERT)
    attn, y_ret, w1b, w3b, w2b, zeroed_slots = _mla(
        lat.reshape(BATCH, SEQ, MLA_LAT_W), wq, wkv, mla_cos, mla_sin, ret.reshape(BATCH, SEQ, 4 * RET_W),
        dec, xi, zeta, cd, w1[0].reshape(e_shape), w3[0].reshape(e_shape), w2[0].reshape(N_EXPERTS, D_EXPERT, D_MODEL))
    h1, u2t, meta_t, wtok, counts = _merge(y_ret.reshape(N_TOK, D_MODEL), attn.reshape(N_TOK, D_MODEL), gates, x2,
                                           mod3, norm2, w_ret_o, w_mla_o, w_out, w_rt, b_rt)
    cnt, seg, off_col, tile_expert, n_valid = _slot_layout(counts)
    dst = _plan(meta_t, off_col)
    d0, d1 = dst[0], dst[1]
    xs = _dispatch(d0, d1, u2t, zeroed_slots)
    ys = _experts(tile_expert, n_valid, xs, w1b, w3b, w2b)
    out = _final(d0, d1, h1, wtok, mod3, final_norm.reshape(1, D_MODEL), ys)
    return out.reshape(BATCH, SEQ, D_MODEL)
```

```python
import numpy as np
import jax
import jax.numpy as jnp
from jax import lax
from jax.experimental import pallas as pl
from jax.experimental.pallas import tpu as pltpu

D_MODEL = 1024
BATCH = 8
SEQ = 2048
N_TOK = BATCH * SEQ

RET_HEADS = 4
RET_DK = 256
RET_DV = 256
RET_CHUNK = 256
RET_W = RET_HEADS * RET_DK

MLA_HEADS = 8
MLA_NOPE = 128
MLA_ROPE = 64
MLA_V = 128
MLA_Q_LORA = 384
MLA_KV_LORA = 256
MLA_LAT_W = MLA_Q_LORA + MLA_KV_LORA + 2 * MLA_ROPE
MLA_QK = MLA_NOPE + MLA_ROPE
ROPE_THETA = 10000.0

N_GROUPS = 4
EXPERTS_PER_GROUP = 8
N_EXPERTS = N_GROUPS * EXPERTS_PER_GROUP
D_EXPERT = 256
EPS = 1e-6
LOG2_E = 1.4426950408889634

LANES = 128
SUBLANES = 8
ROUTER_ROWS = 48
ROW_TILES = D_MODEL // LANES
VMEM_LIMIT = 56 * 1024 * 1024

TM_PROJ = 512
TM_INPROJ = 512
TM_FINAL = 512
TM_DISPATCH = 4096
TQ = 256
MLA_HPS = 2
TE = 512
TE_CHAINS = 2
EXPERT_RING = 3
MERGE_CHAINS = 4
TOP_K = 2
N_TILES = N_TOK * TOP_K // TE + N_EXPERTS
N_SLOTS = N_TILES * TE

F32 = jnp.float32
BF16 = jnp.bfloat16


def _sigmoid(x):
    return 1.0 / (1.0 + jnp.exp(-x))


def _rms(x):
    return x * lax.rsqrt(jnp.mean(x * x, axis=-1, keepdims=True) + EPS)


def _dot(a, b):
    return jnp.dot(a, b, preferred_element_type=F32)


def _dot_nt(a, b):
    return lax.dot_general(a, b, (((1,), (1,)), ((), ())), preferred_element_type=F32)


def _dot_tn(a, b):
    return lax.dot_general(a, b, (((0,), (0,)), ((), ())), preferred_element_type=F32)


def _params(*sem):
    return pltpu.CompilerParams(dimension_semantics=sem, vmem_limit_bytes=VMEM_LIMIT)


def _resident(shape):
    nd = len(shape)
    return pl.BlockSpec(shape, lambda *_: (0,) * nd, pipeline_mode=pl.Buffered(1))


def _layer_weight(w):
    return pl.BlockSpec((None,) + w.shape[1:], lambda *_: (0,) * w.ndim, pipeline_mode=pl.Buffered(1))


def _ada_kernel(c_ref, w_ref, b_ref, o_ref):
    c = c_ref[...]
    act = (c * _sigmoid(c)).astype(BF16)
    o_ref[...] = _dot(act, w_ref[...].astype(BF16)) + b_ref[...]


def _ada(c, w_ada, b_ada):
    n = w_ada.shape[1]
    tn = D_MODEL
    return pl.pallas_call(
        _ada_kernel,
        grid=(n // tn,),
        in_specs=[pl.BlockSpec((BATCH, D_MODEL), lambda j: (0, 0)),
                  pl.BlockSpec((D_MODEL, tn), lambda j: (0, j)),
                  pl.BlockSpec((1, tn), lambda j: (0, j))],
        out_specs=pl.BlockSpec((BATCH, tn), lambda j: (0, j)),
        out_shape=jax.ShapeDtypeStruct((BATCH, n), F32),
        compiler_params=_params("arbitrary"),
        name="ada",
    )(c, w_ada, b_ada.reshape(1, n))


O_LAT = 4 * RET_W
O_PE = O_LAT + MLA_Q_LORA + MLA_KV_LORA
O_GATE = O_PE + MLA_ROPE


N_IN = O_GATE + 2 * D_MODEL
WPREP_ROWS = 512


def _stage_weights(w_hbm, ret_ref, lat_ref, gate_ref, buf, sem):
    chunks = [(s, min(WPREP_ROWS, N_IN - s)) for s in range(0, N_IN, WPREP_ROWS)]

    def copy(i):
        s, n = chunks[i]
        return pltpu.make_async_copy(w_hbm.at[pl.ds(s, n)], buf.at[i % 2, pl.ds(0, n)], sem.at[i % 2])

    groups = [(0, RET_W, ret_ref, 0, 1.0), (RET_W, 2 * RET_W, ret_ref, RET_W, RET_DK ** -0.5),
              (2 * RET_W, O_LAT, ret_ref, 2 * RET_W, 1.0), (O_LAT, O_GATE, lat_ref, 0, 1.0),
              (O_GATE, N_IN, gate_ref, 0, 1.0)]
    half = MLA_ROPE // 2
    rot = [(O_PE + half, O_GATE, O_GATE - O_LAT, -1.0), (O_PE, O_PE + half, O_GATE - O_LAT + half, 1.0)]

    copy(0).start()
    for i, (s, n) in enumerate(chunks):
        if i + 1 < len(chunks):
            copy(i + 1).start()
        copy(i).wait()
        for lo, hi, dst, dst_lo, scale in groups + [(a, b, lat_ref, d, sc) for a, b, d, sc in rot]:
            a, b = max(lo, s), min(hi, s + n)
            if a < b:
                x = buf[i % 2, a - s:b - s, :]
                dst[dst_lo + a - lo:dst_lo + b - lo, :] = (x if scale == 1.0 else x * scale).astype(BF16)


def _inproj_kernel(x_ref, mod_ref, n1_ref, cos_ref, sin_ref, mcos_ref, msin_ref, qn_ref, kvn_ref, w_hbm,
                   ret_ref, lat_ref, gate_ref, wr_ref, wm_ref, wg_ref, stage_buf, stage_sem):
    @pl.when(pl.program_id(0) == 0)
    def _():
        _stage_weights(w_hbm, wr_ref, wm_ref, wg_ref, stage_buf, stage_sem)

    y = _rms(x_ref[...]) * n1_ref[...]
    u = (y * (1.0 + mod_ref[1:2, :]) + mod_ref[0:1, :]).astype(BF16)
    cos, sin = cos_ref[...], sin_ref[...]
    half = RET_DK // 2
    for n in range(0, 2 * RET_W, RET_DK):
        p = _dot_nt(u, wr_ref[n:n + RET_DK, :])
        x1, x2 = p[:, :half], p[:, half:]
        ret_ref[:, n:n + half] = (x1 * cos - x2 * sin).astype(BF16)
        ret_ref[:, n + half:n + RET_DK] = (x2 * cos + x1 * sin).astype(BF16)
    step = 512
    for n in range(2 * RET_W, 3 * RET_W, step):
        ret_ref[:, n:n + step] = _dot_nt(u, wr_ref[n:n + step, :]).astype(BF16)
    for n in range(3 * RET_W, 4 * RET_W, step):
        p = _dot_nt(u, wr_ref[n:n + step, :])
        ret_ref[:, n:n + step] = (p * _sigmoid(p)).astype(BF16)
    lat = _dot_nt(u, wm_ref[...])
    o_kv, o_pe, o_rot = MLA_Q_LORA, MLA_Q_LORA + MLA_KV_LORA, MLA_Q_LORA + MLA_KV_LORA + MLA_ROPE
    lat_ref[:, :o_kv] = (_rms(lat[:, :o_kv]) * qn_ref[...]).astype(BF16)
    lat_ref[:, o_kv:o_pe] = (_rms(lat[:, o_kv:o_pe]) * kvn_ref[...]).astype(BF16)
    lat_ref[:, o_pe:o_rot] = (lat[:, o_pe:o_rot] * mcos_ref[...] + lat[:, o_rot:] * msin_ref[...]).astype(BF16)
    lat_ref[:, o_rot:] = jnp.zeros((lat.shape[0], MLA_ROPE), BF16)
    for n in range(0, 2 * D_MODEL, step):
        gate_ref[:, n:n + step] = _dot_nt(u, wg_ref[n:n + step, :]).astype(BF16)


def _inproj(x2, mod3, norm1, cos, sin, mla_cos, mla_sin, q_norm, kv_norm, w_in_t):
    tm = TM_INPROJ
    per_b = SEQ // tm
    rope_tab = pl.BlockSpec((tm, RET_DK // 2), lambda i: (i % per_b, 0))
    mla_tab = pl.BlockSpec((tm, MLA_ROPE), lambda i: (i % per_b, 0))
    return pl.pallas_call(
        _inproj_kernel,
        grid=(N_TOK // tm,),
        in_specs=[pl.BlockSpec((tm, D_MODEL), lambda i: (i, 0)),
                  pl.BlockSpec((None, 6, D_MODEL), lambda i: (i // per_b, 0, 0)),
                  _resident((1, D_MODEL)), rope_tab, rope_tab, mla_tab, mla_tab,
                  _resident(q_norm.shape), _resident(kv_norm.shape),
                  pl.BlockSpec(memory_space=pl.ANY)],
        out_specs=[pl.BlockSpec((tm, 4 * RET_W), lambda i: (i, 0)),
                   pl.BlockSpec((tm, MLA_LAT_W), lambda i: (i, 0)),
                   pl.BlockSpec((tm, 2 * D_MODEL), lambda i: (i, 0))],
        out_shape=[jax.ShapeDtypeStruct((N_TOK, 4 * RET_W), BF16),
                   jax.ShapeDtypeStruct((N_TOK, MLA_LAT_W), BF16),
                   jax.ShapeDtypeStruct((N_TOK, 2 * D_MODEL), BF16)],
        scratch_shapes=[pltpu.VMEM((O_LAT, D_MODEL), BF16), pltpu.VMEM((MLA_LAT_W, D_MODEL), BF16),
                        pltpu.VMEM((2 * D_MODEL, D_MODEL), BF16),
                        pltpu.VMEM((2, WPREP_ROWS, D_MODEL), F32), pltpu.SemaphoreType.DMA((2,))],
        compiler_params=_params("arbitrary"),
        name="inproj",
    )(x2, mod3, norm1, cos, sin, mla_cos, mla_sin, q_norm, kv_norm, w_in_t)


def _mla_kernel(lat_ref, wq_ref, wkv_ref, cos_ref, sin_ref,
                rq_ref, rk_ref, rv_ref, rg_ref, dec_ref, xi_ref, zeta_ref, cd_ref, w1_ref, w3_ref, w2_ref,
                o_ref, yr_ref, w1b_ref, w3b_ref, w2b_ref, q_s, k_s, v_s, state_ref):
    h = pl.program_id(1)
    o_kv, o_pe, o_rot = MLA_Q_LORA, MLA_Q_LORA + MLA_KV_LORA, MLA_Q_LORA + MLA_KV_LORA + MLA_ROPE
    cos, sin = cos_ref[...], sin_ref[...]

    scale = (MLA_QK ** -0.5) * LOG2_E
    for g in range(MLA_HPS):
        qf = _dot(lat_ref[:, :o_kv], wq_ref[g])
        q_s[g, :, :MLA_NOPE] = (qf[:, :MLA_NOPE] * scale).astype(BF16)
        q_pe = qf[:, MLA_NOPE:MLA_QK] * cos + qf[:, MLA_QK:] * sin
        q_s[g, :, MLA_NOPE:] = (q_pe * scale).astype(BF16)
        kvf = _dot(lat_ref[:, o_kv:o_pe], wkv_ref[g])
        k_s[g, :, :MLA_NOPE] = kvf[:, :MLA_NOPE].astype(BF16)
        k_s[g, :, MLA_NOPE:] = lat_ref[:, o_pe:o_rot]
        v_s[g, :, :MLA_V] = kvf[:, MLA_NOPE:].astype(BF16)
        v_s[g, :, MLA_V:] = jnp.ones((SEQ, MLA_V), BF16)

    causal = lax.broadcasted_iota(jnp.int32, (TQ, TQ), 0) >= lax.broadcasted_iota(jnp.int32, (TQ, TQ), 1)
    heads = range(MLA_HPS)
    n_blk = SEQ // TQ

    def scores(i):
        lo, hi = i * TQ, (i + 1) * TQ
        diag = [jnp.where(causal, _dot_nt(q_s[g, lo:hi, :], k_s[g, lo:hi, :]), -jnp.inf) for g in heads]
        past = [_dot_nt(q_s[g, lo:hi, :], k_s[g, :lo, :]) if i > 0 else None for g in heads]
        return diag, past

    state_ref[...] = jnp.zeros_like(state_ref)
    pending = scores(0)
    for i in range(n_blk):
        lo, hi = i * TQ, (i + 1) * TQ
        diag, past = pending
        if i + 1 < n_blk:
            pending = scores(i + 1)
        rq, rk, rv = rq_ref[lo:hi, :], rk_ref[lo:hi, :], rv_ref[lo:hi, :]
        r_scores = (_dot_nt(rq, rk) * dec_ref[h]).astype(BF16)
        carried = xi_ref[h] * _dot(rq, state_ref[...].astype(BF16))
        m = [jnp.max(diag[g], axis=-1, keepdims=True) for g in heads]
        if i > 0:
            m = [jnp.maximum(m[g], jnp.max(past[g], axis=-1, keepdims=True)) for g in heads]
        ry = _dot(r_scores, rv) + carried
        k_dec = (rk.astype(F32) * zeta_ref[h]).astype(BF16)
        state_ref[...] = state_ref[...] * cd_ref[h] + _dot_tn(k_dec, rv)
        acc = [_dot(jnp.exp2(diag[g] - m[g]).astype(BF16), v_s[g, lo:hi, :]) for g in heads]
        if i > 0:
            acc = [acc[g] + _dot(jnp.exp2(past[g] - m[g]).astype(BF16), v_s[g, :lo, :]) for g in heads]
        if i < 3:
            src, dst = ((w1_ref, w1b_ref), (w3_ref, w3b_ref), (w2_ref, w2b_ref))[i]
            dst[...] = src[...].astype(BF16)
        ryc = ry - jnp.mean(ry, axis=-1, keepdims=True)
        r_inv = lax.rsqrt(jnp.mean(ryc * ryc, axis=-1, keepdims=True) + EPS)
        yr_ref[lo:hi, :] = (rg_ref[lo:hi, :].astype(F32) * (ryc * r_inv)).astype(BF16)
        for g in heads:
            o_ref[lo:hi, g * MLA_V:(g + 1) * MLA_V] = (acc[g][:, :MLA_V] / acc[g][:, MLA_V:]).astype(BF16)


def _mla(lat3, wq, wkv, cos, sin, ret3, dec, xi, zeta, cd, w1, w3, w2):
    hps = MLA_HPS
    steps = MLA_HEADS // hps
    assert steps == RET_HEADS and TQ == RET_CHUNK and BATCH * steps == N_EXPERTS
    expert = lambda w: pl.BlockSpec((None,) + w.shape[1:], lambda b, h: (b * steps + h, 0, 0))
    ret_part = lambda part: pl.BlockSpec((None, SEQ, RET_DK), lambda b, h: (b, 0, part * RET_HEADS + h))
    whole = lambda a: pl.BlockSpec(a.shape, lambda b, h: (0,) * a.ndim)
    return pl.pallas_call(
        _mla_kernel,
        grid=(BATCH, MLA_HEADS // hps),
        in_specs=[pl.BlockSpec((None, SEQ, MLA_LAT_W), lambda b, h: (b, 0, 0)),
                  pl.BlockSpec((hps, MLA_Q_LORA, MLA_QK + MLA_ROPE), lambda b, h: (h, 0, 0)),
                  pl.BlockSpec((hps, MLA_KV_LORA, MLA_NOPE + MLA_V), lambda b, h: (h, 0, 0)),
                  pl.BlockSpec((SEQ, MLA_ROPE), lambda b, h: (0, 0)),
                  pl.BlockSpec((SEQ, MLA_ROPE), lambda b, h: (0, 0)),
                  ret_part(0), ret_part(1), ret_part(2), ret_part(3),
                  whole(dec), whole(xi), whole(zeta), whole(cd), expert(w1), expert(w3), expert(w2)],
        out_specs=[pl.BlockSpec((None, SEQ, hps * MLA_V), lambda b, h: (b, 0, h)),
                   pl.BlockSpec((None, SEQ, RET_DV), lambda b, h: (b, 0, h)),
                   expert(w1), expert(w3), expert(w2)],
        out_shape=[jax.ShapeDtypeStruct((BATCH, SEQ, MLA_HEADS * MLA_V), BF16),
                   jax.ShapeDtypeStruct((BATCH, SEQ, RET_HEADS * RET_DV), BF16)]
                  + [jax.ShapeDtypeStruct(w.shape, BF16) for w in (w1, w3, w2)],
        scratch_shapes=[pltpu.VMEM((hps, SEQ, MLA_QK), BF16),
                        pltpu.VMEM((hps, SEQ, MLA_QK), BF16),
                        pltpu.VMEM((hps, SEQ, 2 * MLA_V), BF16),
                        pltpu.VMEM((RET_DK, RET_DV), F32)],
        compiler_params=_params("arbitrary", "arbitrary"),
        name="mla",
    )(lat3, wq, wkv, cos, sin, ret3, ret3, ret3, ret3, dec, xi, zeta, cd, w1, w3, w2)


def _route(logits_t):
    tm = logits_t.shape[1]
    row = lax.broadcasted_iota(jnp.int32, (SUBLANES, tm), 0)
    neg = -jnp.inf
    gl = jnp.where(row < N_GROUPS, logits_t[:SUBLANES], neg)
    gmax = jnp.max(gl, axis=0, keepdims=True)
    gsel = jnp.min(jnp.where(gl == gmax, row, SUBLANES), axis=0, keepdims=True)
    p_grp = 1.0 / jnp.sum(jnp.exp(gl - gmax), axis=0, keepdims=True)
    el = logits_t[SUBLANES * N_GROUPS:SUBLANES * (N_GROUPS + 1)]
    for g in reversed(range(N_GROUPS - 1)):
        el = jnp.where(gsel == g, logits_t[SUBLANES * (g + 1):SUBLANES * (g + 2)], el)
    v0 = jnp.max(el, axis=0, keepdims=True)
    i0 = jnp.min(jnp.where(el == v0, row, SUBLANES), axis=0, keepdims=True)
    el1 = jnp.where(row == i0, neg, el)
    v1 = jnp.max(el1, axis=0, keepdims=True)
    i1 = jnp.min(jnp.where(el1 == v1, row, SUBLANES), axis=0, keepdims=True)
    t = jnp.exp(v1 - v0)
    w0 = p_grp / (1.0 + t)
    w1 = p_grp * t / (1.0 + t)
    return gsel * EXPERTS_PER_GROUP + i0, gsel * EXPERTS_PER_GROUP + i1, w0, w1


def _stack_rows(rows, n):
    tm = rows[0].shape[1]
    row = lax.broadcasted_iota(jnp.int32, (n, tm), 0)
    out = jnp.zeros((n, tm), F32)
    for k, r in enumerate(rows):
        out = jnp.where(row == k, r, out)
    return out


def _to_token_tiles(ref, val):
    n = val.shape[0]
    for s in range(ROW_TILES):
        ref[pl.ds(s, n, stride=ROW_TILES), :] = val[:, s * LANES:(s + 1) * LANES]


def _from_token_tiles(ref):
    n = ref.shape[0] // ROW_TILES
    return jnp.concatenate([ref[pl.ds(s, n, stride=ROW_TILES), :] for s in range(ROW_TILES)], axis=-1)


def _token_rows(ref, t):
    return ref.at[pl.ds(pl.multiple_of(t * ROW_TILES, ROW_TILES), ROW_TILES)]


def _merge_kernel(yr_ref, at_ref, gr_ref, gm_ref, x_ref, mod_ref, n2_ref, wro_ref, wmo_ref, wo_ref,
                  wrt_ref, brt_ref, h1_ref, u2_ref, meta_ref, wtok_ref, cnt_ref, carry_ref, wro_s, wmo_s, wo_s):
    tm = x_ref.shape[0]

    @pl.when(pl.program_id(0) == 0)
    def _():
        carry_ref[...] = jnp.zeros_like(carry_ref)
        wro_s[...] = wro_ref[...].astype(BF16)
        wmo_s[...] = wmo_ref[...].astype(BF16)
        wo_s[...] = wo_ref[...].astype(BF16)

    chunks = range(MERGE_CHAINS)
    sub = tm // MERGE_CHAINS
    rows = [slice(c * sub, (c + 1) * sub) for c in chunks]
    y_ret = [_dot(yr_ref[r, :], wro_s[...]) for r in rows]
    y_mla = [_dot(at_ref[r, :], wmo_s[...]) for r in rows]
    merged = [(_sigmoid(gr_ref[rows[c], :].astype(F32)) * y_ret[c]
               + _sigmoid(gm_ref[rows[c], :].astype(F32)) * y_mla[c]).astype(BF16) for c in chunks]
    o = [_dot(merged[c], wo_s[...]) for c in chunks]
    h1 = [x_ref[rows[c], :] + mod_ref[2:3, :] * o[c] for c in chunks]
    for c in chunks:
        h1_ref[rows[c], :] = h1[c]
    u2 = [_rms(h1[c]) * n2_ref[...] * (1.0 + mod_ref[4:5, :]) + mod_ref[3:4, :] for c in chunks]
    for c in chunks:
        _to_token_tiles(u2_ref.at[pl.ds(c * sub * ROW_TILES, sub * ROW_TILES)], u2[c])
    w = wrt_ref[...]
    w_hi = w.astype(BF16)
    w_lo = (w - w_hi.astype(F32)).astype(BF16)
    w_both = jnp.concatenate([w_hi, w_lo], axis=0)
    u_hi = [u2[c].astype(BF16) for c in chunks]
    u_lo = [(u2[c] - u_hi[c].astype(F32)).astype(BF16) for c in chunks]
    by_hi = [_dot_nt(w_both, u_hi[c]) for c in chunks]
    logits_t = [by_hi[c][:ROUTER_ROWS] + by_hi[c][ROUTER_ROWS:] + _dot_nt(w_hi, u_lo[c]) + brt_ref[:, 0:1]
                for c in chunks]
    routed = [_route(logits_t[c]) for c in chunks]
    e0, e1, w0, w1 = [jnp.concatenate([routed[c][k] for c in chunks], axis=1) for k in range(4)]
    erow = lax.broadcasted_iota(jnp.int32, (N_EXPERTS, tm), 0)
    m0, m1 = erow == e0, erow == e1
    member = jnp.where(m0 | m1, 1.0, 0.0)
    earlier = jnp.where(lax.broadcasted_iota(jnp.int32, (tm, tm), 0) < lax.broadcasted_iota(jnp.int32, (tm, tm), 1),
                        1.0, 0.0).astype(BF16)
    prefix = _dot(member.astype(BF16), earlier) + carry_ref[:, 0:1]
    rank0 = jnp.sum(jnp.where(m0, prefix, 0.0), axis=0, keepdims=True)
    rank1 = jnp.sum(jnp.where(m1, prefix, 0.0), axis=0, keepdims=True)
    carry_ref[...] = carry_ref[...] + jnp.sum(member, axis=1, keepdims=True)
    cnt_ref[...] = carry_ref[...]
    meta_ref[...] = _stack_rows([e0.astype(F32), e1.astype(F32), rank0, rank1], SUBLANES)
    wt = _stack_rows([w0, w1], 2 * SUBLANES)
    wt_hi = wt.astype(BF16)
    wt_lo = (wt - wt_hi.astype(F32)).astype(BF16)
    place = jnp.where(lax.broadcasted_iota(jnp.int32, (2 * SUBLANES, LANES), 0)
                      == lax.broadcasted_iota(jnp.int32, (2 * SUBLANES, LANES), 1), 1.0, 0.0).astype(BF16)
    wtok_ref[...] = _dot_tn(wt_hi, place) + _dot_tn(wt_lo, place)


def _merge(y_ret, attn, gates, x2, mod3, norm2, w_ret_o, w_mla_o, w_out, w_rt, b_rt):
    tm = TM_PROJ
    per_b = SEQ // tm
    row = lambda j: pl.BlockSpec((tm, D_MODEL), lambda i: (i, j))
    return pl.pallas_call(
        _merge_kernel,
        grid=(N_TOK // tm,),
        in_specs=[row(0), row(0), row(0), row(1), row(0),
                  pl.BlockSpec((None, 6, D_MODEL), lambda i: (i // per_b, 0, 0)),
                  _resident((1, D_MODEL)),
                  _layer_weight(w_ret_o), _layer_weight(w_mla_o), _layer_weight(w_out),
                  _resident(w_rt.shape), _resident(b_rt.shape)],
        out_specs=[row(0),
                   pl.BlockSpec((tm * ROW_TILES, LANES), lambda i: (i, 0)),
                   pl.BlockSpec((SUBLANES, tm), lambda i: (0, i)),
                   pl.BlockSpec((tm, LANES), lambda i: (i, 0)),
                   pl.BlockSpec((N_EXPERTS, LANES), lambda i: (0, 0))],
        out_shape=[jax.ShapeDtypeStruct((N_TOK, D_MODEL), F32),
                   jax.ShapeDtypeStruct((N_TOK * ROW_TILES, LANES), F32),
                   jax.ShapeDtypeStruct((SUBLANES, N_TOK), F32),
                   jax.ShapeDtypeStruct((N_TOK, LANES), F32),
                   jax.ShapeDtypeStruct((N_EXPERTS, LANES), F32)],
        scratch_shapes=[pltpu.VMEM((N_EXPERTS, LANES), F32)] + [pltpu.VMEM((D_MODEL, D_MODEL), BF16)] * 3,
        compiler_params=_params("arbitrary"),
        name="merge",
    )(y_ret, attn, gates, gates, x2, mod3, norm2, w_ret_o, w_mla_o, w_out, w_rt, b_rt)


def _plan_kernel(meta_ref, off_ref, dst_ref):
    m = meta_ref[...]
    erow = lax.broadcasted_iota(jnp.int32, (N_EXPERTS, m.shape[1]), 0)
    off = off_ref[:, 0:1]
    d = [jnp.sum(jnp.where(erow == m[k:k + 1].astype(jnp.int32), off, 0.0), axis=0, keepdims=True) + m[k + 2:k + 3]
         for k in range(TOP_K)]
    dst_ref[...] = _stack_rows(d, SUBLANES).astype(jnp.int32)


def _plan(meta_t, off_col):
    tm = 2048
    return pl.pallas_call(
        _plan_kernel,
        grid=(N_TOK // tm,),
        in_specs=[pl.BlockSpec((SUBLANES, tm), lambda i: (0, i)),
                  pl.BlockSpec((N_EXPERTS, LANES), lambda i: (0, 0))],
        out_specs=pl.BlockSpec((SUBLANES, tm), lambda i: (0, i)),
        out_shape=jax.ShapeDtypeStruct((SUBLANES, N_TOK), jnp.int32),
        compiler_params=_params("arbitrary"),
        name="plan",
    )(meta_t, off_col)


def _row_copy_wait(src_like, dst_like, sem):
    pltpu.make_async_copy(src_like, dst_like, sem).wait()


def _dispatch_kernel(d0_ref, d1_ref, seg_ref, cnt_ref, u_ref, xs_ref, zero_ref, sem, zsem):
    i = pl.program_id(0)
    tm = u_ref.shape[0] // ROW_TILES
    tile_rows = TE * ROW_TILES

    def slot_tile(j):
        return xs_ref.at[pl.ds(pl.multiple_of(j * tile_rows, tile_rows), tile_rows)]

    def pad(e):
        first = seg_ref[e] + cnt_ref[e]
        rows = (seg_ref[e + 1] - first) * ROW_TILES
        start = pl.multiple_of(first * ROW_TILES, ROW_TILES)
        return rows > 0, pltpu.make_async_copy(zero_ref.at[pl.ds(0, rows)], xs_ref.at[pl.ds(start, rows)], zsem)

    def unused(j):
        return pltpu.make_async_copy(zero_ref, slot_tile(j), zsem)

    first_unused = seg_ref[N_EXPERTS] // TE

    @pl.when(i == 0)
    def _():
        zero_ref[...] = jnp.zeros_like(zero_ref)
        for e in range(N_EXPERTS):
            nonempty, copy = pad(e)
            pl.when(nonempty)(copy.start)
        lax.fori_loop(first_unused, N_TILES, lambda j, c: (unused(j).start(), c)[1], 0)

    base = i * tm

    def body(r, carry):
        src = _token_rows(u_ref, r)
        pltpu.make_async_copy(src, _token_rows(xs_ref, d0_ref[base + r]), sem).start(priority=0)
        pltpu.make_async_copy(src, _token_rows(xs_ref, d1_ref[base + r]), sem).start(priority=1)
        return carry

    lax.fori_loop(0, tm, body, 0, unroll=8)
    _row_copy_wait(u_ref, xs_ref.at[pl.ds(0, tm * ROW_TILES)], sem)
    _row_copy_wait(u_ref, xs_ref.at[pl.ds(0, tm * ROW_TILES)], sem)

    @pl.when(i == 0)
    def _():
        for e in range(N_EXPERTS):
            nonempty, copy = pad(e)
            pl.when(nonempty)(copy.wait)
        lax.fori_loop(first_unused, N_TILES, lambda j, c: (unused(j).wait(), c)[1], 0)


def _dispatch(d0, d1, seg, cnt, u2t):
    tm = TM_DISPATCH
    return pl.pallas_call(
        _dispatch_kernel,
        grid_spec=pltpu.PrefetchScalarGridSpec(
            num_scalar_prefetch=4,
            grid=(N_TOK // tm,),
            in_specs=[pl.BlockSpec((tm * ROW_TILES, LANES), lambda i, *_: (i, 0))],
            out_specs=pl.BlockSpec(memory_space=pl.ANY),
            scratch_shapes=[pltpu.VMEM((TE * ROW_TILES, LANES), F32),
                            pltpu.SemaphoreType.DMA(()), pltpu.SemaphoreType.DMA(())]),
        out_shape=jax.ShapeDtypeStruct((N_SLOTS * ROW_TILES, LANES), F32),
        compiler_params=_params("arbitrary"),
        name="dispatch",
    )(d0, d1, seg, cnt, u2t)


def _expert_kernel(te_ref, nv_ref, x_hbm, w1_s, w3_s, w2_s, y_hbm, xbuf, ybuf, sem, wsem):
    j = pl.program_id(0)
    nv = nv_ref[0]
    rows = TE * ROW_TILES

    def fetch(t):
        return pltpu.make_async_copy(x_hbm.at[pl.ds(pl.multiple_of(t * rows, rows), rows)],
                                     xbuf.at[t % EXPERT_RING], sem.at[t % EXPERT_RING])

    def put(t):
        return pltpu.make_async_copy(ybuf.at[t % 2], y_hbm.at[pl.ds(pl.multiple_of(t * rows, rows), rows)],
                                     wsem.at[t % 2])

    @pl.when(j == 0)
    def _():
        fetch(0).start()

        @pl.when(1 < nv)
        def _():
            fetch(1).start()

    @pl.when(j + 2 < nv)
    def _():
        fetch(j + 2).start()

    @pl.when(j < nv)
    def _():
        fetch(j).wait()
        x_ref = xbuf.at[j % EXPERT_RING]
        sub = TE // TE_CHAINS * ROW_TILES
        part = lambda ref, c: ref.at[pl.ds(c * sub, sub)]
        chains = range(TE_CHAINS)
        x = [_from_token_tiles(part(x_ref, c)).astype(BF16) for c in chains]
        a = [_dot(x[c], w1_s[...]) for c in chains]
        b = [_dot(x[c], w3_s[...]) for c in chains]
        hid = [(a[c] * _sigmoid(a[c]) * b[c]).astype(BF16) for c in chains]

        @pl.when(j >= 2)
        def _():
            put(j - 2).wait()

        y_ref = ybuf.at[j % 2]
        for c in chains:
            _to_token_tiles(part(y_ref, c), _dot(hid[c], w2_s[...]))
        put(j).start()

        @pl.when(j == nv - 1)
        def _():
            @pl.when(j >= 1)
            def _():
                put(j - 1).wait()

            put(j).wait()


def _experts(tile_expert, n_valid, xs, w1, w3, w2):
    tile = lambda j, te, nv: jnp.minimum(j, nv[0] - 1)
    wspec = lambda shape: pl.BlockSpec((None,) + shape, lambda j, te, nv: (te[tile(j, te, nv)], 0, 0))
    return pl.pallas_call(
        _expert_kernel,
        grid_spec=pltpu.PrefetchScalarGridSpec(
            num_scalar_prefetch=2,
            grid=(N_TILES,),
            in_specs=[pl.BlockSpec(memory_space=pl.ANY),
                      wspec((D_MODEL, D_EXPERT)), wspec((D_MODEL, D_EXPERT)), wspec((D_EXPERT, D_MODEL))],
            out_specs=pl.BlockSpec(memory_space=pl.ANY),
            scratch_shapes=[pltpu.VMEM((EXPERT_RING, TE * ROW_TILES, LANES), F32),
                            pltpu.VMEM((2, TE * ROW_TILES, LANES), F32),
                            pltpu.SemaphoreType.DMA((EXPERT_RING,)),
                            pltpu.SemaphoreType.DMA((2,))]),
        out_shape=jax.ShapeDtypeStruct((N_SLOTS * ROW_TILES, LANES), F32),
        input_output_aliases={2: 0},
        compiler_params=_params("arbitrary"),
        name="experts",
    )(tile_expert, n_valid, xs, w1, w3, w2)


def _final_kernel(d0_ref, d1_ref, h1_ref, meta_ref, mod_ref, fn_ref, ys_ref, o_ref, ybuf, sem):
    i = pl.program_id(0)
    tm = h1_ref.shape[0]

    def gather(t):
        buf, s = ybuf.at[t % 2], sem.at[t % 2]
        base = t * tm

        def body(r, carry):
            pltpu.make_async_copy(_token_rows(ys_ref, d0_ref[base + r]), _token_rows(buf.at[0], r), s).start(priority=0)
            pltpu.make_async_copy(_token_rows(ys_ref, d1_ref[base + r]), _token_rows(buf.at[1], r), s).start(priority=1)
            return carry

        lax.fori_loop(0, tm, body, 0, unroll=8)

    @pl.when(i == 0)
    def _():
        gather(0)

    @pl.when(i + 1 < pl.num_programs(0))
    def _():
        gather(i + 1)

    buf = ybuf.at[i % 2]
    for k in range(TOP_K):
        _row_copy_wait(ys_ref.at[pl.ds(0, tm * ROW_TILES)], buf.at[k], sem.at[i % 2])
    m = meta_ref[...]
    moe = m[:, 0:1] * _from_token_tiles(buf.at[0]) + m[:, 1:2] * _from_token_tiles(buf.at[1])
    h2 = h1_ref[...] + mod_ref[5:6, :] * moe
    o_ref[...] = _rms(h2) * fn_ref[...]


def _final(d0, d1, h1, meta, mod3, final_norm, ys):
    tm = TM_FINAL
    per_b = SEQ // tm
    return pl.pallas_call(
        _final_kernel,
        grid_spec=pltpu.PrefetchScalarGridSpec(
            num_scalar_prefetch=2,
            grid=(N_TOK // tm,),
            in_specs=[pl.BlockSpec((tm, D_MODEL), lambda i, *_: (i, 0)),
                      pl.BlockSpec((tm, LANES), lambda i, *_: (i, 0)),
                      pl.BlockSpec((None, 6, D_MODEL), lambda i, *_: (i // per_b, 0, 0)),
                      pl.BlockSpec((1, D_MODEL), lambda i, *_: (0, 0)),
                      pl.BlockSpec(memory_space=pl.ANY)],
            out_specs=pl.BlockSpec((tm, D_MODEL), lambda i, *_: (i, 0)),
            scratch_shapes=[pltpu.VMEM((2, TOP_K, tm * ROW_TILES, LANES), F32), pltpu.SemaphoreType.DMA((2,))]),
        out_shape=jax.ShapeDtypeStruct((N_TOK, D_MODEL), F32),
        compiler_params=_params("arbitrary"),
        name="final",
    )(d0, d1, h1, meta, mod3, final_norm, ys)


def _slot_layout(counts):
    cnt = counts[:, 0].astype(jnp.int32)
    tile_end = jnp.cumsum((cnt + TE - 1) // TE)
    seg = jnp.concatenate([jnp.zeros((1,), jnp.int32), tile_end * TE])
    off_col = jnp.broadcast_to(seg[:-1].astype(F32)[:, None], (N_EXPERTS, LANES))
    tile_ids = jnp.arange(N_TILES, dtype=jnp.int32)
    tile_expert = jnp.sum((tile_end[None, :] <= tile_ids[:, None]).astype(jnp.int32), axis=1)
    tile_expert = jnp.minimum(tile_expert, N_EXPERTS - 1)
    return cnt, seg, off_col, tile_expert, tile_end[-1:]


def _rope_tables(dim):
    pos = np.arange(SEQ, dtype=np.float64)
    inv = ROPE_THETA ** (-np.arange(0, dim, 2, dtype=np.float64) / dim)
    ang = pos[:, None] * inv[None, :]
    return np.cos(ang).astype(np.float32), np.sin(ang).astype(np.float32)


def _decay_tables():
    c = RET_CHUNK
    log_gamma = np.log1p(-np.exp2(-5.0 - np.arange(RET_HEADS, dtype=np.float64)))
    idx = np.arange(c, dtype=np.float64)
    rel = idx[:, None] - idx[None, :]
    dec = np.where(rel[None] >= 0, np.exp(log_gamma[:, None, None] * np.maximum(rel, 0.0)[None]), 0.0)
    xi = np.exp(log_gamma[:, None] * (idx[None, :] + 1.0))[:, :, None]
    zeta = np.exp(log_gamma[:, None] * (c - 1.0 - idx[None, :]))[:, :, None]
    cd = np.exp(log_gamma * c)[:, None, None]
    return tuple(jnp.asarray(t.astype(np.float32)) for t in (dec, xi, zeta, cd))


def _rotate_half_cols(w):
    half = w.shape[-1] // 2
    return jnp.concatenate([-w[..., half:], w[..., :half]], axis=-1)


def kernel(x, c, w_ada, b_ada, norm1, norm2, w_in, w_ret_o, q_norm, kv_norm, w_uq, w_ukv, w_mla_o, w_out,
           w_grp, b_grp, w_exp, b_exp, w1, w3, w2, final_norm):
    assert x.shape == (BATCH, SEQ, D_MODEL) and w_ada.shape[0] == 1
    x2 = x.reshape(N_TOK, D_MODEL)

    w_in_t = jnp.transpose(w_in[0])
    wq = w_uq[0].reshape(MLA_Q_LORA, MLA_HEADS, MLA_QK)
    wq = jnp.concatenate([wq, _rotate_half_cols(wq[..., MLA_NOPE:])], axis=-1)
    wq = wq.transpose(1, 0, 2).astype(BF16)
    wkv = w_ukv[0].reshape(MLA_KV_LORA, MLA_HEADS, MLA_NOPE + MLA_V).transpose(1, 0, 2).astype(BF16)
    gap = jnp.zeros((SUBLANES - N_GROUPS, D_MODEL), F32)
    tail = jnp.zeros((ROUTER_ROWS - SUBLANES - N_EXPERTS, D_MODEL), F32)
    w_rt = jnp.concatenate([w_grp[0].T, gap, w_exp[0].T, tail], axis=0)
    b_rt = jnp.concatenate([b_grp[0], gap[:, 0], b_exp[0], tail[:, 0]])
    b_rt = jnp.broadcast_to(b_rt[:, None], (ROUTER_ROWS, LANES))

    ret_cos, ret_sin = (jnp.asarray(t) for t in _rope_tables(RET_DK))
    mla_cos, mla_sin = (jnp.asarray(np.concatenate([t, t], axis=-1)) for t in _rope_tables(MLA_ROPE))
    dec, xi, zeta, cd = _decay_tables()

    mod3 = _ada(c, w_ada[0], b_ada[0]).reshape(BATCH, 6, D_MODEL)
    ret, lat, gates = _inproj(x2, mod3, norm1, ret_cos, ret_sin, mla_cos, mla_sin, q_norm, kv_norm, w_in_t)
    e_shape = (N_EXPERTS, D_MODEL, D_EXPERT)
    attn, y_ret, w1b, w3b, w2b = _mla(
        lat.reshape(BATCH, SEQ, MLA_LAT_W), wq, wkv, mla_cos, mla_sin, ret.reshape(BATCH, SEQ, 4 * RET_W),
        dec, xi, zeta, cd, w1[0].reshape(e_shape), w3[0].reshape(e_shape), w2[0].reshape(N_EXPERTS, D_EXPERT, D_MODEL))
    h1, u2t, meta_t, wtok, counts = _merge(y_ret.reshape(N_TOK, D_MODEL), attn.reshape(N_TOK, D_MODEL), gates, x2,
                                           mod3, norm2, w_ret_o, w_mla_o, w_out, w_rt, b_rt)
    cnt, seg, off_col, tile_expert, n_valid = _slot_layout(counts)
    dst = _plan(meta_t, off_col)
    d0, d1 = dst[0], dst[1]
    xs = _dispatch(d0, d1, seg, cnt, u2t)
    ys = _experts(tile_expert, n_valid, xs, w1b, w3b, w2b)
    out = _final(d0, d1, h1, wtok, mod3, final_norm.reshape(1, D_MODEL), ys)
    return out.reshape(BATCH, SEQ, D_MODEL)
```

```python
import numpy as np
import jax
import jax.numpy as jnp
from jax import lax
from jax.experimental import pallas as pl
from jax.experimental.pallas import tpu as pltpu

D_MODEL = 1024
BATCH = 8
SEQ = 2048
N_TOK = BATCH * SEQ

RET_HEADS = 4
RET_DK = 256
RET_DV = 256
RET_CHUNK = 256
RET_W = RET_HEADS * RET_DK

MLA_HEADS = 8
MLA_NOPE = 128
MLA_ROPE = 64
MLA_V = 128
MLA_Q_LORA = 384
MLA_KV_LORA = 256
MLA_LAT_W = MLA_Q_LORA + MLA_KV_LORA + 2 * MLA_ROPE
MLA_QK = MLA_NOPE + MLA_ROPE
ROPE_THETA = 10000.0

N_GROUPS = 4
EXPERTS_PER_GROUP = 8
N_EXPERTS = N_GROUPS * EXPERTS_PER_GROUP
D_EXPERT = 256
EPS = 1e-6
LOG2_E = 1.4426950408889634

LANES = 128
SUBLANES = 8
ROUTER_ROWS = 48
ROW_TILES = D_MODEL // LANES
VMEM_LIMIT = 56 * 1024 * 1024

TM_PROJ = 512
TM_INPROJ = 512
TM_FINAL = 512
TM_DISPATCH = 4096
TQ = 256
MLA_HPS = 2
TE = 512
TE_CHAINS = 2
EXPERT_RING = 3
MERGE_RING = 3
MERGE_CHAINS = 4
TOP_K = 2
N_TILES = N_TOK * TOP_K // TE + N_EXPERTS
N_SLOTS = N_TILES * TE

F32 = jnp.float32
BF16 = jnp.bfloat16


def _sigmoid(x):
    return 1.0 / (1.0 + jnp.exp(-x))


def _rms(x):
    return x * lax.rsqrt(jnp.mean(x * x, axis=-1, keepdims=True) + EPS)


def _dot(a, b):
    return jnp.dot(a, b, preferred_element_type=F32)


def _dot_nt(a, b):
    return lax.dot_general(a, b, (((1,), (1,)), ((), ())), preferred_element_type=F32)


def _dot_tn(a, b):
    return lax.dot_general(a, b, (((0,), (0,)), ((), ())), preferred_element_type=F32)


def _params(*sem):
    return pltpu.CompilerParams(dimension_semantics=sem, vmem_limit_bytes=VMEM_LIMIT)


def _resident(shape):
    nd = len(shape)
    return pl.BlockSpec(shape, lambda *_: (0,) * nd, pipeline_mode=pl.Buffered(1))


def _layer_weight(w):
    return pl.BlockSpec((None,) + w.shape[1:], lambda *_: (0,) * w.ndim, pipeline_mode=pl.Buffered(1))


def _ada_kernel(c_ref, w_ref, b_ref, o_ref):
    c = c_ref[...]
    act = (c * _sigmoid(c)).astype(BF16)
    o_ref[...] = _dot(act, w_ref[...].astype(BF16)) + b_ref[...]


def _ada(c, w_ada, b_ada):
    n = w_ada.shape[1]
    tn = D_MODEL
    return pl.pallas_call(
        _ada_kernel,
        grid=(n // tn,),
        in_specs=[pl.BlockSpec((BATCH, D_MODEL), lambda j: (0, 0)),
                  pl.BlockSpec((D_MODEL, tn), lambda j: (0, j)),
                  pl.BlockSpec((1, tn), lambda j: (0, j))],
        out_specs=pl.BlockSpec((BATCH, tn), lambda j: (0, j)),
        out_shape=jax.ShapeDtypeStruct((BATCH, n), F32),
        compiler_params=_params("arbitrary"),
        name="ada",
    )(c, w_ada, b_ada.reshape(1, n))


O_LAT = 4 * RET_W
O_PE = O_LAT + MLA_Q_LORA + MLA_KV_LORA
O_GATE = O_PE + MLA_ROPE


N_IN = O_GATE + 2 * D_MODEL
WPREP_ROWS = 512


def _stage_weights(w_hbm, ret_ref, lat_ref, gate_ref, buf, sem):
    chunks = [(s, min(WPREP_ROWS, N_IN - s)) for s in range(0, N_IN, WPREP_ROWS)]

    def copy(i):
        s, n = chunks[i]
        return pltpu.make_async_copy(w_hbm.at[pl.ds(s, n)], buf.at[i % 2, pl.ds(0, n)], sem.at[i % 2])

    groups = [(0, RET_W, ret_ref, 0, 1.0), (RET_W, 2 * RET_W, ret_ref, RET_W, RET_DK ** -0.5),
              (2 * RET_W, O_LAT, ret_ref, 2 * RET_W, 1.0), (O_LAT, O_GATE, lat_ref, 0, 1.0),
              (O_GATE, N_IN, gate_ref, 0, 1.0)]
    half = MLA_ROPE // 2
    rot = [(O_PE + half, O_GATE, O_GATE - O_LAT, -1.0), (O_PE, O_PE + half, O_GATE - O_LAT + half, 1.0)]

    copy(0).start()
    for i, (s, n) in enumerate(chunks):
        if i + 1 < len(chunks):
            copy(i + 1).start()
        copy(i).wait()
        for lo, hi, dst, dst_lo, scale in groups + [(a, b, lat_ref, d, sc) for a, b, d, sc in rot]:
            a, b = max(lo, s), min(hi, s + n)
            if a < b:
                x = buf[i % 2, a - s:b - s, :]
                dst[dst_lo + a - lo:dst_lo + b - lo, :] = (x if scale == 1.0 else x * scale).astype(BF16)


def _inproj_kernel(x_ref, mod_ref, n1_ref, cos_ref, sin_ref, mcos_ref, msin_ref, qn_ref, kvn_ref, w_hbm,
                   ret_ref, lat_ref, gate_ref, wr_ref, wm_ref, wg_ref, stage_buf, stage_sem):
    @pl.when(pl.program_id(0) == 0)
    def _():
        _stage_weights(w_hbm, wr_ref, wm_ref, wg_ref, stage_buf, stage_sem)

    y = _rms(x_ref[...]) * n1_ref[...]
    u = (y * (1.0 + mod_ref[1:2, :]) + mod_ref[0:1, :]).astype(BF16)
    cos, sin = cos_ref[...], sin_ref[...]
    half = RET_DK // 2
    for n in range(0, 2 * RET_W, RET_DK):
        p = _dot_nt(u, wr_ref[n:n + RET_DK, :])
        x1, x2 = p[:, :half], p[:, half:]
        ret_ref[:, n:n + half] = (x1 * cos - x2 * sin).astype(BF16)
        ret_ref[:, n + half:n + RET_DK] = (x2 * cos + x1 * sin).astype(BF16)
    step = 512
    for n in range(2 * RET_W, 3 * RET_W, step):
        ret_ref[:, n:n + step] = _dot_nt(u, wr_ref[n:n + step, :]).astype(BF16)
    for n in range(3 * RET_W, 4 * RET_W, step):
        p = _dot_nt(u, wr_ref[n:n + step, :])
        ret_ref[:, n:n + step] = (p * _sigmoid(p)).astype(BF16)
    lat = _dot_nt(u, wm_ref[...])
    o_kv, o_pe, o_rot = MLA_Q_LORA, MLA_Q_LORA + MLA_KV_LORA, MLA_Q_LORA + MLA_KV_LORA + MLA_ROPE
    lat_ref[:, :o_kv] = (_rms(lat[:, :o_kv]) * qn_ref[...]).astype(BF16)
    lat_ref[:, o_kv:o_pe] = (_rms(lat[:, o_kv:o_pe]) * kvn_ref[...]).astype(BF16)
    lat_ref[:, o_pe:o_rot] = (lat[:, o_pe:o_rot] * mcos_ref[...] + lat[:, o_rot:] * msin_ref[...]).astype(BF16)
    lat_ref[:, o_rot:] = jnp.zeros((lat.shape[0], MLA_ROPE), BF16)
    for n in range(0, 2 * D_MODEL, step):
        gate_ref[:, n:n + step] = _dot_nt(u, wg_ref[n:n + step, :]).astype(BF16)


def _inproj(x2, mod3, norm1, cos, sin, mla_cos, mla_sin, q_norm, kv_norm, w_in_t):
    tm = TM_INPROJ
    per_b = SEQ // tm
    rope_tab = pl.BlockSpec((tm, RET_DK // 2), lambda i: (i % per_b, 0))
    mla_tab = pl.BlockSpec((tm, MLA_ROPE), lambda i: (i % per_b, 0))
    return pl.pallas_call(
        _inproj_kernel,
        grid=(N_TOK // tm,),
        in_specs=[pl.BlockSpec((tm, D_MODEL), lambda i: (i, 0)),
                  pl.BlockSpec((None, 6, D_MODEL), lambda i: (i // per_b, 0, 0)),
                  _resident((1, D_MODEL)), rope_tab, rope_tab, mla_tab, mla_tab,
                  _resident(q_norm.shape), _resident(kv_norm.shape),
                  pl.BlockSpec(memory_space=pl.ANY)],
        out_specs=[pl.BlockSpec((tm, 4 * RET_W), lambda i: (i, 0)),
                   pl.BlockSpec((tm, MLA_LAT_W), lambda i: (i, 0)),
                   pl.BlockSpec((tm, 2 * D_MODEL), lambda i: (i, 0))],
        out_shape=[jax.ShapeDtypeStruct((N_TOK, 4 * RET_W), BF16),
                   jax.ShapeDtypeStruct((N_TOK, MLA_LAT_W), BF16),
                   jax.ShapeDtypeStruct((N_TOK, 2 * D_MODEL), BF16)],
        scratch_shapes=[pltpu.VMEM((O_LAT, D_MODEL), BF16), pltpu.VMEM((MLA_LAT_W, D_MODEL), BF16),
                        pltpu.VMEM((2 * D_MODEL, D_MODEL), BF16),
                        pltpu.VMEM((2, WPREP_ROWS, D_MODEL), F32), pltpu.SemaphoreType.DMA((2,))],
        compiler_params=_params("arbitrary"),
        name="inproj",
    )(x2, mod3, norm1, cos, sin, mla_cos, mla_sin, q_norm, kv_norm, w_in_t)


def _mla_kernel(lat_ref, wq_ref, wkv_ref, cos_ref, sin_ref,
                rq_ref, rk_ref, rv_ref, rg_ref, dec_ref, xi_ref, zeta_ref, cd_ref, w1_ref, w3_ref, w2_ref,
                o_ref, yr_ref, w1b_ref, w3b_ref, w2b_ref, q_s, k_s, v_s, state_ref):
    h = pl.program_id(1)
    o_kv, o_pe, o_rot = MLA_Q_LORA, MLA_Q_LORA + MLA_KV_LORA, MLA_Q_LORA + MLA_KV_LORA + MLA_ROPE
    cos, sin = cos_ref[...], sin_ref[...]

    scale = (MLA_QK ** -0.5) * LOG2_E
    for g in range(MLA_HPS):
        qf = _dot(lat_ref[:, :o_kv], wq_ref[g])
        q_s[g, :, :MLA_NOPE] = (qf[:, :MLA_NOPE] * scale).astype(BF16)
        q_pe = qf[:, MLA_NOPE:MLA_QK] * cos + qf[:, MLA_QK:] * sin
        q_s[g, :, MLA_NOPE:] = (q_pe * scale).astype(BF16)
        kvf = _dot(lat_ref[:, o_kv:o_pe], wkv_ref[g])
        k_s[g, :, :MLA_NOPE] = kvf[:, :MLA_NOPE].astype(BF16)
        k_s[g, :, MLA_NOPE:] = lat_ref[:, o_pe:o_rot]
        v_s[g, :, :MLA_V] = kvf[:, MLA_NOPE:].astype(BF16)
        v_s[g, :, MLA_V:] = jnp.ones((SEQ, MLA_V), BF16)

    causal = lax.broadcasted_iota(jnp.int32, (TQ, TQ), 0) >= lax.broadcasted_iota(jnp.int32, (TQ, TQ), 1)
    heads = range(MLA_HPS)
    n_blk = SEQ // TQ

    def scores(i):
        lo, hi = i * TQ, (i + 1) * TQ
        diag = [jnp.where(causal, _dot_nt(q_s[g, lo:hi, :], k_s[g, lo:hi, :]), -jnp.inf) for g in heads]
        past = [_dot_nt(q_s[g, lo:hi, :], k_s[g, :lo, :]) if i > 0 else None for g in heads]
        return diag, past

    state_ref[...] = jnp.zeros_like(state_ref)
    pending = scores(0)
    for i in range(n_blk):
        lo, hi = i * TQ, (i + 1) * TQ
        diag, past = pending
        if i + 1 < n_blk:
            pending = scores(i + 1)
        rq, rk, rv = rq_ref[lo:hi, :], rk_ref[lo:hi, :], rv_ref[lo:hi, :]
        r_scores = (_dot_nt(rq, rk) * dec_ref[h]).astype(BF16)
        carried = xi_ref[h] * _dot(rq, state_ref[...].astype(BF16))
        m = [jnp.max(diag[g], axis=-1, keepdims=True) for g in heads]
        if i > 0:
            m = [jnp.maximum(m[g], jnp.max(past[g], axis=-1, keepdims=True)) for g in heads]
        ry = _dot(r_scores, rv) + carried
        k_dec = (rk.astype(F32) * zeta_ref[h]).astype(BF16)
        state_ref[...] = state_ref[...] * cd_ref[h] + _dot_tn(k_dec, rv)
        acc = [_dot(jnp.exp2(diag[g] - m[g]).astype(BF16), v_s[g, lo:hi, :]) for g in heads]
        if i > 0:
            acc = [acc[g] + _dot(jnp.exp2(past[g] - m[g]).astype(BF16), v_s[g, :lo, :]) for g in heads]
        if i < 3:
            src, dst = ((w1_ref, w1b_ref), (w3_ref, w3b_ref), (w2_ref, w2b_ref))[i]
            dst[...] = src[...].astype(BF16)
        ryc = ry - jnp.mean(ry, axis=-1, keepdims=True)
        r_inv = lax.rsqrt(jnp.mean(ryc * ryc, axis=-1, keepdims=True) + EPS)
        yr_ref[lo:hi, :] = (rg_ref[lo:hi, :].astype(F32) * (ryc * r_inv)).astype(BF16)
        for g in heads:
            o_ref[lo:hi, g * MLA_V:(g + 1) * MLA_V] = (acc[g][:, :MLA_V] / acc[g][:, MLA_V:]).astype(BF16)


def _mla(lat3, wq, wkv, cos, sin, ret3, dec, xi, zeta, cd, w1, w3, w2):
    hps = MLA_HPS
    steps = MLA_HEADS // hps
    assert steps == RET_HEADS and TQ == RET_CHUNK and BATCH * steps == N_EXPERTS
    expert = lambda w: pl.BlockSpec((None,) + w.shape[1:], lambda b, h: (b * steps + h, 0, 0))
    ret_part = lambda part: pl.BlockSpec((None, SEQ, RET_DK), lambda b, h: (b, 0, part * RET_HEADS + h))
    whole = lambda a: pl.BlockSpec(a.shape, lambda b, h: (0,) * a.ndim)
    return pl.pallas_call(
        _mla_kernel,
        grid=(BATCH, MLA_HEADS // hps),
        in_specs=[pl.BlockSpec((None, SEQ, MLA_LAT_W), lambda b, h: (b, 0, 0)),
                  pl.BlockSpec((hps, MLA_Q_LORA, MLA_QK + MLA_ROPE), lambda b, h: (h, 0, 0)),
                  pl.BlockSpec((hps, MLA_KV_LORA, MLA_NOPE + MLA_V), lambda b, h: (h, 0, 0)),
                  pl.BlockSpec((SEQ, MLA_ROPE), lambda b, h: (0, 0)),
                  pl.BlockSpec((SEQ, MLA_ROPE), lambda b, h: (0, 0)),
                  ret_part(0), ret_part(1), ret_part(2), ret_part(3),
                  whole(dec), whole(xi), whole(zeta), whole(cd), expert(w1), expert(w3), expert(w2)],
        out_specs=[pl.BlockSpec((None, SEQ, hps * MLA_V), lambda b, h: (b, 0, h)),
                   pl.BlockSpec((None, SEQ, RET_DV), lambda b, h: (b, 0, h)),
                   expert(w1), expert(w3), expert(w2)],
        out_shape=[jax.ShapeDtypeStruct((BATCH, SEQ, MLA_HEADS * MLA_V), BF16),
                   jax.ShapeDtypeStruct((BATCH, SEQ, RET_HEADS * RET_DV), BF16)]
                  + [jax.ShapeDtypeStruct(w.shape, BF16) for w in (w1, w3, w2)],
        scratch_shapes=[pltpu.VMEM((hps, SEQ, MLA_QK), BF16),
                        pltpu.VMEM((hps, SEQ, MLA_QK), BF16),
                        pltpu.VMEM((hps, SEQ, 2 * MLA_V), BF16),
                        pltpu.VMEM((RET_DK, RET_DV), F32)],
        compiler_params=_params("arbitrary", "arbitrary"),
        name="mla",
    )(lat3, wq, wkv, cos, sin, ret3, ret3, ret3, ret3, dec, xi, zeta, cd, w1, w3, w2)


def _route(logits_t):
    tm = logits_t.shape[1]
    row = lax.broadcasted_iota(jnp.int32, (SUBLANES, tm), 0)
    neg = -jnp.inf
    gl = jnp.where(row < N_GROUPS, logits_t[:SUBLANES], neg)
    gmax = jnp.max(gl, axis=0, keepdims=True)
    gsel = jnp.min(jnp.where(gl == gmax, row, SUBLANES), axis=0, keepdims=True)
    p_grp = 1.0 / jnp.sum(jnp.exp(gl - gmax), axis=0, keepdims=True)
    el = logits_t[SUBLANES * N_GROUPS:SUBLANES * (N_GROUPS + 1)]
    for g in reversed(range(N_GROUPS - 1)):
        el = jnp.where(gsel == g, logits_t[SUBLANES * (g + 1):SUBLANES * (g + 2)], el)
    v0 = jnp.max(el, axis=0, keepdims=True)
    i0 = jnp.min(jnp.where(el == v0, row, SUBLANES), axis=0, keepdims=True)
    el1 = jnp.where(row == i0, neg, el)
    v1 = jnp.max(el1, axis=0, keepdims=True)
    i1 = jnp.min(jnp.where(el1 == v1, row, SUBLANES), axis=0, keepdims=True)
    t = jnp.exp(v1 - v0)
    w0 = p_grp / (1.0 + t)
    w1 = p_grp * t / (1.0 + t)
    return gsel * EXPERTS_PER_GROUP + i0, gsel * EXPERTS_PER_GROUP + i1, w0, w1


def _stack_rows(rows, n):
    tm = rows[0].shape[1]
    row = lax.broadcasted_iota(jnp.int32, (n, tm), 0)
    out = jnp.zeros((n, tm), F32)
    for k, r in enumerate(rows):
        out = jnp.where(row == k, r, out)
    return out


def _to_token_tiles(ref, val):
    n = val.shape[0]
    for s in range(ROW_TILES):
        ref[pl.ds(s, n, stride=ROW_TILES), :] = val[:, s * LANES:(s + 1) * LANES]


def _from_token_tiles(ref):
    n = ref.shape[0] // ROW_TILES
    return jnp.concatenate([ref[pl.ds(s, n, stride=ROW_TILES), :] for s in range(ROW_TILES)], axis=-1)


def _token_rows(ref, t):
    return ref.at[pl.ds(pl.multiple_of(t * ROW_TILES, ROW_TILES), ROW_TILES)]


def _merge_kernel(yr_hbm, at_hbm, g_hbm, x_hbm, mod_ref, n2_ref, wro_ref, wmo_ref, wo_ref,
                  wrt_ref, brt_ref, h1_ref, u2_ref, meta_ref, wtok_ref, cnt_ref, carry_ref, wro_s, wmo_s, wo_s,
                  yr_buf, at_buf, g_buf, x_buf, sem):
    tm = x_buf.shape[1]
    i = pl.program_id(0)
    n = pl.num_programs(0)

    def fetch(t):
        slot = t % MERGE_RING
        r = pl.ds(pl.multiple_of(t * tm, tm), tm)
        return [pltpu.make_async_copy(src.at[r], dst.at[slot], sem.at[slot])
                for src, dst in ((yr_hbm, yr_buf), (at_hbm, at_buf), (g_hbm, g_buf), (x_hbm, x_buf))]

    @pl.when(i == 0)
    def _():
        for t in range(2):
            for cp in fetch(t):
                cp.start()

    @pl.when(i + 2 < n)
    def _():
        for cp in fetch(i + 2):
            cp.start()

    @pl.when(i == 0)
    def _():
        carry_ref[...] = jnp.zeros_like(carry_ref)
        wro_s[...] = wro_ref[...].astype(BF16)
        wmo_s[...] = wmo_ref[...].astype(BF16)
        wo_s[...] = wo_ref[...].astype(BF16)

    chunks = range(MERGE_CHAINS)
    sub = tm // MERGE_CHAINS
    rows = [slice(c * sub, (c + 1) * sub) for c in chunks]
    for cp in fetch(i):
        cp.wait()
    slot = i % MERGE_RING
    yr_ref, at_ref, g_ref, x_ref = yr_buf.at[slot], at_buf.at[slot], g_buf.at[slot], x_buf.at[slot]
    y_ret = [_dot(yr_ref[r, :], wro_s[...]) for r in rows]
    y_mla = [_dot(at_ref[r, :], wmo_s[...]) for r in rows]
    merged = [(_sigmoid(g_ref[rows[c], :D_MODEL].astype(F32)) * y_ret[c]
               + _sigmoid(g_ref[rows[c], D_MODEL:].astype(F32)) * y_mla[c]).astype(BF16) for c in chunks]
    o = [_dot(merged[c], wo_s[...]) for c in chunks]
    h1 = [x_ref[rows[c], :] + mod_ref[2:3, :] * o[c] for c in chunks]
    for c in chunks:
        h1_ref[rows[c], :] = h1[c]
    u2 = [_rms(h1[c]) * n2_ref[...] * (1.0 + mod_ref[4:5, :]) + mod_ref[3:4, :] for c in chunks]
    for c in chunks:
        _to_token_tiles(u2_ref.at[pl.ds(c * sub * ROW_TILES, sub * ROW_TILES)], u2[c])
    w = wrt_ref[...]
    w_hi = w.astype(BF16)
    w_lo = (w - w_hi.astype(F32)).astype(BF16)
    w_both = jnp.concatenate([w_hi, w_lo], axis=0)
    u_hi = [u2[c].astype(BF16) for c in chunks]
    u_lo = [(u2[c] - u_hi[c].astype(F32)).astype(BF16) for c in chunks]
    by_hi = [_dot_nt(w_both, u_hi[c]) for c in chunks]
    logits_t = [by_hi[c][:ROUTER_ROWS] + by_hi[c][ROUTER_ROWS:] + _dot_nt(w_hi, u_lo[c]) + brt_ref[:, 0:1]
                for c in chunks]
    routed = [_route(logits_t[c]) for c in chunks]
    e0, e1, w0, w1 = [jnp.concatenate([routed[c][k] for c in chunks], axis=1) for k in range(4)]
    erow = lax.broadcasted_iota(jnp.int32, (N_EXPERTS, tm), 0)
    m0, m1 = erow == e0, erow == e1
    member = jnp.where(m0 | m1, 1.0, 0.0)
    earlier = jnp.where(lax.broadcasted_iota(jnp.int32, (tm, tm), 0) < lax.broadcasted_iota(jnp.int32, (tm, tm), 1),
                        1.0, 0.0).astype(BF16)
    prefix = _dot(member.astype(BF16), earlier) + carry_ref[:, 0:1]
    rank0 = jnp.sum(jnp.where(m0, prefix, 0.0), axis=0, keepdims=True)
    rank1 = jnp.sum(jnp.where(m1, prefix, 0.0), axis=0, keepdims=True)
    carry_ref[...] = carry_ref[...] + jnp.sum(member, axis=1, keepdims=True)
    cnt_ref[...] = carry_ref[...]
    meta_ref[...] = _stack_rows([e0.astype(F32), e1.astype(F32), rank0, rank1], SUBLANES)
    wt = _stack_rows([w0, w1], 2 * SUBLANES)
    wt_hi = wt.astype(BF16)
    wt_lo = (wt - wt_hi.astype(F32)).astype(BF16)
    place = jnp.where(lax.broadcasted_iota(jnp.int32, (2 * SUBLANES, LANES), 0)
                      == lax.broadcasted_iota(jnp.int32, (2 * SUBLANES, LANES), 1), 1.0, 0.0).astype(BF16)
    wtok_ref[...] = _dot_tn(wt_hi, place) + _dot_tn(wt_lo, place)


def _merge(y_ret, attn, gates, x2, mod3, norm2, w_ret_o, w_mla_o, w_out, w_rt, b_rt):
    tm = TM_PROJ
    per_b = SEQ // tm
    row = lambda j: pl.BlockSpec((tm, D_MODEL), lambda i: (i, j))
    return pl.pallas_call(
        _merge_kernel,
        grid=(N_TOK // tm,),
        in_specs=[pl.BlockSpec(memory_space=pl.ANY)] * 4 + [
                  pl.BlockSpec((None, 6, D_MODEL), lambda i: (i // per_b, 0, 0)),
                  _resident((1, D_MODEL)),
                  _layer_weight(w_ret_o), _layer_weight(w_mla_o), _layer_weight(w_out),
                  _resident(w_rt.shape), _resident(b_rt.shape)],
        out_specs=[row(0),
                   pl.BlockSpec((tm * ROW_TILES, LANES), lambda i: (i, 0)),
                   pl.BlockSpec((SUBLANES, tm), lambda i: (0, i)),
                   pl.BlockSpec((tm, LANES), lambda i: (i, 0)),
                   pl.BlockSpec((N_EXPERTS, LANES), lambda i: (0, 0))],
        out_shape=[jax.ShapeDtypeStruct((N_TOK, D_MODEL), F32),
                   jax.ShapeDtypeStruct((N_TOK * ROW_TILES, LANES), F32),
                   jax.ShapeDtypeStruct((SUBLANES, N_TOK), F32),
                   jax.ShapeDtypeStruct((N_TOK, LANES), F32),
                   jax.ShapeDtypeStruct((N_EXPERTS, LANES), F32)],
        scratch_shapes=[pltpu.VMEM((N_EXPERTS, LANES), F32)] + [pltpu.VMEM((D_MODEL, D_MODEL), BF16)] * 3 + [
                        pltpu.VMEM((MERGE_RING, tm, D_MODEL), BF16), pltpu.VMEM((MERGE_RING, tm, D_MODEL), BF16),
                        pltpu.VMEM((MERGE_RING, tm, 2 * D_MODEL), BF16), pltpu.VMEM((MERGE_RING, tm, D_MODEL), F32),
                        pltpu.SemaphoreType.DMA((MERGE_RING,))],
        compiler_params=_params("arbitrary"),
        name="merge",
    )(y_ret, attn, gates, x2, mod3, norm2, w_ret_o, w_mla_o, w_out, w_rt, b_rt)


def _plan_kernel(meta_ref, off_ref, dst_ref):
    m = meta_ref[...]
    erow = lax.broadcasted_iota(jnp.int32, (N_EXPERTS, m.shape[1]), 0)
    off = off_ref[:, 0:1]
    d = [jnp.sum(jnp.where(erow == m[k:k + 1].astype(jnp.int32), off, 0.0), axis=0, keepdims=True) + m[k + 2:k + 3]
         for k in range(TOP_K)]
    dst_ref[...] = _stack_rows(d, SUBLANES).astype(jnp.int32)


def _plan(meta_t, off_col):
    tm = 2048
    return pl.pallas_call(
        _plan_kernel,
        grid=(N_TOK // tm,),
        in_specs=[pl.BlockSpec((SUBLANES, tm), lambda i: (0, i)),
                  pl.BlockSpec((N_EXPERTS, LANES), lambda i: (0, 0))],
        out_specs=pl.BlockSpec((SUBLANES, tm), lambda i: (0, i)),
        out_shape=jax.ShapeDtypeStruct((SUBLANES, N_TOK), jnp.int32),
        compiler_params=_params("arbitrary"),
        name="plan",
    )(meta_t, off_col)


def _row_copy_wait(src_like, dst_like, sem):
    pltpu.make_async_copy(src_like, dst_like, sem).wait()


def _dispatch_kernel(d0_ref, d1_ref, seg_ref, cnt_ref, u_ref, xs_ref, zero_ref, sem, zsem):
    i = pl.program_id(0)
    tm = u_ref.shape[0] // ROW_TILES
    tile_rows = TE * ROW_TILES

    def slot_tile(j):
        return xs_ref.at[pl.ds(pl.multiple_of(j * tile_rows, tile_rows), tile_rows)]

    def pad(e):
        first = seg_ref[e] + cnt_ref[e]
        rows = (seg_ref[e + 1] - first) * ROW_TILES
        start = pl.multiple_of(first * ROW_TILES, ROW_TILES)
        return rows > 0, pltpu.make_async_copy(zero_ref.at[pl.ds(0, rows)], xs_ref.at[pl.ds(start, rows)], zsem)

    def unused(j):
        return pltpu.make_async_copy(zero_ref, slot_tile(j), zsem)

    first_unused = seg_ref[N_EXPERTS] // TE

    @pl.when(i == 0)
    def _():
        zero_ref[...] = jnp.zeros_like(zero_ref)
        for e in range(N_EXPERTS):
            nonempty, copy = pad(e)
            pl.when(nonempty)(copy.start)
        lax.fori_loop(first_unused, N_TILES, lambda j, c: (unused(j).start(), c)[1], 0)

    base = i * tm

    def body(r, carry):
        src = _token_rows(u_ref, r)
        pltpu.make_async_copy(src, _token_rows(xs_ref, d0_ref[base + r]), sem).start(priority=0)
        pltpu.make_async_copy(src, _token_rows(xs_ref, d1_ref[base + r]), sem).start(priority=1)
        return carry

    lax.fori_loop(0, tm, body, 0, unroll=8)
    _row_copy_wait(u_ref, xs_ref.at[pl.ds(0, tm * ROW_TILES)], sem)
    _row_copy_wait(u_ref, xs_ref.at[pl.ds(0, tm * ROW_TILES)], sem)

    @pl.when(i == 0)
    def _():
        for e in range(N_EXPERTS):
            nonempty, copy = pad(e)
            pl.when(nonempty)(copy.wait)
        lax.fori_loop(first_unused, N_TILES, lambda j, c: (unused(j).wait(), c)[1], 0)


def _dispatch(d0, d1, seg, cnt, u2t):
    tm = TM_DISPATCH
    return pl.pallas_call(
        _dispatch_kernel,
        grid_spec=pltpu.PrefetchScalarGridSpec(
            num_scalar_prefetch=4,
            grid=(N_TOK // tm,),
            in_specs=[pl.BlockSpec((tm * ROW_TILES, LANES), lambda i, *_: (i, 0))],
            out_specs=pl.BlockSpec(memory_space=pl.ANY),
            scratch_shapes=[pltpu.VMEM((TE * ROW_TILES, LANES), F32),
                            pltpu.SemaphoreType.DMA(()), pltpu.SemaphoreType.DMA(())]),
        out_shape=jax.ShapeDtypeStruct((N_SLOTS * ROW_TILES, LANES), F32),
        compiler_params=_params("arbitrary"),
        name="dispatch",
    )(d0, d1, seg, cnt, u2t)


def _expert_kernel(te_ref, nv_ref, x_hbm, w1_s, w3_s, w2_s, y_hbm, xbuf, ybuf, sem, wsem):
    j = pl.program_id(0)
    nv = nv_ref[0]
    rows = TE * ROW_TILES

    def fetch(t):
        return pltpu.make_async_copy(x_hbm.at[pl.ds(pl.multiple_of(t * rows, rows), rows)],
                                     xbuf.at[t % EXPERT_RING], sem.at[t % EXPERT_RING])

    def put(t):
        return pltpu.make_async_copy(ybuf.at[t % 2], y_hbm.at[pl.ds(pl.multiple_of(t * rows, rows), rows)],
                                     wsem.at[t % 2])

    @pl.when(j == 0)
    def _():
        fetch(0).start()

        @pl.when(1 < nv)
        def _():
            fetch(1).start()

    @pl.when(j + 2 < nv)
    def _():
        fetch(j + 2).start()

    @pl.when(j < nv)
    def _():
        fetch(j).wait()
        x_ref = xbuf.at[j % EXPERT_RING]
        sub = TE // TE_CHAINS * ROW_TILES
        part = lambda ref, c: ref.at[pl.ds(c * sub, sub)]
        chains = range(TE_CHAINS)
        x = [_from_token_tiles(part(x_ref, c)).astype(BF16) for c in chains]
        a = [_dot(x[c], w1_s[...]) for c in chains]
        b = [_dot(x[c], w3_s[...]) for c in chains]
        hid = [(a[c] * _sigmoid(a[c]) * b[c]).astype(BF16) for c in chains]

        @pl.when(j >= 2)
        def _():
            put(j - 2).wait()

        y_ref = ybuf.at[j % 2]
        for c in chains:
            _to_token_tiles(part(y_ref, c), _dot(hid[c], w2_s[...]))
        put(j).start()

        @pl.when(j == nv - 1)
        def _():
            @pl.when(j >= 1)
            def _():
                put(j - 1).wait()

            put(j).wait()


def _experts(tile_expert, n_valid, xs, w1, w3, w2):
    tile = lambda j, te, nv: jnp.minimum(j, nv[0] - 1)
    wspec = lambda shape: pl.BlockSpec((None,) + shape, lambda j, te, nv: (te[tile(j, te, nv)], 0, 0))
    return pl.pallas_call(
        _expert_kernel,
        grid_spec=pltpu.PrefetchScalarGridSpec(
            num_scalar_prefetch=2,
            grid=(N_TILES,),
            in_specs=[pl.BlockSpec(memory_space=pl.ANY),
                      wspec((D_MODEL, D_EXPERT)), wspec((D_MODEL, D_EXPERT)), wspec((D_EXPERT, D_MODEL))],
            out_specs=pl.BlockSpec(memory_space=pl.ANY),
            scratch_shapes=[pltpu.VMEM((EXPERT_RING, TE * ROW_TILES, LANES), F32),
                            pltpu.VMEM((2, TE * ROW_TILES, LANES), F32),
                            pltpu.SemaphoreType.DMA((EXPERT_RING,)),
                            pltpu.SemaphoreType.DMA((2,))]),
        out_shape=jax.ShapeDtypeStruct((N_SLOTS * ROW_TILES, LANES), F32),
        input_output_aliases={2: 0},
        compiler_params=_params("arbitrary"),
        name="experts",
    )(tile_expert, n_valid, xs, w1, w3, w2)


def _final_kernel(d0_ref, d1_ref, h1_ref, meta_ref, mod_ref, fn_ref, ys_ref, o_ref, ybuf, sem):
    i = pl.program_id(0)
    tm = h1_ref.shape[0]

    def gather(t):
        buf, s = ybuf.at[t % 2], sem.at[t % 2]
        base = t * tm

        def body(r, carry):
            pltpu.make_async_copy(_token_rows(ys_ref, d0_ref[base + r]), _token_rows(buf.at[0], r), s).start(priority=0)
            pltpu.make_async_copy(_token_rows(ys_ref, d1_ref[base + r]), _token_rows(buf.at[1], r), s).start(priority=1)
            return carry

        lax.fori_loop(0, tm, body, 0, unroll=8)

    @pl.when(i == 0)
    def _():
        gather(0)

    @pl.when(i + 1 < pl.num_programs(0))
    def _():
        gather(i + 1)

    buf = ybuf.at[i % 2]
    for k in range(TOP_K):
        _row_copy_wait(ys_ref.at[pl.ds(0, tm * ROW_TILES)], buf.at[k], sem.at[i % 2])
    m = meta_ref[...]
    moe = m[:, 0:1] * _from_token_tiles(buf.at[0]) + m[:, 1:2] * _from_token_tiles(buf.at[1])
    h2 = h1_ref[...] + mod_ref[5:6, :] * moe
    o_ref[...] = _rms(h2) * fn_ref[...]


def _final(d0, d1, h1, meta, mod3, final_norm, ys):
    tm = TM_FINAL
    per_b = SEQ // tm
    return pl.pallas_call(
        _final_kernel,
        grid_spec=pltpu.PrefetchScalarGridSpec(
            num_scalar_prefetch=2,
            grid=(N_TOK // tm,),
            in_specs=[pl.BlockSpec((tm, D_MODEL), lambda i, *_: (i, 0)),
                      pl.BlockSpec((tm, LANES), lambda i, *_: (i, 0)),
                      pl.BlockSpec((None, 6, D_MODEL), lambda i, *_: (i // per_b, 0, 0)),
                      pl.BlockSpec((1, D_MODEL), lambda i, *_: (0, 0)),
                      pl.BlockSpec(memory_space=pl.ANY)],
            out_specs=pl.BlockSpec((tm, D_MODEL), lambda i, *_: (i, 0)),
            scratch_shapes=[pltpu.VMEM((2, TOP_K, tm * ROW_TILES, LANES), F32), pltpu.SemaphoreType.DMA((2,))]),
        out_shape=jax.ShapeDtypeStruct((N_TOK, D_MODEL), F32),
        compiler_params=_params("arbitrary"),
        name="final",
    )(d0, d1, h1, meta, mod3, final_norm, ys)


def _slot_layout(counts):
    cnt = counts[:, 0].astype(jnp.int32)
    tile_end = jnp.cumsum((cnt + TE - 1) // TE)
    seg = jnp.concatenate([jnp.zeros((1,), jnp.int32), tile_end * TE])
    off_col = jnp.broadcast_to(seg[:-1].astype(F32)[:, None], (N_EXPERTS, LANES))
    tile_ids = jnp.arange(N_TILES, dtype=jnp.int32)
    tile_expert = jnp.sum((tile_end[None, :] <= tile_ids[:, None]).astype(jnp.int32), axis=1)
    tile_expert = jnp.minimum(tile_expert, N_EXPERTS - 1)
    return cnt, seg, off_col, tile_expert, tile_end[-1:]


def _rope_tables(dim):
    pos = np.arange(SEQ, dtype=np.float64)
    inv = ROPE_THETA ** (-np.arange(0, dim, 2, dtype=np.float64) / dim)
    ang = pos[:, None] * inv[None, :]
    return np.cos(ang).astype(np.float32), np.sin(ang).astype(np.float32)


def _decay_tables():
    c = RET_CHUNK
    log_gamma = np.log1p(-np.exp2(-5.0 - np.arange(RET_HEADS, dtype=np.float64)))
    idx = np.arange(c, dtype=np.float64)
    rel = idx[:, None] - idx[None, :]
    dec = np.where(rel[None] >= 0, np.exp(log_gamma[:, None, None] * np.maximum(rel, 0.0)[None]), 0.0)
    xi = np.exp(log_gamma[:, None] * (idx[None, :] + 1.0))[:, :, None]
    zeta = np.exp(log_gamma[:, None] * (c - 1.0 - idx[None, :]))[:, :, None]
    cd = np.exp(log_gamma * c)[:, None, None]
    return tuple(jnp.asarray(t.astype(np.float32)) for t in (dec, xi, zeta, cd))


def _rotate_half_cols(w):
    half = w.shape[-1] // 2
    return jnp.concatenate([-w[..., half:], w[..., :half]], axis=-1)


def kernel(x, c, w_ada, b_ada, norm1, norm2, w_in, w_ret_o, q_norm, kv_norm, w_uq, w_ukv, w_mla_o, w_out,
           w_grp, b_grp, w_exp, b_exp, w1, w3, w2, final_norm):
    assert x.shape == (BATCH, SEQ, D_MODEL) and w_ada.shape[0] == 1
    x2 = x.reshape(N_TOK, D_MODEL)

    w_in_t = jnp.transpose(w_in[0])
    wq = w_uq[0].reshape(MLA_Q_LORA, MLA_HEADS, MLA_QK)
    wq = jnp.concatenate([wq, _rotate_half_cols(wq[..., MLA_NOPE:])], axis=-1)
    wq = wq.transpose(1, 0, 2).astype(BF16)
    wkv = w_ukv[0].reshape(MLA_KV_LORA, MLA_HEADS, MLA_NOPE + MLA_V).transpose(1, 0, 2).astype(BF16)
    gap = jnp.zeros((SUBLANES - N_GROUPS, D_MODEL), F32)
    tail = jnp.zeros((ROUTER_ROWS - SUBLANES - N_EXPERTS, D_MODEL), F32)
    w_rt = jnp.concatenate([w_grp[0].T, gap, w_exp[0].T, tail], axis=0)
    b_rt = jnp.concatenate([b_grp[0], gap[:, 0], b_exp[0], tail[:, 0]])
    b_rt = jnp.broadcast_to(b_rt[:, None], (ROUTER_ROWS, LANES))

    ret_cos, ret_sin = (jnp.asarray(t) for t in _rope_tables(RET_DK))
    mla_cos, mla_sin = (jnp.asarray(np.concatenate([t, t], axis=-1)) for t in _rope_tables(MLA_ROPE))
    dec, xi, zeta, cd = _decay_tables()

    mod3 = _ada(c, w_ada[0], b_ada[0]).reshape(BATCH, 6, D_MODEL)
    ret, lat, gates = _inproj(x2, mod3, norm1, ret_cos, ret_sin, mla_cos, mla_sin, q_norm, kv_norm, w_in_t)
    e_shape = (N_EXPERTS, D_MODEL, D_EXPERT)
    attn, y_ret, w1b, w3b, w2b = _mla(
        lat.reshape(BATCH, SEQ, MLA_LAT_W), wq, wkv, mla_cos, mla_sin, ret.reshape(BATCH, SEQ, 4 * RET_W),
        dec, xi, zeta, cd, w1[0].reshape(e_shape), w3[0].reshape(e_shape), w2[0].reshape(N_EXPERTS, D_EXPERT, D_MODEL))
    h1, u2t, meta_t, wtok, counts = _merge(y_ret.reshape(N_TOK, D_MODEL), attn.reshape(N_TOK, D_MODEL), gates, x2,
                                           mod3, norm2, w_ret_o, w_mla_o, w_out, w_rt, b_rt)
    cnt, seg, off_col, tile_expert, n_valid = _slot_layout(counts)
    dst = _plan(meta_t, off_col)
    d0, d1 = dst[0], dst[1]
    xs = _dispatch(d0, d1, seg, cnt, u2t)
    ys = _experts(tile_expert, n_valid, xs, w1b, w3b, w2b)
    out = _final(d0, d1, h1, wtok, mod3, final_norm.reshape(1, D_MODEL), ys)
    return out.reshape(BATCH, SEQ, D_MODEL)
```

```python
import numpy as np
import jax
import jax.numpy as jnp
from jax import lax
from jax.experimental import pallas as pl
from jax.experimental.pallas import tpu as pltpu

D_MODEL = 1024
BATCH = 8
SEQ = 2048
N_TOK = BATCH * SEQ

RET_HEADS = 4
RET_DK = 256
RET_DV = 256
RET_CHUNK = 256
RET_W = RET_HEADS * RET_DK

MLA_HEADS = 8
MLA_NOPE = 128
MLA_ROPE = 64
MLA_V = 128
MLA_Q_LORA = 384
MLA_KV_LORA = 256
MLA_LAT_W = MLA_Q_LORA + MLA_KV_LORA + 2 * MLA_ROPE
MLA_QK = MLA_NOPE + MLA_ROPE
ROPE_THETA = 10000.0

N_GROUPS = 4
EXPERTS_PER_GROUP = 8
N_EXPERTS = N_GROUPS * EXPERTS_PER_GROUP
D_EXPERT = 256
EPS = 1e-6
LOG2_E = 1.4426950408889634

LANES = 128
SUBLANES = 8
ROUTER_ROWS = 48
ROW_TILES = D_MODEL // LANES
VMEM_LIMIT = 56 * 1024 * 1024

TM_PROJ = 512
TM_INPROJ = 512
TM_FINAL = 512
TM_DISPATCH = 4096
TQ = 256
MLA_HPS = 2
TE = 512
TE_CHAINS = 4
EXPERT_RING = 3
MERGE_CHAINS = 4
TOP_K = 2
N_TILES = N_TOK * TOP_K // TE + N_EXPERTS
N_SLOTS = N_TILES * TE

F32 = jnp.float32
BF16 = jnp.bfloat16


def _sigmoid(x):
    return 1.0 / (1.0 + jnp.exp(-x))


def _rms(x):
    return x * lax.rsqrt(jnp.mean(x * x, axis=-1, keepdims=True) + EPS)


def _dot(a, b):
    return jnp.dot(a, b, preferred_element_type=F32)


def _dot_nt(a, b):
    return lax.dot_general(a, b, (((1,), (1,)), ((), ())), preferred_element_type=F32)


def _dot_tn(a, b):
    return lax.dot_general(a, b, (((0,), (0,)), ((), ())), preferred_element_type=F32)


def _params(*sem):
    return pltpu.CompilerParams(dimension_semantics=sem, vmem_limit_bytes=VMEM_LIMIT)


def _resident(shape):
    nd = len(shape)
    return pl.BlockSpec(shape, lambda *_: (0,) * nd, pipeline_mode=pl.Buffered(1))


def _layer_weight(w):
    return pl.BlockSpec((None,) + w.shape[1:], lambda *_: (0,) * w.ndim, pipeline_mode=pl.Buffered(1))


def _ada_kernel(c_ref, w_ref, b_ref, o_ref):
    c = c_ref[...]
    act = (c * _sigmoid(c)).astype(BF16)
    o_ref[...] = _dot(act, w_ref[...].astype(BF16)) + b_ref[...]


def _ada(c, w_ada, b_ada):
    n = w_ada.shape[1]
    tn = D_MODEL
    return pl.pallas_call(
        _ada_kernel,
        grid=(n // tn,),
        in_specs=[pl.BlockSpec((BATCH, D_MODEL), lambda j: (0, 0)),
                  pl.BlockSpec((D_MODEL, tn), lambda j: (0, j)),
                  pl.BlockSpec((1, tn), lambda j: (0, j))],
        out_specs=pl.BlockSpec((BATCH, tn), lambda j: (0, j)),
        out_shape=jax.ShapeDtypeStruct((BATCH, n), F32),
        compiler_params=_params("arbitrary"),
        name="ada",
    )(c, w_ada, b_ada.reshape(1, n))


O_LAT = 4 * RET_W
O_PE = O_LAT + MLA_Q_LORA + MLA_KV_LORA
O_GATE = O_PE + MLA_ROPE


N_IN = O_GATE + 2 * D_MODEL
WPREP_ROWS = 512


def _stage_weights(w_hbm, ret_ref, lat_ref, gate_ref, buf, sem):
    chunks = [(s, min(WPREP_ROWS, N_IN - s)) for s in range(0, N_IN, WPREP_ROWS)]

    def copy(i):
        s, n = chunks[i]
        return pltpu.make_async_copy(w_hbm.at[pl.ds(s, n)], buf.at[i % 2, pl.ds(0, n)], sem.at[i % 2])

    groups = [(0, RET_W, ret_ref, 0, 1.0), (RET_W, 2 * RET_W, ret_ref, RET_W, RET_DK ** -0.5),
              (2 * RET_W, O_LAT, ret_ref, 2 * RET_W, 1.0), (O_LAT, O_GATE, lat_ref, 0, 1.0),
              (O_GATE, N_IN, gate_ref, 0, 1.0)]
    half = MLA_ROPE // 2
    rot = [(O_PE + half, O_GATE, O_GATE - O_LAT, -1.0), (O_PE, O_PE + half, O_GATE - O_LAT + half, 1.0)]

    copy(0).start()
    for i, (s, n) in enumerate(chunks):
        if i + 1 < len(chunks):
            copy(i + 1).start()
        copy(i).wait()
        for lo, hi, dst, dst_lo, scale in groups + [(a, b, lat_ref, d, sc) for a, b, d, sc in rot]:
            a, b = max(lo, s), min(hi, s + n)
            if a < b:
                x = buf[i % 2, a - s:b - s, :]
                dst[dst_lo + a - lo:dst_lo + b - lo, :] = (x if scale == 1.0 else x * scale).astype(BF16)


def _inproj_kernel(x_ref, mod_ref, n1_ref, cos_ref, sin_ref, mcos_ref, msin_ref, qn_ref, kvn_ref, w_hbm,
                   ret_ref, lat_ref, gate_ref, wr_ref, wm_ref, wg_ref, stage_buf, stage_sem):
    @pl.when(pl.program_id(0) == 0)
    def _():
        _stage_weights(w_hbm, wr_ref, wm_ref, wg_ref, stage_buf, stage_sem)

    y = _rms(x_ref[...]) * n1_ref[...]
    u = (y * (1.0 + mod_ref[1:2, :]) + mod_ref[0:1, :]).astype(BF16)
    cos, sin = cos_ref[...], sin_ref[...]
    half = RET_DK // 2
    for n in range(0, 2 * RET_W, RET_DK):
        p = _dot_nt(u, wr_ref[n:n + RET_DK, :])
        x1, x2 = p[:, :half], p[:, half:]
        ret_ref[:, n:n + half] = (x1 * cos - x2 * sin).astype(BF16)
        ret_ref[:, n + half:n + RET_DK] = (x2 * cos + x1 * sin).astype(BF16)
    step = 512
    for n in range(2 * RET_W, 3 * RET_W, step):
        ret_ref[:, n:n + step] = _dot_nt(u, wr_ref[n:n + step, :]).astype(BF16)
    for n in range(3 * RET_W, 4 * RET_W, step):
        p = _dot_nt(u, wr_ref[n:n + step, :])
        ret_ref[:, n:n + step] = (p * _sigmoid(p)).astype(BF16)
    lat = _dot_nt(u, wm_ref[...])
    o_kv, o_pe, o_rot = MLA_Q_LORA, MLA_Q_LORA + MLA_KV_LORA, MLA_Q_LORA + MLA_KV_LORA + MLA_ROPE
    lat_ref[:, :o_kv] = (_rms(lat[:, :o_kv]) * qn_ref[...]).astype(BF16)
    lat_ref[:, o_kv:o_pe] = (_rms(lat[:, o_kv:o_pe]) * kvn_ref[...]).astype(BF16)
    lat_ref[:, o_pe:o_rot] = (lat[:, o_pe:o_rot] * mcos_ref[...] + lat[:, o_rot:] * msin_ref[...]).astype(BF16)
    lat_ref[:, o_rot:] = jnp.zeros((lat.shape[0], MLA_ROPE), BF16)
    for n in range(0, 2 * D_MODEL, step):
        gate_ref[:, n:n + step] = _dot_nt(u, wg_ref[n:n + step, :]).astype(BF16)


def _inproj(x2, mod3, norm1, cos, sin, mla_cos, mla_sin, q_norm, kv_norm, w_in_t):
    tm = TM_INPROJ
    per_b = SEQ // tm
    rope_tab = pl.BlockSpec((tm, RET_DK // 2), lambda i: (i % per_b, 0))
    mla_tab = pl.BlockSpec((tm, MLA_ROPE), lambda i: (i % per_b, 0))
    return pl.pallas_call(
        _inproj_kernel,
        grid=(N_TOK // tm,),
        in_specs=[pl.BlockSpec((tm, D_MODEL), lambda i: (i, 0)),
                  pl.BlockSpec((None, 6, D_MODEL), lambda i: (i // per_b, 0, 0)),
                  _resident((1, D_MODEL)), rope_tab, rope_tab, mla_tab, mla_tab,
                  _resident(q_norm.shape), _resident(kv_norm.shape),
                  pl.BlockSpec(memory_space=pl.ANY)],
        out_specs=[pl.BlockSpec((tm, 4 * RET_W), lambda i: (i, 0)),
                   pl.BlockSpec((tm, MLA_LAT_W), lambda i: (i, 0)),
                   pl.BlockSpec((tm, 2 * D_MODEL), lambda i: (i, 0))],
        out_shape=[jax.ShapeDtypeStruct((N_TOK, 4 * RET_W), BF16),
                   jax.ShapeDtypeStruct((N_TOK, MLA_LAT_W), BF16),
                   jax.ShapeDtypeStruct((N_TOK, 2 * D_MODEL), BF16)],
        scratch_shapes=[pltpu.VMEM((O_LAT, D_MODEL), BF16), pltpu.VMEM((MLA_LAT_W, D_MODEL), BF16),
                        pltpu.VMEM((2 * D_MODEL, D_MODEL), BF16),
                        pltpu.VMEM((2, WPREP_ROWS, D_MODEL), F32), pltpu.SemaphoreType.DMA((2,))],
        compiler_params=_params("arbitrary"),
        name="inproj",
    )(x2, mod3, norm1, cos, sin, mla_cos, mla_sin, q_norm, kv_norm, w_in_t)


def _mla_kernel(lat_ref, wq_ref, wkv_ref, cos_ref, sin_ref,
                rq_ref, rk_ref, rv_ref, rg_ref, dec_ref, xi_ref, zeta_ref, cd_ref, w1_ref, w3_ref, w2_ref,
                o_ref, yr_ref, w1b_ref, w3b_ref, w2b_ref, q_s, k_s, v_s, state_ref):
    h = pl.program_id(1)
    o_kv, o_pe, o_rot = MLA_Q_LORA, MLA_Q_LORA + MLA_KV_LORA, MLA_Q_LORA + MLA_KV_LORA + MLA_ROPE
    cos, sin = cos_ref[...], sin_ref[...]

    scale = (MLA_QK ** -0.5) * LOG2_E
    for g in range(MLA_HPS):
        qf = _dot(lat_ref[:, :o_kv], wq_ref[g])
        q_s[g, :, :MLA_NOPE] = (qf[:, :MLA_NOPE] * scale).astype(BF16)
        q_pe = qf[:, MLA_NOPE:MLA_QK] * cos + qf[:, MLA_QK:] * sin
        q_s[g, :, MLA_NOPE:] = (q_pe * scale).astype(BF16)
        kvf = _dot(lat_ref[:, o_kv:o_pe], wkv_ref[g])
        k_s[g, :, :MLA_NOPE] = kvf[:, :MLA_NOPE].astype(BF16)
        k_s[g, :, MLA_NOPE:] = lat_ref[:, o_pe:o_rot]
        v_s[g, :, :MLA_V] = kvf[:, MLA_NOPE:].astype(BF16)
        v_s[g, :, MLA_V:] = jnp.ones((SEQ, MLA_V), BF16)

    causal = lax.broadcasted_iota(jnp.int32, (TQ, TQ), 0) >= lax.broadcasted_iota(jnp.int32, (TQ, TQ), 1)
    heads = range(MLA_HPS)
    n_blk = SEQ // TQ

    def scores(i):
        lo, hi = i * TQ, (i + 1) * TQ
        diag = [jnp.where(causal, _dot_nt(q_s[g, lo:hi, :], k_s[g, lo:hi, :]), -jnp.inf) for g in heads]
        past = [_dot_nt(q_s[g, lo:hi, :], k_s[g, :lo, :]) if i > 0 else None for g in heads]
        return diag, past

    state_ref[...] = jnp.zeros_like(state_ref)
    pending = scores(0)
    for i in range(n_blk):
        lo, hi = i * TQ, (i + 1) * TQ
        diag, past = pending
        if i + 1 < n_blk:
            pending = scores(i + 1)
        rq, rk, rv = rq_ref[lo:hi, :], rk_ref[lo:hi, :], rv_ref[lo:hi, :]
        r_scores = (_dot_nt(rq, rk) * dec_ref[h]).astype(BF16)
        carried = xi_ref[h] * _dot(rq, state_ref[...].astype(BF16))
        m = [jnp.max(diag[g], axis=-1, keepdims=True) for g in heads]
        if i > 0:
            m = [jnp.maximum(m[g], jnp.max(past[g], axis=-1, keepdims=True)) for g in heads]
        ry = _dot(r_scores, rv) + carried
        k_dec = (rk.astype(F32) * zeta_ref[h]).astype(BF16)
        state_ref[...] = state_ref[...] * cd_ref[h] + _dot_tn(k_dec, rv)
        acc = [_dot(jnp.exp2(diag[g] - m[g]).astype(BF16), v_s[g, lo:hi, :]) for g in heads]
        if i > 0:
            acc = [acc[g] + _dot(jnp.exp2(past[g] - m[g]).astype(BF16), v_s[g, :lo, :]) for g in heads]
        if i < 3:
            src, dst = ((w1_ref, w1b_ref), (w3_ref, w3b_ref), (w2_ref, w2b_ref))[i]
            dst[...] = src[...].astype(BF16)
        ryc = ry - jnp.mean(ry, axis=-1, keepdims=True)
        r_inv = lax.rsqrt(jnp.mean(ryc * ryc, axis=-1, keepdims=True) + EPS)
        yr_ref[lo:hi, :] = (rg_ref[lo:hi, :].astype(F32) * (ryc * r_inv)).astype(BF16)
        for g in heads:
            o_ref[lo:hi, g * MLA_V:(g + 1) * MLA_V] = (acc[g][:, :MLA_V] / acc[g][:, MLA_V:]).astype(BF16)


def _mla(lat3, wq, wkv, cos, sin, ret3, dec, xi, zeta, cd, w1, w3, w2):
    hps = MLA_HPS
    steps = MLA_HEADS // hps
    assert steps == RET_HEADS and TQ == RET_CHUNK and BATCH * steps == N_EXPERTS
    expert = lambda w: pl.BlockSpec((None,) + w.shape[1:], lambda b, h: (b * steps + h, 0, 0))
    ret_part = lambda part: pl.BlockSpec((None, SEQ, RET_DK), lambda b, h: (b, 0, part * RET_HEADS + h))
    whole = lambda a: pl.BlockSpec(a.shape, lambda b, h: (0,) * a.ndim)
    return pl.pallas_call(
        _mla_kernel,
        grid=(BATCH, MLA_HEADS // hps),
        in_specs=[pl.BlockSpec((None, SEQ, MLA_LAT_W), lambda b, h: (b, 0, 0)),
                  pl.BlockSpec((hps, MLA_Q_LORA, MLA_QK + MLA_ROPE), lambda b, h: (h, 0, 0)),
                  pl.BlockSpec((hps, MLA_KV_LORA, MLA_NOPE + MLA_V), lambda b, h: (h, 0, 0)),
                  pl.BlockSpec((SEQ, MLA_ROPE), lambda b, h: (0, 0)),
                  pl.BlockSpec((SEQ, MLA_ROPE), lambda b, h: (0, 0)),
                  ret_part(0), ret_part(1), ret_part(2), ret_part(3),
                  whole(dec), whole(xi), whole(zeta), whole(cd), expert(w1), expert(w3), expert(w2)],
        out_specs=[pl.BlockSpec((None, SEQ, hps * MLA_V), lambda b, h: (b, 0, h)),
                   pl.BlockSpec((None, SEQ, RET_DV), lambda b, h: (b, 0, h)),
                   expert(w1), expert(w3), expert(w2)],
        out_shape=[jax.ShapeDtypeStruct((BATCH, SEQ, MLA_HEADS * MLA_V), BF16),
                   jax.ShapeDtypeStruct((BATCH, SEQ, RET_HEADS * RET_DV), BF16)]
                  + [jax.ShapeDtypeStruct(w.shape, BF16) for w in (w1, w3, w2)],
        scratch_shapes=[pltpu.VMEM((hps, SEQ, MLA_QK), BF16),
                        pltpu.VMEM((hps, SEQ, MLA_QK), BF16),
                        pltpu.VMEM((hps, SEQ, 2 * MLA_V), BF16),
                        pltpu.VMEM((RET_DK, RET_DV), F32)],
        compiler_params=_params("arbitrary", "arbitrary"),
        name="mla",
    )(lat3, wq, wkv, cos, sin, ret3, ret3, ret3, ret3, dec, xi, zeta, cd, w1, w3, w2)


def _route(logits_t):
    tm = logits_t.shape[1]
    row = lax.broadcasted_iota(jnp.int32, (SUBLANES, tm), 0)
    neg = -jnp.inf
    gl = jnp.where(row < N_GROUPS, logits_t[:SUBLANES], neg)
    gmax = jnp.max(gl, axis=0, keepdims=True)
    gsel = jnp.min(jnp.where(gl == gmax, row, SUBLANES), axis=0, keepdims=True)
    p_grp = 1.0 / jnp.sum(jnp.exp(gl - gmax), axis=0, keepdims=True)
    el = logits_t[SUBLANES * N_GROUPS:SUBLANES * (N_GROUPS + 1)]
    for g in reversed(range(N_GROUPS - 1)):
        el = jnp.where(gsel == g, logits_t[SUBLANES * (g + 1):SUBLANES * (g + 2)], el)
    v0 = jnp.max(el, axis=0, keepdims=True)
    i0 = jnp.min(jnp.where(el == v0, row, SUBLANES), axis=0, keepdims=True)
    el1 = jnp.where(row == i0, neg, el)
    v1 = jnp.max(el1, axis=0, keepdims=True)
    i1 = jnp.min(jnp.where(el1 == v1, row, SUBLANES), axis=0, keepdims=True)
    t = jnp.exp(v1 - v0)
    w0 = p_grp / (1.0 + t)
    w1 = p_grp * t / (1.0 + t)
    return gsel * EXPERTS_PER_GROUP + i0, gsel * EXPERTS_PER_GROUP + i1, w0, w1


def _stack_rows(rows, n):
    tm = rows[0].shape[1]
    row = lax.broadcasted_iota(jnp.int32, (n, tm), 0)
    out = jnp.zeros((n, tm), F32)
    for k, r in enumerate(rows):
        out = jnp.where(row == k, r, out)
    return out


def _to_token_tiles(ref, val):
    n = val.shape[0]
    for s in range(ROW_TILES):
        ref[pl.ds(s, n, stride=ROW_TILES), :] = val[:, s * LANES:(s + 1) * LANES]


def _from_token_tiles(ref):
    n = ref.shape[0] // ROW_TILES
    return jnp.concatenate([ref[pl.ds(s, n, stride=ROW_TILES), :] for s in range(ROW_TILES)], axis=-1)


def _token_rows(ref, t):
    return ref.at[pl.ds(pl.multiple_of(t * ROW_TILES, ROW_TILES), ROW_TILES)]


def _merge_kernel(yr_ref, at_ref, gr_ref, gm_ref, x_ref, mod_ref, n2_ref, wro_ref, wmo_ref, wo_ref,
                  wrt_ref, brt_ref, h1_ref, u2_ref, meta_ref, wtok_ref, cnt_ref, carry_ref, wro_s, wmo_s, wo_s):
    tm = x_ref.shape[0]

    @pl.when(pl.program_id(0) == 0)
    def _():
        carry_ref[...] = jnp.zeros_like(carry_ref)
        wro_s[...] = wro_ref[...].astype(BF16)
        wmo_s[...] = wmo_ref[...].astype(BF16)
        wo_s[...] = wo_ref[...].astype(BF16)

    chunks = range(MERGE_CHAINS)
    sub = tm // MERGE_CHAINS
    rows = [slice(c * sub, (c + 1) * sub) for c in chunks]
    y_ret = [_dot(yr_ref[r, :], wro_s[...]) for r in rows]
    y_mla = [_dot(at_ref[r, :], wmo_s[...]) for r in rows]
    merged = [(_sigmoid(gr_ref[rows[c], :].astype(F32)) * y_ret[c]
               + _sigmoid(gm_ref[rows[c], :].astype(F32)) * y_mla[c]).astype(BF16) for c in chunks]
    o = [_dot(merged[c], wo_s[...]) for c in chunks]
    h1 = [x_ref[rows[c], :] + mod_ref[2:3, :] * o[c] for c in chunks]
    for c in chunks:
        h1_ref[rows[c], :] = h1[c]
    u2 = [_rms(h1[c]) * n2_ref[...] * (1.0 + mod_ref[4:5, :]) + mod_ref[3:4, :] for c in chunks]
    for c in chunks:
        _to_token_tiles(u2_ref.at[pl.ds(c * sub * ROW_TILES, sub * ROW_TILES)], u2[c])
    w = wrt_ref[...]
    w_hi = w.astype(BF16)
    w_lo = (w - w_hi.astype(F32)).astype(BF16)
    w_both = jnp.concatenate([w_hi, w_lo], axis=0)
    u_hi = [u2[c].astype(BF16) for c in chunks]
    u_lo = [(u2[c] - u_hi[c].astype(F32)).astype(BF16) for c in chunks]
    by_hi = [_dot_nt(w_both, u_hi[c]) for c in chunks]
    logits_t = [by_hi[c][:ROUTER_ROWS] + by_hi[c][ROUTER_ROWS:] + _dot_nt(w_hi, u_lo[c]) + brt_ref[:, 0:1]
                for c in chunks]
    routed = [_route(logits_t[c]) for c in chunks]
    e0, e1, w0, w1 = [jnp.concatenate([routed[c][k] for c in chunks], axis=1) for k in range(4)]
    erow = lax.broadcasted_iota(jnp.int32, (N_EXPERTS, tm), 0)
    m0, m1 = erow == e0, erow == e1
    member = jnp.where(m0 | m1, 1.0, 0.0)
    earlier = jnp.where(lax.broadcasted_iota(jnp.int32, (tm, tm), 0) < lax.broadcasted_iota(jnp.int32, (tm, tm), 1),
                        1.0, 0.0).astype(BF16)
    prefix = _dot(member.astype(BF16), earlier) + carry_ref[:, 0:1]
    rank0 = jnp.sum(jnp.where(m0, prefix, 0.0), axis=0, keepdims=True)
    rank1 = jnp.sum(jnp.where(m1, prefix, 0.0), axis=0, keepdims=True)
    carry_ref[...] = carry_ref[...] + jnp.sum(member, axis=1, keepdims=True)
    cnt_ref[...] = carry_ref[...]
    meta_ref[...] = _stack_rows([e0.astype(F32), e1.astype(F32), rank0, rank1], SUBLANES)
    wt = _stack_rows([w0, w1], 2 * SUBLANES)
    wt_hi = wt.astype(BF16)
    wt_lo = (wt - wt_hi.astype(F32)).astype(BF16)
    place = jnp.where(lax.broadcasted_iota(jnp.int32, (2 * SUBLANES, LANES), 0)
                      == lax.broadcasted_iota(jnp.int32, (2 * SUBLANES, LANES), 1), 1.0, 0.0).astype(BF16)
    wtok_ref[...] = _dot_tn(wt_hi, place) + _dot_tn(wt_lo, place)


def _merge(y_ret, attn, gates, x2, mod3, norm2, w_ret_o, w_mla_o, w_out, w_rt, b_rt):
    tm = TM_PROJ
    per_b = SEQ // tm
    row = lambda j: pl.BlockSpec((tm, D_MODEL), lambda i: (i, j))
    return pl.pallas_call(
        _merge_kernel,
        grid=(N_TOK // tm,),
        in_specs=[row(0), row(0), row(0), row(1), row(0),
                  pl.BlockSpec((None, 6, D_MODEL), lambda i: (i // per_b, 0, 0)),
                  _resident((1, D_MODEL)),
                  _layer_weight(w_ret_o), _layer_weight(w_mla_o), _layer_weight(w_out),
                  _resident(w_rt.shape), _resident(b_rt.shape)],
        out_specs=[row(0),
                   pl.BlockSpec((tm * ROW_TILES, LANES), lambda i: (i, 0)),
                   pl.BlockSpec((SUBLANES, tm), lambda i: (0, i)),
                   pl.BlockSpec((tm, LANES), lambda i: (i, 0)),
                   pl.BlockSpec((N_EXPERTS, LANES), lambda i: (0, 0))],
        out_shape=[jax.ShapeDtypeStruct((N_TOK, D_MODEL), F32),
                   jax.ShapeDtypeStruct((N_TOK * ROW_TILES, LANES), F32),
                   jax.ShapeDtypeStruct((SUBLANES, N_TOK), F32),
                   jax.ShapeDtypeStruct((N_TOK, LANES), F32),
                   jax.ShapeDtypeStruct((N_EXPERTS, LANES), F32)],
        scratch_shapes=[pltpu.VMEM((N_EXPERTS, LANES), F32)] + [pltpu.VMEM((D_MODEL, D_MODEL), BF16)] * 3,
        compiler_params=_params("arbitrary"),
        name="merge",
    )(y_ret, attn, gates, gates, x2, mod3, norm2, w_ret_o, w_mla_o, w_out, w_rt, b_rt)


def _plan_kernel(meta_ref, off_ref, dst_ref):
    m = meta_ref[...]
    erow = lax.broadcasted_iota(jnp.int32, (N_EXPERTS, m.shape[1]), 0)
    off = off_ref[:, 0:1]
    d = [jnp.sum(jnp.where(erow == m[k:k + 1].astype(jnp.int32), off, 0.0), axis=0, keepdims=True) + m[k + 2:k + 3]
         for k in range(TOP_K)]
    dst_ref[...] = _stack_rows(d, SUBLANES).astype(jnp.int32)


def _plan(meta_t, off_col):
    tm = 2048
    return pl.pallas_call(
        _plan_kernel,
        grid=(N_TOK // tm,),
        in_specs=[pl.BlockSpec((SUBLANES, tm), lambda i: (0, i)),
                  pl.BlockSpec((N_EXPERTS, LANES), lambda i: (0, 0))],
        out_specs=pl.BlockSpec((SUBLANES, tm), lambda i: (0, i)),
        out_shape=jax.ShapeDtypeStruct((SUBLANES, N_TOK), jnp.int32),
        compiler_params=_params("arbitrary"),
        name="plan",
    )(meta_t, off_col)


def _row_copy_wait(src_like, dst_like, sem):
    pltpu.make_async_copy(src_like, dst_like, sem).wait()


def _dispatch_kernel(d0_ref, d1_ref, seg_ref, cnt_ref, u_ref, xs_ref, zero_ref, sem, zsem):
    i = pl.program_id(0)
    tm = u_ref.shape[0] // ROW_TILES
    tile_rows = TE * ROW_TILES

    def slot_tile(j):
        return xs_ref.at[pl.ds(pl.multiple_of(j * tile_rows, tile_rows), tile_rows)]

    def pad(e):
        first = seg_ref[e] + cnt_ref[e]
        rows = (seg_ref[e + 1] - first) * ROW_TILES
        start = pl.multiple_of(first * ROW_TILES, ROW_TILES)
        return rows > 0, pltpu.make_async_copy(zero_ref.at[pl.ds(0, rows)], xs_ref.at[pl.ds(start, rows)], zsem)

    def unused(j):
        return pltpu.make_async_copy(zero_ref, slot_tile(j), zsem)

    first_unused = seg_ref[N_EXPERTS] // TE

    @pl.when(i == 0)
    def _():
        zero_ref[...] = jnp.zeros_like(zero_ref)
        for e in range(N_EXPERTS):
            nonempty, copy = pad(e)
            pl.when(nonempty)(copy.start)
        lax.fori_loop(first_unused, N_TILES, lambda j, c: (unused(j).start(), c)[1], 0)

    base = i * tm

    def body(r, carry):
        src = _token_rows(u_ref, r)
        pltpu.make_async_copy(src, _token_rows(xs_ref, d0_ref[base + r]), sem).start(priority=0)
        pltpu.make_async_copy(src, _token_rows(xs_ref, d1_ref[base + r]), sem).start(priority=1)
        return carry

    lax.fori_loop(0, tm, body, 0, unroll=8)
    _row_copy_wait(u_ref, xs_ref.at[pl.ds(0, tm * ROW_TILES)], sem)
    _row_copy_wait(u_ref, xs_ref.at[pl.ds(0, tm * ROW_TILES)], sem)

    @pl.when(i == 0)
    def _():
        for e in range(N_EXPERTS):
            nonempty, copy = pad(e)
            pl.when(nonempty)(copy.wait)
        lax.fori_loop(first_unused, N_TILES, lambda j, c: (unused(j).wait(), c)[1], 0)


def _dispatch(d0, d1, seg, cnt, u2t):
    tm = TM_DISPATCH
    return pl.pallas_call(
        _dispatch_kernel,
        grid_spec=pltpu.PrefetchScalarGridSpec(
            num_scalar_prefetch=4,
            grid=(N_TOK // tm,),
            in_specs=[pl.BlockSpec((tm * ROW_TILES, LANES), lambda i, *_: (i, 0))],
            out_specs=pl.BlockSpec(memory_space=pl.ANY),
            scratch_shapes=[pltpu.VMEM((TE * ROW_TILES, LANES), F32),
                            pltpu.SemaphoreType.DMA(()), pltpu.SemaphoreType.DMA(())]),
        out_shape=jax.ShapeDtypeStruct((N_SLOTS * ROW_TILES, LANES), F32),
        compiler_params=_params("arbitrary"),
        name="dispatch",
    )(d0, d1, seg, cnt, u2t)


def _expert_kernel(te_ref, nv_ref, x_hbm, w1_s, w3_s, w2_s, y_hbm, xbuf, ybuf, sem, wsem):
    j = pl.program_id(0)
    nv = nv_ref[0]
    rows = TE * ROW_TILES

    def fetch(t):
        return pltpu.make_async_copy(x_hbm.at[pl.ds(pl.multiple_of(t * rows, rows), rows)],
                                     xbuf.at[t % EXPERT_RING], sem.at[t % EXPERT_RING])

    def put(t):
        return pltpu.make_async_copy(ybuf.at[t % 2], y_hbm.at[pl.ds(pl.multiple_of(t * rows, rows), rows)],
                                     wsem.at[t % 2])

    @pl.when(j == 0)
    def _():
        fetch(0).start()

        @pl.when(1 < nv)
        def _():
            fetch(1).start()

    @pl.when(j + 2 < nv)
    def _():
        fetch(j + 2).start()

    @pl.when(j < nv)
    def _():
        fetch(j).wait()
        x_ref = xbuf.at[j % EXPERT_RING]
        sub = TE // TE_CHAINS * ROW_TILES
        part = lambda ref, c: ref.at[pl.ds(c * sub, sub)]
        chains = range(TE_CHAINS)
        x = [_from_token_tiles(part(x_ref, c)).astype(BF16) for c in chains]
        a = [_dot(x[c], w1_s[...]) for c in chains]
        b = [_dot(x[c], w3_s[...]) for c in chains]
        hid = [(a[c] * _sigmoid(a[c]) * b[c]).astype(BF16) for c in chains]

        @pl.when(j >= 2)
        def _():
            put(j - 2).wait()

        y_ref = ybuf.at[j % 2]
        for c in chains:
            _to_token_tiles(part(y_ref, c), _dot(hid[c], w2_s[...]))
        put(j).start()

        @pl.when(j == nv - 1)
        def _():
            @pl.when(j >= 1)
            def _():
                put(j - 1).wait()

            put(j).wait()


def _experts(tile_expert, n_valid, xs, w1, w3, w2):
    tile = lambda j, te, nv: jnp.minimum(j, nv[0] - 1)
    wspec = lambda shape: pl.BlockSpec((None,) + shape, lambda j, te, nv: (te[tile(j, te, nv)], 0, 0))
    return pl.pallas_call(
        _expert_kernel,
        grid_spec=pltpu.PrefetchScalarGridSpec(
            num_scalar_prefetch=2,
            grid=(N_TILES,),
            in_specs=[pl.BlockSpec(memory_space=pl.ANY),
                      wspec((D_MODEL, D_EXPERT)), wspec((D_MODEL, D_EXPERT)), wspec((D_EXPERT, D_MODEL))],
            out_specs=pl.BlockSpec(memory_space=pl.ANY),
            scratch_shapes=[pltpu.VMEM((EXPERT_RING, TE * ROW_TILES, LANES), F32),
                            pltpu.VMEM((2, TE * ROW_TILES, LANES), F32),
                            pltpu.SemaphoreType.DMA((EXPERT_RING,)),
                            pltpu.SemaphoreType.DMA((2,))]),
        out_shape=jax.ShapeDtypeStruct((N_SLOTS * ROW_TILES, LANES), F32),
        input_output_aliases={2: 0},
        compiler_params=_params("arbitrary"),
        name="experts",
    )(tile_expert, n_valid, xs, w1, w3, w2)


def _final_kernel(d0_ref, d1_ref, h1_ref, meta_ref, mod_ref, fn_ref, ys_ref, o_ref, ybuf, sem):
    i = pl.program_id(0)
    tm = h1_ref.shape[0]

    def gather(t):
        buf, s = ybuf.at[t % 2], sem.at[t % 2]
        base = t * tm

        def body(r, carry):
            pltpu.make_async_copy(_token_rows(ys_ref, d0_ref[base + r]), _token_rows(buf.at[0], r), s).start(priority=0)
            pltpu.make_async_copy(_token_rows(ys_ref, d1_ref[base + r]), _token_rows(buf.at[1], r), s).start(priority=1)
            return carry

        lax.fori_loop(0, tm, body, 0, unroll=8)

    @pl.when(i == 0)
    def _():
        gather(0)

    @pl.when(i + 1 < pl.num_programs(0))
    def _():
        gather(i + 1)

    buf = ybuf.at[i % 2]
    for k in range(TOP_K):
        _row_copy_wait(ys_ref.at[pl.ds(0, tm * ROW_TILES)], buf.at[k], sem.at[i % 2])
    m = meta_ref[...]
    moe = m[:, 0:1] * _from_token_tiles(buf.at[0]) + m[:, 1:2] * _from_token_tiles(buf.at[1])
    h2 = h1_ref[...] + mod_ref[5:6, :] * moe
    o_ref[...] = _rms(h2) * fn_ref[...]


def _final(d0, d1, h1, meta, mod3, final_norm, ys):
    tm = TM_FINAL
    per_b = SEQ // tm
    return pl.pallas_call(
        _final_kernel,
        grid_spec=pltpu.PrefetchScalarGridSpec(
            num_scalar_prefetch=2,
            grid=(N_TOK // tm,),
            in_specs=[pl.BlockSpec((tm, D_MODEL), lambda i, *_: (i, 0)),
                      pl.BlockSpec((tm, LANES), lambda i, *_: (i, 0)),
                      pl.BlockSpec((None, 6, D_MODEL), lambda i, *_: (i // per_b, 0, 0)),
                      pl.BlockSpec((1, D_MODEL), lambda i, *_: (0, 0)),
                      pl.BlockSpec(memory_space=pl.ANY)],
            out_specs=pl.BlockSpec((tm, D_MODEL), lambda i, *_: (i, 0)),
            scratch_shapes=[pltpu.VMEM((2, TOP_K, tm * ROW_TILES, LANES), F32), pltpu.SemaphoreType.DMA((2,))]),
        out_shape=jax.ShapeDtypeStruct((N_TOK, D_MODEL), F32),
        compiler_params=_params("arbitrary"),
        name="final",
    )(d0, d1, h1, meta, mod3, final_norm, ys)


def _slot_layout(counts):
    cnt = counts[:, 0].astype(jnp.int32)
    tile_end = jnp.cumsum((cnt + TE - 1) // TE)
    seg = jnp.concatenate([jnp.zeros((1,), jnp.int32), tile_end * TE])
    off_col = jnp.broadcast_to(seg[:-1].astype(F32)[:, None], (N_EXPERTS, LANES))
    tile_ids = jnp.arange(N_TILES, dtype=jnp.int32)
    tile_expert = jnp.sum((tile_end[None, :] <= tile_ids[:, None]).astype(jnp.int32), axis=1)
    tile_expert = jnp.minimum(tile_expert, N_EXPERTS - 1)
    return cnt, seg, off_col, tile_expert, tile_end[-1:]


def _rope_tables(dim):
    pos = np.arange(SEQ, dtype=np.float64)
    inv = ROPE_THETA ** (-np.arange(0, dim, 2, dtype=np.float64) / dim)
    ang = pos[:, None] * inv[None, :]
    return np.cos(ang).astype(np.float32), np.sin(ang).astype(np.float32)


def _decay_tables():
    c = RET_CHUNK
    log_gamma = np.log1p(-np.exp2(-5.0 - np.arange(RET_HEADS, dtype=np.float64)))
    idx = np.arange(c, dtype=np.float64)
    rel = idx[:, None] - idx[None, :]
    dec = np.where(rel[None] >= 0, np.exp(log_gamma[:, None, None] * np.maximum(rel, 0.0)[None]), 0.0)
    xi = np.exp(log_gamma[:, None] * (idx[None, :] + 1.0))[:, :, None]
    zeta = np.exp(log_gamma[:, None] * (c - 1.0 - idx[None, :]))[:, :, None]
    cd = np.exp(log_gamma * c)[:, None, None]
    return tuple(jnp.asarray(t.astype(np.float32)) for t in (dec, xi, zeta, cd))


def _rotate_half_cols(w):
    half = w.shape[-1] // 2
    return jnp.concatenate([-w[..., half:], w[..., :half]], axis=-1)


def kernel(x, c, w_ada, b_ada, norm1, norm2, w_in, w_ret_o, q_norm, kv_norm, w_uq, w_ukv, w_mla_o, w_out,
           w_grp, b_grp, w_exp, b_exp, w1, w3, w2, final_norm):
    assert x.shape == (BATCH, SEQ, D_MODEL) and w_ada.shape[0] == 1
    x2 = x.reshape(N_TOK, D_MODEL)

    w_in_t = jnp.transpose(w_in[0])
    wq = w_uq[0].reshape(MLA_Q_LORA, MLA_HEADS, MLA_QK)
    wq = jnp.concatenate([wq, _rotate_half_cols(wq[..., MLA_NOPE:])], axis=-1)
    wq = wq.transpose(1, 0, 2).astype(BF16)
    wkv = w_ukv[0].reshape(MLA_KV_LORA, MLA_HEADS, MLA_NOPE + MLA_V).transpose(1, 0, 2).astype(BF16)
    gap = jnp.zeros((SUBLANES - N_GROUPS, D_MODEL), F32)
    tail = jnp.zeros((ROUTER_ROWS - SUBLANES - N_EXPERTS, D_MODEL), F32)
    w_rt = jnp.concatenate([w_grp[0].T, gap, w_exp[0].T, tail], axis=0)
    b_rt = jnp.concatenate([b_grp[0], gap[:, 0], b_exp[0], tail[:, 0]])
    b_rt = jnp.broadcast_to(b_rt[:, None], (ROUTER_ROWS, LANES))

    ret_cos, ret_sin = (jnp.asarray(t) for t in _rope_tables(RET_DK))
    mla_cos, mla_sin = (jnp.asarray(np.concatenate([t, t], axis=-1)) for t in _rope_tables(MLA_ROPE))
    dec, xi, zeta, cd = _decay_tables()

    mod3 = _ada(c, w_ada[0], b_ada[0]).reshape(BATCH, 6, D_MODEL)
    ret, lat, gates = _inproj(x2, mod3, norm1, ret_cos, ret_sin, mla_cos, mla_sin, q_norm, kv_norm, w_in_t)
    e_shape = (N_EXPERTS, D_MODEL, D_EXPERT)
    attn, y_ret, w1b, w3b, w2b = _mla(
        lat.reshape(BATCH, SEQ, MLA_LAT_W), wq, wkv, mla_cos, mla_sin, ret.reshape(BATCH, SEQ, 4 * RET_W),
        dec, xi, zeta, cd, w1[0].reshape(e_shape), w3[0].reshape(e_shape), w2[0].reshape(N_EXPERTS, D_EXPERT, D_MODEL))
    h1, u2t, meta_t, wtok, counts = _merge(y_ret.reshape(N_TOK, D_MODEL), attn.reshape(N_TOK, D_MODEL), gates, x2,
                                           mod3, norm2, w_ret_o, w_mla_o, w_out, w_rt, b_rt)
    cnt, seg, off_col, tile_expert, n_valid = _slot_layout(counts)
    dst = _plan(meta_t, off_col)
    d0, d1 = dst[0], dst[1]
    xs = _dispatch(d0, d1, seg, cnt, u2t)
    ys = _experts(tile_expert, n_valid, xs, w1b, w3b, w2b)
    out = _final(d0, d1, h1, wtok, mod3, final_norm.reshape(1, D_MODEL), ys)
    return out.reshape(BATCH, SEQ, D_MODEL)
```
